```python
import jax, jax.numpy as jnp
from jax import lax
import numpy as np

D_MODEL = 2048
BATCH = 8
SEQ = 4096
DEPTH = 1

N_META = 16
POOL_WIDTH = D_MODEL
POOL_WINDOWS = (2, 4, 8, 16)
POOL_GROUPS = len(POOL_WINDOWS)
POOL_GROUP_DIM = POOL_WIDTH // POOL_GROUPS
LRU_WIDTH = D_MODEL
LRU_HEAD_DIM = 256
LRU_HEADS = LRU_WIDTH // LRU_HEAD_DIM
CONV_WIDTH = 4
LRU_C = 8.0
D_FF = 4 * D_MODEL
NORM_EPS = 1e-6
IN_SPLITS = (POOL_WIDTH,
             POOL_WIDTH + LRU_WIDTH,
             POOL_WIDTH + 2 * LRU_WIDTH,
             POOL_WIDTH + 2 * LRU_WIDTH + D_MODEL)
IN_COLS = POOL_WIDTH + 2 * LRU_WIDTH + 2 * D_MODEL

kernel_name = "hybrid_pool_rglru_gated_block"


def rmsnorm(x, g):
    xf = x.astype(jnp.float32)
    y = xf * lax.rsqrt(jnp.mean(xf * xf, axis=-1, keepdims=True) + NORM_EPS)
    return (y * g.astype(jnp.float32)).astype(x.dtype)


def causal_window_mean(v, w):
    T = v.shape[1]
    c = lax.cumsum(v, axis=1)
    c_shift = jnp.pad(c, ((0, 0), (w, 0), (0, 0)))[:, :T]
    cnt = jnp.minimum(jnp.arange(1, T + 1), w).astype(jnp.float32)
    return (c - c_shift) / cnt[None, :, None]


def pool_mixer(v, pool_w, pool_scale):
    B, T, _ = v.shape
    vf = v.astype(jnp.float32)
    diffs = []
    for g, w in enumerate(POOL_WINDOWS):
        vg = vf[..., g * POOL_GROUP_DIM:(g + 1) * POOL_GROUP_DIM]
        diffs.append(causal_window_mean(vg, w) - vg)
    d = jnp.stack(diffs, axis=2)
    y = jnp.einsum('btgc,gcd->btgd', d, pool_w.astype(jnp.float32))
    y = y.reshape(B, T, POOL_WIDTH) * pool_scale.astype(jnp.float32)
    return y.astype(v.dtype)


def causal_depthwise_conv(x, w, b):
    y = lax.conv_general_dilated(
        x, w[:, None, :].astype(x.dtype), window_strides=(1,),
        padding=((CONV_WIDTH - 1, 0),),
        dimension_numbers=('NWC', 'WIO', 'NWC'),
        feature_group_count=x.shape[-1])
    return y + b.astype(x.dtype)


def rg_lru(xc, gate_a_w, gate_a_b, gate_x_w, gate_x_b, lam):
    B, T, W = xc.shape
    xf = xc.astype(jnp.float32)
    xh = xf.reshape(B, T, LRU_HEADS, LRU_HEAD_DIM)
    r = jax.nn.sigmoid(jnp.einsum('bthi,hij->bthj', xh, gate_a_w.astype(jnp.float32))
                       + gate_a_b.astype(jnp.float32)).reshape(B, T, W)
    i = jax.nn.sigmoid(jnp.einsum('bthi,hij->bthj', xh, gate_x_w.astype(jnp.float32))
                       + gate_x_b.astype(jnp.float32)).reshape(B, T, W)
    log_a = -LRU_C * r * jax.nn.softplus(-lam.astype(jnp.float32))
    a = jnp.exp(log_a)
    mult = jnp.sqrt(-jnp.expm1(2.0 * log_a))
    bt = mult * (i * xf)

    def combine(left, right):
        a1, b1 = left
        a2, b2 = right
        return a1 * a2, a2 * b1 + b2

    _, h = lax.associative_scan(combine, (a, bt), axis=1)
    return h.astype(xc.dtype)


def _fwd_setup_inputs(seed: int = 0) -> dict:
    key = jax.random.key(seed)
    ks = jax.random.split(key, 20)
    f32 = jnp.float32
    x = jax.random.normal(ks[0], (BATCH, SEQ, D_MODEL), f32)
    meta_tokens = jax.random.normal(ks[1], (N_META, D_MODEL), f32)
    norm1_g = 1.0 + 0.02 * jax.random.normal(ks[2], (DEPTH, D_MODEL), f32)
    w_in = jax.random.normal(ks[3], (DEPTH, D_MODEL, IN_COLS), f32) * D_MODEL ** -0.5
    pool_w = jax.random.normal(ks[4], (DEPTH, POOL_GROUPS, POOL_GROUP_DIM, POOL_GROUP_DIM), f32) * POOL_GROUP_DIM ** -0.5
    pool_scale = 1.0 + 0.02 * jax.random.normal(ks[5], (DEPTH, POOL_WIDTH), f32)
    conv_w = jax.random.normal(ks[6], (DEPTH, CONV_WIDTH, LRU_WIDTH), f32) * CONV_WIDTH ** -0.5
    conv_b = 0.01 * jax.random.normal(ks[7], (DEPTH, LRU_WIDTH), f32)
    gate_a_w = jax.random.normal(ks[8], (DEPTH, LRU_HEADS, LRU_HEAD_DIM, LRU_HEAD_DIM), f32) * LRU_HEAD_DIM ** -0.5
    gate_a_b = 0.01 * jax.random.normal(ks[9], (DEPTH, LRU_HEADS, LRU_HEAD_DIM), f32)
    gate_x_w = jax.random.normal(ks[10], (DEPTH, LRU_HEADS, LRU_HEAD_DIM, LRU_HEAD_DIM), f32) * LRU_HEAD_DIM ** -0.5
    gate_x_b = 0.01 * jax.random.normal(ks[11], (DEPTH, LRU_HEADS, LRU_HEAD_DIM), f32)
    u = jax.random.uniform(ks[12], (DEPTH, LRU_WIDTH), f32, minval=0.9, maxval=0.999)
    s = u ** (1.0 / LRU_C)
    lru_lambda = jnp.log(s) - jnp.log1p(-s)
    w_out = jax.random.normal(ks[13], (DEPTH, D_MODEL, D_MODEL), f32) * D_MODEL ** -0.5
    norm2_g = 1.0 + 0.02 * jax.random.normal(ks[14], (DEPTH, D_MODEL), f32)
    mlp_w1 = jax.random.normal(ks[15], (DEPTH, D_MODEL, D_FF), f32) * D_MODEL ** -0.5
    mlp_w2 = jax.random.normal(ks[16], (DEPTH, D_FF, D_MODEL), f32) * D_FF ** -0.5
    final_g = 1.0 + 0.02 * jax.random.normal(ks[17], (D_MODEL,), f32)
    return {"x": x, "meta_tokens": meta_tokens, "norm1_g": norm1_g, "w_in": w_in,
            "pool_w": pool_w, "pool_scale": pool_scale, "conv_w": conv_w, "conv_b": conv_b,
            "gate_a_w": gate_a_w, "gate_a_b": gate_a_b, "gate_x_w": gate_x_w, "gate_x_b": gate_x_b,
            "lru_lambda": lru_lambda, "w_out": w_out, "norm2_g": norm2_g,
            "mlp_w1": mlp_w1, "mlp_w2": mlp_w2, "final_g": final_g}


def _fwd_reference(x, meta_tokens, norm1_g, w_in, pool_w, pool_scale, conv_w, conv_b,
              gate_a_w, gate_a_b, gate_x_w, gate_x_b, lru_lambda, w_out, norm2_g,
              mlp_w1, mlp_w2, final_g):
    B = x.shape[0]
    meta = jnp.broadcast_to(meta_tokens[None].astype(x.dtype), (B, N_META, x.shape[-1]))
    h = jnp.concatenate([meta, x], axis=1)
    for l in range(DEPTH):
        u = rmsnorm(h, norm1_g[l])
        proj = u @ w_in[l]
        v_pool, v_lru, v_gelu, g_pool, g_lru = jnp.split(proj, IN_SPLITS, axis=-1)
        pool_out = pool_mixer(v_pool, pool_w[l], pool_scale[l])
        xc = causal_depthwise_conv(v_lru, conv_w[l], conv_b[l])
        lru_out = rg_lru(xc, gate_a_w[l], gate_a_b[l], gate_x_w[l], gate_x_b[l],
                         lru_lambda[l]) * jax.nn.gelu(v_gelu)
        merged = jax.nn.sigmoid(g_pool) * pool_out + jax.nn.sigmoid(g_lru) * lru_out
        h = h + merged @ w_out[l]
        u2 = rmsnorm(h, norm2_g[l])
        h = h + jnp.square(jax.nn.relu(u2 @ mlp_w1[l])) @ mlp_w2[l]
    out = rmsnorm(h, final_g)
    return out[:, N_META:]


import jax as _jax
import jax.numpy as _jnp

TWIN_FORMAT = 'train_step'
FWD_PARAMS = ['x', 'meta_tokens', 'norm1_g', 'w_in', 'pool_w', 'pool_scale', 'conv_w', 'conv_b', 'gate_a_w', 'gate_a_b', 'gate_x_w', 'gate_x_b', 'lru_lambda', 'w_out', 'norm2_g', 'mlp_w1', 'mlp_w2', 'final_g']
TWIN_WEIGHTS = ['meta_tokens', 'norm1_g', 'w_in', 'pool_w', 'pool_scale', 'conv_w', 'conv_b', 'gate_a_w', 'gate_a_b', 'gate_x_w', 'gate_x_b', 'lru_lambda', 'w_out', 'norm2_g', 'mlp_w1', 'mlp_w2', 'final_g']
TWIN_DIFF_INPUT = 'x'
TWIN_INPUTS = ['x', 'meta_tokens', 'norm1_g', 'w_in', 'pool_w', 'pool_scale', 'conv_w', 'conv_b', 'gate_a_w', 'gate_a_b', 'gate_x_w', 'gate_x_b', 'lru_lambda', 'w_out', 'norm2_g', 'mlp_w1', 'mlp_w2', 'final_g', 'loss_target', 'm_meta_tokens', 'm_norm1_g', 'm_w_in', 'm_pool_w', 'm_pool_scale', 'm_conv_w', 'm_conv_b', 'm_gate_a_w', 'm_gate_a_b', 'm_gate_x_w', 'm_gate_x_b', 'm_lru_lambda', 'm_w_out', 'm_norm2_g', 'm_mlp_w1', 'm_mlp_w2', 'm_final_g', 'v_meta_tokens', 'v_norm1_g', 'v_w_in', 'v_pool_w', 'v_pool_scale', 'v_conv_w', 'v_conv_b', 'v_gate_a_w', 'v_gate_a_b', 'v_gate_x_w', 'v_gate_x_b', 'v_lru_lambda', 'v_w_out', 'v_norm2_g', 'v_mlp_w1', 'v_mlp_w2', 'v_final_g']
TWIN_OUTPUTS = ['loss', 'grad_x', 'grad_meta_tokens', 'grad_norm1_g', 'grad_w_in', 'grad_pool_w', 'grad_pool_scale', 'grad_conv_w', 'grad_conv_b', 'grad_gate_a_w', 'grad_gate_a_b', 'grad_gate_x_w', 'grad_gate_x_b', 'grad_lru_lambda', 'grad_w_out', 'grad_norm2_g', 'grad_mlp_w1', 'grad_mlp_w2', 'grad_final_g', 'delta_meta_tokens', 'delta_norm1_g', 'delta_w_in', 'delta_pool_w', 'delta_pool_scale', 'delta_conv_w', 'delta_conv_b', 'delta_gate_a_w', 'delta_gate_a_b', 'delta_gate_x_w', 'delta_gate_x_b', 'delta_lru_lambda', 'delta_w_out', 'delta_norm2_g', 'delta_mlp_w1', 'delta_mlp_w2', 'delta_final_g', 'new_m_meta_tokens', 'new_m_norm1_g', 'new_m_w_in', 'new_m_pool_w', 'new_m_pool_scale', 'new_m_conv_w', 'new_m_conv_b', 'new_m_gate_a_w', 'new_m_gate_a_b', 'new_m_gate_x_w', 'new_m_gate_x_b', 'new_m_lru_lambda', 'new_m_w_out', 'new_m_norm2_g', 'new_m_mlp_w1', 'new_m_mlp_w2', 'new_m_final_g', 'new_v_meta_tokens', 'new_v_norm1_g', 'new_v_w_in', 'new_v_pool_w', 'new_v_pool_scale', 'new_v_conv_w', 'new_v_conv_b', 'new_v_gate_a_w', 'new_v_gate_a_b', 'new_v_gate_x_w', 'new_v_gate_x_b', 'new_v_lru_lambda', 'new_v_w_out', 'new_v_norm2_g', 'new_v_mlp_w1', 'new_v_mlp_w2', 'new_v_final_g']
TWIN_LEAF_KINDS = {'loss': 'loss', 'grad_x': 'grad_x', 'grad_meta_tokens': 'grad_w', 'grad_norm1_g': 'grad_w', 'grad_w_in': 'grad_w', 'grad_pool_w': 'grad_w', 'grad_pool_scale': 'grad_w', 'grad_conv_w': 'grad_w', 'grad_conv_b': 'grad_w', 'grad_gate_a_w': 'grad_w', 'grad_gate_a_b': 'grad_w', 'grad_gate_x_w': 'grad_w', 'grad_gate_x_b': 'grad_w', 'grad_lru_lambda': 'grad_w', 'grad_w_out': 'grad_w', 'grad_norm2_g': 'grad_w', 'grad_mlp_w1': 'grad_w', 'grad_mlp_w2': 'grad_w', 'grad_final_g': 'grad_w', 'delta_meta_tokens': 'delta_w', 'delta_norm1_g': 'delta_w', 'delta_w_in': 'delta_w', 'delta_pool_w': 'delta_w', 'delta_pool_scale': 'delta_w', 'delta_conv_w': 'delta_w', 'delta_conv_b': 'delta_w', 'delta_gate_a_w': 'delta_w', 'delta_gate_a_b': 'delta_w', 'delta_gate_x_w': 'delta_w', 'delta_gate_x_b': 'delta_w', 'delta_lru_lambda': 'delta_w', 'delta_w_out': 'delta_w', 'delta_norm2_g': 'delta_w', 'delta_mlp_w1': 'delta_w', 'delta_mlp_w2': 'delta_w', 'delta_final_g': 'delta_w', 'new_m_meta_tokens': 'new_m', 'new_m_norm1_g': 'new_m', 'new_m_w_in': 'new_m', 'new_m_pool_w': 'new_m', 'new_m_pool_scale': 'new_m', 'new_m_conv_w': 'new_m', 'new_m_conv_b': 'new_m', 'new_m_gate_a_w': 'new_m', 'new_m_gate_a_b': 'new_m', 'new_m_gate_x_w': 'new_m', 'new_m_gate_x_b': 'new_m', 'new_m_lru_lambda': 'new_m', 'new_m_w_out': 'new_m', 'new_m_norm2_g': 'new_m', 'new_m_mlp_w1': 'new_m', 'new_m_mlp_w2': 'new_m', 'new_m_final_g': 'new_m', 'new_v_meta_tokens': 'new_v', 'new_v_norm1_g': 'new_v', 'new_v_w_in': 'new_v', 'new_v_pool_w': 'new_v', 'new_v_pool_scale': 'new_v', 'new_v_conv_w': 'new_v', 'new_v_conv_b': 'new_v', 'new_v_gate_a_w': 'new_v', 'new_v_gate_a_b': 'new_v', 'new_v_gate_x_w': 'new_v', 'new_v_gate_x_b': 'new_v', 'new_v_lru_lambda': 'new_v', 'new_v_w_out': 'new_v', 'new_v_norm2_g': 'new_v', 'new_v_mlp_w1': 'new_v', 'new_v_mlp_w2': 'new_v', 'new_v_final_g': 'new_v'}


def _forward(args):
    return _fwd_reference(*[args[k] for k in FWD_PARAMS])


def _output_shape():
    def fwd():
        inp = _fwd_setup_inputs(0)
        return _fwd_reference(*[inp[k] for k in FWD_PARAMS])
    out = _jax.eval_shape(fwd)
    return out.shape, out.dtype

N_MICROBATCH = 1
ADAM_LR = 0.001
ADAM_B1 = 0.9
ADAM_B2 = 0.999
ADAM_EPS = 1e-08
ADAM_WD = 0.01
ADAM_STEP = 10
PER_EXAMPLE_BATCH_AXIS = {'x': 0, 'loss_target': 0}
SHARED_INPUTS = []
_WEIGHT_DTYPES = {'meta_tokens': _jnp.float32, 'norm1_g': _jnp.float32, 'w_in': _jnp.float32, 'pool_w': _jnp.float32, 'pool_scale': _jnp.float32, 'conv_w': _jnp.float32, 'conv_b': _jnp.float32, 'gate_a_w': _jnp.float32, 'gate_a_b': _jnp.float32, 'gate_x_w': _jnp.float32, 'gate_x_b': _jnp.float32, 'lru_lambda': _jnp.float32, 'w_out': _jnp.float32, 'norm2_g': _jnp.float32, 'mlp_w1': _jnp.float32, 'mlp_w2': _jnp.float32, 'final_g': _jnp.float32}
MOMENT_SCALE = {'meta_tokens': 1.074353e-03, 'norm1_g': 5.270506e-02, 'w_in': 2.301957e-02, 'pool_w': 4.066961e-02, 'pool_scale': 4.236961e-02, 'conv_w': 1.958468e-02, 'conv_b': 2.131179e-01, 'gate_a_w': 5.130902e-03, 'gate_a_b': 4.430772e-03, 'gate_x_w': 9.047688e-03, 'gate_x_b': 7.427935e-03, 'lru_lambda': 9.460449e-03, 'w_out': 4.462735e-02, 'norm2_g': 7.697156e-02, 'mlp_w1': 3.770710e-02, 'mlp_w2': 7.705371e-02, 'final_g': 1.611941e+01}


def _to_microbatches(a, axis):
    t = _jnp.moveaxis(a, axis, 0)
    t = t.reshape((N_MICROBATCH, t.shape[0] // N_MICROBATCH) + t.shape[1:])
    return _jnp.moveaxis(t, 1, axis + 1)


def setup_inputs(seed: int = 0) -> dict:
    inp = _fwd_setup_inputs(seed)
    key = _jax.random.fold_in(_jax.random.key(seed), 7919)
    shape, _ = _output_shape()
    out = dict(inp)
    out["loss_target"] = _jax.random.normal(_jax.random.fold_in(key, 0), shape, _jnp.float32)
    for i, name in enumerate(TWIN_WEIGHTS):
        w = inp[name].astype(_jnp.float32)
        if MOMENT_SCALE is None:
            s = _jnp.sqrt(_jnp.mean(_jnp.square(w)) + 1e-30)
        else:
            s = MOMENT_SCALE[name]
        km, kv = _jax.random.split(_jax.random.fold_in(key, i + 1))
        out[name] = w
        out["m_" + name] = s * _jax.random.normal(km, w.shape, _jnp.float32)
        out["v_" + name] = (s * s) * _jax.random.uniform(kv, w.shape, _jnp.float32, 0.5, 1.5)
    if N_MICROBATCH > 1:
        for name, axis in PER_EXAMPLE_BATCH_AXIS.items():
            out[name] = _to_microbatches(out[name], axis)
    return {'x': out['x'], 'meta_tokens': out['meta_tokens'], 'norm1_g': out['norm1_g'], 'w_in': out['w_in'], 'pool_w': out['pool_w'], 'pool_scale': out['pool_scale'], 'conv_w': out['conv_w'], 'conv_b': out['conv_b'], 'gate_a_w': out['gate_a_w'], 'gate_a_b': out['gate_a_b'], 'gate_x_w': out['gate_x_w'], 'gate_x_b': out['gate_x_b'], 'lru_lambda': out['lru_lambda'], 'w_out': out['w_out'], 'norm2_g': out['norm2_g'], 'mlp_w1': out['mlp_w1'], 'mlp_w2': out['mlp_w2'], 'final_g': out['final_g'], 'loss_target': out['loss_target'], 'm_meta_tokens': out['m_meta_tokens'], 'm_norm1_g': out['m_norm1_g'], 'm_w_in': out['m_w_in'], 'm_pool_w': out['m_pool_w'], 'm_pool_scale': out['m_pool_scale'], 'm_conv_w': out['m_conv_w'], 'm_conv_b': out['m_conv_b'], 'm_gate_a_w': out['m_gate_a_w'], 'm_gate_a_b': out['m_gate_a_b'], 'm_gate_x_w': out['m_gate_x_w'], 'm_gate_x_b': out['m_gate_x_b'], 'm_lru_lambda': out['m_lru_lambda'], 'm_w_out': out['m_w_out'], 'm_norm2_g': out['m_norm2_g'], 'm_mlp_w1': out['m_mlp_w1'], 'm_mlp_w2': out['m_mlp_w2'], 'm_final_g': out['m_final_g'], 'v_meta_tokens': out['v_meta_tokens'], 'v_norm1_g': out['v_norm1_g'], 'v_w_in': out['v_w_in'], 'v_pool_w': out['v_pool_w'], 'v_pool_scale': out['v_pool_scale'], 'v_conv_w': out['v_conv_w'], 'v_conv_b': out['v_conv_b'], 'v_gate_a_w': out['v_gate_a_w'], 'v_gate_a_b': out['v_gate_a_b'], 'v_gate_x_w': out['v_gate_x_w'], 'v_gate_x_b': out['v_gate_x_b'], 'v_lru_lambda': out['v_lru_lambda'], 'v_w_out': out['v_w_out'], 'v_norm2_g': out['v_norm2_g'], 'v_mlp_w1': out['v_mlp_w1'], 'v_mlp_w2': out['v_mlp_w2'], 'v_final_g': out['v_final_g']}


def _loss(weights, diff, rest, loss_target):
    with _jax.named_scope("forward"):
        args = {**rest, TWIN_DIFF_INPUT: diff, **{k: w.astype(_WEIGHT_DTYPES[k]) for k, w in weights.items()}}
        y = _forward(args)
    with _jax.named_scope("loss_head"):
        err = _jnp.square(y.astype(_jnp.float32) - loss_target)
        return 0.5 * _jnp.sum(_jnp.mean(err, axis=-1)) if err.ndim else 0.5 * err


def _adamw(w, g, m, v):
    m = ADAM_B1 * m + (1.0 - ADAM_B1) * g
    v = ADAM_B2 * v + (1.0 - ADAM_B2) * _jnp.square(g)
    m_hat = m / (1.0 - ADAM_B1 ** ADAM_STEP)
    v_hat = v / (1.0 - ADAM_B2 ** ADAM_STEP)
    delta = -ADAM_LR * (m_hat / (_jnp.sqrt(v_hat) + ADAM_EPS) + ADAM_WD * w)
    return delta, m, v


def reference(x, meta_tokens, norm1_g, w_in, pool_w, pool_scale, conv_w, conv_b, gate_a_w, gate_a_b, gate_x_w, gate_x_b, lru_lambda, w_out, norm2_g, mlp_w1, mlp_w2, final_g, loss_target, m_meta_tokens, m_norm1_g, m_w_in, m_pool_w, m_pool_scale, m_conv_w, m_conv_b, m_gate_a_w, m_gate_a_b, m_gate_x_w, m_gate_x_b, m_lru_lambda, m_w_out, m_norm2_g, m_mlp_w1, m_mlp_w2, m_final_g, v_meta_tokens, v_norm1_g, v_w_in, v_pool_w, v_pool_scale, v_conv_w, v_conv_b, v_gate_a_w, v_gate_a_b, v_gate_x_w, v_gate_x_b, v_lru_lambda, v_w_out, v_norm2_g, v_mlp_w1, v_mlp_w2, v_final_g):
    given = dict(x=x, meta_tokens=meta_tokens, norm1_g=norm1_g, w_in=w_in, pool_w=pool_w, pool_scale=pool_scale, conv_w=conv_w, conv_b=conv_b, gate_a_w=gate_a_w, gate_a_b=gate_a_b, gate_x_w=gate_x_w, gate_x_b=gate_x_b, lru_lambda=lru_lambda, w_out=w_out, norm2_g=norm2_g, mlp_w1=mlp_w1, mlp_w2=mlp_w2, final_g=final_g, loss_target=loss_target, m_meta_tokens=m_meta_tokens, m_norm1_g=m_norm1_g, m_w_in=m_w_in, m_pool_w=m_pool_w, m_pool_scale=m_pool_scale, m_conv_w=m_conv_w, m_conv_b=m_conv_b, m_gate_a_w=m_gate_a_w, m_gate_a_b=m_gate_a_b, m_gate_x_w=m_gate_x_w, m_gate_x_b=m_gate_x_b, m_lru_lambda=m_lru_lambda, m_w_out=m_w_out, m_norm2_g=m_norm2_g, m_mlp_w1=m_mlp_w1, m_mlp_w2=m_mlp_w2, m_final_g=m_final_g, v_meta_tokens=v_meta_tokens, v_norm1_g=v_norm1_g, v_w_in=v_w_in, v_pool_w=v_pool_w, v_pool_scale=v_pool_scale, v_conv_w=v_conv_w, v_conv_b=v_conv_b, v_gate_a_w=v_gate_a_w, v_gate_a_b=v_gate_a_b, v_gate_x_w=v_gate_x_w, v_gate_x_b=v_gate_x_b, v_lru_lambda=v_lru_lambda, v_w_out=v_w_out, v_norm2_g=v_norm2_g, v_mlp_w1=v_mlp_w1, v_mlp_w2=v_mlp_w2, v_final_g=v_final_g)
    weights = {n: given[n] for n in TWIN_WEIGHTS}
    shared = {n: given[n] for n in SHARED_INPUTS}
    per_example = {n: given[n] for n in ['x']}
    grad_fn = _jax.value_and_grad(_loss, argnums=(0, 1))

    def one_microbatch(ex, loss_target):
        ex = dict(ex)
        diff = ex.pop(TWIN_DIFF_INPUT)
        return grad_fn(weights, diff, {**shared, **ex}, loss_target)

    if N_MICROBATCH == 1:
        loss, (grad_w, grad_x) = one_microbatch(per_example, given["loss_target"])
    else:
        def body(carry, xs):
            loss_sum, grad_sum = carry
            l_k, (gw_k, gx_k) = one_microbatch(xs[0], xs[1])
            with _jax.named_scope("update"):
                return (loss_sum + l_k, _jax.tree.map(_jnp.add, grad_sum, gw_k)), gx_k

        init = (_jnp.zeros((), _jnp.float32), _jax.tree.map(_jnp.zeros_like, weights))
        (loss, grad_w), grad_x = _jax.lax.scan(body, init, (per_example, given["loss_target"]))
    with _jax.named_scope("update"):
        delta_w, new_m, new_v = {}, {}, {}
        for n in TWIN_WEIGHTS:
            delta_w[n], new_m[n], new_v[n] = _adamw(weights[n], grad_w[n], given["m_" + n], given["v_" + n])
    return (loss, grad_x, *[grad_w[n] for n in TWIN_WEIGHTS], *[delta_w[n] for n in TWIN_WEIGHTS],
            *[new_m[n] for n in TWIN_WEIGHTS], *[new_v[n] for n in TWIN_WEIGHTS])
```

```python
import functools

import jax
import jax.numpy as jnp
from jax import lax
from jax.experimental import pallas as pl
from jax.experimental.pallas import tpu as pltpu

F32 = jnp.float32
BF16 = jnp.bfloat16
MESH = pl.DeviceIdType.MESH
N_DEV = 8
POOL_WINDOWS = (2, 4, 8, 16)
MAX_WINDOW = 16
CONV_WIDTH = 4
HALO = 8
LRU_C = 8.0
NORM_EPS = 1e-6
ADAM_LR, ADAM_B1, ADAM_B2, ADAM_EPS, ADAM_WD, ADAM_STEP = 0.001, 0.9, 0.999, 1e-08, 0.01, 10
ROW_ALIGN = 128
VMEM_LIMIT = 56 << 20
TILE = dict(norm=384, proj=384, pool=384, lru=384, wout=384, mlp1=384, mlp2=384, dact=384, tn=384,
            nt=384, dmerged=384, update=256)
MLP2_K = 1024

_NT = (((1,), (1,)), ((), ()))
_TN = (((0,), (0,)), ((), ()))


def _call(body, **kw):
    return pl.pallas_call(body, **kw)


def _cp(*sem):
    return pltpu.CompilerParams(dimension_semantics=sem, vmem_limit_bytes=VMEM_LIMIT)


def _tile(total, pref):
    best = None
    for t in range(16, min(total, pref) + 1, 16):
        if total % t == 0:
            best = t
    assert best is not None, (total, pref)
    return best


def _sds(shape, dtype):
    return jax.ShapeDtypeStruct(shape, dtype)


def _pos():
    return lax.axis_index("x"), lax.axis_index("y"), lax.axis_index("c")


def _all_gather(shards, name):
    n = len(shards)

    def body(*refs):
        ins, outs = refs[:n], refs[n:2 * n]
        send_sems, recv_sems, local_sems = refs[2 * n:]
        x, y, c = _pos()
        me, sib = (x, y, c), (x, y, 1 - c)
        chips = [(1 - x, y), (x, 1 - y), (1 - x, 1 - y)]

        def slot(p):
            return 4 * p[0] + 2 * p[1] + p[2]

        def copy(a, k, block, to, src=None):
            dst = outs[a].at[slot(block)]
            return pltpu.make_async_remote_copy(
                src_ref=dst if src is None else src, dst_ref=dst,
                send_sem=send_sems.at[7 * a + k], recv_sem=recv_sems.at[7 * a + k],
                device_id=to, device_id_type=MESH)

        mine = [pltpu.make_async_copy(ins[a], outs[a].at[slot(me)], local_sems.at[a]) for a in range(n)]
        for m in mine:
            m.start()
        first = []
        for a in range(n):
            first.append(copy(a, 0, me, sib, src=ins[a]))
            first += [copy(a, 1 + j, me, (*chip, c), src=ins[a]) for j, chip in enumerate(chips)]
        for cp in first:
            cp.start()
        passed = []
        for a in range(n):
            for j, chip in enumerate(chips):
                copy(a, 1 + j, (*chip, c), me).wait_recv()
                fwd = copy(a, 4 + j, (*chip, c), sib)
                fwd.start()
                passed.append(fwd)
        for a in range(n):
            copy(a, 0, sib, me).wait_recv()
            for j, chip in enumerate(chips):
                copy(a, 4 + j, (*chip, 1 - c), me).wait_recv()
        for cp in first + passed:
            cp.wait_send()
        for m in mine:
            m.wait()

    hbm = pl.BlockSpec(memory_space=pl.ANY)
    return _call(
        body, name=name,
        out_shape=[_sds((N_DEV,) + s.shape, s.dtype) for s in shards],
        in_specs=[hbm] * n, out_specs=[hbm] * n,
        scratch_shapes=[pltpu.SemaphoreType.DMA((7 * n,)), pltpu.SemaphoreType.DMA((7 * n,)),
                        pltpu.SemaphoreType.DMA((n,))],
    )(*shards)


def _exchange_sibling(parts, name):
    n = len(parts)

    def body(*refs):
        ins, outs = refs[:n], refs[n:2 * n]
        send_sems, recv_sems = refs[2 * n:]
        x, y, c = _pos()
        copies = []
        for a in range(n):
            for q in range(4):
                cp = pltpu.make_async_remote_copy(
                    src_ref=ins[a].at[2 * q + (1 - c)], dst_ref=outs[a].at[q],
                    send_sem=send_sems.at[4 * a + q], recv_sem=recv_sems.at[4 * a + q],
                    device_id=(x, y, 1 - c), device_id_type=MESH)
                cp.start()
                copies.append(cp)
        for cp in copies:
            cp.wait()

    hbm = pl.BlockSpec(memory_space=pl.ANY)
    return _call(
        body, name=name,
        out_shape=[_sds((4,) + p.shape[1:], p.dtype) for p in parts],
        in_specs=[hbm] * n, out_specs=[hbm] * n,
        scratch_shapes=[pltpu.SemaphoreType.DMA((4 * n,)), pltpu.SemaphoreType.DMA((4 * n,))],
    )(*parts)


def _exchange_chips(sums, name):
    n = len(sums)

    def body(*refs):
        ins, outs = refs[:n], refs[n:2 * n]
        send_sems, recv_sems = refs[2 * n:]
        x, y, c = _pos()
        chips = [(1 - x, y), (x, 1 - y), (1 - x, 1 - y)]
        copies = []
        for a in range(n):
            for k, chip in enumerate(chips):
                cp = pltpu.make_async_remote_copy(
                    src_ref=ins[a].at[2 * chip[0] + chip[1]], dst_ref=outs[a].at[k],
                    send_sem=send_sems.at[3 * a + k], recv_sem=recv_sems.at[3 * a + k],
                    device_id=(*chip, c), device_id_type=MESH)
                cp.start()
                copies.append(cp)
        for cp in copies:
            cp.wait()

    hbm = pl.BlockSpec(memory_space=pl.ANY)
    return _call(
        body, name=name,
        out_shape=[_sds((3,) + s.shape[1:], s.dtype) for s in sums],
        in_specs=[hbm] * n, out_specs=[hbm] * n,
        scratch_shapes=[pltpu.SemaphoreType.DMA((3 * n,)), pltpu.SemaphoreType.DMA((3 * n,))],
    )(*sums)


def _pair_sum(part, got, core, name):
    _, R, C = part.shape
    tr = _tile(R, TILE["update"]) if R % 16 == 0 else R

    def body(core_ref, p_ref, g_ref, o_ref):
        o_ref[...] = (p_ref[...].astype(F32) + g_ref[...].astype(F32)).astype(o_ref.dtype)

    return _call(
        body, name=name,
        grid_spec=pltpu.PrefetchScalarGridSpec(
            num_scalar_prefetch=1, grid=(4, R // tr),
            in_specs=[pl.BlockSpec((None, tr, C), lambda q, i, cr: (2 * q + cr[0], i, 0)),
                      pl.BlockSpec((None, tr, C), lambda q, i, cr: (q, i, 0))],
            out_specs=pl.BlockSpec((None, tr, C), lambda q, i, cr: (q, i, 0))),
        out_shape=_sds((4, R, C), part.dtype),
        compiler_params=_cp("parallel", "parallel"),
    )(core, part, got)


def _sum_slots(gathered, name):
    _, R, C = gathered.shape

    def body(g_ref, o_ref):
        acc = g_ref[0]
        for s in range(1, N_DEV):
            acc = acc + g_ref[s]
        o_ref[...] = acc

    return _call(body, name=name, out_shape=_sds((R, C), F32))(gathered)


def _sigmoid(z):
    return jax.nn.sigmoid(z)


def _softplus(z):
    e = jnp.exp(-jnp.abs(z))
    log1p_e = jnp.where(e < 0.01, e * (1.0 - e * (0.5 - e * (1.0 / 3.0))), jnp.log(1.0 + e))
    return jnp.maximum(z, 0.0) + log1p_e


def _neg_expm1(z):
    poly = z * (1.0 + z * (0.5 + z * (1.0 / 6.0 + z * (1.0 / 24.0 + z * (1.0 / 120.0)))))
    return -jnp.where(jnp.abs(z) < 0.05, poly, jnp.exp(z) - 1.0)


_GELU_K = 0.7978845608028654
_GELU_C = 0.044715


def _gelu_and_grad(z):
    t = jnp.tanh(_GELU_K * (z + _GELU_C * z * z * z))
    g = 0.5 * z * (1.0 + t)
    dg = 0.5 * (1.0 + t) + 0.5 * z * (1.0 - t * t) * _GELU_K * (1.0 + 3.0 * _GELU_C * z * z)
    return g, dg


def _gelu(z):
    t = jnp.tanh(_GELU_K * (z + _GELU_C * z * z * z))
    return 0.5 * z * (1.0 + t)


def _row_ids(tile_index, tm, width=1):
    return tile_index * tm + lax.broadcasted_iota(jnp.int32, (tm, width), 0)


def _shift_down(prev, cur, k):
    if k == 0:
        return cur
    ext = jnp.concatenate([prev, cur], axis=0)
    return pltpu.roll(ext, k, axis=0)[prev.shape[0]:]


def _shift_up(cur, nxt, k):
    if k == 0:
        return cur
    ext = jnp.concatenate([cur, nxt], axis=0)
    return pltpu.roll(ext, ext.shape[0] - k, axis=0)[:cur.shape[0]]


def _lru_gates(r, sp):
    log_a = -LRU_C * r * sp
    a = jnp.exp(log_a)
    mult = jnp.sqrt(_neg_expm1(2.0 * log_a))
    return a, mult


def _scan_chunks(a_ref, b_ref, out_ref, carry, n_rows, reverse):
    n_chunks = n_rows // 8
    cols = a_ref.shape[1]
    rid = lax.broadcasted_iota(jnp.int32, (8, cols), 0)
    edge = 0 if reverse else 7

    def chunk(k, h):
        ci = (n_chunks - 1 - k) if reverse else k
        rows = pl.ds(pl.multiple_of(ci * 8, 8), 8)
        a = a_ref[rows, :]
        b = b_ref[rows, :]
        for s in (1, 2, 4):
            if reverse:
                keep = rid < 8 - s
                a_n, b_n = pltpu.roll(a, 8 - s, axis=0), pltpu.roll(b, 8 - s, axis=0)
            else:
                keep = rid >= s
                a_n, b_n = pltpu.roll(a, s, axis=0), pltpu.roll(b, s, axis=0)
            b = a * jnp.where(keep, b_n, 0.0) + b
            a = a * jnp.where(keep, a_n, 1.0)
        out_ref[rows, :] = a * h + b
        a_e = jnp.sum(jnp.where(rid == edge, a, 0.0), axis=0, keepdims=True)
        b_e = jnp.sum(jnp.where(rid == edge, b, 0.0), axis=0, keepdims=True)
        return a_e * h + b_e

    return lax.fori_loop(0, n_chunks, chunk, carry)


def _norm_fwd(h, g, name):
    Tp, D = h.shape
    tm = _tile(Tp, TILE["norm"])

    def body(h_ref, g_ref, u_ref, r_ref):
        x = h_ref[...]
        r = lax.rsqrt(jnp.mean(x * x, axis=-1, keepdims=True) + NORM_EPS)
        u_ref[...] = (x * r * g_ref[...]).astype(BF16)
        r_ref[...] = r

    return _call(
        body, name=name, grid=(Tp // tm,),
        in_specs=[pl.BlockSpec((tm, D), lambda i: (i, 0)), pl.BlockSpec((1, D), lambda i: (0, 0))],
        out_specs=[pl.BlockSpec((tm, D), lambda i: (i, 0)), pl.BlockSpec((tm, 1), lambda i: (i, 0))],
        out_shape=[_sds((Tp, D), BF16), _sds((Tp, 1), F32)],
        compiler_params=_cp("parallel"),
    )(h, g)


def _proj_fwd(u, w_slots, name):
    Tp, K = u.shape
    S, _, n = w_slots.shape
    tm = _tile(Tp, TILE["proj"])

    def body(a_ref, b_ref, o_ref):
        o_ref[...] = jnp.dot(a_ref[...], b_ref[...], preferred_element_type=F32)

    return _call(
        body, name=name, grid=(S, Tp // tm),
        in_specs=[pl.BlockSpec((tm, K), lambda j, i: (i, 0)), pl.BlockSpec((None, K, n), lambda j, i: (j, 0, 0))],
        out_specs=pl.BlockSpec((tm, n), lambda j, i: (i, j)),
        out_shape=_sds((Tp, S * n), F32),
        compiler_params=_cp("parallel", "parallel"),
    )(u, w_slots)


def _mlp1_fwd(u2, w_slots, name):
    Tp, K = u2.shape
    S, _, n = w_slots.shape
    tm = _tile(Tp, TILE["mlp1"])

    def body(a_ref, b_ref, act_ref, a1_ref):
        a1 = jnp.dot(a_ref[...], b_ref[...], preferred_element_type=F32)
        relu = jnp.maximum(a1, 0.0)
        act_ref[...] = (relu * relu).astype(BF16)
        a1_ref[...] = a1.astype(BF16)

    return _call(
        body, name=name, grid=(S, Tp // tm),
        in_specs=[pl.BlockSpec((tm, K), lambda j, i: (i, 0)), pl.BlockSpec((None, K, n), lambda j, i: (j, 0, 0))],
        out_specs=[pl.BlockSpec((tm, n), lambda j, i: (i, j))] * 2,
        out_shape=[_sds((Tp, S * n), BF16)] * 2,
        compiler_params=_cp("parallel", "parallel"),
    )(u2, w_slots)


def _pool_fwd(proj, pool_w, name):
    Tp = proj.shape[0]
    G, Cg, _ = pool_w.shape
    D = G * Cg
    tm = _tile(Tp, TILE["pool"])

    def body(v_ref, w_ref, d_ref, y_ref, prev_ref):
        t = pl.program_id(0)

        @pl.when(t == 0)
        def _():
            prev_ref[...] = jnp.zeros_like(prev_ref)

        rows = _row_ids(t, tm)
        for g, win in enumerate(POOL_WINDOWS):
            cols = slice(g * Cg, (g + 1) * Cg)
            v = v_ref[:, cols]
            s = jnp.concatenate([prev_ref[:, cols], v], axis=0)
            k = 1
            while k < win:
                s = s + pltpu.roll(s, k, axis=0)
                k *= 2
            cnt = jnp.minimum(rows + 1, win).astype(F32)
            d = s[MAX_WINDOW:] / cnt - v
            d_ref[:, cols] = d.astype(BF16)
            y_ref[:, cols] = jnp.dot(d.astype(BF16), w_ref[g], preferred_element_type=F32)
        prev_ref[...] = v_ref[tm - MAX_WINDOW:, :]

    return _call(
        body, name=name, grid=(Tp // tm,),
        in_specs=[pl.BlockSpec((tm, D), lambda t: (t, 0)), pl.BlockSpec((G, Cg, Cg), lambda t: (0, 0, 0))],
        out_specs=[pl.BlockSpec((tm, D), lambda t: (t, 0))] * 2,
        out_shape=[_sds((Tp, D), BF16), _sds((Tp, D), F32)],
        scratch_shapes=[pltpu.VMEM((MAX_WINDOW, D), F32)],
        compiler_params=_cp("arbitrary"),
    )(proj, pool_w)


def _lru_fwd(proj, y_pool, scale, conv_w, conv_b, wa, ba, wx, bx, lam, name):
    Tp = proj.shape[0]
    H, hd, _ = wa.shape
    D = H * hd
    tm = _tile(Tp, TILE["lru"])
    nb = D // hd

    def body(vl_ref, vg_ref, gp_ref, gl_ref, y_ref, sc_ref, cw_ref, cb_ref, wa_ref, ba_ref, wx_ref, bx_ref,
             lam_ref, xc_ref, r_ref, i_ref, hs_ref, m_ref, prev_ref, carry_ref, a_s, b_s):
        t = pl.program_id(1)

        @pl.when(t == 0)
        def _():
            prev_ref[...] = jnp.zeros_like(prev_ref)
            carry_ref[...] = jnp.zeros_like(carry_ref)

        v = vl_ref[...]
        prev = prev_ref[...]
        xc = jnp.zeros_like(v) + cb_ref[...]
        for k in range(CONV_WIDTH):
            xc = xc + cw_ref[k:k + 1, :] * _shift_down(prev, v, CONV_WIDTH - 1 - k)
        prev_ref[...] = v[tm - HALO:, :]
        xcb = xc.astype(BF16)
        r = _sigmoid(jnp.dot(xcb, wa_ref[...], preferred_element_type=F32) + ba_ref[...])
        i = _sigmoid(jnp.dot(xcb, wx_ref[...], preferred_element_type=F32) + bx_ref[...])
        a, mult = _lru_gates(r, _softplus(-lam_ref[...]))
        a_s[...] = a
        b_s[...] = mult * (i * xc)
        xc_ref[...] = xc
        r_ref[...] = r
        i_ref[...] = i
        carry_ref[0:1, :] = _scan_chunks(a_s, b_s, hs_ref, carry_ref[0:1, :], tm, reverse=False)
        lru_out = hs_ref[...] * _gelu(vg_ref[...])
        pool_out = y_ref[...] * sc_ref[...]
        m_ref[...] = (_sigmoid(gp_ref[...]) * pool_out + _sigmoid(gl_ref[...]) * lru_out).astype(BF16)

    def piece(p):
        return pl.BlockSpec((tm, hd), lambda h, t: (t, p * nb + h))

    blk = pl.BlockSpec((tm, hd), lambda h, t: (t, h))
    vec = pl.BlockSpec((1, hd), lambda h, t: (0, h))
    mat = pl.BlockSpec((None, hd, hd), lambda h, t: (h, 0, 0))
    bias = pl.BlockSpec((None, 1, hd), lambda h, t: (h, 0, 0))
    return _call(
        body, name=name, grid=(H, Tp // tm),
        in_specs=[piece(1), piece(2), piece(3), piece(4), blk, vec,
                  pl.BlockSpec((CONV_WIDTH, hd), lambda h, t: (0, h)), vec, mat, bias, mat, bias, vec],
        out_specs=[blk] * 5,
        out_shape=[_sds((Tp, D), F32)] * 4 + [_sds((Tp, D), BF16)],
        scratch_shapes=[pltpu.VMEM((HALO, hd), F32), pltpu.VMEM((8, hd), F32),
                        pltpu.VMEM((tm, hd), F32), pltpu.VMEM((tm, hd), F32)],
        compiler_params=_cp("parallel", "arbitrary"),
    )(proj, proj, proj, proj, y_pool, scale, conv_w, conv_b, wa, ba.reshape(H, 1, hd), wx, bx.reshape(H, 1, hd), lam)


def _wout_norm_fwd(merged, w_out, h0, g2, name):
    Tp, D = h0.shape
    tm = _tile(Tp, TILE["wout"])

    def body(m_ref, w_ref, h0_ref, g_ref, h1_ref, u2_ref, r2_ref):
        h1 = h0_ref[...] + jnp.dot(m_ref[...], w_ref[...], preferred_element_type=F32)
        r = lax.rsqrt(jnp.mean(h1 * h1, axis=-1, keepdims=True) + NORM_EPS)
        h1_ref[...] = h1
        u2_ref[...] = (h1 * r * g_ref[...]).astype(BF16)
        r2_ref[...] = r

    row = pl.BlockSpec((tm, D), lambda i: (i, 0))
    return _call(
        body, name=name, grid=(Tp // tm,),
        in_specs=[row, pl.BlockSpec((D, D), lambda i: (0, 0)), row, pl.BlockSpec((1, D), lambda i: (0, 0))],
        out_specs=[row, row, pl.BlockSpec((tm, 1), lambda i: (i, 0))],
        out_shape=[_sds((Tp, D), F32), _sds((Tp, D), BF16), _sds((Tp, 1), F32)],
        compiler_params=_cp("parallel"),
    )(merged, w_out, h0, g2)


def _mlp2_loss(act, w2, h1, target, gf, n_meta, seq, name):
    Tp, D = h1.shape
    K = act.shape[1]
    tm = _tile(Tp, TILE["mlp2"])
    tk = min(K, MLP2_K)
    nk = K // tk

    def body(a_ref, w_ref, h1_ref, t_ref, g_ref, dh_ref, dhb_ref, loss_ref, dg_ref, acc_ref):
        i, k = pl.program_id(0), pl.program_id(1)

        @pl.when(k == 0)
        def _():
            acc_ref[...] = jnp.zeros_like(acc_ref)

        @pl.when((i == 0) & (k == 0))
        def _():
            loss_ref[...] = jnp.zeros_like(loss_ref)
            dg_ref[...] = jnp.zeros_like(dg_ref)

        acc_ref[...] += jnp.dot(a_ref[...], w_ref[...], preferred_element_type=F32)

        @pl.when(k == nk - 1)
        def _():
            h2 = h1_ref[...] + acc_ref[...]
            g = g_ref[...]
            r = lax.rsqrt(jnp.mean(h2 * h2, axis=-1, keepdims=True) + NORM_EPS)
            out = h2 * r * g
            rows = _row_ids(i, tm)
            valid = (rows >= n_meta) & (rows < n_meta + seq)
            diff = jnp.where(valid, out - t_ref[...], 0.0)
            loss_ref[...] += 0.5 * jnp.sum(jnp.mean(diff * diff, axis=-1, keepdims=True))
            dout = diff / D
            dg_ref[...] += jnp.sum(dout * (h2 * r), axis=0, keepdims=True)
            dog = dout * g
            dh = r * dog - h2 * (r * r * r * jnp.mean(dog * h2, axis=-1, keepdims=True))
            dh_ref[...] = dh
            dhb_ref[...] = dh.astype(BF16)

    row = pl.BlockSpec((tm, D), lambda i, k: (i, 0))
    return _call(
        body, name=name, grid=(Tp // tm, nk),
        in_specs=[pl.BlockSpec((tm, tk), lambda i, k: (i, k)), pl.BlockSpec((tk, D), lambda i, k: (k, 0)),
                  row, row, pl.BlockSpec((1, D), lambda i, k: (0, 0))],
        out_specs=[row, row, pl.BlockSpec((8, 128), lambda i, k: (0, 0)), pl.BlockSpec((1, D), lambda i, k: (0, 0))],
        out_shape=[_sds((Tp, D), F32), _sds((Tp, D), BF16), _sds((8, 128), F32), _sds((1, D), F32)],
        scratch_shapes=[pltpu.VMEM((tm, D), F32)],
        compiler_params=_cp("arbitrary", "arbitrary"),
    )(act, w2, h1, target, gf)


def _dact_bwd(dh2b, w2_slots, a1, name):
    Tp, D = dh2b.shape
    S, n, _ = w2_slots.shape
    tm = _tile(Tp, TILE["dact"])

    def body(g_ref, w_ref, a1_ref, o_ref):
        dact = lax.dot_general(g_ref[...], w_ref[...], _NT, preferred_element_type=F32)
        o_ref[...] = (dact * (2.0 * jnp.maximum(a1_ref[...].astype(F32), 0.0))).astype(BF16)

    return _call(
        body, name=name, grid=(S, Tp // tm),
        in_specs=[pl.BlockSpec((tm, D), lambda j, i: (i, 0)), pl.BlockSpec((None, n, D), lambda j, i: (j, 0, 0)),
                  pl.BlockSpec((tm, n), lambda j, i: (i, j))],
        out_specs=pl.BlockSpec((tm, n), lambda j, i: (i, j)),
        out_shape=_sds((Tp, S * n), BF16),
        compiler_params=_cp("parallel", "parallel"),
    )(dh2b, w2_slots, a1)


def _weight_grad(a, g, blocks, block_a, name):
    Tp, Ka = a.shape
    Ng = g.shape[1]
    ka = Ka // blocks if block_a else Ka
    ng = Ng if block_a else Ng // blocks
    tt = _tile(Tp, TILE["tn"])
    nt = Tp // tt

    def body(a_ref, g_ref, o_ref, acc_ref):
        t = pl.program_id(1)

        @pl.when(t == 0)
        def _():
            acc_ref[...] = jnp.zeros_like(acc_ref)

        acc_ref[...] += lax.dot_general(a_ref[...], g_ref[...], _TN, preferred_element_type=F32)

        @pl.when(t == nt - 1)
        def _():
            o_ref[...] = acc_ref[...].astype(o_ref.dtype)

    if block_a:
        a_spec = pl.BlockSpec((tt, ka), lambda j, t: (t, j))
        g_spec = pl.BlockSpec((tt, ng), lambda j, t: (t, 0))
    else:
        a_spec = pl.BlockSpec((tt, ka), lambda j, t: (t, 0))
        g_spec = pl.BlockSpec((tt, ng), lambda j, t: (t, j))
    return _call(
        body, name=name, grid=(blocks, nt),
        in_specs=[a_spec, g_spec],
        out_specs=pl.BlockSpec((None, ka, ng), lambda j, t: (j, 0, 0)),
        out_shape=_sds((blocks, ka, ng), BF16),
        scratch_shapes=[pltpu.VMEM((ka, ng), F32)],
        compiler_params=_cp("parallel", "arbitrary"),
    )(a, g)


def _nt_norm_bwd(dz, w_slots, dres, hin, rin, g, want_bf16, name):
    Tp, D = hin.shape
    S, _, n = w_slots.shape
    tm = _tile(Tp, TILE["nt"])

    def body(dz_ref, w_ref, dres_ref, h_ref, r_ref, g_ref, *rest):
        if want_bf16:
            dh_ref, dhb_ref, dg_ref, acc_ref = rest
        else:
            dh_ref, dg_ref, acc_ref = rest
        i, k = pl.program_id(0), pl.program_id(1)

        @pl.when(k == 0)
        def _():
            acc_ref[...] = jnp.zeros_like(acc_ref)

        @pl.when((i == 0) & (k == 0))
        def _():
            dg_ref[...] = jnp.zeros_like(dg_ref)

        acc_ref[...] += lax.dot_general(dz_ref[...], w_ref[...], _NT, preferred_element_type=F32)

        @pl.when(k == S - 1)
        def _():
            du = acc_ref[...]
            h = h_ref[...]
            r = r_ref[...]
            dg_ref[...] += jnp.sum(du * (h * r), axis=0, keepdims=True)
            dug = du * g_ref[...]
            dh = dres_ref[...] + r * dug - h * (r * r * r * jnp.mean(dug * h, axis=-1, keepdims=True))
            dh_ref[...] = dh
            if want_bf16:
                dhb_ref[...] = dh.astype(BF16)

    row = pl.BlockSpec((tm, D), lambda i, k: (i, 0))
    out_specs = [row] + ([row] if want_bf16 else []) + [pl.BlockSpec((1, D), lambda i, k: (0, 0))]
    out_shape = [_sds((Tp, D), F32)] + ([_sds((Tp, D), BF16)] if want_bf16 else []) + [_sds((1, D), F32)]
    return _call(
        body, name=name, grid=(Tp // tm, S),
        in_specs=[pl.BlockSpec((tm, n), lambda i, k: (i, k)), pl.BlockSpec((None, D, n), lambda i, k: (k, 0, 0)),
                  row, row, pl.BlockSpec((tm, 1), lambda i, k: (i, 0)), pl.BlockSpec((1, D), lambda i, k: (0, 0))],
        out_specs=out_specs, out_shape=out_shape,
        scratch_shapes=[pltpu.VMEM((tm, D), F32)],
        compiler_params=_cp("arbitrary", "arbitrary"),
    )(dz, w_slots, dres, hin, rin, g)


def _dmerged_bwd(dh1b, w_out, name):
    Tp, D = dh1b.shape
    tm = _tile(Tp, TILE["dmerged"])

    def body(g_ref, w_ref, o_ref):
        o_ref[...] = lax.dot_general(g_ref[...], w_ref[...], _NT, preferred_element_type=F32)

    row = pl.BlockSpec((tm, D), lambda i: (i, 0))
    return _call(
        body, name=name, grid=(Tp // tm,),
        in_specs=[row, pl.BlockSpec((D, D), lambda i: (0, 0))],
        out_specs=row, out_shape=_sds((Tp, D), F32),
        compiler_params=_cp("parallel"),
    )(dh1b, w_out)


def _pool_bwd(dmerged, proj, y_pool, d_pool, scale, pool_w, name):
    Tp, D = dmerged.shape
    G, Cg, _ = pool_w.shape
    tm = _tile(Tp, TILE["pool"])
    nt = Tp // tm

    def body(dm_ref, gp_ref, y_ref, d_ref, sc_ref, w_ref, dv_ref, dgp_ref, dw_ref, dsc_ref, next_ref):
        t = pl.program_id(0)
        tile = nt - 1 - t

        @pl.when(t == 0)
        def _():
            next_ref[...] = jnp.zeros_like(next_ref)
            dw_ref[...] = jnp.zeros_like(dw_ref)
            dsc_ref[...] = jnp.zeros_like(dsc_ref)

        rows = _row_ids(tile, tm)
        dm = dm_ref[...]
        y = y_ref[...]
        sc = sc_ref[...]
        sg = _sigmoid(gp_ref[...])
        dpo = dm * sg
        dgp_ref[...] = (dm * (y * sc) * sg * (1.0 - sg)).astype(BF16)
        dsc_ref[...] += jnp.sum(dpo * y, axis=0, keepdims=True)
        dyb = (dpo * sc).astype(BF16)
        for g, win in enumerate(POOL_WINDOWS):
            cols = slice(g * Cg, (g + 1) * Cg)
            dy = dyb[:, cols]
            dd = lax.dot_general(dy, w_ref[g], _NT, preferred_element_type=F32)
            dw_ref[g] += lax.dot_general(d_ref[:, cols], dy, _TN, preferred_element_type=F32)
            q = dd / jnp.minimum(rows + 1, win).astype(F32)
            s = jnp.concatenate([q, next_ref[:, cols]], axis=0)
            k = 1
            while k < win:
                s = s + pltpu.roll(s, s.shape[0] - k, axis=0)
                k *= 2
            dv_ref[:, cols] = (s[:tm] - dd).astype(BF16)
            next_ref[:, cols] = q[:MAX_WINDOW]

    row = pl.BlockSpec((tm, D), lambda t: (nt - 1 - t, 0))
    nb = 1
    return _call(
        body, name=name, grid=(nt,),
        in_specs=[row, pl.BlockSpec((tm, D), lambda t: (nt - 1 - t, 3 * nb)), row, row,
                  pl.BlockSpec((1, D), lambda t: (0, 0)), pl.BlockSpec((G, Cg, Cg), lambda t: (0, 0, 0))],
        out_specs=[row, row, pl.BlockSpec((G, Cg, Cg), lambda t: (0, 0, 0)), pl.BlockSpec((1, D), lambda t: (0, 0))],
        out_shape=[_sds((Tp, D), BF16), _sds((Tp, D), BF16), _sds((G, Cg, Cg), F32), _sds((1, D), F32)],
        scratch_shapes=[pltpu.VMEM((MAX_WINDOW, D), F32)],
        compiler_params=_cp("arbitrary"),
    )(dmerged, proj, y_pool, d_pool, scale, pool_w)


LRU_SMALL_ROWS = 8


def _lru_bwd(dmerged, proj, xc, r_gate, i_gate, hs, lam, conv_w, wa, wx, name):
    Tp, D = dmerged.shape
    H, hd, _ = wa.shape
    tm = _tile(Tp, TILE["lru"])
    nt = Tp // tm
    nb = D // hd
    halo_blocks = tm // HALO

    def body(dm_ref, vl_ref, vlp_ref, vg_ref, gl_ref, xc_ref, r_ref, i_ref, hs_ref, hsp_ref, lam_ref, cw_ref,
             wa_ref, wx_ref, dvl_ref, dvg_ref, dgl_ref, dwa_ref, dwx_ref, small_ref,
             mu_next_ref, dxc_next_ref, a_s, q_s, mu_s):
        t = pl.program_id(1)
        tile = nt - 1 - t

        @pl.when(t == 0)
        def _():
            mu_next_ref[...] = jnp.zeros_like(mu_next_ref)
            dxc_next_ref[...] = jnp.zeros_like(dxc_next_ref)
            dwa_ref[...] = jnp.zeros_like(dwa_ref)
            dwx_ref[...] = jnp.zeros_like(dwx_ref)
            small_ref[...] = jnp.zeros_like(small_ref)

        first = tile == 0
        dm = dm_ref[...]
        hs_t = hs_ref[...]
        xc_t = xc_ref[...]
        r = r_ref[...]
        i = i_ref[...]
        lam_v = lam_ref[...]
        sp = _softplus(-lam_v)
        a, mult = _lru_gates(r, sp)

        sg = _sigmoid(gl_ref[...])
        ge, dge = _gelu_and_grad(vg_ref[...])
        dlo = dm * sg
        dgl_ref[...] = (dm * (hs_t * ge) * sg * (1.0 - sg)).astype(BF16)
        dvg_ref[...] = (dlo * hs_t * dge).astype(BF16)
        dhs = dlo * ge

        a_s[...] = a
        q_s[...] = a * dhs
        mu_first = _scan_chunks(a_s, q_s, mu_s, mu_next_ref[0:1, :], tm, reverse=True)
        lam_t = dhs + _shift_up(mu_s[...], mu_next_ref[...], 1)
        mu_next_ref[...] = jnp.broadcast_to(mu_first, mu_next_ref.shape)

        h_prev = _shift_down(jnp.where(first, 0.0, hsp_ref[...]), hs_t, 1)
        da = lam_t * h_prev
        dmult = lam_t * (i * xc_t)
        di = lam_t * mult * xc_t
        dxc = lam_t * mult * i
        dlog_a = da * a - dmult * (a * a) / mult
        dr = dlog_a * (-LRU_C * sp)
        dlam_rows = dlog_a * (-LRU_C * r)
        dza = dr * r * (1.0 - r)
        dzx = di * i * (1.0 - i)
        dzab, dzxb = dza.astype(BF16), dzx.astype(BF16)
        xcb = xc_t.astype(BF16)
        dxc = dxc + lax.dot_general(dzab, wa_ref[...], _NT, preferred_element_type=F32)
        dxc = dxc + lax.dot_general(dzxb, wx_ref[...], _NT, preferred_element_type=F32)
        dwa_ref[...] += lax.dot_general(xcb, dzab, _TN, preferred_element_type=F32)
        dwx_ref[...] += lax.dot_general(xcb, dzxb, _TN, preferred_element_type=F32)

        dxc_next = dxc_next_ref[...]
        dv = jnp.zeros_like(dxc)
        for k in range(CONV_WIDTH):
            dv = dv + cw_ref[k:k + 1, :] * _shift_up(dxc, dxc_next, CONV_WIDTH - 1 - k)
        dvl_ref[...] = dv.astype(BF16)
        dxc_next_ref[...] = dxc[:HALO, :]

        v_t = vl_ref[...]
        v_prev = jnp.where(first, 0.0, vlp_ref[...])
        small = [jnp.sum(dza, axis=0, keepdims=True), jnp.sum(dzx, axis=0, keepdims=True),
                 jnp.sum(dlam_rows, axis=0, keepdims=True) * (-_sigmoid(-lam_v)),
                 jnp.sum(dxc, axis=0, keepdims=True)]
        for k in range(CONV_WIDTH):
            small.append(jnp.sum(dxc * _shift_down(v_prev, v_t, CONV_WIDTH - 1 - k), axis=0, keepdims=True))
        for k, row in enumerate(small):
            small_ref[k:k + 1, :] += row

    def piece(p):
        return pl.BlockSpec((tm, hd), lambda h, t: (nt - 1 - t, p * nb + h))

    def halo(p):
        return pl.BlockSpec((HALO, hd), lambda h, t: (jnp.maximum((nt - 1 - t) * halo_blocks - 1, 0), p * nb + h))

    blk = pl.BlockSpec((tm, hd), lambda h, t: (nt - 1 - t, h))
    vec = pl.BlockSpec((1, hd), lambda h, t: (0, h))
    mat = pl.BlockSpec((None, hd, hd), lambda h, t: (h, 0, 0))
    return _call(
        body, name=name, grid=(H, nt),
        in_specs=[blk, piece(1), halo(1), piece(2), piece(4), blk, blk, blk, blk, halo(0), vec,
                  pl.BlockSpec((CONV_WIDTH, hd), lambda h, t: (0, h)), mat, mat],
        out_specs=[blk, blk, blk, mat, mat, pl.BlockSpec((None, LRU_SMALL_ROWS, hd), lambda h, t: (h, 0, 0))],
        out_shape=[_sds((Tp, D), BF16)] * 3 + [_sds((H, hd, hd), F32)] * 2 + [_sds((H, LRU_SMALL_ROWS, hd), F32)],
        scratch_shapes=[pltpu.VMEM((HALO, hd), F32), pltpu.VMEM((HALO, hd), F32),
                        pltpu.VMEM((tm, hd), F32), pltpu.VMEM((tm, hd), F32), pltpu.VMEM((tm, hd), F32)],
        compiler_params=_cp("parallel", "arbitrary"),
    )(dmerged, proj, proj, proj, proj, xc, r_gate, i_gate, hs, hs, lam, conv_w, wa, wx)


def _adamw(w, g, m, v):
    m = ADAM_B1 * m + (1.0 - ADAM_B1) * g
    v = ADAM_B2 * v + (1.0 - ADAM_B2) * (g * g)
    m_hat = m / (1.0 - ADAM_B1 ** ADAM_STEP)
    v_hat = v / (1.0 - ADAM_B2 ** ADAM_STEP)
    delta = -ADAM_LR * (m_hat / (jnp.sqrt(v_hat) + ADAM_EPS) + ADAM_WD * w)
    return delta, m, v


def _reduce_update(pair_sums, chip_sums, w, m, v, chip_slot, name):
    R, C = w.shape
    tr = _tile(R, TILE["update"])

    def body(slot_ref, own_ref, got_ref, w_ref, m_ref, v_ref, g_out, d_out, m_out, v_out):
        g = own_ref[...].astype(F32)
        for k in range(3):
            g = g + got_ref[k].astype(F32)
        d, m_new, v_new = _adamw(w_ref[...], g, m_ref[...], v_ref[...])
        g_out[...] = g
        d_out[...] = d
        m_out[...] = m_new
        v_out[...] = v_new

    blk = pl.BlockSpec((tr, C), lambda i, s: (i, 0))
    return _call(
        body, name=name,
        grid_spec=pltpu.PrefetchScalarGridSpec(
            num_scalar_prefetch=1, grid=(R // tr,),
            in_specs=[pl.BlockSpec((None, tr, C), lambda i, s: (s[0], i, 0)),
                      pl.BlockSpec((3, tr, C), lambda i, s: (0, i, 0)), blk, blk, blk],
            out_specs=[blk] * 4),
        out_shape=[_sds((R, C), F32)] * 4,
        compiler_params=_cp("parallel"),
    )(chip_slot, pair_sums, chip_sums, w, m, v)


def _small_update(w, g, m, v, name):
    def body(w_ref, g_ref, m_ref, v_ref, d_out, m_out, v_out):
        d, m_new, v_new = _adamw(w_ref[...], g_ref[...], m_ref[...], v_ref[...])
        d_out[...] = d
        m_out[...] = m_new
        v_out[...] = v_new

    return _call(body, name=name, out_shape=[_sds(w.shape, F32)] * 3)(w, g, m, v)


def _slots_from_rows(full, lead):
    L, R, C = full.shape
    r = R // N_DEV
    return full.reshape(L, N_DEV, r, C).transpose(1, 0, 2, 3).reshape(N_DEV, L * r, C)


def _rows_from_slots(slots, lead):
    _, LR, C = slots.shape
    r = LR // lead
    return slots.reshape(N_DEV, lead, r, C).transpose(1, 0, 2, 3).reshape(lead, N_DEV * r, C)


def kernel(x, meta_tokens, norm1_g, w_in, pool_w, pool_scale, conv_w, conv_b, gate_a_w, gate_a_b, gate_x_w, gate_x_b, lru_lambda, w_out, norm2_g, mlp_w1, mlp_w2, final_g, loss_target, m_meta_tokens, m_norm1_g, m_w_in, m_pool_w, m_pool_scale, m_conv_w, m_conv_b, m_gate_a_w, m_gate_a_b, m_gate_x_w, m_gate_x_b, m_lru_lambda, m_w_out, m_norm2_g, m_mlp_w1, m_mlp_w2, m_final_g, v_meta_tokens, v_norm1_g, v_w_in, v_pool_w, v_pool_scale, v_conv_w, v_conv_b, v_gate_a_w, v_gate_a_b, v_gate_x_w, v_gate_x_b, v_lru_lambda, v_w_out, v_norm2_g, v_mlp_w1, v_mlp_w2, v_final_g):
    seq, D = x.shape[1], x.shape[2]
    n_meta = meta_tokens.shape[0]
    G, Cg = pool_w.shape[1], pool_w.shape[3]
    H, hd = gate_a_w.shape[1], gate_a_w.shape[3]
    T = n_meta + seq
    Tp = -(-T // ROW_ALIGN) * ROW_ALIGN
    ix, iy, ic = _pos()
    me = 4 * ix + 2 * iy + ic
    core = jnp.reshape(ic, (1,)).astype(jnp.int32)
    chip_slot = jnp.reshape(2 * ix + iy, (1,)).astype(jnp.int32)

    w_in_l, w1_l, w2_l, w_out_l = w_in[0], mlp_w1[0], mlp_w2[0], w_out[0]
    pool_l = pool_w[0].reshape(G * (Cg // N_DEV), Cg)
    wa_l = gate_a_w[0].reshape(H * (hd // N_DEV), hd)
    wx_l = gate_x_w[0].reshape(H * (hd // N_DEV), hd)
    small_params = jnp.concatenate(
        [meta_tokens, conv_w[0], jnp.zeros((4, D // N_DEV), F32)], axis=0)
    biases = jnp.concatenate([gate_a_b[0], gate_x_b[0]], axis=0)
    (w_in_g, pool_g, wa_g, wx_g, w_out_g, w1_g, w2_g, small_g, bias_g) = _all_gather(
        [w_in_l.astype(BF16), pool_l.astype(BF16), wa_l.astype(BF16), wx_l.astype(BF16), w_out_l.astype(BF16),
         w1_l.astype(BF16), w2_l.astype(BF16), small_params, biases], "gather_weights")
    pool_full = _rows_from_slots(pool_g, G)
    wa_full = _rows_from_slots(wa_g, H)
    wx_full = _rows_from_slots(wx_g, H)
    w_out_full = w_out_g.reshape(D, D)
    w2_slots = w2_g
    small_full = small_g.transpose(1, 0, 2).reshape(n_meta + 8, D)
    meta_full = small_full[:n_meta]
    conv_full = small_full[n_meta:n_meta + CONV_WIDTH]
    bias_full = bias_g.transpose(1, 0, 2).reshape(2 * H, hd)
    ba_full, bx_full = bias_full[:H], bias_full[H:]

    h0 = jnp.concatenate([meta_full, x[0], jnp.zeros((Tp - T, D), F32)], axis=0)
    target = jnp.concatenate([jnp.zeros((n_meta, D), F32), loss_target[0], jnp.zeros((Tp - T, D), F32)], axis=0)
    u, r1 = _norm_fwd(h0, norm1_g, "norm1")
    proj = _proj_fwd(u, w_in_g, "proj")
    d_pool, y_pool = _pool_fwd(proj, pool_full, "pool_fwd")
    xc, r_gate, i_gate, hs, merged = _lru_fwd(
        proj, y_pool, pool_scale, conv_full, conv_b, wa_full, ba_full, wx_full, bx_full, lru_lambda, "lru_fwd")
    h1, u2, r2 = _wout_norm_fwd(merged, w_out_full, h0, norm2_g, "wout_norm2")
    act, a1 = _mlp1_fwd(u2, w1_g, "mlp1")
    dh2, dh2b, loss_tile, d_final_g = _mlp2_loss(
        act, w2_g.reshape(-1, D), h1, target, final_g.reshape(1, D), n_meta, seq, "mlp2_loss")

    d_a1 = _dact_bwd(dh2b, w2_slots, a1, "dact")
    dw2_p = _weight_grad(act, dh2b, N_DEV, True, "dw2")
    dw1_p = _weight_grad(u2, d_a1, N_DEV, False, "dw1")
    dh1, dh1b, d_norm2_g = _nt_norm_bwd(d_a1, w1_g, dh2, h1, r2, norm2_g, True, "du2_norm2")
    dmerged = _dmerged_bwd(dh1b, w_out_full, "dmerged")
    dwout_p = _weight_grad(merged, dh1b, N_DEV, True, "dwout")
    d_vpool, d_gpool, dpool_full, d_scale = _pool_bwd(dmerged, proj, y_pool, d_pool, pool_scale, pool_full, "pool_bwd")
    d_vlru, d_vgelu, d_glru, dwa_full, dwx_full, lru_small = _lru_bwd(
        dmerged, proj, xc, r_gate, i_gate, hs, lru_lambda, conv_full, wa_full, wx_full, "lru_bwd")
    dproj = jnp.concatenate([d_vpool, d_vlru, d_vgelu, d_gpool, d_glru], axis=1)
    dwin_p = _weight_grad(u, dproj, N_DEV, False, "dwin")
    dh0, d_norm1_g = _nt_norm_bwd(dproj, w_in_g, dh1, h0, r1, norm1_g, False, "du_norm1")
    grad_x = dh0[n_meta:T][None]

    dpool_p = _slots_from_rows(dpool_full, G).astype(BF16)
    dwa_p = _slots_from_rows(dwa_full, H).astype(BF16)
    dwx_p = _slots_from_rows(dwx_full, H).astype(BF16)
    parts = [dw2_p, dw1_p, dwout_p, dwin_p, dpool_p, dwa_p, dwx_p]
    from_sibling = _exchange_sibling(parts, "reduce_sibling")
    pair_sums = [_pair_sum(p, s, core, "pair_sum_%d" % k) for k, (p, s) in enumerate(zip(parts, from_sibling))]
    chip_sums = _exchange_chips(pair_sums, "reduce_chips")
    big = {}
    names = ["mlp_w2", "mlp_w1", "w_out", "w_in", "pool_w", "gate_a_w", "gate_x_w"]
    trip = {"mlp_w2": (mlp_w2, m_mlp_w2, v_mlp_w2), "mlp_w1": (mlp_w1, m_mlp_w1, v_mlp_w1),
            "w_out": (w_out, m_w_out, v_w_out), "w_in": (w_in, m_w_in, v_w_in),
            "pool_w": (pool_w, m_pool_w, v_pool_w), "gate_a_w": (gate_a_w, m_gate_a_w, v_gate_a_w),
            "gate_x_w": (gate_x_w, m_gate_x_w, v_gate_x_w)}
    for k, nm in enumerate(names):
        w_, m_, v_ = trip[nm]
        shape2 = pair_sums[k].shape[1:]
        outs = _reduce_update(pair_sums[k], chip_sums[k], w_.reshape(shape2), m_.reshape(shape2), v_.reshape(shape2),
                              chip_slot, "update_" + nm)
        big[nm] = [o.reshape(w_.shape) for o in outs]

    lru_rows = lru_small.transpose(1, 0, 2).reshape(LRU_SMALL_ROWS, D)
    small_part = jnp.concatenate(
        [dh0[:n_meta], d_norm1_g, d_scale, d_norm2_g, d_final_g, lru_rows, jnp.zeros((4, D), F32)], axis=0)
    (small_all,) = _all_gather([small_part], "gather_small_grads")
    small_sum = _sum_slots(small_all, "sum_small_grads")
    o = n_meta
    g_meta_full = small_sum[:o]
    g_norm1, g_scale, g_norm2, g_final = (small_sum[o + k:o + k + 1] for k in range(4))
    g_ba_full, g_bx_full, g_lam, g_cb = (small_sum[o + 4 + k:o + 5 + k] for k in range(4))
    g_cw_full = small_sum[o + 8:o + 8 + CONV_WIDTH]
    dcol = D // N_DEV
    g_meta = lax.dynamic_slice_in_dim(g_meta_full, me * dcol, dcol, axis=1)
    g_cw = lax.dynamic_slice_in_dim(g_cw_full, me * dcol, dcol, axis=1)
    hcol = hd // N_DEV
    g_ba = lax.dynamic_slice_in_dim(g_ba_full.reshape(H, hd), me * hcol, hcol, axis=1)
    g_bx = lax.dynamic_slice_in_dim(g_bx_full.reshape(H, hd), me * hcol, hcol, axis=1)

    rep_w = jnp.concatenate([norm1_g, pool_scale, conv_b, lru_lambda, norm2_g, final_g.reshape(1, D)], axis=0)
    rep_g = jnp.concatenate([g_norm1, g_scale, g_cb, g_lam, g_norm2, g_final], axis=0)
    rep_m = jnp.concatenate([m_norm1_g, m_pool_scale, m_conv_b, m_lru_lambda, m_norm2_g, m_final_g.reshape(1, D)], axis=0)
    rep_v = jnp.concatenate([v_norm1_g, v_pool_scale, v_conv_b, v_lru_lambda, v_norm2_g, v_final_g.reshape(1, D)], axis=0)
    rep_d, rep_nm, rep_nv = _small_update(rep_w, rep_g, rep_m, rep_v, "update_vectors")
    col_w = jnp.concatenate([meta_tokens, conv_w[0]], axis=0)
    col_g = jnp.concatenate([g_meta, g_cw], axis=0)
    col_m = jnp.concatenate([m_meta_tokens, m_conv_w[0]], axis=0)
    col_v = jnp.concatenate([v_meta_tokens, v_conv_w[0]], axis=0)
    col_d, col_nm, col_nv = _small_update(col_w, col_g, col_m, col_v, "update_columns")
    b_w = jnp.concatenate([gate_a_b[0], gate_x_b[0]], axis=0)
    b_g = jnp.concatenate([g_ba, g_bx], axis=0)
    b_m = jnp.concatenate([m_gate_a_b[0], m_gate_x_b[0]], axis=0)
    b_v = jnp.concatenate([v_gate_a_b[0], v_gate_x_b[0]], axis=0)
    b_d, b_nm, b_nv = _small_update(b_w, b_g, b_m, b_v, "update_biases")

    def rep(arr, k, like):
        return arr[k:k + 1].reshape(like.shape)

    rep_order = {"norm1_g": 0, "pool_scale": 1, "conv_b": 2, "lru_lambda": 3, "norm2_g": 4, "final_g": 5}
    like = {"norm1_g": norm1_g, "pool_scale": pool_scale, "conv_b": conv_b, "lru_lambda": lru_lambda,
            "norm2_g": norm2_g, "final_g": final_g}

    def leaves(kind):
        rep_src = [rep_g, rep_d, rep_nm, rep_nv][kind]
        col_src = [col_g, col_d, col_nm, col_nv][kind]
        b_src = [b_g, b_d, b_nm, b_nv][kind]
        out = {}
        out["meta_tokens"] = col_src[:n_meta]
        out["conv_w"] = col_src[n_meta:][None]
        out["gate_a_b"] = b_src[:H][None]
        out["gate_x_b"] = b_src[H:][None]
        for nm, k in rep_order.items():
            out[nm] = rep(rep_src, k, like[nm])
        for nm in names:
            out[nm] = big[nm][kind]
        order = ["meta_tokens", "norm1_g", "w_in", "pool_w", "pool_scale", "conv_w", "conv_b", "gate_a_w", "gate_a_b",
                 "gate_x_w", "gate_x_b", "lru_lambda", "w_out", "norm2_g", "mlp_w1", "mlp_w2", "final_g"]
        return [out[nm] for nm in order]

    loss = lax.psum(loss_tile[0, 0], ("x", "y", "c"))
    return (loss, grad_x, *leaves(0), *leaves(1), *leaves(2), *leaves(3))
```

```python
import functools

import jax
import jax.numpy as jnp
from jax import lax
from jax.experimental import pallas as pl
from jax.experimental.pallas import tpu as pltpu

F32 = jnp.float32
BF16 = jnp.bfloat16
MESH = pl.DeviceIdType.MESH
N_DEV = 8
POOL_WINDOWS = (2, 4, 8, 16)
MAX_WINDOW = 16
CONV_WIDTH = 4
HALO = 8
LRU_C = 8.0
NORM_EPS = 1e-6
ADAM_LR, ADAM_B1, ADAM_B2, ADAM_EPS, ADAM_WD, ADAM_STEP = 0.001, 0.9, 0.999, 1e-08, 0.01, 10
ROW_ALIGN = 128
VMEM_LIMIT = 56 << 20
TILE = dict(norm=384, proj=384, pool=384, lru=384, wout=384, mlp1=384, mlp2=384, dact=384, tn=384,
            nt=384, dmerged=384, update=256)
MLP2_K = 1024

_NT = (((1,), (1,)), ((), ()))
_TN = (((0,), (0,)), ((), ()))


def _call(body, **kw):
    return pl.pallas_call(body, **kw)


def _cp(*sem):
    return pltpu.CompilerParams(dimension_semantics=sem, vmem_limit_bytes=VMEM_LIMIT)


def _tile(total, pref):
    best = None
    for t in range(16, min(total, pref) + 1, 16):
        if total % t == 0:
            best = t
    assert best is not None, (total, pref)
    return best


def _sds(shape, dtype):
    return jax.ShapeDtypeStruct(shape, dtype)


def _pos():
    return lax.axis_index("x"), lax.axis_index("y"), lax.axis_index("c")


def _all_gather(shards, name):
    n = len(shards)

    def body(*refs):
        ins, outs = refs[:n], refs[n:2 * n]
        send_sems, recv_sems, local_sems = refs[2 * n:]
        x, y, c = _pos()
        me, sib = (x, y, c), (x, y, 1 - c)
        chips = [(1 - x, y), (x, 1 - y), (1 - x, 1 - y)]

        def slot(p):
            return 4 * p[0] + 2 * p[1] + p[2]

        def copy(a, k, block, to, src=None):
            dst = outs[a].at[slot(block)]
            return pltpu.make_async_remote_copy(
                src_ref=dst if src is None else src, dst_ref=dst,
                send_sem=send_sems.at[7 * a + k], recv_sem=recv_sems.at[7 * a + k],
                device_id=to, device_id_type=MESH)

        mine = [pltpu.make_async_copy(ins[a], outs[a].at[slot(me)], local_sems.at[a]) for a in range(n)]
        for m in mine:
            m.start()
        first = []
        for a in range(n):
            first.append(copy(a, 0, me, sib, src=ins[a]))
            first += [copy(a, 1 + j, me, (*chip, c), src=ins[a]) for j, chip in enumerate(chips)]
        for cp in first:
            cp.start()
        passed = []
        for a in range(n):
            for j, chip in enumerate(chips):
                copy(a, 1 + j, (*chip, c), me).wait_recv()
                fwd = copy(a, 4 + j, (*chip, c), sib)
                fwd.start()
                passed.append(fwd)
        for a in range(n):
            copy(a, 0, sib, me).wait_recv()
            for j, chip in enumerate(chips):
                copy(a, 4 + j, (*chip, 1 - c), me).wait_recv()
        for cp in first + passed:
            cp.wait_send()
        for m in mine:
            m.wait()

    hbm = pl.BlockSpec(memory_space=pl.ANY)
    return _call(
        body, name=name,
        out_shape=[_sds((N_DEV,) + s.shape, s.dtype) for s in shards],
        in_specs=[hbm] * n, out_specs=[hbm] * n,
        scratch_shapes=[pltpu.SemaphoreType.DMA((7 * n,)), pltpu.SemaphoreType.DMA((7 * n,)),
                        pltpu.SemaphoreType.DMA((n,))],
    )(*shards)


def _other_chips(x, y):
    return [(1 - x, y), (x, 1 - y), (1 - x, 1 - y)]


class _AgStart:
    n_sem, n_local = 4, 1

    def __init__(self, shard):
        self.ins = [shard]
        self.out_shapes = [_sds((N_DEV,) + shard.shape, shard.dtype)]
        self.aliases = []

    def _peers(self):
        x, y, c = _pos()
        return [(x, y, 1 - c)] + [(*chip, c) for chip in _other_chips(x, y)]

    def _sends(self, ins, outs, sems):
        send, recv, _, base, _ = sems
        x, y, c = _pos()
        mine = outs[0].at[4 * x + 2 * y + c]
        return [pltpu.make_async_remote_copy(src_ref=ins[0], dst_ref=mine, send_sem=send.at[base + k],
                                             recv_sem=recv.at[base + k], device_id=p, device_id_type=MESH)
                for k, p in enumerate(self._peers())]

    def _arrivals(self, outs, sems):
        send, recv, _, base, _ = sems
        res = []
        for k, p in enumerate(self._peers()):
            blk = outs[0].at[4 * p[0] + 2 * p[1] + p[2]]
            res.append(pltpu.make_async_remote_copy(src_ref=blk, dst_ref=blk, send_sem=send.at[base + k],
                                                    recv_sem=recv.at[base + k], device_id=p, device_id_type=MESH))
        return res

    def _own(self, ins, outs, sems):
        x, y, c = _pos()
        return pltpu.make_async_copy(ins[0], outs[0].at[4 * x + 2 * y + c], sems[2].at[sems[4]])

    def start(self, ins, outs, sems):
        self._own(ins, outs, sems).start()
        for cp in self._sends(ins, outs, sems):
            cp.start()

    def finish(self, ins, outs, sems):
        for cp in self._arrivals(outs, sems):
            cp.wait_recv()
        for cp in self._sends(ins, outs, sems):
            cp.wait_send()
        self._own(ins, outs, sems).wait()


class _AgForward:
    n_sem, n_local = 3, 0

    def __init__(self, gathered):
        self.ins = [gathered]
        self.out_shapes = [_sds(gathered.shape, gathered.dtype)]
        self.aliases = [(0, 0)]

    def _copies(self, outs, sems, core_of_block):
        send, recv, _, base, _ = sems
        x, y, c = _pos()
        res = []
        for k, chip in enumerate(_other_chips(x, y)):
            blk = outs[0].at[4 * chip[0] + 2 * chip[1] + (c if core_of_block == "mine" else 1 - c)]
            res.append(pltpu.make_async_remote_copy(src_ref=blk, dst_ref=blk, send_sem=send.at[base + k],
                                                    recv_sem=recv.at[base + k], device_id=(x, y, 1 - c),
                                                    device_id_type=MESH))
        return res

    def start(self, ins, outs, sems):
        for cp in self._copies(outs, sems, "mine"):
            cp.start()

    def finish(self, ins, outs, sems):
        for cp in self._copies(outs, sems, "sibling"):
            cp.wait_recv()
        for cp in self._copies(outs, sems, "mine"):
            cp.wait_send()


class _RsSibling:
    n_sem, n_local = 4, 0

    def __init__(self, part):
        self.ins = [part]
        self.out_shapes = [_sds((4,) + part.shape[1:], part.dtype)]
        self.aliases = []

    def _copies(self, ins, outs, sems):
        send, recv, _, base, _ = sems
        x, y, c = _pos()
        return [pltpu.make_async_remote_copy(src_ref=ins[0].at[2 * q + (1 - c)], dst_ref=outs[0].at[q],
                                             send_sem=send.at[base + q], recv_sem=recv.at[base + q],
                                             device_id=(x, y, 1 - c), device_id_type=MESH) for q in range(4)]

    def start(self, ins, outs, sems):
        for cp in self._copies(ins, outs, sems):
            cp.start()

    def finish(self, ins, outs, sems):
        for cp in self._copies(ins, outs, sems):
            cp.wait()


class _RsChips:
    n_sem, n_local = 3, 0

    def __init__(self, pair):
        self.ins = [pair]
        self.out_shapes = [_sds((3,) + pair.shape[1:], pair.dtype)]
        self.aliases = []

    def _copies(self, ins, outs, sems):
        send, recv, _, base, _ = sems
        x, y, c = _pos()
        return [pltpu.make_async_remote_copy(src_ref=ins[0].at[2 * chip[0] + chip[1]], dst_ref=outs[0].at[k],
                                             send_sem=send.at[base + k], recv_sem=recv.at[base + k],
                                             device_id=(*chip, c), device_id_type=MESH)
                for k, chip in enumerate(_other_chips(x, y))]

    def start(self, ins, outs, sems):
        for cp in self._copies(ins, outs, sems):
            cp.start()

    def finish(self, ins, outs, sems):
        for cp in self._copies(ins, outs, sems):
            cp.wait()


def _hosted(body, tasks, *, grid, in_specs, out_specs, out_shape, scratch_shapes=(), name, semantics, operands,
            aliases=None):
    in_specs, out_specs, out_shape = list(in_specs), list(out_specs), list(out_shape)
    scratch_shapes = list(scratch_shapes)
    aliases = dict(aliases or {})
    if not tasks:
        res = _call(body, name=name, grid=grid, in_specs=in_specs, out_specs=out_specs, out_shape=out_shape,
                    scratch_shapes=scratch_shapes, input_output_aliases=aliases,
                    compiler_params=_cp(*semantics))(*operands)
        return list(res), []
    n_in, n_out, n_scr = len(in_specs), len(out_specs), len(scratch_shapes)
    t_ins = [a for t in tasks for a in t.ins]
    t_outs = [o for t in tasks for o in t.out_shapes]
    i0, o0 = n_in, n_out
    for t in tasks:
        for (i, o) in t.aliases:
            aliases[i0 + i] = o0 + o
        i0 += len(t.ins)
        o0 += len(t.out_shapes)
    n_sem = sum(t.n_sem for t in tasks)
    n_local = max(1, sum(t.n_local for t in tasks))

    def wrapped(*refs):
        cut = [n_in, len(t_ins), n_out, len(t_outs), n_scr]
        parts, at = [], 0
        for n in cut:
            parts.append(refs[at:at + n])
            at += n
        ins, tin, outs, tout, scratch = parts
        send, recv, local = refs[at:]
        first = functools.reduce(lambda a, b: a & b, [pl.program_id(d) == 0 for d in range(len(grid))])
        last = functools.reduce(lambda a, b: a & b, [pl.program_id(d) == grid[d] - 1 for d in range(len(grid))])

        def each(method):
            i, o, s, l = 0, 0, 0, 0
            for t in tasks:
                getattr(t, method)(tin[i:i + len(t.ins)], tout[o:o + len(t.out_shapes)], (send, recv, local, s, l))
                i, o, s, l = i + len(t.ins), o + len(t.out_shapes), s + t.n_sem, l + t.n_local

        @pl.when(first)
        def _():
            each("start")

        body(*ins, *outs, *scratch)

        @pl.when(last)
        def _():
            each("finish")

    hbm = pl.BlockSpec(memory_space=pl.ANY)
    res = _call(
        wrapped, name=name, grid=grid,
        in_specs=in_specs + [hbm] * len(t_ins), out_specs=out_specs + [hbm] * len(t_outs),
        out_shape=out_shape + t_outs,
        scratch_shapes=scratch_shapes + [pltpu.SemaphoreType.DMA((n_sem,)), pltpu.SemaphoreType.DMA((n_sem,)),
                                         pltpu.SemaphoreType.DMA((n_local,))],
        input_output_aliases=aliases,
        compiler_params=_cp(*(["arbitrary"] * len(grid))),
    )(*operands, *t_ins)
    res = list(res)
    task_outs, o = [], n_out
    for t in tasks:
        task_outs.append(res[o:o + len(t.out_shapes)])
        o += len(t.out_shapes)
    return res[:n_out], task_outs


def _pair_sum(part, got, core, name):
    _, R, C = part.shape
    tr = _tile(R, TILE["update"]) if R % 16 == 0 else R

    def body(core_ref, p_ref, g_ref, o_ref):
        o_ref[...] = (p_ref[...].astype(F32) + g_ref[...].astype(F32)).astype(o_ref.dtype)

    return _call(
        body, name=name,
        grid_spec=pltpu.PrefetchScalarGridSpec(
            num_scalar_prefetch=1, grid=(4, R // tr),
            in_specs=[pl.BlockSpec((None, tr, C), lambda q, i, cr: (2 * q + cr[0], i, 0)),
                      pl.BlockSpec((None, tr, C), lambda q, i, cr: (q, i, 0))],
            out_specs=pl.BlockSpec((None, tr, C), lambda q, i, cr: (q, i, 0))),
        out_shape=_sds((4, R, C), part.dtype),
        compiler_params=_cp("parallel", "parallel"),
    )(core, part, got)


def _sum_slots(gathered, name):
    _, R, C = gathered.shape

    def body(g_ref, o_ref):
        acc = g_ref[0]
        for s in range(1, N_DEV):
            acc = acc + g_ref[s]
        o_ref[...] = acc

    return _call(body, name=name, out_shape=_sds((R, C), F32))(gathered)


def _sigmoid(z):
    return jax.nn.sigmoid(z)


def _softplus(z):
    e = jnp.exp(-jnp.abs(z))
    log1p_e = jnp.where(e < 0.01, e * (1.0 - e * (0.5 - e * (1.0 / 3.0))), jnp.log(1.0 + e))
    return jnp.maximum(z, 0.0) + log1p_e


def _neg_expm1(z):
    poly = z * (1.0 + z * (0.5 + z * (1.0 / 6.0 + z * (1.0 / 24.0 + z * (1.0 / 120.0)))))
    return -jnp.where(jnp.abs(z) < 0.05, poly, jnp.exp(z) - 1.0)


_GELU_K = 0.7978845608028654
_GELU_C = 0.044715


def _gelu_and_grad(z):
    t = jnp.tanh(_GELU_K * (z + _GELU_C * z * z * z))
    g = 0.5 * z * (1.0 + t)
    dg = 0.5 * (1.0 + t) + 0.5 * z * (1.0 - t * t) * _GELU_K * (1.0 + 3.0 * _GELU_C * z * z)
    return g, dg


def _gelu(z):
    t = jnp.tanh(_GELU_K * (z + _GELU_C * z * z * z))
    return 0.5 * z * (1.0 + t)


def _row_ids(tile_index, tm, width=1):
    return tile_index * tm + lax.broadcasted_iota(jnp.int32, (tm, width), 0)


def _shift_down(prev, cur, k):
    if k == 0:
        return cur
    ext = jnp.concatenate([prev, cur], axis=0)
    return pltpu.roll(ext, k, axis=0)[prev.shape[0]:]


def _shift_up(cur, nxt, k):
    if k == 0:
        return cur
    ext = jnp.concatenate([cur, nxt], axis=0)
    return pltpu.roll(ext, ext.shape[0] - k, axis=0)[:cur.shape[0]]


def _lru_gates(r, sp):
    log_a = -LRU_C * r * sp
    a = jnp.exp(log_a)
    mult = jnp.sqrt(_neg_expm1(2.0 * log_a))
    return a, mult


def _scan_chunks(a_ref, b_ref, out_ref, carry, n_rows, reverse):
    n_chunks = n_rows // 8
    cols = a_ref.shape[1]
    rid = lax.broadcasted_iota(jnp.int32, (8, cols), 0)
    edge = 0 if reverse else 7

    def chunk(k, h):
        ci = (n_chunks - 1 - k) if reverse else k
        rows = pl.ds(pl.multiple_of(ci * 8, 8), 8)
        a = a_ref[rows, :]
        b = b_ref[rows, :]
        for s in (1, 2, 4):
            if reverse:
                keep = rid < 8 - s
                a_n, b_n = pltpu.roll(a, 8 - s, axis=0), pltpu.roll(b, 8 - s, axis=0)
            else:
                keep = rid >= s
                a_n, b_n = pltpu.roll(a, s, axis=0), pltpu.roll(b, s, axis=0)
            b = a * jnp.where(keep, b_n, 0.0) + b
            a = a * jnp.where(keep, a_n, 1.0)
        out_ref[rows, :] = a * h + b
        a_e = jnp.sum(jnp.where(rid == edge, a, 0.0), axis=0, keepdims=True)
        b_e = jnp.sum(jnp.where(rid == edge, b, 0.0), axis=0, keepdims=True)
        return a_e * h + b_e

    return lax.fori_loop(0, n_chunks, chunk, carry)


def _norm_fwd(h, g, name):
    Tp, D = h.shape
    tm = _tile(Tp, TILE["norm"])

    def body(h_ref, g_ref, u_ref, r_ref):
        x = h_ref[...]
        r = lax.rsqrt(jnp.mean(x * x, axis=-1, keepdims=True) + NORM_EPS)
        u_ref[...] = (x * r * g_ref[...]).astype(BF16)
        r_ref[...] = r

    return _call(
        body, name=name, grid=(Tp // tm,),
        in_specs=[pl.BlockSpec((tm, D), lambda i: (i, 0)), pl.BlockSpec((1, D), lambda i: (0, 0))],
        out_specs=[pl.BlockSpec((tm, D), lambda i: (i, 0)), pl.BlockSpec((tm, 1), lambda i: (i, 0))],
        out_shape=[_sds((Tp, D), BF16), _sds((Tp, 1), F32)],
        compiler_params=_cp("parallel"),
    )(h, g)


def _proj_fwd(u, w_slots, name, tasks=()):
    Tp, K = u.shape
    S, _, n = w_slots.shape
    tm = _tile(Tp, TILE["proj"])

    def body(a_ref, b_ref, o_ref):
        o_ref[...] = jnp.dot(a_ref[...], b_ref[...], preferred_element_type=F32)

    (proj,), extra = _hosted(
        body, tasks, name=name, grid=(S, Tp // tm),
        in_specs=[pl.BlockSpec((tm, K), lambda j, i: (i, 0)), pl.BlockSpec((None, K, n), lambda j, i: (j, 0, 0))],
        out_specs=[pl.BlockSpec((tm, n), lambda j, i: (i, j))],
        out_shape=[_sds((Tp, S * n), F32)],
        semantics=("parallel", "parallel"), operands=(u, w_slots))
    return proj, extra


def _mlp1_fwd(u2, w_slots, name):
    Tp, K = u2.shape
    S, _, n = w_slots.shape
    tm = _tile(Tp, TILE["mlp1"])

    def body(a_ref, b_ref, act_ref, a1_ref):
        a1 = jnp.dot(a_ref[...], b_ref[...], preferred_element_type=F32)
        relu = jnp.maximum(a1, 0.0)
        act_ref[...] = (relu * relu).astype(BF16)
        a1_ref[...] = a1.astype(BF16)

    return _call(
        body, name=name, grid=(S, Tp // tm),
        in_specs=[pl.BlockSpec((tm, K), lambda j, i: (i, 0)), pl.BlockSpec((None, K, n), lambda j, i: (j, 0, 0))],
        out_specs=[pl.BlockSpec((tm, n), lambda j, i: (i, j))] * 2,
        out_shape=[_sds((Tp, S * n), BF16)] * 2,
        compiler_params=_cp("parallel", "parallel"),
    )(u2, w_slots)


def _pool_fwd(proj, pool_w, name):
    Tp = proj.shape[0]
    G, Cg, _ = pool_w.shape
    D = G * Cg
    tm = _tile(Tp, TILE["pool"])

    def body(v_ref, w_ref, d_ref, y_ref, prev_ref):
        t = pl.program_id(0)

        @pl.when(t == 0)
        def _():
            prev_ref[...] = jnp.zeros_like(prev_ref)

        rows = _row_ids(t, tm)
        for g, win in enumerate(POOL_WINDOWS):
            cols = slice(g * Cg, (g + 1) * Cg)
            v = v_ref[:, cols]
            s = jnp.concatenate([prev_ref[:, cols], v], axis=0)
            k = 1
            while k < win:
                s = s + pltpu.roll(s, k, axis=0)
                k *= 2
            cnt = jnp.minimum(rows + 1, win).astype(F32)
            d = s[MAX_WINDOW:] / cnt - v
            d_ref[:, cols] = d.astype(BF16)
            y_ref[:, cols] = jnp.dot(d.astype(BF16), w_ref[g], preferred_element_type=F32)
        prev_ref[...] = v_ref[tm - MAX_WINDOW:, :]

    return _call(
        body, name=name, grid=(Tp // tm,),
        in_specs=[pl.BlockSpec((tm, D), lambda t: (t, 0)), pl.BlockSpec((G, Cg, Cg), lambda t: (0, 0, 0))],
        out_specs=[pl.BlockSpec((tm, D), lambda t: (t, 0))] * 2,
        out_shape=[_sds((Tp, D), BF16), _sds((Tp, D), F32)],
        scratch_shapes=[pltpu.VMEM((MAX_WINDOW, D), F32)],
        compiler_params=_cp("arbitrary"),
    )(proj, pool_w)


def _lru_fwd(proj, y_pool, scale, conv_w, conv_b, wa, ba, wx, bx, lam, name, tasks=()):
    Tp = proj.shape[0]
    H, hd, _ = wa.shape
    D = H * hd
    tm = _tile(Tp, TILE["lru"])
    nb = D // hd

    def body(vl_ref, vg_ref, gp_ref, gl_ref, y_ref, sc_ref, cw_ref, cb_ref, wa_ref, ba_ref, wx_ref, bx_ref,
             lam_ref, xc_ref, r_ref, i_ref, hs_ref, m_ref, prev_ref, carry_ref, a_s, b_s):
        t = pl.program_id(1)

        @pl.when(t == 0)
        def _():
            prev_ref[...] = jnp.zeros_like(prev_ref)
            carry_ref[...] = jnp.zeros_like(carry_ref)

        v = vl_ref[...]
        prev = prev_ref[...]
        xc = jnp.zeros_like(v) + cb_ref[...]
        for k in range(CONV_WIDTH):
            xc = xc + cw_ref[k:k + 1, :] * _shift_down(prev, v, CONV_WIDTH - 1 - k)
        prev_ref[...] = v[tm - HALO:, :]
        xcb = xc.astype(BF16)
        r = _sigmoid(jnp.dot(xcb, wa_ref[...], preferred_element_type=F32) + ba_ref[...])
        i = _sigmoid(jnp.dot(xcb, wx_ref[...], preferred_element_type=F32) + bx_ref[...])
        a, mult = _lru_gates(r, _softplus(-lam_ref[...]))
        a_s[...] = a
        b_s[...] = mult * (i * xc)
        xc_ref[...] = xc
        r_ref[...] = r
        i_ref[...] = i
        carry_ref[0:1, :] = _scan_chunks(a_s, b_s, hs_ref, carry_ref[0:1, :], tm, reverse=False)
        lru_out = hs_ref[...] * _gelu(vg_ref[...])
        pool_out = y_ref[...] * sc_ref[...]
        m_ref[...] = (_sigmoid(gp_ref[...]) * pool_out + _sigmoid(gl_ref[...]) * lru_out).astype(BF16)

    def piece(p):
        return pl.BlockSpec((tm, hd), lambda h, t: (t, p * nb + h))

    blk = pl.BlockSpec((tm, hd), lambda h, t: (t, h))
    vec = pl.BlockSpec((1, hd), lambda h, t: (0, h))
    mat = pl.BlockSpec((None, hd, hd), lambda h, t: (h, 0, 0))
    bias = pl.BlockSpec((None, 1, hd), lambda h, t: (h, 0, 0))
    return _hosted(
        body, tasks, name=name, grid=(H, Tp // tm),
        in_specs=[piece(1), piece(2), piece(3), piece(4), blk, vec,
                  pl.BlockSpec((CONV_WIDTH, hd), lambda h, t: (0, h)), vec, mat, bias, mat, bias, vec],
        out_specs=[blk] * 5,
        out_shape=[_sds((Tp, D), F32)] * 4 + [_sds((Tp, D), BF16)],
        scratch_shapes=[pltpu.VMEM((HALO, hd), F32), pltpu.VMEM((8, hd), F32),
                        pltpu.VMEM((tm, hd), F32), pltpu.VMEM((tm, hd), F32)],
        semantics=("parallel", "arbitrary"),
        operands=(proj, proj, proj, proj, y_pool, scale, conv_w, conv_b, wa, ba.reshape(H, 1, hd), wx,
                  bx.reshape(H, 1, hd), lam))


def _wout_norm_fwd(merged, w_out, h0, g2, name, tasks=()):
    Tp, D = h0.shape
    tm = _tile(Tp, TILE["wout"])

    def body(m_ref, w_ref, h0_ref, g_ref, h1_ref, u2_ref, r2_ref):
        h1 = h0_ref[...] + jnp.dot(m_ref[...], w_ref[...], preferred_element_type=F32)
        r = lax.rsqrt(jnp.mean(h1 * h1, axis=-1, keepdims=True) + NORM_EPS)
        h1_ref[...] = h1
        u2_ref[...] = (h1 * r * g_ref[...]).astype(BF16)
        r2_ref[...] = r

    row = pl.BlockSpec((tm, D), lambda i: (i, 0))
    return _hosted(
        body, tasks, name=name, grid=(Tp // tm,),
        in_specs=[row, pl.BlockSpec((D, D), lambda i: (0, 0)), row, pl.BlockSpec((1, D), lambda i: (0, 0))],
        out_specs=[row, row, pl.BlockSpec((tm, 1), lambda i: (i, 0))],
        out_shape=[_sds((Tp, D), F32), _sds((Tp, D), BF16), _sds((Tp, 1), F32)],
        semantics=("parallel",), operands=(merged, w_out, h0, g2))


def _mlp2_loss(act, w2, h1, target, gf, n_meta, seq, name):
    Tp, D = h1.shape
    K = act.shape[1]
    tm = _tile(Tp, TILE["mlp2"])
    tk = min(K, MLP2_K)
    nk = K // tk

    def body(a_ref, w_ref, h1_ref, t_ref, g_ref, dh_ref, dhb_ref, loss_ref, dg_ref, acc_ref):
        i, k = pl.program_id(0), pl.program_id(1)

        @pl.when(k == 0)
        def _():
            acc_ref[...] = jnp.zeros_like(acc_ref)

        @pl.when((i == 0) & (k == 0))
        def _():
            loss_ref[...] = jnp.zeros_like(loss_ref)
            dg_ref[...] = jnp.zeros_like(dg_ref)

        acc_ref[...] += jnp.dot(a_ref[...], w_ref[...], preferred_element_type=F32)

        @pl.when(k == nk - 1)
        def _():
            h2 = h1_ref[...] + acc_ref[...]
            g = g_ref[...]
            r = lax.rsqrt(jnp.mean(h2 * h2, axis=-1, keepdims=True) + NORM_EPS)
            out = h2 * r * g
            rows = _row_ids(i, tm)
            valid = (rows >= n_meta) & (rows < n_meta + seq)
            diff = jnp.where(valid, out - t_ref[...], 0.0)
            loss_ref[...] += 0.5 * jnp.sum(jnp.mean(diff * diff, axis=-1, keepdims=True))
            dout = diff / D
            dg_ref[...] += jnp.sum(dout * (h2 * r), axis=0, keepdims=True)
            dog = dout * g
            dh = r * dog - h2 * (r * r * r * jnp.mean(dog * h2, axis=-1, keepdims=True))
            dh_ref[...] = dh
            dhb_ref[...] = dh.astype(BF16)

    row = pl.BlockSpec((tm, D), lambda i, k: (i, 0))
    return _call(
        body, name=name, grid=(Tp // tm, nk),
        in_specs=[pl.BlockSpec((tm, tk), lambda i, k: (i, k)), pl.BlockSpec((tk, D), lambda i, k: (k, 0)),
                  row, row, pl.BlockSpec((1, D), lambda i, k: (0, 0))],
        out_specs=[row, row, pl.BlockSpec((8, 128), lambda i, k: (0, 0)), pl.BlockSpec((1, D), lambda i, k: (0, 0))],
        out_shape=[_sds((Tp, D), F32), _sds((Tp, D), BF16), _sds((8, 128), F32), _sds((1, D), F32)],
        scratch_shapes=[pltpu.VMEM((tm, D), F32)],
        compiler_params=_cp("arbitrary", "arbitrary"),
    )(act, w2, h1, target, gf)


def _dact_bwd(dh2b, w2_slots, a1, name):
    Tp, D = dh2b.shape
    S, n, _ = w2_slots.shape
    tm = _tile(Tp, TILE["dact"])

    def body(g_ref, w_ref, a1_ref, o_ref):
        dact = lax.dot_general(g_ref[...], w_ref[...], _NT, preferred_element_type=F32)
        o_ref[...] = (dact * (2.0 * jnp.maximum(a1_ref[...].astype(F32), 0.0))).astype(BF16)

    return _call(
        body, name=name, grid=(S, Tp // tm),
        in_specs=[pl.BlockSpec((tm, D), lambda j, i: (i, 0)), pl.BlockSpec((None, n, D), lambda j, i: (j, 0, 0)),
                  pl.BlockSpec((tm, n), lambda j, i: (i, j))],
        out_specs=pl.BlockSpec((tm, n), lambda j, i: (i, j)),
        out_shape=_sds((Tp, S * n), BF16),
        compiler_params=_cp("parallel", "parallel"),
    )(dh2b, w2_slots, a1)


def _weight_grad(a, g, blocks, block_a, name, tasks=()):
    Tp, Ka = a.shape
    Ng = g.shape[1]
    ka = Ka // blocks if block_a else Ka
    ng = Ng if block_a else Ng // blocks
    tt = _tile(Tp, TILE["tn"])
    nt = Tp // tt

    def body(a_ref, g_ref, o_ref, acc_ref):
        t = pl.program_id(1)

        @pl.when(t == 0)
        def _():
            acc_ref[...] = jnp.zeros_like(acc_ref)

        acc_ref[...] += lax.dot_general(a_ref[...], g_ref[...], _TN, preferred_element_type=F32)

        @pl.when(t == nt - 1)
        def _():
            o_ref[...] = acc_ref[...].astype(o_ref.dtype)

    if block_a:
        a_spec = pl.BlockSpec((tt, ka), lambda j, t: (t, j))
        g_spec = pl.BlockSpec((tt, ng), lambda j, t: (t, 0))
    else:
        a_spec = pl.BlockSpec((tt, ka), lambda j, t: (t, 0))
        g_spec = pl.BlockSpec((tt, ng), lambda j, t: (t, j))
    (dw,), extra = _hosted(
        body, tasks, name=name, grid=(blocks, nt),
        in_specs=[a_spec, g_spec],
        out_specs=[pl.BlockSpec((None, ka, ng), lambda j, t: (j, 0, 0))],
        out_shape=[_sds((blocks, ka, ng), BF16)],
        scratch_shapes=[pltpu.VMEM((ka, ng), F32)],
        semantics=("parallel", "arbitrary"), operands=(a, g))
    return dw, extra


def _nt_norm_bwd(dz, w_slots, dres, hin, rin, g, want_bf16, name, tasks=(), tiles=None, earlier=None):
    Tp, D = hin.shape
    S, _, n = w_slots.shape
    tm = _tile(Tp, TILE["nt"])
    t0, nt = tiles if tiles is not None else (0, Tp // tm)
    assert not (want_bf16 and earlier is not None)

    def body(dz_ref, w_ref, dres_ref, h_ref, r_ref, g_ref, *rest):
        if earlier is not None:
            _, dg0_ref, dh_ref, dg_ref, acc_ref = rest
        elif want_bf16:
            dh_ref, dhb_ref, dg_ref, acc_ref = rest
        else:
            dh_ref, dg_ref, acc_ref = rest
        i, k = pl.program_id(0), pl.program_id(1)

        @pl.when(k == 0)
        def _():
            acc_ref[...] = jnp.zeros_like(acc_ref)

        @pl.when((i == 0) & (k == 0))
        def _():
            dg_ref[...] = jnp.zeros_like(dg_ref) if earlier is None else dg0_ref[...]

        acc_ref[...] += lax.dot_general(dz_ref[...], w_ref[...], _NT, preferred_element_type=F32)

        @pl.when(k == S - 1)
        def _():
            du = acc_ref[...]
            h = h_ref[...]
            r = r_ref[...]
            dg_ref[...] += jnp.sum(du * (h * r), axis=0, keepdims=True)
            dug = du * g_ref[...]
            dh = dres_ref[...] + r * dug - h * (r * r * r * jnp.mean(dug * h, axis=-1, keepdims=True))
            dh_ref[...] = dh
            if want_bf16:
                dhb_ref[...] = dh.astype(BF16)

    row = pl.BlockSpec((tm, D), lambda i, k: (t0 + i, 0))
    vec = pl.BlockSpec((1, D), lambda i, k: (0, 0))
    out_specs = [row] + ([row] if want_bf16 else []) + [vec]
    out_shape = [_sds((Tp, D), F32)] + ([_sds((Tp, D), BF16)] if want_bf16 else []) + [_sds((1, D), F32)]
    in_specs = [pl.BlockSpec((tm, n), lambda i, k: (t0 + i, k)), pl.BlockSpec((None, D, n), lambda i, k: (k, 0, 0)),
                row, row, pl.BlockSpec((tm, 1), lambda i, k: (t0 + i, 0)), vec]
    operands = (dz, w_slots, dres, hin, rin, g)
    aliases = {}
    if earlier is not None:
        in_specs += [pl.BlockSpec(memory_space=pl.ANY), vec]
        operands += tuple(earlier)
        aliases = {6: 0}
    return _hosted(
        body, tasks, name=name, grid=(nt, S), in_specs=in_specs, out_specs=out_specs, out_shape=out_shape,
        scratch_shapes=[pltpu.VMEM((tm, D), F32)], semantics=("arbitrary", "arbitrary"), operands=operands,
        aliases=aliases)


def _dmerged_bwd(dh1b, w_out, name):
    Tp, D = dh1b.shape
    tm = _tile(Tp, TILE["dmerged"])

    def body(g_ref, w_ref, o_ref):
        o_ref[...] = lax.dot_general(g_ref[...], w_ref[...], _NT, preferred_element_type=F32)

    row = pl.BlockSpec((tm, D), lambda i: (i, 0))
    return _call(
        body, name=name, grid=(Tp // tm,),
        in_specs=[row, pl.BlockSpec((D, D), lambda i: (0, 0))],
        out_specs=row, out_shape=_sds((Tp, D), F32),
        compiler_params=_cp("parallel"),
    )(dh1b, w_out)


def _pool_bwd(dmerged, proj, y_pool, d_pool, scale, pool_w, name, tasks=()):
    Tp, D = dmerged.shape
    G, Cg, _ = pool_w.shape
    tm = _tile(Tp, TILE["pool"])
    nt = Tp // tm

    def body(dm_ref, gp_ref, y_ref, d_ref, sc_ref, w_ref, dv_ref, dgp_ref, dw_ref, dsc_ref, next_ref):
        t = pl.program_id(0)
        tile = nt - 1 - t

        @pl.when(t == 0)
        def _():
            next_ref[...] = jnp.zeros_like(next_ref)
            dw_ref[...] = jnp.zeros_like(dw_ref)
            dsc_ref[...] = jnp.zeros_like(dsc_ref)

        rows = _row_ids(tile, tm)
        dm = dm_ref[...]
        y = y_ref[...]
        sc = sc_ref[...]
        sg = _sigmoid(gp_ref[...])
        dpo = dm * sg
        dgp_ref[...] = (dm * (y * sc) * sg * (1.0 - sg)).astype(BF16)
        dsc_ref[...] += jnp.sum(dpo * y, axis=0, keepdims=True)
        dyb = (dpo * sc).astype(BF16)
        for g, win in enumerate(POOL_WINDOWS):
            cols = slice(g * Cg, (g + 1) * Cg)
            dy = dyb[:, cols]
            dd = lax.dot_general(dy, w_ref[g], _NT, preferred_element_type=F32)
            dw_ref[g] += lax.dot_general(d_ref[:, cols], dy, _TN, preferred_element_type=F32)
            q = dd / jnp.minimum(rows + 1, win).astype(F32)
            s = jnp.concatenate([q, next_ref[:, cols]], axis=0)
            k = 1
            while k < win:
                s = s + pltpu.roll(s, s.shape[0] - k, axis=0)
                k *= 2
            dv_ref[:, cols] = (s[:tm] - dd).astype(BF16)
            next_ref[:, cols] = q[:MAX_WINDOW]

    row = pl.BlockSpec((tm, D), lambda t: (nt - 1 - t, 0))
    return _hosted(
        body, tasks, name=name, grid=(nt,),
        in_specs=[row, pl.BlockSpec((tm, D), lambda t: (nt - 1 - t, 3)), row, row,
                  pl.BlockSpec((1, D), lambda t: (0, 0)), pl.BlockSpec((G, Cg, Cg), lambda t: (0, 0, 0))],
        out_specs=[row, row, pl.BlockSpec((G, Cg, Cg), lambda t: (0, 0, 0)), pl.BlockSpec((1, D), lambda t: (0, 0))],
        out_shape=[_sds((Tp, D), BF16), _sds((Tp, D), BF16), _sds((G, Cg, Cg), F32), _sds((1, D), F32)],
        scratch_shapes=[pltpu.VMEM((MAX_WINDOW, D), F32)],
        semantics=("arbitrary",), operands=(dmerged, proj, y_pool, d_pool, scale, pool_w))


LRU_SMALL_ROWS = 8


def _lru_bwd(dmerged, proj, xc, r_gate, i_gate, hs, lam, conv_w, wa, wx, name, tasks=()):
    Tp, D = dmerged.shape
    H, hd, _ = wa.shape
    tm = _tile(Tp, TILE["lru"])
    nt = Tp // tm
    nb = D // hd
    halo_blocks = tm // HALO

    def body(dm_ref, vl_ref, vlp_ref, vg_ref, gl_ref, xc_ref, r_ref, i_ref, hs_ref, hsp_ref, lam_ref, cw_ref,
             wa_ref, wx_ref, dvl_ref, dvg_ref, dgl_ref, dwa_ref, dwx_ref, small_ref,
             mu_next_ref, dxc_next_ref, a_s, q_s, mu_s):
        t = pl.program_id(1)
        tile = nt - 1 - t

        @pl.when(t == 0)
        def _():
            mu_next_ref[...] = jnp.zeros_like(mu_next_ref)
            dxc_next_ref[...] = jnp.zeros_like(dxc_next_ref)
            dwa_ref[...] = jnp.zeros_like(dwa_ref)
            dwx_ref[...] = jnp.zeros_like(dwx_ref)
            small_ref[...] = jnp.zeros_like(small_ref)

        first = tile == 0
        dm = dm_ref[...]
        hs_t = hs_ref[...]
        xc_t = xc_ref[...]
        r = r_ref[...]
        i = i_ref[...]
        lam_v = lam_ref[...]
        sp = _softplus(-lam_v)
        a, mult = _lru_gates(r, sp)

        sg = _sigmoid(gl_ref[...])
        ge, dge = _gelu_and_grad(vg_ref[...])
        dlo = dm * sg
        dgl_ref[...] = (dm * (hs_t * ge) * sg * (1.0 - sg)).astype(BF16)
        dvg_ref[...] = (dlo * hs_t * dge).astype(BF16)
        dhs = dlo * ge

        a_s[...] = a
        q_s[...] = a * dhs
        mu_first = _scan_chunks(a_s, q_s, mu_s, mu_next_ref[0:1, :], tm, reverse=True)
        lam_t = dhs + _shift_up(mu_s[...], mu_next_ref[...], 1)
        mu_next_ref[...] = jnp.broadcast_to(mu_first, mu_next_ref.shape)

        h_prev = _shift_down(jnp.where(first, 0.0, hsp_ref[...]), hs_t, 1)
        da = lam_t * h_prev
        dmult = lam_t * (i * xc_t)
        di = lam_t * mult * xc_t
        dxc = lam_t * mult * i
        dlog_a = da * a - dmult * (a * a) / mult
        dr = dlog_a * (-LRU_C * sp)
        dlam_rows = dlog_a * (-LRU_C * r)
        dza = dr * r * (1.0 - r)
        dzx = di * i * (1.0 - i)
        dzab, dzxb = dza.astype(BF16), dzx.astype(BF16)
        xcb = xc_t.astype(BF16)
        dxc = dxc + lax.dot_general(dzab, wa_ref[...], _NT, preferred_element_type=F32)
        dxc = dxc + lax.dot_general(dzxb, wx_ref[...], _NT, preferred_element_type=F32)
        dwa_ref[...] += lax.dot_general(xcb, dzab, _TN, preferred_element_type=F32)
        dwx_ref[...] += lax.dot_general(xcb, dzxb, _TN, preferred_element_type=F32)

        dxc_next = dxc_next_ref[...]
        dv = jnp.zeros_like(dxc)
        for k in range(CONV_WIDTH):
            dv = dv + cw_ref[k:k + 1, :] * _shift_up(dxc, dxc_next, CONV_WIDTH - 1 - k)
        dvl_ref[...] = dv.astype(BF16)
        dxc_next_ref[...] = dxc[:HALO, :]

        v_t = vl_ref[...]
        v_prev = jnp.where(first, 0.0, vlp_ref[...])
        small = [jnp.sum(dza, axis=0, keepdims=True), jnp.sum(dzx, axis=0, keepdims=True),
                 jnp.sum(dlam_rows, axis=0, keepdims=True) * (-_sigmoid(-lam_v)),
                 jnp.sum(dxc, axis=0, keepdims=True)]
        for k in range(CONV_WIDTH):
            small.append(jnp.sum(dxc * _shift_down(v_prev, v_t, CONV_WIDTH - 1 - k), axis=0, keepdims=True))
        for k, row in enumerate(small):
            small_ref[k:k + 1, :] += row

    def piece(p):
        return pl.BlockSpec((tm, hd), lambda h, t: (nt - 1 - t, p * nb + h))

    def halo(p):
        return pl.BlockSpec((HALO, hd), lambda h, t: (jnp.maximum((nt - 1 - t) * halo_blocks - 1, 0), p * nb + h))

    blk = pl.BlockSpec((tm, hd), lambda h, t: (nt - 1 - t, h))
    vec = pl.BlockSpec((1, hd), lambda h, t: (0, h))
    mat = pl.BlockSpec((None, hd, hd), lambda h, t: (h, 0, 0))
    return _hosted(
        body, tasks, name=name, grid=(H, nt),
        in_specs=[blk, piece(1), halo(1), piece(2), piece(4), blk, blk, blk, blk, halo(0), vec,
                  pl.BlockSpec((CONV_WIDTH, hd), lambda h, t: (0, h)), mat, mat],
        out_specs=[blk, blk, blk, mat, mat, pl.BlockSpec((None, LRU_SMALL_ROWS, hd), lambda h, t: (h, 0, 0))],
        out_shape=[_sds((Tp, D), BF16)] * 3 + [_sds((H, hd, hd), F32)] * 2 + [_sds((H, LRU_SMALL_ROWS, hd), F32)],
        scratch_shapes=[pltpu.VMEM((HALO, hd), F32), pltpu.VMEM((HALO, hd), F32),
                        pltpu.VMEM((tm, hd), F32), pltpu.VMEM((tm, hd), F32), pltpu.VMEM((tm, hd), F32)],
        semantics=("parallel", "arbitrary"),
        operands=(dmerged, proj, proj, proj, proj, xc, r_gate, i_gate, hs, hs, lam, conv_w, wa, wx))


def _adamw(w, g, m, v):
    m = ADAM_B1 * m + (1.0 - ADAM_B1) * g
    v = ADAM_B2 * v + (1.0 - ADAM_B2) * (g * g)
    m_hat = m / (1.0 - ADAM_B1 ** ADAM_STEP)
    v_hat = v / (1.0 - ADAM_B2 ** ADAM_STEP)
    delta = -ADAM_LR * (m_hat / (jnp.sqrt(v_hat) + ADAM_EPS) + ADAM_WD * w)
    return delta, m, v


def _reduce_update(pair_sums, chip_sums, w, m, v, chip_slot, name):
    R, C = w.shape
    tr = _tile(R, TILE["update"])

    def body(slot_ref, own_ref, got_ref, w_ref, m_ref, v_ref, g_out, d_out, m_out, v_out):
        g = own_ref[...].astype(F32)
        for k in range(3):
            g = g + got_ref[k].astype(F32)
        d, m_new, v_new = _adamw(w_ref[...], g, m_ref[...], v_ref[...])
        g_out[...] = g
        d_out[...] = d
        m_out[...] = m_new
        v_out[...] = v_new

    blk = pl.BlockSpec((tr, C), lambda i, s: (i, 0))
    return _call(
        body, name=name,
        grid_spec=pltpu.PrefetchScalarGridSpec(
            num_scalar_prefetch=1, grid=(R // tr,),
            in_specs=[pl.BlockSpec((None, tr, C), lambda i, s: (s[0], i, 0)),
                      pl.BlockSpec((3, tr, C), lambda i, s: (0, i, 0)), blk, blk, blk],
            out_specs=[blk] * 4),
        out_shape=[_sds((R, C), F32)] * 4,
        compiler_params=_cp("parallel"),
    )(chip_slot, pair_sums, chip_sums, w, m, v)


def _small_update(w, g, m, v, name):
    def body(w_ref, g_ref, m_ref, v_ref, d_out, m_out, v_out):
        d, m_new, v_new = _adamw(w_ref[...], g_ref[...], m_ref[...], v_ref[...])
        d_out[...] = d
        m_out[...] = m_new
        v_out[...] = v_new

    return _call(body, name=name, out_shape=[_sds(w.shape, F32)] * 3)(w, g, m, v)


def _slots_from_rows(full, lead):
    L, R, C = full.shape
    r = R // N_DEV
    return full.reshape(L, N_DEV, r, C).transpose(1, 0, 2, 3).reshape(N_DEV, L * r, C)


def _rows_from_slots(slots, lead):
    _, LR, C = slots.shape
    r = LR // lead
    return slots.reshape(N_DEV, lead, r, C).transpose(1, 0, 2, 3).reshape(lead, N_DEV * r, C)


def kernel(x, meta_tokens, norm1_g, w_in, pool_w, pool_scale, conv_w, conv_b, gate_a_w, gate_a_b, gate_x_w, gate_x_b, lru_lambda, w_out, norm2_g, mlp_w1, mlp_w2, final_g, loss_target, m_meta_tokens, m_norm1_g, m_w_in, m_pool_w, m_pool_scale, m_conv_w, m_conv_b, m_gate_a_w, m_gate_a_b, m_gate_x_w, m_gate_x_b, m_lru_lambda, m_w_out, m_norm2_g, m_mlp_w1, m_mlp_w2, m_final_g, v_meta_tokens, v_norm1_g, v_w_in, v_pool_w, v_pool_scale, v_conv_w, v_conv_b, v_gate_a_w, v_gate_a_b, v_gate_x_w, v_gate_x_b, v_lru_lambda, v_w_out, v_norm2_g, v_mlp_w1, v_mlp_w2, v_final_g):
    seq, D = x.shape[1], x.shape[2]
    n_meta = meta_tokens.shape[0]
    G, Cg = pool_w.shape[1], pool_w.shape[3]
    H, hd = gate_a_w.shape[1], gate_a_w.shape[3]
    T = n_meta + seq
    Tp = -(-T // ROW_ALIGN) * ROW_ALIGN
    ix, iy, ic = _pos()
    me = 4 * ix + 2 * iy + ic
    core = jnp.reshape(ic, (1,)).astype(jnp.int32)
    chip_slot = jnp.reshape(2 * ix + iy, (1,)).astype(jnp.int32)

    w_in_l, w1_l, w2_l, w_out_l = w_in[0], mlp_w1[0], mlp_w2[0], w_out[0]
    pool_l = pool_w[0].reshape(G * (Cg // N_DEV), Cg)
    wa_l = gate_a_w[0].reshape(H * (hd // N_DEV), hd)
    wx_l = gate_x_w[0].reshape(H * (hd // N_DEV), hd)
    small_params = jnp.concatenate(
        [meta_tokens, conv_w[0], jnp.zeros((4, D // N_DEV), F32)], axis=0)
    biases = jnp.concatenate([gate_a_b[0], gate_x_b[0]], axis=0)
    (w_in_g, pool_g, wa_g, wx_g, small_g, bias_g) = _all_gather(
        [w_in_l.astype(BF16), pool_l.astype(BF16), wa_l.astype(BF16), wx_l.astype(BF16), small_params, biases],
        "gather_first")
    pool_full = _rows_from_slots(pool_g, G)
    wa_full = _rows_from_slots(wa_g, H)
    wx_full = _rows_from_slots(wx_g, H)
    small_full = small_g.transpose(1, 0, 2).reshape(n_meta + 8, D)
    meta_full = small_full[:n_meta]
    conv_full = small_full[n_meta:n_meta + CONV_WIDTH]
    bias_full = bias_g.transpose(1, 0, 2).reshape(2 * H, hd)
    ba_full, bx_full = bias_full[:H], bias_full[H:]

    h0 = jnp.concatenate([meta_full, x[0], jnp.zeros((Tp - T, D), F32)], axis=0)
    target = jnp.concatenate([jnp.zeros((n_meta, D), F32), loss_target[0], jnp.zeros((Tp - T, D), F32)], axis=0)
    u, r1 = _norm_fwd(h0, norm1_g, "norm1")
    proj, ((w_out_g1,), (w1_g1,)) = _proj_fwd(
        u, w_in_g, "proj", tasks=[_AgStart(w_out_l.astype(BF16)), _AgStart(w1_l.astype(BF16))])
    d_pool, y_pool = _pool_fwd(proj, pool_full, "pool_fwd")
    (xc, r_gate, i_gate, hs, merged), ((w_out_g,), (w1_g,), (w2_g1,)) = _lru_fwd(
        proj, y_pool, pool_scale, conv_full, conv_b, wa_full, ba_full, wx_full, bx_full, lru_lambda, "lru_fwd",
        tasks=[_AgForward(w_out_g1), _AgForward(w1_g1), _AgStart(w2_l.astype(BF16))])
    w_out_full = w_out_g.reshape(D, D)
    (h1, u2, r2), ((w2_g,),) = _wout_norm_fwd(merged, w_out_full, h0, norm2_g, "wout_norm2", tasks=[_AgForward(w2_g1)])
    act, a1 = _mlp1_fwd(u2, w1_g, "mlp1")
    dh2, dh2b, loss_tile, d_final_g = _mlp2_loss(
        act, w2_g.reshape(-1, D), h1, target, final_g.reshape(1, D), n_meta, seq, "mlp2_loss")

    def pair(part, got, tag):
        return _pair_sum(part, got, core, "pair_sum_" + tag)

    d_a1 = _dact_bwd(dh2b, w2_g, a1, "dact")
    dw2_p, _ = _weight_grad(act, dh2b, N_DEV, True, "dw2")
    dw1_p, ((dw2_got,),) = _weight_grad(u2, d_a1, N_DEV, False, "dw1", tasks=[_RsSibling(dw2_p)])
    dw2_pair = pair(dw2_p, dw2_got, "w2")
    (dh1, dh1b, d_norm2_g), ((dw2_chips,), (dw1_got,)) = _nt_norm_bwd(
        d_a1, w1_g, dh2, h1, r2, norm2_g, True, "du2_norm2", tasks=[_RsChips(dw2_pair), _RsSibling(dw1_p)])
    dw1_pair = pair(dw1_p, dw1_got, "w1")
    dmerged = _dmerged_bwd(dh1b, w_out_full, "dmerged")
    dwout_p, _ = _weight_grad(merged, dh1b, N_DEV, True, "dwout")
    (d_vpool, d_gpool, dpool_full, d_scale), ((dwout_got,),) = _pool_bwd(
        dmerged, proj, y_pool, d_pool, pool_scale, pool_full, "pool_bwd", tasks=[_RsSibling(dwout_p)])
    dwout_pair = pair(dwout_p, dwout_got, "wout")
    (d_vlru, d_vgelu, d_glru, dwa_full, dwx_full, lru_small), ((dw1_chips,), (dwout_chips,)) = _lru_bwd(
        dmerged, proj, xc, r_gate, i_gate, hs, lru_lambda, conv_full, wa_full, wx_full, "lru_bwd",
        tasks=[_RsChips(dw1_pair), _RsChips(dwout_pair)])
    dproj = jnp.concatenate([d_vpool, d_vlru, d_vgelu, d_gpool, d_glru], axis=1)
    dwin_p, _ = _weight_grad(u, dproj, N_DEV, False, "dwin")
    dpool_p = _slots_from_rows(dpool_full, G).astype(BF16)
    dwa_p = _slots_from_rows(dwa_full, H).astype(BF16)
    dwx_p = _slots_from_rows(dwx_full, H).astype(BF16)
    late = [dwin_p, dpool_p, dwa_p, dwx_p]
    n_tiles = Tp // _tile(Tp, TILE["nt"])
    n_early = max(1, n_tiles // 5)
    (dh0_a, d_norm1_a), late_got = _nt_norm_bwd(
        dproj, w_in_g, dh1, h0, r1, norm1_g, False, "du_norm1_a", tasks=[_RsSibling(p) for p in late],
        tiles=(0, n_early))
    late_pair = [pair(p, g[0], "late%d" % k) for k, (p, g) in enumerate(zip(late, late_got))]
    (dh0, d_norm1_g), late_chips = _nt_norm_bwd(
        dproj, w_in_g, dh1, h0, r1, norm1_g, False, "du_norm1_b", tasks=[_RsChips(p) for p in late_pair],
        tiles=(n_early, n_tiles - n_early), earlier=(dh0_a, d_norm1_a))
    grad_x = dh0[n_meta:T][None]

    pair_sums = [dw2_pair, dw1_pair, dwout_pair] + late_pair
    chip_sums = [dw2_chips, dw1_chips, dwout_chips] + [c[0] for c in late_chips]
    big = {}
    names = ["mlp_w2", "mlp_w1", "w_out", "w_in", "pool_w", "gate_a_w", "gate_x_w"]
    trip = {"mlp_w2": (mlp_w2, m_mlp_w2, v_mlp_w2), "mlp_w1": (mlp_w1, m_mlp_w1, v_mlp_w1),
            "w_out": (w_out, m_w_out, v_w_out), "w_in": (w_in, m_w_in, v_w_in),
            "pool_w": (pool_w, m_pool_w, v_pool_w), "gate_a_w": (gate_a_w, m_gate_a_w, v_gate_a_w),
            "gate_x_w": (gate_x_w, m_gate_x_w, v_gate_x_w)}
    for k, nm in enumerate(names):
        w_, m_, v_ = trip[nm]
        shape2 = pair_sums[k].shape[1:]
        outs = _reduce_update(pair_sums[k], chip_sums[k], w_.reshape(shape2), m_.reshape(shape2), v_.reshape(shape2),
                              chip_slot, "update_" + nm)
        big[nm] = [o.reshape(w_.shape) for o in outs]

    lru_rows = lru_small.transpose(1, 0, 2).reshape(LRU_SMALL_ROWS, D)
    small_part = jnp.concatenate(
        [dh0[:n_meta], d_norm1_g, d_scale, d_norm2_g, d_final_g, lru_rows, jnp.zeros((4, D), F32)], axis=0)
    (small_all,) = _all_gather([small_part], "gather_small_grads")
    small_sum = _sum_slots(small_all, "sum_small_grads")
    o = n_meta
    g_meta_full = small_sum[:o]
    g_norm1, g_scale, g_norm2, g_final = (small_sum[o + k:o + k + 1] for k in range(4))
    g_ba_full, g_bx_full, g_lam, g_cb = (small_sum[o + 4 + k:o + 5 + k] for k in range(4))
    g_cw_full = small_sum[o + 8:o + 8 + CONV_WIDTH]
    dcol = D // N_DEV
    g_meta = lax.dynamic_slice_in_dim(g_meta_full, me * dcol, dcol, axis=1)
    g_cw = lax.dynamic_slice_in_dim(g_cw_full, me * dcol, dcol, axis=1)
    hcol = hd // N_DEV
    g_ba = lax.dynamic_slice_in_dim(g_ba_full.reshape(H, hd), me * hcol, hcol, axis=1)
    g_bx = lax.dynamic_slice_in_dim(g_bx_full.reshape(H, hd), me * hcol, hcol, axis=1)

    rep_w = jnp.concatenate([norm1_g, pool_scale, conv_b, lru_lambda, norm2_g, final_g.reshape(1, D)], axis=0)
    rep_g = jnp.concatenate([g_norm1, g_scale, g_cb, g_lam, g_norm2, g_final], axis=0)
    rep_m = jnp.concatenate([m_norm1_g, m_pool_scale, m_conv_b, m_lru_lambda, m_norm2_g, m_final_g.reshape(1, D)], axis=0)
    rep_v = jnp.concatenate([v_norm1_g, v_pool_scale, v_conv_b, v_lru_lambda, v_norm2_g, v_final_g.reshape(1, D)], axis=0)
    rep_d, rep_nm, rep_nv = _small_update(rep_w, rep_g, rep_m, rep_v, "update_vectors")
    col_w = jnp.concatenate([meta_tokens, conv_w[0]], axis=0)
    col_g = jnp.concatenate([g_meta, g_cw], axis=0)
    col_m = jnp.concatenate([m_meta_tokens, m_conv_w[0]], axis=0)
    col_v = jnp.concatenate([v_meta_tokens, v_conv_w[0]], axis=0)
    col_d, col_nm, col_nv = _small_update(col_w, col_g, col_m, col_v, "update_columns")
    b_w = jnp.concatenate([gate_a_b[0], gate_x_b[0]], axis=0)
    b_g = jnp.concatenate([g_ba, g_bx], axis=0)
    b_m = jnp.concatenate([m_gate_a_b[0], m_gate_x_b[0]], axis=0)
    b_v = jnp.concatenate([v_gate_a_b[0], v_gate_x_b[0]], axis=0)
    b_d, b_nm, b_nv = _small_update(b_w, b_g, b_m, b_v, "update_biases")

    def rep(arr, k, like):
        return arr[k:k + 1].reshape(like.shape)

    rep_order = {"norm1_g": 0, "pool_scale": 1, "conv_b": 2, "lru_lambda": 3, "norm2_g": 4, "final_g": 5}
    like = {"norm1_g": norm1_g, "pool_scale": pool_scale, "conv_b": conv_b, "lru_lambda": lru_lambda,
            "norm2_g": norm2_g, "final_g": final_g}

    def leaves(kind):
        rep_src = [rep_g, rep_d, rep_nm, rep_nv][kind]
        col_src = [col_g, col_d, col_nm, col_nv][kind]
        b_src = [b_g, b_d, b_nm, b_nv][kind]
        out = {}
        out["meta_tokens"] = col_src[:n_meta]
        out["conv_w"] = col_src[n_meta:][None]
        out["gate_a_b"] = b_src[:H][None]
        out["gate_x_b"] = b_src[H:][None]
        for nm, k in rep_order.items():
            out[nm] = rep(rep_src, k, like[nm])
        for nm in names:
            out[nm] = big[nm][kind]
        order = ["meta_tokens", "norm1_g", "w_in", "pool_w", "pool_scale", "conv_w", "conv_b", "gate_a_w", "gate_a_b",
                 "gate_x_w", "gate_x_b", "lru_lambda", "w_out", "norm2_g", "mlp_w1", "mlp_w2", "final_g"]
        return [out[nm] for nm in order]

    loss = lax.psum(loss_tile[0, 0], ("x", "y", "c"))
    return (loss, grad_x, *leaves(0), *leaves(1), *leaves(2), *leaves(3))
```

```python
import functools

import jax
import jax.numpy as jnp
from jax import lax
from jax.experimental import pallas as pl
from jax.experimental.pallas import tpu as pltpu

F32 = jnp.float32
BF16 = jnp.bfloat16
MESH = pl.DeviceIdType.MESH
N_DEV = 8
POOL_WINDOWS = (2, 4, 8, 16)
MAX_WINDOW = 16
CONV_WIDTH = 4
HALO = 8
LRU_C = 8.0
NORM_EPS = 1e-6
ADAM_LR, ADAM_B1, ADAM_B2, ADAM_EPS, ADAM_WD, ADAM_STEP = 0.001, 0.9, 0.999, 1e-08, 0.01, 10
ROW_ALIGN = 128
VMEM_LIMIT = 56 << 20
TILE = dict(norm=384, proj=704, pool=384, lru=384, wout=384, mlp1=704, mlp2=384, dact=704, tn=1408,
            nt=384, dmerged=384, update=256)
MLP2_K = 1024

_NT = (((1,), (1,)), ((), ()))
_TN = (((0,), (0,)), ((), ()))


def _call(body, **kw):
    return pl.pallas_call(body, **kw)


def _cp(*sem):
    return pltpu.CompilerParams(dimension_semantics=sem, vmem_limit_bytes=VMEM_LIMIT)


def _tile(total, pref):
    best = None
    for t in range(16, min(total, pref) + 1, 16):
        if total % t == 0:
            best = t
    assert best is not None, (total, pref)
    return best


def _sds(shape, dtype):
    return jax.ShapeDtypeStruct(shape, dtype)


def _pos():
    return lax.axis_index("x"), lax.axis_index("y"), lax.axis_index("c")


def _all_gather(shards, name):
    n = len(shards)

    def body(*refs):
        ins, outs = refs[:n], refs[n:2 * n]
        send_sems, recv_sems, local_sems = refs[2 * n:]
        x, y, c = _pos()
        me, sib = (x, y, c), (x, y, 1 - c)
        chips = [(1 - x, y), (x, 1 - y), (1 - x, 1 - y)]

        def slot(p):
            return 4 * p[0] + 2 * p[1] + p[2]

        def copy(a, k, block, to, src=None):
            dst = outs[a].at[slot(block)]
            return pltpu.make_async_remote_copy(
                src_ref=dst if src is None else src, dst_ref=dst,
                send_sem=send_sems.at[7 * a + k], recv_sem=recv_sems.at[7 * a + k],
                device_id=to, device_id_type=MESH)

        mine = [pltpu.make_async_copy(ins[a], outs[a].at[slot(me)], local_sems.at[a]) for a in range(n)]
        for m in mine:
            m.start()
        first = []
        for a in range(n):
            first.append(copy(a, 0, me, sib, src=ins[a]))
            first += [copy(a, 1 + j, me, (*chip, c), src=ins[a]) for j, chip in enumerate(chips)]
        for cp in first:
            cp.start()
        passed = []
        for a in range(n):
            for j, chip in enumerate(chips):
                copy(a, 1 + j, (*chip, c), me).wait_recv()
                fwd = copy(a, 4 + j, (*chip, c), sib)
                fwd.start()
                passed.append(fwd)
        for a in range(n):
            copy(a, 0, sib, me).wait_recv()
            for j, chip in enumerate(chips):
                copy(a, 4 + j, (*chip, 1 - c), me).wait_recv()
        for cp in first + passed:
            cp.wait_send()
        for m in mine:
            m.wait()

    hbm = pl.BlockSpec(memory_space=pl.ANY)
    return _call(
        body, name=name,
        out_shape=[_sds((N_DEV,) + s.shape, s.dtype) for s in shards],
        in_specs=[hbm] * n, out_specs=[hbm] * n,
        scratch_shapes=[pltpu.SemaphoreType.DMA((7 * n,)), pltpu.SemaphoreType.DMA((7 * n,)),
                        pltpu.SemaphoreType.DMA((n,))],
    )(*shards)


def _other_chips(x, y):
    return [(1 - x, y), (x, 1 - y), (1 - x, 1 - y)]


class _AgStart:
    n_sem, n_local = 4, 1

    def __init__(self, shard):
        self.ins = [shard]
        self.out_shapes = [_sds((N_DEV,) + shard.shape, shard.dtype)]
        self.aliases = []

    def _peers(self):
        x, y, c = _pos()
        return [(x, y, 1 - c)] + [(*chip, c) for chip in _other_chips(x, y)]

    def _sends(self, ins, outs, sems):
        send, recv, _, base, _ = sems
        x, y, c = _pos()
        mine = outs[0].at[4 * x + 2 * y + c]
        return [pltpu.make_async_remote_copy(src_ref=ins[0], dst_ref=mine, send_sem=send.at[base + k],
                                             recv_sem=recv.at[base + k], device_id=p, device_id_type=MESH)
                for k, p in enumerate(self._peers())]

    def _arrivals(self, outs, sems):
        send, recv, _, base, _ = sems
        res = []
        for k, p in enumerate(self._peers()):
            blk = outs[0].at[4 * p[0] + 2 * p[1] + p[2]]
            res.append(pltpu.make_async_remote_copy(src_ref=blk, dst_ref=blk, send_sem=send.at[base + k],
                                                    recv_sem=recv.at[base + k], device_id=p, device_id_type=MESH))
        return res

    def _own(self, ins, outs, sems):
        x, y, c = _pos()
        return pltpu.make_async_copy(ins[0], outs[0].at[4 * x + 2 * y + c], sems[2].at[sems[4]])

    def start(self, ins, outs, sems):
        self._own(ins, outs, sems).start()
        for cp in self._sends(ins, outs, sems):
            cp.start()

    def finish(self, ins, outs, sems):
        for cp in self._arrivals(outs, sems):
            cp.wait_recv()
        for cp in self._sends(ins, outs, sems):
            cp.wait_send()
        self._own(ins, outs, sems).wait()


class _AgForward:
    n_sem, n_local = 3, 0

    def __init__(self, gathered):
        self.ins = [gathered]
        self.out_shapes = [_sds(gathered.shape, gathered.dtype)]
        self.aliases = [(0, 0)]

    def _copies(self, outs, sems, core_of_block):
        send, recv, _, base, _ = sems
        x, y, c = _pos()
        res = []
        for k, chip in enumerate(_other_chips(x, y)):
            blk = outs[0].at[4 * chip[0] + 2 * chip[1] + (c if core_of_block == "mine" else 1 - c)]
            res.append(pltpu.make_async_remote_copy(src_ref=blk, dst_ref=blk, send_sem=send.at[base + k],
                                                    recv_sem=recv.at[base + k], device_id=(x, y, 1 - c),
                                                    device_id_type=MESH))
        return res

    def start(self, ins, outs, sems):
        for cp in self._copies(outs, sems, "mine"):
            cp.start()

    def finish(self, ins, outs, sems):
        for cp in self._copies(outs, sems, "sibling"):
            cp.wait_recv()
        for cp in self._copies(outs, sems, "mine"):
            cp.wait_send()


class _RsSibling:
    n_sem, n_local = 4, 0

    def __init__(self, part):
        self.ins = [part]
        self.out_shapes = [_sds((4,) + part.shape[1:], part.dtype)]
        self.aliases = []

    def _copies(self, ins, outs, sems):
        send, recv, _, base, _ = sems
        x, y, c = _pos()
        return [pltpu.make_async_remote_copy(src_ref=ins[0].at[2 * q + (1 - c)], dst_ref=outs[0].at[q],
                                             send_sem=send.at[base + q], recv_sem=recv.at[base + q],
                                             device_id=(x, y, 1 - c), device_id_type=MESH) for q in range(4)]

    def start(self, ins, outs, sems):
        for cp in self._copies(ins, outs, sems):
            cp.start()

    def finish(self, ins, outs, sems):
        for cp in self._copies(ins, outs, sems):
            cp.wait()


class _RsChips:
    n_sem, n_local = 3, 0

    def __init__(self, pair):
        self.ins = [pair]
        self.out_shapes = [_sds((3,) + pair.shape[1:], pair.dtype)]
        self.aliases = []

    def _copies(self, ins, outs, sems):
        send, recv, _, base, _ = sems
        x, y, c = _pos()
        return [pltpu.make_async_remote_copy(src_ref=ins[0].at[2 * chip[0] + chip[1]], dst_ref=outs[0].at[k],
                                             send_sem=send.at[base + k], recv_sem=recv.at[base + k],
                                             device_id=(*chip, c), device_id_type=MESH)
                for k, chip in enumerate(_other_chips(x, y))]

    def start(self, ins, outs, sems):
        for cp in self._copies(ins, outs, sems):
            cp.start()

    def finish(self, ins, outs, sems):
        for cp in self._copies(ins, outs, sems):
            cp.wait()


def _hosted(body, tasks, *, grid, in_specs, out_specs, out_shape, scratch_shapes=(), name, semantics, operands,
            aliases=None):
    in_specs, out_specs, out_shape = list(in_specs), list(out_specs), list(out_shape)
    scratch_shapes = list(scratch_shapes)
    aliases = dict(aliases or {})
    if not tasks:
        res = _call(body, name=name, grid=grid, in_specs=in_specs, out_specs=out_specs, out_shape=out_shape,
                    scratch_shapes=scratch_shapes, input_output_aliases=aliases,
                    compiler_params=_cp(*semantics))(*operands)
        return list(res), []
    n_in, n_out, n_scr = len(in_specs), len(out_specs), len(scratch_shapes)
    t_ins = [a for t in tasks for a in t.ins]
    t_outs = [o for t in tasks for o in t.out_shapes]
    i0, o0 = n_in, n_out
    for t in tasks:
        for (i, o) in t.aliases:
            aliases[i0 + i] = o0 + o
        i0 += len(t.ins)
        o0 += len(t.out_shapes)
    n_sem = sum(t.n_sem for t in tasks)
    n_local = max(1, sum(t.n_local for t in tasks))

    def wrapped(*refs):
        cut = [n_in, len(t_ins), n_out, len(t_outs), n_scr]
        parts, at = [], 0
        for n in cut:
            parts.append(refs[at:at + n])
            at += n
        ins, tin, outs, tout, scratch = parts
        send, recv, local = refs[at:]
        first = functools.reduce(lambda a, b: a & b, [pl.program_id(d) == 0 for d in range(len(grid))])
        last = functools.reduce(lambda a, b: a & b, [pl.program_id(d) == grid[d] - 1 for d in range(len(grid))])

        def each(method):
            i, o, s, l = 0, 0, 0, 0
            for t in tasks:
                getattr(t, method)(tin[i:i + len(t.ins)], tout[o:o + len(t.out_shapes)], (send, recv, local, s, l))
                i, o, s, l = i + len(t.ins), o + len(t.out_shapes), s + t.n_sem, l + t.n_local

        @pl.when(first)
        def _():
            each("start")

        body(*ins, *outs, *scratch)

        @pl.when(last)
        def _():
            each("finish")

    hbm = pl.BlockSpec(memory_space=pl.ANY)
    res = _call(
        wrapped, name=name, grid=grid,
        in_specs=in_specs + [hbm] * len(t_ins), out_specs=out_specs + [hbm] * len(t_outs),
        out_shape=out_shape + t_outs,
        scratch_shapes=scratch_shapes + [pltpu.SemaphoreType.DMA((n_sem,)), pltpu.SemaphoreType.DMA((n_sem,)),
                                         pltpu.SemaphoreType.DMA((n_local,))],
        input_output_aliases=aliases,
        compiler_params=_cp(*(["arbitrary"] * len(grid))),
    )(*operands, *t_ins)
    res = list(res)
    task_outs, o = [], n_out
    for t in tasks:
        task_outs.append(res[o:o + len(t.out_shapes)])
        o += len(t.out_shapes)
    return res[:n_out], task_outs


def _pair_sum(part, got, core, name):
    _, R, C = part.shape
    tr = _tile(R, TILE["update"]) if R % 16 == 0 else R

    def body(core_ref, p_ref, g_ref, o_ref):
        o_ref[...] = (p_ref[...].astype(F32) + g_ref[...].astype(F32)).astype(o_ref.dtype)

    return _call(
        body, name=name,
        grid_spec=pltpu.PrefetchScalarGridSpec(
            num_scalar_prefetch=1, grid=(4, R // tr),
            in_specs=[pl.BlockSpec((None, tr, C), lambda q, i, cr: (2 * q + cr[0], i, 0)),
                      pl.BlockSpec((None, tr, C), lambda q, i, cr: (q, i, 0))],
            out_specs=pl.BlockSpec((None, tr, C), lambda q, i, cr: (q, i, 0))),
        out_shape=_sds((4, R, C), part.dtype),
        compiler_params=_cp("parallel", "parallel"),
    )(core, part, got)


def _sum_slots(gathered, name):
    _, R, C = gathered.shape

    def body(g_ref, o_ref):
        acc = g_ref[0]
        for s in range(1, N_DEV):
            acc = acc + g_ref[s]
        o_ref[...] = acc

    return _call(body, name=name, out_shape=_sds((R, C), F32))(gathered)


def _sigmoid(z):
    return jax.nn.sigmoid(z)


def _softplus(z):
    e = jnp.exp(-jnp.abs(z))
    log1p_e = jnp.where(e < 0.01, e * (1.0 - e * (0.5 - e * (1.0 / 3.0))), jnp.log(1.0 + e))
    return jnp.maximum(z, 0.0) + log1p_e


_GELU_K = 0.7978845608028654
_GELU_C = 0.044715


def _gelu_and_grad(z):
    t = jnp.tanh(_GELU_K * (z + _GELU_C * z * z * z))
    g = 0.5 * z * (1.0 + t)
    dg = 0.5 * (1.0 + t) + 0.5 * z * (1.0 - t * t) * _GELU_K * (1.0 + 3.0 * _GELU_C * z * z)
    return g, dg


def _gelu(z):
    t = jnp.tanh(_GELU_K * (z + _GELU_C * z * z * z))
    return 0.5 * z * (1.0 + t)


def _row_ids(tile_index, tm, width=1):
    return tile_index * tm + lax.broadcasted_iota(jnp.int32, (tm, width), 0)


def _shift_down(prev, cur, k):
    if k == 0:
        return cur
    ext = jnp.concatenate([prev, cur], axis=0)
    return pltpu.roll(ext, k, axis=0)[prev.shape[0]:]


def _shift_up(cur, nxt, k):
    if k == 0:
        return cur
    ext = jnp.concatenate([cur, nxt], axis=0)
    return pltpu.roll(ext, ext.shape[0] - k, axis=0)[:cur.shape[0]]


def _lru_gates(r, sp):
    z = LRU_C * r * sp
    a = jnp.exp(-z)
    t = jnp.tanh(z)
    mult = jnp.sqrt(2.0 * t / (1.0 + t))
    return a, mult


def _scan_chunks(a_ref, b_ref, out_ref, carry, n_rows, reverse):
    n_chunks = n_rows // 8
    cols = a_ref.shape[1]
    rid = lax.broadcasted_iota(jnp.int32, (8, cols), 0)
    edge = 0 if reverse else 7

    def chunk(k, h):
        ci = (n_chunks - 1 - k) if reverse else k
        rows = pl.ds(pl.multiple_of(ci * 8, 8), 8)
        a = a_ref[rows, :]
        b = b_ref[rows, :]
        for s in (1, 2, 4):
            if reverse:
                keep = rid < 8 - s
                a_n, b_n = pltpu.roll(a, 8 - s, axis=0), pltpu.roll(b, 8 - s, axis=0)
            else:
                keep = rid >= s
                a_n, b_n = pltpu.roll(a, s, axis=0), pltpu.roll(b, s, axis=0)
            b = a * jnp.where(keep, b_n, 0.0) + b
            a = a * jnp.where(keep, a_n, 1.0)
        out_ref[rows, :] = a * h + b
        a_e = jnp.sum(jnp.where(rid == edge, a, 0.0), axis=0, keepdims=True)
        b_e = jnp.sum(jnp.where(rid == edge, b, 0.0), axis=0, keepdims=True)
        return a_e * h + b_e

    return lax.fori_loop(0, n_chunks, chunk, carry, unroll=4 if n_chunks % 4 == 0 else 1)


def _norm_fwd(h, g, name):
    Tp, D = h.shape
    tm = _tile(Tp, TILE["norm"])

    def body(h_ref, g_ref, u_ref, r_ref):
        x = h_ref[...]
        r = lax.rsqrt(jnp.mean(x * x, axis=-1, keepdims=True) + NORM_EPS)
        u_ref[...] = (x * r * g_ref[...]).astype(BF16)
        r_ref[...] = r

    return _call(
        body, name=name, grid=(Tp // tm,),
        in_specs=[pl.BlockSpec((tm, D), lambda i: (i, 0)), pl.BlockSpec((1, D), lambda i: (0, 0))],
        out_specs=[pl.BlockSpec((tm, D), lambda i: (i, 0)), pl.BlockSpec((tm, 1), lambda i: (i, 0))],
        out_shape=[_sds((Tp, D), BF16), _sds((Tp, 1), F32)],
        compiler_params=_cp("parallel"),
    )(h, g)


def _proj_fwd(u, w_slots, name, tasks=()):
    Tp, K = u.shape
    S, _, n = w_slots.shape
    tm = _tile(Tp, TILE["proj"])

    def body(a_ref, b_ref, o_ref):
        o_ref[...] = jnp.dot(a_ref[...], b_ref[...], preferred_element_type=F32)

    (proj,), extra = _hosted(
        body, tasks, name=name, grid=(S, Tp // tm),
        in_specs=[pl.BlockSpec((tm, K), lambda j, i: (i, 0)), pl.BlockSpec((None, K, n), lambda j, i: (j, 0, 0))],
        out_specs=[pl.BlockSpec((tm, n), lambda j, i: (i, j))],
        out_shape=[_sds((Tp, S * n), F32)],
        semantics=("parallel", "parallel"), operands=(u, w_slots))
    return proj, extra


def _mlp1_fwd(u2, w_slots, name):
    Tp, K = u2.shape
    S, _, n = w_slots.shape
    tm = _tile(Tp, TILE["mlp1"])

    def body(a_ref, b_ref, act_ref, a1_ref):
        a1 = jnp.dot(a_ref[...], b_ref[...], preferred_element_type=F32)
        relu = jnp.maximum(a1, 0.0)
        act_ref[...] = (relu * relu).astype(BF16)
        a1_ref[...] = a1.astype(BF16)

    return _call(
        body, name=name, grid=(S, Tp // tm),
        in_specs=[pl.BlockSpec((tm, K), lambda j, i: (i, 0)), pl.BlockSpec((None, K, n), lambda j, i: (j, 0, 0))],
        out_specs=[pl.BlockSpec((tm, n), lambda j, i: (i, j))] * 2,
        out_shape=[_sds((Tp, S * n), BF16)] * 2,
        compiler_params=_cp("parallel", "parallel"),
    )(u2, w_slots)


def _pool_fwd(proj, pool_w, name):
    Tp = proj.shape[0]
    G, Cg, _ = pool_w.shape
    D = G * Cg
    tm = _tile(Tp, TILE["pool"])

    def body(v_ref, w_ref, d_ref, y_ref, prev_ref):
        t = pl.program_id(0)

        @pl.when(t == 0)
        def _():
            prev_ref[...] = jnp.zeros_like(prev_ref)

        rows = _row_ids(t, tm)
        for g, win in enumerate(POOL_WINDOWS):
            cols = slice(g * Cg, (g + 1) * Cg)
            v = v_ref[:, cols]
            s = jnp.concatenate([prev_ref[:, cols], v], axis=0)
            k = 1
            while k < win:
                s = s + pltpu.roll(s, k, axis=0)
                k *= 2
            cnt = jnp.minimum(rows + 1, win).astype(F32)
            d = s[MAX_WINDOW:] / cnt - v
            d_ref[:, cols] = d.astype(BF16)
            y_ref[:, cols] = jnp.dot(d.astype(BF16), w_ref[g], preferred_element_type=F32)
        prev_ref[...] = v_ref[tm - MAX_WINDOW:, :]

    return _call(
        body, name=name, grid=(Tp // tm,),
        in_specs=[pl.BlockSpec((tm, D), lambda t: (t, 0)), pl.BlockSpec((G, Cg, Cg), lambda t: (0, 0, 0))],
        out_specs=[pl.BlockSpec((tm, D), lambda t: (t, 0))] * 2,
        out_shape=[_sds((Tp, D), BF16), _sds((Tp, D), F32)],
        scratch_shapes=[pltpu.VMEM((MAX_WINDOW, D), F32)],
        compiler_params=_cp("arbitrary"),
    )(proj, pool_w)


def _lru_fwd(proj, y_pool, scale, conv_w, conv_b, wa, ba, wx, bx, lam, name, tasks=()):
    Tp = proj.shape[0]
    H, hd, _ = wa.shape
    D = H * hd
    tm = _tile(Tp, TILE["lru"])
    nb = D // hd

    def body(vl_ref, vg_ref, gp_ref, gl_ref, y_ref, sc_ref, cw_ref, cb_ref, wa_ref, ba_ref, wx_ref, bx_ref,
             lam_ref, xc_ref, r_ref, i_ref, a_ref, mult_ref, hs_ref, m_ref, prev_ref, carry_ref, b_s):
        t = pl.program_id(1)

        @pl.when(t == 0)
        def _():
            prev_ref[...] = jnp.zeros_like(prev_ref)
            carry_ref[...] = jnp.zeros_like(carry_ref)

        v = vl_ref[...]
        prev = prev_ref[...]
        xc = jnp.zeros_like(v) + cb_ref[...]
        for k in range(CONV_WIDTH):
            xc = xc + cw_ref[k:k + 1, :] * _shift_down(prev, v, CONV_WIDTH - 1 - k)
        prev_ref[...] = v[tm - HALO:, :]
        xcb = xc.astype(BF16)
        r = _sigmoid(jnp.dot(xcb, wa_ref[...], preferred_element_type=F32) + ba_ref[...])
        i = _sigmoid(jnp.dot(xcb, wx_ref[...], preferred_element_type=F32) + bx_ref[...])
        a, mult = _lru_gates(r, _softplus(-lam_ref[...]))
        a_ref[...] = a
        mult_ref[...] = mult
        b_s[...] = mult * (i * xc)
        xc_ref[...] = xc
        r_ref[...] = r
        i_ref[...] = i
        carry_ref[0:1, :] = _scan_chunks(a_ref, b_s, hs_ref, carry_ref[0:1, :], tm, reverse=False)
        lru_out = hs_ref[...] * _gelu(vg_ref[...])
        pool_out = y_ref[...] * sc_ref[...]
        m_ref[...] = (_sigmoid(gp_ref[...]) * pool_out + _sigmoid(gl_ref[...]) * lru_out).astype(BF16)

    def piece(p):
        return pl.BlockSpec((tm, hd), lambda h, t: (t, p * nb + h))

    blk = pl.BlockSpec((tm, hd), lambda h, t: (t, h))
    vec = pl.BlockSpec((1, hd), lambda h, t: (0, h))
    mat = pl.BlockSpec((None, hd, hd), lambda h, t: (h, 0, 0))
    bias = pl.BlockSpec((None, 1, hd), lambda h, t: (h, 0, 0))
    return _hosted(
        body, tasks, name=name, grid=(H, Tp // tm),
        in_specs=[piece(1), piece(2), piece(3), piece(4), blk, vec,
                  pl.BlockSpec((CONV_WIDTH, hd), lambda h, t: (0, h)), vec, mat, bias, mat, bias, vec],
        out_specs=[blk] * 7,
        out_shape=[_sds((Tp, D), F32)] * 6 + [_sds((Tp, D), BF16)],
        scratch_shapes=[pltpu.VMEM((HALO, hd), F32), pltpu.VMEM((8, hd), F32), pltpu.VMEM((tm, hd), F32)],
        semantics=("parallel", "arbitrary"),
        operands=(proj, proj, proj, proj, y_pool, scale, conv_w, conv_b, wa, ba.reshape(H, 1, hd), wx,
                  bx.reshape(H, 1, hd), lam))


def _wout_norm_fwd(merged, w_out, h0, g2, name, tasks=()):
    Tp, D = h0.shape
    tm = _tile(Tp, TILE["wout"])

    def body(m_ref, w_ref, h0_ref, g_ref, h1_ref, u2_ref, r2_ref):
        h1 = h0_ref[...] + jnp.dot(m_ref[...], w_ref[...], preferred_element_type=F32)
        r = lax.rsqrt(jnp.mean(h1 * h1, axis=-1, keepdims=True) + NORM_EPS)
        h1_ref[...] = h1
        u2_ref[...] = (h1 * r * g_ref[...]).astype(BF16)
        r2_ref[...] = r

    row = pl.BlockSpec((tm, D), lambda i: (i, 0))
    return _hosted(
        body, tasks, name=name, grid=(Tp // tm,),
        in_specs=[row, pl.BlockSpec((D, D), lambda i: (0, 0)), row, pl.BlockSpec((1, D), lambda i: (0, 0))],
        out_specs=[row, row, pl.BlockSpec((tm, 1), lambda i: (i, 0))],
        out_shape=[_sds((Tp, D), F32), _sds((Tp, D), BF16), _sds((Tp, 1), F32)],
        semantics=("parallel",), operands=(merged, w_out, h0, g2))


def _mlp2_loss(act, w2, h1, target, gf, n_meta, seq, name):
    Tp, D = h1.shape
    K = act.shape[1]
    tm = _tile(Tp, TILE["mlp2"])
    tk = min(K, MLP2_K)
    nk = K // tk

    def body(a_ref, w_ref, h1_ref, t_ref, g_ref, dh_ref, dhb_ref, loss_ref, dg_ref, acc_ref):
        i, k = pl.program_id(0), pl.program_id(1)

        @pl.when(k == 0)
        def _():
            acc_ref[...] = jnp.zeros_like(acc_ref)

        @pl.when((i == 0) & (k == 0))
        def _():
            loss_ref[...] = jnp.zeros_like(loss_ref)
            dg_ref[...] = jnp.zeros_like(dg_ref)

        acc_ref[...] += jnp.dot(a_ref[...], w_ref[...], preferred_element_type=F32)

        @pl.when(k == nk - 1)
        def _():
            h2 = h1_ref[...] + acc_ref[...]
            g = g_ref[...]
            r = lax.rsqrt(jnp.mean(h2 * h2, axis=-1, keepdims=True) + NORM_EPS)
            out = h2 * r * g
            rows = _row_ids(i, tm)
            valid = (rows >= n_meta) & (rows < n_meta + seq)
            diff = jnp.where(valid, out - t_ref[...], 0.0)
            loss_ref[...] += 0.5 * jnp.sum(jnp.mean(diff * diff, axis=-1, keepdims=True))
            dout = diff / D
            dg_ref[...] += jnp.sum(dout * (h2 * r), axis=0, keepdims=True)
            dog = dout * g
            dh = r * dog - h2 * (r * r * r * jnp.mean(dog * h2, axis=-1, keepdims=True))
            dh_ref[...] = dh
            dhb_ref[...] = dh.astype(BF16)

    row = pl.BlockSpec((tm, D), lambda i, k: (i, 0))
    return _call(
        body, name=name, grid=(Tp // tm, nk),
        in_specs=[pl.BlockSpec((tm, tk), lambda i, k: (i, k)), pl.BlockSpec((tk, D), lambda i, k: (k, 0)),
                  row, row, pl.BlockSpec((1, D), lambda i, k: (0, 0))],
        out_specs=[row, row, pl.BlockSpec((8, 128), lambda i, k: (0, 0)), pl.BlockSpec((1, D), lambda i, k: (0, 0))],
        out_shape=[_sds((Tp, D), F32), _sds((Tp, D), BF16), _sds((8, 128), F32), _sds((1, D), F32)],
        scratch_shapes=[pltpu.VMEM((tm, D), F32)],
        compiler_params=_cp("arbitrary", "arbitrary"),
    )(act, w2, h1, target, gf)


def _dact_bwd(dh2b, w2_slots, a1, name):
    Tp, D = dh2b.shape
    S, n, _ = w2_slots.shape
    tm = _tile(Tp, TILE["dact"])

    def body(g_ref, w_ref, a1_ref, o_ref):
        dact = lax.dot_general(g_ref[...], w_ref[...], _NT, preferred_element_type=F32)
        o_ref[...] = (dact * (2.0 * jnp.maximum(a1_ref[...].astype(F32), 0.0))).astype(BF16)

    return _call(
        body, name=name, grid=(S, Tp // tm),
        in_specs=[pl.BlockSpec((tm, D), lambda j, i: (i, 0)), pl.BlockSpec((None, n, D), lambda j, i: (j, 0, 0)),
                  pl.BlockSpec((tm, n), lambda j, i: (i, j))],
        out_specs=pl.BlockSpec((tm, n), lambda j, i: (i, j)),
        out_shape=_sds((Tp, S * n), BF16),
        compiler_params=_cp("parallel", "parallel"),
    )(dh2b, w2_slots, a1)


def _weight_grad(a, g, blocks, block_a, name, tasks=()):
    Tp, Ka = a.shape
    Ng = g.shape[1]
    ka = Ka // blocks if block_a else Ka
    ng = Ng if block_a else Ng // blocks
    tt = _tile(Tp, TILE["tn"])
    nt = Tp // tt

    def body(a_ref, g_ref, o_ref, acc_ref):
        t = pl.program_id(1)

        @pl.when(t == 0)
        def _():
            acc_ref[...] = jnp.zeros_like(acc_ref)

        acc_ref[...] += lax.dot_general(a_ref[...], g_ref[...], _TN, preferred_element_type=F32)

        @pl.when(t == nt - 1)
        def _():
            o_ref[...] = acc_ref[...].astype(o_ref.dtype)

    if block_a:
        a_spec = pl.BlockSpec((tt, ka), lambda j, t: (t, j))
        g_spec = pl.BlockSpec((tt, ng), lambda j, t: (t, 0))
    else:
        a_spec = pl.BlockSpec((tt, ka), lambda j, t: (t, 0))
        g_spec = pl.BlockSpec((tt, ng), lambda j, t: (t, j))
    (dw,), extra = _hosted(
        body, tasks, name=name, grid=(blocks, nt),
        in_specs=[a_spec, g_spec],
        out_specs=[pl.BlockSpec((None, ka, ng), lambda j, t: (j, 0, 0))],
        out_shape=[_sds((blocks, ka, ng), BF16)],
        scratch_shapes=[pltpu.VMEM((ka, ng), F32)],
        semantics=("parallel", "arbitrary"), operands=(a, g))
    return dw, extra


def _nt_norm_bwd(dz, w_slots, dres, hin, rin, g, want_bf16, name, tasks=(), tiles=None, earlier=None):
    Tp, D = hin.shape
    S, _, n = w_slots.shape
    tm = _tile(Tp, TILE["nt"])
    t0, nt = tiles if tiles is not None else (0, Tp // tm)
    assert not (want_bf16 and earlier is not None)

    def body(dz_ref, w_ref, dres_ref, h_ref, r_ref, g_ref, *rest):
        if earlier is not None:
            _, dg0_ref, dh_ref, dg_ref, acc_ref = rest
        elif want_bf16:
            dh_ref, dhb_ref, dg_ref, acc_ref = rest
        else:
            dh_ref, dg_ref, acc_ref = rest
        i, k = pl.program_id(0), pl.program_id(1)

        @pl.when(k == 0)
        def _():
            acc_ref[...] = jnp.zeros_like(acc_ref)

        @pl.when((i == 0) & (k == 0))
        def _():
            dg_ref[...] = jnp.zeros_like(dg_ref) if earlier is None else dg0_ref[...]

        acc_ref[...] += lax.dot_general(dz_ref[...], w_ref[...], _NT, preferred_element_type=F32)

        @pl.when(k == S - 1)
        def _():
            du = acc_ref[...]
            h = h_ref[...]
            r = r_ref[...]
            dg_ref[...] += jnp.sum(du * (h * r), axis=0, keepdims=True)
            dug = du * g_ref[...]
            dh = dres_ref[...] + r * dug - h * (r * r * r * jnp.mean(dug * h, axis=-1, keepdims=True))
            dh_ref[...] = dh
            if want_bf16:
                dhb_ref[...] = dh.astype(BF16)

    row = pl.BlockSpec((tm, D), lambda i, k: (t0 + i, 0))
    vec = pl.BlockSpec((1, D), lambda i, k: (0, 0))
    out_specs = [row] + ([row] if want_bf16 else []) + [vec]
    out_shape = [_sds((Tp, D), F32)] + ([_sds((Tp, D), BF16)] if want_bf16 else []) + [_sds((1, D), F32)]
    in_specs = [pl.BlockSpec((tm, n), lambda i, k: (t0 + i, k)), pl.BlockSpec((None, D, n), lambda i, k: (k, 0, 0)),
                row, row, pl.BlockSpec((tm, 1), lambda i, k: (t0 + i, 0)), vec]
    operands = (dz, w_slots, dres, hin, rin, g)
    aliases = {}
    if earlier is not None:
        in_specs += [pl.BlockSpec(memory_space=pl.ANY), vec]
        operands += tuple(earlier)
        aliases = {6: 0}
    return _hosted(
        body, tasks, name=name, grid=(nt, S), in_specs=in_specs, out_specs=out_specs, out_shape=out_shape,
        scratch_shapes=[pltpu.VMEM((tm, D), F32)], semantics=("arbitrary", "arbitrary"), operands=operands,
        aliases=aliases)


def _dmerged_bwd(dh1b, w_out, name):
    Tp, D = dh1b.shape
    tm = _tile(Tp, TILE["dmerged"])

    def body(g_ref, w_ref, o_ref):
        o_ref[...] = lax.dot_general(g_ref[...], w_ref[...], _NT, preferred_element_type=F32)

    row = pl.BlockSpec((tm, D), lambda i: (i, 0))
    return _call(
        body, name=name, grid=(Tp // tm,),
        in_specs=[row, pl.BlockSpec((D, D), lambda i: (0, 0))],
        out_specs=row, out_shape=_sds((Tp, D), F32),
        compiler_params=_cp("parallel"),
    )(dh1b, w_out)


def _pool_bwd(dmerged, proj, y_pool, d_pool, scale, pool_w, name, tasks=()):
    Tp, D = dmerged.shape
    G, Cg, _ = pool_w.shape
    tm = _tile(Tp, TILE["pool"])
    nt = Tp // tm

    def body(dm_ref, gp_ref, y_ref, d_ref, sc_ref, w_ref, dv_ref, dgp_ref, dw_ref, dsc_ref, next_ref):
        t = pl.program_id(0)
        tile = nt - 1 - t

        @pl.when(t == 0)
        def _():
            next_ref[...] = jnp.zeros_like(next_ref)
            dw_ref[...] = jnp.zeros_like(dw_ref)
            dsc_ref[...] = jnp.zeros_like(dsc_ref)

        rows = _row_ids(tile, tm)
        dm = dm_ref[...]
        y = y_ref[...]
        sc = sc_ref[...]
        sg = _sigmoid(gp_ref[...])
        dpo = dm * sg
        dgp_ref[...] = (dm * (y * sc) * sg * (1.0 - sg)).astype(BF16)
        dsc_ref[...] += jnp.sum(dpo * y, axis=0, keepdims=True)
        dyb = (dpo * sc).astype(BF16)
        for g, win in enumerate(POOL_WINDOWS):
            cols = slice(g * Cg, (g + 1) * Cg)
            dy = dyb[:, cols]
            dd = lax.dot_general(dy, w_ref[g], _NT, preferred_element_type=F32)
            dw_ref[g] += lax.dot_general(d_ref[:, cols], dy, _TN, preferred_element_type=F32)
            q = dd / jnp.minimum(rows + 1, win).astype(F32)
            s = jnp.concatenate([q, next_ref[:, cols]], axis=0)
            k = 1
            while k < win:
                s = s + pltpu.roll(s, s.shape[0] - k, axis=0)
                k *= 2
            dv_ref[:, cols] = (s[:tm] - dd).astype(BF16)
            next_ref[:, cols] = q[:MAX_WINDOW]

    row = pl.BlockSpec((tm, D), lambda t: (nt - 1 - t, 0))
    return _hosted(
        body, tasks, name=name, grid=(nt,),
        in_specs=[row, pl.BlockSpec((tm, D), lambda t: (nt - 1 - t, 3)), row, row,
                  pl.BlockSpec((1, D), lambda t: (0, 0)), pl.BlockSpec((G, Cg, Cg), lambda t: (0, 0, 0))],
        out_specs=[row, row, pl.BlockSpec((G, Cg, Cg), lambda t: (0, 0, 0)), pl.BlockSpec((1, D), lambda t: (0, 0))],
        out_shape=[_sds((Tp, D), BF16), _sds((Tp, D), BF16), _sds((G, Cg, Cg), F32), _sds((1, D), F32)],
        scratch_shapes=[pltpu.VMEM((MAX_WINDOW, D), F32)],
        semantics=("arbitrary",), operands=(dmerged, proj, y_pool, d_pool, scale, pool_w))


LRU_SMALL_ROWS = 8


def _lru_bwd(dmerged, proj, xc, r_gate, i_gate, a_gate, mult_gate, hs, lam, conv_w, wa, wx, name, tasks=()):
    Tp, D = dmerged.shape
    H, hd, _ = wa.shape
    tm = _tile(Tp, TILE["lru"])
    nt = Tp // tm
    nb = D // hd
    halo_blocks = tm // HALO

    def body(dm_ref, vl_ref, vg_ref, gl_ref, xc_ref, r_ref, i_ref, a_ref, mult_ref, hs_ref, hsp_ref, lam_ref, cw_ref,
             wa_ref, wx_ref, dvl_ref, dvg_ref, dgl_ref, dwa_ref, dwx_ref, small_ref,
             mu_next_ref, dxc_next_ref, q_s, mu_s):
        t = pl.program_id(1)
        tile = nt - 1 - t

        @pl.when(t == 0)
        def _():
            mu_next_ref[...] = jnp.zeros_like(mu_next_ref)
            dxc_next_ref[...] = jnp.zeros_like(dxc_next_ref)
            dwa_ref[...] = jnp.zeros_like(dwa_ref)
            dwx_ref[...] = jnp.zeros_like(dwx_ref)
            small_ref[...] = jnp.zeros_like(small_ref)

        first = tile == 0
        dm = dm_ref[...]
        hs_t = hs_ref[...]
        xc_t = xc_ref[...]
        r = r_ref[...]
        i = i_ref[...]
        lam_v = lam_ref[...]
        sp = _softplus(-lam_v)
        a = a_ref[...]
        mult = mult_ref[...]

        sg = _sigmoid(gl_ref[...])
        ge, dge = _gelu_and_grad(vg_ref[...])
        dlo = dm * sg
        dgl_ref[...] = (dm * (hs_t * ge) * sg * (1.0 - sg)).astype(BF16)
        dvg_ref[...] = (dlo * hs_t * dge).astype(BF16)
        dhs = dlo * ge

        q_s[...] = a * dhs
        mu_first = _scan_chunks(a_ref, q_s, mu_s, mu_next_ref[0:1, :], tm, reverse=True)
        lam_t = dhs + _shift_up(mu_s[...], mu_next_ref[...], 1)
        mu_next_ref[...] = jnp.broadcast_to(mu_first, mu_next_ref.shape)

        h_prev = _shift_down(jnp.where(first, 0.0, hsp_ref[...]), hs_t, 1)
        da = lam_t * h_prev
        dmult = lam_t * (i * xc_t)
        di = lam_t * mult * xc_t
        dxc = lam_t * mult * i
        dlog_a = da * a - dmult * (a * a) / mult
        dr = dlog_a * (-LRU_C * sp)
        dlam_rows = dlog_a * (-LRU_C * r)
        dza = dr * r * (1.0 - r)
        dzx = di * i * (1.0 - i)
        dzab, dzxb = dza.astype(BF16), dzx.astype(BF16)
        xcb = xc_t.astype(BF16)
        dxc = dxc + lax.dot_general(dzab, wa_ref[...], _NT, preferred_element_type=F32)
        dxc = dxc + lax.dot_general(dzxb, wx_ref[...], _NT, preferred_element_type=F32)
        dwa_ref[...] += lax.dot_general(xcb, dzab, _TN, preferred_element_type=F32)
        dwx_ref[...] += lax.dot_general(xcb, dzxb, _TN, preferred_element_type=F32)

        dxc_next = dxc_next_ref[...]
        taps = [_shift_up(dxc, dxc_next, CONV_WIDTH - 1 - k) for k in range(CONV_WIDTH)]
        dv = jnp.zeros_like(dxc)
        for k in range(CONV_WIDTH):
            dv = dv + cw_ref[k:k + 1, :] * taps[k]
        dvl_ref[...] = dv.astype(BF16)
        dxc_next_ref[...] = dxc[:HALO, :]

        v_t = vl_ref[...]
        small = [jnp.sum(dza, axis=0, keepdims=True), jnp.sum(dzx, axis=0, keepdims=True),
                 jnp.sum(dlam_rows, axis=0, keepdims=True) * (-_sigmoid(-lam_v)),
                 jnp.sum(dxc, axis=0, keepdims=True)]
        for k in range(CONV_WIDTH):
            small.append(jnp.sum(taps[k] * v_t, axis=0, keepdims=True))
        for k, row in enumerate(small):
            small_ref[k:k + 1, :] += row

    def piece(p):
        return pl.BlockSpec((tm, hd), lambda h, t: (nt - 1 - t, p * nb + h))

    def halo(p):
        return pl.BlockSpec((HALO, hd), lambda h, t: (jnp.maximum((nt - 1 - t) * halo_blocks - 1, 0), p * nb + h))

    blk = pl.BlockSpec((tm, hd), lambda h, t: (nt - 1 - t, h))
    vec = pl.BlockSpec((1, hd), lambda h, t: (0, h))
    mat = pl.BlockSpec((None, hd, hd), lambda h, t: (h, 0, 0))
    return _hosted(
        body, tasks, name=name, grid=(H, nt),
        in_specs=[blk, piece(1), piece(2), piece(4), blk, blk, blk, blk, blk, blk, halo(0), vec,
                  pl.BlockSpec((CONV_WIDTH, hd), lambda h, t: (0, h)), mat, mat],
        out_specs=[blk, blk, blk, mat, mat, pl.BlockSpec((None, LRU_SMALL_ROWS, hd), lambda h, t: (h, 0, 0))],
        out_shape=[_sds((Tp, D), BF16)] * 3 + [_sds((H, hd, hd), F32)] * 2 + [_sds((H, LRU_SMALL_ROWS, hd), F32)],
        scratch_shapes=[pltpu.VMEM((HALO, hd), F32), pltpu.VMEM((HALO, hd), F32),
                        pltpu.VMEM((tm, hd), F32), pltpu.VMEM((tm, hd), F32)],
        semantics=("parallel", "arbitrary"),
        operands=(dmerged, proj, proj, proj, xc, r_gate, i_gate, a_gate, mult_gate, hs, hs, lam, conv_w, wa, wx))


def _adamw(w, g, m, v):
    m = ADAM_B1 * m + (1.0 - ADAM_B1) * g
    v = ADAM_B2 * v + (1.0 - ADAM_B2) * (g * g)
    m_hat = m / (1.0 - ADAM_B1 ** ADAM_STEP)
    v_hat = v / (1.0 - ADAM_B2 ** ADAM_STEP)
    delta = -ADAM_LR * (m_hat / (jnp.sqrt(v_hat) + ADAM_EPS) + ADAM_WD * w)
    return delta, m, v


def _reduce_update(pair_sums, chip_sums, w, m, v, chip_slot, name):
    R, C = w.shape
    tr = _tile(R, TILE["update"])

    def body(slot_ref, own_ref, got_ref, w_ref, m_ref, v_ref, g_out, d_out, m_out, v_out):
        g = own_ref[...].astype(F32)
        for k in range(3):
            g = g + got_ref[k].astype(F32)
        d, m_new, v_new = _adamw(w_ref[...], g, m_ref[...], v_ref[...])
        g_out[...] = g
        d_out[...] = d
        m_out[...] = m_new
        v_out[...] = v_new

    blk = pl.BlockSpec((tr, C), lambda i, s: (i, 0))
    return _call(
        body, name=name,
        grid_spec=pltpu.PrefetchScalarGridSpec(
            num_scalar_prefetch=1, grid=(R // tr,),
            in_specs=[pl.BlockSpec((None, tr, C), lambda i, s: (s[0], i, 0)),
                      pl.BlockSpec((3, tr, C), lambda i, s: (0, i, 0)), blk, blk, blk],
            out_specs=[blk] * 4),
        out_shape=[_sds((R, C), F32)] * 4,
        compiler_params=_cp("parallel"),
    )(chip_slot, pair_sums, chip_sums, w, m, v)


def _small_update(w, g, m, v, name):
    def body(w_ref, g_ref, m_ref, v_ref, d_out, m_out, v_out):
        d, m_new, v_new = _adamw(w_ref[...], g_ref[...], m_ref[...], v_ref[...])
        d_out[...] = d
        m_out[...] = m_new
        v_out[...] = v_new

    return _call(body, name=name, out_shape=[_sds(w.shape, F32)] * 3)(w, g, m, v)


def _slots_from_rows(full, lead):
    L, R, C = full.shape
    r = R // N_DEV
    return full.reshape(L, N_DEV, r, C).transpose(1, 0, 2, 3).reshape(N_DEV, L * r, C)


def _rows_from_slots(slots, lead):
    _, LR, C = slots.shape
    r = LR // lead
    return slots.reshape(N_DEV, lead, r, C).transpose(1, 0, 2, 3).reshape(lead, N_DEV * r, C)


def kernel(x, meta_tokens, norm1_g, w_in, pool_w, pool_scale, conv_w, conv_b, gate_a_w, gate_a_b, gate_x_w, gate_x_b, lru_lambda, w_out, norm2_g, mlp_w1, mlp_w2, final_g, loss_target, m_meta_tokens, m_norm1_g, m_w_in, m_pool_w, m_pool_scale, m_conv_w, m_conv_b, m_gate_a_w, m_gate_a_b, m_gate_x_w, m_gate_x_b, m_lru_lambda, m_w_out, m_norm2_g, m_mlp_w1, m_mlp_w2, m_final_g, v_meta_tokens, v_norm1_g, v_w_in, v_pool_w, v_pool_scale, v_conv_w, v_conv_b, v_gate_a_w, v_gate_a_b, v_gate_x_w, v_gate_x_b, v_lru_lambda, v_w_out, v_norm2_g, v_mlp_w1, v_mlp_w2, v_final_g):
    seq, D = x.shape[1], x.shape[2]
    n_meta = meta_tokens.shape[0]
    G, Cg = pool_w.shape[1], pool_w.shape[3]
    H, hd = gate_a_w.shape[1], gate_a_w.shape[3]
    T = n_meta + seq
    Tp = -(-T // ROW_ALIGN) * ROW_ALIGN
    ix, iy, ic = _pos()
    me = 4 * ix + 2 * iy + ic
    core = jnp.reshape(ic, (1,)).astype(jnp.int32)
    chip_slot = jnp.reshape(2 * ix + iy, (1,)).astype(jnp.int32)

    w_in_l, w1_l, w2_l, w_out_l = w_in[0], mlp_w1[0], mlp_w2[0], w_out[0]
    pool_l = pool_w[0].reshape(G * (Cg // N_DEV), Cg)
    wa_l = gate_a_w[0].reshape(H * (hd // N_DEV), hd)
    wx_l = gate_x_w[0].reshape(H * (hd // N_DEV), hd)
    small_params = jnp.concatenate(
        [meta_tokens, conv_w[0], jnp.zeros((4, D // N_DEV), F32)], axis=0)
    biases = jnp.concatenate([gate_a_b[0], gate_x_b[0]], axis=0)
    (w_in_g, pool_g, wa_g, wx_g, small_g, bias_g) = _all_gather(
        [w_in_l.astype(BF16), pool_l.astype(BF16), wa_l.astype(BF16), wx_l.astype(BF16), small_params, biases],
        "gather_first")
    pool_full = _rows_from_slots(pool_g, G)
    wa_full = _rows_from_slots(wa_g, H)
    wx_full = _rows_from_slots(wx_g, H)
    small_full = small_g.transpose(1, 0, 2).reshape(n_meta + 8, D)
    meta_full = small_full[:n_meta]
    conv_full = small_full[n_meta:n_meta + CONV_WIDTH]
    bias_full = bias_g.transpose(1, 0, 2).reshape(2 * H, hd)
    ba_full, bx_full = bias_full[:H], bias_full[H:]

    h0 = jnp.concatenate([meta_full, x[0], jnp.zeros((Tp - T, D), F32)], axis=0)
    target = jnp.concatenate([jnp.zeros((n_meta, D), F32), loss_target[0], jnp.zeros((Tp - T, D), F32)], axis=0)
    u, r1 = _norm_fwd(h0, norm1_g, "norm1")
    proj, ((w_out_g1,), (w1_g1,)) = _proj_fwd(
        u, w_in_g, "proj", tasks=[_AgStart(w_out_l.astype(BF16)), _AgStart(w1_l.astype(BF16))])
    d_pool, y_pool = _pool_fwd(proj, pool_full, "pool_fwd")
    (xc, r_gate, i_gate, a_gate, mult_gate, hs, merged), ((w_out_g,), (w1_g,), (w2_g1,)) = _lru_fwd(
        proj, y_pool, pool_scale, conv_full, conv_b, wa_full, ba_full, wx_full, bx_full, lru_lambda, "lru_fwd",
        tasks=[_AgForward(w_out_g1), _AgForward(w1_g1), _AgStart(w2_l.astype(BF16))])
    w_out_full = w_out_g.reshape(D, D)
    (h1, u2, r2), ((w2_g,),) = _wout_norm_fwd(merged, w_out_full, h0, norm2_g, "wout_norm2", tasks=[_AgForward(w2_g1)])
    act, a1 = _mlp1_fwd(u2, w1_g, "mlp1")
    dh2, dh2b, loss_tile, d_final_g = _mlp2_loss(
        act, w2_g.reshape(-1, D), h1, target, final_g.reshape(1, D), n_meta, seq, "mlp2_loss")

    def pair(part, got, tag):
        return _pair_sum(part, got, core, "pair_sum_" + tag)

    d_a1 = _dact_bwd(dh2b, w2_g, a1, "dact")
    dw2_p, _ = _weight_grad(act, dh2b, N_DEV, True, "dw2")
    dw1_p, ((dw2_got,),) = _weight_grad(u2, d_a1, N_DEV, False, "dw1", tasks=[_RsSibling(dw2_p)])
    dw2_pair = pair(dw2_p, dw2_got, "w2")
    (dh1, dh1b, d_norm2_g), ((dw2_chips,), (dw1_got,)) = _nt_norm_bwd(
        d_a1, w1_g, dh2, h1, r2, norm2_g, True, "du2_norm2", tasks=[_RsChips(dw2_pair), _RsSibling(dw1_p)])
    dw1_pair = pair(dw1_p, dw1_got, "w1")
    dmerged = _dmerged_bwd(dh1b, w_out_full, "dmerged")
    dwout_p, _ = _weight_grad(merged, dh1b, 2, True, "dwout")
    dwout_p = dwout_p.reshape(N_DEV, D // N_DEV, D)
    (d_vpool, d_gpool, dpool_full, d_scale), ((dwout_got,),) = _pool_bwd(
        dmerged, proj, y_pool, d_pool, pool_scale, pool_full, "pool_bwd", tasks=[_RsSibling(dwout_p)])
    dwout_pair = pair(dwout_p, dwout_got, "wout")
    (d_vlru, d_vgelu, d_glru, dwa_full, dwx_full, lru_small), ((dw1_chips,), (dwout_chips,)) = _lru_bwd(
        dmerged, proj, xc, r_gate, i_gate, a_gate, mult_gate, hs, lru_lambda, conv_full, wa_full, wx_full, "lru_bwd",
        tasks=[_RsChips(dw1_pair), _RsChips(dwout_pair)])
    dproj = jnp.concatenate([d_vpool, d_vlru, d_vgelu, d_gpool, d_glru], axis=1)
    dwin_p, _ = _weight_grad(u, dproj, N_DEV, False, "dwin")
    dpool_p = _slots_from_rows(dpool_full, G).astype(BF16)
    dwa_p = _slots_from_rows(dwa_full, H).astype(BF16)
    dwx_p = _slots_from_rows(dwx_full, H).astype(BF16)
    late = [dwin_p, dpool_p, dwa_p, dwx_p]
    n_tiles = Tp // _tile(Tp, TILE["nt"])
    n_early = max(1, n_tiles // 5)
    (dh0_a, d_norm1_a), late_got = _nt_norm_bwd(
        dproj, w_in_g, dh1, h0, r1, norm1_g, False, "du_norm1_a", tasks=[_RsSibling(p) for p in late],
        tiles=(0, n_early))
    late_pair = [pair(p, g[0], "late%d" % k) for k, (p, g) in enumerate(zip(late, late_got))]
    (dh0, d_norm1_g), late_chips = _nt_norm_bwd(
        dproj, w_in_g, dh1, h0, r1, norm1_g, False, "du_norm1_b", tasks=[_RsChips(p) for p in late_pair],
        tiles=(n_early, n_tiles - n_early), earlier=(dh0_a, d_norm1_a))
    grad_x = dh0[n_meta:T][None]

    pair_sums = [dw2_pair, dw1_pair, dwout_pair] + late_pair
    chip_sums = [dw2_chips, dw1_chips, dwout_chips] + [c[0] for c in late_chips]
    big = {}
    names = ["mlp_w2", "mlp_w1", "w_out", "w_in", "pool_w", "gate_a_w", "gate_x_w"]
    trip = {"mlp_w2": (mlp_w2, m_mlp_w2, v_mlp_w2), "mlp_w1": (mlp_w1, m_mlp_w1, v_mlp_w1),
            "w_out": (w_out, m_w_out, v_w_out), "w_in": (w_in, m_w_in, v_w_in),
            "pool_w": (pool_w, m_pool_w, v_pool_w), "gate_a_w": (gate_a_w, m_gate_a_w, v_gate_a_w),
            "gate_x_w": (gate_x_w, m_gate_x_w, v_gate_x_w)}
    for k, nm in enumerate(names):
        w_, m_, v_ = trip[nm]
        shape2 = pair_sums[k].shape[1:]
        outs = _reduce_update(pair_sums[k], chip_sums[k], w_.reshape(shape2), m_.reshape(shape2), v_.reshape(shape2),
                              chip_slot, "update_" + nm)
        big[nm] = [o.reshape(w_.shape) for o in outs]

    lru_rows = lru_small.transpose(1, 0, 2).reshape(LRU_SMALL_ROWS, D)
    small_part = jnp.concatenate(
        [dh0[:n_meta], d_norm1_g, d_scale, d_norm2_g, d_final_g, lru_rows, jnp.zeros((4, D), F32)], axis=0)
    (small_all,) = _all_gather([small_part], "gather_small_grads")
    small_sum = _sum_slots(small_all, "sum_small_grads")
    o = n_meta
    g_meta_full = small_sum[:o]
    g_norm1, g_scale, g_norm2, g_final = (small_sum[o + k:o + k + 1] for k in range(4))
    g_ba_full, g_bx_full, g_lam, g_cb = (small_sum[o + 4 + k:o + 5 + k] for k in range(4))
    g_cw_full = small_sum[o + 8:o + 8 + CONV_WIDTH]
    dcol = D // N_DEV
    g_meta = lax.dynamic_slice_in_dim(g_meta_full, me * dcol, dcol, axis=1)
    g_cw = lax.dynamic_slice_in_dim(g_cw_full, me * dcol, dcol, axis=1)
    hcol = hd // N_DEV
    g_ba = lax.dynamic_slice_in_dim(g_ba_full.reshape(H, hd), me * hcol, hcol, axis=1)
    g_bx = lax.dynamic_slice_in_dim(g_bx_full.reshape(H, hd), me * hcol, hcol, axis=1)

    rep_w = jnp.concatenate([norm1_g, pool_scale, conv_b, lru_lambda, norm2_g, final_g.reshape(1, D)], axis=0)
    rep_g = jnp.concatenate([g_norm1, g_scale, g_cb, g_lam, g_norm2, g_final], axis=0)
    rep_m = jnp.concatenate([m_norm1_g, m_pool_scale, m_conv_b, m_lru_lambda, m_norm2_g, m_final_g.reshape(1, D)], axis=0)
    rep_v = jnp.concatenate([v_norm1_g, v_pool_scale, v_conv_b, v_lru_lambda, v_norm2_g, v_final_g.reshape(1, D)], axis=0)
    rep_d, rep_nm, rep_nv = _small_update(rep_w, rep_g, rep_m, rep_v, "update_vectors")
    col_w = jnp.concatenate([meta_tokens, conv_w[0]], axis=0)
    col_g = jnp.concatenate([g_meta, g_cw], axis=0)
    col_m = jnp.concatenate([m_meta_tokens, m_conv_w[0]], axis=0)
    col_v = jnp.concatenate([v_meta_tokens, v_conv_w[0]], axis=0)
    col_d, col_nm, col_nv = _small_update(col_w, col_g, col_m, col_v, "update_columns")
    b_w = jnp.concatenate([gate_a_b[0], gate_x_b[0]], axis=0)
    b_g = jnp.concatenate([g_ba, g_bx], axis=0)
    b_m = jnp.concatenate([m_gate_a_b[0], m_gate_x_b[0]], axis=0)
    b_v = jnp.concatenate([v_gate_a_b[0], v_gate_x_b[0]], axis=0)
    b_d, b_nm, b_nv = _small_update(b_w, b_g, b_m, b_v, "update_biases")

    def rep(arr, k, like):
        return arr[k:k + 1].reshape(like.shape)

    rep_order = {"norm1_g": 0, "pool_scale": 1, "conv_b": 2, "lru_lambda": 3, "norm2_g": 4, "final_g": 5}
    like = {"norm1_g": norm1_g, "pool_scale": pool_scale, "conv_b": conv_b, "lru_lambda": lru_lambda,
            "norm2_g": norm2_g, "final_g": final_g}

    def leaves(kind):
        rep_src = [rep_g, rep_d, rep_nm, rep_nv][kind]
        col_src = [col_g, col_d, col_nm, col_nv][kind]
        b_src = [b_g, b_d, b_nm, b_nv][kind]
        out = {}
        out["meta_tokens"] = col_src[:n_meta]
        out["conv_w"] = col_src[n_meta:][None]
        out["gate_a_b"] = b_src[:H][None]
        out["gate_x_b"] = b_src[H:][None]
        for nm, k in rep_order.items():
            out[nm] = rep(rep_src, k, like[nm])
        for nm in names:
            out[nm] = big[nm][kind]
        order = ["meta_tokens", "norm1_g", "w_in", "pool_w", "pool_scale", "conv_w", "conv_b", "gate_a_w", "gate_a_b",
                 "gate_x_w", "gate_x_b", "lru_lambda", "w_out", "norm2_g", "mlp_w1", "mlp_w2", "final_g"]
        return [out[nm] for nm in order]

    loss = lax.psum(loss_tile[0, 0], ("x", "y", "c"))
    return (loss, grad_x, *leaves(0), *leaves(1), *leaves(2), *leaves(3))
```

```python
import functools

import jax
import jax.numpy as jnp
from jax import lax
from jax.experimental import pallas as pl
from jax.experimental.pallas import tpu as pltpu

F32 = jnp.float32
BF16 = jnp.bfloat16
MESH = pl.DeviceIdType.MESH
N_DEV = 8
POOL_WINDOWS = (2, 4, 8, 16)
MAX_WINDOW = 16
CONV_WIDTH = 4
HALO = 8
LRU_C = 8.0
NORM_EPS = 1e-6
ADAM_LR, ADAM_B1, ADAM_B2, ADAM_EPS, ADAM_WD, ADAM_STEP = 0.001, 0.9, 0.999, 1e-08, 0.01, 10
ROW_ALIGN = 128
VMEM_LIMIT = 56 << 20
TILE = dict(norm=384, proj=704, pool=384, lru=384, wout=384, mlp1=704, mlp2=704, dact=704, tn=1408,
            nt=704, dmerged=384, update=256)
MLP2_K = 1024
EPILOGUE_ROWS = 176

_NT = (((1,), (1,)), ((), ()))
_TN = (((0,), (0,)), ((), ()))


def _call(body, **kw):
    return pl.pallas_call(body, **kw)


def _cp(*sem):
    return pltpu.CompilerParams(dimension_semantics=sem, vmem_limit_bytes=VMEM_LIMIT)


def _tile(total, pref):
    best = None
    for t in range(16, min(total, pref) + 1, 16):
        if total % t == 0:
            best = t
    assert best is not None, (total, pref)
    return best


def _sds(shape, dtype):
    return jax.ShapeDtypeStruct(shape, dtype)


def _pos():
    return lax.axis_index("x"), lax.axis_index("y"), lax.axis_index("c")


def _all_gather(shards, name):
    n = len(shards)

    def body(*refs):
        ins, outs = refs[:n], refs[n:2 * n]
        send_sems, recv_sems, local_sems = refs[2 * n:]
        x, y, c = _pos()
        me, sib = (x, y, c), (x, y, 1 - c)
        chips = [(1 - x, y), (x, 1 - y), (1 - x, 1 - y)]

        def slot(p):
            return 4 * p[0] + 2 * p[1] + p[2]

        def copy(a, k, block, to, src=None):
            dst = outs[a].at[slot(block)]
            return pltpu.make_async_remote_copy(
                src_ref=dst if src is None else src, dst_ref=dst,
                send_sem=send_sems.at[7 * a + k], recv_sem=recv_sems.at[7 * a + k],
                device_id=to, device_id_type=MESH)

        mine = [pltpu.make_async_copy(ins[a], outs[a].at[slot(me)], local_sems.at[a]) for a in range(n)]
        for m in mine:
            m.start()
        first = []
        for a in range(n):
            first.append(copy(a, 0, me, sib, src=ins[a]))
            first += [copy(a, 1 + j, me, (*chip, c), src=ins[a]) for j, chip in enumerate(chips)]
        for cp in first:
            cp.start()
        passed = []
        for a in range(n):
            for j, chip in enumerate(chips):
                copy(a, 1 + j, (*chip, c), me).wait_recv()
                fwd = copy(a, 4 + j, (*chip, c), sib)
                fwd.start()
                passed.append(fwd)
        for a in range(n):
            copy(a, 0, sib, me).wait_recv()
            for j, chip in enumerate(chips):
                copy(a, 4 + j, (*chip, 1 - c), me).wait_recv()
        for cp in first + passed:
            cp.wait_send()
        for m in mine:
            m.wait()

    hbm = pl.BlockSpec(memory_space=pl.ANY)
    return _call(
        body, name=name,
        out_shape=[_sds((N_DEV,) + s.shape, s.dtype) for s in shards],
        in_specs=[hbm] * n, out_specs=[hbm] * n,
        scratch_shapes=[pltpu.SemaphoreType.DMA((7 * n,)), pltpu.SemaphoreType.DMA((7 * n,)),
                        pltpu.SemaphoreType.DMA((n,))],
    )(*shards)


def _other_chips(x, y):
    return [(1 - x, y), (x, 1 - y), (1 - x, 1 - y)]


class _AgStart:
    n_sem, n_local = 4, 1

    def __init__(self, shard):
        self.ins = [shard]
        self.out_shapes = [_sds((N_DEV,) + shard.shape, shard.dtype)]
        self.aliases = []

    def _peers(self):
        x, y, c = _pos()
        return [(x, y, 1 - c)] + [(*chip, c) for chip in _other_chips(x, y)]

    def _sends(self, ins, outs, sems):
        send, recv, _, base, _ = sems
        x, y, c = _pos()
        mine = outs[0].at[4 * x + 2 * y + c]
        return [pltpu.make_async_remote_copy(src_ref=ins[0], dst_ref=mine, send_sem=send.at[base + k],
                                             recv_sem=recv.at[base + k], device_id=p, device_id_type=MESH)
                for k, p in enumerate(self._peers())]

    def _arrivals(self, outs, sems):
        send, recv, _, base, _ = sems
        res = []
        for k, p in enumerate(self._peers()):
            blk = outs[0].at[4 * p[0] + 2 * p[1] + p[2]]
            res.append(pltpu.make_async_remote_copy(src_ref=blk, dst_ref=blk, send_sem=send.at[base + k],
                                                    recv_sem=recv.at[base + k], device_id=p, device_id_type=MESH))
        return res

    def _own(self, ins, outs, sems):
        x, y, c = _pos()
        return pltpu.make_async_copy(ins[0], outs[0].at[4 * x + 2 * y + c], sems[2].at[sems[4]])

    def start(self, ins, outs, sems):
        self._own(ins, outs, sems).start()
        for cp in self._sends(ins, outs, sems):
            cp.start()

    def finish(self, ins, outs, sems):
        for cp in self._arrivals(outs, sems):
            cp.wait_recv()
        for cp in self._sends(ins, outs, sems):
            cp.wait_send()
        self._own(ins, outs, sems).wait()


class _AgForward:
    n_sem, n_local = 3, 0

    def __init__(self, gathered):
        self.ins = [gathered]
        self.out_shapes = [_sds(gathered.shape, gathered.dtype)]
        self.aliases = [(0, 0)]

    def _copies(self, outs, sems, core_of_block):
        send, recv, _, base, _ = sems
        x, y, c = _pos()
        res = []
        for k, chip in enumerate(_other_chips(x, y)):
            blk = outs[0].at[4 * chip[0] + 2 * chip[1] + (c if core_of_block == "mine" else 1 - c)]
            res.append(pltpu.make_async_remote_copy(src_ref=blk, dst_ref=blk, send_sem=send.at[base + k],
                                                    recv_sem=recv.at[base + k], device_id=(x, y, 1 - c),
                                                    device_id_type=MESH))
        return res

    def start(self, ins, outs, sems):
        for cp in self._copies(outs, sems, "mine"):
            cp.start()

    def finish(self, ins, outs, sems):
        for cp in self._copies(outs, sems, "sibling"):
            cp.wait_recv()
        for cp in self._copies(outs, sems, "mine"):
            cp.wait_send()


class _RsSibling:
    n_sem, n_local = 4, 0

    def __init__(self, part):
        self.ins = [part]
        self.out_shapes = [_sds((4,) + part.shape[1:], part.dtype)]
        self.aliases = []

    def _copies(self, ins, outs, sems):
        send, recv, _, base, _ = sems
        x, y, c = _pos()
        return [pltpu.make_async_remote_copy(src_ref=ins[0].at[2 * q + (1 - c)], dst_ref=outs[0].at[q],
                                             send_sem=send.at[base + q], recv_sem=recv.at[base + q],
                                             device_id=(x, y, 1 - c), device_id_type=MESH) for q in range(4)]

    def start(self, ins, outs, sems):
        for cp in self._copies(ins, outs, sems):
            cp.start()

    def finish(self, ins, outs, sems):
        for cp in self._copies(ins, outs, sems):
            cp.wait()


class _RsChips:
    n_sem, n_local = 3, 0

    def __init__(self, pair):
        self.ins = [pair]
        self.out_shapes = [_sds((3,) + pair.shape[1:], pair.dtype)]
        self.aliases = []

    def _copies(self, ins, outs, sems):
        send, recv, _, base, _ = sems
        x, y, c = _pos()
        return [pltpu.make_async_remote_copy(src_ref=ins[0].at[2 * chip[0] + chip[1]], dst_ref=outs[0].at[k],
                                             send_sem=send.at[base + k], recv_sem=recv.at[base + k],
                                             device_id=(*chip, c), device_id_type=MESH)
                for k, chip in enumerate(_other_chips(x, y))]

    def start(self, ins, outs, sems):
        for cp in self._copies(ins, outs, sems):
            cp.start()

    def finish(self, ins, outs, sems):
        for cp in self._copies(ins, outs, sems):
            cp.wait()


def _hosted(body, tasks, *, grid, in_specs, out_specs, out_shape, scratch_shapes=(), name, semantics, operands,
            aliases=None):
    in_specs, out_specs, out_shape = list(in_specs), list(out_specs), list(out_shape)
    scratch_shapes = list(scratch_shapes)
    aliases = dict(aliases or {})
    if not tasks:
        res = _call(body, name=name, grid=grid, in_specs=in_specs, out_specs=out_specs, out_shape=out_shape,
                    scratch_shapes=scratch_shapes, input_output_aliases=aliases,
                    compiler_params=_cp(*semantics))(*operands)
        return list(res), []
    n_in, n_out, n_scr = len(in_specs), len(out_specs), len(scratch_shapes)
    t_ins = [a for t in tasks for a in t.ins]
    t_outs = [o for t in tasks for o in t.out_shapes]
    i0, o0 = n_in, n_out
    for t in tasks:
        for (i, o) in t.aliases:
            aliases[i0 + i] = o0 + o
        i0 += len(t.ins)
        o0 += len(t.out_shapes)
    n_sem = sum(t.n_sem for t in tasks)
    n_local = max(1, sum(t.n_local for t in tasks))

    def wrapped(*refs):
        cut = [n_in, len(t_ins), n_out, len(t_outs), n_scr]
        parts, at = [], 0
        for n in cut:
            parts.append(refs[at:at + n])
            at += n
        ins, tin, outs, tout, scratch = parts
        send, recv, local = refs[at:]
        first = functools.reduce(lambda a, b: a & b, [pl.program_id(d) == 0 for d in range(len(grid))])
        last = functools.reduce(lambda a, b: a & b, [pl.program_id(d) == grid[d] - 1 for d in range(len(grid))])

        def each(method):
            i, o, s, l = 0, 0, 0, 0
            for t in tasks:
                getattr(t, method)(tin[i:i + len(t.ins)], tout[o:o + len(t.out_shapes)], (send, recv, local, s, l))
                i, o, s, l = i + len(t.ins), o + len(t.out_shapes), s + t.n_sem, l + t.n_local

        @pl.when(first)
        def _():
            each("start")

        body(*ins, *outs, *scratch)

        @pl.when(last)
        def _():
            each("finish")

    hbm = pl.BlockSpec(memory_space=pl.ANY)
    res = _call(
        wrapped, name=name, grid=grid,
        in_specs=in_specs + [hbm] * len(t_ins), out_specs=out_specs + [hbm] * len(t_outs),
        out_shape=out_shape + t_outs,
        scratch_shapes=scratch_shapes + [pltpu.SemaphoreType.DMA((n_sem,)), pltpu.SemaphoreType.DMA((n_sem,)),
                                         pltpu.SemaphoreType.DMA((n_local,))],
        input_output_aliases=aliases,
        compiler_params=_cp(*(["arbitrary"] * len(grid))),
    )(*operands, *t_ins)
    res = list(res)
    task_outs, o = [], n_out
    for t in tasks:
        task_outs.append(res[o:o + len(t.out_shapes)])
        o += len(t.out_shapes)
    return res[:n_out], task_outs


def _pair_sum(part, got, core, name):
    _, R, C = part.shape
    tr = _tile(R, TILE["update"]) if R % 16 == 0 else R

    def body(core_ref, p_ref, g_ref, o_ref):
        o_ref[...] = (p_ref[...].astype(F32) + g_ref[...].astype(F32)).astype(o_ref.dtype)

    return _call(
        body, name=name,
        grid_spec=pltpu.PrefetchScalarGridSpec(
            num_scalar_prefetch=1, grid=(4, R // tr),
            in_specs=[pl.BlockSpec((None, tr, C), lambda q, i, cr: (2 * q + cr[0], i, 0)),
                      pl.BlockSpec((None, tr, C), lambda q, i, cr: (q, i, 0))],
            out_specs=pl.BlockSpec((None, tr, C), lambda q, i, cr: (q, i, 0))),
        out_shape=_sds((4, R, C), part.dtype),
        compiler_params=_cp("parallel", "parallel"),
    )(core, part, got)


def _sum_slots(gathered, name):
    _, R, C = gathered.shape

    def body(g_ref, o_ref):
        acc = g_ref[0]
        for s in range(1, N_DEV):
            acc = acc + g_ref[s]
        o_ref[...] = acc

    return _call(body, name=name, out_shape=_sds((R, C), F32))(gathered)


def _sigmoid(z):
    return jax.nn.sigmoid(z)


def _softplus(z):
    e = jnp.exp(-jnp.abs(z))
    log1p_e = jnp.where(e < 0.01, e * (1.0 - e * (0.5 - e * (1.0 / 3.0))), jnp.log(1.0 + e))
    return jnp.maximum(z, 0.0) + log1p_e


_GELU_K = 0.7978845608028654
_GELU_C = 0.044715


def _gelu_and_grad(z):
    t = jnp.tanh(_GELU_K * (z + _GELU_C * z * z * z))
    g = 0.5 * z * (1.0 + t)
    dg = 0.5 * (1.0 + t) + 0.5 * z * (1.0 - t * t) * _GELU_K * (1.0 + 3.0 * _GELU_C * z * z)
    return g, dg


def _gelu(z):
    t = jnp.tanh(_GELU_K * (z + _GELU_C * z * z * z))
    return 0.5 * z * (1.0 + t)


def _row_ids(tile_index, tm, width=1):
    return tile_index * tm + lax.broadcasted_iota(jnp.int32, (tm, width), 0)


def _shift_down(prev, cur, k):
    if k == 0:
        return cur
    ext = jnp.concatenate([prev, cur], axis=0)
    return pltpu.roll(ext, k, axis=0)[prev.shape[0]:]


def _shift_up(cur, nxt, k):
    if k == 0:
        return cur
    ext = jnp.concatenate([cur, nxt], axis=0)
    return pltpu.roll(ext, ext.shape[0] - k, axis=0)[:cur.shape[0]]


def _lru_gates(r, sp):
    z = LRU_C * r * sp
    a = jnp.exp(-z)
    t = jnp.tanh(z)
    mult = jnp.sqrt(2.0 * t / (1.0 + t))
    return a, mult


def _scan_chunks(a_ref, b_ref, out_ref, carry, n_rows, reverse):
    n_chunks = n_rows // 8
    cols = a_ref.shape[1]
    rid = lax.broadcasted_iota(jnp.int32, (8, cols), 0)
    edge = 0 if reverse else 7

    def chunk(k, h):
        ci = (n_chunks - 1 - k) if reverse else k
        rows = pl.ds(pl.multiple_of(ci * 8, 8), 8)
        a = a_ref[rows, :]
        b = b_ref[rows, :]
        for s in (1, 2, 4):
            if reverse:
                keep = rid < 8 - s
                a_n, b_n = pltpu.roll(a, 8 - s, axis=0), pltpu.roll(b, 8 - s, axis=0)
            else:
                keep = rid >= s
                a_n, b_n = pltpu.roll(a, s, axis=0), pltpu.roll(b, s, axis=0)
            b = a * jnp.where(keep, b_n, 0.0) + b
            a = a * jnp.where(keep, a_n, 1.0)
        out_ref[rows, :] = a * h + b
        a_e = jnp.sum(jnp.where(rid == edge, a, 0.0), axis=0, keepdims=True)
        b_e = jnp.sum(jnp.where(rid == edge, b, 0.0), axis=0, keepdims=True)
        return a_e * h + b_e

    return lax.fori_loop(0, n_chunks, chunk, carry, unroll=4 if n_chunks % 4 == 0 else 1)


def _norm_fwd(h, g, name):
    Tp, D = h.shape
    tm = _tile(Tp, TILE["norm"])

    def body(h_ref, g_ref, u_ref, r_ref):
        x = h_ref[...]
        r = lax.rsqrt(jnp.mean(x * x, axis=-1, keepdims=True) + NORM_EPS)
        u_ref[...] = (x * r * g_ref[...]).astype(BF16)
        r_ref[...] = r

    return _call(
        body, name=name, grid=(Tp // tm,),
        in_specs=[pl.BlockSpec((tm, D), lambda i: (i, 0)), pl.BlockSpec((1, D), lambda i: (0, 0))],
        out_specs=[pl.BlockSpec((tm, D), lambda i: (i, 0)), pl.BlockSpec((tm, 1), lambda i: (i, 0))],
        out_shape=[_sds((Tp, D), BF16), _sds((Tp, 1), F32)],
        compiler_params=_cp("parallel"),
    )(h, g)


def _proj_fwd(u, w_slots, name, tasks=()):
    Tp, K = u.shape
    S, _, n = w_slots.shape
    tm = _tile(Tp, TILE["proj"])

    def body(a_ref, b_ref, o_ref):
        o_ref[...] = jnp.dot(a_ref[...], b_ref[...], preferred_element_type=F32)

    (proj,), extra = _hosted(
        body, tasks, name=name, grid=(S, Tp // tm),
        in_specs=[pl.BlockSpec((tm, K), lambda j, i: (i, 0)), pl.BlockSpec((None, K, n), lambda j, i: (j, 0, 0))],
        out_specs=[pl.BlockSpec((tm, n), lambda j, i: (i, j))],
        out_shape=[_sds((Tp, S * n), F32)],
        semantics=("parallel", "parallel"), operands=(u, w_slots))
    return proj, extra


def _mlp1_fwd(u2, w_slots, name):
    Tp, K = u2.shape
    S, _, n = w_slots.shape
    tm = _tile(Tp, TILE["mlp1"])

    def body(a_ref, b_ref, act_ref, a1_ref):
        a1 = jnp.dot(a_ref[...], b_ref[...], preferred_element_type=F32)
        relu = jnp.maximum(a1, 0.0)
        act_ref[...] = (relu * relu).astype(BF16)
        a1_ref[...] = a1.astype(BF16)

    return _call(
        body, name=name, grid=(S, Tp // tm),
        in_specs=[pl.BlockSpec((tm, K), lambda j, i: (i, 0)), pl.BlockSpec((None, K, n), lambda j, i: (j, 0, 0))],
        out_specs=[pl.BlockSpec((tm, n), lambda j, i: (i, j))] * 2,
        out_shape=[_sds((Tp, S * n), BF16)] * 2,
        compiler_params=_cp("parallel", "parallel"),
    )(u2, w_slots)


def _pool_fwd(proj, pool_w, name):
    Tp = proj.shape[0]
    G, Cg, _ = pool_w.shape
    D = G * Cg
    tm = _tile(Tp, TILE["pool"])

    def body(v_ref, w_ref, d_ref, y_ref, prev_ref):
        t = pl.program_id(0)

        @pl.when(t == 0)
        def _():
            prev_ref[...] = jnp.zeros_like(prev_ref)

        rows = _row_ids(t, tm)
        for g, win in enumerate(POOL_WINDOWS):
            cols = slice(g * Cg, (g + 1) * Cg)
            v = v_ref[:, cols]
            s = jnp.concatenate([prev_ref[:, cols], v], axis=0)
            k = 1
            while k < win:
                s = s + pltpu.roll(s, k, axis=0)
                k *= 2
            cnt = jnp.minimum(rows + 1, win).astype(F32)
            d = s[MAX_WINDOW:] / cnt - v
            d_ref[:, cols] = d.astype(BF16)
            y_ref[:, cols] = jnp.dot(d.astype(BF16), w_ref[g], preferred_element_type=F32)
        prev_ref[...] = v_ref[tm - MAX_WINDOW:, :]

    return _call(
        body, name=name, grid=(Tp // tm,),
        in_specs=[pl.BlockSpec((tm, D), lambda t: (t, 0)), pl.BlockSpec((G, Cg, Cg), lambda t: (0, 0, 0))],
        out_specs=[pl.BlockSpec((tm, D), lambda t: (t, 0))] * 2,
        out_shape=[_sds((Tp, D), BF16), _sds((Tp, D), F32)],
        scratch_shapes=[pltpu.VMEM((MAX_WINDOW, D), F32)],
        compiler_params=_cp("arbitrary"),
    )(proj, pool_w)


def _lru_fwd(proj, y_pool, scale, conv_w, conv_b, wa, ba, wx, bx, lam, name, tasks=()):
    Tp = proj.shape[0]
    H, hd, _ = wa.shape
    D = H * hd
    tm = _tile(Tp, TILE["lru"])
    nb = D // hd

    def body(vl_ref, vg_ref, gp_ref, gl_ref, y_ref, sc_ref, cw_ref, cb_ref, wa_ref, ba_ref, wx_ref, bx_ref,
             lam_ref, xc_ref, r_ref, i_ref, a_ref, mult_ref, hs_ref, m_ref, prev_ref, carry_ref, b_s):
        t = pl.program_id(1)

        @pl.when(t == 0)
        def _():
            prev_ref[...] = jnp.zeros_like(prev_ref)
            carry_ref[...] = jnp.zeros_like(carry_ref)

        v = vl_ref[...]
        prev = prev_ref[...]
        xc = jnp.zeros_like(v) + cb_ref[...]
        for k in range(CONV_WIDTH):
            xc = xc + cw_ref[k:k + 1, :] * _shift_down(prev, v, CONV_WIDTH - 1 - k)
        prev_ref[...] = v[tm - HALO:, :]
        xcb = xc.astype(BF16)
        r = _sigmoid(jnp.dot(xcb, wa_ref[...], preferred_element_type=F32) + ba_ref[...])
        i = _sigmoid(jnp.dot(xcb, wx_ref[...], preferred_element_type=F32) + bx_ref[...])
        a, mult = _lru_gates(r, _softplus(-lam_ref[...]))
        a_ref[...] = a
        mult_ref[...] = mult
        b_s[...] = mult * (i * xc)
        xc_ref[...] = xc
        r_ref[...] = r
        i_ref[...] = i
        carry_ref[0:1, :] = _scan_chunks(a_ref, b_s, hs_ref, carry_ref[0:1, :], tm, reverse=False)
        lru_out = hs_ref[...] * _gelu(vg_ref[...])
        pool_out = y_ref[...] * sc_ref[...]
        m_ref[...] = (_sigmoid(gp_ref[...]) * pool_out + _sigmoid(gl_ref[...]) * lru_out).astype(BF16)

    def piece(p):
        return pl.BlockSpec((tm, hd), lambda h, t: (t, p * nb + h))

    blk = pl.BlockSpec((tm, hd), lambda h, t: (t, h))
    vec = pl.BlockSpec((1, hd), lambda h, t: (0, h))
    mat = pl.BlockSpec((None, hd, hd), lambda h, t: (h, 0, 0))
    bias = pl.BlockSpec((None, 1, hd), lambda h, t: (h, 0, 0))
    return _hosted(
        body, tasks, name=name, grid=(H, Tp // tm),
        in_specs=[piece(1), piece(2), piece(3), piece(4), blk, vec,
                  pl.BlockSpec((CONV_WIDTH, hd), lambda h, t: (0, h)), vec, mat, bias, mat, bias, vec],
        out_specs=[blk] * 7,
        out_shape=[_sds((Tp, D), F32)] * 6 + [_sds((Tp, D), BF16)],
        scratch_shapes=[pltpu.VMEM((HALO, hd), F32), pltpu.VMEM((8, hd), F32), pltpu.VMEM((tm, hd), F32)],
        semantics=("parallel", "arbitrary"),
        operands=(proj, proj, proj, proj, y_pool, scale, conv_w, conv_b, wa, ba.reshape(H, 1, hd), wx,
                  bx.reshape(H, 1, hd), lam))


def _wout_norm_fwd(merged, w_out, h0, g2, name, tasks=()):
    Tp, D = h0.shape
    tm = _tile(Tp, TILE["wout"])

    def body(m_ref, w_ref, h0_ref, g_ref, h1_ref, u2_ref, r2_ref):
        h1 = h0_ref[...] + jnp.dot(m_ref[...], w_ref[...], preferred_element_type=F32)
        r = lax.rsqrt(jnp.mean(h1 * h1, axis=-1, keepdims=True) + NORM_EPS)
        h1_ref[...] = h1
        u2_ref[...] = (h1 * r * g_ref[...]).astype(BF16)
        r2_ref[...] = r

    row = pl.BlockSpec((tm, D), lambda i: (i, 0))
    return _hosted(
        body, tasks, name=name, grid=(Tp // tm,),
        in_specs=[row, pl.BlockSpec((D, D), lambda i: (0, 0)), row, pl.BlockSpec((1, D), lambda i: (0, 0))],
        out_specs=[row, row, pl.BlockSpec((tm, 1), lambda i: (i, 0))],
        out_shape=[_sds((Tp, D), F32), _sds((Tp, D), BF16), _sds((Tp, 1), F32)],
        semantics=("parallel",), operands=(merged, w_out, h0, g2))


def _mlp2_loss(act, w2, h1, target, gf, n_meta, seq, name):
    Tp, D = h1.shape
    K = act.shape[1]
    tm = _tile(Tp, TILE["mlp2"])
    tk = min(K, MLP2_K)
    nk = K // tk

    rc = _tile(tm, EPILOGUE_ROWS)

    def body(a_ref, w_ref, h1_hbm, t_hbm, g_ref, dh_ref, dhb_ref, loss_ref, dg_ref, h1_buf, t_buf, sems):
        i, k = pl.program_id(0), pl.program_id(1)
        tile_rows = pl.ds(pl.multiple_of(i * tm, tm), tm)
        fetch = [pltpu.make_async_copy(h1_hbm.at[tile_rows, :], h1_buf, sems.at[0]),
                 pltpu.make_async_copy(t_hbm.at[tile_rows, :], t_buf, sems.at[1])]

        @pl.when(k == 0)
        def _():
            for f in fetch:
                f.start()
            dh_ref[...] = jnp.zeros_like(dh_ref)

        @pl.when((i == 0) & (k == 0))
        def _():
            loss_ref[...] = jnp.zeros_like(loss_ref)
            dg_ref[...] = jnp.zeros_like(dg_ref)

        dh_ref[...] += jnp.dot(a_ref[...], w_ref[...], preferred_element_type=F32)

        @pl.when(k == nk - 1)
        def _():
            for f in fetch:
                f.wait()
            g = g_ref[...]

            def chunk(c, carry):
                loss_acc, dg_acc = carry
                rows = pl.ds(pl.multiple_of(c * rc, rc), rc)
                h2 = h1_buf[rows, :] + dh_ref[rows, :]
                r = lax.rsqrt(jnp.mean(h2 * h2, axis=-1, keepdims=True) + NORM_EPS)
                out = h2 * r * g
                row_id = i * tm + c * rc + lax.broadcasted_iota(jnp.int32, (rc, 1), 0)
                valid = (row_id >= n_meta) & (row_id < n_meta + seq)
                diff = jnp.where(valid, out - t_buf[rows, :], 0.0)
                dout = diff / D
                dog = dout * g
                dh = r * dog - h2 * (r * r * r * jnp.mean(dog * h2, axis=-1, keepdims=True))
                dh_ref[rows, :] = dh
                dhb_ref[rows, :] = dh.astype(BF16)
                loss_acc = loss_acc + 0.5 * jnp.sum(jnp.mean(diff * diff, axis=-1, keepdims=True), axis=0, keepdims=True)
                return loss_acc, dg_acc + jnp.sum(dout * (h2 * r), axis=0, keepdims=True)

            loss_sum, dg_sum = lax.fori_loop(0, tm // rc, chunk, (jnp.zeros((1, 1), F32), jnp.zeros((1, D), F32)))
            loss_ref[...] += loss_sum
            dg_ref[...] += dg_sum

    row = pl.BlockSpec((tm, D), lambda i, k: (i, 0))
    hbm = pl.BlockSpec(memory_space=pl.ANY)
    return _call(
        body, name=name, grid=(Tp // tm, nk),
        in_specs=[pl.BlockSpec((tm, tk), lambda i, k: (i, k)), pl.BlockSpec((tk, D), lambda i, k: (k, 0)),
                  hbm, hbm, pl.BlockSpec((1, D), lambda i, k: (0, 0))],
        out_specs=[row, row, pl.BlockSpec((8, 128), lambda i, k: (0, 0)), pl.BlockSpec((1, D), lambda i, k: (0, 0))],
        out_shape=[_sds((Tp, D), F32), _sds((Tp, D), BF16), _sds((8, 128), F32), _sds((1, D), F32)],
        scratch_shapes=[pltpu.VMEM((tm, D), F32), pltpu.VMEM((tm, D), F32), pltpu.SemaphoreType.DMA((2,))],
        compiler_params=_cp("arbitrary", "arbitrary"),
    )(act, w2, h1, target, gf)


def _dact_bwd(dh2b, w2_slots, a1, name):
    Tp, D = dh2b.shape
    S, n, _ = w2_slots.shape
    tm = _tile(Tp, TILE["dact"])

    def body(g_ref, w_ref, a1_ref, o_ref):
        dact = lax.dot_general(g_ref[...], w_ref[...], _NT, preferred_element_type=F32)
        o_ref[...] = (dact * (2.0 * jnp.maximum(a1_ref[...].astype(F32), 0.0))).astype(BF16)

    return _call(
        body, name=name, grid=(S, Tp // tm),
        in_specs=[pl.BlockSpec((tm, D), lambda j, i: (i, 0)), pl.BlockSpec((None, n, D), lambda j, i: (j, 0, 0)),
                  pl.BlockSpec((tm, n), lambda j, i: (i, j))],
        out_specs=pl.BlockSpec((tm, n), lambda j, i: (i, j)),
        out_shape=_sds((Tp, S * n), BF16),
        compiler_params=_cp("parallel", "parallel"),
    )(dh2b, w2_slots, a1)


def _weight_grad(a, g, blocks, block_a, name, tasks=()):
    Tp, Ka = a.shape
    Ng = g.shape[1]
    ka = Ka // blocks if block_a else Ka
    ng = Ng if block_a else Ng // blocks
    tt = _tile(Tp, TILE["tn"])
    nt = Tp // tt

    def body(a_ref, g_ref, o_ref, acc_ref):
        t = pl.program_id(1)

        @pl.when(t == 0)
        def _():
            acc_ref[...] = jnp.zeros_like(acc_ref)

        acc_ref[...] += lax.dot_general(a_ref[...], g_ref[...], _TN, preferred_element_type=F32)

        @pl.when(t == nt - 1)
        def _():
            o_ref[...] = acc_ref[...].astype(o_ref.dtype)

    if block_a:
        a_spec = pl.BlockSpec((tt, ka), lambda j, t: (t, j))
        g_spec = pl.BlockSpec((tt, ng), lambda j, t: (t, 0))
    else:
        a_spec = pl.BlockSpec((tt, ka), lambda j, t: (t, 0))
        g_spec = pl.BlockSpec((tt, ng), lambda j, t: (t, j))
    (dw,), extra = _hosted(
        body, tasks, name=name, grid=(blocks, nt),
        in_specs=[a_spec, g_spec],
        out_specs=[pl.BlockSpec((None, ka, ng), lambda j, t: (j, 0, 0))],
        out_shape=[_sds((blocks, ka, ng), BF16)],
        scratch_shapes=[pltpu.VMEM((ka, ng), F32)],
        semantics=("parallel", "arbitrary"), operands=(a, g))
    return dw, extra


def _nt_norm_bwd(dz, w_slots, dres, hin, rin, g, want_bf16, name, tasks=(), tiles=None, earlier=None):
    Tp, D = hin.shape
    S, _, n = w_slots.shape
    tm = _tile(Tp, TILE["nt"])
    t0, nt = tiles if tiles is not None else (0, Tp // tm)
    assert not (want_bf16 and earlier is not None)

    rc = _tile(tm, EPILOGUE_ROWS)

    def body(dz_ref, w_ref, dres_hbm, h_hbm, r_ref, g_ref, *rest):
        if earlier is not None:
            _, dg0_ref, dh_ref, dg_ref, dres_buf, h_buf, sems = rest
        elif want_bf16:
            dh_ref, dhb_ref, dg_ref, dres_buf, h_buf, sems = rest
        else:
            dh_ref, dg_ref, dres_buf, h_buf, sems = rest
        i, k = pl.program_id(0), pl.program_id(1)
        tile_rows = pl.ds(pl.multiple_of((t0 + i) * tm, tm), tm)
        fetch = [pltpu.make_async_copy(dres_hbm.at[tile_rows, :], dres_buf, sems.at[0]),
                 pltpu.make_async_copy(h_hbm.at[tile_rows, :], h_buf, sems.at[1])]

        @pl.when(k == 0)
        def _():
            for f in fetch:
                f.start()
            dh_ref[...] = jnp.zeros_like(dh_ref)

        @pl.when((i == 0) & (k == 0))
        def _():
            dg_ref[...] = jnp.zeros_like(dg_ref) if earlier is None else dg0_ref[...]

        dh_ref[...] += lax.dot_general(dz_ref[...], w_ref[...], _NT, preferred_element_type=F32)

        @pl.when(k == S - 1)
        def _():
            for f in fetch:
                f.wait()
            g = g_ref[...]

            def chunk(c, dg_acc):
                rows = pl.ds(pl.multiple_of(c * rc, rc), rc)
                du = dh_ref[rows, :]
                h = h_buf[rows, :]
                r = r_ref[rows, :]
                dug = du * g
                dh = dres_buf[rows, :] + r * dug - h * (r * r * r * jnp.mean(dug * h, axis=-1, keepdims=True))
                dh_ref[rows, :] = dh
                if want_bf16:
                    dhb_ref[rows, :] = dh.astype(BF16)
                return dg_acc + jnp.sum(du * (h * r), axis=0, keepdims=True)

            dg_ref[...] += lax.fori_loop(0, tm // rc, chunk, jnp.zeros((1, D), F32))

    row = pl.BlockSpec((tm, D), lambda i, k: (t0 + i, 0))
    vec = pl.BlockSpec((1, D), lambda i, k: (0, 0))
    hbm = pl.BlockSpec(memory_space=pl.ANY)
    out_specs = [row] + ([row] if want_bf16 else []) + [vec]
    out_shape = [_sds((Tp, D), F32)] + ([_sds((Tp, D), BF16)] if want_bf16 else []) + [_sds((1, D), F32)]
    in_specs = [pl.BlockSpec((tm, n), lambda i, k: (t0 + i, k)), pl.BlockSpec((None, D, n), lambda i, k: (k, 0, 0)),
                hbm, hbm, pl.BlockSpec((tm, 1), lambda i, k: (t0 + i, 0)), vec]
    operands = (dz, w_slots, dres, hin, rin, g)
    aliases = {}
    if earlier is not None:
        in_specs += [hbm, vec]
        operands += tuple(earlier)
        aliases = {6: 0}
    return _hosted(
        body, tasks, name=name, grid=(nt, S), in_specs=in_specs, out_specs=out_specs, out_shape=out_shape,
        scratch_shapes=[pltpu.VMEM((tm, D), F32), pltpu.VMEM((tm, D), F32), pltpu.SemaphoreType.DMA((2,))],
        semantics=("arbitrary", "arbitrary"), operands=operands, aliases=aliases)


def _dmerged_bwd(dh1b, w_out, name):
    Tp, D = dh1b.shape
    tm = _tile(Tp, TILE["dmerged"])

    def body(g_ref, w_ref, o_ref):
        o_ref[...] = lax.dot_general(g_ref[...], w_ref[...], _NT, preferred_element_type=F32)

    row = pl.BlockSpec((tm, D), lambda i: (i, 0))
    return _call(
        body, name=name, grid=(Tp // tm,),
        in_specs=[row, pl.BlockSpec((D, D), lambda i: (0, 0))],
        out_specs=row, out_shape=_sds((Tp, D), F32),
        compiler_params=_cp("parallel"),
    )(dh1b, w_out)


def _pool_bwd(dmerged, proj, y_pool, d_pool, scale, pool_w, name, tasks=()):
    Tp, D = dmerged.shape
    G, Cg, _ = pool_w.shape
    tm = _tile(Tp, TILE["pool"])
    nt = Tp // tm

    def body(dm_ref, gp_ref, y_ref, d_ref, sc_ref, w_ref, dv_ref, dgp_ref, dw_ref, dsc_ref, next_ref):
        t = pl.program_id(0)
        tile = nt - 1 - t

        @pl.when(t == 0)
        def _():
            next_ref[...] = jnp.zeros_like(next_ref)
            dw_ref[...] = jnp.zeros_like(dw_ref)
            dsc_ref[...] = jnp.zeros_like(dsc_ref)

        rows = _row_ids(tile, tm)
        dm = dm_ref[...]
        y = y_ref[...]
        sc = sc_ref[...]
        sg = _sigmoid(gp_ref[...])
        dpo = dm * sg
        dgp_ref[...] = (dm * (y * sc) * sg * (1.0 - sg)).astype(BF16)
        dsc_ref[...] += jnp.sum(dpo * y, axis=0, keepdims=True)
        dyb = (dpo * sc).astype(BF16)
        for g, win in enumerate(POOL_WINDOWS):
            cols = slice(g * Cg, (g + 1) * Cg)
            dy = dyb[:, cols]
            dd = lax.dot_general(dy, w_ref[g], _NT, preferred_element_type=F32)
            dw_ref[g] += lax.dot_general(d_ref[:, cols], dy, _TN, preferred_element_type=F32)
            q = dd / jnp.minimum(rows + 1, win).astype(F32)
            s = jnp.concatenate([q, next_ref[:, cols]], axis=0)
            k = 1
            while k < win:
                s = s + pltpu.roll(s, s.shape[0] - k, axis=0)
                k *= 2
            dv_ref[:, cols] = (s[:tm] - dd).astype(BF16)
            next_ref[:, cols] = q[:MAX_WINDOW]

    row = pl.BlockSpec((tm, D), lambda t: (nt - 1 - t, 0))
    return _hosted(
        body, tasks, name=name, grid=(nt,),
        in_specs=[row, pl.BlockSpec((tm, D), lambda t: (nt - 1 - t, 3)), row, row,
                  pl.BlockSpec((1, D), lambda t: (0, 0)), pl.BlockSpec((G, Cg, Cg), lambda t: (0, 0, 0))],
        out_specs=[row, row, pl.BlockSpec((G, Cg, Cg), lambda t: (0, 0, 0)), pl.BlockSpec((1, D), lambda t: (0, 0))],
        out_shape=[_sds((Tp, D), BF16), _sds((Tp, D), BF16), _sds((G, Cg, Cg), F32), _sds((1, D), F32)],
        scratch_shapes=[pltpu.VMEM((MAX_WINDOW, D), F32)],
        semantics=("arbitrary",), operands=(dmerged, proj, y_pool, d_pool, scale, pool_w))


LRU_SMALL_ROWS = 8


def _lru_bwd(dmerged, proj, xc, r_gate, i_gate, a_gate, mult_gate, hs, lam, conv_w, wa, wx, name, tasks=()):
    Tp, D = dmerged.shape
    H, hd, _ = wa.shape
    tm = _tile(Tp, TILE["lru"])
    nt = Tp // tm
    nb = D // hd
    halo_blocks = tm // HALO

    def body(dm_ref, vl_ref, vg_ref, gl_ref, xc_ref, r_ref, i_ref, a_ref, mult_ref, hs_ref, hsp_ref, lam_ref, cw_ref,
             wa_ref, wx_ref, dvl_ref, dvg_ref, dgl_ref, dwa_ref, dwx_ref, small_ref,
             mu_next_ref, dxc_next_ref, q_s, mu_s):
        t = pl.program_id(1)
        tile = nt - 1 - t

        @pl.when(t == 0)
        def _():
            mu_next_ref[...] = jnp.zeros_like(mu_next_ref)
            dxc_next_ref[...] = jnp.zeros_like(dxc_next_ref)
            dwa_ref[...] = jnp.zeros_like(dwa_ref)
            dwx_ref[...] = jnp.zeros_like(dwx_ref)
            small_ref[...] = jnp.zeros_like(small_ref)

        first = tile == 0
        dm = dm_ref[...]
        hs_t = hs_ref[...]
        xc_t = xc_ref[...]
        r = r_ref[...]
        i = i_ref[...]
        lam_v = lam_ref[...]
        sp = _softplus(-lam_v)
        a = a_ref[...]
        mult = mult_ref[...]

        sg = _sigmoid(gl_ref[...])
        ge, dge = _gelu_and_grad(vg_ref[...])
        dlo = dm * sg
        dgl_ref[...] = (dm * (hs_t * ge) * sg * (1.0 - sg)).astype(BF16)
        dvg_ref[...] = (dlo * hs_t * dge).astype(BF16)
        dhs = dlo * ge

        q_s[...] = a * dhs
        mu_first = _scan_chunks(a_ref, q_s, mu_s, mu_next_ref[0:1, :], tm, reverse=True)
        lam_t = dhs + _shift_up(mu_s[...], mu_next_ref[...], 1)
        mu_next_ref[...] = jnp.broadcast_to(mu_first, mu_next_ref.shape)

        h_prev = _shift_down(jnp.where(first, 0.0, hsp_ref[...]), hs_t, 1)
        da = lam_t * h_prev
        dmult = lam_t * (i * xc_t)
        di = lam_t * mult * xc_t
        dxc = lam_t * mult * i
        dlog_a = da * a - dmult * (a * a) / mult
        dr = dlog_a * (-LRU_C * sp)
        dlam_rows = dlog_a * (-LRU_C * r)
        dza = dr * r * (1.0 - r)
        dzx = di * i * (1.0 - i)
        dzab, dzxb = dza.astype(BF16), dzx.astype(BF16)
        xcb = xc_t.astype(BF16)
        dxc = dxc + lax.dot_general(dzab, wa_ref[...], _NT, preferred_element_type=F32)
        dxc = dxc + lax.dot_general(dzxb, wx_ref[...], _NT, preferred_element_type=F32)
        dwa_ref[...] += lax.dot_general(xcb, dzab, _TN, preferred_element_type=F32)
        dwx_ref[...] += lax.dot_general(xcb, dzxb, _TN, preferred_element_type=F32)

        dxc_next = dxc_next_ref[...]
        taps = [_shift_up(dxc, dxc_next, CONV_WIDTH - 1 - k) for k in range(CONV_WIDTH)]
        dv = jnp.zeros_like(dxc)
        for k in range(CONV_WIDTH):
            dv = dv + cw_ref[k:k + 1, :] * taps[k]
        dvl_ref[...] = dv.astype(BF16)
        dxc_next_ref[...] = dxc[:HALO, :]

        v_t = vl_ref[...]
        small = [jnp.sum(dza, axis=0, keepdims=True), jnp.sum(dzx, axis=0, keepdims=True),
                 jnp.sum(dlam_rows, axis=0, keepdims=True) * (-_sigmoid(-lam_v)),
                 jnp.sum(dxc, axis=0, keepdims=True)]
        for k in range(CONV_WIDTH):
            small.append(jnp.sum(taps[k] * v_t, axis=0, keepdims=True))
        for k, row in enumerate(small):
            small_ref[k:k + 1, :] += row

    def piece(p):
        return pl.BlockSpec((tm, hd), lambda h, t: (nt - 1 - t, p * nb + h))

    def halo(p):
        return pl.BlockSpec((HALO, hd), lambda h, t: (jnp.maximum((nt - 1 - t) * halo_blocks - 1, 0), p * nb + h))

    blk = pl.BlockSpec((tm, hd), lambda h, t: (nt - 1 - t, h))
    vec = pl.BlockSpec((1, hd), lambda h, t: (0, h))
    mat = pl.BlockSpec((None, hd, hd), lambda h, t: (h, 0, 0))
    return _hosted(
        body, tasks, name=name, grid=(H, nt),
        in_specs=[blk, piece(1), piece(2), piece(4), blk, blk, blk, blk, blk, blk, halo(0), vec,
                  pl.BlockSpec((CONV_WIDTH, hd), lambda h, t: (0, h)), mat, mat],
        out_specs=[blk, blk, blk, mat, mat, pl.BlockSpec((None, LRU_SMALL_ROWS, hd), lambda h, t: (h, 0, 0))],
        out_shape=[_sds((Tp, D), BF16)] * 3 + [_sds((H, hd, hd), F32)] * 2 + [_sds((H, LRU_SMALL_ROWS, hd), F32)],
        scratch_shapes=[pltpu.VMEM((HALO, hd), F32), pltpu.VMEM((HALO, hd), F32),
                        pltpu.VMEM((tm, hd), F32), pltpu.VMEM((tm, hd), F32)],
        semantics=("parallel", "arbitrary"),
        operands=(dmerged, proj, proj, proj, xc, r_gate, i_gate, a_gate, mult_gate, hs, hs, lam, conv_w, wa, wx))


def _adamw(w, g, m, v):
    m = ADAM_B1 * m + (1.0 - ADAM_B1) * g
    v = ADAM_B2 * v + (1.0 - ADAM_B2) * (g * g)
    m_hat = m / (1.0 - ADAM_B1 ** ADAM_STEP)
    v_hat = v / (1.0 - ADAM_B2 ** ADAM_STEP)
    delta = -ADAM_LR * (m_hat / (jnp.sqrt(v_hat) + ADAM_EPS) + ADAM_WD * w)
    return delta, m, v


def _reduce_update(pair_sums, chip_sums, w, m, v, chip_slot, name):
    R, C = w.shape
    tr = _tile(R, TILE["update"])

    def body(slot_ref, own_ref, got_ref, w_ref, m_ref, v_ref, g_out, d_out, m_out, v_out):
        g = own_ref[...].astype(F32)
        for k in range(3):
            g = g + got_ref[k].astype(F32)
        d, m_new, v_new = _adamw(w_ref[...], g, m_ref[...], v_ref[...])
        g_out[...] = g
        d_out[...] = d
        m_out[...] = m_new
        v_out[...] = v_new

    blk = pl.BlockSpec((tr, C), lambda i, s: (i, 0))
    return _call(
        body, name=name,
        grid_spec=pltpu.PrefetchScalarGridSpec(
            num_scalar_prefetch=1, grid=(R // tr,),
            in_specs=[pl.BlockSpec((None, tr, C), lambda i, s: (s[0], i, 0)),
                      pl.BlockSpec((3, tr, C), lambda i, s: (0, i, 0)), blk, blk, blk],
            out_specs=[blk] * 4),
        out_shape=[_sds((R, C), F32)] * 4,
        compiler_params=_cp("parallel"),
    )(chip_slot, pair_sums, chip_sums, w, m, v)


def _small_update(w, g, m, v, name):
    def body(w_ref, g_ref, m_ref, v_ref, d_out, m_out, v_out):
        d, m_new, v_new = _adamw(w_ref[...], g_ref[...], m_ref[...], v_ref[...])
        d_out[...] = d
        m_out[...] = m_new
        v_out[...] = v_new

    return _call(body, name=name, out_shape=[_sds(w.shape, F32)] * 3)(w, g, m, v)


def _slots_from_rows(full, lead):
    L, R, C = full.shape
    r = R // N_DEV
    return full.reshape(L, N_DEV, r, C).transpose(1, 0, 2, 3).reshape(N_DEV, L * r, C)


def _rows_from_slots(slots, lead):
    _, LR, C = slots.shape
    r = LR // lead
    return slots.reshape(N_DEV, lead, r, C).transpose(1, 0, 2, 3).reshape(lead, N_DEV * r, C)


def kernel(x, meta_tokens, norm1_g, w_in, pool_w, pool_scale, conv_w, conv_b, gate_a_w, gate_a_b, gate_x_w, gate_x_b, lru_lambda, w_out, norm2_g, mlp_w1, mlp_w2, final_g, loss_target, m_meta_tokens, m_norm1_g, m_w_in, m_pool_w, m_pool_scale, m_conv_w, m_conv_b, m_gate_a_w, m_gate_a_b, m_gate_x_w, m_gate_x_b, m_lru_lambda, m_w_out, m_norm2_g, m_mlp_w1, m_mlp_w2, m_final_g, v_meta_tokens, v_norm1_g, v_w_in, v_pool_w, v_pool_scale, v_conv_w, v_conv_b, v_gate_a_w, v_gate_a_b, v_gate_x_w, v_gate_x_b, v_lru_lambda, v_w_out, v_norm2_g, v_mlp_w1, v_mlp_w2, v_final_g):
    seq, D = x.shape[1], x.shape[2]
    n_meta = meta_tokens.shape[0]
    G, Cg = pool_w.shape[1], pool_w.shape[3]
    H, hd = gate_a_w.shape[1], gate_a_w.shape[3]
    T = n_meta + seq
    Tp = -(-T // ROW_ALIGN) * ROW_ALIGN
    ix, iy, ic = _pos()
    me = 4 * ix + 2 * iy + ic
    core = jnp.reshape(ic, (1,)).astype(jnp.int32)
    chip_slot = jnp.reshape(2 * ix + iy, (1,)).astype(jnp.int32)

    w_in_l, w1_l, w2_l, w_out_l = w_in[0], mlp_w1[0], mlp_w2[0], w_out[0]
    pool_l = pool_w[0].reshape(G * (Cg // N_DEV), Cg)
    wa_l = gate_a_w[0].reshape(H * (hd // N_DEV), hd)
    wx_l = gate_x_w[0].reshape(H * (hd // N_DEV), hd)
    small_params = jnp.concatenate(
        [meta_tokens, conv_w[0], jnp.zeros((4, D // N_DEV), F32)], axis=0)
    biases = jnp.concatenate([gate_a_b[0], gate_x_b[0]], axis=0)
    (w_in_g, pool_g, wa_g, wx_g, small_g, bias_g) = _all_gather(
        [w_in_l.astype(BF16), pool_l.astype(BF16), wa_l.astype(BF16), wx_l.astype(BF16), small_params, biases],
        "gather_first")
    pool_full = _rows_from_slots(pool_g, G)
    wa_full = _rows_from_slots(wa_g, H)
    wx_full = _rows_from_slots(wx_g, H)
    small_full = small_g.transpose(1, 0, 2).reshape(n_meta + 8, D)
    meta_full = small_full[:n_meta]
    conv_full = small_full[n_meta:n_meta + CONV_WIDTH]
    bias_full = bias_g.transpose(1, 0, 2).reshape(2 * H, hd)
    ba_full, bx_full = bias_full[:H], bias_full[H:]

    h0 = jnp.concatenate([meta_full, x[0], jnp.zeros((Tp - T, D), F32)], axis=0)
    target = jnp.concatenate([jnp.zeros((n_meta, D), F32), loss_target[0], jnp.zeros((Tp - T, D), F32)], axis=0)
    u, r1 = _norm_fwd(h0, norm1_g, "norm1")
    proj, ((w_out_g1,), (w1_g1,)) = _proj_fwd(
        u, w_in_g, "proj", tasks=[_AgStart(w_out_l.astype(BF16)), _AgStart(w1_l.astype(BF16))])
    d_pool, y_pool = _pool_fwd(proj, pool_full, "pool_fwd")
    (xc, r_gate, i_gate, a_gate, mult_gate, hs, merged), ((w_out_g,), (w1_g,), (w2_g1,)) = _lru_fwd(
        proj, y_pool, pool_scale, conv_full, conv_b, wa_full, ba_full, wx_full, bx_full, lru_lambda, "lru_fwd",
        tasks=[_AgForward(w_out_g1), _AgForward(w1_g1), _AgStart(w2_l.astype(BF16))])
    w_out_full = w_out_g.reshape(D, D)
    (h1, u2, r2), ((w2_g,),) = _wout_norm_fwd(merged, w_out_full, h0, norm2_g, "wout_norm2", tasks=[_AgForward(w2_g1)])
    act, a1 = _mlp1_fwd(u2, w1_g, "mlp1")
    dh2, dh2b, loss_tile, d_final_g = _mlp2_loss(
        act, w2_g.reshape(-1, D), h1, target, final_g.reshape(1, D), n_meta, seq, "mlp2_loss")

    def pair(part, got, tag):
        return _pair_sum(part, got, core, "pair_sum_" + tag)

    d_a1 = _dact_bwd(dh2b, w2_g, a1, "dact")
    dw2_p, _ = _weight_grad(act, dh2b, N_DEV, True, "dw2")
    dw1_p, ((dw2_got,),) = _weight_grad(u2, d_a1, N_DEV, False, "dw1", tasks=[_RsSibling(dw2_p)])
    dw2_pair = pair(dw2_p, dw2_got, "w2")
    (dh1, dh1b, d_norm2_g), ((dw2_chips,), (dw1_got,)) = _nt_norm_bwd(
        d_a1, w1_g, dh2, h1, r2, norm2_g, True, "du2_norm2", tasks=[_RsChips(dw2_pair), _RsSibling(dw1_p)])
    dw1_pair = pair(dw1_p, dw1_got, "w1")
    dmerged = _dmerged_bwd(dh1b, w_out_full, "dmerged")
    dwout_p, _ = _weight_grad(merged, dh1b, 2, True, "dwout")
    dwout_p = dwout_p.reshape(N_DEV, D // N_DEV, D)
    (d_vpool, d_gpool, dpool_full, d_scale), ((dwout_got,),) = _pool_bwd(
        dmerged, proj, y_pool, d_pool, pool_scale, pool_full, "pool_bwd", tasks=[_RsSibling(dwout_p)])
    dwout_pair = pair(dwout_p, dwout_got, "wout")
    (d_vlru, d_vgelu, d_glru, dwa_full, dwx_full, lru_small), ((dw1_chips,), (dwout_chips,)) = _lru_bwd(
        dmerged, proj, xc, r_gate, i_gate, a_gate, mult_gate, hs, lru_lambda, conv_full, wa_full, wx_full, "lru_bwd",
        tasks=[_RsChips(dw1_pair), _RsChips(dwout_pair)])
    dproj = jnp.concatenate([d_vpool, d_vlru, d_vgelu, d_gpool, d_glru], axis=1)
    dwin_p, _ = _weight_grad(u, dproj, N_DEV, False, "dwin")
    dpool_p = _slots_from_rows(dpool_full, G).astype(BF16)
    dwa_p = _slots_from_rows(dwa_full, H).astype(BF16)
    dwx_p = _slots_from_rows(dwx_full, H).astype(BF16)
    late = [dwin_p, dpool_p, dwa_p, dwx_p]
    n_tiles = Tp // _tile(Tp, TILE["nt"])
    n_early = max(1, n_tiles // 5)
    (dh0_a, d_norm1_a), late_got = _nt_norm_bwd(
        dproj, w_in_g, dh1, h0, r1, norm1_g, False, "du_norm1_a", tasks=[_RsSibling(p) for p in late],
        tiles=(0, n_early))
    late_pair = [pair(p, g[0], "late%d" % k) for k, (p, g) in enumerate(zip(late, late_got))]
    (dh0, d_norm1_g), late_chips = _nt_norm_bwd(
        dproj, w_in_g, dh1, h0, r1, norm1_g, False, "du_norm1_b", tasks=[_RsChips(p) for p in late_pair],
        tiles=(n_early, n_tiles - n_early), earlier=(dh0_a, d_norm1_a))
    grad_x = dh0[n_meta:T][None]

    pair_sums = [dw2_pair, dw1_pair, dwout_pair] + late_pair
    chip_sums = [dw2_chips, dw1_chips, dwout_chips] + [c[0] for c in late_chips]
    big = {}
    names = ["mlp_w2", "mlp_w1", "w_out", "w_in", "pool_w", "gate_a_w", "gate_x_w"]
    trip = {"mlp_w2": (mlp_w2, m_mlp_w2, v_mlp_w2), "mlp_w1": (mlp_w1, m_mlp_w1, v_mlp_w1),
            "w_out": (w_out, m_w_out, v_w_out), "w_in": (w_in, m_w_in, v_w_in),
            "pool_w": (pool_w, m_pool_w, v_pool_w), "gate_a_w": (gate_a_w, m_gate_a_w, v_gate_a_w),
            "gate_x_w": (gate_x_w, m_gate_x_w, v_gate_x_w)}
    for k, nm in enumerate(names):
        w_, m_, v_ = trip[nm]
        shape2 = pair_sums[k].shape[1:]
        outs = _reduce_update(pair_sums[k], chip_sums[k], w_.reshape(shape2), m_.reshape(shape2), v_.reshape(shape2),
                              chip_slot, "update_" + nm)
        big[nm] = [o.reshape(w_.shape) for o in outs]

    lru_rows = lru_small.transpose(1, 0, 2).reshape(LRU_SMALL_ROWS, D)
    small_part = jnp.concatenate(
        [dh0[:n_meta], d_norm1_g, d_scale, d_norm2_g, d_final_g, lru_rows, jnp.zeros((4, D), F32)], axis=0)
    (small_all,) = _all_gather([small_part], "gather_small_grads")
    small_sum = _sum_slots(small_all, "sum_small_grads")
    o = n_meta
    g_meta_full = small_sum[:o]
    g_norm1, g_scale, g_norm2, g_final = (small_sum[o + k:o + k + 1] for k in range(4))
    g_ba_full, g_bx_full, g_lam, g_cb = (small_sum[o + 4 + k:o + 5 + k] for k in range(4))
    g_cw_full = small_sum[o + 8:o + 8 + CONV_WIDTH]
    dcol = D // N_DEV
    g_meta = lax.dynamic_slice_in_dim(g_meta_full, me * dcol, dcol, axis=1)
    g_cw = lax.dynamic_slice_in_dim(g_cw_full, me * dcol, dcol, axis=1)
    hcol = hd // N_DEV
    g_ba = lax.dynamic_slice_in_dim(g_ba_full.reshape(H, hd), me * hcol, hcol, axis=1)
    g_bx = lax.dynamic_slice_in_dim(g_bx_full.reshape(H, hd), me * hcol, hcol, axis=1)

    rep_w = jnp.concatenate([norm1_g, pool_scale, conv_b, lru_lambda, norm2_g, final_g.reshape(1, D)], axis=0)
    rep_g = jnp.concatenate([g_norm1, g_scale, g_cb, g_lam, g_norm2, g_final], axis=0)
    rep_m = jnp.concatenate([m_norm1_g, m_pool_scale, m_conv_b, m_lru_lambda, m_norm2_g, m_final_g.reshape(1, D)], axis=0)
    rep_v = jnp.concatenate([v_norm1_g, v_pool_scale, v_conv_b, v_lru_lambda, v_norm2_g, v_final_g.reshape(1, D)], axis=0)
    rep_d, rep_nm, rep_nv = _small_update(rep_w, rep_g, rep_m, rep_v, "update_vectors")
    col_w = jnp.concatenate([meta_tokens, conv_w[0]], axis=0)
    col_g = jnp.concatenate([g_meta, g_cw], axis=0)
    col_m = jnp.concatenate([m_meta_tokens, m_conv_w[0]], axis=0)
    col_v = jnp.concatenate([v_meta_tokens, v_conv_w[0]], axis=0)
    col_d, col_nm, col_nv = _small_update(col_w, col_g, col_m, col_v, "update_columns")
    b_w = jnp.concatenate([gate_a_b[0], gate_x_b[0]], axis=0)
    b_g = jnp.concatenate([g_ba, g_bx], axis=0)
    b_m = jnp.concatenate([m_gate_a_b[0], m_gate_x_b[0]], axis=0)
    b_v = jnp.concatenate([v_gate_a_b[0], v_gate_x_b[0]], axis=0)
    b_d, b_nm, b_nv = _small_update(b_w, b_g, b_m, b_v, "update_biases")

    def rep(arr, k, like):
        return arr[k:k + 1].reshape(like.shape)

    rep_order = {"norm1_g": 0, "pool_scale": 1, "conv_b": 2, "lru_lambda": 3, "norm2_g": 4, "final_g": 5}
    like = {"norm1_g": norm1_g, "pool_scale": pool_scale, "conv_b": conv_b, "lru_lambda": lru_lambda,
            "norm2_g": norm2_g, "final_g": final_g}

    def leaves(kind):
        rep_src = [rep_g, rep_d, rep_nm, rep_nv][kind]
        col_src = [col_g, col_d, col_nm, col_nv][kind]
        b_src = [b_g, b_d, b_nm, b_nv][kind]
        out = {}
        out["meta_tokens"] = col_src[:n_meta]
        out["conv_w"] = col_src[n_meta:][None]
        out["gate_a_b"] = b_src[:H][None]
        out["gate_x_b"] = b_src[H:][None]
        for nm, k in rep_order.items():
            out[nm] = rep(rep_src, k, like[nm])
        for nm in names:
            out[nm] = big[nm][kind]
        order = ["meta_tokens", "norm1_g", "w_in", "pool_w", "pool_scale", "conv_w", "conv_b", "gate_a_w", "gate_a_b",
                 "gate_x_w", "gate_x_b", "lru_lambda", "w_out", "norm2_g", "mlp_w1", "mlp_w2", "final_g"]
        return [out[nm] for nm in order]

    loss = lax.psum(loss_tile[0, 0], ("x", "y", "c"))
    return (loss, grad_x, *leaves(0), *leaves(1), *leaves(2), *leaves(3))
```

```python
import functools

import jax
import jax.numpy as jnp
from jax import lax
from jax.experimental import pallas as pl
from jax.experimental.pallas import tpu as pltpu

F32 = jnp.float32
BF16 = jnp.bfloat16
MESH = pl.DeviceIdType.MESH
N_DEV = 8
POOL_WINDOWS = (2, 4, 8, 16)
MAX_WINDOW = 16
CONV_WIDTH = 4
HALO = 8
LRU_C = 8.0
NORM_EPS = 1e-6
ADAM_LR, ADAM_B1, ADAM_B2, ADAM_EPS, ADAM_WD, ADAM_STEP = 0.001, 0.9, 0.999, 1e-08, 0.01, 10
ROW_ALIGN = 128
VMEM_LIMIT = 56 << 20
TILE = dict(norm=384, proj=704, pool=384, lru=384, wout=384, mlp1=704, mlp2=704, dact=704, tn=1408,
            nt=704, dmerged=384, update=256)
MLP2_K = 1024
EPILOGUE_ROWS = 32
MID_STEP_PERCENT = 88

_NT = (((1,), (1,)), ((), ()))
_TN = (((0,), (0,)), ((), ()))


def _call(body, **kw):
    return pl.pallas_call(body, **kw)


def _cp(*sem):
    return pltpu.CompilerParams(dimension_semantics=sem, vmem_limit_bytes=VMEM_LIMIT)


def _tile(total, pref):
    best = None
    for t in range(16, min(total, pref) + 1, 16):
        if total % t == 0:
            best = t
    assert best is not None, (total, pref)
    return best


def _sds(shape, dtype):
    return jax.ShapeDtypeStruct(shape, dtype)


def _pos():
    return lax.axis_index("x"), lax.axis_index("y"), lax.axis_index("c")


def _all_gather(shards, name):
    n = len(shards)

    def body(*refs):
        ins, outs = refs[:n], refs[n:2 * n]
        send_sems, recv_sems, local_sems = refs[2 * n:]
        x, y, c = _pos()
        me, sib = (x, y, c), (x, y, 1 - c)
        chips = [(1 - x, y), (x, 1 - y), (1 - x, 1 - y)]

        def slot(p):
            return 4 * p[0] + 2 * p[1] + p[2]

        def copy(a, k, block, to, src=None):
            dst = outs[a].at[slot(block)]
            return pltpu.make_async_remote_copy(
                src_ref=dst if src is None else src, dst_ref=dst,
                send_sem=send_sems.at[7 * a + k], recv_sem=recv_sems.at[7 * a + k],
                device_id=to, device_id_type=MESH)

        mine = [pltpu.make_async_copy(ins[a], outs[a].at[slot(me)], local_sems.at[a]) for a in range(n)]
        for m in mine:
            m.start()
        first = []
        for a in range(n):
            first.append(copy(a, 0, me, sib, src=ins[a]))
            first += [copy(a, 1 + j, me, (*chip, c), src=ins[a]) for j, chip in enumerate(chips)]
        for cp in first:
            cp.start()
        passed = []
        for a in range(n):
            for j, chip in enumerate(chips):
                copy(a, 1 + j, (*chip, c), me).wait_recv()
                fwd = copy(a, 4 + j, (*chip, c), sib)
                fwd.start()
                passed.append(fwd)
        for a in range(n):
            copy(a, 0, sib, me).wait_recv()
            for j, chip in enumerate(chips):
                copy(a, 4 + j, (*chip, 1 - c), me).wait_recv()
        for cp in first + passed:
            cp.wait_send()
        for m in mine:
            m.wait()

    hbm = pl.BlockSpec(memory_space=pl.ANY)
    return _call(
        body, name=name,
        out_shape=[_sds((N_DEV,) + s.shape, s.dtype) for s in shards],
        in_specs=[hbm] * n, out_specs=[hbm] * n,
        scratch_shapes=[pltpu.SemaphoreType.DMA((7 * n,)), pltpu.SemaphoreType.DMA((7 * n,)),
                        pltpu.SemaphoreType.DMA((n,))],
    )(*shards)


def _other_chips(x, y):
    return [(1 - x, y), (x, 1 - y), (1 - x, 1 - y)]


class _AgFull:
    n_sem, n_local = 7, 1

    def __init__(self, shard):
        self.ins = [shard]
        self.out_shapes = [_sds((N_DEV,) + shard.shape, shard.dtype)]
        self.aliases = []

    def _peers(self):
        x, y, c = _pos()
        return [(x, y, 1 - c)] + [(*chip, c) for chip in _other_chips(x, y)]

    def _sends(self, ins, outs, sems):
        send, recv, _, base, _ = sems
        x, y, c = _pos()
        mine = outs[0].at[4 * x + 2 * y + c]
        return [pltpu.make_async_remote_copy(src_ref=ins[0], dst_ref=mine, send_sem=send.at[base + k],
                                             recv_sem=recv.at[base + k], device_id=p, device_id_type=MESH)
                for k, p in enumerate(self._peers())]

    def _arrivals(self, outs, sems):
        send, recv, _, base, _ = sems
        res = []
        for k, p in enumerate(self._peers()):
            blk = outs[0].at[4 * p[0] + 2 * p[1] + p[2]]
            res.append(pltpu.make_async_remote_copy(src_ref=blk, dst_ref=blk, send_sem=send.at[base + k],
                                                    recv_sem=recv.at[base + k], device_id=p, device_id_type=MESH))
        return res

    def _own(self, ins, outs, sems):
        x, y, c = _pos()
        return pltpu.make_async_copy(ins[0], outs[0].at[4 * x + 2 * y + c], sems[2].at[sems[4]])

    def _forwards(self, outs, sems, core_of_block):
        send, recv, _, base, _ = sems
        x, y, c = _pos()
        res = []
        for k, chip in enumerate(_other_chips(x, y)):
            blk = outs[0].at[4 * chip[0] + 2 * chip[1] + (c if core_of_block == "mine" else 1 - c)]
            res.append(pltpu.make_async_remote_copy(src_ref=blk, dst_ref=blk, send_sem=send.at[base + 4 + k],
                                                    recv_sem=recv.at[base + 4 + k], device_id=(x, y, 1 - c),
                                                    device_id_type=MESH))
        return res

    def start(self, ins, outs, sems):
        self._own(ins, outs, sems).start()
        for cp in self._sends(ins, outs, sems):
            cp.start()

    def mid(self, ins, outs, sems):
        for cp in self._arrivals(outs, sems)[1:]:
            cp.wait_recv()
        for cp in self._forwards(outs, sems, "mine"):
            cp.start()

    def finish(self, ins, outs, sems):
        self._arrivals(outs, sems)[0].wait_recv()
        for cp in self._forwards(outs, sems, "sibling"):
            cp.wait_recv()
        for cp in self._sends(ins, outs, sems) + self._forwards(outs, sems, "mine"):
            cp.wait_send()
        self._own(ins, outs, sems).wait()


class _RsSibling:
    n_sem, n_local = 4, 0

    def __init__(self, part):
        self.ins = [part]
        self.out_shapes = [_sds((4,) + part.shape[1:], part.dtype)]
        self.aliases = []

    def _copies(self, ins, outs, sems):
        send, recv, _, base, _ = sems
        x, y, c = _pos()
        return [pltpu.make_async_remote_copy(src_ref=ins[0].at[2 * q + (1 - c)], dst_ref=outs[0].at[q],
                                             send_sem=send.at[base + q], recv_sem=recv.at[base + q],
                                             device_id=(x, y, 1 - c), device_id_type=MESH) for q in range(4)]

    def start(self, ins, outs, sems):
        for cp in self._copies(ins, outs, sems):
            cp.start()

    def finish(self, ins, outs, sems):
        for cp in self._copies(ins, outs, sems):
            cp.wait()


class _RsChips:
    n_sem, n_local = 3, 0

    def __init__(self, pair):
        self.ins = [pair]
        self.out_shapes = [_sds((3,) + pair.shape[1:], pair.dtype)]
        self.aliases = []

    def _copies(self, ins, outs, sems):
        send, recv, _, base, _ = sems
        x, y, c = _pos()
        return [pltpu.make_async_remote_copy(src_ref=ins[0].at[2 * chip[0] + chip[1]], dst_ref=outs[0].at[k],
                                             send_sem=send.at[base + k], recv_sem=recv.at[base + k],
                                             device_id=(*chip, c), device_id_type=MESH)
                for k, chip in enumerate(_other_chips(x, y))]

    def start(self, ins, outs, sems):
        for cp in self._copies(ins, outs, sems):
            cp.start()

    def finish(self, ins, outs, sems):
        for cp in self._copies(ins, outs, sems):
            cp.wait()


def _hosted(body, tasks, *, grid, in_specs, out_specs, out_shape, scratch_shapes=(), name, semantics, operands,
            aliases=None):
    in_specs, out_specs, out_shape = list(in_specs), list(out_specs), list(out_shape)
    scratch_shapes = list(scratch_shapes)
    aliases = dict(aliases or {})
    if not tasks:
        res = _call(body, name=name, grid=grid, in_specs=in_specs, out_specs=out_specs, out_shape=out_shape,
                    scratch_shapes=scratch_shapes, input_output_aliases=aliases,
                    compiler_params=_cp(*semantics))(*operands)
        return list(res), []
    n_in, n_out, n_scr = len(in_specs), len(out_specs), len(scratch_shapes)
    t_ins = [a for t in tasks for a in t.ins]
    t_outs = [o for t in tasks for o in t.out_shapes]
    i0, o0 = n_in, n_out
    for t in tasks:
        for (i, o) in t.aliases:
            aliases[i0 + i] = o0 + o
        i0 += len(t.ins)
        o0 += len(t.out_shapes)
    n_sem = sum(t.n_sem for t in tasks)
    n_local = max(1, sum(t.n_local for t in tasks))
    n_steps = 1
    for g in grid:
        n_steps *= g
    mid_step = min(n_steps - 1, (n_steps * MID_STEP_PERCENT) // 100)

    def wrapped(*refs):
        cut = [n_in, len(t_ins), n_out, len(t_outs), n_scr]
        parts, at = [], 0
        for n in cut:
            parts.append(refs[at:at + n])
            at += n
        ins, tin, outs, tout, scratch = parts
        send, recv, local = refs[at:]
        step = pl.program_id(0)
        for d in range(1, len(grid)):
            step = step * grid[d] + pl.program_id(d)

        def each(method):
            i, o, s, l = 0, 0, 0, 0
            for t in tasks:
                if hasattr(t, method):
                    getattr(t, method)(tin[i:i + len(t.ins)], tout[o:o + len(t.out_shapes)], (send, recv, local, s, l))
                i, o, s, l = i + len(t.ins), o + len(t.out_shapes), s + t.n_sem, l + t.n_local

        @pl.when(step == 0)
        def _():
            each("start")

        body(*ins, *outs, *scratch)

        @pl.when(step == mid_step)
        def _():
            each("mid")

        @pl.when(step == n_steps - 1)
        def _():
            each("finish")

    hbm = pl.BlockSpec(memory_space=pl.ANY)
    res = _call(
        wrapped, name=name, grid=grid,
        in_specs=in_specs + [hbm] * len(t_ins), out_specs=out_specs + [hbm] * len(t_outs),
        out_shape=out_shape + t_outs,
        scratch_shapes=scratch_shapes + [pltpu.SemaphoreType.DMA((n_sem,)), pltpu.SemaphoreType.DMA((n_sem,)),
                                         pltpu.SemaphoreType.DMA((n_local,))],
        input_output_aliases=aliases,
        compiler_params=_cp(*(["arbitrary"] * len(grid))),
    )(*operands, *t_ins)
    res = list(res)
    task_outs, o = [], n_out
    for t in tasks:
        task_outs.append(res[o:o + len(t.out_shapes)])
        o += len(t.out_shapes)
    return res[:n_out], task_outs


def _pair_sum(part, got, core, name):
    _, R, C = part.shape
    tr = _tile(R, TILE["update"]) if R % 16 == 0 else R

    def body(core_ref, p_ref, g_ref, o_ref):
        o_ref[...] = (p_ref[...].astype(F32) + g_ref[...].astype(F32)).astype(o_ref.dtype)

    return _call(
        body, name=name,
        grid_spec=pltpu.PrefetchScalarGridSpec(
            num_scalar_prefetch=1, grid=(4, R // tr),
            in_specs=[pl.BlockSpec((None, tr, C), lambda q, i, cr: (2 * q + cr[0], i, 0)),
                      pl.BlockSpec((None, tr, C), lambda q, i, cr: (q, i, 0))],
            out_specs=pl.BlockSpec((None, tr, C), lambda q, i, cr: (q, i, 0))),
        out_shape=_sds((4, R, C), part.dtype),
        compiler_params=_cp("parallel", "parallel"),
    )(core, part, got)


def _sum_slots(gathered, name):
    _, R, C = gathered.shape

    def body(g_ref, o_ref):
        acc = g_ref[0]
        for s in range(1, N_DEV):
            acc = acc + g_ref[s]
        o_ref[...] = acc

    return _call(body, name=name, out_shape=_sds((R, C), F32))(gathered)


def _sigmoid(z):
    return jax.nn.sigmoid(z)


def _softplus(z):
    e = jnp.exp(-jnp.abs(z))
    log1p_e = jnp.where(e < 0.01, e * (1.0 - e * (0.5 - e * (1.0 / 3.0))), jnp.log(1.0 + e))
    return jnp.maximum(z, 0.0) + log1p_e


_GELU_K = 0.7978845608028654
_GELU_C = 0.044715


def _gelu_and_grad(z):
    t = jnp.tanh(_GELU_K * (z + _GELU_C * z * z * z))
    g = 0.5 * z * (1.0 + t)
    dg = 0.5 * (1.0 + t) + 0.5 * z * (1.0 - t * t) * _GELU_K * (1.0 + 3.0 * _GELU_C * z * z)
    return g, dg


def _gelu(z):
    t = jnp.tanh(_GELU_K * (z + _GELU_C * z * z * z))
    return 0.5 * z * (1.0 + t)


def _row_ids(tile_index, tm, width=1):
    return tile_index * tm + lax.broadcasted_iota(jnp.int32, (tm, width), 0)


def _shift_down(prev, cur, k):
    if k == 0:
        return cur
    ext = jnp.concatenate([prev, cur], axis=0)
    return pltpu.roll(ext, k, axis=0)[prev.shape[0]:]


def _shift_up(cur, nxt, k):
    if k == 0:
        return cur
    ext = jnp.concatenate([cur, nxt], axis=0)
    return pltpu.roll(ext, ext.shape[0] - k, axis=0)[:cur.shape[0]]


def _lru_gates(r, sp):
    z = LRU_C * r * sp
    a = jnp.exp(-z)
    t = jnp.tanh(z)
    mult = jnp.sqrt(2.0 * t / (1.0 + t))
    return a, mult


def _scan_chunks(a_ref, b_ref, out_ref, carry, n_rows, reverse):
    n_chunks = n_rows // 8
    cols = a_ref.shape[1]
    rid = lax.broadcasted_iota(jnp.int32, (8, cols), 0)
    edge = 0 if reverse else 7

    def chunk(k, h):
        ci = (n_chunks - 1 - k) if reverse else k
        rows = pl.ds(pl.multiple_of(ci * 8, 8), 8)
        a = a_ref[rows, :]
        b = b_ref[rows, :]
        for s in (1, 2, 4):
            if reverse:
                keep = rid < 8 - s
                a_n, b_n = pltpu.roll(a, 8 - s, axis=0), pltpu.roll(b, 8 - s, axis=0)
            else:
                keep = rid >= s
                a_n, b_n = pltpu.roll(a, s, axis=0), pltpu.roll(b, s, axis=0)
            b = a * jnp.where(keep, b_n, 0.0) + b
            a = a * jnp.where(keep, a_n, 1.0)
        out_ref[rows, :] = a * h + b
        a_e = jnp.sum(jnp.where(rid == edge, a, 0.0), axis=0, keepdims=True)
        b_e = jnp.sum(jnp.where(rid == edge, b, 0.0), axis=0, keepdims=True)
        return a_e * h + b_e

    return lax.fori_loop(0, n_chunks, chunk, carry, unroll=4 if n_chunks % 4 == 0 else 1)


def _norm_fwd(h, g, name):
    Tp, D = h.shape
    tm = _tile(Tp, TILE["norm"])

    def body(h_ref, g_ref, u_ref, r_ref):
        x = h_ref[...]
        r = lax.rsqrt(jnp.mean(x * x, axis=-1, keepdims=True) + NORM_EPS)
        u_ref[...] = (x * r * g_ref[...]).astype(BF16)
        r_ref[...] = r

    return _call(
        body, name=name, grid=(Tp // tm,),
        in_specs=[pl.BlockSpec((tm, D), lambda i: (i, 0)), pl.BlockSpec((1, D), lambda i: (0, 0))],
        out_specs=[pl.BlockSpec((tm, D), lambda i: (i, 0)), pl.BlockSpec((tm, 1), lambda i: (i, 0))],
        out_shape=[_sds((Tp, D), BF16), _sds((Tp, 1), F32)],
        compiler_params=_cp("parallel"),
    )(h, g)


def _proj_fwd(u, w_slots, name, tasks=(), part=(0, 1), earlier=None):
    Tp, K = u.shape
    S, _, n = w_slots.shape
    p, parts = part
    tm = _tile(Tp, TILE["proj"])

    def body(a_ref, b_ref, *rest):
        o_ref = rest[-1]
        o_ref[...] = jnp.dot(a_ref[...], b_ref[...], preferred_element_type=F32)

    in_specs = [pl.BlockSpec((tm, K), lambda j, i: (i, 0)), pl.BlockSpec((None, K, n), lambda j, i: (j, 0, 0))]
    operands = (u, w_slots)
    aliases = {}
    if earlier is not None:
        in_specs.append(pl.BlockSpec(memory_space=pl.ANY))
        operands += (earlier,)
        aliases = {2: 0}
    (proj,), extra = _hosted(
        body, tasks, name=name, grid=(S, Tp // tm), in_specs=in_specs,
        out_specs=[pl.BlockSpec((tm, n), lambda j, i: (i, j * parts + p))],
        out_shape=[_sds((Tp, S * parts * n), F32)],
        semantics=("parallel", "parallel"), operands=operands, aliases=aliases)
    return proj, extra


def _mlp1_fwd(u2, w_slots, name, tasks=()):
    Tp, K = u2.shape
    S, _, n = w_slots.shape
    tm = _tile(Tp, TILE["mlp1"])

    def body(a_ref, b_ref, act_ref, a1_ref):
        a1 = jnp.dot(a_ref[...], b_ref[...], preferred_element_type=F32)
        relu = jnp.maximum(a1, 0.0)
        act_ref[...] = (relu * relu).astype(BF16)
        a1_ref[...] = a1.astype(BF16)

    return _hosted(
        body, tasks, name=name, grid=(S, Tp // tm),
        in_specs=[pl.BlockSpec((tm, K), lambda j, i: (i, 0)), pl.BlockSpec((None, K, n), lambda j, i: (j, 0, 0))],
        out_specs=[pl.BlockSpec((tm, n), lambda j, i: (i, j))] * 2,
        out_shape=[_sds((Tp, S * n), BF16)] * 2,
        semantics=("parallel", "parallel"), operands=(u2, w_slots))


def _pool_fwd(proj, pool_w, name):
    Tp = proj.shape[0]
    G, Cg, _ = pool_w.shape
    D = G * Cg
    tm = _tile(Tp, TILE["pool"])

    def body(v_ref, w_ref, d_ref, y_ref, prev_ref):
        t = pl.program_id(0)

        @pl.when(t == 0)
        def _():
            prev_ref[...] = jnp.zeros_like(prev_ref)

        rows = _row_ids(t, tm)
        for g, win in enumerate(POOL_WINDOWS):
            cols = slice(g * Cg, (g + 1) * Cg)
            v = v_ref[:, cols]
            s = jnp.concatenate([prev_ref[:, cols], v], axis=0)
            k = 1
            while k < win:
                s = s + pltpu.roll(s, k, axis=0)
                k *= 2
            cnt = jnp.minimum(rows + 1, win).astype(F32)
            d = s[MAX_WINDOW:] / cnt - v
            d_ref[:, cols] = d.astype(BF16)
            y_ref[:, cols] = jnp.dot(d.astype(BF16), w_ref[g], preferred_element_type=F32)
        prev_ref[...] = v_ref[tm - MAX_WINDOW:, :]

    return _call(
        body, name=name, grid=(Tp // tm,),
        in_specs=[pl.BlockSpec((tm, D), lambda t: (t, 0)), pl.BlockSpec((G, Cg, Cg), lambda t: (0, 0, 0))],
        out_specs=[pl.BlockSpec((tm, D), lambda t: (t, 0))] * 2,
        out_shape=[_sds((Tp, D), BF16), _sds((Tp, D), F32)],
        scratch_shapes=[pltpu.VMEM((MAX_WINDOW, D), F32)],
        compiler_params=_cp("arbitrary"),
    )(proj, pool_w)


def _lru_fwd(proj, y_pool, scale, conv_w, conv_b, wa, ba, wx, bx, lam, name, tasks=()):
    Tp = proj.shape[0]
    H, hd, _ = wa.shape
    D = H * hd
    tm = _tile(Tp, TILE["lru"])
    nb = D // hd

    def body(vl_ref, vg_ref, gp_ref, gl_ref, y_ref, sc_ref, cw_ref, cb_ref, wa_ref, ba_ref, wx_ref, bx_ref,
             lam_ref, xc_ref, r_ref, i_ref, a_ref, mult_ref, hs_ref, m_ref, prev_ref, carry_ref, b_s):
        t = pl.program_id(1)

        @pl.when(t == 0)
        def _():
            prev_ref[...] = jnp.zeros_like(prev_ref)
            carry_ref[...] = jnp.zeros_like(carry_ref)

        v = vl_ref[...]
        prev = prev_ref[...]
        xc = jnp.zeros_like(v) + cb_ref[...]
        for k in range(CONV_WIDTH):
            xc = xc + cw_ref[k:k + 1, :] * _shift_down(prev, v, CONV_WIDTH - 1 - k)
        prev_ref[...] = v[tm - HALO:, :]
        xcb = xc.astype(BF16)
        r = _sigmoid(jnp.dot(xcb, wa_ref[...], preferred_element_type=F32) + ba_ref[...])
        i = _sigmoid(jnp.dot(xcb, wx_ref[...], preferred_element_type=F32) + bx_ref[...])
        a, mult = _lru_gates(r, _softplus(-lam_ref[...]))
        a_ref[...] = a
        mult_ref[...] = mult
        b_s[...] = mult * (i * xc)
        xc_ref[...] = xc
        r_ref[...] = r
        i_ref[...] = i
        carry_ref[0:1, :] = _scan_chunks(a_ref, b_s, hs_ref, carry_ref[0:1, :], tm, reverse=False)
        lru_out = hs_ref[...] * _gelu(vg_ref[...])
        pool_out = y_ref[...] * sc_ref[...]
        m_ref[...] = (_sigmoid(gp_ref[...]) * pool_out + _sigmoid(gl_ref[...]) * lru_out).astype(BF16)

    def piece(p):
        return pl.BlockSpec((tm, hd), lambda h, t: (t, p * nb + h))

    blk = pl.BlockSpec((tm, hd), lambda h, t: (t, h))
    vec = pl.BlockSpec((1, hd), lambda h, t: (0, h))
    mat = pl.BlockSpec((None, hd, hd), lambda h, t: (h, 0, 0))
    bias = pl.BlockSpec((None, 1, hd), lambda h, t: (h, 0, 0))
    return _hosted(
        body, tasks, name=name, grid=(H, Tp // tm),
        in_specs=[piece(1), piece(2), piece(3), piece(4), blk, vec,
                  pl.BlockSpec((CONV_WIDTH, hd), lambda h, t: (0, h)), vec, mat, bias, mat, bias, vec],
        out_specs=[blk] * 7,
        out_shape=[_sds((Tp, D), F32)] * 6 + [_sds((Tp, D), BF16)],
        scratch_shapes=[pltpu.VMEM((HALO, hd), F32), pltpu.VMEM((8, hd), F32), pltpu.VMEM((tm, hd), F32)],
        semantics=("parallel", "arbitrary"),
        operands=(proj, proj, proj, proj, y_pool, scale, conv_w, conv_b, wa, ba.reshape(H, 1, hd), wx,
                  bx.reshape(H, 1, hd), lam))


def _wout_norm_fwd(merged, w_out, h0, g2, name, tasks=()):
    Tp, D = h0.shape
    tm = _tile(Tp, TILE["wout"])

    def body(m_ref, w_ref, h0_ref, g_ref, h1_ref, u2_ref, r2_ref):
        h1 = h0_ref[...] + jnp.dot(m_ref[...], w_ref[...], preferred_element_type=F32)
        r = lax.rsqrt(jnp.mean(h1 * h1, axis=-1, keepdims=True) + NORM_EPS)
        h1_ref[...] = h1
        u2_ref[...] = (h1 * r * g_ref[...]).astype(BF16)
        r2_ref[...] = r

    row = pl.BlockSpec((tm, D), lambda i: (i, 0))
    return _hosted(
        body, tasks, name=name, grid=(Tp // tm,),
        in_specs=[row, pl.BlockSpec((D, D), lambda i: (0, 0)), row, pl.BlockSpec((1, D), lambda i: (0, 0))],
        out_specs=[row, row, pl.BlockSpec((tm, 1), lambda i: (i, 0))],
        out_shape=[_sds((Tp, D), F32), _sds((Tp, D), BF16), _sds((Tp, 1), F32)],
        semantics=("parallel",), operands=(merged, w_out, h0, g2))


def _mlp2_loss(act, w2, h1, target, gf, n_meta, seq, name):
    Tp, D = h1.shape
    K = act.shape[1]
    tm = _tile(Tp, TILE["mlp2"])
    tk = min(K, MLP2_K)
    nk = K // tk

    rc = _tile(tm, EPILOGUE_ROWS)

    def body(a_ref, w_ref, h1_hbm, t_hbm, g_ref, dh_ref, dhb_ref, loss_ref, dg_ref, h1_buf, t_buf, sems):
        i, k = pl.program_id(0), pl.program_id(1)
        tile_rows = pl.ds(pl.multiple_of(i * tm, tm), tm)
        fetch = [pltpu.make_async_copy(h1_hbm.at[tile_rows, :], h1_buf, sems.at[0]),
                 pltpu.make_async_copy(t_hbm.at[tile_rows, :], t_buf, sems.at[1])]

        @pl.when(k == 0)
        def _():
            for f in fetch:
                f.start()
            dh_ref[...] = jnp.zeros_like(dh_ref)

        @pl.when((i == 0) & (k == 0))
        def _():
            loss_ref[...] = jnp.zeros_like(loss_ref)
            dg_ref[...] = jnp.zeros_like(dg_ref)

        dh_ref[...] += jnp.dot(a_ref[...], w_ref[...], preferred_element_type=F32)

        @pl.when(k == nk - 1)
        def _():
            for f in fetch:
                f.wait()
            g = g_ref[...]

            def chunk(c, carry):
                loss_acc, dg_acc = carry
                rows = pl.ds(pl.multiple_of(c * rc, rc), rc)
                h2 = h1_buf[rows, :] + dh_ref[rows, :]
                r = lax.rsqrt(jnp.mean(h2 * h2, axis=-1, keepdims=True) + NORM_EPS)
                out = h2 * r * g
                row_id = i * tm + c * rc + lax.broadcasted_iota(jnp.int32, (rc, 1), 0)
                valid = (row_id >= n_meta) & (row_id < n_meta + seq)
                diff = jnp.where(valid, out - t_buf[rows, :], 0.0)
                dout = diff / D
                dog = dout * g
                dh = r * dog - h2 * (r * r * r * jnp.mean(dog * h2, axis=-1, keepdims=True))
                dh_ref[rows, :] = dh
                dhb_ref[rows, :] = dh.astype(BF16)
                loss_acc = loss_acc + 0.5 * jnp.sum(jnp.mean(diff * diff, axis=-1, keepdims=True), axis=0, keepdims=True)
                return loss_acc, dg_acc + jnp.sum(dout * (h2 * r), axis=0, keepdims=True)

            loss_sum, dg_sum = lax.fori_loop(0, tm // rc, chunk, (jnp.zeros((1, 1), F32), jnp.zeros((1, D), F32)))
            loss_ref[...] += loss_sum
            dg_ref[...] += dg_sum

    row = pl.BlockSpec((tm, D), lambda i, k: (i, 0))
    hbm = pl.BlockSpec(memory_space=pl.ANY)
    return _call(
        body, name=name, grid=(Tp // tm, nk),
        in_specs=[pl.BlockSpec((tm, tk), lambda i, k: (i, k)), pl.BlockSpec((tk, D), lambda i, k: (k, 0)),
                  hbm, hbm, pl.BlockSpec((1, D), lambda i, k: (0, 0))],
        out_specs=[row, row, pl.BlockSpec((8, 128), lambda i, k: (0, 0)), pl.BlockSpec((1, D), lambda i, k: (0, 0))],
        out_shape=[_sds((Tp, D), F32), _sds((Tp, D), BF16), _sds((8, 128), F32), _sds((1, D), F32)],
        scratch_shapes=[pltpu.VMEM((tm, D), F32), pltpu.VMEM((tm, D), F32), pltpu.SemaphoreType.DMA((2,))],
        compiler_params=_cp("arbitrary", "arbitrary"),
    )(act, w2, h1, target, gf)


def _dact_bwd(dh2b, w2_slots, a1, name):
    Tp, D = dh2b.shape
    S, n, _ = w2_slots.shape
    tm = _tile(Tp, TILE["dact"])

    def body(g_ref, w_ref, a1_ref, o_ref):
        dact = lax.dot_general(g_ref[...], w_ref[...], _NT, preferred_element_type=F32)
        o_ref[...] = (dact * (2.0 * jnp.maximum(a1_ref[...].astype(F32), 0.0))).astype(BF16)

    return _call(
        body, name=name, grid=(S, Tp // tm),
        in_specs=[pl.BlockSpec((tm, D), lambda j, i: (i, 0)), pl.BlockSpec((None, n, D), lambda j, i: (j, 0, 0)),
                  pl.BlockSpec((tm, n), lambda j, i: (i, j))],
        out_specs=pl.BlockSpec((tm, n), lambda j, i: (i, j)),
        out_shape=_sds((Tp, S * n), BF16),
        compiler_params=_cp("parallel", "parallel"),
    )(dh2b, w2_slots, a1)


def _weight_grad(a, g, blocks, block_a, name, tasks=(), part=(0, 1)):
    Tp, Ka = a.shape
    Ng = g.shape[1]
    p, parts = part
    assert parts == 1 or not block_a
    ka = Ka // blocks if block_a else Ka
    ng = Ng if block_a else Ng // (blocks * parts)
    tt = _tile(Tp, TILE["tn"])
    nt = Tp // tt

    def body(a_ref, g_ref, o_ref, acc_ref):
        t = pl.program_id(1)

        @pl.when(t == 0)
        def _():
            acc_ref[...] = jnp.zeros_like(acc_ref)

        acc_ref[...] += lax.dot_general(a_ref[...], g_ref[...], _TN, preferred_element_type=F32)

        @pl.when(t == nt - 1)
        def _():
            o_ref[...] = acc_ref[...].astype(o_ref.dtype)

    if block_a:
        a_spec = pl.BlockSpec((tt, ka), lambda j, t: (t, j))
        g_spec = pl.BlockSpec((tt, ng), lambda j, t: (t, 0))
    else:
        a_spec = pl.BlockSpec((tt, ka), lambda j, t: (t, 0))
        g_spec = pl.BlockSpec((tt, ng), lambda j, t: (t, j * parts + p))
    (dw,), extra = _hosted(
        body, tasks, name=name, grid=(blocks, nt),
        in_specs=[a_spec, g_spec],
        out_specs=[pl.BlockSpec((None, ka, ng), lambda j, t: (j, 0, 0))],
        out_shape=[_sds((blocks, ka, ng), BF16)],
        scratch_shapes=[pltpu.VMEM((ka, ng), F32)],
        semantics=("parallel", "arbitrary"), operands=(a, g))
    return dw, extra


def _nt_norm_bwd(dz, w_parts, dres, hin, rin, g, want_bf16, name, tasks=(), tiles=None, earlier=None):
    Tp, D = hin.shape
    P = len(w_parts)
    S, _, n = w_parts[0].shape
    K = S * P
    tm = _tile(Tp, TILE["nt"])
    t0, nt = tiles if tiles is not None else (0, Tp // tm)
    assert not (want_bf16 and earlier is not None)

    rc = _tile(tm, EPILOGUE_ROWS)

    def body(dz_ref, *rest):
        w_refs, (dres_hbm, h_hbm, r_ref, g_ref), rest = rest[:P], rest[P:P + 4], rest[P + 4:]
        if earlier is not None:
            _, dg0_ref, dh_ref, dg_ref, dres_buf, h_buf, sems = rest
        elif want_bf16:
            dh_ref, dhb_ref, dg_ref, dres_buf, h_buf, sems = rest
        else:
            dh_ref, dg_ref, dres_buf, h_buf, sems = rest
        i, k = pl.program_id(0), pl.program_id(1)
        tile_rows = pl.ds(pl.multiple_of((t0 + i) * tm, tm), tm)
        fetch = [pltpu.make_async_copy(dres_hbm.at[tile_rows, :], dres_buf, sems.at[0]),
                 pltpu.make_async_copy(h_hbm.at[tile_rows, :], h_buf, sems.at[1])]

        @pl.when(k == 0)
        def _():
            for f in fetch:
                f.start()
            dh_ref[...] = jnp.zeros_like(dh_ref)

        @pl.when((i == 0) & (k == 0))
        def _():
            dg_ref[...] = jnp.zeros_like(dg_ref) if earlier is None else dg0_ref[...]

        for q in range(P):
            @pl.when(k % P == q)
            def _(q=q):
                dh_ref[...] += lax.dot_general(dz_ref[...], w_refs[q][...], _NT, preferred_element_type=F32)

        @pl.when(k == K - 1)
        def _():
            for f in fetch:
                f.wait()
            g = g_ref[...]

            def chunk(c, dg_acc):
                rows = pl.ds(pl.multiple_of(c * rc, rc), rc)
                du = dh_ref[rows, :]
                h = h_buf[rows, :]
                r = r_ref[rows, :]
                dug = du * g
                dh = dres_buf[rows, :] + r * dug - h * (r * r * r * jnp.mean(dug * h, axis=-1, keepdims=True))
                dh_ref[rows, :] = dh
                if want_bf16:
                    dhb_ref[rows, :] = dh.astype(BF16)
                return dg_acc + jnp.sum(du * (h * r), axis=0, keepdims=True)

            dg_ref[...] += lax.fori_loop(0, tm // rc, chunk, jnp.zeros((1, D), F32))

    row = pl.BlockSpec((tm, D), lambda i, k: (t0 + i, 0))
    vec = pl.BlockSpec((1, D), lambda i, k: (0, 0))
    hbm = pl.BlockSpec(memory_space=pl.ANY)
    out_specs = [row] + ([row] if want_bf16 else []) + [vec]
    out_shape = [_sds((Tp, D), F32)] + ([_sds((Tp, D), BF16)] if want_bf16 else []) + [_sds((1, D), F32)]
    in_specs = ([pl.BlockSpec((tm, n), lambda i, k: (t0 + i, k))]
                + [pl.BlockSpec((None, D, n), lambda i, k: (k // P, 0, 0))] * P
                + [hbm, hbm, pl.BlockSpec((tm, 1), lambda i, k: (t0 + i, 0)), vec])
    operands = (dz, *w_parts, dres, hin, rin, g)
    aliases = {}
    if earlier is not None:
        in_specs += [hbm, vec]
        operands += tuple(earlier)
        aliases = {P + 5: 0}
    return _hosted(
        body, tasks, name=name, grid=(nt, K), in_specs=in_specs, out_specs=out_specs, out_shape=out_shape,
        scratch_shapes=[pltpu.VMEM((tm, D), F32), pltpu.VMEM((tm, D), F32), pltpu.SemaphoreType.DMA((2,))],
        semantics=("arbitrary", "arbitrary"), operands=operands, aliases=aliases)


def _dmerged_bwd(dh1b, w_out, name):
    Tp, D = dh1b.shape
    tm = _tile(Tp, TILE["dmerged"])

    def body(g_ref, w_ref, o_ref):
        o_ref[...] = lax.dot_general(g_ref[...], w_ref[...], _NT, preferred_element_type=F32)

    row = pl.BlockSpec((tm, D), lambda i: (i, 0))
    return _call(
        body, name=name, grid=(Tp // tm,),
        in_specs=[row, pl.BlockSpec((D, D), lambda i: (0, 0))],
        out_specs=row, out_shape=_sds((Tp, D), F32),
        compiler_params=_cp("parallel"),
    )(dh1b, w_out)


def _pool_bwd(dmerged, proj, y_pool, d_pool, scale, pool_w, name, tasks=()):
    Tp, D = dmerged.shape
    G, Cg, _ = pool_w.shape
    tm = _tile(Tp, TILE["pool"])
    nt = Tp // tm

    def body(dm_ref, gp_ref, y_ref, d_ref, sc_ref, w_ref, dv_ref, dgp_ref, dw_ref, dsc_ref, next_ref):
        t = pl.program_id(0)
        tile = nt - 1 - t

        @pl.when(t == 0)
        def _():
            next_ref[...] = jnp.zeros_like(next_ref)
            dw_ref[...] = jnp.zeros_like(dw_ref)
            dsc_ref[...] = jnp.zeros_like(dsc_ref)

        rows = _row_ids(tile, tm)
        dm = dm_ref[...]
        y = y_ref[...]
        sc = sc_ref[...]
        sg = _sigmoid(gp_ref[...])
        dpo = dm * sg
        dgp_ref[...] = (dm * (y * sc) * sg * (1.0 - sg)).astype(BF16)
        dsc_ref[...] += jnp.sum(dpo * y, axis=0, keepdims=True)
        dyb = (dpo * sc).astype(BF16)
        for g, win in enumerate(POOL_WINDOWS):
            cols = slice(g * Cg, (g + 1) * Cg)
            dy = dyb[:, cols]
            dd = lax.dot_general(dy, w_ref[g], _NT, preferred_element_type=F32)
            dw_ref[g] += lax.dot_general(d_ref[:, cols], dy, _TN, preferred_element_type=F32)
            q = dd / jnp.minimum(rows + 1, win).astype(F32)
            s = jnp.concatenate([q, next_ref[:, cols]], axis=0)
            k = 1
            while k < win:
                s = s + pltpu.roll(s, s.shape[0] - k, axis=0)
                k *= 2
            dv_ref[:, cols] = (s[:tm] - dd).astype(BF16)
            next_ref[:, cols] = q[:MAX_WINDOW]

    row = pl.BlockSpec((tm, D), lambda t: (nt - 1 - t, 0))
    return _hosted(
        body, tasks, name=name, grid=(nt,),
        in_specs=[row, pl.BlockSpec((tm, D), lambda t: (nt - 1 - t, 3)), row, row,
                  pl.BlockSpec((1, D), lambda t: (0, 0)), pl.BlockSpec((G, Cg, Cg), lambda t: (0, 0, 0))],
        out_specs=[row, row, pl.BlockSpec((G, Cg, Cg), lambda t: (0, 0, 0)), pl.BlockSpec((1, D), lambda t: (0, 0))],
        out_shape=[_sds((Tp, D), BF16), _sds((Tp, D), BF16), _sds((G, Cg, Cg), F32), _sds((1, D), F32)],
        scratch_shapes=[pltpu.VMEM((MAX_WINDOW, D), F32)],
        semantics=("arbitrary",), operands=(dmerged, proj, y_pool, d_pool, scale, pool_w))


LRU_SMALL_ROWS = 8


def _lru_bwd(dmerged, proj, xc, r_gate, i_gate, a_gate, mult_gate, hs, lam, conv_w, wa, wx, name, tasks=()):
    Tp, D = dmerged.shape
    H, hd, _ = wa.shape
    tm = _tile(Tp, TILE["lru"])
    nt = Tp // tm
    nb = D // hd
    halo_blocks = tm // HALO

    def body(dm_ref, vl_ref, vg_ref, gl_ref, xc_ref, r_ref, i_ref, a_ref, mult_ref, hs_ref, hsp_ref, lam_ref, cw_ref,
             wa_ref, wx_ref, dvl_ref, dvg_ref, dgl_ref, dwa_ref, dwx_ref, small_ref,
             mu_next_ref, dxc_next_ref, q_s, mu_s):
        t = pl.program_id(1)
        tile = nt - 1 - t

        @pl.when(t == 0)
        def _():
            mu_next_ref[...] = jnp.zeros_like(mu_next_ref)
            dxc_next_ref[...] = jnp.zeros_like(dxc_next_ref)
            dwa_ref[...] = jnp.zeros_like(dwa_ref)
            dwx_ref[...] = jnp.zeros_like(dwx_ref)
            small_ref[...] = jnp.zeros_like(small_ref)

        first = tile == 0
        dm = dm_ref[...]
        hs_t = hs_ref[...]
        xc_t = xc_ref[...]
        r = r_ref[...]
        i = i_ref[...]
        lam_v = lam_ref[...]
        sp = _softplus(-lam_v)
        a = a_ref[...]
        mult = mult_ref[...]

        sg = _sigmoid(gl_ref[...])
        ge, dge = _gelu_and_grad(vg_ref[...])
        dlo = dm * sg
        dgl_ref[...] = (dm * (hs_t * ge) * sg * (1.0 - sg)).astype(BF16)
        dvg_ref[...] = (dlo * hs_t * dge).astype(BF16)
        dhs = dlo * ge

        q_s[...] = a * dhs
        mu_first = _scan_chunks(a_ref, q_s, mu_s, mu_next_ref[0:1, :], tm, reverse=True)
        lam_t = dhs + _shift_up(mu_s[...], mu_next_ref[...], 1)
        mu_next_ref[...] = jnp.broadcast_to(mu_first, mu_next_ref.shape)

        h_prev = _shift_down(jnp.where(first, 0.0, hsp_ref[...]), hs_t, 1)
        da = lam_t * h_prev
        dmult = lam_t * (i * xc_t)
        di = lam_t * mult * xc_t
        dxc = lam_t * mult * i
        dlog_a = da * a - dmult * (a * a) / mult
        dr = dlog_a * (-LRU_C * sp)
        dlam_rows = dlog_a * (-LRU_C * r)
        dza = dr * r * (1.0 - r)
        dzx = di * i * (1.0 - i)
        dzab, dzxb = dza.astype(BF16), dzx.astype(BF16)
        xcb = xc_t.astype(BF16)
        dxc = dxc + lax.dot_general(dzab, wa_ref[...], _NT, preferred_element_type=F32)
        dxc = dxc + lax.dot_general(dzxb, wx_ref[...], _NT, preferred_element_type=F32)
        dwa_ref[...] += lax.dot_general(xcb, dzab, _TN, preferred_element_type=F32)
        dwx_ref[...] += lax.dot_general(xcb, dzxb, _TN, preferred_element_type=F32)

        dxc_next = dxc_next_ref[...]
        taps = [_shift_up(dxc, dxc_next, CONV_WIDTH - 1 - k) for k in range(CONV_WIDTH)]
        dv = jnp.zeros_like(dxc)
        for k in range(CONV_WIDTH):
            dv = dv + cw_ref[k:k + 1, :] * taps[k]
        dvl_ref[...] = dv.astype(BF16)
        dxc_next_ref[...] = dxc[:HALO, :]

        v_t = vl_ref[...]
        small = [jnp.sum(dza, axis=0, keepdims=True), jnp.sum(dzx, axis=0, keepdims=True),
                 jnp.sum(dlam_rows, axis=0, keepdims=True) * (-_sigmoid(-lam_v)),
                 jnp.sum(dxc, axis=0, keepdims=True)]
        for k in range(CONV_WIDTH):
            small.append(jnp.sum(taps[k] * v_t, axis=0, keepdims=True))
        for k, row in enumerate(small):
            small_ref[k:k + 1, :] += row

    def piece(p):
        return pl.BlockSpec((tm, hd), lambda h, t: (nt - 1 - t, p * nb + h))

    def halo(p):
        return pl.BlockSpec((HALO, hd), lambda h, t: (jnp.maximum((nt - 1 - t) * halo_blocks - 1, 0), p * nb + h))

    blk = pl.BlockSpec((tm, hd), lambda h, t: (nt - 1 - t, h))
    vec = pl.BlockSpec((1, hd), lambda h, t: (0, h))
    mat = pl.BlockSpec((None, hd, hd), lambda h, t: (h, 0, 0))
    return _hosted(
        body, tasks, name=name, grid=(H, nt),
        in_specs=[blk, piece(1), piece(2), piece(4), blk, blk, blk, blk, blk, blk, halo(0), vec,
                  pl.BlockSpec((CONV_WIDTH, hd), lambda h, t: (0, h)), mat, mat],
        out_specs=[blk, blk, blk, mat, mat, pl.BlockSpec((None, LRU_SMALL_ROWS, hd), lambda h, t: (h, 0, 0))],
        out_shape=[_sds((Tp, D), BF16)] * 3 + [_sds((H, hd, hd), F32)] * 2 + [_sds((H, LRU_SMALL_ROWS, hd), F32)],
        scratch_shapes=[pltpu.VMEM((HALO, hd), F32), pltpu.VMEM((HALO, hd), F32),
                        pltpu.VMEM((tm, hd), F32), pltpu.VMEM((tm, hd), F32)],
        semantics=("parallel", "arbitrary"),
        operands=(dmerged, proj, proj, proj, xc, r_gate, i_gate, a_gate, mult_gate, hs, hs, lam, conv_w, wa, wx))


def _adamw(w, g, m, v):
    m = ADAM_B1 * m + (1.0 - ADAM_B1) * g
    v = ADAM_B2 * v + (1.0 - ADAM_B2) * (g * g)
    m_hat = m / (1.0 - ADAM_B1 ** ADAM_STEP)
    v_hat = v / (1.0 - ADAM_B2 ** ADAM_STEP)
    delta = -ADAM_LR * (m_hat / (jnp.sqrt(v_hat) + ADAM_EPS) + ADAM_WD * w)
    return delta, m, v


def _reduce_update(pair_sums, chip_sums, w, m, v, chip_slot, name, part=(0, 1), earlier=None):
    R, C = pair_sums.shape[1:]
    p, parts = part
    tr = _tile(R, TILE["update"])

    def body(slot_ref, own_ref, got_ref, w_ref, m_ref, v_ref, *rest):
        g_out, d_out, m_out, v_out = rest[-4:]
        g = own_ref[...].astype(F32)
        for k in range(3):
            g = g + got_ref[k].astype(F32)
        d, m_new, v_new = _adamw(w_ref[...], g, m_ref[...], v_ref[...])
        g_out[...] = g
        d_out[...] = d
        m_out[...] = m_new
        v_out[...] = v_new

    blk = pl.BlockSpec((tr, C), lambda i, s: (i, p))
    in_specs = [pl.BlockSpec((None, tr, C), lambda i, s: (s[0], i, 0)),
                pl.BlockSpec((3, tr, C), lambda i, s: (0, i, 0)), blk, blk, blk]
    operands = (chip_slot, pair_sums, chip_sums, w, m, v)
    aliases = {}
    if earlier is not None:
        in_specs += [pl.BlockSpec(memory_space=pl.ANY)] * 4
        operands += tuple(earlier)
        aliases = {6 + k: k for k in range(4)}
    return _call(
        body, name=name,
        grid_spec=pltpu.PrefetchScalarGridSpec(
            num_scalar_prefetch=1, grid=(R // tr,), in_specs=in_specs, out_specs=[blk] * 4),
        out_shape=[_sds((R, parts * C), F32)] * 4,
        input_output_aliases=aliases,
        compiler_params=_cp("parallel"),
    )(*operands)


def _small_update(w, g, m, v, name):
    def body(w_ref, g_ref, m_ref, v_ref, d_out, m_out, v_out):
        d, m_new, v_new = _adamw(w_ref[...], g_ref[...], m_ref[...], v_ref[...])
        d_out[...] = d
        m_out[...] = m_new
        v_out[...] = v_new

    return _call(body, name=name, out_shape=[_sds(w.shape, F32)] * 3)(w, g, m, v)


def _slots_from_rows(full, lead):
    L, R, C = full.shape
    r = R // N_DEV
    return full.reshape(L, N_DEV, r, C).transpose(1, 0, 2, 3).reshape(N_DEV, L * r, C)


def _rows_from_slots(slots, lead):
    _, LR, C = slots.shape
    r = LR // lead
    return slots.reshape(N_DEV, lead, r, C).transpose(1, 0, 2, 3).reshape(lead, N_DEV * r, C)


def kernel(x, meta_tokens, norm1_g, w_in, pool_w, pool_scale, conv_w, conv_b, gate_a_w, gate_a_b, gate_x_w, gate_x_b, lru_lambda, w_out, norm2_g, mlp_w1, mlp_w2, final_g, loss_target, m_meta_tokens, m_norm1_g, m_w_in, m_pool_w, m_pool_scale, m_conv_w, m_conv_b, m_gate_a_w, m_gate_a_b, m_gate_x_w, m_gate_x_b, m_lru_lambda, m_w_out, m_norm2_g, m_mlp_w1, m_mlp_w2, m_final_g, v_meta_tokens, v_norm1_g, v_w_in, v_pool_w, v_pool_scale, v_conv_w, v_conv_b, v_gate_a_w, v_gate_a_b, v_gate_x_w, v_gate_x_b, v_lru_lambda, v_w_out, v_norm2_g, v_mlp_w1, v_mlp_w2, v_final_g):
    seq, D = x.shape[1], x.shape[2]
    n_meta = meta_tokens.shape[0]
    G, Cg = pool_w.shape[1], pool_w.shape[3]
    H, hd = gate_a_w.shape[1], gate_a_w.shape[3]
    T = n_meta + seq
    Tp = -(-T // ROW_ALIGN) * ROW_ALIGN
    ix, iy, ic = _pos()
    me = 4 * ix + 2 * iy + ic
    core = jnp.reshape(ic, (1,)).astype(jnp.int32)
    chip_slot = jnp.reshape(2 * ix + iy, (1,)).astype(jnp.int32)

    w_in_l, w1_l, w2_l, w_out_l = w_in[0], mlp_w1[0], mlp_w2[0], w_out[0]
    half = w_in_l.shape[1] // 2
    pool_l = pool_w[0].reshape(G * (Cg // N_DEV), Cg)
    wa_l = gate_a_w[0].reshape(H * (hd // N_DEV), hd)
    wx_l = gate_x_w[0].reshape(H * (hd // N_DEV), hd)
    small_params = jnp.concatenate(
        [meta_tokens, conv_w[0], jnp.zeros((4, D // N_DEV), F32)], axis=0)
    biases = jnp.concatenate([gate_a_b[0], gate_x_b[0]], axis=0)
    (w_in_ga, pool_g, wa_g, wx_g, small_g, bias_g) = _all_gather(
        [w_in_l[:, :half].astype(BF16), pool_l.astype(BF16), wa_l.astype(BF16), wx_l.astype(BF16), small_params,
         biases], "gather_first")
    pool_full = _rows_from_slots(pool_g, G)
    wa_full = _rows_from_slots(wa_g, H)
    wx_full = _rows_from_slots(wx_g, H)
    small_full = small_g.transpose(1, 0, 2).reshape(n_meta + 8, D)
    meta_full = small_full[:n_meta]
    conv_full = small_full[n_meta:n_meta + CONV_WIDTH]
    bias_full = bias_g.transpose(1, 0, 2).reshape(2 * H, hd)
    ba_full, bx_full = bias_full[:H], bias_full[H:]

    h0 = jnp.concatenate([meta_full, x[0], jnp.zeros((Tp - T, D), F32)], axis=0)
    target = jnp.concatenate([jnp.zeros((n_meta, D), F32), loss_target[0], jnp.zeros((Tp - T, D), F32)], axis=0)
    u, r1 = _norm_fwd(h0, norm1_g, "norm1")
    proj_a, ((w_in_gb,),) = _proj_fwd(u, w_in_ga, "proj_a", tasks=[_AgFull(w_in_l[:, half:].astype(BF16))], part=(0, 2))
    proj, ((w_out_g,),) = _proj_fwd(u, w_in_gb, "proj_b", tasks=[_AgFull(w_out_l.astype(BF16))], part=(1, 2),
                                    earlier=proj_a)
    w_in_parts = [w_in_ga, w_in_gb]
    d_pool, y_pool = _pool_fwd(proj, pool_full, "pool_fwd")
    (xc, r_gate, i_gate, a_gate, mult_gate, hs, merged), ((w1_g,),) = _lru_fwd(
        proj, y_pool, pool_scale, conv_full, conv_b, wa_full, ba_full, wx_full, bx_full, lru_lambda, "lru_fwd",
        tasks=[_AgFull(w1_l.astype(BF16))])
    w_out_full = w_out_g.reshape(D, D)
    (h1, u2, r2), _ = _wout_norm_fwd(merged, w_out_full, h0, norm2_g, "wout_norm2")
    (act, a1), ((w2_g,),) = _mlp1_fwd(u2, w1_g, "mlp1", tasks=[_AgFull(w2_l.astype(BF16))])
    dh2, dh2b, loss_tile, d_final_g = _mlp2_loss(
        act, w2_g.reshape(-1, D), h1, target, final_g.reshape(1, D), n_meta, seq, "mlp2_loss")

    def pair(part, got, tag):
        return _pair_sum(part, got, core, "pair_sum_" + tag)

    d_a1 = _dact_bwd(dh2b, w2_g, a1, "dact")
    dw2_p, _ = _weight_grad(act, dh2b, N_DEV, True, "dw2")
    dw1_p, ((dw2_got,),) = _weight_grad(u2, d_a1, N_DEV, False, "dw1", tasks=[_RsSibling(dw2_p)])
    dw2_pair = pair(dw2_p, dw2_got, "w2")
    (dh1, dh1b, d_norm2_g), ((dw2_chips,), (dw1_got,)) = _nt_norm_bwd(
        d_a1, [w1_g], dh2, h1, r2, norm2_g, True, "du2_norm2", tasks=[_RsChips(dw2_pair), _RsSibling(dw1_p)])
    dw1_pair = pair(dw1_p, dw1_got, "w1")
    dmerged = _dmerged_bwd(dh1b, w_out_full, "dmerged")
    dwout_p, _ = _weight_grad(merged, dh1b, 2, True, "dwout")
    dwout_p = dwout_p.reshape(N_DEV, D // N_DEV, D)
    (d_vpool, d_gpool, dpool_full, d_scale), ((dwout_got,),) = _pool_bwd(
        dmerged, proj, y_pool, d_pool, pool_scale, pool_full, "pool_bwd", tasks=[_RsSibling(dwout_p)])
    dwout_pair = pair(dwout_p, dwout_got, "wout")
    (d_vlru, d_vgelu, d_glru, dwa_full, dwx_full, lru_small), ((dw1_chips,), (dwout_chips,)) = _lru_bwd(
        dmerged, proj, xc, r_gate, i_gate, a_gate, mult_gate, hs, lru_lambda, conv_full, wa_full, wx_full, "lru_bwd",
        tasks=[_RsChips(dw1_pair), _RsChips(dwout_pair)])
    dproj = jnp.concatenate([d_vpool, d_vlru, d_vgelu, d_gpool, d_glru], axis=1)
    dwin_a, _ = _weight_grad(u, dproj, N_DEV, False, "dwin_a", part=(0, 2))
    dwin_b, ((dwin_a_got,),) = _weight_grad(u, dproj, N_DEV, False, "dwin_b", part=(1, 2), tasks=[_RsSibling(dwin_a)])
    dwin_a_pair = pair(dwin_a, dwin_a_got, "win_a")
    dpool_p = _slots_from_rows(dpool_full, G).astype(BF16)
    dwa_p = _slots_from_rows(dwa_full, H).astype(BF16)
    dwx_p = _slots_from_rows(dwx_full, H).astype(BF16)
    late = [dwin_b, dpool_p, dwa_p, dwx_p]
    n_tiles = Tp // _tile(Tp, TILE["nt"])
    n_first = max(1, n_tiles // 2)
    (dh0_a, d_norm1_a), ((dwin_a_chips,), *late_got) = _nt_norm_bwd(
        dproj, w_in_parts, dh1, h0, r1, norm1_g, False, "du_norm1_a",
        tasks=[_RsChips(dwin_a_pair)] + [_RsSibling(p) for p in late], tiles=(0, n_first))
    late_pair = [pair(p, g[0], "late%d" % k) for k, (p, g) in enumerate(zip(late, late_got))]
    (dh0, d_norm1_g), late_chips = _nt_norm_bwd(
        dproj, w_in_parts, dh1, h0, r1, norm1_g, False, "du_norm1_b", tasks=[_RsChips(p) for p in late_pair],
        tiles=(n_first, n_tiles - n_first), earlier=(dh0_a, d_norm1_a))
    grad_x = dh0[n_meta:T][None]

    pair_sums = [dw2_pair, dw1_pair, dwout_pair] + late_pair[1:]
    chip_sums = [dw2_chips, dw1_chips, dwout_chips] + [c[0] for c in late_chips[1:]]
    big = {}
    names = ["mlp_w2", "mlp_w1", "w_out", "pool_w", "gate_a_w", "gate_x_w"]
    trip = {"mlp_w2": (mlp_w2, m_mlp_w2, v_mlp_w2), "mlp_w1": (mlp_w1, m_mlp_w1, v_mlp_w1),
            "w_out": (w_out, m_w_out, v_w_out),
            "pool_w": (pool_w, m_pool_w, v_pool_w), "gate_a_w": (gate_a_w, m_gate_a_w, v_gate_a_w),
            "gate_x_w": (gate_x_w, m_gate_x_w, v_gate_x_w)}
    for k, nm in enumerate(names):
        w_, m_, v_ = trip[nm]
        shape2 = pair_sums[k].shape[1:]
        outs = _reduce_update(pair_sums[k], chip_sums[k], w_.reshape(shape2), m_.reshape(shape2), v_.reshape(shape2),
                              chip_slot, "update_" + nm)
        big[nm] = [o.reshape(w_.shape) for o in outs]
    win2 = [a_[0] for a_ in (w_in, m_w_in, v_w_in)]
    win_a = _reduce_update(dwin_a_pair, dwin_a_chips, *win2, chip_slot, "update_w_in_a", part=(0, 2))
    win_b = _reduce_update(late_pair[0], late_chips[0][0], *win2, chip_slot, "update_w_in_b", part=(1, 2), earlier=win_a)
    big["w_in"] = [o.reshape(w_in.shape) for o in win_b]
    names = names + ["w_in"]

    lru_rows = lru_small.transpose(1, 0, 2).reshape(LRU_SMALL_ROWS, D)
    small_part = jnp.concatenate(
        [dh0[:n_meta], d_norm1_g, d_scale, d_norm2_g, d_final_g, lru_rows, jnp.zeros((4, D), F32)], axis=0)
    (small_all,) = _all_gather([small_part], "gather_small_grads")
    small_sum = _sum_slots(small_all, "sum_small_grads")
    o = n_meta
    g_meta_full = small_sum[:o]
    g_norm1, g_scale, g_norm2, g_final = (small_sum[o + k:o + k + 1] for k in range(4))
    g_ba_full, g_bx_full, g_lam, g_cb = (small_sum[o + 4 + k:o + 5 + k] for k in range(4))
    g_cw_full = small_sum[o + 8:o + 8 + CONV_WIDTH]
    dcol = D // N_DEV
    g_meta = lax.dynamic_slice_in_dim(g_meta_full, me * dcol, dcol, axis=1)
    g_cw = lax.dynamic_slice_in_dim(g_cw_full, me * dcol, dcol, axis=1)
    hcol = hd // N_DEV
    g_ba = lax.dynamic_slice_in_dim(g_ba_full.reshape(H, hd), me * hcol, hcol, axis=1)
    g_bx = lax.dynamic_slice_in_dim(g_bx_full.reshape(H, hd), me * hcol, hcol, axis=1)

    rep_w = jnp.concatenate([norm1_g, pool_scale, conv_b, lru_lambda, norm2_g, final_g.reshape(1, D)], axis=0)
    rep_g = jnp.concatenate([g_norm1, g_scale, g_cb, g_lam, g_norm2, g_final], axis=0)
    rep_m = jnp.concatenate([m_norm1_g, m_pool_scale, m_conv_b, m_lru_lambda, m_norm2_g, m_final_g.reshape(1, D)], axis=0)
    rep_v = jnp.concatenate([v_norm1_g, v_pool_scale, v_conv_b, v_lru_lambda, v_norm2_g, v_final_g.reshape(1, D)], axis=0)
    rep_d, rep_nm, rep_nv = _small_update(rep_w, rep_g, rep_m, rep_v, "update_vectors")
    col_w = jnp.concatenate([meta_tokens, conv_w[0]], axis=0)
    col_g = jnp.concatenate([g_meta, g_cw], axis=0)
    col_m = jnp.concatenate([m_meta_tokens, m_conv_w[0]], axis=0)
    col_v = jnp.concatenate([v_meta_tokens, v_conv_w[0]], axis=0)
    col_d, col_nm, col_nv = _small_update(col_w, col_g, col_m, col_v, "update_columns")
    b_w = jnp.concatenate([gate_a_b[0], gate_x_b[0]], axis=0)
    b_g = jnp.concatenate([g_ba, g_bx], axis=0)
    b_m = jnp.concatenate([m_gate_a_b[0], m_gate_x_b[0]], axis=0)
    b_v = jnp.concatenate([v_gate_a_b[0], v_gate_x_b[0]], axis=0)
    b_d, b_nm, b_nv = _small_update(b_w, b_g, b_m, b_v, "update_biases")

    def rep(arr, k, like):
        return arr[k:k + 1].reshape(like.shape)

    rep_order = {"norm1_g": 0, "pool_scale": 1, "conv_b": 2, "lru_lambda": 3, "norm2_g": 4, "final_g": 5}
    like = {"norm1_g": norm1_g, "pool_scale": pool_scale, "conv_b": conv_b, "lru_lambda": lru_lambda,
            "norm2_g": norm2_g, "final_g": final_g}

    def leaves(kind):
        rep_src = [rep_g, rep_d, rep_nm, rep_nv][kind]
        col_src = [col_g, col_d, col_nm, col_nv][kind]
        b_src = [b_g, b_d, b_nm, b_nv][kind]
        out = {}
        out["meta_tokens"] = col_src[:n_meta]
        out["conv_w"] = col_src[n_meta:][None]
        out["gate_a_b"] = b_src[:H][None]
        out["gate_x_b"] = b_src[H:][None]
        for nm, k in rep_order.items():
            out[nm] = rep(rep_src, k, like[nm])
        for nm in names:
            out[nm] = big[nm][kind]
        order = ["meta_tokens", "norm1_g", "w_in", "pool_w", "pool_scale", "conv_w", "conv_b", "gate_a_w", "gate_a_b",
                 "gate_x_w", "gate_x_b", "lru_lambda", "w_out", "norm2_g", "mlp_w1", "mlp_w2", "final_g"]
        return [out[nm] for nm in order]

    loss = lax.psum(loss_tile[0, 0], ("x", "y", "c"))
    return (loss, grad_x, *leaves(0), *leaves(1), *leaves(2), *leaves(3))
```

```python
import functools

import jax
import jax.numpy as jnp
from jax import lax
from jax.experimental import pallas as pl
from jax.experimental.pallas import tpu as pltpu

F32 = jnp.float32
BF16 = jnp.bfloat16
MESH = pl.DeviceIdType.MESH
N_DEV = 8
POOL_WINDOWS = (2, 4, 8, 16)
MAX_WINDOW = 16
CONV_WIDTH = 4
HALO = 8
LRU_C = 8.0
NORM_EPS = 1e-6
ADAM_LR, ADAM_B1, ADAM_B2, ADAM_EPS, ADAM_WD, ADAM_STEP = 0.001, 0.9, 0.999, 1e-08, 0.01, 10
ROW_ALIGN = 128
VMEM_LIMIT = 56 << 20
TILE = dict(norm=384, proj=704, pool=384, lru=384, wout=384, mlp1=704, mlp2=704, dact=704, tn=1408,
            nt=704, dmerged=384, update=256, pair=1024)
MLP2_K = 1024
EPILOGUE_ROWS = 176
MID_STEP_PERCENT = 88

_NT = (((1,), (1,)), ((), ()))
_TN = (((0,), (0,)), ((), ()))


def _call(body, **kw):
    return pl.pallas_call(body, **kw)


def _cp(*sem):
    return pltpu.CompilerParams(dimension_semantics=sem, vmem_limit_bytes=VMEM_LIMIT)


def _tile(total, pref):
    best = None
    for t in range(16, min(total, pref) + 1, 16):
        if total % t == 0:
            best = t
    assert best is not None, (total, pref)
    return best


def _sds(shape, dtype):
    return jax.ShapeDtypeStruct(shape, dtype)


def _pos():
    return lax.axis_index("x"), lax.axis_index("y"), lax.axis_index("c")


def _all_gather(shards, name):
    n = len(shards)

    def body(*refs):
        ins, outs = refs[:n], refs[n:2 * n]
        send_sems, recv_sems, local_sems = refs[2 * n:]
        x, y, c = _pos()
        me, sib = (x, y, c), (x, y, 1 - c)
        chips = [(1 - x, y), (x, 1 - y), (1 - x, 1 - y)]

        def slot(p):
            return 4 * p[0] + 2 * p[1] + p[2]

        def copy(a, k, block, to, src=None):
            dst = outs[a].at[slot(block)]
            return pltpu.make_async_remote_copy(
                src_ref=dst if src is None else src, dst_ref=dst,
                send_sem=send_sems.at[7 * a + k], recv_sem=recv_sems.at[7 * a + k],
                device_id=to, device_id_type=MESH)

        mine = [pltpu.make_async_copy(ins[a], outs[a].at[slot(me)], local_sems.at[a]) for a in range(n)]
        for m in mine:
            m.start()
        first = []
        for a in range(n):
            first.append(copy(a, 0, me, sib, src=ins[a]))
            first += [copy(a, 1 + j, me, (*chip, c), src=ins[a]) for j, chip in enumerate(chips)]
        for cp in first:
            cp.start()
        passed = []
        for a in range(n):
            for j, chip in enumerate(chips):
                copy(a, 1 + j, (*chip, c), me).wait_recv()
                fwd = copy(a, 4 + j, (*chip, c), sib)
                fwd.start()
                passed.append(fwd)
        for a in range(n):
            copy(a, 0, sib, me).wait_recv()
            for j, chip in enumerate(chips):
                copy(a, 4 + j, (*chip, 1 - c), me).wait_recv()
        for cp in first + passed:
            cp.wait_send()
        for m in mine:
            m.wait()

    hbm = pl.BlockSpec(memory_space=pl.ANY)
    return _call(
        body, name=name,
        out_shape=[_sds((N_DEV,) + s.shape, s.dtype) for s in shards],
        in_specs=[hbm] * n, out_specs=[hbm] * n,
        scratch_shapes=[pltpu.SemaphoreType.DMA((7 * n,)), pltpu.SemaphoreType.DMA((7 * n,)),
                        pltpu.SemaphoreType.DMA((n,))],
    )(*shards)


def _other_chips(x, y):
    return [(1 - x, y), (x, 1 - y), (1 - x, 1 - y)]


class _AgFull:
    n_sem, n_local = 7, 1

    def __init__(self, shard):
        self.ins = [shard]
        self.out_shapes = [_sds((N_DEV,) + shard.shape, shard.dtype)]
        self.aliases = []

    def _peers(self):
        x, y, c = _pos()
        return [(x, y, 1 - c)] + [(*chip, c) for chip in _other_chips(x, y)]

    def _sends(self, ins, outs, sems):
        send, recv, _, base, _ = sems
        x, y, c = _pos()
        mine = outs[0].at[4 * x + 2 * y + c]
        return [pltpu.make_async_remote_copy(src_ref=ins[0], dst_ref=mine, send_sem=send.at[base + k],
                                             recv_sem=recv.at[base + k], device_id=p, device_id_type=MESH)
                for k, p in enumerate(self._peers())]

    def _arrivals(self, outs, sems):
        send, recv, _, base, _ = sems
        res = []
        for k, p in enumerate(self._peers()):
            blk = outs[0].at[4 * p[0] + 2 * p[1] + p[2]]
            res.append(pltpu.make_async_remote_copy(src_ref=blk, dst_ref=blk, send_sem=send.at[base + k],
                                                    recv_sem=recv.at[base + k], device_id=p, device_id_type=MESH))
        return res

    def _own(self, ins, outs, sems):
        x, y, c = _pos()
        return pltpu.make_async_copy(ins[0], outs[0].at[4 * x + 2 * y + c], sems[2].at[sems[4]])

    def _forwards(self, outs, sems, core_of_block):
        send, recv, _, base, _ = sems
        x, y, c = _pos()
        res = []
        for k, chip in enumerate(_other_chips(x, y)):
            blk = outs[0].at[4 * chip[0] + 2 * chip[1] + (c if core_of_block == "mine" else 1 - c)]
            res.append(pltpu.make_async_remote_copy(src_ref=blk, dst_ref=blk, send_sem=send.at[base + 4 + k],
                                                    recv_sem=recv.at[base + 4 + k], device_id=(x, y, 1 - c),
                                                    device_id_type=MESH))
        return res

    def start(self, ins, outs, sems):
        self._own(ins, outs, sems).start()
        for cp in self._sends(ins, outs, sems):
            cp.start()

    def mid(self, ins, outs, sems):
        for cp in self._arrivals(outs, sems)[1:]:
            cp.wait_recv()
        for cp in self._forwards(outs, sems, "mine"):
            cp.start()

    def finish(self, ins, outs, sems):
        self._arrivals(outs, sems)[0].wait_recv()
        for cp in self._forwards(outs, sems, "sibling"):
            cp.wait_recv()
        for cp in self._sends(ins, outs, sems) + self._forwards(outs, sems, "mine"):
            cp.wait_send()
        self._own(ins, outs, sems).wait()


class _RsSibling:
    n_sem, n_local = 4, 0

    def __init__(self, part):
        self.ins = [part]
        self.out_shapes = [_sds((4,) + part.shape[1:], part.dtype)]
        self.aliases = []

    def _copies(self, ins, outs, sems):
        send, recv, _, base, _ = sems
        x, y, c = _pos()
        return [pltpu.make_async_remote_copy(src_ref=ins[0].at[2 * q + (1 - c)], dst_ref=outs[0].at[q],
                                             send_sem=send.at[base + q], recv_sem=recv.at[base + q],
                                             device_id=(x, y, 1 - c), device_id_type=MESH) for q in range(4)]

    def start(self, ins, outs, sems):
        for cp in self._copies(ins, outs, sems):
            cp.start()

    def finish(self, ins, outs, sems):
        for cp in self._copies(ins, outs, sems):
            cp.wait()


class _RsChips:
    n_sem, n_local = 3, 0

    def __init__(self, pair):
        self.ins = [pair]
        self.out_shapes = [_sds((3,) + pair.shape[1:], pair.dtype)]
        self.aliases = []

    def _copies(self, ins, outs, sems):
        send, recv, _, base, _ = sems
        x, y, c = _pos()
        return [pltpu.make_async_remote_copy(src_ref=ins[0].at[2 * chip[0] + chip[1]], dst_ref=outs[0].at[k],
                                             send_sem=send.at[base + k], recv_sem=recv.at[base + k],
                                             device_id=(*chip, c), device_id_type=MESH)
                for k, chip in enumerate(_other_chips(x, y))]

    def start(self, ins, outs, sems):
        for cp in self._copies(ins, outs, sems):
            cp.start()

    def finish(self, ins, outs, sems):
        for cp in self._copies(ins, outs, sems):
            cp.wait()


def _hosted(body, tasks, *, grid, in_specs, out_specs, out_shape, scratch_shapes=(), name, semantics, operands,
            aliases=None):
    in_specs, out_specs, out_shape = list(in_specs), list(out_specs), list(out_shape)
    scratch_shapes = list(scratch_shapes)
    aliases = dict(aliases or {})
    if not tasks:
        res = _call(body, name=name, grid=grid, in_specs=in_specs, out_specs=out_specs, out_shape=out_shape,
                    scratch_shapes=scratch_shapes, input_output_aliases=aliases,
                    compiler_params=_cp(*semantics))(*operands)
        return list(res), []
    n_in, n_out, n_scr = len(in_specs), len(out_specs), len(scratch_shapes)
    t_ins = [a for t in tasks for a in t.ins]
    t_outs = [o for t in tasks for o in t.out_shapes]
    i0, o0 = n_in, n_out
    for t in tasks:
        for (i, o) in t.aliases:
            aliases[i0 + i] = o0 + o
        i0 += len(t.ins)
        o0 += len(t.out_shapes)
    n_sem = sum(t.n_sem for t in tasks)
    n_local = max(1, sum(t.n_local for t in tasks))
    n_steps = 1
    for g in grid:
        n_steps *= g
    mid_step = min(n_steps - 1, (n_steps * MID_STEP_PERCENT) // 100)

    def wrapped(*refs):
        cut = [n_in, len(t_ins), n_out, len(t_outs), n_scr]
        parts, at = [], 0
        for n in cut:
            parts.append(refs[at:at + n])
            at += n
        ins, tin, outs, tout, scratch = parts
        send, recv, local = refs[at:]
        step = pl.program_id(0)
        for d in range(1, len(grid)):
            step = step * grid[d] + pl.program_id(d)

        def each(method):
            i, o, s, l = 0, 0, 0, 0
            for t in tasks:
                if hasattr(t, method):
                    getattr(t, method)(tin[i:i + len(t.ins)], tout[o:o + len(t.out_shapes)], (send, recv, local, s, l))
                i, o, s, l = i + len(t.ins), o + len(t.out_shapes), s + t.n_sem, l + t.n_local

        @pl.when(step == 0)
        def _():
            each("start")

        body(*ins, *outs, *scratch)

        @pl.when(step == mid_step)
        def _():
            each("mid")

        @pl.when(step == n_steps - 1)
        def _():
            each("finish")

    hbm = pl.BlockSpec(memory_space=pl.ANY)
    res = _call(
        wrapped, name=name, grid=grid,
        in_specs=in_specs + [hbm] * len(t_ins), out_specs=out_specs + [hbm] * len(t_outs),
        out_shape=out_shape + t_outs,
        scratch_shapes=scratch_shapes + [pltpu.SemaphoreType.DMA((n_sem,)), pltpu.SemaphoreType.DMA((n_sem,)),
                                         pltpu.SemaphoreType.DMA((n_local,))],
        input_output_aliases=aliases,
        compiler_params=_cp(*(["arbitrary"] * len(grid))),
    )(*operands, *t_ins)
    res = list(res)
    task_outs, o = [], n_out
    for t in tasks:
        task_outs.append(res[o:o + len(t.out_shapes)])
        o += len(t.out_shapes)
    return res[:n_out], task_outs


def _pair_sum(part, got, core, name):
    _, R, C = part.shape
    tr = _tile(R, TILE["pair"]) if R % 16 == 0 else R

    def body(core_ref, p_ref, g_ref, o_ref):
        o_ref[...] = (p_ref[...].astype(F32) + g_ref[...].astype(F32)).astype(o_ref.dtype)

    return _call(
        body, name=name,
        grid_spec=pltpu.PrefetchScalarGridSpec(
            num_scalar_prefetch=1, grid=(4, R // tr),
            in_specs=[pl.BlockSpec((None, tr, C), lambda q, i, cr: (2 * q + cr[0], i, 0)),
                      pl.BlockSpec((None, tr, C), lambda q, i, cr: (q, i, 0))],
            out_specs=pl.BlockSpec((None, tr, C), lambda q, i, cr: (q, i, 0))),
        out_shape=_sds((4, R, C), part.dtype),
        compiler_params=_cp("parallel", "parallel"),
    )(core, part, got)


def _sum_slots(gathered, name):
    _, R, C = gathered.shape

    def body(g_ref, o_ref):
        acc = g_ref[0]
        for s in range(1, N_DEV):
            acc = acc + g_ref[s]
        o_ref[...] = acc

    return _call(body, name=name, out_shape=_sds((R, C), F32))(gathered)


def _sigmoid(z):
    return jax.nn.sigmoid(z)


def _softplus(z):
    e = jnp.exp(-jnp.abs(z))
    log1p_e = jnp.where(e < 0.01, e * (1.0 - e * (0.5 - e * (1.0 / 3.0))), jnp.log(1.0 + e))
    return jnp.maximum(z, 0.0) + log1p_e


_GELU_K = 0.7978845608028654
_GELU_C = 0.044715


def _gelu_and_grad(z):
    t = jnp.tanh(_GELU_K * (z + _GELU_C * z * z * z))
    g = 0.5 * z * (1.0 + t)
    dg = 0.5 * (1.0 + t) + 0.5 * z * (1.0 - t * t) * _GELU_K * (1.0 + 3.0 * _GELU_C * z * z)
    return g, dg


def _gelu(z):
    t = jnp.tanh(_GELU_K * (z + _GELU_C * z * z * z))
    return 0.5 * z * (1.0 + t)


def _row_ids(tile_index, tm, width=1):
    return tile_index * tm + lax.broadcasted_iota(jnp.int32, (tm, width), 0)


def _shift_down(prev, cur, k):
    if k == 0:
        return cur
    ext = jnp.concatenate([prev, cur], axis=0)
    return pltpu.roll(ext, k, axis=0)[prev.shape[0]:]


def _shift_up(cur, nxt, k):
    if k == 0:
        return cur
    ext = jnp.concatenate([cur, nxt], axis=0)
    return pltpu.roll(ext, ext.shape[0] - k, axis=0)[:cur.shape[0]]


def _lru_gates(r, sp):
    z = LRU_C * r * sp
    a = jnp.exp(-z)
    t = jnp.tanh(z)
    mult = jnp.sqrt(2.0 * t / (1.0 + t))
    return a, mult


def _scan_chunks(a_ref, b_ref, out_ref, carry, n_rows, reverse):
    n_chunks = n_rows // 8
    cols = a_ref.shape[1]
    rid = lax.broadcasted_iota(jnp.int32, (8, cols), 0)
    edge = 0 if reverse else 7

    def chunk(k, h):
        ci = (n_chunks - 1 - k) if reverse else k
        rows = pl.ds(pl.multiple_of(ci * 8, 8), 8)
        a = a_ref[rows, :]
        b = b_ref[rows, :]
        for s in (1, 2, 4):
            if reverse:
                keep = rid < 8 - s
                a_n, b_n = pltpu.roll(a, 8 - s, axis=0), pltpu.roll(b, 8 - s, axis=0)
            else:
                keep = rid >= s
                a_n, b_n = pltpu.roll(a, s, axis=0), pltpu.roll(b, s, axis=0)
            b = a * jnp.where(keep, b_n, 0.0) + b
            a = a * jnp.where(keep, a_n, 1.0)
        out_ref[rows, :] = a * h + b
        a_e = jnp.sum(jnp.where(rid == edge, a, 0.0), axis=0, keepdims=True)
        b_e = jnp.sum(jnp.where(rid == edge, b, 0.0), axis=0, keepdims=True)
        return a_e * h + b_e

    return lax.fori_loop(0, n_chunks, chunk, carry, unroll=4 if n_chunks % 4 == 0 else 1)


def _norm_fwd(h, g, name):
    Tp, D = h.shape
    tm = _tile(Tp, TILE["norm"])

    def body(h_ref, g_ref, u_ref, r_ref):
        x = h_ref[...]
        r = lax.rsqrt(jnp.mean(x * x, axis=-1, keepdims=True) + NORM_EPS)
        u_ref[...] = (x * r * g_ref[...]).astype(BF16)
        r_ref[...] = r

    return _call(
        body, name=name, grid=(Tp // tm,),
        in_specs=[pl.BlockSpec((tm, D), lambda i: (i, 0)), pl.BlockSpec((1, D), lambda i: (0, 0))],
        out_specs=[pl.BlockSpec((tm, D), lambda i: (i, 0)), pl.BlockSpec((tm, 1), lambda i: (i, 0))],
        out_shape=[_sds((Tp, D), BF16), _sds((Tp, 1), F32)],
        compiler_params=_cp("parallel"),
    )(h, g)


def _proj_fwd(u, w_slots, name, tasks=(), part=(0, 1), earlier=None):
    Tp, K = u.shape
    S, _, n = w_slots.shape
    p, parts = part
    tm = _tile(Tp, TILE["proj"])

    def body(a_ref, b_ref, *rest):
        o_ref = rest[-1]
        o_ref[...] = jnp.dot(a_ref[...], b_ref[...], preferred_element_type=F32)

    in_specs = [pl.BlockSpec((tm, K), lambda j, i: (i, 0)), pl.BlockSpec((None, K, n), lambda j, i: (j, 0, 0))]
    operands = (u, w_slots)
    aliases = {}
    if earlier is not None:
        in_specs.append(pl.BlockSpec(memory_space=pl.ANY))
        operands += (earlier,)
        aliases = {2: 0}
    (proj,), extra = _hosted(
        body, tasks, name=name, grid=(S, Tp // tm), in_specs=in_specs,
        out_specs=[pl.BlockSpec((tm, n), lambda j, i: (i, j * parts + p))],
        out_shape=[_sds((Tp, S * parts * n), F32)],
        semantics=("parallel", "parallel"), operands=operands, aliases=aliases)
    return proj, extra


def _mlp1_fwd(u2, w_slots, name, tasks=()):
    Tp, K = u2.shape
    S, _, n = w_slots.shape
    tm = _tile(Tp, TILE["mlp1"])

    def body(a_ref, b_ref, act_ref, a1_ref):
        a1 = jnp.dot(a_ref[...], b_ref[...], preferred_element_type=F32)
        relu = jnp.maximum(a1, 0.0)
        act_ref[...] = (relu * relu).astype(BF16)
        a1_ref[...] = a1.astype(BF16)

    return _hosted(
        body, tasks, name=name, grid=(S, Tp // tm),
        in_specs=[pl.BlockSpec((tm, K), lambda j, i: (i, 0)), pl.BlockSpec((None, K, n), lambda j, i: (j, 0, 0))],
        out_specs=[pl.BlockSpec((tm, n), lambda j, i: (i, j))] * 2,
        out_shape=[_sds((Tp, S * n), BF16)] * 2,
        semantics=("parallel", "parallel"), operands=(u2, w_slots))


def _pool_fwd(proj, pool_w, name):
    Tp = proj.shape[0]
    G, Cg, _ = pool_w.shape
    D = G * Cg
    tm = _tile(Tp, TILE["pool"])

    def body(v_ref, w_ref, d_ref, y_ref, prev_ref):
        t = pl.program_id(0)

        @pl.when(t == 0)
        def _():
            prev_ref[...] = jnp.zeros_like(prev_ref)

        rows = _row_ids(t, tm)
        for g, win in enumerate(POOL_WINDOWS):
            cols = slice(g * Cg, (g + 1) * Cg)
            v = v_ref[:, cols]
            s = jnp.concatenate([prev_ref[:, cols], v], axis=0)
            k = 1
            while k < win:
                s = s + pltpu.roll(s, k, axis=0)
                k *= 2
            cnt = jnp.minimum(rows + 1, win).astype(F32)
            d = s[MAX_WINDOW:] / cnt - v
            d_ref[:, cols] = d.astype(BF16)
            y_ref[:, cols] = jnp.dot(d.astype(BF16), w_ref[g], preferred_element_type=F32)
        prev_ref[...] = v_ref[tm - MAX_WINDOW:, :]

    return _call(
        body, name=name, grid=(Tp // tm,),
        in_specs=[pl.BlockSpec((tm, D), lambda t: (t, 0)), pl.BlockSpec((G, Cg, Cg), lambda t: (0, 0, 0))],
        out_specs=[pl.BlockSpec((tm, D), lambda t: (t, 0))] * 2,
        out_shape=[_sds((Tp, D), BF16), _sds((Tp, D), F32)],
        scratch_shapes=[pltpu.VMEM((MAX_WINDOW, D), F32)],
        compiler_params=_cp("arbitrary"),
    )(proj, pool_w)


def _lru_fwd(proj, y_pool, scale, conv_w, conv_b, wa, ba, wx, bx, lam, name, tasks=()):
    Tp = proj.shape[0]
    H, hd, _ = wa.shape
    D = H * hd
    tm = _tile(Tp, TILE["lru"])
    nb = D // hd

    def body(vl_ref, vg_ref, gp_ref, gl_ref, y_ref, sc_ref, cw_ref, cb_ref, wa_ref, ba_ref, wx_ref, bx_ref,
             lam_ref, xc_ref, r_ref, i_ref, a_ref, mult_ref, hs_ref, m_ref, prev_ref, carry_ref, b_s):
        t = pl.program_id(1)

        @pl.when(t == 0)
        def _():
            prev_ref[...] = jnp.zeros_like(prev_ref)
            carry_ref[...] = jnp.zeros_like(carry_ref)

        v = vl_ref[...]
        prev = prev_ref[...]
        xc = jnp.zeros_like(v) + cb_ref[...]
        for k in range(CONV_WIDTH):
            xc = xc + cw_ref[k:k + 1, :] * _shift_down(prev, v, CONV_WIDTH - 1 - k)
        prev_ref[...] = v[tm - HALO:, :]
        xcb = xc.astype(BF16)
        r = _sigmoid(jnp.dot(xcb, wa_ref[...], preferred_element_type=F32) + ba_ref[...])
        i = _sigmoid(jnp.dot(xcb, wx_ref[...], preferred_element_type=F32) + bx_ref[...])
        a, mult = _lru_gates(r, _softplus(-lam_ref[...]))
        a_ref[...] = a
        mult_ref[...] = mult
        b_s[...] = mult * (i * xc)
        xc_ref[...] = xc
        r_ref[...] = r
        i_ref[...] = i
        carry_ref[0:1, :] = _scan_chunks(a_ref, b_s, hs_ref, carry_ref[0:1, :], tm, reverse=False)
        lru_out = hs_ref[...] * _gelu(vg_ref[...])
        pool_out = y_ref[...] * sc_ref[...]
        m_ref[...] = (_sigmoid(gp_ref[...]) * pool_out + _sigmoid(gl_ref[...]) * lru_out).astype(BF16)

    def piece(p):
        return pl.BlockSpec((tm, hd), lambda h, t: (t, p * nb + h))

    blk = pl.BlockSpec((tm, hd), lambda h, t: (t, h))
    vec = pl.BlockSpec((1, hd), lambda h, t: (0, h))
    mat = pl.BlockSpec((None, hd, hd), lambda h, t: (h, 0, 0))
    bias = pl.BlockSpec((None, 1, hd), lambda h, t: (h, 0, 0))
    return _hosted(
        body, tasks, name=name, grid=(H, Tp // tm),
        in_specs=[piece(1), piece(2), piece(3), piece(4), blk, vec,
                  pl.BlockSpec((CONV_WIDTH, hd), lambda h, t: (0, h)), vec, mat, bias, mat, bias, vec],
        out_specs=[blk] * 7,
        out_shape=[_sds((Tp, D), F32)] * 6 + [_sds((Tp, D), BF16)],
        scratch_shapes=[pltpu.VMEM((HALO, hd), F32), pltpu.VMEM((8, hd), F32), pltpu.VMEM((tm, hd), F32)],
        semantics=("parallel", "arbitrary"),
        operands=(proj, proj, proj, proj, y_pool, scale, conv_w, conv_b, wa, ba.reshape(H, 1, hd), wx,
                  bx.reshape(H, 1, hd), lam))


def _wout_norm_fwd(merged, w_out, h0, g2, name, tasks=()):
    Tp, D = h0.shape
    tm = _tile(Tp, TILE["wout"])

    def body(m_ref, w_ref, h0_ref, g_ref, h1_ref, u2_ref, r2_ref):
        h1 = h0_ref[...] + jnp.dot(m_ref[...], w_ref[...], preferred_element_type=F32)
        r = lax.rsqrt(jnp.mean(h1 * h1, axis=-1, keepdims=True) + NORM_EPS)
        h1_ref[...] = h1
        u2_ref[...] = (h1 * r * g_ref[...]).astype(BF16)
        r2_ref[...] = r

    row = pl.BlockSpec((tm, D), lambda i: (i, 0))
    return _hosted(
        body, tasks, name=name, grid=(Tp // tm,),
        in_specs=[row, pl.BlockSpec((D, D), lambda i: (0, 0)), row, pl.BlockSpec((1, D), lambda i: (0, 0))],
        out_specs=[row, row, pl.BlockSpec((tm, 1), lambda i: (i, 0))],
        out_shape=[_sds((Tp, D), F32), _sds((Tp, D), BF16), _sds((Tp, 1), F32)],
        semantics=("parallel",), operands=(merged, w_out, h0, g2))


def _mlp2_loss(act, w2, h1, target, gf, n_meta, seq, name):
    Tp, D = h1.shape
    K = act.shape[1]
    tm = _tile(Tp, TILE["mlp2"])
    tk = min(K, MLP2_K)
    nk = K // tk

    rc = _tile(tm, EPILOGUE_ROWS)

    def body(a_ref, w_ref, h1_hbm, t_hbm, g_ref, dh_ref, dhb_ref, loss_ref, dg_ref, h1_buf, t_buf, sems):
        i, k = pl.program_id(0), pl.program_id(1)
        tile_rows = pl.ds(pl.multiple_of(i * tm, tm), tm)
        fetch = [pltpu.make_async_copy(h1_hbm.at[tile_rows, :], h1_buf, sems.at[0]),
                 pltpu.make_async_copy(t_hbm.at[tile_rows, :], t_buf, sems.at[1])]

        @pl.when(k == 0)
        def _():
            for f in fetch:
                f.start()
            dh_ref[...] = jnp.zeros_like(dh_ref)

        @pl.when((i == 0) & (k == 0))
        def _():
            loss_ref[...] = jnp.zeros_like(loss_ref)
            dg_ref[...] = jnp.zeros_like(dg_ref)

        dh_ref[...] += jnp.dot(a_ref[...], w_ref[...], preferred_element_type=F32)

        @pl.when(k == nk - 1)
        def _():
            for f in fetch:
                f.wait()
            g = g_ref[...]

            def chunk(c, carry):
                loss_acc, dg_acc = carry
                rows = pl.ds(pl.multiple_of(c * rc, rc), rc)
                h2 = h1_buf[rows, :] + dh_ref[rows, :]
                r = lax.rsqrt(jnp.mean(h2 * h2, axis=-1, keepdims=True) + NORM_EPS)
                out = h2 * r * g
                row_id = i * tm + c * rc + lax.broadcasted_iota(jnp.int32, (rc, 1), 0)
                valid = (row_id >= n_meta) & (row_id < n_meta + seq)
                diff = jnp.where(valid, out - t_buf[rows, :], 0.0)
                dout = diff / D
                dog = dout * g
                dh = r * dog - h2 * (r * r * r * jnp.mean(dog * h2, axis=-1, keepdims=True))
                dh_ref[rows, :] = dh
                dhb_ref[rows, :] = dh.astype(BF16)
                loss_acc = loss_acc + 0.5 * jnp.sum(jnp.mean(diff * diff, axis=-1, keepdims=True), axis=0, keepdims=True)
                return loss_acc, dg_acc + jnp.sum(dout * (h2 * r), axis=0, keepdims=True)

            loss_sum, dg_sum = lax.fori_loop(0, tm // rc, chunk, (jnp.zeros((1, 1), F32), jnp.zeros((1, D), F32)))
            loss_ref[...] += loss_sum
            dg_ref[...] += dg_sum

    row = pl.BlockSpec((tm, D), lambda i, k: (i, 0))
    hbm = pl.BlockSpec(memory_space=pl.ANY)
    return _call(
        body, name=name, grid=(Tp // tm, nk),
        in_specs=[pl.BlockSpec((tm, tk), lambda i, k: (i, k)), pl.BlockSpec((tk, D), lambda i, k: (k, 0)),
                  hbm, hbm, pl.BlockSpec((1, D), lambda i, k: (0, 0))],
        out_specs=[row, row, pl.BlockSpec((8, 128), lambda i, k: (0, 0)), pl.BlockSpec((1, D), lambda i, k: (0, 0))],
        out_shape=[_sds((Tp, D), F32), _sds((Tp, D), BF16), _sds((8, 128), F32), _sds((1, D), F32)],
        scratch_shapes=[pltpu.VMEM((tm, D), F32), pltpu.VMEM((tm, D), F32), pltpu.SemaphoreType.DMA((2,))],
        compiler_params=_cp("arbitrary", "arbitrary"),
    )(act, w2, h1, target, gf)


def _dact_bwd(dh2b, w2_slots, a1, name):
    Tp, D = dh2b.shape
    S, n, _ = w2_slots.shape
    tm = _tile(Tp, TILE["dact"])

    def body(g_ref, w_ref, a1_ref, o_ref):
        dact = lax.dot_general(g_ref[...], w_ref[...], _NT, preferred_element_type=F32)
        o_ref[...] = (dact * (2.0 * jnp.maximum(a1_ref[...].astype(F32), 0.0))).astype(BF16)

    return _call(
        body, name=name, grid=(S, Tp // tm),
        in_specs=[pl.BlockSpec((tm, D), lambda j, i: (i, 0)), pl.BlockSpec((None, n, D), lambda j, i: (j, 0, 0)),
                  pl.BlockSpec((tm, n), lambda j, i: (i, j))],
        out_specs=pl.BlockSpec((tm, n), lambda j, i: (i, j)),
        out_shape=_sds((Tp, S * n), BF16),
        compiler_params=_cp("parallel", "parallel"),
    )(dh2b, w2_slots, a1)


def _weight_grad(a, g, blocks, block_a, name, tasks=(), part=(0, 1)):
    Tp, Ka = a.shape
    Ng = g.shape[1]
    p, parts = part
    assert parts == 1 or not block_a
    ka = Ka // blocks if block_a else Ka // parts
    ng = Ng if block_a else Ng // blocks
    tt = _tile(Tp, TILE["tn"])
    nt = Tp // tt

    def body(a_ref, g_ref, o_ref, acc_ref):
        t = pl.program_id(1)

        @pl.when(t == 0)
        def _():
            acc_ref[...] = jnp.zeros_like(acc_ref)

        acc_ref[...] += lax.dot_general(a_ref[...], g_ref[...], _TN, preferred_element_type=F32)

        @pl.when(t == nt - 1)
        def _():
            o_ref[...] = acc_ref[...].astype(o_ref.dtype)

    if block_a:
        a_spec = pl.BlockSpec((tt, ka), lambda j, t: (t, j))
        g_spec = pl.BlockSpec((tt, ng), lambda j, t: (t, 0))
    else:
        a_spec = pl.BlockSpec((tt, ka), lambda j, t: (t, p))
        g_spec = pl.BlockSpec((tt, ng), lambda j, t: (t, j))
    (dw,), extra = _hosted(
        body, tasks, name=name, grid=(blocks, nt),
        in_specs=[a_spec, g_spec],
        out_specs=[pl.BlockSpec((None, ka, ng), lambda j, t: (j, 0, 0))],
        out_shape=[_sds((blocks, ka, ng), BF16)],
        scratch_shapes=[pltpu.VMEM((ka, ng), F32)],
        semantics=("parallel", "arbitrary"), operands=(a, g))
    return dw, extra


def _nt_norm_bwd(dz, w_parts, dres, hin, rin, g, want_bf16, name, tasks=(), tiles=None, earlier=None):
    Tp, D = hin.shape
    P = len(w_parts)
    S, _, n = w_parts[0].shape
    K = S * P
    tm = _tile(Tp, TILE["nt"])
    t0, nt = tiles if tiles is not None else (0, Tp // tm)
    assert not (want_bf16 and earlier is not None)

    rc = _tile(tm, EPILOGUE_ROWS)

    def body(dz_ref, *rest):
        w_refs, (dres_hbm, h_hbm, r_ref, g_ref), rest = rest[:P], rest[P:P + 4], rest[P + 4:]
        if earlier is not None:
            _, dg0_ref, dh_ref, dg_ref, dres_buf, h_buf, sems = rest
        elif want_bf16:
            dh_ref, dhb_ref, dg_ref, dres_buf, h_buf, sems = rest
        else:
            dh_ref, dg_ref, dres_buf, h_buf, sems = rest
        i, k = pl.program_id(0), pl.program_id(1)
        tile_rows = pl.ds(pl.multiple_of((t0 + i) * tm, tm), tm)
        fetch = [pltpu.make_async_copy(dres_hbm.at[tile_rows, :], dres_buf, sems.at[0]),
                 pltpu.make_async_copy(h_hbm.at[tile_rows, :], h_buf, sems.at[1])]

        @pl.when(k == 0)
        def _():
            for f in fetch:
                f.start()
            dh_ref[...] = jnp.zeros_like(dh_ref)

        @pl.when((i == 0) & (k == 0))
        def _():
            dg_ref[...] = jnp.zeros_like(dg_ref) if earlier is None else dg0_ref[...]

        for q in range(P):
            @pl.when(k % P == q)
            def _(q=q):
                dh_ref[...] += lax.dot_general(dz_ref[...], w_refs[q][...], _NT, preferred_element_type=F32)

        @pl.when(k == K - 1)
        def _():
            for f in fetch:
                f.wait()
            g = g_ref[...]

            def chunk(c, dg_acc):
                rows = pl.ds(pl.multiple_of(c * rc, rc), rc)
                du = dh_ref[rows, :]
                h = h_buf[rows, :]
                r = r_ref[rows, :]
                dug = du * g
                dh = dres_buf[rows, :] + r * dug - h * (r * r * r * jnp.mean(dug * h, axis=-1, keepdims=True))
                dh_ref[rows, :] = dh
                if want_bf16:
                    dhb_ref[rows, :] = dh.astype(BF16)
                return dg_acc + jnp.sum(du * (h * r), axis=0, keepdims=True)

            dg_ref[...] += lax.fori_loop(0, tm // rc, chunk, jnp.zeros((1, D), F32))

    row = pl.BlockSpec((tm, D), lambda i, k: (t0 + i, 0))
    vec = pl.BlockSpec((1, D), lambda i, k: (0, 0))
    hbm = pl.BlockSpec(memory_space=pl.ANY)
    out_specs = [row] + ([row] if want_bf16 else []) + [vec]
    out_shape = [_sds((Tp, D), F32)] + ([_sds((Tp, D), BF16)] if want_bf16 else []) + [_sds((1, D), F32)]
    in_specs = ([pl.BlockSpec((tm, n), lambda i, k: (t0 + i, k))]
                + [pl.BlockSpec((None, D, n), lambda i, k: (k // P, 0, 0))] * P
                + [hbm, hbm, pl.BlockSpec((tm, 1), lambda i, k: (t0 + i, 0)), vec])
    operands = (dz, *w_parts, dres, hin, rin, g)
    aliases = {}
    if earlier is not None:
        in_specs += [hbm, vec]
        operands += tuple(earlier)
        aliases = {P + 5: 0}
    return _hosted(
        body, tasks, name=name, grid=(nt, K), in_specs=in_specs, out_specs=out_specs, out_shape=out_shape,
        scratch_shapes=[pltpu.VMEM((tm, D), F32), pltpu.VMEM((tm, D), F32), pltpu.SemaphoreType.DMA((2,))],
        semantics=("arbitrary", "arbitrary"), operands=operands, aliases=aliases)


def _dmerged_bwd(dh1b, w_out, name):
    Tp, D = dh1b.shape
    tm = _tile(Tp, TILE["dmerged"])

    def body(g_ref, w_ref, o_ref):
        o_ref[...] = lax.dot_general(g_ref[...], w_ref[...], _NT, preferred_element_type=F32)

    row = pl.BlockSpec((tm, D), lambda i: (i, 0))
    return _call(
        body, name=name, grid=(Tp // tm,),
        in_specs=[row, pl.BlockSpec((D, D), lambda i: (0, 0))],
        out_specs=row, out_shape=_sds((Tp, D), F32),
        compiler_params=_cp("parallel"),
    )(dh1b, w_out)


def _pool_bwd(dmerged, proj, y_pool, d_pool, scale, pool_w, name, tasks=()):
    Tp, D = dmerged.shape
    G, Cg, _ = pool_w.shape
    tm = _tile(Tp, TILE["pool"])
    nt = Tp // tm

    def body(dm_ref, gp_ref, y_ref, d_ref, sc_ref, w_ref, dproj_hbm, dw_ref, dsc_ref, next_ref, out_buf, out_sems):
        t = pl.program_id(0)
        tile = nt - 1 - t
        dv_ref, dgp_ref = out_buf.at[0], out_buf.at[1]
        tile_rows = pl.ds(pl.multiple_of(tile * tm, tm), tm)
        out_copies = [pltpu.make_async_copy(out_buf.at[k], dproj_hbm.at[tile_rows, pl.ds(piece * D, D)], out_sems.at[k])
                      for k, piece in enumerate((0, 3))]

        @pl.when(t > 0)
        def _():
            for cp in out_copies:
                cp.wait()

        @pl.when(t == 0)
        def _():
            next_ref[...] = jnp.zeros_like(next_ref)
            dw_ref[...] = jnp.zeros_like(dw_ref)
            dsc_ref[...] = jnp.zeros_like(dsc_ref)

        rows = _row_ids(tile, tm)
        dm = dm_ref[...]
        y = y_ref[...]
        sc = sc_ref[...]
        sg = _sigmoid(gp_ref[...])
        dpo = dm * sg
        dgp_ref[...] = (dm * (y * sc) * sg * (1.0 - sg)).astype(BF16)
        dsc_ref[...] += jnp.sum(dpo * y, axis=0, keepdims=True)
        dyb = (dpo * sc).astype(BF16)
        for g, win in enumerate(POOL_WINDOWS):
            cols = slice(g * Cg, (g + 1) * Cg)
            dy = dyb[:, cols]
            dd = lax.dot_general(dy, w_ref[g], _NT, preferred_element_type=F32)
            dw_ref[g] += lax.dot_general(d_ref[:, cols], dy, _TN, preferred_element_type=F32)
            q = dd / jnp.minimum(rows + 1, win).astype(F32)
            s = jnp.concatenate([q, next_ref[:, cols]], axis=0)
            k = 1
            while k < win:
                s = s + pltpu.roll(s, s.shape[0] - k, axis=0)
                k *= 2
            dv_ref[:, cols] = (s[:tm] - dd).astype(BF16)
            next_ref[:, cols] = q[:MAX_WINDOW]
        for cp in out_copies:
            cp.start()

        @pl.when(t == nt - 1)
        def _():
            for cp in out_copies:
                cp.wait()

    row = pl.BlockSpec((tm, D), lambda t: (nt - 1 - t, 0))
    return _hosted(
        body, tasks, name=name, grid=(nt,),
        in_specs=[row, pl.BlockSpec((tm, D), lambda t: (nt - 1 - t, 3)), row, row,
                  pl.BlockSpec((1, D), lambda t: (0, 0)), pl.BlockSpec((G, Cg, Cg), lambda t: (0, 0, 0))],
        out_specs=[pl.BlockSpec(memory_space=pl.ANY), pl.BlockSpec((G, Cg, Cg), lambda t: (0, 0, 0)),
                   pl.BlockSpec((1, D), lambda t: (0, 0))],
        out_shape=[_sds((Tp, proj.shape[1]), BF16), _sds((G, Cg, Cg), F32), _sds((1, D), F32)],
        scratch_shapes=[pltpu.VMEM((MAX_WINDOW, D), F32), pltpu.VMEM((2, tm, D), BF16), pltpu.SemaphoreType.DMA((2,))],
        semantics=("arbitrary",), operands=(dmerged, proj, y_pool, d_pool, scale, pool_w))


LRU_SMALL_ROWS = 8


def _lru_bwd(dmerged, proj, xc, r_gate, i_gate, a_gate, mult_gate, hs, lam, conv_w, wa, wx, dproj, name, tasks=()):
    Tp, D = dmerged.shape
    H, hd, _ = wa.shape
    tm = _tile(Tp, TILE["lru"])
    nt = Tp // tm
    nb = D // hd
    halo_blocks = tm // HALO

    def body(dm_ref, vl_ref, vg_ref, gl_ref, xc_ref, r_ref, i_ref, a_ref, mult_ref, hs_ref, hsp_ref, lam_ref, cw_ref,
             wa_ref, wx_ref, _, dproj_hbm, dwa_ref, dwx_ref, small_ref,
             mu_next_ref, dxc_next_ref, q_s, mu_s, out_buf, out_sems):
        h_id, t = pl.program_id(0), pl.program_id(1)
        tile = nt - 1 - t
        step = h_id * nt + t
        dvl_ref, dvg_ref, dgl_ref = out_buf.at[0], out_buf.at[1], out_buf.at[2]
        tile_rows = pl.ds(pl.multiple_of(tile * tm, tm), tm)
        out_copies = [pltpu.make_async_copy(
            out_buf.at[k], dproj_hbm.at[tile_rows, pl.ds(pl.multiple_of((piece * nb + h_id) * hd, hd), hd)],
            out_sems.at[k]) for k, piece in enumerate((1, 2, 4))]

        @pl.when(step > 0)
        def _():
            for cp in out_copies:
                cp.wait()

        @pl.when(t == 0)
        def _():
            mu_next_ref[...] = jnp.zeros_like(mu_next_ref)
            dxc_next_ref[...] = jnp.zeros_like(dxc_next_ref)
            dwa_ref[...] = jnp.zeros_like(dwa_ref)
            dwx_ref[...] = jnp.zeros_like(dwx_ref)
            small_ref[...] = jnp.zeros_like(small_ref)

        first = tile == 0
        dm = dm_ref[...]
        hs_t = hs_ref[...]
        xc_t = xc_ref[...]
        r = r_ref[...]
        i = i_ref[...]
        lam_v = lam_ref[...]
        sp = _softplus(-lam_v)
        a = a_ref[...]
        mult = mult_ref[...]

        sg = _sigmoid(gl_ref[...])
        ge, dge = _gelu_and_grad(vg_ref[...])
        dlo = dm * sg
        dgl_ref[...] = (dm * (hs_t * ge) * sg * (1.0 - sg)).astype(BF16)
        dvg_ref[...] = (dlo * hs_t * dge).astype(BF16)
        dhs = dlo * ge

        q_s[...] = a * dhs
        mu_first = _scan_chunks(a_ref, q_s, mu_s, mu_next_ref[0:1, :], tm, reverse=True)
        lam_t = dhs + _shift_up(mu_s[...], mu_next_ref[...], 1)
        mu_next_ref[...] = jnp.broadcast_to(mu_first, mu_next_ref.shape)

        h_prev = _shift_down(jnp.where(first, 0.0, hsp_ref[...]), hs_t, 1)
        da = lam_t * h_prev
        dmult = lam_t * (i * xc_t)
        di = lam_t * mult * xc_t
        dxc = lam_t * mult * i
        dlog_a = da * a - dmult * (a * a) / mult
        dr = dlog_a * (-LRU_C * sp)
        dlam_rows = dlog_a * (-LRU_C * r)
        dza = dr * r * (1.0 - r)
        dzx = di * i * (1.0 - i)
        dzab, dzxb = dza.astype(BF16), dzx.astype(BF16)
        xcb = xc_t.astype(BF16)
        dxc = dxc + lax.dot_general(dzab, wa_ref[...], _NT, preferred_element_type=F32)
        dxc = dxc + lax.dot_general(dzxb, wx_ref[...], _NT, preferred_element_type=F32)
        dwa_ref[...] += lax.dot_general(xcb, dzab, _TN, preferred_element_type=F32)
        dwx_ref[...] += lax.dot_general(xcb, dzxb, _TN, preferred_element_type=F32)

        dxc_next = dxc_next_ref[...]
        taps = [_shift_up(dxc, dxc_next, CONV_WIDTH - 1 - k) for k in range(CONV_WIDTH)]
        dv = jnp.zeros_like(dxc)
        for k in range(CONV_WIDTH):
            dv = dv + cw_ref[k:k + 1, :] * taps[k]
        dvl_ref[...] = dv.astype(BF16)
        dxc_next_ref[...] = dxc[:HALO, :]

        v_t = vl_ref[...]
        small = [jnp.sum(dza, axis=0, keepdims=True), jnp.sum(dzx, axis=0, keepdims=True),
                 jnp.sum(dlam_rows, axis=0, keepdims=True) * (-_sigmoid(-lam_v)),
                 jnp.sum(dxc, axis=0, keepdims=True)]
        for k in range(CONV_WIDTH):
            small.append(jnp.sum(taps[k] * v_t, axis=0, keepdims=True))
        for k, row in enumerate(small):
            small_ref[k:k + 1, :] += row
        for cp in out_copies:
            cp.start()

        @pl.when(step == H * nt - 1)
        def _():
            for cp in out_copies:
                cp.wait()

    def piece(p):
        return pl.BlockSpec((tm, hd), lambda h, t: (nt - 1 - t, p * nb + h))

    def halo(p):
        return pl.BlockSpec((HALO, hd), lambda h, t: (jnp.maximum((nt - 1 - t) * halo_blocks - 1, 0), p * nb + h))

    blk = pl.BlockSpec((tm, hd), lambda h, t: (nt - 1 - t, h))
    vec = pl.BlockSpec((1, hd), lambda h, t: (0, h))
    mat = pl.BlockSpec((None, hd, hd), lambda h, t: (h, 0, 0))
    return _hosted(
        body, tasks, name=name, grid=(H, nt),
        in_specs=[blk, piece(1), piece(2), piece(4), blk, blk, blk, blk, blk, blk, halo(0), vec,
                  pl.BlockSpec((CONV_WIDTH, hd), lambda h, t: (0, h)), mat, mat, pl.BlockSpec(memory_space=pl.ANY)],
        out_specs=[pl.BlockSpec(memory_space=pl.ANY), mat, mat,
                   pl.BlockSpec((None, LRU_SMALL_ROWS, hd), lambda h, t: (h, 0, 0))],
        out_shape=[_sds(dproj.shape, BF16)] + [_sds((H, hd, hd), F32)] * 2 + [_sds((H, LRU_SMALL_ROWS, hd), F32)],
        scratch_shapes=[pltpu.VMEM((HALO, hd), F32), pltpu.VMEM((HALO, hd), F32),
                        pltpu.VMEM((tm, hd), F32), pltpu.VMEM((tm, hd), F32),
                        pltpu.VMEM((3, tm, hd), BF16), pltpu.SemaphoreType.DMA((3,))],
        semantics=("arbitrary", "arbitrary"), aliases={15: 0},
        operands=(dmerged, proj, proj, proj, xc, r_gate, i_gate, a_gate, mult_gate, hs, hs, lam, conv_w, wa, wx,
                  dproj))


def _adamw(w, g, m, v):
    m = ADAM_B1 * m + (1.0 - ADAM_B1) * g
    v = ADAM_B2 * v + (1.0 - ADAM_B2) * (g * g)
    m_hat = m / (1.0 - ADAM_B1 ** ADAM_STEP)
    v_hat = v / (1.0 - ADAM_B2 ** ADAM_STEP)
    delta = -ADAM_LR * (m_hat / (jnp.sqrt(v_hat) + ADAM_EPS) + ADAM_WD * w)
    return delta, m, v


def _reduce_update(pair_sums, chip_sums, w, m, v, chip_slot, name, part=(0, 1), earlier=None):
    R, C = pair_sums.shape[1:]
    p, parts = part
    tr = _tile(R, TILE["update"])
    nblk = R // tr

    def body(slot_ref, own_ref, got_ref, w_ref, m_ref, v_ref, *rest):
        g_out, d_out, m_out, v_out = rest[-4:]
        g = own_ref[...].astype(F32)
        for k in range(3):
            g = g + got_ref[k].astype(F32)
        d, m_new, v_new = _adamw(w_ref[...], g, m_ref[...], v_ref[...])
        g_out[...] = g
        d_out[...] = d
        m_out[...] = m_new
        v_out[...] = v_new

    blk = pl.BlockSpec((tr, C), lambda i, s: (p * nblk + i, 0))
    in_specs = [pl.BlockSpec((None, tr, C), lambda i, s: (s[0], i, 0)),
                pl.BlockSpec((3, tr, C), lambda i, s: (0, i, 0)), blk, blk, blk]
    operands = (chip_slot, pair_sums, chip_sums, w, m, v)
    aliases = {}
    if earlier is not None:
        in_specs += [pl.BlockSpec(memory_space=pl.ANY)] * 4
        operands += tuple(earlier)
        aliases = {6 + k: k for k in range(4)}
    return _call(
        body, name=name,
        grid_spec=pltpu.PrefetchScalarGridSpec(
            num_scalar_prefetch=1, grid=(R // tr,), in_specs=in_specs, out_specs=[blk] * 4),
        out_shape=[_sds((parts * R, C), F32)] * 4,
        input_output_aliases=aliases,
        compiler_params=_cp("parallel"),
    )(*operands)


def _small_update(w, g, m, v, name):
    def body(w_ref, g_ref, m_ref, v_ref, d_out, m_out, v_out):
        d, m_new, v_new = _adamw(w_ref[...], g_ref[...], m_ref[...], v_ref[...])
        d_out[...] = d
        m_out[...] = m_new
        v_out[...] = v_new

    return _call(body, name=name, out_shape=[_sds(w.shape, F32)] * 3)(w, g, m, v)


def _slots_from_rows(full, lead):
    L, R, C = full.shape
    r = R // N_DEV
    return full.reshape(L, N_DEV, r, C).transpose(1, 0, 2, 3).reshape(N_DEV, L * r, C)


def _rows_from_slots(slots, lead):
    _, LR, C = slots.shape
    r = LR // lead
    return slots.reshape(N_DEV, lead, r, C).transpose(1, 0, 2, 3).reshape(lead, N_DEV * r, C)


def kernel(x, meta_tokens, norm1_g, w_in, pool_w, pool_scale, conv_w, conv_b, gate_a_w, gate_a_b, gate_x_w, gate_x_b, lru_lambda, w_out, norm2_g, mlp_w1, mlp_w2, final_g, loss_target, m_meta_tokens, m_norm1_g, m_w_in, m_pool_w, m_pool_scale, m_conv_w, m_conv_b, m_gate_a_w, m_gate_a_b, m_gate_x_w, m_gate_x_b, m_lru_lambda, m_w_out, m_norm2_g, m_mlp_w1, m_mlp_w2, m_final_g, v_meta_tokens, v_norm1_g, v_w_in, v_pool_w, v_pool_scale, v_conv_w, v_conv_b, v_gate_a_w, v_gate_a_b, v_gate_x_w, v_gate_x_b, v_lru_lambda, v_w_out, v_norm2_g, v_mlp_w1, v_mlp_w2, v_final_g):
    seq, D = x.shape[1], x.shape[2]
    n_meta = meta_tokens.shape[0]
    G, Cg = pool_w.shape[1], pool_w.shape[3]
    H, hd = gate_a_w.shape[1], gate_a_w.shape[3]
    T = n_meta + seq
    Tp = -(-T // ROW_ALIGN) * ROW_ALIGN
    ix, iy, ic = _pos()
    me = 4 * ix + 2 * iy + ic
    core = jnp.reshape(ic, (1,)).astype(jnp.int32)
    chip_slot = jnp.reshape(2 * ix + iy, (1,)).astype(jnp.int32)

    w_in_l, w1_l, w2_l, w_out_l = w_in[0], mlp_w1[0], mlp_w2[0], w_out[0]
    pool_l = pool_w[0].reshape(G * (Cg // N_DEV), Cg)
    wa_l = gate_a_w[0].reshape(H * (hd // N_DEV), hd)
    wx_l = gate_x_w[0].reshape(H * (hd // N_DEV), hd)
    small_params = jnp.concatenate(
        [meta_tokens, conv_w[0], jnp.zeros((4, D // N_DEV), F32)], axis=0)
    biases = jnp.concatenate([gate_a_b[0], gate_x_b[0]], axis=0)
    (w_in_g, pool_g, wa_g, wx_g, small_g, bias_g) = _all_gather(
        [w_in_l.astype(BF16), pool_l.astype(BF16), wa_l.astype(BF16), wx_l.astype(BF16), small_params, biases],
        "gather_first")
    pool_full = _rows_from_slots(pool_g, G)
    wa_full = _rows_from_slots(wa_g, H)
    wx_full = _rows_from_slots(wx_g, H)
    small_full = small_g.transpose(1, 0, 2).reshape(n_meta + 8, D)
    meta_full = small_full[:n_meta]
    conv_full = small_full[n_meta:n_meta + CONV_WIDTH]
    bias_full = bias_g.transpose(1, 0, 2).reshape(2 * H, hd)
    ba_full, bx_full = bias_full[:H], bias_full[H:]

    h0 = jnp.concatenate([meta_full, x[0], jnp.zeros((Tp - T, D), F32)], axis=0)
    target = jnp.concatenate([jnp.zeros((n_meta, D), F32), loss_target[0], jnp.zeros((Tp - T, D), F32)], axis=0)
    u, r1 = _norm_fwd(h0, norm1_g, "norm1")
    proj, ((w_out_g,), (w1_g,)) = _proj_fwd(
        u, w_in_g, "proj", tasks=[_AgFull(w_out_l.astype(BF16)), _AgFull(w1_l.astype(BF16))])
    w_in_parts = [w_in_g]
    d_pool, y_pool = _pool_fwd(proj, pool_full, "pool_fwd")
    (xc, r_gate, i_gate, a_gate, mult_gate, hs, merged), ((w2_g,),) = _lru_fwd(
        proj, y_pool, pool_scale, conv_full, conv_b, wa_full, ba_full, wx_full, bx_full, lru_lambda, "lru_fwd",
        tasks=[_AgFull(w2_l.astype(BF16))])
    w_out_full = w_out_g.reshape(D, D)
    (h1, u2, r2), _ = _wout_norm_fwd(merged, w_out_full, h0, norm2_g, "wout_norm2")
    (act, a1), _ = _mlp1_fwd(u2, w1_g, "mlp1")
    dh2, dh2b, loss_tile, d_final_g = _mlp2_loss(
        act, w2_g.reshape(-1, D), h1, target, final_g.reshape(1, D), n_meta, seq, "mlp2_loss")

    def pair(part, got, tag):
        return _pair_sum(part, got, core, "pair_sum_" + tag)

    d_a1 = _dact_bwd(dh2b, w2_g, a1, "dact")
    dw2_p, _ = _weight_grad(act, dh2b, N_DEV, True, "dw2")
    dw1_p, ((dw2_got,),) = _weight_grad(u2, d_a1, N_DEV, False, "dw1", tasks=[_RsSibling(dw2_p)])
    dw2_pair = pair(dw2_p, dw2_got, "w2")
    (dh1, dh1b, d_norm2_g), ((dw2_chips,), (dw1_got,)) = _nt_norm_bwd(
        d_a1, [w1_g], dh2, h1, r2, norm2_g, True, "du2_norm2", tasks=[_RsChips(dw2_pair), _RsSibling(dw1_p)])
    dw1_pair = pair(dw1_p, dw1_got, "w1")
    dmerged = _dmerged_bwd(dh1b, w_out_full, "dmerged")
    dwout_p, _ = _weight_grad(merged, dh1b, 2, True, "dwout")
    dwout_p = dwout_p.reshape(N_DEV, D // N_DEV, D)
    (dproj_pool, dpool_full, d_scale), ((dwout_got,),) = _pool_bwd(
        dmerged, proj, y_pool, d_pool, pool_scale, pool_full, "pool_bwd", tasks=[_RsSibling(dwout_p)])
    dwout_pair = pair(dwout_p, dwout_got, "wout")
    (dproj, dwa_full, dwx_full, lru_small), ((dw1_chips,), (dwout_chips,)) = _lru_bwd(
        dmerged, proj, xc, r_gate, i_gate, a_gate, mult_gate, hs, lru_lambda, conv_full, wa_full, wx_full, dproj_pool,
        "lru_bwd", tasks=[_RsChips(dw1_pair), _RsChips(dwout_pair)])
    dwin_a, _ = _weight_grad(u, dproj, N_DEV, False, "dwin_a", part=(0, 2))
    dwin_b, ((dwin_a_got,),) = _weight_grad(u, dproj, N_DEV, False, "dwin_b", part=(1, 2), tasks=[_RsSibling(dwin_a)])
    dwin_a_pair = pair(dwin_a, dwin_a_got, "win_a")
    dpool_p = _slots_from_rows(dpool_full, G).astype(BF16)
    dwa_p = _slots_from_rows(dwa_full, H).astype(BF16)
    dwx_p = _slots_from_rows(dwx_full, H).astype(BF16)
    late = [dwin_b, dpool_p, dwa_p, dwx_p]
    n_tiles = Tp // _tile(Tp, TILE["nt"])
    n_first = max(1, n_tiles // 2)
    (dh0_a, d_norm1_a), ((dwin_a_chips,), *late_got) = _nt_norm_bwd(
        dproj, w_in_parts, dh1, h0, r1, norm1_g, False, "du_norm1_a",
        tasks=[_RsChips(dwin_a_pair)] + [_RsSibling(p) for p in late], tiles=(0, n_first))
    late_pair = [pair(p, g[0], "late%d" % k) for k, (p, g) in enumerate(zip(late, late_got))]
    (dh0, d_norm1_g), late_chips = _nt_norm_bwd(
        dproj, w_in_parts, dh1, h0, r1, norm1_g, False, "du_norm1_b", tasks=[_RsChips(p) for p in late_pair],
        tiles=(n_first, n_tiles - n_first), earlier=(dh0_a, d_norm1_a))
    grad_x = dh0[n_meta:T][None]

    pair_sums = [dw2_pair, dw1_pair, dwout_pair] + late_pair[1:]
    chip_sums = [dw2_chips, dw1_chips, dwout_chips] + [c[0] for c in late_chips[1:]]
    big = {}
    names = ["mlp_w2", "mlp_w1", "w_out", "pool_w", "gate_a_w", "gate_x_w"]
    trip = {"mlp_w2": (mlp_w2, m_mlp_w2, v_mlp_w2), "mlp_w1": (mlp_w1, m_mlp_w1, v_mlp_w1),
            "w_out": (w_out, m_w_out, v_w_out),
            "pool_w": (pool_w, m_pool_w, v_pool_w), "gate_a_w": (gate_a_w, m_gate_a_w, v_gate_a_w),
            "gate_x_w": (gate_x_w, m_gate_x_w, v_gate_x_w)}
    for k, nm in enumerate(names):
        w_, m_, v_ = trip[nm]
        shape2 = pair_sums[k].shape[1:]
        outs = _reduce_update(pair_sums[k], chip_sums[k], w_.reshape(shape2), m_.reshape(shape2), v_.reshape(shape2),
                              chip_slot, "update_" + nm)
        big[nm] = [o.reshape(w_.shape) for o in outs]
    win2 = [a_[0] for a_ in (w_in, m_w_in, v_w_in)]
    win_a = _reduce_update(dwin_a_pair, dwin_a_chips, *win2, chip_slot, "update_w_in_a", part=(0, 2))
    win_b = _reduce_update(late_pair[0], late_chips[0][0], *win2, chip_slot, "update_w_in_b", part=(1, 2), earlier=win_a)
    big["w_in"] = [o.reshape(w_in.shape) for o in win_b]
    names = names + ["w_in"]

    lru_rows = lru_small.transpose(1, 0, 2).reshape(LRU_SMALL_ROWS, D)
    small_part = jnp.concatenate(
        [dh0[:n_meta], d_norm1_g, d_scale, d_norm2_g, d_final_g, lru_rows, jnp.zeros((4, D), F32)], axis=0)
    (small_all,) = _all_gather([small_part], "gather_small_grads")
    small_sum = _sum_slots(small_all, "sum_small_grads")
    o = n_meta
    g_meta_full = small_sum[:o]
    g_norm1, g_scale, g_norm2, g_final = (small_sum[o + k:o + k + 1] for k in range(4))
    g_ba_full, g_bx_full, g_lam, g_cb = (small_sum[o + 4 + k:o + 5 + k] for k in range(4))
    g_cw_full = small_sum[o + 8:o + 8 + CONV_WIDTH]
    dcol = D // N_DEV
    g_meta = lax.dynamic_slice_in_dim(g_meta_full, me * dcol, dcol, axis=1)
    g_cw = lax.dynamic_slice_in_dim(g_cw_full, me * dcol, dcol, axis=1)
    hcol = hd // N_DEV
    g_ba = lax.dynamic_slice_in_dim(g_ba_full.reshape(H, hd), me * hcol, hcol, axis=1)
    g_bx = lax.dynamic_slice_in_dim(g_bx_full.reshape(H, hd), me * hcol, hcol, axis=1)

    rep_w = jnp.concatenate([norm1_g, pool_scale, conv_b, lru_lambda, norm2_g, final_g.reshape(1, D)], axis=0)
    rep_g = jnp.concatenate([g_norm1, g_scale, g_cb, g_lam, g_norm2, g_final], axis=0)
    rep_m = jnp.concatenate([m_norm1_g, m_pool_scale, m_conv_b, m_lru_lambda, m_norm2_g, m_final_g.reshape(1, D)], axis=0)
    rep_v = jnp.concatenate([v_norm1_g, v_pool_scale, v_conv_b, v_lru_lambda, v_norm2_g, v_final_g.reshape(1, D)], axis=0)
    rep_d, rep_nm, rep_nv = _small_update(rep_w, rep_g, rep_m, rep_v, "update_vectors")
    col_w = jnp.concatenate([meta_tokens, conv_w[0]], axis=0)
    col_g = jnp.concatenate([g_meta, g_cw], axis=0)
    col_m = jnp.concatenate([m_meta_tokens, m_conv_w[0]], axis=0)
    col_v = jnp.concatenate([v_meta_tokens, v_conv_w[0]], axis=0)
    col_d, col_nm, col_nv = _small_update(col_w, col_g, col_m, col_v, "update_columns")
    b_w = jnp.concatenate([gate_a_b[0], gate_x_b[0]], axis=0)
    b_g = jnp.concatenate([g_ba, g_bx], axis=0)
    b_m = jnp.concatenate([m_gate_a_b[0], m_gate_x_b[0]], axis=0)
    b_v = jnp.concatenate([v_gate_a_b[0], v_gate_x_b[0]], axis=0)
    b_d, b_nm, b_nv = _small_update(b_w, b_g, b_m, b_v, "update_biases")

    def rep(arr, k, like):
        return arr[k:k + 1].reshape(like.shape)

    rep_order = {"norm1_g": 0, "pool_scale": 1, "conv_b": 2, "lru_lambda": 3, "norm2_g": 4, "final_g": 5}
    like = {"norm1_g": norm1_g, "pool_scale": pool_scale, "conv_b": conv_b, "lru_lambda": lru_lambda,
            "norm2_g": norm2_g, "final_g": final_g}

    def leaves(kind):
        rep_src = [rep_g, rep_d, rep_nm, rep_nv][kind]
        col_src = [col_g, col_d, col_nm, col_nv][kind]
        b_src = [b_g, b_d, b_nm, b_nv][kind]
        out = {}
        out["meta_tokens"] = col_src[:n_meta]
        out["conv_w"] = col_src[n_meta:][None]
        out["gate_a_b"] = b_src[:H][None]
        out["gate_x_b"] = b_src[H:][None]
        for nm, k in rep_order.items():
            out[nm] = rep(rep_src, k, like[nm])
        for nm in names:
            out[nm] = big[nm][kind]
        order = ["meta_tokens", "norm1_g", "w_in", "pool_w", "pool_scale", "conv_w", "conv_b", "gate_a_w", "gate_a_b",
                 "gate_x_w", "gate_x_b", "lru_lambda", "w_out", "norm2_g", "mlp_w1", "mlp_w2", "final_g"]
        return [out[nm] for nm in order]

    loss = lax.psum(loss_tile[0, 0], ("x", "y", "c"))
    return (loss, grad_x, *leaves(0), *leaves(1), *leaves(2), *leaves(3))
```

```python
import functools

import jax
import jax.numpy as jnp
from jax import lax
from jax.experimental import pallas as pl
from jax.experimental.pallas import tpu as pltpu

F32 = jnp.float32
BF16 = jnp.bfloat16
MESH = pl.DeviceIdType.MESH
N_DEV = 8
POOL_WINDOWS = (2, 4, 8, 16)
MAX_WINDOW = 16
CONV_WIDTH = 4
HALO = 8
LRU_C = 8.0
NORM_EPS = 1e-6
ADAM_LR, ADAM_B1, ADAM_B2, ADAM_EPS, ADAM_WD, ADAM_STEP = 0.001, 0.9, 0.999, 1e-08, 0.01, 10
ROW_ALIGN = 128
VMEM_LIMIT = 56 << 20
TILE = dict(norm=384, proj=1408, pool=384, lru=384, wout=384, mlp1=1408, mlp2=704, dact=1408, tn=1408,
            nt=704, dmerged=384, update=256, pair=1024)
MLP2_K = 1024
EPILOGUE_ROWS = 176
MID_STEP_PERCENT = 88

_NT = (((1,), (1,)), ((), ()))
_TN = (((0,), (0,)), ((), ()))


def _call(body, **kw):
    return pl.pallas_call(body, **kw)


def _cp(*sem):
    return pltpu.CompilerParams(dimension_semantics=sem, vmem_limit_bytes=VMEM_LIMIT)


def _tile(total, pref):
    best = None
    for t in range(16, min(total, pref) + 1, 16):
        if total % t == 0:
            best = t
    assert best is not None, (total, pref)
    return best


def _sds(shape, dtype):
    return jax.ShapeDtypeStruct(shape, dtype)


def _pos():
    return lax.axis_index("x"), lax.axis_index("y"), lax.axis_index("c")


def _all_gather(shards, name):
    n = len(shards)

    def body(*refs):
        ins, outs = refs[:n], refs[n:2 * n]
        send_sems, recv_sems, local_sems = refs[2 * n:]
        x, y, c = _pos()
        me, sib = (x, y, c), (x, y, 1 - c)
        chips = [(1 - x, y), (x, 1 - y), (1 - x, 1 - y)]

        def slot(p):
            return 4 * p[0] + 2 * p[1] + p[2]

        def copy(a, k, block, to, src=None):
            dst = outs[a].at[slot(block)]
            return pltpu.make_async_remote_copy(
                src_ref=dst if src is None else src, dst_ref=dst,
                send_sem=send_sems.at[7 * a + k], recv_sem=recv_sems.at[7 * a + k],
                device_id=to, device_id_type=MESH)

        mine = [pltpu.make_async_copy(ins[a], outs[a].at[slot(me)], local_sems.at[a]) for a in range(n)]
        for m in mine:
            m.start()
        first = []
        for a in range(n):
            first.append(copy(a, 0, me, sib, src=ins[a]))
            first += [copy(a, 1 + j, me, (*chip, c), src=ins[a]) for j, chip in enumerate(chips)]
        for cp in first:
            cp.start()
        passed = []
        for a in range(n):
            for j, chip in enumerate(chips):
                copy(a, 1 + j, (*chip, c), me).wait_recv()
                fwd = copy(a, 4 + j, (*chip, c), sib)
                fwd.start()
                passed.append(fwd)
        for a in range(n):
            copy(a, 0, sib, me).wait_recv()
            for j, chip in enumerate(chips):
                copy(a, 4 + j, (*chip, 1 - c), me).wait_recv()
        for cp in first + passed:
            cp.wait_send()
        for m in mine:
            m.wait()

    hbm = pl.BlockSpec(memory_space=pl.ANY)
    return _call(
        body, name=name,
        out_shape=[_sds((N_DEV,) + s.shape, s.dtype) for s in shards],
        in_specs=[hbm] * n, out_specs=[hbm] * n,
        scratch_shapes=[pltpu.SemaphoreType.DMA((7 * n,)), pltpu.SemaphoreType.DMA((7 * n,)),
                        pltpu.SemaphoreType.DMA((n,))],
    )(*shards)


def _other_chips(x, y):
    return [(1 - x, y), (x, 1 - y), (1 - x, 1 - y)]


class _AgFull:
    n_sem, n_local = 7, 1

    def __init__(self, shard, forward_here=True):
        self.forward_here = forward_here
        self.ins = [shard]
        self.out_shapes = [_sds((N_DEV,) + shard.shape, shard.dtype)]
        self.aliases = []

    def _peers(self):
        x, y, c = _pos()
        return [(x, y, 1 - c)] + [(*chip, c) for chip in _other_chips(x, y)]

    def _sends(self, ins, outs, sems):
        send, recv, _, base, _ = sems
        x, y, c = _pos()
        mine = outs[0].at[4 * x + 2 * y + c]
        return [pltpu.make_async_remote_copy(src_ref=ins[0], dst_ref=mine, send_sem=send.at[base + k],
                                             recv_sem=recv.at[base + k], device_id=p, device_id_type=MESH)
                for k, p in enumerate(self._peers())]

    def _arrivals(self, outs, sems):
        send, recv, _, base, _ = sems
        res = []
        for k, p in enumerate(self._peers()):
            blk = outs[0].at[4 * p[0] + 2 * p[1] + p[2]]
            res.append(pltpu.make_async_remote_copy(src_ref=blk, dst_ref=blk, send_sem=send.at[base + k],
                                                    recv_sem=recv.at[base + k], device_id=p, device_id_type=MESH))
        return res

    def _own(self, ins, outs, sems):
        x, y, c = _pos()
        return pltpu.make_async_copy(ins[0], outs[0].at[4 * x + 2 * y + c], sems[2].at[sems[4]])

    def _forwards(self, outs, sems, core_of_block):
        return _forward_copies(outs[0], sems[0], sems[1], sems[3] + 4, core_of_block)

    def start(self, ins, outs, sems):
        self._own(ins, outs, sems).start()
        for cp in self._sends(ins, outs, sems):
            cp.start()

    def mid(self, ins, outs, sems):
        if self.forward_here:
            for cp in self._arrivals(outs, sems)[1:]:
                cp.wait_recv()
            for cp in self._forwards(outs, sems, "mine"):
                cp.start()

    def finish(self, ins, outs, sems):
        if self.forward_here:
            self._arrivals(outs, sems)[0].wait_recv()
            for cp in self._forwards(outs, sems, "sibling"):
                cp.wait_recv()
            for cp in self._sends(ins, outs, sems) + self._forwards(outs, sems, "mine"):
                cp.wait_send()
        else:
            for cp in self._arrivals(outs, sems):
                cp.wait_recv()
            for cp in self._sends(ins, outs, sems):
                cp.wait_send()
        self._own(ins, outs, sems).wait()


def _forward_copies(gathered_ref, send, recv, base, core_of_block):
    x, y, c = _pos()
    res = []
    for k, chip in enumerate(_other_chips(x, y)):
        blk = gathered_ref.at[4 * chip[0] + 2 * chip[1] + (c if core_of_block == "mine" else 1 - c)]
        res.append(pltpu.make_async_remote_copy(src_ref=blk, dst_ref=blk, send_sem=send.at[base + k],
                                                recv_sem=recv.at[base + k], device_id=(x, y, 1 - c),
                                                device_id_type=MESH))
    return res


class _AgForward:
    n_sem, n_local = 3, 0

    def __init__(self, gathered):
        self.ins = [gathered]
        self.out_shapes = [_sds(gathered.shape, gathered.dtype)]
        self.aliases = [(0, 0)]

    def start(self, ins, outs, sems):
        for cp in _forward_copies(outs[0], sems[0], sems[1], sems[3], "mine"):
            cp.start()

    def finish(self, ins, outs, sems):
        for cp in _forward_copies(outs[0], sems[0], sems[1], sems[3], "sibling"):
            cp.wait_recv()
        for cp in _forward_copies(outs[0], sems[0], sems[1], sems[3], "mine"):
            cp.wait_send()


class _RsSibling:
    n_sem, n_local = 4, 0

    def __init__(self, part):
        self.ins = [part]
        self.out_shapes = [_sds((4,) + part.shape[1:], part.dtype)]
        self.aliases = []

    def _copies(self, ins, outs, sems):
        send, recv, _, base, _ = sems
        x, y, c = _pos()
        return [pltpu.make_async_remote_copy(src_ref=ins[0].at[2 * q + (1 - c)], dst_ref=outs[0].at[q],
                                             send_sem=send.at[base + q], recv_sem=recv.at[base + q],
                                             device_id=(x, y, 1 - c), device_id_type=MESH) for q in range(4)]

    def start(self, ins, outs, sems):
        for cp in self._copies(ins, outs, sems):
            cp.start()

    def finish(self, ins, outs, sems):
        for cp in self._copies(ins, outs, sems):
            cp.wait()


class _RsChips:
    n_sem, n_local = 3, 0

    def __init__(self, pair):
        self.ins = [pair]
        self.out_shapes = [_sds((3,) + pair.shape[1:], pair.dtype)]
        self.aliases = []

    def _copies(self, ins, outs, sems):
        send, recv, _, base, _ = sems
        x, y, c = _pos()
        return [pltpu.make_async_remote_copy(src_ref=ins[0].at[2 * chip[0] + chip[1]], dst_ref=outs[0].at[k],
                                             send_sem=send.at[base + k], recv_sem=recv.at[base + k],
                                             device_id=(*chip, c), device_id_type=MESH)
                for k, chip in enumerate(_other_chips(x, y))]

    def start(self, ins, outs, sems):
        for cp in self._copies(ins, outs, sems):
            cp.start()

    def finish(self, ins, outs, sems):
        for cp in self._copies(ins, outs, sems):
            cp.wait()


def _hosted(body, tasks, *, grid, in_specs, out_specs, out_shape, scratch_shapes=(), name, semantics, operands,
            aliases=None):
    in_specs, out_specs, out_shape = list(in_specs), list(out_specs), list(out_shape)
    scratch_shapes = list(scratch_shapes)
    aliases = dict(aliases or {})
    if not tasks:
        res = _call(body, name=name, grid=grid, in_specs=in_specs, out_specs=out_specs, out_shape=out_shape,
                    scratch_shapes=scratch_shapes, input_output_aliases=aliases,
                    compiler_params=_cp(*semantics))(*operands)
        return list(res), []
    n_in, n_out, n_scr = len(in_specs), len(out_specs), len(scratch_shapes)
    t_ins = [a for t in tasks for a in t.ins]
    t_outs = [o for t in tasks for o in t.out_shapes]
    i0, o0 = n_in, n_out
    for t in tasks:
        for (i, o) in t.aliases:
            aliases[i0 + i] = o0 + o
        i0 += len(t.ins)
        o0 += len(t.out_shapes)
    n_sem = sum(t.n_sem for t in tasks)
    n_local = max(1, sum(t.n_local for t in tasks))
    n_steps = 1
    for g in grid:
        n_steps *= g
    mid_step = min(n_steps - 1, (n_steps * MID_STEP_PERCENT) // 100)

    def wrapped(*refs):
        cut = [n_in, len(t_ins), n_out, len(t_outs), n_scr]
        parts, at = [], 0
        for n in cut:
            parts.append(refs[at:at + n])
            at += n
        ins, tin, outs, tout, scratch = parts
        send, recv, local = refs[at:]
        step = pl.program_id(0)
        for d in range(1, len(grid)):
            step = step * grid[d] + pl.program_id(d)

        def each(method):
            i, o, s, l = 0, 0, 0, 0
            for t in tasks:
                if hasattr(t, method):
                    getattr(t, method)(tin[i:i + len(t.ins)], tout[o:o + len(t.out_shapes)], (send, recv, local, s, l))
                i, o, s, l = i + len(t.ins), o + len(t.out_shapes), s + t.n_sem, l + t.n_local

        @pl.when(step == 0)
        def _():
            each("start")

        body(*ins, *outs, *scratch)

        @pl.when(step == mid_step)
        def _():
            each("mid")

        @pl.when(step == n_steps - 1)
        def _():
            each("finish")

    hbm = pl.BlockSpec(memory_space=pl.ANY)
    res = _call(
        wrapped, name=name, grid=grid,
        in_specs=in_specs + [hbm] * len(t_ins), out_specs=out_specs + [hbm] * len(t_outs),
        out_shape=out_shape + t_outs,
        scratch_shapes=scratch_shapes + [pltpu.SemaphoreType.DMA((n_sem,)), pltpu.SemaphoreType.DMA((n_sem,)),
                                         pltpu.SemaphoreType.DMA((n_local,))],
        input_output_aliases=aliases,
        compiler_params=_cp(*(["arbitrary"] * len(grid))),
    )(*operands, *t_ins)
    res = list(res)
    task_outs, o = [], n_out
    for t in tasks:
        task_outs.append(res[o:o + len(t.out_shapes)])
        o += len(t.out_shapes)
    return res[:n_out], task_outs


def _pair_sum(part, got, core, name):
    _, R, C = part.shape
    tr = _tile(R, TILE["pair"]) if R % 16 == 0 else R

    def body(core_ref, p_ref, g_ref, o_ref):
        o_ref[...] = (p_ref[...].astype(F32) + g_ref[...].astype(F32)).astype(o_ref.dtype)

    return _call(
        body, name=name,
        grid_spec=pltpu.PrefetchScalarGridSpec(
            num_scalar_prefetch=1, grid=(4, R // tr),
            in_specs=[pl.BlockSpec((None, tr, C), lambda q, i, cr: (2 * q + cr[0], i, 0)),
                      pl.BlockSpec((None, tr, C), lambda q, i, cr: (q, i, 0))],
            out_specs=pl.BlockSpec((None, tr, C), lambda q, i, cr: (q, i, 0))),
        out_shape=_sds((4, R, C), part.dtype),
        compiler_params=_cp("parallel", "parallel"),
    )(core, part, got)


def _sum_slots(gathered, name):
    _, R, C = gathered.shape

    def body(g_ref, o_ref):
        acc = g_ref[0]
        for s in range(1, N_DEV):
            acc = acc + g_ref[s]
        o_ref[...] = acc

    return _call(body, name=name, out_shape=_sds((R, C), F32))(gathered)


def _sigmoid(z):
    return jax.nn.sigmoid(z)


def _softplus(z):
    e = jnp.exp(-jnp.abs(z))
    log1p_e = jnp.where(e < 0.01, e * (1.0 - e * (0.5 - e * (1.0 / 3.0))), jnp.log(1.0 + e))
    return jnp.maximum(z, 0.0) + log1p_e


_GELU_K = 0.7978845608028654
_GELU_C = 0.044715


def _gelu_and_grad(z):
    t = jnp.tanh(_GELU_K * (z + _GELU_C * z * z * z))
    g = 0.5 * z * (1.0 + t)
    dg = 0.5 * (1.0 + t) + 0.5 * z * (1.0 - t * t) * _GELU_K * (1.0 + 3.0 * _GELU_C * z * z)
    return g, dg


def _gelu(z):
    t = jnp.tanh(_GELU_K * (z + _GELU_C * z * z * z))
    return 0.5 * z * (1.0 + t)


def _row_ids(tile_index, tm, width=1):
    return tile_index * tm + lax.broadcasted_iota(jnp.int32, (tm, width), 0)


def _shift_down(prev, cur, k):
    if k == 0:
        return cur
    ext = jnp.concatenate([prev, cur], axis=0)
    return pltpu.roll(ext, k, axis=0)[prev.shape[0]:]


def _shift_up(cur, nxt, k):
    if k == 0:
        return cur
    ext = jnp.concatenate([cur, nxt], axis=0)
    return pltpu.roll(ext, ext.shape[0] - k, axis=0)[:cur.shape[0]]


def _lru_gates(r, sp):
    z = LRU_C * r * sp
    a = jnp.exp(-z)
    t = jnp.tanh(z)
    mult = jnp.sqrt(2.0 * t / (1.0 + t))
    return a, mult


def _scan_chunks(a_ref, b_ref, out_ref, carry, n_rows, reverse):
    n_chunks = n_rows // 8
    cols = a_ref.shape[1]
    rid = lax.broadcasted_iota(jnp.int32, (8, cols), 0)
    edge = 0 if reverse else 7

    def chunk(k, h):
        ci = (n_chunks - 1 - k) if reverse else k
        rows = pl.ds(pl.multiple_of(ci * 8, 8), 8)
        a = a_ref[rows, :]
        b = b_ref[rows, :]
        for s in (1, 2, 4):
            if reverse:
                keep = rid < 8 - s
                a_n, b_n = pltpu.roll(a, 8 - s, axis=0), pltpu.roll(b, 8 - s, axis=0)
            else:
                keep = rid >= s
                a_n, b_n = pltpu.roll(a, s, axis=0), pltpu.roll(b, s, axis=0)
            b = a * jnp.where(keep, b_n, 0.0) + b
            a = a * jnp.where(keep, a_n, 1.0)
        out_ref[rows, :] = a * h + b
        a_e = jnp.sum(jnp.where(rid == edge, a, 0.0), axis=0, keepdims=True)
        b_e = jnp.sum(jnp.where(rid == edge, b, 0.0), axis=0, keepdims=True)
        return a_e * h + b_e

    return lax.fori_loop(0, n_chunks, chunk, carry, unroll=4 if n_chunks % 4 == 0 else 1)


def _norm_fwd(h, g, name):
    Tp, D = h.shape
    tm = _tile(Tp, TILE["norm"])

    def body(h_ref, g_ref, u_ref, r_ref):
        x = h_ref[...]
        r = lax.rsqrt(jnp.mean(x * x, axis=-1, keepdims=True) + NORM_EPS)
        u_ref[...] = (x * r * g_ref[...]).astype(BF16)
        r_ref[...] = r

    return _call(
        body, name=name, grid=(Tp // tm,),
        in_specs=[pl.BlockSpec((tm, D), lambda i: (i, 0)), pl.BlockSpec((1, D), lambda i: (0, 0))],
        out_specs=[pl.BlockSpec((tm, D), lambda i: (i, 0)), pl.BlockSpec((tm, 1), lambda i: (i, 0))],
        out_shape=[_sds((Tp, D), BF16), _sds((Tp, 1), F32)],
        compiler_params=_cp("parallel"),
    )(h, g)


def _proj_fwd(u, w_slots, name, tasks=(), part=(0, 1), earlier=None):
    Tp, K = u.shape
    S, _, n = w_slots.shape
    p, parts = part
    tm = _tile(Tp, TILE["proj"])

    def body(a_ref, b_ref, *rest):
        o_ref = rest[-1]
        o_ref[...] = jnp.dot(a_ref[...], b_ref[...], preferred_element_type=F32)

    in_specs = [pl.BlockSpec((tm, K), lambda j, i: (i, 0)), pl.BlockSpec((None, K, n), lambda j, i: (j, 0, 0))]
    operands = (u, w_slots)
    aliases = {}
    if earlier is not None:
        in_specs.append(pl.BlockSpec(memory_space=pl.ANY))
        operands += (earlier,)
        aliases = {2: 0}
    (proj,), extra = _hosted(
        body, tasks, name=name, grid=(S, Tp // tm), in_specs=in_specs,
        out_specs=[pl.BlockSpec((tm, n), lambda j, i: (i, j * parts + p))],
        out_shape=[_sds((Tp, S * parts * n), F32)],
        semantics=("parallel", "parallel"), operands=operands, aliases=aliases)
    return proj, extra


def _mlp1_fwd(u2, w_slots, name, tasks=()):
    Tp, K = u2.shape
    S, _, n = w_slots.shape
    tm = _tile(Tp, TILE["mlp1"])

    def body(a_ref, b_ref, act_ref, a1_ref):
        a1 = jnp.dot(a_ref[...], b_ref[...], preferred_element_type=F32)
        relu = jnp.maximum(a1, 0.0)
        act_ref[...] = (relu * relu).astype(BF16)
        a1_ref[...] = a1.astype(BF16)

    return _hosted(
        body, tasks, name=name, grid=(S, Tp // tm),
        in_specs=[pl.BlockSpec((tm, K), lambda j, i: (i, 0)), pl.BlockSpec((None, K, n), lambda j, i: (j, 0, 0))],
        out_specs=[pl.BlockSpec((tm, n), lambda j, i: (i, j))] * 2,
        out_shape=[_sds((Tp, S * n), BF16)] * 2,
        semantics=("parallel", "parallel"), operands=(u2, w_slots))


def _pool_fwd(proj, pool_w, name):
    Tp = proj.shape[0]
    G, Cg, _ = pool_w.shape
    D = G * Cg
    tm = _tile(Tp, TILE["pool"])

    def body(v_ref, w_ref, d_ref, y_ref, prev_ref):
        t = pl.program_id(0)

        @pl.when(t == 0)
        def _():
            prev_ref[...] = jnp.zeros_like(prev_ref)

        rows = _row_ids(t, tm)
        for g, win in enumerate(POOL_WINDOWS):
            cols = slice(g * Cg, (g + 1) * Cg)
            v = v_ref[:, cols]
            s = jnp.concatenate([prev_ref[:, cols], v], axis=0)
            k = 1
            while k < win:
                s = s + pltpu.roll(s, k, axis=0)
                k *= 2
            cnt = jnp.minimum(rows + 1, win).astype(F32)
            d = s[MAX_WINDOW:] / cnt - v
            d_ref[:, cols] = d.astype(BF16)
            y_ref[:, cols] = jnp.dot(d.astype(BF16), w_ref[g], preferred_element_type=F32)
        prev_ref[...] = v_ref[tm - MAX_WINDOW:, :]

    return _call(
        body, name=name, grid=(Tp // tm,),
        in_specs=[pl.BlockSpec((tm, D), lambda t: (t, 0)), pl.BlockSpec((G, Cg, Cg), lambda t: (0, 0, 0))],
        out_specs=[pl.BlockSpec((tm, D), lambda t: (t, 0))] * 2,
        out_shape=[_sds((Tp, D), BF16), _sds((Tp, D), F32)],
        scratch_shapes=[pltpu.VMEM((MAX_WINDOW, D), F32)],
        compiler_params=_cp("arbitrary"),
    )(proj, pool_w)


def _lru_fwd(proj, y_pool, scale, conv_w, conv_b, wa, ba, wx, bx, lam, name, tasks=()):
    Tp = proj.shape[0]
    H, hd, _ = wa.shape
    D = H * hd
    tm = _tile(Tp, TILE["lru"])
    nb = D // hd

    def body(vl_ref, vg_ref, gp_ref, gl_ref, y_ref, sc_ref, cw_ref, cb_ref, wa_ref, ba_ref, wx_ref, bx_ref,
             lam_ref, xc_ref, r_ref, i_ref, a_ref, mult_ref, hs_ref, m_ref, prev_ref, carry_ref, b_s):
        t = pl.program_id(1)

        @pl.when(t == 0)
        def _():
            prev_ref[...] = jnp.zeros_like(prev_ref)
            carry_ref[...] = jnp.zeros_like(carry_ref)

        v = vl_ref[...]
        prev = prev_ref[...]
        xc = jnp.zeros_like(v) + cb_ref[...]
        for k in range(CONV_WIDTH):
            xc = xc + cw_ref[k:k + 1, :] * _shift_down(prev, v, CONV_WIDTH - 1 - k)
        prev_ref[...] = v[tm - HALO:, :]
        xcb = xc.astype(BF16)
        r = _sigmoid(jnp.dot(xcb, wa_ref[...], preferred_element_type=F32) + ba_ref[...])
        i = _sigmoid(jnp.dot(xcb, wx_ref[...], preferred_element_type=F32) + bx_ref[...])
        a, mult = _lru_gates(r, _softplus(-lam_ref[...]))
        a_ref[...] = a
        mult_ref[...] = mult
        b_s[...] = mult * (i * xc)
        xc_ref[...] = xc
        r_ref[...] = r
        i_ref[...] = i
        carry_ref[0:1, :] = _scan_chunks(a_ref, b_s, hs_ref, carry_ref[0:1, :], tm, reverse=False)
        lru_out = hs_ref[...] * _gelu(vg_ref[...])
        pool_out = y_ref[...] * sc_ref[...]
        m_ref[...] = (_sigmoid(gp_ref[...]) * pool_out + _sigmoid(gl_ref[...]) * lru_out).astype(BF16)

    def piece(p):
        return pl.BlockSpec((tm, hd), lambda h, t: (t, p * nb + h))

    blk = pl.BlockSpec((tm, hd), lambda h, t: (t, h))
    vec = pl.BlockSpec((1, hd), lambda h, t: (0, h))
    mat = pl.BlockSpec((None, hd, hd), lambda h, t: (h, 0, 0))
    bias = pl.BlockSpec((None, 1, hd), lambda h, t: (h, 0, 0))
    return _hosted(
        body, tasks, name=name, grid=(H, Tp // tm),
        in_specs=[piece(1), piece(2), piece(3), piece(4), blk, vec,
                  pl.BlockSpec((CONV_WIDTH, hd), lambda h, t: (0, h)), vec, mat, bias, mat, bias, vec],
        out_specs=[blk] * 7,
        out_shape=[_sds((Tp, D), F32)] * 6 + [_sds((Tp, D), BF16)],
        scratch_shapes=[pltpu.VMEM((HALO, hd), F32), pltpu.VMEM((8, hd), F32), pltpu.VMEM((tm, hd), F32)],
        semantics=("parallel", "arbitrary"),
        operands=(proj, proj, proj, proj, y_pool, scale, conv_w, conv_b, wa, ba.reshape(H, 1, hd), wx,
                  bx.reshape(H, 1, hd), lam))


def _wout_norm_fwd(merged, w_out, h0, g2, name, tasks=()):
    Tp, D = h0.shape
    tm = _tile(Tp, TILE["wout"])

    def body(m_ref, w_ref, h0_ref, g_ref, h1_ref, u2_ref, r2_ref):
        h1 = h0_ref[...] + jnp.dot(m_ref[...], w_ref[...], preferred_element_type=F32)
        r = lax.rsqrt(jnp.mean(h1 * h1, axis=-1, keepdims=True) + NORM_EPS)
        h1_ref[...] = h1
        u2_ref[...] = (h1 * r * g_ref[...]).astype(BF16)
        r2_ref[...] = r

    row = pl.BlockSpec((tm, D), lambda i: (i, 0))
    return _hosted(
        body, tasks, name=name, grid=(Tp // tm,),
        in_specs=[row, pl.BlockSpec((D, D), lambda i: (0, 0)), row, pl.BlockSpec((1, D), lambda i: (0, 0))],
        out_specs=[row, row, pl.BlockSpec((tm, 1), lambda i: (i, 0))],
        out_shape=[_sds((Tp, D), F32), _sds((Tp, D), BF16), _sds((Tp, 1), F32)],
        semantics=("parallel",), operands=(merged, w_out, h0, g2))


def _mlp2_loss(act, w2, h1, target, gf, n_meta, seq, name):
    Tp, D = h1.shape
    K = act.shape[1]
    tm = _tile(Tp, TILE["mlp2"])
    tk = min(K, MLP2_K)
    nk = K // tk

    rc = _tile(tm, EPILOGUE_ROWS)

    def body(a_ref, w_ref, h1_hbm, t_hbm, g_ref, dh_ref, dhb_ref, loss_ref, dg_ref, h1_buf, t_buf, sems):
        i, k = pl.program_id(0), pl.program_id(1)
        tile_rows = pl.ds(pl.multiple_of(i * tm, tm), tm)
        fetch = [pltpu.make_async_copy(h1_hbm.at[tile_rows, :], h1_buf, sems.at[0]),
                 pltpu.make_async_copy(t_hbm.at[tile_rows, :], t_buf, sems.at[1])]

        @pl.when(k == 0)
        def _():
            for f in fetch:
                f.start()
            dh_ref[...] = jnp.zeros_like(dh_ref)

        @pl.when((i == 0) & (k == 0))
        def _():
            loss_ref[...] = jnp.zeros_like(loss_ref)
            dg_ref[...] = jnp.zeros_like(dg_ref)

        dh_ref[...] += jnp.dot(a_ref[...], w_ref[...], preferred_element_type=F32)

        @pl.when(k == nk - 1)
        def _():
            for f in fetch:
                f.wait()
            g = g_ref[...]

            def chunk(c, carry):
                loss_acc, dg_acc = carry
                rows = pl.ds(pl.multiple_of(c * rc, rc), rc)
                h2 = h1_buf[rows, :] + dh_ref[rows, :]
                r = lax.rsqrt(jnp.mean(h2 * h2, axis=-1, keepdims=True) + NORM_EPS)
                out = h2 * r * g
                row_id = i * tm + c * rc + lax.broadcasted_iota(jnp.int32, (rc, 1), 0)
                valid = (row_id >= n_meta) & (row_id < n_meta + seq)
                diff = jnp.where(valid, out - t_buf[rows, :], 0.0)
                dout = diff / D
                dog = dout * g
                dh = r * dog - h2 * (r * r * r * jnp.mean(dog * h2, axis=-1, keepdims=True))
                dh_ref[rows, :] = dh
                dhb_ref[rows, :] = dh.astype(BF16)
                loss_acc = loss_acc + 0.5 * jnp.sum(jnp.mean(diff * diff, axis=-1, keepdims=True), axis=0, keepdims=True)
                return loss_acc, dg_acc + jnp.sum(dout * (h2 * r), axis=0, keepdims=True)

            loss_sum, dg_sum = lax.fori_loop(0, tm // rc, chunk, (jnp.zeros((1, 1), F32), jnp.zeros((1, D), F32)))
            loss_ref[...] += loss_sum
            dg_ref[...] += dg_sum

    row = pl.BlockSpec((tm, D), lambda i, k: (i, 0))
    hbm = pl.BlockSpec(memory_space=pl.ANY)
    return _call(
        body, name=name, grid=(Tp // tm, nk),
        in_specs=[pl.BlockSpec((tm, tk), lambda i, k: (i, k)), pl.BlockSpec((tk, D), lambda i, k: (k, 0)),
                  hbm, hbm, pl.BlockSpec((1, D), lambda i, k: (0, 0))],
        out_specs=[row, row, pl.BlockSpec((8, 128), lambda i, k: (0, 0)), pl.BlockSpec((1, D), lambda i, k: (0, 0))],
        out_shape=[_sds((Tp, D), F32), _sds((Tp, D), BF16), _sds((8, 128), F32), _sds((1, D), F32)],
        scratch_shapes=[pltpu.VMEM((tm, D), F32), pltpu.VMEM((tm, D), F32), pltpu.SemaphoreType.DMA((2,))],
        compiler_params=_cp("arbitrary", "arbitrary"),
    )(act, w2, h1, target, gf)


def _dact_bwd(dh2b, w2_slots, a1, name):
    Tp, D = dh2b.shape
    S, n, _ = w2_slots.shape
    tm = _tile(Tp, TILE["dact"])

    def body(g_ref, w_ref, a1_ref, o_ref):
        dact = lax.dot_general(g_ref[...], w_ref[...], _NT, preferred_element_type=F32)
        o_ref[...] = (dact * (2.0 * jnp.maximum(a1_ref[...].astype(F32), 0.0))).astype(BF16)

    return _call(
        body, name=name, grid=(S, Tp // tm),
        in_specs=[pl.BlockSpec((tm, D), lambda j, i: (i, 0)), pl.BlockSpec((None, n, D), lambda j, i: (j, 0, 0)),
                  pl.BlockSpec((tm, n), lambda j, i: (i, j))],
        out_specs=pl.BlockSpec((tm, n), lambda j, i: (i, j)),
        out_shape=_sds((Tp, S * n), BF16),
        compiler_params=_cp("parallel", "parallel"),
    )(dh2b, w2_slots, a1)


def _weight_grad(a, g, blocks, block_a, name, tasks=(), part=(0, 1)):
    Tp, Ka = a.shape
    Ng = g.shape[1]
    p, parts = part
    assert parts == 1 or not block_a
    ka = Ka // blocks if block_a else Ka // parts
    ng = Ng if block_a else Ng // blocks
    tt = _tile(Tp, TILE["tn"])
    nt = Tp // tt

    def body(a_ref, g_ref, o_ref, acc_ref):
        t = pl.program_id(1)

        @pl.when(t == 0)
        def _():
            acc_ref[...] = jnp.zeros_like(acc_ref)

        acc_ref[...] += lax.dot_general(a_ref[...], g_ref[...], _TN, preferred_element_type=F32)

        @pl.when(t == nt - 1)
        def _():
            o_ref[...] = acc_ref[...].astype(o_ref.dtype)

    if block_a:
        a_spec = pl.BlockSpec((tt, ka), lambda j, t: (t, j))
        g_spec = pl.BlockSpec((tt, ng), lambda j, t: (t, 0))
    else:
        a_spec = pl.BlockSpec((tt, ka), lambda j, t: (t, p))
        g_spec = pl.BlockSpec((tt, ng), lambda j, t: (t, j))
    (dw,), extra = _hosted(
        body, tasks, name=name, grid=(blocks, nt),
        in_specs=[a_spec, g_spec],
        out_specs=[pl.BlockSpec((None, ka, ng), lambda j, t: (j, 0, 0))],
        out_shape=[_sds((blocks, ka, ng), BF16)],
        scratch_shapes=[pltpu.VMEM((ka, ng), F32)],
        semantics=("parallel", "arbitrary"), operands=(a, g))
    return dw, extra


def _nt_norm_bwd(dz, w_parts, dres, hin, rin, g, want_bf16, name, tasks=(), tiles=None, earlier=None):
    Tp, D = hin.shape
    P = len(w_parts)
    S, _, n = w_parts[0].shape
    K = S * P
    tm = _tile(Tp, TILE["nt"])
    t0, nt = tiles if tiles is not None else (0, Tp // tm)
    assert not (want_bf16 and earlier is not None)

    rc = _tile(tm, EPILOGUE_ROWS)

    def body(dz_ref, *rest):
        w_refs, (dres_hbm, h_hbm, r_ref, g_ref), rest = rest[:P], rest[P:P + 4], rest[P + 4:]
        if earlier is not None:
            _, dg0_ref, dh_ref, dg_ref, dres_buf, h_buf, sems = rest
        elif want_bf16:
            dh_ref, dhb_ref, dg_ref, dres_buf, h_buf, sems = rest
        else:
            dh_ref, dg_ref, dres_buf, h_buf, sems = rest
        i, k = pl.program_id(0), pl.program_id(1)
        tile_rows = pl.ds(pl.multiple_of((t0 + i) * tm, tm), tm)
        fetch = [pltpu.make_async_copy(dres_hbm.at[tile_rows, :], dres_buf, sems.at[0]),
                 pltpu.make_async_copy(h_hbm.at[tile_rows, :], h_buf, sems.at[1])]

        @pl.when(k == 0)
        def _():
            for f in fetch:
                f.start()
            dh_ref[...] = jnp.zeros_like(dh_ref)

        @pl.when((i == 0) & (k == 0))
        def _():
            dg_ref[...] = jnp.zeros_like(dg_ref) if earlier is None else dg0_ref[...]

        for q in range(P):
            @pl.when(k % P == q)
            def _(q=q):
                dh_ref[...] += lax.dot_general(dz_ref[...], w_refs[q][...], _NT, preferred_element_type=F32)

        @pl.when(k == K - 1)
        def _():
            for f in fetch:
                f.wait()
            g = g_ref[...]

            def chunk(c, dg_acc):
                rows = pl.ds(pl.multiple_of(c * rc, rc), rc)
                du = dh_ref[rows, :]
                h = h_buf[rows, :]
                r = r_ref[rows, :]
                dug = du * g
                dh = dres_buf[rows, :] + r * dug - h * (r * r * r * jnp.mean(dug * h, axis=-1, keepdims=True))
                dh_ref[rows, :] = dh
                if want_bf16:
                    dhb_ref[rows, :] = dh.astype(BF16)
                return dg_acc + jnp.sum(du * (h * r), axis=0, keepdims=True)

            dg_ref[...] += lax.fori_loop(0, tm // rc, chunk, jnp.zeros((1, D), F32))

    row = pl.BlockSpec((tm, D), lambda i, k: (t0 + i, 0))
    vec = pl.BlockSpec((1, D), lambda i, k: (0, 0))
    hbm = pl.BlockSpec(memory_space=pl.ANY)
    out_specs = [row] + ([row] if want_bf16 else []) + [vec]
    out_shape = [_sds((Tp, D), F32)] + ([_sds((Tp, D), BF16)] if want_bf16 else []) + [_sds((1, D), F32)]
    in_specs = ([pl.BlockSpec((tm, n), lambda i, k: (t0 + i, k))]
                + [pl.BlockSpec((None, D, n), lambda i, k: (k // P, 0, 0))] * P
                + [hbm, hbm, pl.BlockSpec((tm, 1), lambda i, k: (t0 + i, 0)), vec])
    operands = (dz, *w_parts, dres, hin, rin, g)
    aliases = {}
    if earlier is not None:
        in_specs += [hbm, vec]
        operands += tuple(earlier)
        aliases = {P + 5: 0}
    return _hosted(
        body, tasks, name=name, grid=(nt, K), in_specs=in_specs, out_specs=out_specs, out_shape=out_shape,
        scratch_shapes=[pltpu.VMEM((tm, D), F32), pltpu.VMEM((tm, D), F32), pltpu.SemaphoreType.DMA((2,))],
        semantics=("arbitrary", "arbitrary"), operands=operands, aliases=aliases)


def _dmerged_bwd(dh1b, w_out, name):
    Tp, D = dh1b.shape
    tm = _tile(Tp, TILE["dmerged"])

    def body(g_ref, w_ref, o_ref):
        o_ref[...] = lax.dot_general(g_ref[...], w_ref[...], _NT, preferred_element_type=F32)

    row = pl.BlockSpec((tm, D), lambda i: (i, 0))
    return _call(
        body, name=name, grid=(Tp // tm,),
        in_specs=[row, pl.BlockSpec((D, D), lambda i: (0, 0))],
        out_specs=row, out_shape=_sds((Tp, D), F32),
        compiler_params=_cp("parallel"),
    )(dh1b, w_out)


def _pool_bwd(dmerged, proj, y_pool, d_pool, scale, pool_w, name, tasks=()):
    Tp, D = dmerged.shape
    G, Cg, _ = pool_w.shape
    tm = _tile(Tp, TILE["pool"])
    nt = Tp // tm

    def body(dm_ref, gp_ref, y_ref, d_ref, sc_ref, w_ref, dproj_hbm, dw_ref, dsc_ref, next_ref, out_buf, out_sems):
        t = pl.program_id(0)
        tile = nt - 1 - t
        slot = t % 2
        dv_ref, dgp_ref = out_buf.at[slot, 0], out_buf.at[slot, 1]
        tile_rows = pl.ds(pl.multiple_of(tile * tm, tm), tm)

        def out_copies(s):
            return [pltpu.make_async_copy(out_buf.at[s, k], dproj_hbm.at[tile_rows, pl.ds(piece * D, D)],
                                          out_sems.at[2 * s + k]) for k, piece in enumerate((0, 3))]

        @pl.when(t >= 2)
        def _():
            for cp in out_copies(slot):
                cp.wait()

        @pl.when(t == 0)
        def _():
            next_ref[...] = jnp.zeros_like(next_ref)
            dw_ref[...] = jnp.zeros_like(dw_ref)
            dsc_ref[...] = jnp.zeros_like(dsc_ref)

        rows = _row_ids(tile, tm)
        dm = dm_ref[...]
        y = y_ref[...]
        sc = sc_ref[...]
        sg = _sigmoid(gp_ref[...])
        dpo = dm * sg
        dgp_ref[...] = (dm * (y * sc) * sg * (1.0 - sg)).astype(BF16)
        dsc_ref[...] += jnp.sum(dpo * y, axis=0, keepdims=True)
        dyb = (dpo * sc).astype(BF16)
        for g, win in enumerate(POOL_WINDOWS):
            cols = slice(g * Cg, (g + 1) * Cg)
            dy = dyb[:, cols]
            dd = lax.dot_general(dy, w_ref[g], _NT, preferred_element_type=F32)
            dw_ref[g] += lax.dot_general(d_ref[:, cols], dy, _TN, preferred_element_type=F32)
            q = dd / jnp.minimum(rows + 1, win).astype(F32)
            s = jnp.concatenate([q, next_ref[:, cols]], axis=0)
            k = 1
            while k < win:
                s = s + pltpu.roll(s, s.shape[0] - k, axis=0)
                k *= 2
            dv_ref[:, cols] = (s[:tm] - dd).astype(BF16)
            next_ref[:, cols] = q[:MAX_WINDOW]
        for cp in out_copies(slot):
            cp.start()

        @pl.when(t == nt - 1)
        def _():
            for cp in out_copies(slot) + (out_copies(1 - slot) if nt > 1 else []):
                cp.wait()

    row = pl.BlockSpec((tm, D), lambda t: (nt - 1 - t, 0))
    return _hosted(
        body, tasks, name=name, grid=(nt,),
        in_specs=[row, pl.BlockSpec((tm, D), lambda t: (nt - 1 - t, 3)), row, row,
                  pl.BlockSpec((1, D), lambda t: (0, 0)), pl.BlockSpec((G, Cg, Cg), lambda t: (0, 0, 0))],
        out_specs=[pl.BlockSpec(memory_space=pl.ANY), pl.BlockSpec((G, Cg, Cg), lambda t: (0, 0, 0)),
                   pl.BlockSpec((1, D), lambda t: (0, 0))],
        out_shape=[_sds((Tp, proj.shape[1]), BF16), _sds((G, Cg, Cg), F32), _sds((1, D), F32)],
        scratch_shapes=[pltpu.VMEM((MAX_WINDOW, D), F32), pltpu.VMEM((2, 2, tm, D), BF16),
                        pltpu.SemaphoreType.DMA((4,))],
        semantics=("arbitrary",), operands=(dmerged, proj, y_pool, d_pool, scale, pool_w))


LRU_SMALL_ROWS = 8


def _lru_bwd(dmerged, proj, xc, r_gate, i_gate, a_gate, mult_gate, hs, lam, conv_w, wa, wx, dproj, name, tasks=()):
    Tp, D = dmerged.shape
    H, hd, _ = wa.shape
    tm = _tile(Tp, TILE["lru"])
    nt = Tp // tm
    nb = D // hd
    halo_blocks = tm // HALO

    def body(dm_ref, vl_ref, vg_ref, gl_ref, xc_ref, r_ref, i_ref, a_ref, mult_ref, hs_ref, hsp_ref, lam_ref, cw_ref,
             wa_ref, wx_ref, _, dproj_hbm, dwa_ref, dwx_ref, small_ref,
             mu_next_ref, dxc_next_ref, q_s, mu_s, out_buf, out_sems):
        h_id, t = pl.program_id(0), pl.program_id(1)
        tile = nt - 1 - t
        step = h_id * nt + t
        slot = step % 2
        dvl_ref, dvg_ref, dgl_ref = out_buf.at[slot, 0], out_buf.at[slot, 1], out_buf.at[slot, 2]
        tile_rows = pl.ds(pl.multiple_of(tile * tm, tm), tm)

        def out_copies(s):
            return [pltpu.make_async_copy(
                out_buf.at[s, k], dproj_hbm.at[tile_rows, pl.ds(pl.multiple_of((piece * nb + h_id) * hd, hd), hd)],
                out_sems.at[3 * s + k]) for k, piece in enumerate((1, 2, 4))]

        @pl.when(step >= 2)
        def _():
            for cp in out_copies(slot):
                cp.wait()

        @pl.when(t == 0)
        def _():
            mu_next_ref[...] = jnp.zeros_like(mu_next_ref)
            dxc_next_ref[...] = jnp.zeros_like(dxc_next_ref)
            dwa_ref[...] = jnp.zeros_like(dwa_ref)
            dwx_ref[...] = jnp.zeros_like(dwx_ref)
            small_ref[...] = jnp.zeros_like(small_ref)

        first = tile == 0
        dm = dm_ref[...]
        hs_t = hs_ref[...]
        xc_t = xc_ref[...]
        r = r_ref[...]
        i = i_ref[...]
        lam_v = lam_ref[...]
        sp = _softplus(-lam_v)
        a = a_ref[...]
        mult = mult_ref[...]

        sg = _sigmoid(gl_ref[...])
        ge, dge = _gelu_and_grad(vg_ref[...])
        dlo = dm * sg
        dgl_ref[...] = (dm * (hs_t * ge) * sg * (1.0 - sg)).astype(BF16)
        dvg_ref[...] = (dlo * hs_t * dge).astype(BF16)
        dhs = dlo * ge

        q_s[...] = a * dhs
        mu_first = _scan_chunks(a_ref, q_s, mu_s, mu_next_ref[0:1, :], tm, reverse=True)
        lam_t = dhs + _shift_up(mu_s[...], mu_next_ref[...], 1)
        mu_next_ref[...] = jnp.broadcast_to(mu_first, mu_next_ref.shape)

        h_prev = _shift_down(jnp.where(first, 0.0, hsp_ref[...]), hs_t, 1)
        da = lam_t * h_prev
        dmult = lam_t * (i * xc_t)
        di = lam_t * mult * xc_t
        dxc = lam_t * mult * i
        dlog_a = da * a - dmult * (a * a) / mult
        dr = dlog_a * (-LRU_C * sp)
        dlam_rows = dlog_a * (-LRU_C * r)
        dza = dr * r * (1.0 - r)
        dzx = di * i * (1.0 - i)
        dzab, dzxb = dza.astype(BF16), dzx.astype(BF16)
        xcb = xc_t.astype(BF16)
        dxc = dxc + lax.dot_general(dzab, wa_ref[...], _NT, preferred_element_type=F32)
        dxc = dxc + lax.dot_general(dzxb, wx_ref[...], _NT, preferred_element_type=F32)
        dwa_ref[...] += lax.dot_general(xcb, dzab, _TN, preferred_element_type=F32)
        dwx_ref[...] += lax.dot_general(xcb, dzxb, _TN, preferred_element_type=F32)

        dxc_next = dxc_next_ref[...]
        taps = [_shift_up(dxc, dxc_next, CONV_WIDTH - 1 - k) for k in range(CONV_WIDTH)]
        dv = jnp.zeros_like(dxc)
        for k in range(CONV_WIDTH):
            dv = dv + cw_ref[k:k + 1, :] * taps[k]
        dvl_ref[...] = dv.astype(BF16)
        dxc_next_ref[...] = dxc[:HALO, :]

        v_t = vl_ref[...]
        small = [jnp.sum(dza, axis=0, keepdims=True), jnp.sum(dzx, axis=0, keepdims=True),
                 jnp.sum(dlam_rows, axis=0, keepdims=True) * (-_sigmoid(-lam_v)),
                 jnp.sum(dxc, axis=0, keepdims=True)]
        for k in range(CONV_WIDTH):
            small.append(jnp.sum(taps[k] * v_t, axis=0, keepdims=True))
        for k, row in enumerate(small):
            small_ref[k:k + 1, :] += row
        for cp in out_copies(slot):
            cp.start()

        @pl.when(step == H * nt - 1)
        def _():
            for cp in out_copies(slot) + (out_copies(1 - slot) if H * nt > 1 else []):
                cp.wait()

    def piece(p):
        return pl.BlockSpec((tm, hd), lambda h, t: (nt - 1 - t, p * nb + h))

    def halo(p):
        return pl.BlockSpec((HALO, hd), lambda h, t: (jnp.maximum((nt - 1 - t) * halo_blocks - 1, 0), p * nb + h))

    blk = pl.BlockSpec((tm, hd), lambda h, t: (nt - 1 - t, h))
    vec = pl.BlockSpec((1, hd), lambda h, t: (0, h))
    mat = pl.BlockSpec((None, hd, hd), lambda h, t: (h, 0, 0))
    return _hosted(
        body, tasks, name=name, grid=(H, nt),
        in_specs=[blk, piece(1), piece(2), piece(4), blk, blk, blk, blk, blk, blk, halo(0), vec,
                  pl.BlockSpec((CONV_WIDTH, hd), lambda h, t: (0, h)), mat, mat, pl.BlockSpec(memory_space=pl.ANY)],
        out_specs=[pl.BlockSpec(memory_space=pl.ANY), mat, mat,
                   pl.BlockSpec((None, LRU_SMALL_ROWS, hd), lambda h, t: (h, 0, 0))],
        out_shape=[_sds(dproj.shape, BF16)] + [_sds((H, hd, hd), F32)] * 2 + [_sds((H, LRU_SMALL_ROWS, hd), F32)],
        scratch_shapes=[pltpu.VMEM((HALO, hd), F32), pltpu.VMEM((HALO, hd), F32),
                        pltpu.VMEM((tm, hd), F32), pltpu.VMEM((tm, hd), F32),
                        pltpu.VMEM((2, 3, tm, hd), BF16), pltpu.SemaphoreType.DMA((6,))],
        semantics=("arbitrary", "arbitrary"), aliases={15: 0},
        operands=(dmerged, proj, proj, proj, xc, r_gate, i_gate, a_gate, mult_gate, hs, hs, lam, conv_w, wa, wx,
                  dproj))


def _adamw(w, g, m, v):
    m = ADAM_B1 * m + (1.0 - ADAM_B1) * g
    v = ADAM_B2 * v + (1.0 - ADAM_B2) * (g * g)
    m_hat = m / (1.0 - ADAM_B1 ** ADAM_STEP)
    v_hat = v / (1.0 - ADAM_B2 ** ADAM_STEP)
    delta = -ADAM_LR * (m_hat / (jnp.sqrt(v_hat) + ADAM_EPS) + ADAM_WD * w)
    return delta, m, v


def _reduce_update(pair_sums, chip_sums, w, m, v, chip_slot, name, part=(0, 1), earlier=None):
    R, C = pair_sums.shape[1:]
    p, parts = part
    tr = _tile(R, TILE["update"])
    nblk = R // tr

    def body(slot_ref, own_ref, got_ref, w_ref, m_ref, v_ref, *rest):
        g_out, d_out, m_out, v_out = rest[-4:]
        g = own_ref[...].astype(F32)
        for k in range(3):
            g = g + got_ref[k].astype(F32)
        d, m_new, v_new = _adamw(w_ref[...], g, m_ref[...], v_ref[...])
        g_out[...] = g
        d_out[...] = d
        m_out[...] = m_new
        v_out[...] = v_new

    blk = pl.BlockSpec((tr, C), lambda i, s: (p * nblk + i, 0))
    in_specs = [pl.BlockSpec((None, tr, C), lambda i, s: (s[0], i, 0)),
                pl.BlockSpec((3, tr, C), lambda i, s: (0, i, 0)), blk, blk, blk]
    operands = (chip_slot, pair_sums, chip_sums, w, m, v)
    aliases = {}
    if earlier is not None:
        in_specs += [pl.BlockSpec(memory_space=pl.ANY)] * 4
        operands += tuple(earlier)
        aliases = {6 + k: k for k in range(4)}
    return _call(
        body, name=name,
        grid_spec=pltpu.PrefetchScalarGridSpec(
            num_scalar_prefetch=1, grid=(R // tr,), in_specs=in_specs, out_specs=[blk] * 4),
        out_shape=[_sds((parts * R, C), F32)] * 4,
        input_output_aliases=aliases,
        compiler_params=_cp("parallel"),
    )(*operands)


def _small_update(w, g, m, v, name):
    def body(w_ref, g_ref, m_ref, v_ref, d_out, m_out, v_out):
        d, m_new, v_new = _adamw(w_ref[...], g_ref[...], m_ref[...], v_ref[...])
        d_out[...] = d
        m_out[...] = m_new
        v_out[...] = v_new

    return _call(body, name=name, out_shape=[_sds(w.shape, F32)] * 3)(w, g, m, v)


def _slots_from_rows(full, lead):
    L, R, C = full.shape
    r = R // N_DEV
    return full.reshape(L, N_DEV, r, C).transpose(1, 0, 2, 3).reshape(N_DEV, L * r, C)


def _rows_from_slots(slots, lead):
    _, LR, C = slots.shape
    r = LR // lead
    return slots.reshape(N_DEV, lead, r, C).transpose(1, 0, 2, 3).reshape(lead, N_DEV * r, C)


def kernel(x, meta_tokens, norm1_g, w_in, pool_w, pool_scale, conv_w, conv_b, gate_a_w, gate_a_b, gate_x_w, gate_x_b, lru_lambda, w_out, norm2_g, mlp_w1, mlp_w2, final_g, loss_target, m_meta_tokens, m_norm1_g, m_w_in, m_pool_w, m_pool_scale, m_conv_w, m_conv_b, m_gate_a_w, m_gate_a_b, m_gate_x_w, m_gate_x_b, m_lru_lambda, m_w_out, m_norm2_g, m_mlp_w1, m_mlp_w2, m_final_g, v_meta_tokens, v_norm1_g, v_w_in, v_pool_w, v_pool_scale, v_conv_w, v_conv_b, v_gate_a_w, v_gate_a_b, v_gate_x_w, v_gate_x_b, v_lru_lambda, v_w_out, v_norm2_g, v_mlp_w1, v_mlp_w2, v_final_g):
    seq, D = x.shape[1], x.shape[2]
    n_meta = meta_tokens.shape[0]
    G, Cg = pool_w.shape[1], pool_w.shape[3]
    H, hd = gate_a_w.shape[1], gate_a_w.shape[3]
    T = n_meta + seq
    Tp = -(-T // ROW_ALIGN) * ROW_ALIGN
    ix, iy, ic = _pos()
    me = 4 * ix + 2 * iy + ic
    core = jnp.reshape(ic, (1,)).astype(jnp.int32)
    chip_slot = jnp.reshape(2 * ix + iy, (1,)).astype(jnp.int32)

    w_in_l, w1_l, w2_l, w_out_l = w_in[0], mlp_w1[0], mlp_w2[0], w_out[0]
    pool_l = pool_w[0].reshape(G * (Cg // N_DEV), Cg)
    wa_l = gate_a_w[0].reshape(H * (hd // N_DEV), hd)
    wx_l = gate_x_w[0].reshape(H * (hd // N_DEV), hd)
    small_params = jnp.concatenate(
        [meta_tokens, conv_w[0], jnp.zeros((4, D // N_DEV), F32)], axis=0)
    biases = jnp.concatenate([gate_a_b[0], gate_x_b[0]], axis=0)
    (w_in_g, pool_g, wa_g, wx_g, small_g, bias_g) = _all_gather(
        [w_in_l.astype(BF16), pool_l.astype(BF16), wa_l.astype(BF16), wx_l.astype(BF16), small_params, biases],
        "gather_first")
    pool_full = _rows_from_slots(pool_g, G)
    wa_full = _rows_from_slots(wa_g, H)
    wx_full = _rows_from_slots(wx_g, H)
    small_full = small_g.transpose(1, 0, 2).reshape(n_meta + 8, D)
    meta_full = small_full[:n_meta]
    conv_full = small_full[n_meta:n_meta + CONV_WIDTH]
    bias_full = bias_g.transpose(1, 0, 2).reshape(2 * H, hd)
    ba_full, bx_full = bias_full[:H], bias_full[H:]

    h0 = jnp.concatenate([meta_full, x[0], jnp.zeros((Tp - T, D), F32)], axis=0)
    target = jnp.concatenate([jnp.zeros((n_meta, D), F32), loss_target[0], jnp.zeros((Tp - T, D), F32)], axis=0)
    u, r1 = _norm_fwd(h0, norm1_g, "norm1")
    proj, ((w_out_g,), (w1_ici,)) = _proj_fwd(
        u, w_in_g, "proj", tasks=[_AgFull(w_out_l.astype(BF16)), _AgFull(w1_l.astype(BF16), forward_here=False)])
    w_in_parts = [w_in_g]
    d_pool, y_pool = _pool_fwd(proj, pool_full, "pool_fwd")
    (xc, r_gate, i_gate, a_gate, mult_gate, hs, merged), ((w1_g,), (w2_ici,)) = _lru_fwd(
        proj, y_pool, pool_scale, conv_full, conv_b, wa_full, ba_full, wx_full, bx_full, lru_lambda, "lru_fwd",
        tasks=[_AgForward(w1_ici), _AgFull(w2_l.astype(BF16), forward_here=False)])
    w_out_full = w_out_g.reshape(D, D)
    (h1, u2, r2), ((w2_g,),) = _wout_norm_fwd(merged, w_out_full, h0, norm2_g, "wout_norm2", tasks=[_AgForward(w2_ici)])
    (act, a1), _ = _mlp1_fwd(u2, w1_g, "mlp1")
    dh2, dh2b, loss_tile, d_final_g = _mlp2_loss(
        act, w2_g.reshape(-1, D), h1, target, final_g.reshape(1, D), n_meta, seq, "mlp2_loss")

    def pair(part, got, tag):
        return _pair_sum(part, got, core, "pair_sum_" + tag)

    d_a1 = _dact_bwd(dh2b, w2_g, a1, "dact")
    dw2_p, _ = _weight_grad(act, dh2b, N_DEV, True, "dw2")
    dw1_p, ((dw2_got,),) = _weight_grad(u2, d_a1, N_DEV, False, "dw1", tasks=[_RsSibling(dw2_p)])
    dw2_pair = pair(dw2_p, dw2_got, "w2")
    (dh1, dh1b, d_norm2_g), ((dw2_chips,), (dw1_got,)) = _nt_norm_bwd(
        d_a1, [w1_g], dh2, h1, r2, norm2_g, True, "du2_norm2", tasks=[_RsChips(dw2_pair), _RsSibling(dw1_p)])
    dw1_pair = pair(dw1_p, dw1_got, "w1")
    dmerged = _dmerged_bwd(dh1b, w_out_full, "dmerged")
    dwout_p, _ = _weight_grad(merged, dh1b, 2, True, "dwout")
    dwout_p = dwout_p.reshape(N_DEV, D // N_DEV, D)
    (dproj_pool, dpool_full, d_scale), ((dwout_got,),) = _pool_bwd(
        dmerged, proj, y_pool, d_pool, pool_scale, pool_full, "pool_bwd", tasks=[_RsSibling(dwout_p)])
    dwout_pair = pair(dwout_p, dwout_got, "wout")
    (dproj, dwa_full, dwx_full, lru_small), ((dw1_chips,), (dwout_chips,)) = _lru_bwd(
        dmerged, proj, xc, r_gate, i_gate, a_gate, mult_gate, hs, lru_lambda, conv_full, wa_full, wx_full, dproj_pool,
        "lru_bwd", tasks=[_RsChips(dw1_pair), _RsChips(dwout_pair)])
    dwin_a, _ = _weight_grad(u, dproj, N_DEV, False, "dwin_a", part=(0, 2))
    dwin_b, ((dwin_a_got,),) = _weight_grad(u, dproj, N_DEV, False, "dwin_b", part=(1, 2), tasks=[_RsSibling(dwin_a)])
    dwin_a_pair = pair(dwin_a, dwin_a_got, "win_a")
    dpool_p = _slots_from_rows(dpool_full, G).astype(BF16)
    dwa_p = _slots_from_rows(dwa_full, H).astype(BF16)
    dwx_p = _slots_from_rows(dwx_full, H).astype(BF16)
    late = [dwin_b, dpool_p, dwa_p, dwx_p]
    n_tiles = Tp // _tile(Tp, TILE["nt"])
    n_first = max(1, n_tiles // 2)
    (dh0_a, d_norm1_a), ((dwin_a_chips,), *late_got) = _nt_norm_bwd(
        dproj, w_in_parts, dh1, h0, r1, norm1_g, False, "du_norm1_a",
        tasks=[_RsChips(dwin_a_pair)] + [_RsSibling(p) for p in late], tiles=(0, n_first))
    late_pair = [pair(p, g[0], "late%d" % k) for k, (p, g) in enumerate(zip(late, late_got))]
    (dh0, d_norm1_g), late_chips = _nt_norm_bwd(
        dproj, w_in_parts, dh1, h0, r1, norm1_g, False, "du_norm1_b", tasks=[_RsChips(p) for p in late_pair],
        tiles=(n_first, n_tiles - n_first), earlier=(dh0_a, d_norm1_a))
    grad_x = dh0[n_meta:T][None]

    pair_sums = [dw2_pair, dw1_pair, dwout_pair] + late_pair[1:]
    chip_sums = [dw2_chips, dw1_chips, dwout_chips] + [c[0] for c in late_chips[1:]]
    big = {}
    names = ["mlp_w2", "mlp_w1", "w_out", "pool_w", "gate_a_w", "gate_x_w"]
    trip = {"mlp_w2": (mlp_w2, m_mlp_w2, v_mlp_w2), "mlp_w1": (mlp_w1, m_mlp_w1, v_mlp_w1),
            "w_out": (w_out, m_w_out, v_w_out),
            "pool_w": (pool_w, m_pool_w, v_pool_w), "gate_a_w": (gate_a_w, m_gate_a_w, v_gate_a_w),
            "gate_x_w": (gate_x_w, m_gate_x_w, v_gate_x_w)}
    for k, nm in enumerate(names):
        w_, m_, v_ = trip[nm]
        shape2 = pair_sums[k].shape[1:]
        outs = _reduce_update(pair_sums[k], chip_sums[k], w_.reshape(shape2), m_.reshape(shape2), v_.reshape(shape2),
                              chip_slot, "update_" + nm)
        big[nm] = [o.reshape(w_.shape) for o in outs]
    win2 = [a_[0] for a_ in (w_in, m_w_in, v_w_in)]
    win_a = _reduce_update(dwin_a_pair, dwin_a_chips, *win2, chip_slot, "update_w_in_a", part=(0, 2))
    win_b = _reduce_update(late_pair[0], late_chips[0][0], *win2, chip_slot, "update_w_in_b", part=(1, 2), earlier=win_a)
    big["w_in"] = [o.reshape(w_in.shape) for o in win_b]
    names = names + ["w_in"]

    lru_rows = lru_small.transpose(1, 0, 2).reshape(LRU_SMALL_ROWS, D)
    small_part = jnp.concatenate(
        [dh0[:n_meta], d_norm1_g, d_scale, d_norm2_g, d_final_g, lru_rows, jnp.zeros((4, D), F32)], axis=0)
    (small_all,) = _all_gather([small_part], "gather_small_grads")
    small_sum = _sum_slots(small_all, "sum_small_grads")
    o = n_meta
    g_meta_full = small_sum[:o]
    g_norm1, g_scale, g_norm2, g_final = (small_sum[o + k:o + k + 1] for k in range(4))
    g_ba_full, g_bx_full, g_lam, g_cb = (small_sum[o + 4 + k:o + 5 + k] for k in range(4))
    g_cw_full = small_sum[o + 8:o + 8 + CONV_WIDTH]
    dcol = D // N_DEV
    g_meta = lax.dynamic_slice_in_dim(g_meta_full, me * dcol, dcol, axis=1)
    g_cw = lax.dynamic_slice_in_dim(g_cw_full, me * dcol, dcol, axis=1)
    hcol = hd // N_DEV
    g_ba = lax.dynamic_slice_in_dim(g_ba_full.reshape(H, hd), me * hcol, hcol, axis=1)
    g_bx = lax.dynamic_slice_in_dim(g_bx_full.reshape(H, hd), me * hcol, hcol, axis=1)

    rep_w = jnp.concatenate([norm1_g, pool_scale, conv_b, lru_lambda, norm2_g, final_g.reshape(1, D)], axis=0)
    rep_g = jnp.concatenate([g_norm1, g_scale, g_cb, g_lam, g_norm2, g_final], axis=0)
    rep_m = jnp.concatenate([m_norm1_g, m_pool_scale, m_conv_b, m_lru_lambda, m_norm2_g, m_final_g.reshape(1, D)], axis=0)
    rep_v = jnp.concatenate([v_norm1_g, v_pool_scale, v_conv_b, v_lru_lambda, v_norm2_g, v_final_g.reshape(1, D)], axis=0)
    rep_d, rep_nm, rep_nv = _small_update(rep_w, rep_g, rep_m, rep_v, "update_vectors")
    col_w = jnp.concatenate([meta_tokens, conv_w[0]], axis=0)
    col_g = jnp.concatenate([g_meta, g_cw], axis=0)
    col_m = jnp.concatenate([m_meta_tokens, m_conv_w[0]], axis=0)
    col_v = jnp.concatenate([v_meta_tokens, v_conv_w[0]], axis=0)
    col_d, col_nm, col_nv = _small_update(col_w, col_g, col_m, col_v, "update_columns")
    b_w = jnp.concatenate([gate_a_b[0], gate_x_b[0]], axis=0)
    b_g = jnp.concatenate([g_ba, g_bx], axis=0)
    b_m = jnp.concatenate([m_gate_a_b[0], m_gate_x_b[0]], axis=0)
    b_v = jnp.concatenate([v_gate_a_b[0], v_gate_x_b[0]], axis=0)
    b_d, b_nm, b_nv = _small_update(b_w, b_g, b_m, b_v, "update_biases")

    def rep(arr, k, like):
        return arr[k:k + 1].reshape(like.shape)

    rep_order = {"norm1_g": 0, "pool_scale": 1, "conv_b": 2, "lru_lambda": 3, "norm2_g": 4, "final_g": 5}
    like = {"norm1_g": norm1_g, "pool_scale": pool_scale, "conv_b": conv_b, "lru_lambda": lru_lambda,
            "norm2_g": norm2_g, "final_g": final_g}

    def leaves(kind):
        rep_src = [rep_g, rep_d, rep_nm, rep_nv][kind]
        col_src = [col_g, col_d, col_nm, col_nv][kind]
        b_src = [b_g, b_d, b_nm, b_nv][kind]
        out = {}
        out["meta_tokens"] = col_src[:n_meta]
        out["conv_w"] = col_src[n_meta:][None]
        out["gate_a_b"] = b_src[:H][None]
        out["gate_x_b"] = b_src[H:][None]
        for nm, k in rep_order.items():
            out[nm] = rep(rep_src, k, like[nm])
        for nm in names:
            out[nm] = big[nm][kind]
        order = ["meta_tokens", "norm1_g", "w_in", "pool_w", "pool_scale", "conv_w", "conv_b", "gate_a_w", "gate_a_b",
                 "gate_x_w", "gate_x_b", "lru_lambda", "w_out", "norm2_g", "mlp_w1", "mlp_w2", "final_g"]
        return [out[nm] for nm in order]

    loss = lax.psum(loss_tile[0, 0], ("x", "y", "c"))
    return (loss, grad_x, *leaves(0), *leaves(1), *leaves(2), *leaves(3))
```

```python
import functools

import jax
import jax.numpy as jnp
from jax import lax
from jax.experimental import pallas as pl
from jax.experimental.pallas import tpu as pltpu

F32 = jnp.float32
BF16 = jnp.bfloat16
MESH = pl.DeviceIdType.MESH
N_DEV = 8
POOL_WINDOWS = (2, 4, 8, 16)
MAX_WINDOW = 16
CONV_WIDTH = 4
HALO = 8
LRU_C = 8.0
NORM_EPS = 1e-6
ADAM_LR, ADAM_B1, ADAM_B2, ADAM_EPS, ADAM_WD, ADAM_STEP = 0.001, 0.9, 0.999, 1e-08, 0.01, 10
ROW_ALIGN = 128
VMEM_LIMIT = 56 << 20
TILE = dict(norm=384, proj=1408, pool=384, lru=704, wout=384, mlp1=1408, mlp2=704, dact=1408, tn=1408,
            nt=704, dmerged=384, update=256, pair=1024)
MLP2_K = 1024
EPILOGUE_ROWS = 176
W_IN_CHUNKS = 5
MID_STEP_PERCENT = 88

_NT = (((1,), (1,)), ((), ()))
_TN = (((0,), (0,)), ((), ()))


def _call(body, **kw):
    return pl.pallas_call(body, **kw)


def _cp(*sem):
    return pltpu.CompilerParams(dimension_semantics=sem, vmem_limit_bytes=VMEM_LIMIT)


def _tile(total, pref):
    best = None
    for t in range(16, min(total, pref) + 1, 16):
        if total % t == 0:
            best = t
    assert best is not None, (total, pref)
    return best


def _sds(shape, dtype):
    return jax.ShapeDtypeStruct(shape, dtype)


def _pos():
    return lax.axis_index("x"), lax.axis_index("y"), lax.axis_index("c")


def _all_gather(shards, name):
    n = len(shards)

    def body(*refs):
        ins, outs = refs[:n], refs[n:2 * n]
        send_sems, recv_sems, local_sems = refs[2 * n:]
        x, y, c = _pos()
        me, sib = (x, y, c), (x, y, 1 - c)
        chips = [(1 - x, y), (x, 1 - y), (1 - x, 1 - y)]

        def slot(p):
            return 4 * p[0] + 2 * p[1] + p[2]

        def copy(a, k, block, to, src=None):
            dst = outs[a].at[slot(block)]
            return pltpu.make_async_remote_copy(
                src_ref=dst if src is None else src, dst_ref=dst,
                send_sem=send_sems.at[7 * a + k], recv_sem=recv_sems.at[7 * a + k],
                device_id=to, device_id_type=MESH)

        mine = [pltpu.make_async_copy(ins[a], outs[a].at[slot(me)], local_sems.at[a]) for a in range(n)]
        for m in mine:
            m.start()
        first = []
        for a in range(n):
            first.append(copy(a, 0, me, sib, src=ins[a]))
            first += [copy(a, 1 + j, me, (*chip, c), src=ins[a]) for j, chip in enumerate(chips)]
        for cp in first:
            cp.start()
        passed = []
        for a in range(n):
            for j, chip in enumerate(chips):
                copy(a, 1 + j, (*chip, c), me).wait_recv()
                fwd = copy(a, 4 + j, (*chip, c), sib)
                fwd.start()
                passed.append(fwd)
        for a in range(n):
            copy(a, 0, sib, me).wait_recv()
            for j, chip in enumerate(chips):
                copy(a, 4 + j, (*chip, 1 - c), me).wait_recv()
        for cp in first + passed:
            cp.wait_send()
        for m in mine:
            m.wait()

    hbm = pl.BlockSpec(memory_space=pl.ANY)
    return _call(
        body, name=name,
        out_shape=[_sds((N_DEV,) + s.shape, s.dtype) for s in shards],
        in_specs=[hbm] * n, out_specs=[hbm] * n,
        scratch_shapes=[pltpu.SemaphoreType.DMA((7 * n,)), pltpu.SemaphoreType.DMA((7 * n,)),
                        pltpu.SemaphoreType.DMA((n,))],
    )(*shards)


def _all_gather_relay(shards, chunks, name):
    n = len(shards)
    units = []
    for a, s in enumerate(shards):
        if chunks[a] == 1:
            units.append((a, None, None))
        else:
            w = s.shape[-1] // chunks[a]
            units += [(a, q * w, w) for q in range(chunks[a])]
    nu = len(units)

    def body(*refs):
        ins, outs = refs[:n], refs[n:2 * n]
        send_sems, recv_sems, local_sems = refs[2 * n:]
        x, y, c = _pos()
        me, sib = (x, y, c), (x, y, 1 - c)
        x_nbr, y_nbr, diag = (1 - x, y, c), (x, 1 - y, c), (1 - x, 1 - y, c)
        came_from = (x + (1 - c) * (1 - 2 * x), y + c * (1 - 2 * y), c)
        pass_to = (x + c * (1 - 2 * x), y + (1 - c) * (1 - 2 * y), c)

        def slot(p):
            return 4 * p[0] + 2 * p[1] + p[2]

        def src_view(u):
            a, c0, w = units[u]
            return ins[a] if c0 is None else ins[a].at[:, pl.ds(c0, w)]

        def dst_view(u, p):
            a, c0, w = units[u]
            return outs[a].at[slot(p)] if c0 is None else outs[a].at[slot(p), :, pl.ds(c0, w)]

        def copy(u, k, block, to, from_shard=False):
            dst = dst_view(u, block)
            return pltpu.make_async_remote_copy(
                src_ref=src_view(u) if from_shard else dst, dst_ref=dst,
                send_sem=send_sems.at[7 * u + k], recv_sem=recv_sems.at[7 * u + k],
                device_id=to, device_id_type=MESH)

        mine = [pltpu.make_async_copy(src_view(u), dst_view(u, me), local_sems.at[u]) for u in range(nu)]
        for m in mine:
            m.start()
        sent = []
        for u in range(nu):
            sent += [copy(u, 0, me, sib, True), copy(u, 1, me, x_nbr, True), copy(u, 2, me, y_nbr, True)]
        for cp in sent:
            cp.start()
        for u in range(nu):
            copy(u, 1, x_nbr, me).wait_recv()
            copy(u, 2, y_nbr, me).wait_recv()
            sent += [copy(u, 3, came_from, pass_to), copy(u, 4, x_nbr, sib), copy(u, 5, y_nbr, sib)]
            for cp in sent[-3:]:
                cp.start()
        for u in range(nu):
            copy(u, 3, diag, me).wait_recv()
            sent.append(copy(u, 6, diag, sib))
            sent[-1].start()
        for u in range(nu):
            copy(u, 0, sib, me).wait_recv()
            for k, p in ((4, x_nbr), (5, y_nbr), (6, diag)):
                copy(u, k, (p[0], p[1], 1 - c), me).wait_recv()
        for cp in sent:
            cp.wait_send()
        for m in mine:
            m.wait()

    hbm = pl.BlockSpec(memory_space=pl.ANY)
    return _call(
        body, name=name,
        out_shape=[_sds((N_DEV,) + s.shape, s.dtype) for s in shards],
        in_specs=[hbm] * n, out_specs=[hbm] * n,
        scratch_shapes=[pltpu.SemaphoreType.DMA((7 * nu,)), pltpu.SemaphoreType.DMA((7 * nu,)),
                        pltpu.SemaphoreType.DMA((nu,))],
    )(*shards)


def _other_chips(x, y):
    return [(1 - x, y), (x, 1 - y), (1 - x, 1 - y)]


class _AgFull:
    n_sem, n_local = 7, 1

    def __init__(self, shard, forward_here=True):
        self.forward_here = forward_here
        self.ins = [shard]
        self.out_shapes = [_sds((N_DEV,) + shard.shape, shard.dtype)]
        self.aliases = []

    def _peers(self):
        x, y, c = _pos()
        return [(x, y, 1 - c)] + [(*chip, c) for chip in _other_chips(x, y)]

    def _sends(self, ins, outs, sems):
        send, recv, _, base, _ = sems
        x, y, c = _pos()
        mine = outs[0].at[4 * x + 2 * y + c]
        return [pltpu.make_async_remote_copy(src_ref=ins[0], dst_ref=mine, send_sem=send.at[base + k],
                                             recv_sem=recv.at[base + k], device_id=p, device_id_type=MESH)
                for k, p in enumerate(self._peers())]

    def _arrivals(self, outs, sems):
        send, recv, _, base, _ = sems
        res = []
        for k, p in enumerate(self._peers()):
            blk = outs[0].at[4 * p[0] + 2 * p[1] + p[2]]
            res.append(pltpu.make_async_remote_copy(src_ref=blk, dst_ref=blk, send_sem=send.at[base + k],
                                                    recv_sem=recv.at[base + k], device_id=p, device_id_type=MESH))
        return res

    def _own(self, ins, outs, sems):
        x, y, c = _pos()
        return pltpu.make_async_copy(ins[0], outs[0].at[4 * x + 2 * y + c], sems[2].at[sems[4]])

    def _forwards(self, outs, sems, core_of_block):
        return _forward_copies(outs[0], sems[0], sems[1], sems[3] + 4, core_of_block)

    def start(self, ins, outs, sems):
        self._own(ins, outs, sems).start()
        for cp in self._sends(ins, outs, sems):
            cp.start()

    def mid(self, ins, outs, sems):
        if self.forward_here:
            for cp in self._arrivals(outs, sems)[1:]:
                cp.wait_recv()
            for cp in self._forwards(outs, sems, "mine"):
                cp.start()

    def finish(self, ins, outs, sems):
        if self.forward_here:
            self._arrivals(outs, sems)[0].wait_recv()
            for cp in self._forwards(outs, sems, "sibling"):
                cp.wait_recv()
            for cp in self._sends(ins, outs, sems) + self._forwards(outs, sems, "mine"):
                cp.wait_send()
        else:
            for cp in self._arrivals(outs, sems):
                cp.wait_recv()
            for cp in self._sends(ins, outs, sems):
                cp.wait_send()
        self._own(ins, outs, sems).wait()


def _forward_copies(gathered_ref, send, recv, base, core_of_block):
    x, y, c = _pos()
    res = []
    for k, chip in enumerate(_other_chips(x, y)):
        blk = gathered_ref.at[4 * chip[0] + 2 * chip[1] + (c if core_of_block == "mine" else 1 - c)]
        res.append(pltpu.make_async_remote_copy(src_ref=blk, dst_ref=blk, send_sem=send.at[base + k],
                                                recv_sem=recv.at[base + k], device_id=(x, y, 1 - c),
                                                device_id_type=MESH))
    return res


class _AgForward:
    n_sem, n_local = 3, 0

    def __init__(self, gathered):
        self.ins = [gathered]
        self.out_shapes = [_sds(gathered.shape, gathered.dtype)]
        self.aliases = [(0, 0)]

    def start(self, ins, outs, sems):
        for cp in _forward_copies(outs[0], sems[0], sems[1], sems[3], "mine"):
            cp.start()

    def finish(self, ins, outs, sems):
        for cp in _forward_copies(outs[0], sems[0], sems[1], sems[3], "sibling"):
            cp.wait_recv()
        for cp in _forward_copies(outs[0], sems[0], sems[1], sems[3], "mine"):
            cp.wait_send()


class _RsSibling:
    n_sem, n_local = 4, 0

    def __init__(self, part):
        self.ins = [part]
        self.out_shapes = [_sds((4,) + part.shape[1:], part.dtype)]
        self.aliases = []

    def _copies(self, ins, outs, sems):
        send, recv, _, base, _ = sems
        x, y, c = _pos()
        return [pltpu.make_async_remote_copy(src_ref=ins[0].at[2 * q + (1 - c)], dst_ref=outs[0].at[q],
                                             send_sem=send.at[base + q], recv_sem=recv.at[base + q],
                                             device_id=(x, y, 1 - c), device_id_type=MESH) for q in range(4)]

    def start(self, ins, outs, sems):
        for cp in self._copies(ins, outs, sems):
            cp.start()

    def finish(self, ins, outs, sems):
        for cp in self._copies(ins, outs, sems):
            cp.wait()


class _RsChips:
    n_sem, n_local = 3, 0

    def __init__(self, pair):
        self.ins = [pair]
        self.out_shapes = [_sds((3,) + pair.shape[1:], pair.dtype)]
        self.aliases = []

    def _copies(self, ins, outs, sems):
        send, recv, _, base, _ = sems
        x, y, c = _pos()
        return [pltpu.make_async_remote_copy(src_ref=ins[0].at[2 * chip[0] + chip[1]], dst_ref=outs[0].at[k],
                                             send_sem=send.at[base + k], recv_sem=recv.at[base + k],
                                             device_id=(*chip, c), device_id_type=MESH)
                for k, chip in enumerate(_other_chips(x, y))]

    def start(self, ins, outs, sems):
        for cp in self._copies(ins, outs, sems):
            cp.start()

    def finish(self, ins, outs, sems):
        for cp in self._copies(ins, outs, sems):
            cp.wait()


def _hosted(body, tasks, *, grid, in_specs, out_specs, out_shape, scratch_shapes=(), name, semantics, operands,
            aliases=None):
    in_specs, out_specs, out_shape = list(in_specs), list(out_specs), list(out_shape)
    scratch_shapes = list(scratch_shapes)
    aliases = dict(aliases or {})
    if not tasks:
        res = _call(body, name=name, grid=grid, in_specs=in_specs, out_specs=out_specs, out_shape=out_shape,
                    scratch_shapes=scratch_shapes, input_output_aliases=aliases,
                    compiler_params=_cp(*semantics))(*operands)
        return list(res), []
    n_in, n_out, n_scr = len(in_specs), len(out_specs), len(scratch_shapes)
    t_ins = [a for t in tasks for a in t.ins]
    t_outs = [o for t in tasks for o in t.out_shapes]
    i0, o0 = n_in, n_out
    for t in tasks:
        for (i, o) in t.aliases:
            aliases[i0 + i] = o0 + o
        i0 += len(t.ins)
        o0 += len(t.out_shapes)
    n_sem = sum(t.n_sem for t in tasks)
    n_local = max(1, sum(t.n_local for t in tasks))
    n_steps = 1
    for g in grid:
        n_steps *= g
    mid_step = min(n_steps - 1, (n_steps * MID_STEP_PERCENT) // 100)

    def wrapped(*refs):
        cut = [n_in, len(t_ins), n_out, len(t_outs), n_scr]
        parts, at = [], 0
        for n in cut:
            parts.append(refs[at:at + n])
            at += n
        ins, tin, outs, tout, scratch = parts
        send, recv, local = refs[at:]
        step = pl.program_id(0)
        for d in range(1, len(grid)):
            step = step * grid[d] + pl.program_id(d)

        def each(method):
            i, o, s, l = 0, 0, 0, 0
            for t in tasks:
                if hasattr(t, method):
                    getattr(t, method)(tin[i:i + len(t.ins)], tout[o:o + len(t.out_shapes)], (send, recv, local, s, l))
                i, o, s, l = i + len(t.ins), o + len(t.out_shapes), s + t.n_sem, l + t.n_local

        @pl.when(step == 0)
        def _():
            each("start")

        body(*ins, *outs, *scratch)

        @pl.when(step == mid_step)
        def _():
            each("mid")

        @pl.when(step == n_steps - 1)
        def _():
            each("finish")

    hbm = pl.BlockSpec(memory_space=pl.ANY)
    res = _call(
        wrapped, name=name, grid=grid,
        in_specs=in_specs + [hbm] * len(t_ins), out_specs=out_specs + [hbm] * len(t_outs),
        out_shape=out_shape + t_outs,
        scratch_shapes=scratch_shapes + [pltpu.SemaphoreType.DMA((n_sem,)), pltpu.SemaphoreType.DMA((n_sem,)),
                                         pltpu.SemaphoreType.DMA((n_local,))],
        input_output_aliases=aliases,
        compiler_params=_cp(*(["arbitrary"] * len(grid))),
    )(*operands, *t_ins)
    res = list(res)
    task_outs, o = [], n_out
    for t in tasks:
        task_outs.append(res[o:o + len(t.out_shapes)])
        o += len(t.out_shapes)
    return res[:n_out], task_outs


def _pair_sum(part, got, core, name):
    _, R, C = part.shape
    tr = _tile(R, TILE["pair"]) if R % 16 == 0 else R

    def body(core_ref, p_ref, g_ref, o_ref):
        o_ref[...] = (p_ref[...].astype(F32) + g_ref[...].astype(F32)).astype(o_ref.dtype)

    return _call(
        body, name=name,
        grid_spec=pltpu.PrefetchScalarGridSpec(
            num_scalar_prefetch=1, grid=(4, R // tr),
            in_specs=[pl.BlockSpec((None, tr, C), lambda q, i, cr: (2 * q + cr[0], i, 0)),
                      pl.BlockSpec((None, tr, C), lambda q, i, cr: (q, i, 0))],
            out_specs=pl.BlockSpec((None, tr, C), lambda q, i, cr: (q, i, 0))),
        out_shape=_sds((4, R, C), part.dtype),
        compiler_params=_cp("parallel", "parallel"),
    )(core, part, got)


def _sum_slots(gathered, name):
    _, R, C = gathered.shape

    def body(g_ref, o_ref):
        acc = g_ref[0]
        for s in range(1, N_DEV):
            acc = acc + g_ref[s]
        o_ref[...] = acc

    return _call(body, name=name, out_shape=_sds((R, C), F32))(gathered)


def _sigmoid(z):
    return jax.nn.sigmoid(z)


def _softplus(z):
    e = jnp.exp(-jnp.abs(z))
    log1p_e = jnp.where(e < 0.01, e * (1.0 - e * (0.5 - e * (1.0 / 3.0))), jnp.log(1.0 + e))
    return jnp.maximum(z, 0.0) + log1p_e


_GELU_K = 0.7978845608028654
_GELU_C = 0.044715


def _gelu_and_grad(z):
    t = jnp.tanh(_GELU_K * (z + _GELU_C * z * z * z))
    g = 0.5 * z * (1.0 + t)
    dg = 0.5 * (1.0 + t) + 0.5 * z * (1.0 - t * t) * _GELU_K * (1.0 + 3.0 * _GELU_C * z * z)
    return g, dg


def _gelu(z):
    t = jnp.tanh(_GELU_K * (z + _GELU_C * z * z * z))
    return 0.5 * z * (1.0 + t)


def _row_ids(tile_index, tm, width=1):
    return tile_index * tm + lax.broadcasted_iota(jnp.int32, (tm, width), 0)


def _shift_down(prev, cur, k):
    if k == 0:
        return cur
    ext = jnp.concatenate([prev, cur], axis=0)
    return pltpu.roll(ext, k, axis=0)[prev.shape[0]:]


def _shift_up(cur, nxt, k):
    if k == 0:
        return cur
    ext = jnp.concatenate([cur, nxt], axis=0)
    return pltpu.roll(ext, ext.shape[0] - k, axis=0)[:cur.shape[0]]


def _lru_gates(r, sp):
    z = LRU_C * r * sp
    a = jnp.exp(-z)
    t = jnp.tanh(z)
    mult = jnp.sqrt(2.0 * t / (1.0 + t))
    return a, mult


def _scan_chunks(a_ref, b_ref, out_ref, carry, n_rows, reverse):
    n_chunks = n_rows // 8
    cols = a_ref.shape[1]
    rid = lax.broadcasted_iota(jnp.int32, (8, cols), 0)
    edge = 0 if reverse else 7

    def chunk(k, h):
        ci = (n_chunks - 1 - k) if reverse else k
        rows = pl.ds(pl.multiple_of(ci * 8, 8), 8)
        a = a_ref[rows, :]
        b = b_ref[rows, :]
        for s in (1, 2, 4):
            if reverse:
                keep = rid < 8 - s
                a_n, b_n = pltpu.roll(a, 8 - s, axis=0), pltpu.roll(b, 8 - s, axis=0)
            else:
                keep = rid >= s
                a_n, b_n = pltpu.roll(a, s, axis=0), pltpu.roll(b, s, axis=0)
            b = a * jnp.where(keep, b_n, 0.0) + b
            a = a * jnp.where(keep, a_n, 1.0)
        out_ref[rows, :] = a * h + b
        a_e = jnp.sum(jnp.where(rid == edge, a, 0.0), axis=0, keepdims=True)
        b_e = jnp.sum(jnp.where(rid == edge, b, 0.0), axis=0, keepdims=True)
        return a_e * h + b_e

    return lax.fori_loop(0, n_chunks, chunk, carry, unroll=4 if n_chunks % 4 == 0 else 1)


def _norm_fwd(h, g, name):
    Tp, D = h.shape
    tm = _tile(Tp, TILE["norm"])

    def body(h_ref, g_ref, u_ref, r_ref):
        x = h_ref[...]
        r = lax.rsqrt(jnp.mean(x * x, axis=-1, keepdims=True) + NORM_EPS)
        u_ref[...] = (x * r * g_ref[...]).astype(BF16)
        r_ref[...] = r

    return _call(
        body, name=name, grid=(Tp // tm,),
        in_specs=[pl.BlockSpec((tm, D), lambda i: (i, 0)), pl.BlockSpec((1, D), lambda i: (0, 0))],
        out_specs=[pl.BlockSpec((tm, D), lambda i: (i, 0)), pl.BlockSpec((tm, 1), lambda i: (i, 0))],
        out_shape=[_sds((Tp, D), BF16), _sds((Tp, 1), F32)],
        compiler_params=_cp("parallel"),
    )(h, g)


def _proj_fwd(u, w_slots, name, tasks=(), part=(0, 1), earlier=None):
    Tp, K = u.shape
    S, _, n = w_slots.shape
    p, parts = part
    tm = _tile(Tp, TILE["proj"])

    def body(a_ref, b_ref, *rest):
        o_ref = rest[-1]
        o_ref[...] = jnp.dot(a_ref[...], b_ref[...], preferred_element_type=F32)

    in_specs = [pl.BlockSpec((tm, K), lambda j, i: (i, 0)), pl.BlockSpec((None, K, n), lambda j, i: (j, 0, 0))]
    operands = (u, w_slots)
    aliases = {}
    if earlier is not None:
        in_specs.append(pl.BlockSpec(memory_space=pl.ANY))
        operands += (earlier,)
        aliases = {2: 0}
    (proj,), extra = _hosted(
        body, tasks, name=name, grid=(S, Tp // tm), in_specs=in_specs,
        out_specs=[pl.BlockSpec((tm, n), lambda j, i: (i, j * parts + p))],
        out_shape=[_sds((Tp, S * parts * n), F32)],
        semantics=("parallel", "parallel"), operands=operands, aliases=aliases)
    return proj, extra


def _mlp1_fwd(u2, w_slots, name, tasks=()):
    Tp, K = u2.shape
    S, _, n = w_slots.shape
    tm = _tile(Tp, TILE["mlp1"])

    def body(a_ref, b_ref, act_ref, a1_ref):
        a1 = jnp.dot(a_ref[...], b_ref[...], preferred_element_type=F32)
        relu = jnp.maximum(a1, 0.0)
        act_ref[...] = (relu * relu).astype(BF16)
        a1_ref[...] = a1.astype(BF16)

    return _hosted(
        body, tasks, name=name, grid=(S, Tp // tm),
        in_specs=[pl.BlockSpec((tm, K), lambda j, i: (i, 0)), pl.BlockSpec((None, K, n), lambda j, i: (j, 0, 0))],
        out_specs=[pl.BlockSpec((tm, n), lambda j, i: (i, j))] * 2,
        out_shape=[_sds((Tp, S * n), BF16)] * 2,
        semantics=("parallel", "parallel"), operands=(u2, w_slots))


def _pool_fwd(proj, pool_w, name):
    Tp = proj.shape[0]
    G, Cg, _ = pool_w.shape
    D = G * Cg
    tm = _tile(Tp, TILE["pool"])

    def body(v_ref, w_ref, d_ref, y_ref, prev_ref):
        t = pl.program_id(0)

        @pl.when(t == 0)
        def _():
            prev_ref[...] = jnp.zeros_like(prev_ref)

        rows = _row_ids(t, tm)
        for g, win in enumerate(POOL_WINDOWS):
            cols = slice(g * Cg, (g + 1) * Cg)
            v = v_ref[:, cols]
            s = jnp.concatenate([prev_ref[:, cols], v], axis=0)
            k = 1
            while k < win:
                s = s + pltpu.roll(s, k, axis=0)
                k *= 2
            cnt = jnp.minimum(rows + 1, win).astype(F32)
            d = s[MAX_WINDOW:] / cnt - v
            d_ref[:, cols] = d.astype(BF16)
            y_ref[:, cols] = jnp.dot(d.astype(BF16), w_ref[g], preferred_element_type=F32)
        prev_ref[...] = v_ref[tm - MAX_WINDOW:, :]

    return _call(
        body, name=name, grid=(Tp // tm,),
        in_specs=[pl.BlockSpec((tm, D), lambda t: (t, 0)), pl.BlockSpec((G, Cg, Cg), lambda t: (0, 0, 0))],
        out_specs=[pl.BlockSpec((tm, D), lambda t: (t, 0))] * 2,
        out_shape=[_sds((Tp, D), BF16), _sds((Tp, D), F32)],
        scratch_shapes=[pltpu.VMEM((MAX_WINDOW, D), F32)],
        compiler_params=_cp("arbitrary"),
    )(proj, pool_w)


def _lru_fwd(proj, y_pool, scale, conv_w, conv_b, wa, ba, wx, bx, lam, name, tasks=()):
    Tp = proj.shape[0]
    H, hd, _ = wa.shape
    D = H * hd
    tm = _tile(Tp, TILE["lru"])
    nb = D // hd

    def body(vl_ref, vg_ref, gp_ref, gl_ref, y_ref, sc_ref, cw_ref, cb_ref, wa_ref, ba_ref, wx_ref, bx_ref,
             lam_ref, xc_ref, r_ref, i_ref, a_ref, mult_ref, hs_ref, m_ref, prev_ref, carry_ref, b_s):
        t = pl.program_id(1)

        @pl.when(t == 0)
        def _():
            prev_ref[...] = jnp.zeros_like(prev_ref)
            carry_ref[...] = jnp.zeros_like(carry_ref)

        v = vl_ref[...]
        prev = prev_ref[...]
        xc = jnp.zeros_like(v) + cb_ref[...]
        for k in range(CONV_WIDTH):
            xc = xc + cw_ref[k:k + 1, :] * _shift_down(prev, v, CONV_WIDTH - 1 - k)
        prev_ref[...] = v[tm - HALO:, :]
        xcb = xc.astype(BF16)
        r = _sigmoid(jnp.dot(xcb, wa_ref[...], preferred_element_type=F32) + ba_ref[...])
        i = _sigmoid(jnp.dot(xcb, wx_ref[...], preferred_element_type=F32) + bx_ref[...])
        a, mult = _lru_gates(r, _softplus(-lam_ref[...]))
        a_ref[...] = a
        mult_ref[...] = mult
        b_s[...] = mult * (i * xc)
        xc_ref[...] = xc
        r_ref[...] = r
        i_ref[...] = i
        carry_ref[0:1, :] = _scan_chunks(a_ref, b_s, hs_ref, carry_ref[0:1, :], tm, reverse=False)
        lru_out = hs_ref[...] * _gelu(vg_ref[...])
        pool_out = y_ref[...] * sc_ref[...]
        m_ref[...] = (_sigmoid(gp_ref[...]) * pool_out + _sigmoid(gl_ref[...]) * lru_out).astype(BF16)

    def piece(p):
        return pl.BlockSpec((tm, hd), lambda h, t: (t, p * nb + h))

    blk = pl.BlockSpec((tm, hd), lambda h, t: (t, h))
    vec = pl.BlockSpec((1, hd), lambda h, t: (0, h))
    mat = pl.BlockSpec((None, hd, hd), lambda h, t: (h, 0, 0))
    bias = pl.BlockSpec((None, 1, hd), lambda h, t: (h, 0, 0))
    return _hosted(
        body, tasks, name=name, grid=(H, Tp // tm),
        in_specs=[piece(1), piece(2), piece(3), piece(4), blk, vec,
                  pl.BlockSpec((CONV_WIDTH, hd), lambda h, t: (0, h)), vec, mat, bias, mat, bias, vec],
        out_specs=[blk] * 7,
        out_shape=[_sds((Tp, D), F32)] * 6 + [_sds((Tp, D), BF16)],
        scratch_shapes=[pltpu.VMEM((HALO, hd), F32), pltpu.VMEM((8, hd), F32), pltpu.VMEM((tm, hd), F32)],
        semantics=("parallel", "arbitrary"),
        operands=(proj, proj, proj, proj, y_pool, scale, conv_w, conv_b, wa, ba.reshape(H, 1, hd), wx,
                  bx.reshape(H, 1, hd), lam))


def _wout_norm_fwd(merged, w_out, h0, g2, name, tasks=()):
    Tp, D = h0.shape
    tm = _tile(Tp, TILE["wout"])

    def body(m_ref, w_ref, h0_ref, g_ref, h1_ref, u2_ref, r2_ref):
        h1 = h0_ref[...] + jnp.dot(m_ref[...], w_ref[...], preferred_element_type=F32)
        r = lax.rsqrt(jnp.mean(h1 * h1, axis=-1, keepdims=True) + NORM_EPS)
        h1_ref[...] = h1
        u2_ref[...] = (h1 * r * g_ref[...]).astype(BF16)
        r2_ref[...] = r

    row = pl.BlockSpec((tm, D), lambda i: (i, 0))
    return _hosted(
        body, tasks, name=name, grid=(Tp // tm,),
        in_specs=[row, pl.BlockSpec((D, D), lambda i: (0, 0)), row, pl.BlockSpec((1, D), lambda i: (0, 0))],
        out_specs=[row, row, pl.BlockSpec((tm, 1), lambda i: (i, 0))],
        out_shape=[_sds((Tp, D), F32), _sds((Tp, D), BF16), _sds((Tp, 1), F32)],
        semantics=("parallel",), operands=(merged, w_out, h0, g2))


def _mlp2_loss(act, w2, h1, target, gf, n_meta, seq, name):
    Tp, D = h1.shape
    K = act.shape[1]
    tm = _tile(Tp, TILE["mlp2"])
    tk = min(K, MLP2_K)
    nk = K // tk

    rc = _tile(tm, EPILOGUE_ROWS)

    def body(a_ref, w_ref, h1_hbm, t_hbm, g_ref, dh_ref, dhb_ref, loss_ref, dg_ref, h1_buf, t_buf, sems):
        i, k = pl.program_id(0), pl.program_id(1)
        tile_rows = pl.ds(pl.multiple_of(i * tm, tm), tm)
        fetch = [pltpu.make_async_copy(h1_hbm.at[tile_rows, :], h1_buf, sems.at[0]),
                 pltpu.make_async_copy(t_hbm.at[tile_rows, :], t_buf, sems.at[1])]

        @pl.when(k == 0)
        def _():
            for f in fetch:
                f.start()
            dh_ref[...] = jnp.zeros_like(dh_ref)

        @pl.when((i == 0) & (k == 0))
        def _():
            loss_ref[...] = jnp.zeros_like(loss_ref)
            dg_ref[...] = jnp.zeros_like(dg_ref)

        dh_ref[...] += jnp.dot(a_ref[...], w_ref[...], preferred_element_type=F32)

        @pl.when(k == nk - 1)
        def _():
            for f in fetch:
                f.wait()
            g = g_ref[...]

            def chunk(c, carry):
                loss_acc, dg_acc = carry
                rows = pl.ds(pl.multiple_of(c * rc, rc), rc)
                h2 = h1_buf[rows, :] + dh_ref[rows, :]
                r = lax.rsqrt(jnp.mean(h2 * h2, axis=-1, keepdims=True) + NORM_EPS)
                out = h2 * r * g
                row_id = i * tm + c * rc + lax.broadcasted_iota(jnp.int32, (rc, 1), 0)
                valid = (row_id >= n_meta) & (row_id < n_meta + seq)
                diff = jnp.where(valid, out - t_buf[rows, :], 0.0)
                dout = diff / D
                dog = dout * g
                dh = r * dog - h2 * (r * r * r * jnp.mean(dog * h2, axis=-1, keepdims=True))
                dh_ref[rows, :] = dh
                dhb_ref[rows, :] = dh.astype(BF16)
                loss_acc = loss_acc + 0.5 * jnp.sum(jnp.mean(diff * diff, axis=-1, keepdims=True), axis=0, keepdims=True)
                return loss_acc, dg_acc + jnp.sum(dout * (h2 * r), axis=0, keepdims=True)

            loss_sum, dg_sum = lax.fori_loop(0, tm // rc, chunk, (jnp.zeros((1, 1), F32), jnp.zeros((1, D), F32)))
            loss_ref[...] += loss_sum
            dg_ref[...] += dg_sum

    row = pl.BlockSpec((tm, D), lambda i, k: (i, 0))
    hbm = pl.BlockSpec(memory_space=pl.ANY)
    return _call(
        body, name=name, grid=(Tp // tm, nk),
        in_specs=[pl.BlockSpec((tm, tk), lambda i, k: (i, k)), pl.BlockSpec((tk, D), lambda i, k: (k, 0)),
                  hbm, hbm, pl.BlockSpec((1, D), lambda i, k: (0, 0))],
        out_specs=[row, row, pl.BlockSpec((8, 128), lambda i, k: (0, 0)), pl.BlockSpec((1, D), lambda i, k: (0, 0))],
        out_shape=[_sds((Tp, D), F32), _sds((Tp, D), BF16), _sds((8, 128), F32), _sds((1, D), F32)],
        scratch_shapes=[pltpu.VMEM((tm, D), F32), pltpu.VMEM((tm, D), F32), pltpu.SemaphoreType.DMA((2,))],
        compiler_params=_cp("arbitrary", "arbitrary"),
    )(act, w2, h1, target, gf)


def _dact_bwd(dh2b, w2_slots, a1, name):
    Tp, D = dh2b.shape
    S, n, _ = w2_slots.shape
    tm = _tile(Tp, TILE["dact"])

    def body(g_ref, w_ref, a1_ref, o_ref):
        dact = lax.dot_general(g_ref[...], w_ref[...], _NT, preferred_element_type=F32)
        o_ref[...] = (dact * (2.0 * jnp.maximum(a1_ref[...].astype(F32), 0.0))).astype(BF16)

    return _call(
        body, name=name, grid=(S, Tp // tm),
        in_specs=[pl.BlockSpec((tm, D), lambda j, i: (i, 0)), pl.BlockSpec((None, n, D), lambda j, i: (j, 0, 0)),
                  pl.BlockSpec((tm, n), lambda j, i: (i, j))],
        out_specs=pl.BlockSpec((tm, n), lambda j, i: (i, j)),
        out_shape=_sds((Tp, S * n), BF16),
        compiler_params=_cp("parallel", "parallel"),
    )(dh2b, w2_slots, a1)


def _weight_grad(a, g, blocks, block_a, name, tasks=(), part=(0, 1)):
    Tp, Ka = a.shape
    Ng = g.shape[1]
    p, parts = part
    assert parts == 1 or not block_a
    ka = Ka // blocks if block_a else Ka // parts
    ng = Ng if block_a else Ng // blocks
    tt = _tile(Tp, TILE["tn"])
    nt = Tp // tt

    def body(a_ref, g_ref, o_ref, acc_ref):
        t = pl.program_id(1)

        @pl.when(t == 0)
        def _():
            acc_ref[...] = jnp.zeros_like(acc_ref)

        acc_ref[...] += lax.dot_general(a_ref[...], g_ref[...], _TN, preferred_element_type=F32)

        @pl.when(t == nt - 1)
        def _():
            o_ref[...] = acc_ref[...].astype(o_ref.dtype)

    if block_a:
        a_spec = pl.BlockSpec((tt, ka), lambda j, t: (t, j))
        g_spec = pl.BlockSpec((tt, ng), lambda j, t: (t, 0))
    else:
        a_spec = pl.BlockSpec((tt, ka), lambda j, t: (t, p))
        g_spec = pl.BlockSpec((tt, ng), lambda j, t: (t, j))
    (dw,), extra = _hosted(
        body, tasks, name=name, grid=(blocks, nt),
        in_specs=[a_spec, g_spec],
        out_specs=[pl.BlockSpec((None, ka, ng), lambda j, t: (j, 0, 0))],
        out_shape=[_sds((blocks, ka, ng), BF16)],
        scratch_shapes=[pltpu.VMEM((ka, ng), F32)],
        semantics=("parallel", "arbitrary"), operands=(a, g))
    return dw, extra


def _nt_norm_bwd(dz, w_parts, dres, hin, rin, g, want_bf16, name, tasks=(), tiles=None, earlier=None):
    Tp, D = hin.shape
    P = len(w_parts)
    S, _, n = w_parts[0].shape
    K = S * P
    tm = _tile(Tp, TILE["nt"])
    t0, nt = tiles if tiles is not None else (0, Tp // tm)
    assert not (want_bf16 and earlier is not None)

    rc = _tile(tm, EPILOGUE_ROWS)

    def body(dz_ref, *rest):
        w_refs, (dres_hbm, h_hbm, r_ref, g_ref), rest = rest[:P], rest[P:P + 4], rest[P + 4:]
        if earlier is not None:
            _, dg0_ref, dh_ref, dg_ref, dres_buf, h_buf, sems = rest
        elif want_bf16:
            dh_ref, dhb_ref, dg_ref, dres_buf, h_buf, sems = rest
        else:
            dh_ref, dg_ref, dres_buf, h_buf, sems = rest
        i, k = pl.program_id(0), pl.program_id(1)
        tile_rows = pl.ds(pl.multiple_of((t0 + i) * tm, tm), tm)
        fetch = [pltpu.make_async_copy(dres_hbm.at[tile_rows, :], dres_buf, sems.at[0]),
                 pltpu.make_async_copy(h_hbm.at[tile_rows, :], h_buf, sems.at[1])]

        @pl.when(k == 0)
        def _():
            for f in fetch:
                f.start()
            dh_ref[...] = jnp.zeros_like(dh_ref)

        @pl.when((i == 0) & (k == 0))
        def _():
            dg_ref[...] = jnp.zeros_like(dg_ref) if earlier is None else dg0_ref[...]

        for q in range(P):
            @pl.when(k % P == q)
            def _(q=q):
                dh_ref[...] += lax.dot_general(dz_ref[...], w_refs[q][...], _NT, preferred_element_type=F32)

        @pl.when(k == K - 1)
        def _():
            for f in fetch:
                f.wait()
            g = g_ref[...]

            def chunk(c, dg_acc):
                rows = pl.ds(pl.multiple_of(c * rc, rc), rc)
                du = dh_ref[rows, :]
                h = h_buf[rows, :]
                r = r_ref[rows, :]
                dug = du * g
                dh = dres_buf[rows, :] + r * dug - h * (r * r * r * jnp.mean(dug * h, axis=-1, keepdims=True))
                dh_ref[rows, :] = dh
                if want_bf16:
                    dhb_ref[rows, :] = dh.astype(BF16)
                return dg_acc + jnp.sum(du * (h * r), axis=0, keepdims=True)

            dg_ref[...] += lax.fori_loop(0, tm // rc, chunk, jnp.zeros((1, D), F32))

    row = pl.BlockSpec((tm, D), lambda i, k: (t0 + i, 0))
    vec = pl.BlockSpec((1, D), lambda i, k: (0, 0))
    hbm = pl.BlockSpec(memory_space=pl.ANY)
    out_specs = [row] + ([row] if want_bf16 else []) + [vec]
    out_shape = [_sds((Tp, D), F32)] + ([_sds((Tp, D), BF16)] if want_bf16 else []) + [_sds((1, D), F32)]
    in_specs = ([pl.BlockSpec((tm, n), lambda i, k: (t0 + i, k))]
                + [pl.BlockSpec((None, D, n), lambda i, k: (k // P, 0, 0))] * P
                + [hbm, hbm, pl.BlockSpec((tm, 1), lambda i, k: (t0 + i, 0)), vec])
    operands = (dz, *w_parts, dres, hin, rin, g)
    aliases = {}
    if earlier is not None:
        in_specs += [hbm, vec]
        operands += tuple(earlier)
        aliases = {P + 5: 0}
    return _hosted(
        body, tasks, name=name, grid=(nt, K), in_specs=in_specs, out_specs=out_specs, out_shape=out_shape,
        scratch_shapes=[pltpu.VMEM((tm, D), F32), pltpu.VMEM((tm, D), F32), pltpu.SemaphoreType.DMA((2,))],
        semantics=("arbitrary", "arbitrary"), operands=operands, aliases=aliases)


def _dmerged_bwd(dh1b, w_out, name):
    Tp, D = dh1b.shape
    tm = _tile(Tp, TILE["dmerged"])

    def body(g_ref, w_ref, o_ref):
        o_ref[...] = lax.dot_general(g_ref[...], w_ref[...], _NT, preferred_element_type=F32)

    row = pl.BlockSpec((tm, D), lambda i: (i, 0))
    return _call(
        body, name=name, grid=(Tp // tm,),
        in_specs=[row, pl.BlockSpec((D, D), lambda i: (0, 0))],
        out_specs=row, out_shape=_sds((Tp, D), F32),
        compiler_params=_cp("parallel"),
    )(dh1b, w_out)


def _pool_bwd(dmerged, proj, y_pool, d_pool, scale, pool_w, name, tasks=()):
    Tp, D = dmerged.shape
    G, Cg, _ = pool_w.shape
    tm = _tile(Tp, TILE["pool"])
    nt = Tp // tm

    def body(dm_ref, gp_ref, y_ref, d_ref, sc_ref, w_ref, dproj_hbm, dw_ref, dsc_ref, next_ref, out_buf, out_sems):
        t = pl.program_id(0)
        tile = nt - 1 - t
        slot = t % 2
        dv_ref, dgp_ref = out_buf.at[slot, 0], out_buf.at[slot, 1]
        tile_rows = pl.ds(pl.multiple_of(tile * tm, tm), tm)

        def out_copies(s):
            return [pltpu.make_async_copy(out_buf.at[s, k], dproj_hbm.at[tile_rows, pl.ds(piece * D, D)],
                                          out_sems.at[2 * s + k]) for k, piece in enumerate((0, 3))]

        @pl.when(t >= 2)
        def _():
            for cp in out_copies(slot):
                cp.wait()

        @pl.when(t == 0)
        def _():
            next_ref[...] = jnp.zeros_like(next_ref)
            dw_ref[...] = jnp.zeros_like(dw_ref)
            dsc_ref[...] = jnp.zeros_like(dsc_ref)

        rows = _row_ids(tile, tm)
        dm = dm_ref[...]
        y = y_ref[...]
        sc = sc_ref[...]
        sg = _sigmoid(gp_ref[...])
        dpo = dm * sg
        dgp_ref[...] = (dm * (y * sc) * sg * (1.0 - sg)).astype(BF16)
        dsc_ref[...] += jnp.sum(dpo * y, axis=0, keepdims=True)
        dyb = (dpo * sc).astype(BF16)
        for g, win in enumerate(POOL_WINDOWS):
            cols = slice(g * Cg, (g + 1) * Cg)
            dy = dyb[:, cols]
            dd = lax.dot_general(dy, w_ref[g], _NT, preferred_element_type=F32)
            dw_ref[g] += lax.dot_general(d_ref[:, cols], dy, _TN, preferred_element_type=F32)
            q = dd / jnp.minimum(rows + 1, win).astype(F32)
            s = jnp.concatenate([q, next_ref[:, cols]], axis=0)
            k = 1
            while k < win:
                s = s + pltpu.roll(s, s.shape[0] - k, axis=0)
                k *= 2
            dv_ref[:, cols] = (s[:tm] - dd).astype(BF16)
            next_ref[:, cols] = q[:MAX_WINDOW]
        for cp in out_copies(slot):
            cp.start()

        @pl.when(t == nt - 1)
        def _():
            for cp in out_copies(slot) + (out_copies(1 - slot) if nt > 1 else []):
                cp.wait()

    row = pl.BlockSpec((tm, D), lambda t: (nt - 1 - t, 0))
    return _hosted(
        body, tasks, name=name, grid=(nt,),
        in_specs=[row, pl.BlockSpec((tm, D), lambda t: (nt - 1 - t, 3)), row, row,
                  pl.BlockSpec((1, D), lambda t: (0, 0)), pl.BlockSpec((G, Cg, Cg), lambda t: (0, 0, 0))],
        out_specs=[pl.BlockSpec(memory_space=pl.ANY), pl.BlockSpec((G, Cg, Cg), lambda t: (0, 0, 0)),
                   pl.BlockSpec((1, D), lambda t: (0, 0))],
        out_shape=[_sds((Tp, proj.shape[1]), BF16), _sds((G, Cg, Cg), F32), _sds((1, D), F32)],
        scratch_shapes=[pltpu.VMEM((MAX_WINDOW, D), F32), pltpu.VMEM((2, 2, tm, D), BF16),
                        pltpu.SemaphoreType.DMA((4,))],
        semantics=("arbitrary",), operands=(dmerged, proj, y_pool, d_pool, scale, pool_w))


LRU_SMALL_ROWS = 8


def _lru_bwd(dmerged, proj, xc, r_gate, i_gate, a_gate, mult_gate, hs, lam, conv_w, wa, wx, dproj, name, tasks=()):
    Tp, D = dmerged.shape
    H, hd, _ = wa.shape
    tm = _tile(Tp, TILE["lru"])
    nt = Tp // tm
    nb = D // hd
    halo_blocks = tm // HALO

    def body(dm_ref, vl_ref, vg_ref, gl_ref, xc_ref, r_ref, i_ref, a_ref, mult_ref, hs_ref, hsp_ref, lam_ref, cw_ref,
             wa_ref, wx_ref, _, dproj_hbm, dwa_ref, dwx_ref, small_ref,
             mu_next_ref, dxc_next_ref, q_s, mu_s, out_buf, out_sems):
        h_id, t = pl.program_id(0), pl.program_id(1)
        tile = nt - 1 - t
        step = h_id * nt + t
        slot = step % 2
        dvl_ref, dvg_ref, dgl_ref = out_buf.at[slot, 0], out_buf.at[slot, 1], out_buf.at[slot, 2]
        tile_rows = pl.ds(pl.multiple_of(tile * tm, tm), tm)

        def out_copies(s):
            return [pltpu.make_async_copy(
                out_buf.at[s, k], dproj_hbm.at[tile_rows, pl.ds(pl.multiple_of((piece * nb + h_id) * hd, hd), hd)],
                out_sems.at[3 * s + k]) for k, piece in enumerate((1, 2, 4))]

        @pl.when(step >= 2)
        def _():
            for cp in out_copies(slot):
                cp.wait()

        @pl.when(t == 0)
        def _():
            mu_next_ref[...] = jnp.zeros_like(mu_next_ref)
            dxc_next_ref[...] = jnp.zeros_like(dxc_next_ref)
            dwa_ref[...] = jnp.zeros_like(dwa_ref)
            dwx_ref[...] = jnp.zeros_like(dwx_ref)
            small_ref[...] = jnp.zeros_like(small_ref)

        first = tile == 0
        dm = dm_ref[...]
        hs_t = hs_ref[...]
        xc_t = xc_ref[...]
        r = r_ref[...]
        i = i_ref[...]
        lam_v = lam_ref[...]
        sp = _softplus(-lam_v)
        a = a_ref[...]
        mult = mult_ref[...]

        sg = _sigmoid(gl_ref[...])
        ge, dge = _gelu_and_grad(vg_ref[...])
        dlo = dm * sg
        dgl_ref[...] = (dm * (hs_t * ge) * sg * (1.0 - sg)).astype(BF16)
        dvg_ref[...] = (dlo * hs_t * dge).astype(BF16)
        dhs = dlo * ge

        q_s[...] = a * dhs
        mu_first = _scan_chunks(a_ref, q_s, mu_s, mu_next_ref[0:1, :], tm, reverse=True)
        lam_t = dhs + _shift_up(mu_s[...], mu_next_ref[...], 1)
        mu_next_ref[...] = jnp.broadcast_to(mu_first, mu_next_ref.shape)

        h_prev = _shift_down(jnp.where(first, 0.0, hsp_ref[...]), hs_t, 1)
        da = lam_t * h_prev
        dmult = lam_t * (i * xc_t)
        di = lam_t * mult * xc_t
        dxc = lam_t * mult * i
        dlog_a = da * a - dmult * (a * a) / mult
        dr = dlog_a * (-LRU_C * sp)
        dlam_rows = dlog_a * (-LRU_C * r)
        dza = dr * r * (1.0 - r)
        dzx = di * i * (1.0 - i)
        dzab, dzxb = dza.astype(BF16), dzx.astype(BF16)
        xcb = xc_t.astype(BF16)
        dxc = dxc + lax.dot_general(dzab, wa_ref[...], _NT, preferred_element_type=F32)
        dxc = dxc + lax.dot_general(dzxb, wx_ref[...], _NT, preferred_element_type=F32)
        dwa_ref[...] += lax.dot_general(xcb, dzab, _TN, preferred_element_type=F32)
        dwx_ref[...] += lax.dot_general(xcb, dzxb, _TN, preferred_element_type=F32)

        dxc_next = dxc_next_ref[...]
        taps = [_shift_up(dxc, dxc_next, CONV_WIDTH - 1 - k) for k in range(CONV_WIDTH)]
        dv = jnp.zeros_like(dxc)
        for k in range(CONV_WIDTH):
            dv = dv + cw_ref[k:k + 1, :] * taps[k]
        dvl_ref[...] = dv.astype(BF16)
        dxc_next_ref[...] = dxc[:HALO, :]

        v_t = vl_ref[...]
        small = [jnp.sum(dza, axis=0, keepdims=True), jnp.sum(dzx, axis=0, keepdims=True),
                 jnp.sum(dlam_rows, axis=0, keepdims=True) * (-_sigmoid(-lam_v)),
                 jnp.sum(dxc, axis=0, keepdims=True)]
        for k in range(CONV_WIDTH):
            small.append(jnp.sum(taps[k] * v_t, axis=0, keepdims=True))
        for k, row in enumerate(small):
            small_ref[k:k + 1, :] += row
        for cp in out_copies(slot):
            cp.start()

        @pl.when(step == H * nt - 1)
        def _():
            for cp in out_copies(slot) + (out_copies(1 - slot) if H * nt > 1 else []):
                cp.wait()

    def piece(p):
        return pl.BlockSpec((tm, hd), lambda h, t: (nt - 1 - t, p * nb + h))

    def halo(p):
        return pl.BlockSpec((HALO, hd), lambda h, t: (jnp.maximum((nt - 1 - t) * halo_blocks - 1, 0), p * nb + h))

    blk = pl.BlockSpec((tm, hd), lambda h, t: (nt - 1 - t, h))
    vec = pl.BlockSpec((1, hd), lambda h, t: (0, h))
    mat = pl.BlockSpec((None, hd, hd), lambda h, t: (h, 0, 0))
    return _hosted(
        body, tasks, name=name, grid=(H, nt),
        in_specs=[blk, piece(1), piece(2), piece(4), blk, blk, blk, blk, blk, blk, halo(0), vec,
                  pl.BlockSpec((CONV_WIDTH, hd), lambda h, t: (0, h)), mat, mat, pl.BlockSpec(memory_space=pl.ANY)],
        out_specs=[pl.BlockSpec(memory_space=pl.ANY), mat, mat,
                   pl.BlockSpec((None, LRU_SMALL_ROWS, hd), lambda h, t: (h, 0, 0))],
        out_shape=[_sds(dproj.shape, BF16)] + [_sds((H, hd, hd), F32)] * 2 + [_sds((H, LRU_SMALL_ROWS, hd), F32)],
        scratch_shapes=[pltpu.VMEM((HALO, hd), F32), pltpu.VMEM((HALO, hd), F32),
                        pltpu.VMEM((tm, hd), F32), pltpu.VMEM((tm, hd), F32),
                        pltpu.VMEM((2, 3, tm, hd), BF16), pltpu.SemaphoreType.DMA((6,))],
        semantics=("arbitrary", "arbitrary"), aliases={15: 0},
        operands=(dmerged, proj, proj, proj, xc, r_gate, i_gate, a_gate, mult_gate, hs, hs, lam, conv_w, wa, wx,
                  dproj))


def _adamw(w, g, m, v):
    m = ADAM_B1 * m + (1.0 - ADAM_B1) * g
    v = ADAM_B2 * v + (1.0 - ADAM_B2) * (g * g)
    m_hat = m / (1.0 - ADAM_B1 ** ADAM_STEP)
    v_hat = v / (1.0 - ADAM_B2 ** ADAM_STEP)
    delta = -ADAM_LR * (m_hat / (jnp.sqrt(v_hat) + ADAM_EPS) + ADAM_WD * w)
    return delta, m, v


def _reduce_update(pair_sums, chip_sums, w, m, v, chip_slot, name, part=(0, 1), earlier=None):
    R, C = pair_sums.shape[1:]
    p, parts = part
    tr = _tile(R, TILE["update"])
    nblk = R // tr

    def body(slot_ref, own_ref, got_ref, w_ref, m_ref, v_ref, *rest):
        g_out, d_out, m_out, v_out = rest[-4:]
        g = own_ref[...].astype(F32)
        for k in range(3):
            g = g + got_ref[k].astype(F32)
        d, m_new, v_new = _adamw(w_ref[...], g, m_ref[...], v_ref[...])
        g_out[...] = g
        d_out[...] = d
        m_out[...] = m_new
        v_out[...] = v_new

    blk = pl.BlockSpec((tr, C), lambda i, s: (p * nblk + i, 0))
    in_specs = [pl.BlockSpec((None, tr, C), lambda i, s: (s[0], i, 0)),
                pl.BlockSpec((3, tr, C), lambda i, s: (0, i, 0)), blk, blk, blk]
    operands = (chip_slot, pair_sums, chip_sums, w, m, v)
    aliases = {}
    if earlier is not None:
        in_specs += [pl.BlockSpec(memory_space=pl.ANY)] * 4
        operands += tuple(earlier)
        aliases = {6 + k: k for k in range(4)}
    return _call(
        body, name=name,
        grid_spec=pltpu.PrefetchScalarGridSpec(
            num_scalar_prefetch=1, grid=(R // tr,), in_specs=in_specs, out_specs=[blk] * 4),
        out_shape=[_sds((parts * R, C), F32)] * 4,
        input_output_aliases=aliases,
        compiler_params=_cp("parallel"),
    )(*operands)


def _small_update(w, g, m, v, name):
    def body(w_ref, g_ref, m_ref, v_ref, d_out, m_out, v_out):
        d, m_new, v_new = _adamw(w_ref[...], g_ref[...], m_ref[...], v_ref[...])
        d_out[...] = d
        m_out[...] = m_new
        v_out[...] = v_new

    return _call(body, name=name, out_shape=[_sds(w.shape, F32)] * 3)(w, g, m, v)


def _slots_from_rows(full, lead):
    L, R, C = full.shape
    r = R // N_DEV
    return full.reshape(L, N_DEV, r, C).transpose(1, 0, 2, 3).reshape(N_DEV, L * r, C)


def _rows_from_slots(slots, lead):
    _, LR, C = slots.shape
    r = LR // lead
    return slots.reshape(N_DEV, lead, r, C).transpose(1, 0, 2, 3).reshape(lead, N_DEV * r, C)


def kernel(x, meta_tokens, norm1_g, w_in, pool_w, pool_scale, conv_w, conv_b, gate_a_w, gate_a_b, gate_x_w, gate_x_b, lru_lambda, w_out, norm2_g, mlp_w1, mlp_w2, final_g, loss_target, m_meta_tokens, m_norm1_g, m_w_in, m_pool_w, m_pool_scale, m_conv_w, m_conv_b, m_gate_a_w, m_gate_a_b, m_gate_x_w, m_gate_x_b, m_lru_lambda, m_w_out, m_norm2_g, m_mlp_w1, m_mlp_w2, m_final_g, v_meta_tokens, v_norm1_g, v_w_in, v_pool_w, v_pool_scale, v_conv_w, v_conv_b, v_gate_a_w, v_gate_a_b, v_gate_x_w, v_gate_x_b, v_lru_lambda, v_w_out, v_norm2_g, v_mlp_w1, v_mlp_w2, v_final_g):
    seq, D = x.shape[1], x.shape[2]
    n_meta = meta_tokens.shape[0]
    G, Cg = pool_w.shape[1], pool_w.shape[3]
    H, hd = gate_a_w.shape[1], gate_a_w.shape[3]
    T = n_meta + seq
    Tp = -(-T // ROW_ALIGN) * ROW_ALIGN
    ix, iy, ic = _pos()
    me = 4 * ix + 2 * iy + ic
    core = jnp.reshape(ic, (1,)).astype(jnp.int32)
    chip_slot = jnp.reshape(2 * ix + iy, (1,)).astype(jnp.int32)

    w_in_l, w1_l, w2_l, w_out_l = w_in[0], mlp_w1[0], mlp_w2[0], w_out[0]
    pool_l = pool_w[0].reshape(G * (Cg // N_DEV), Cg)
    wa_l = gate_a_w[0].reshape(H * (hd // N_DEV), hd)
    wx_l = gate_x_w[0].reshape(H * (hd // N_DEV), hd)
    small_params = jnp.concatenate(
        [meta_tokens, conv_w[0], jnp.zeros((4, D // N_DEV), F32)], axis=0)
    biases = jnp.concatenate([gate_a_b[0], gate_x_b[0]], axis=0)
    (w_in_g, pool_g, wa_g, wx_g, small_g, bias_g) = _all_gather_relay(
        [w_in_l.astype(BF16), pool_l.astype(BF16), wa_l.astype(BF16), wx_l.astype(BF16), small_params, biases],
        [W_IN_CHUNKS, 1, 1, 1, 1, 1], "gather_first")
    pool_full = _rows_from_slots(pool_g, G)
    wa_full = _rows_from_slots(wa_g, H)
    wx_full = _rows_from_slots(wx_g, H)
    small_full = small_g.transpose(1, 0, 2).reshape(n_meta + 8, D)
    meta_full = small_full[:n_meta]
    conv_full = small_full[n_meta:n_meta + CONV_WIDTH]
    bias_full = bias_g.transpose(1, 0, 2).reshape(2 * H, hd)
    ba_full, bx_full = bias_full[:H], bias_full[H:]

    h0 = jnp.concatenate([meta_full, x[0], jnp.zeros((Tp - T, D), F32)], axis=0)
    target = jnp.concatenate([jnp.zeros((n_meta, D), F32), loss_target[0], jnp.zeros((Tp - T, D), F32)], axis=0)
    u, r1 = _norm_fwd(h0, norm1_g, "norm1")
    proj, ((w_out_g,), (w1_ici,)) = _proj_fwd(
        u, w_in_g, "proj", tasks=[_AgFull(w_out_l.astype(BF16)), _AgFull(w1_l.astype(BF16), forward_here=False)])
    w_in_parts = [w_in_g]
    d_pool, y_pool = _pool_fwd(proj, pool_full, "pool_fwd")
    (xc, r_gate, i_gate, a_gate, mult_gate, hs, merged), ((w1_g,), (w2_ici,)) = _lru_fwd(
        proj, y_pool, pool_scale, conv_full, conv_b, wa_full, ba_full, wx_full, bx_full, lru_lambda, "lru_fwd",
        tasks=[_AgForward(w1_ici), _AgFull(w2_l.astype(BF16), forward_here=False)])
    w_out_full = w_out_g.reshape(D, D)
    (h1, u2, r2), ((w2_g,),) = _wout_norm_fwd(merged, w_out_full, h0, norm2_g, "wout_norm2", tasks=[_AgForward(w2_ici)])
    (act, a1), _ = _mlp1_fwd(u2, w1_g, "mlp1")
    dh2, dh2b, loss_tile, d_final_g = _mlp2_loss(
        act, w2_g.reshape(-1, D), h1, target, final_g.reshape(1, D), n_meta, seq, "mlp2_loss")

    def pair(part, got, tag):
        return _pair_sum(part, got, core, "pair_sum_" + tag)

    d_a1 = _dact_bwd(dh2b, w2_g, a1, "dact")
    dw2_p, _ = _weight_grad(act, dh2b, N_DEV, True, "dw2")
    dw1_p, ((dw2_got,),) = _weight_grad(u2, d_a1, N_DEV, False, "dw1", tasks=[_RsSibling(dw2_p)])
    dw2_pair = pair(dw2_p, dw2_got, "w2")
    (dh1, dh1b, d_norm2_g), ((dw2_chips,), (dw1_got,)) = _nt_norm_bwd(
        d_a1, [w1_g], dh2, h1, r2, norm2_g, True, "du2_norm2", tasks=[_RsChips(dw2_pair), _RsSibling(dw1_p)])
    dw1_pair = pair(dw1_p, dw1_got, "w1")
    dmerged = _dmerged_bwd(dh1b, w_out_full, "dmerged")
    dwout_p, _ = _weight_grad(merged, dh1b, 2, True, "dwout")
    dwout_p = dwout_p.reshape(N_DEV, D // N_DEV, D)
    (dproj_pool, dpool_full, d_scale), ((dwout_got,),) = _pool_bwd(
        dmerged, proj, y_pool, d_pool, pool_scale, pool_full, "pool_bwd", tasks=[_RsSibling(dwout_p)])
    dwout_pair = pair(dwout_p, dwout_got, "wout")
    (dproj, dwa_full, dwx_full, lru_small), ((dw1_chips,), (dwout_chips,)) = _lru_bwd(
        dmerged, proj, xc, r_gate, i_gate, a_gate, mult_gate, hs, lru_lambda, conv_full, wa_full, wx_full, dproj_pool,
        "lru_bwd", tasks=[_RsChips(dw1_pair), _RsChips(dwout_pair)])
    dwin_a, _ = _weight_grad(u, dproj, N_DEV, False, "dwin_a", part=(0, 2))
    dwin_b, ((dwin_a_got,),) = _weight_grad(u, dproj, N_DEV, False, "dwin_b", part=(1, 2), tasks=[_RsSibling(dwin_a)])
    dwin_a_pair = pair(dwin_a, dwin_a_got, "win_a")
    dpool_p = _slots_from_rows(dpool_full, G).astype(BF16)
    dwa_p = _slots_from_rows(dwa_full, H).astype(BF16)
    dwx_p = _slots_from_rows(dwx_full, H).astype(BF16)
    late = [dwin_b, dpool_p, dwa_p, dwx_p]
    n_tiles = Tp // _tile(Tp, TILE["nt"])
    n_first = max(1, n_tiles // 2)
    (dh0_a, d_norm1_a), ((dwin_a_chips,), *late_got) = _nt_norm_bwd(
        dproj, w_in_parts, dh1, h0, r1, norm1_g, False, "du_norm1_a",
        tasks=[_RsChips(dwin_a_pair)] + [_RsSibling(p) for p in late], tiles=(0, n_first))
    late_pair = [pair(p, g[0], "late%d" % k) for k, (p, g) in enumerate(zip(late, late_got))]
    (dh0, d_norm1_g), late_chips = _nt_norm_bwd(
        dproj, w_in_parts, dh1, h0, r1, norm1_g, False, "du_norm1_b", tasks=[_RsChips(p) for p in late_pair],
        tiles=(n_first, n_tiles - n_first), earlier=(dh0_a, d_norm1_a))
    grad_x = dh0[n_meta:T][None]

    pair_sums = [dw2_pair, dw1_pair, dwout_pair] + late_pair[1:]
    chip_sums = [dw2_chips, dw1_chips, dwout_chips] + [c[0] for c in late_chips[1:]]
    big = {}
    names = ["mlp_w2", "mlp_w1", "w_out", "pool_w", "gate_a_w", "gate_x_w"]
    trip = {"mlp_w2": (mlp_w2, m_mlp_w2, v_mlp_w2), "mlp_w1": (mlp_w1, m_mlp_w1, v_mlp_w1),
            "w_out": (w_out, m_w_out, v_w_out),
            "pool_w": (pool_w, m_pool_w, v_pool_w), "gate_a_w": (gate_a_w, m_gate_a_w, v_gate_a_w),
            "gate_x_w": (gate_x_w, m_gate_x_w, v_gate_x_w)}
    for k, nm in enumerate(names):
        w_, m_, v_ = trip[nm]
        shape2 = pair_sums[k].shape[1:]
        outs = _reduce_update(pair_sums[k], chip_sums[k], w_.reshape(shape2), m_.reshape(shape2), v_.reshape(shape2),
                              chip_slot, "update_" + nm)
        big[nm] = [o.reshape(w_.shape) for o in outs]
    win2 = [a_[0] for a_ in (w_in, m_w_in, v_w_in)]
    win_a = _reduce_update(dwin_a_pair, dwin_a_chips, *win2, chip_slot, "update_w_in_a", part=(0, 2))
    win_b = _reduce_update(late_pair[0], late_chips[0][0], *win2, chip_slot, "update_w_in_b", part=(1, 2), earlier=win_a)
    big["w_in"] = [o.reshape(w_in.shape) for o in win_b]
    names = names + ["w_in"]

    lru_rows = lru_small.transpose(1, 0, 2).reshape(LRU_SMALL_ROWS, D)
    small_part = jnp.concatenate(
        [dh0[:n_meta], d_norm1_g, d_scale, d_norm2_g, d_final_g, lru_rows, jnp.zeros((4, D), F32)], axis=0)
    (small_all,) = _all_gather([small_part], "gather_small_grads")
    small_sum = _sum_slots(small_all, "sum_small_grads")
    o = n_meta
    g_meta_full = small_sum[:o]
    g_norm1, g_scale, g_norm2, g_final = (small_sum[o + k:o + k + 1] for k in range(4))
    g_ba_full, g_bx_full, g_lam, g_cb = (small_sum[o + 4 + k:o + 5 + k] for k in range(4))
    g_cw_full = small_sum[o + 8:o + 8 + CONV_WIDTH]
    dcol = D // N_DEV
    g_meta = lax.dynamic_slice_in_dim(g_meta_full, me * dcol, dcol, axis=1)
    g_cw = lax.dynamic_slice_in_dim(g_cw_full, me * dcol, dcol, axis=1)
    hcol = hd // N_DEV
    g_ba = lax.dynamic_slice_in_dim(g_ba_full.reshape(H, hd), me * hcol, hcol, axis=1)
    g_bx = lax.dynamic_slice_in_dim(g_bx_full.reshape(H, hd), me * hcol, hcol, axis=1)

    rep_w = jnp.concatenate([norm1_g, pool_scale, conv_b, lru_lambda, norm2_g, final_g.reshape(1, D)], axis=0)
    rep_g = jnp.concatenate([g_norm1, g_scale, g_cb, g_lam, g_norm2, g_final], axis=0)
    rep_m = jnp.concatenate([m_norm1_g, m_pool_scale, m_conv_b, m_lru_lambda, m_norm2_g, m_final_g.reshape(1, D)], axis=0)
    rep_v = jnp.concatenate([v_norm1_g, v_pool_scale, v_conv_b, v_lru_lambda, v_norm2_g, v_final_g.reshape(1, D)], axis=0)
    rep_d, rep_nm, rep_nv = _small_update(rep_w, rep_g, rep_m, rep_v, "update_vectors")
    col_w = jnp.concatenate([meta_tokens, conv_w[0]], axis=0)
    col_g = jnp.concatenate([g_meta, g_cw], axis=0)
    col_m = jnp.concatenate([m_meta_tokens, m_conv_w[0]], axis=0)
    col_v = jnp.concatenate([v_meta_tokens, v_conv_w[0]], axis=0)
    col_d, col_nm, col_nv = _small_update(col_w, col_g, col_m, col_v, "update_columns")
    b_w = jnp.concatenate([gate_a_b[0], gate_x_b[0]], axis=0)
    b_g = jnp.concatenate([g_ba, g_bx], axis=0)
    b_m = jnp.concatenate([m_gate_a_b[0], m_gate_x_b[0]], axis=0)
    b_v = jnp.concatenate([v_gate_a_b[0], v_gate_x_b[0]], axis=0)
    b_d, b_nm, b_nv = _small_update(b_w, b_g, b_m, b_v, "update_biases")

    def rep(arr, k, like):
        return arr[k:k + 1].reshape(like.shape)

    rep_order = {"norm1_g": 0, "pool_scale": 1, "conv_b": 2, "lru_lambda": 3, "norm2_g": 4, "final_g": 5}
    like = {"norm1_g": norm1_g, "pool_scale": pool_scale, "conv_b": conv_b, "lru_lambda": lru_lambda,
            "norm2_g": norm2_g, "final_g": final_g}

    def leaves(kind):
        rep_src = [rep_g, rep_d, rep_nm, rep_nv][kind]
        col_src = [col_g, col_d, col_nm, col_nv][kind]
        b_src = [b_g, b_d, b_nm, b_nv][kind]
        out = {}
        out["meta_tokens"] = col_src[:n_meta]
        out["conv_w"] = col_src[n_meta:][None]
        out["gate_a_b"] = b_src[:H][None]
        out["gate_x_b"] = b_src[H:][None]
        for nm, k in rep_order.items():
            out[nm] = rep(rep_src, k, like[nm])
        for nm in names:
            out[nm] = big[nm][kind]
        order = ["meta_tokens", "norm1_g", "w_in", "pool_w", "pool_scale", "conv_w", "conv_b", "gate_a_w", "gate_a_b",
                 "gate_x_w", "gate_x_b", "lru_lambda", "w_out", "norm2_g", "mlp_w1", "mlp_w2", "final_g"]
        return [out[nm] for nm in order]

    loss = lax.psum(loss_tile[0, 0], ("x", "y", "c"))
    return (loss, grad_x, *leaves(0), *leaves(1), *leaves(2), *leaves(3))
```

```python
import functools

import jax
import jax.numpy as jnp
from jax import lax
from jax.experimental import pallas as pl
from jax.experimental.pallas import tpu as pltpu

F32 = jnp.float32
BF16 = jnp.bfloat16
MESH = pl.DeviceIdType.MESH
N_DEV = 8
POOL_WINDOWS = (2, 4, 8, 16)
MAX_WINDOW = 16
CONV_WIDTH = 4
HALO = 8
LRU_C = 8.0
NORM_EPS = 1e-6
ADAM_LR, ADAM_B1, ADAM_B2, ADAM_EPS, ADAM_WD, ADAM_STEP = 0.001, 0.9, 0.999, 1e-08, 0.01, 10
ROW_ALIGN = 128
VMEM_LIMIT = 56 << 20
TILE = dict(norm=384, proj=1408, pool=384, lru=704, wout=384, mlp1=1408, mlp2=704, dact=1408, tn=1408,
            nt=704, dmerged=384, update=256, pair=1024)
MLP2_K = 1024
EPILOGUE_ROWS = 176
W_IN_CHUNKS = 5
MID_STEP_PERCENT = 88

_NT = (((1,), (1,)), ((), ()))
_TN = (((0,), (0,)), ((), ()))


def _call(body, **kw):
    return pl.pallas_call(body, **kw)


def _cp(*sem):
    return pltpu.CompilerParams(dimension_semantics=sem, vmem_limit_bytes=VMEM_LIMIT)


def _tile(total, pref):
    best = None
    for t in range(16, min(total, pref) + 1, 16):
        if total % t == 0:
            best = t
    assert best is not None, (total, pref)
    return best


def _sds(shape, dtype):
    return jax.ShapeDtypeStruct(shape, dtype)


def _pos():
    return lax.axis_index("x"), lax.axis_index("y"), lax.axis_index("c")


def _all_gather(shards, name):
    n = len(shards)

    def body(*refs):
        ins, outs = refs[:n], refs[n:2 * n]
        send_sems, recv_sems, local_sems = refs[2 * n:]
        x, y, c = _pos()
        me, sib = (x, y, c), (x, y, 1 - c)
        chips = [(1 - x, y), (x, 1 - y), (1 - x, 1 - y)]

        def slot(p):
            return 4 * p[0] + 2 * p[1] + p[2]

        def copy(a, k, block, to, src=None):
            dst = outs[a].at[slot(block)]
            return pltpu.make_async_remote_copy(
                src_ref=dst if src is None else src, dst_ref=dst,
                send_sem=send_sems.at[7 * a + k], recv_sem=recv_sems.at[7 * a + k],
                device_id=to, device_id_type=MESH)

        mine = [pltpu.make_async_copy(ins[a], outs[a].at[slot(me)], local_sems.at[a]) for a in range(n)]
        for m in mine:
            m.start()
        first = []
        for a in range(n):
            first.append(copy(a, 0, me, sib, src=ins[a]))
            first += [copy(a, 1 + j, me, (*chip, c), src=ins[a]) for j, chip in enumerate(chips)]
        for cp in first:
            cp.start()
        passed = []
        for a in range(n):
            for j, chip in enumerate(chips):
                copy(a, 1 + j, (*chip, c), me).wait_recv()
                fwd = copy(a, 4 + j, (*chip, c), sib)
                fwd.start()
                passed.append(fwd)
        for a in range(n):
            copy(a, 0, sib, me).wait_recv()
            for j, chip in enumerate(chips):
                copy(a, 4 + j, (*chip, 1 - c), me).wait_recv()
        for cp in first + passed:
            cp.wait_send()
        for m in mine:
            m.wait()

    hbm = pl.BlockSpec(memory_space=pl.ANY)
    return _call(
        body, name=name,
        out_shape=[_sds((N_DEV,) + s.shape, s.dtype) for s in shards],
        in_specs=[hbm] * n, out_specs=[hbm] * n,
        scratch_shapes=[pltpu.SemaphoreType.DMA((7 * n,)), pltpu.SemaphoreType.DMA((7 * n,)),
                        pltpu.SemaphoreType.DMA((n,))],
    )(*shards)


def _all_gather_relay(shards, chunks, name):
    n = len(shards)
    units = []
    for a, s in enumerate(shards):
        if chunks[a] == 1:
            units.append((a, None, None))
        else:
            w = s.shape[-1] // chunks[a]
            units += [(a, q * w, w) for q in range(chunks[a])]
    nu = len(units)

    def body(*refs):
        ins, outs = refs[:n], refs[n:2 * n]
        send_sems, recv_sems, local_sems = refs[2 * n:]
        x, y, c = _pos()
        me, sib = (x, y, c), (x, y, 1 - c)
        x_nbr, y_nbr, diag = (1 - x, y, c), (x, 1 - y, c), (1 - x, 1 - y, c)
        came_from = (x + (1 - c) * (1 - 2 * x), y + c * (1 - 2 * y), c)
        pass_to = (x + c * (1 - 2 * x), y + (1 - c) * (1 - 2 * y), c)

        def slot(p):
            return 4 * p[0] + 2 * p[1] + p[2]

        def src_view(u):
            a, c0, w = units[u]
            return ins[a] if c0 is None else ins[a].at[:, pl.ds(c0, w)]

        def dst_view(u, p):
            a, c0, w = units[u]
            return outs[a].at[slot(p)] if c0 is None else outs[a].at[slot(p), :, pl.ds(c0, w)]

        def copy(u, k, block, to, from_shard=False):
            dst = dst_view(u, block)
            return pltpu.make_async_remote_copy(
                src_ref=src_view(u) if from_shard else dst, dst_ref=dst,
                send_sem=send_sems.at[7 * u + k], recv_sem=recv_sems.at[7 * u + k],
                device_id=to, device_id_type=MESH)

        mine = [pltpu.make_async_copy(src_view(u), dst_view(u, me), local_sems.at[u]) for u in range(nu)]
        for m in mine:
            m.start()
        sent = []
        for u in range(nu):
            sent += [copy(u, 0, me, sib, True), copy(u, 1, me, x_nbr, True), copy(u, 2, me, y_nbr, True)]
        for cp in sent:
            cp.start()
        for u in range(nu):
            copy(u, 1, x_nbr, me).wait_recv()
            copy(u, 2, y_nbr, me).wait_recv()
            sent += [copy(u, 3, came_from, pass_to), copy(u, 4, x_nbr, sib), copy(u, 5, y_nbr, sib)]
            for cp in sent[-3:]:
                cp.start()
        for u in range(nu):
            copy(u, 3, diag, me).wait_recv()
            sent.append(copy(u, 6, diag, sib))
            sent[-1].start()
        for u in range(nu):
            copy(u, 0, sib, me).wait_recv()
            for k, p in ((4, x_nbr), (5, y_nbr), (6, diag)):
                copy(u, k, (p[0], p[1], 1 - c), me).wait_recv()
        for cp in sent:
            cp.wait_send()
        for m in mine:
            m.wait()

    hbm = pl.BlockSpec(memory_space=pl.ANY)
    return _call(
        body, name=name,
        out_shape=[_sds((N_DEV,) + s.shape, s.dtype) for s in shards],
        in_specs=[hbm] * n, out_specs=[hbm] * n,
        scratch_shapes=[pltpu.SemaphoreType.DMA((7 * nu,)), pltpu.SemaphoreType.DMA((7 * nu,)),
                        pltpu.SemaphoreType.DMA((nu,))],
    )(*shards)


def _other_chips(x, y):
    return [(1 - x, y), (x, 1 - y), (1 - x, 1 - y)]


class _AgFull:
    n_sem, n_local = 7, 1

    def __init__(self, shard, forward_here=True):
        self.forward_here = forward_here
        self.ins = [shard]
        self.out_shapes = [_sds((N_DEV,) + shard.shape, shard.dtype)]
        self.aliases = []

    def _peers(self):
        x, y, c = _pos()
        return [(x, y, 1 - c)] + [(*chip, c) for chip in _other_chips(x, y)]

    def _sends(self, ins, outs, sems):
        send, recv, _, base, _ = sems
        x, y, c = _pos()
        mine = outs[0].at[4 * x + 2 * y + c]
        return [pltpu.make_async_remote_copy(src_ref=ins[0], dst_ref=mine, send_sem=send.at[base + k],
                                             recv_sem=recv.at[base + k], device_id=p, device_id_type=MESH)
                for k, p in enumerate(self._peers())]

    def _arrivals(self, outs, sems):
        send, recv, _, base, _ = sems
        res = []
        for k, p in enumerate(self._peers()):
            blk = outs[0].at[4 * p[0] + 2 * p[1] + p[2]]
            res.append(pltpu.make_async_remote_copy(src_ref=blk, dst_ref=blk, send_sem=send.at[base + k],
                                                    recv_sem=recv.at[base + k], device_id=p, device_id_type=MESH))
        return res

    def _own(self, ins, outs, sems):
        x, y, c = _pos()
        return pltpu.make_async_copy(ins[0], outs[0].at[4 * x + 2 * y + c], sems[2].at[sems[4]])

    def _forwards(self, outs, sems, core_of_block):
        return _forward_copies(outs[0], sems[0], sems[1], sems[3] + 4, core_of_block)

    def start(self, ins, outs, sems):
        self._own(ins, outs, sems).start()
        for cp in self._sends(ins, outs, sems):
            cp.start()

    def mid(self, ins, outs, sems):
        if self.forward_here:
            for cp in self._arrivals(outs, sems)[1:]:
                cp.wait_recv()
            for cp in self._forwards(outs, sems, "mine"):
                cp.start()

    def finish(self, ins, outs, sems):
        if self.forward_here:
            self._arrivals(outs, sems)[0].wait_recv()
            for cp in self._forwards(outs, sems, "sibling"):
                cp.wait_recv()
            for cp in self._sends(ins, outs, sems) + self._forwards(outs, sems, "mine"):
                cp.wait_send()
        else:
            for cp in self._arrivals(outs, sems):
                cp.wait_recv()
            for cp in self._sends(ins, outs, sems):
                cp.wait_send()
        self._own(ins, outs, sems).wait()


def _forward_copies(gathered_ref, send, recv, base, core_of_block):
    x, y, c = _pos()
    res = []
    for k, chip in enumerate(_other_chips(x, y)):
        blk = gathered_ref.at[4 * chip[0] + 2 * chip[1] + (c if core_of_block == "mine" else 1 - c)]
        res.append(pltpu.make_async_remote_copy(src_ref=blk, dst_ref=blk, send_sem=send.at[base + k],
                                                recv_sem=recv.at[base + k], device_id=(x, y, 1 - c),
                                                device_id_type=MESH))
    return res


class _AgForward:
    n_sem, n_local = 3, 0

    def __init__(self, gathered):
        self.ins = [gathered]
        self.out_shapes = [_sds(gathered.shape, gathered.dtype)]
        self.aliases = [(0, 0)]

    def start(self, ins, outs, sems):
        for cp in _forward_copies(outs[0], sems[0], sems[1], sems[3], "mine"):
            cp.start()

    def finish(self, ins, outs, sems):
        for cp in _forward_copies(outs[0], sems[0], sems[1], sems[3], "sibling"):
            cp.wait_recv()
        for cp in _forward_copies(outs[0], sems[0], sems[1], sems[3], "mine"):
            cp.wait_send()


class _RsSibling:
    n_sem, n_local = 4, 0

    def __init__(self, part):
        self.ins = [part]
        self.out_shapes = [_sds((4,) + part.shape[1:], part.dtype)]
        self.aliases = []

    def _copies(self, ins, outs, sems):
        send, recv, _, base, _ = sems
        x, y, c = _pos()
        return [pltpu.make_async_remote_copy(src_ref=ins[0].at[2 * q + (1 - c)], dst_ref=outs[0].at[q],
                                             send_sem=send.at[base + q], recv_sem=recv.at[base + q],
                                             device_id=(x, y, 1 - c), device_id_type=MESH) for q in range(4)]

    def start(self, ins, outs, sems):
        for cp in self._copies(ins, outs, sems):
            cp.start()

    def finish(self, ins, outs, sems):
        for cp in self._copies(ins, outs, sems):
            cp.wait()


class _RsChips:
    n_sem, n_local = 3, 0

    def __init__(self, pair):
        self.ins = [pair]
        self.out_shapes = [_sds((3,) + pair.shape[1:], pair.dtype)]
        self.aliases = []

    def _copies(self, ins, outs, sems):
        send, recv, _, base, _ = sems
        x, y, c = _pos()
        return [pltpu.make_async_remote_copy(src_ref=ins[0].at[2 * chip[0] + chip[1]], dst_ref=outs[0].at[k],
                                             send_sem=send.at[base + k], recv_sem=recv.at[base + k],
                                             device_id=(*chip, c), device_id_type=MESH)
                for k, chip in enumerate(_other_chips(x, y))]

    def start(self, ins, outs, sems):
        for cp in self._copies(ins, outs, sems):
            cp.start()

    def finish(self, ins, outs, sems):
        for cp in self._copies(ins, outs, sems):
            cp.wait()


def _hosted(body, tasks, *, grid, in_specs, out_specs, out_shape, scratch_shapes=(), name, semantics, operands,
            aliases=None):
    in_specs, out_specs, out_shape = list(in_specs), list(out_specs), list(out_shape)
    scratch_shapes = list(scratch_shapes)
    aliases = dict(aliases or {})
    if not tasks:
        res = _call(body, name=name, grid=grid, in_specs=in_specs, out_specs=out_specs, out_shape=out_shape,
                    scratch_shapes=scratch_shapes, input_output_aliases=aliases,
                    compiler_params=_cp(*semantics))(*operands)
        return list(res), []
    n_in, n_out, n_scr = len(in_specs), len(out_specs), len(scratch_shapes)
    t_ins = [a for t in tasks for a in t.ins]
    t_outs = [o for t in tasks for o in t.out_shapes]
    i0, o0 = n_in, n_out
    for t in tasks:
        for (i, o) in t.aliases:
            aliases[i0 + i] = o0 + o
        i0 += len(t.ins)
        o0 += len(t.out_shapes)
    n_sem = sum(t.n_sem for t in tasks)
    n_local = max(1, sum(t.n_local for t in tasks))
    n_steps = 1
    for g in grid:
        n_steps *= g
    mid_step = min(n_steps - 1, (n_steps * MID_STEP_PERCENT) // 100)

    def wrapped(*refs):
        cut = [n_in, len(t_ins), n_out, len(t_outs), n_scr]
        parts, at = [], 0
        for n in cut:
            parts.append(refs[at:at + n])
            at += n
        ins, tin, outs, tout, scratch = parts
        send, recv, local = refs[at:]
        step = pl.program_id(0)
        for d in range(1, len(grid)):
            step = step * grid[d] + pl.program_id(d)

        def each(method):
            i, o, s, l = 0, 0, 0, 0
            for t in tasks:
                if hasattr(t, method):
                    getattr(t, method)(tin[i:i + len(t.ins)], tout[o:o + len(t.out_shapes)], (send, recv, local, s, l))
                i, o, s, l = i + len(t.ins), o + len(t.out_shapes), s + t.n_sem, l + t.n_local

        @pl.when(step == 0)
        def _():
            each("start")

        body(*ins, *outs, *scratch)

        @pl.when(step == mid_step)
        def _():
            each("mid")

        @pl.when(step == n_steps - 1)
        def _():
            each("finish")

    hbm = pl.BlockSpec(memory_space=pl.ANY)
    res = _call(
        wrapped, name=name, grid=grid,
        in_specs=in_specs + [hbm] * len(t_ins), out_specs=out_specs + [hbm] * len(t_outs),
        out_shape=out_shape + t_outs,
        scratch_shapes=scratch_shapes + [pltpu.SemaphoreType.DMA((n_sem,)), pltpu.SemaphoreType.DMA((n_sem,)),
                                         pltpu.SemaphoreType.DMA((n_local,))],
        input_output_aliases=aliases,
        compiler_params=_cp(*(["arbitrary"] * len(grid))),
    )(*operands, *t_ins)
    res = list(res)
    task_outs, o = [], n_out
    for t in tasks:
        task_outs.append(res[o:o + len(t.out_shapes)])
        o += len(t.out_shapes)
    return res[:n_out], task_outs


def _pair_sum(part, got, core, name):
    _, R, C = part.shape
    tr = _tile(R, TILE["pair"]) if R % 16 == 0 else R

    def body(core_ref, p_ref, g_ref, o_ref):
        o_ref[...] = (p_ref[...].astype(F32) + g_ref[...].astype(F32)).astype(o_ref.dtype)

    return _call(
        body, name=name,
        grid_spec=pltpu.PrefetchScalarGridSpec(
            num_scalar_prefetch=1, grid=(4, R // tr),
            in_specs=[pl.BlockSpec((None, tr, C), lambda q, i, cr: (2 * q + cr[0], i, 0)),
                      pl.BlockSpec((None, tr, C), lambda q, i, cr: (q, i, 0))],
            out_specs=pl.BlockSpec((None, tr, C), lambda q, i, cr: (q, i, 0))),
        out_shape=_sds((4, R, C), part.dtype),
        compiler_params=_cp("parallel", "parallel"),
    )(core, part, got)


def _sum_slots(gathered, name):
    _, R, C = gathered.shape

    def body(g_ref, o_ref):
        acc = g_ref[0]
        for s in range(1, N_DEV):
            acc = acc + g_ref[s]
        o_ref[...] = acc

    return _call(body, name=name, out_shape=_sds((R, C), F32))(gathered)


def _sigmoid(z):
    return jax.nn.sigmoid(z)


def _softplus(z):
    e = jnp.exp(-jnp.abs(z))
    log1p_e = jnp.where(e < 0.01, e * (1.0 - e * (0.5 - e * (1.0 / 3.0))), jnp.log(1.0 + e))
    return jnp.maximum(z, 0.0) + log1p_e


_GELU_K = 0.7978845608028654
_GELU_C = 0.044715


def _gelu_and_grad(z):
    t = jnp.tanh(_GELU_K * (z + _GELU_C * z * z * z))
    g = 0.5 * z * (1.0 + t)
    dg = 0.5 * (1.0 + t) + 0.5 * z * (1.0 - t * t) * _GELU_K * (1.0 + 3.0 * _GELU_C * z * z)
    return g, dg


def _gelu(z):
    t = jnp.tanh(_GELU_K * (z + _GELU_C * z * z * z))
    return 0.5 * z * (1.0 + t)


def _row_ids(tile_index, tm, width=1):
    return tile_index * tm + lax.broadcasted_iota(jnp.int32, (tm, width), 0)


def _shift_down(prev, cur, k):
    if k == 0:
        return cur
    ext = jnp.concatenate([prev, cur], axis=0)
    return pltpu.roll(ext, k, axis=0)[prev.shape[0]:]


def _shift_up(cur, nxt, k):
    if k == 0:
        return cur
    ext = jnp.concatenate([cur, nxt], axis=0)
    return pltpu.roll(ext, ext.shape[0] - k, axis=0)[:cur.shape[0]]


def _lru_gates(r, sp):
    z = LRU_C * r * sp
    a = jnp.exp(-z)
    t = jnp.tanh(z)
    mult = jnp.sqrt(2.0 * t / (1.0 + t))
    return a, mult


def _scan_chunks(a_ref, b_ref, out_ref, carry, n_rows, reverse):
    n_chunks = n_rows // 8
    cols = a_ref.shape[1]
    rid = lax.broadcasted_iota(jnp.int32, (8, cols), 0)
    edge = 0 if reverse else 7

    def chunk(k, h):
        ci = (n_chunks - 1 - k) if reverse else k
        rows = pl.ds(pl.multiple_of(ci * 8, 8), 8)
        a = a_ref[rows, :]
        b = b_ref[rows, :]
        for s in (1, 2, 4):
            if reverse:
                keep = rid < 8 - s
                a_n, b_n = pltpu.roll(a, 8 - s, axis=0), pltpu.roll(b, 8 - s, axis=0)
            else:
                keep = rid >= s
                a_n, b_n = pltpu.roll(a, s, axis=0), pltpu.roll(b, s, axis=0)
            b = a * jnp.where(keep, b_n, 0.0) + b
            a = a * jnp.where(keep, a_n, 1.0)
        out_ref[rows, :] = a * h + b
        a_e = jnp.sum(jnp.where(rid == edge, a, 0.0), axis=0, keepdims=True)
        b_e = jnp.sum(jnp.where(rid == edge, b, 0.0), axis=0, keepdims=True)
        return a_e * h + b_e

    return lax.fori_loop(0, n_chunks, chunk, carry, unroll=4 if n_chunks % 4 == 0 else 1)


def _norm_fwd(h, g, name):
    Tp, D = h.shape
    tm = _tile(Tp, TILE["norm"])

    def body(h_ref, g_ref, u_ref, r_ref):
        x = h_ref[...]
        r = lax.rsqrt(jnp.mean(x * x, axis=-1, keepdims=True) + NORM_EPS)
        u_ref[...] = (x * r * g_ref[...]).astype(BF16)
        r_ref[...] = r

    return _call(
        body, name=name, grid=(Tp // tm,),
        in_specs=[pl.BlockSpec((tm, D), lambda i: (i, 0)), pl.BlockSpec((1, D), lambda i: (0, 0))],
        out_specs=[pl.BlockSpec((tm, D), lambda i: (i, 0)), pl.BlockSpec((tm, 1), lambda i: (i, 0))],
        out_shape=[_sds((Tp, D), BF16), _sds((Tp, 1), F32)],
        compiler_params=_cp("parallel"),
    )(h, g)


def _proj_gather(u, w_shard, extras, order, name, tasks=()):
    Tp, K = u.shape
    n = w_shard.shape[1]
    tm = _tile(Tp, TILE["proj"])
    nt = Tp // tm
    shards = [w_shard] + list(extras)
    nu = len(shards)
    t_ins = [a for t in tasks for a in t.ins]
    t_outs = [o for t in tasks for o in t.out_shapes]
    assert not any(t.aliases for t in tasks)
    n_task_sem = sum(t.n_sem for t in tasks)
    n_task_local = sum(t.n_local for t in tasks)

    def body(ord_ref, a_ref, *refs):
        cut = [nu, len(t_ins), 1, nu, len(t_outs)]
        parts, at = [], 0
        for m in cut:
            parts.append(refs[at:at + m])
            at += m
        ins, tin, (proj_ref,), outs, tout = parts
        wbuf, wsem, send, recv, local = refs[at:]
        k, i = pl.program_id(0), pl.program_id(1)
        x, y, c = _pos()
        me, sib = (x, y, c), (x, y, 1 - c)
        x_nbr, y_nbr, diag = (1 - x, y, c), (x, 1 - y, c), (1 - x, 1 - y, c)
        came_from = (x + (1 - c) * (1 - 2 * x), y + c * (1 - 2 * y), c)
        pass_to = (x + c * (1 - 2 * x), y + (1 - c) * (1 - 2 * y), c)

        def other(p):
            return (p[0], p[1], 1 - c)

        def slot(p):
            return 4 * p[0] + 2 * p[1] + p[2]

        def copy(u_, kk, block, to, from_shard=False):
            dst = outs[u_].at[slot(block)]
            return pltpu.make_async_remote_copy(
                src_ref=ins[u_] if from_shard else dst, dst_ref=dst,
                send_sem=send.at[7 * u_ + kk], recv_sem=recv.at[7 * u_ + kk], device_id=to, device_id_type=MESH)

        def own_copy(u_):
            return pltpu.make_async_copy(ins[u_], outs[u_].at[slot(me)], local.at[u_])

        def w_load(kk):
            return pltpu.make_async_copy(outs[0].at[ord_ref[kk]], wbuf.at[kk % 2], wsem.at[kk % 2])

        def each_task(method):
            ti, to_, s_, l_ = 0, 0, 7 * nu, nu
            for t in tasks:
                if hasattr(t, method):
                    getattr(t, method)(tin[ti:ti + len(t.ins)], tout[to_:to_ + len(t.out_shapes)],
                                       (send, recv, local, s_, l_))
                ti, to_, s_, l_ = ti + len(t.ins), to_ + len(t.out_shapes), s_ + t.n_sem, l_ + t.n_local

        def all_units(make):
            return [make(u_) for u_ in range(nu)]

        def ready(kk):
            if kk == 1:
                for cp in all_units(lambda u_: copy(u_, 0, sib, me)):
                    cp.wait_recv()
            elif kk == 2:
                for u_ in range(nu):
                    copy(u_, 1, x_nbr, me).wait_recv()
                    copy(u_, 4, x_nbr, sib).start()

                @pl.when(c == 0)
                def _():
                    for u_ in range(nu):
                        copy(u_, 3, came_from, pass_to).start()
            elif kk == 3:
                for u_ in range(nu):
                    copy(u_, 2, y_nbr, me).wait_recv()
                    copy(u_, 5, y_nbr, sib).start()

                @pl.when(c == 1)
                def _():
                    for u_ in range(nu):
                        copy(u_, 3, came_from, pass_to).start()
                each_task("start")
            elif kk == 4:
                for cp in all_units(lambda u_: copy(u_, 4, other(x_nbr), me)):
                    cp.wait_recv()
            elif kk == 5:
                for cp in all_units(lambda u_: copy(u_, 5, other(y_nbr), me)):
                    cp.wait_recv()
            elif kk == 6:
                for u_ in range(nu):
                    copy(u_, 3, diag, me).wait_recv()
                    copy(u_, 6, diag, sib).start()
            elif kk == 7:
                for cp in all_units(lambda u_: copy(u_, 6, other(diag), me)):
                    cp.wait_recv()
                each_task("mid")

        @pl.when((k == 0) & (i == 0))
        def _():
            for cp in all_units(own_copy):
                cp.start()
            for u_ in range(nu):
                for kk, p in ((0, sib), (1, x_nbr), (2, y_nbr)):
                    copy(u_, kk, me, p, True).start()
            own_copy(0).wait()
            w_load(0).start()

        for kk in range(1, N_DEV):
            @pl.when((k == kk - 1) & (i == nt - 1))
            def _(kk=kk):
                ready(kk)
                w_load(kk).start()

        for kk in range(N_DEV):
            @pl.when((k == kk) & (i == 0))
            def _(kk=kk):
                w_load(kk).wait()

        proj_ref[...] = jnp.dot(a_ref[...], wbuf[k % 2], preferred_element_type=F32)

        @pl.when((k == N_DEV - 1) & (i == nt - 1))
        def _():
            for u_ in range(nu):
                for kk, p in ((0, sib), (1, x_nbr), (2, y_nbr)):
                    copy(u_, kk, me, p, True).wait_send()
                copy(u_, 3, came_from, pass_to).wait_send()
                for kk, p in ((4, x_nbr), (5, y_nbr), (6, diag)):
                    copy(u_, kk, p, sib).wait_send()
            for u_ in range(1, nu):
                own_copy(u_).wait()
            each_task("finish")

    hbm = pl.BlockSpec(memory_space=pl.ANY)
    res = _call(
        body, name=name,
        grid_spec=pltpu.PrefetchScalarGridSpec(
            num_scalar_prefetch=1, grid=(N_DEV, nt),
            in_specs=[pl.BlockSpec((tm, K), lambda k, i, o: (i, 0))] + [hbm] * (nu + len(t_ins)),
            out_specs=[pl.BlockSpec((tm, n), lambda k, i, o: (i, o[k]))] + [hbm] * (nu + len(t_outs)),
            scratch_shapes=[pltpu.VMEM((2, K, n), w_shard.dtype), pltpu.SemaphoreType.DMA((2,)),
                            pltpu.SemaphoreType.DMA((7 * nu + n_task_sem,)),
                            pltpu.SemaphoreType.DMA((7 * nu + n_task_sem,)),
                            pltpu.SemaphoreType.DMA((nu + max(1, n_task_local),))]),
        out_shape=[_sds((Tp, N_DEV * n), F32)] + [_sds((N_DEV,) + s.shape, s.dtype) for s in shards] + t_outs,
        compiler_params=_cp("arbitrary", "arbitrary"),
    )(order, u, *shards, *t_ins)
    res = list(res)
    task_outs, o = [], 1 + nu
    for t in tasks:
        task_outs.append(res[o:o + len(t.out_shapes)])
        o += len(t.out_shapes)
    return res[0], res[1], res[2:1 + nu], task_outs


def _proj_fwd(u, w_slots, name, tasks=(), part=(0, 1), earlier=None):
    Tp, K = u.shape
    S, _, n = w_slots.shape
    p, parts = part
    tm = _tile(Tp, TILE["proj"])

    def body(a_ref, b_ref, *rest):
        o_ref = rest[-1]
        o_ref[...] = jnp.dot(a_ref[...], b_ref[...], preferred_element_type=F32)

    in_specs = [pl.BlockSpec((tm, K), lambda j, i: (i, 0)), pl.BlockSpec((None, K, n), lambda j, i: (j, 0, 0))]
    operands = (u, w_slots)
    aliases = {}
    if earlier is not None:
        in_specs.append(pl.BlockSpec(memory_space=pl.ANY))
        operands += (earlier,)
        aliases = {2: 0}
    (proj,), extra = _hosted(
        body, tasks, name=name, grid=(S, Tp // tm), in_specs=in_specs,
        out_specs=[pl.BlockSpec((tm, n), lambda j, i: (i, j * parts + p))],
        out_shape=[_sds((Tp, S * parts * n), F32)],
        semantics=("parallel", "parallel"), operands=operands, aliases=aliases)
    return proj, extra


def _mlp1_fwd(u2, w_slots, name, tasks=()):
    Tp, K = u2.shape
    S, _, n = w_slots.shape
    tm = _tile(Tp, TILE["mlp1"])

    def body(a_ref, b_ref, act_ref, a1_ref):
        a1 = jnp.dot(a_ref[...], b_ref[...], preferred_element_type=F32)
        relu = jnp.maximum(a1, 0.0)
        act_ref[...] = (relu * relu).astype(BF16)
        a1_ref[...] = a1.astype(BF16)

    return _hosted(
        body, tasks, name=name, grid=(S, Tp // tm),
        in_specs=[pl.BlockSpec((tm, K), lambda j, i: (i, 0)), pl.BlockSpec((None, K, n), lambda j, i: (j, 0, 0))],
        out_specs=[pl.BlockSpec((tm, n), lambda j, i: (i, j))] * 2,
        out_shape=[_sds((Tp, S * n), BF16)] * 2,
        semantics=("parallel", "parallel"), operands=(u2, w_slots))


def _pool_fwd(proj, pool_w, name):
    Tp = proj.shape[0]
    G, Cg, _ = pool_w.shape
    D = G * Cg
    tm = _tile(Tp, TILE["pool"])

    def body(v_ref, w_ref, d_ref, y_ref, prev_ref):
        t = pl.program_id(0)

        @pl.when(t == 0)
        def _():
            prev_ref[...] = jnp.zeros_like(prev_ref)

        rows = _row_ids(t, tm)
        for g, win in enumerate(POOL_WINDOWS):
            cols = slice(g * Cg, (g + 1) * Cg)
            v = v_ref[:, cols]
            s = jnp.concatenate([prev_ref[:, cols], v], axis=0)
            k = 1
            while k < win:
                s = s + pltpu.roll(s, k, axis=0)
                k *= 2
            cnt = jnp.minimum(rows + 1, win).astype(F32)
            d = s[MAX_WINDOW:] / cnt - v
            d_ref[:, cols] = d.astype(BF16)
            y_ref[:, cols] = jnp.dot(d.astype(BF16), w_ref[g], preferred_element_type=F32)
        prev_ref[...] = v_ref[tm - MAX_WINDOW:, :]

    return _call(
        body, name=name, grid=(Tp // tm,),
        in_specs=[pl.BlockSpec((tm, D), lambda t: (t, 0)), pl.BlockSpec((G, Cg, Cg), lambda t: (0, 0, 0))],
        out_specs=[pl.BlockSpec((tm, D), lambda t: (t, 0))] * 2,
        out_shape=[_sds((Tp, D), BF16), _sds((Tp, D), F32)],
        scratch_shapes=[pltpu.VMEM((MAX_WINDOW, D), F32)],
        compiler_params=_cp("arbitrary"),
    )(proj, pool_w)


def _lru_fwd(proj, y_pool, scale, conv_w, conv_b, wa, ba, wx, bx, lam, name, tasks=()):
    Tp = proj.shape[0]
    H, hd, _ = wa.shape
    D = H * hd
    tm = _tile(Tp, TILE["lru"])
    nb = D // hd

    def body(vl_ref, vg_ref, gp_ref, gl_ref, y_ref, sc_ref, cw_ref, cb_ref, wa_ref, ba_ref, wx_ref, bx_ref,
             lam_ref, xc_ref, r_ref, i_ref, a_ref, mult_ref, hs_ref, m_ref, prev_ref, carry_ref, b_s):
        t = pl.program_id(1)

        @pl.when(t == 0)
        def _():
            prev_ref[...] = jnp.zeros_like(prev_ref)
            carry_ref[...] = jnp.zeros_like(carry_ref)

        v = vl_ref[...]
        prev = prev_ref[...]
        xc = jnp.zeros_like(v) + cb_ref[...]
        for k in range(CONV_WIDTH):
            xc = xc + cw_ref[k:k + 1, :] * _shift_down(prev, v, CONV_WIDTH - 1 - k)
        prev_ref[...] = v[tm - HALO:, :]
        xcb = xc.astype(BF16)
        r = _sigmoid(jnp.dot(xcb, wa_ref[...], preferred_element_type=F32) + ba_ref[...])
        i = _sigmoid(jnp.dot(xcb, wx_ref[...], preferred_element_type=F32) + bx_ref[...])
        a, mult = _lru_gates(r, _softplus(-lam_ref[...]))
        a_ref[...] = a
        mult_ref[...] = mult
        b_s[...] = mult * (i * xc)
        xc_ref[...] = xc
        r_ref[...] = r
        i_ref[...] = i
        carry_ref[0:1, :] = _scan_chunks(a_ref, b_s, hs_ref, carry_ref[0:1, :], tm, reverse=False)
        lru_out = hs_ref[...] * _gelu(vg_ref[...])
        pool_out = y_ref[...] * sc_ref[...]
        m_ref[...] = (_sigmoid(gp_ref[...]) * pool_out + _sigmoid(gl_ref[...]) * lru_out).astype(BF16)

    def piece(p):
        return pl.BlockSpec((tm, hd), lambda h, t: (t, p * nb + h))

    blk = pl.BlockSpec((tm, hd), lambda h, t: (t, h))
    vec = pl.BlockSpec((1, hd), lambda h, t: (0, h))
    mat = pl.BlockSpec((None, hd, hd), lambda h, t: (h, 0, 0))
    bias = pl.BlockSpec((None, 1, hd), lambda h, t: (h, 0, 0))
    return _hosted(
        body, tasks, name=name, grid=(H, Tp // tm),
        in_specs=[piece(1), piece(2), piece(3), piece(4), blk, vec,
                  pl.BlockSpec((CONV_WIDTH, hd), lambda h, t: (0, h)), vec, mat, bias, mat, bias, vec],
        out_specs=[blk] * 7,
        out_shape=[_sds((Tp, D), F32)] * 6 + [_sds((Tp, D), BF16)],
        scratch_shapes=[pltpu.VMEM((HALO, hd), F32), pltpu.VMEM((8, hd), F32), pltpu.VMEM((tm, hd), F32)],
        semantics=("parallel", "arbitrary"),
        operands=(proj, proj, proj, proj, y_pool, scale, conv_w, conv_b, wa, ba.reshape(H, 1, hd), wx,
                  bx.reshape(H, 1, hd), lam))


def _wout_norm_fwd(merged, w_out, h0, g2, name, tasks=()):
    Tp, D = h0.shape
    tm = _tile(Tp, TILE["wout"])

    def body(m_ref, w_ref, h0_ref, g_ref, h1_ref, u2_ref, r2_ref):
        h1 = h0_ref[...] + jnp.dot(m_ref[...], w_ref[...], preferred_element_type=F32)
        r = lax.rsqrt(jnp.mean(h1 * h1, axis=-1, keepdims=True) + NORM_EPS)
        h1_ref[...] = h1
        u2_ref[...] = (h1 * r * g_ref[...]).astype(BF16)
        r2_ref[...] = r

    row = pl.BlockSpec((tm, D), lambda i: (i, 0))
    return _hosted(
        body, tasks, name=name, grid=(Tp // tm,),
        in_specs=[row, pl.BlockSpec((D, D), lambda i: (0, 0)), row, pl.BlockSpec((1, D), lambda i: (0, 0))],
        out_specs=[row, row, pl.BlockSpec((tm, 1), lambda i: (i, 0))],
        out_shape=[_sds((Tp, D), F32), _sds((Tp, D), BF16), _sds((Tp, 1), F32)],
        semantics=("parallel",), operands=(merged, w_out, h0, g2))


def _mlp2_loss(act, w2, h1, target, gf, n_meta, seq, name):
    Tp, D = h1.shape
    K = act.shape[1]
    tm = _tile(Tp, TILE["mlp2"])
    tk = min(K, MLP2_K)
    nk = K // tk

    rc = _tile(tm, EPILOGUE_ROWS)

    def body(a_ref, w_ref, h1_hbm, t_hbm, g_ref, dh_ref, dhb_ref, loss_ref, dg_ref, h1_buf, t_buf, sems):
        i, k = pl.program_id(0), pl.program_id(1)
        tile_rows = pl.ds(pl.multiple_of(i * tm, tm), tm)
        fetch = [pltpu.make_async_copy(h1_hbm.at[tile_rows, :], h1_buf, sems.at[0]),
                 pltpu.make_async_copy(t_hbm.at[tile_rows, :], t_buf, sems.at[1])]

        @pl.when(k == 0)
        def _():
            for f in fetch:
                f.start()
            dh_ref[...] = jnp.zeros_like(dh_ref)

        @pl.when((i == 0) & (k == 0))
        def _():
            loss_ref[...] = jnp.zeros_like(loss_ref)
            dg_ref[...] = jnp.zeros_like(dg_ref)

        dh_ref[...] += jnp.dot(a_ref[...], w_ref[...], preferred_element_type=F32)

        @pl.when(k == nk - 1)
        def _():
            for f in fetch:
                f.wait()
            g = g_ref[...]

            def chunk(c, carry):
                loss_acc, dg_acc = carry
                rows = pl.ds(pl.multiple_of(c * rc, rc), rc)
                h2 = h1_buf[rows, :] + dh_ref[rows, :]
                r = lax.rsqrt(jnp.mean(h2 * h2, axis=-1, keepdims=True) + NORM_EPS)
                out = h2 * r * g
                row_id = i * tm + c * rc + lax.broadcasted_iota(jnp.int32, (rc, 1), 0)
                valid = (row_id >= n_meta) & (row_id < n_meta + seq)
                diff = jnp.where(valid, out - t_buf[rows, :], 0.0)
                dout = diff / D
                dog = dout * g
                dh = r * dog - h2 * (r * r * r * jnp.mean(dog * h2, axis=-1, keepdims=True))
                dh_ref[rows, :] = dh
                dhb_ref[rows, :] = dh.astype(BF16)
                loss_acc = loss_acc + 0.5 * jnp.sum(jnp.mean(diff * diff, axis=-1, keepdims=True), axis=0, keepdims=True)
                return loss_acc, dg_acc + jnp.sum(dout * (h2 * r), axis=0, keepdims=True)

            loss_sum, dg_sum = lax.fori_loop(0, tm // rc, chunk, (jnp.zeros((1, 1), F32), jnp.zeros((1, D), F32)))
            loss_ref[...] += loss_sum
            dg_ref[...] += dg_sum

    row = pl.BlockSpec((tm, D), lambda i, k: (i, 0))
    hbm = pl.BlockSpec(memory_space=pl.ANY)
    return _call(
        body, name=name, grid=(Tp // tm, nk),
        in_specs=[pl.BlockSpec((tm, tk), lambda i, k: (i, k)), pl.BlockSpec((tk, D), lambda i, k: (k, 0)),
                  hbm, hbm, pl.BlockSpec((1, D), lambda i, k: (0, 0))],
        out_specs=[row, row, pl.BlockSpec((8, 128), lambda i, k: (0, 0)), pl.BlockSpec((1, D), lambda i, k: (0, 0))],
        out_shape=[_sds((Tp, D), F32), _sds((Tp, D), BF16), _sds((8, 128), F32), _sds((1, D), F32)],
        scratch_shapes=[pltpu.VMEM((tm, D), F32), pltpu.VMEM((tm, D), F32), pltpu.SemaphoreType.DMA((2,))],
        compiler_params=_cp("arbitrary", "arbitrary"),
    )(act, w2, h1, target, gf)


def _dact_bwd(dh2b, w2_slots, a1, name):
    Tp, D = dh2b.shape
    S, n, _ = w2_slots.shape
    tm = _tile(Tp, TILE["dact"])

    def body(g_ref, w_ref, a1_ref, o_ref):
        dact = lax.dot_general(g_ref[...], w_ref[...], _NT, preferred_element_type=F32)
        o_ref[...] = (dact * (2.0 * jnp.maximum(a1_ref[...].astype(F32), 0.0))).astype(BF16)

    return _call(
        body, name=name, grid=(S, Tp // tm),
        in_specs=[pl.BlockSpec((tm, D), lambda j, i: (i, 0)), pl.BlockSpec((None, n, D), lambda j, i: (j, 0, 0)),
                  pl.BlockSpec((tm, n), lambda j, i: (i, j))],
        out_specs=pl.BlockSpec((tm, n), lambda j, i: (i, j)),
        out_shape=_sds((Tp, S * n), BF16),
        compiler_params=_cp("parallel", "parallel"),
    )(dh2b, w2_slots, a1)


def _weight_grad(a, g, blocks, block_a, name, tasks=(), part=(0, 1)):
    Tp, Ka = a.shape
    Ng = g.shape[1]
    p, parts = part
    assert parts == 1 or not block_a
    ka = Ka // blocks if block_a else Ka // parts
    ng = Ng if block_a else Ng // blocks
    tt = _tile(Tp, TILE["tn"])
    nt = Tp // tt

    def body(a_ref, g_ref, o_ref, acc_ref):
        t = pl.program_id(1)

        @pl.when(t == 0)
        def _():
            acc_ref[...] = jnp.zeros_like(acc_ref)

        acc_ref[...] += lax.dot_general(a_ref[...], g_ref[...], _TN, preferred_element_type=F32)

        @pl.when(t == nt - 1)
        def _():
            o_ref[...] = acc_ref[...].astype(o_ref.dtype)

    if block_a:
        a_spec = pl.BlockSpec((tt, ka), lambda j, t: (t, j))
        g_spec = pl.BlockSpec((tt, ng), lambda j, t: (t, 0))
    else:
        a_spec = pl.BlockSpec((tt, ka), lambda j, t: (t, p))
        g_spec = pl.BlockSpec((tt, ng), lambda j, t: (t, j))
    (dw,), extra = _hosted(
        body, tasks, name=name, grid=(blocks, nt),
        in_specs=[a_spec, g_spec],
        out_specs=[pl.BlockSpec((None, ka, ng), lambda j, t: (j, 0, 0))],
        out_shape=[_sds((blocks, ka, ng), BF16)],
        scratch_shapes=[pltpu.VMEM((ka, ng), F32)],
        semantics=("parallel", "arbitrary"), operands=(a, g))
    return dw, extra


def _nt_norm_bwd(dz, w_parts, dres, hin, rin, g, want_bf16, name, tasks=(), tiles=None, earlier=None):
    Tp, D = hin.shape
    P = len(w_parts)
    S, _, n = w_parts[0].shape
    K = S * P
    tm = _tile(Tp, TILE["nt"])
    t0, nt = tiles if tiles is not None else (0, Tp // tm)
    assert not (want_bf16 and earlier is not None)

    rc = _tile(tm, EPILOGUE_ROWS)

    def body(dz_ref, *rest):
        w_refs, (dres_hbm, h_hbm, r_ref, g_ref), rest = rest[:P], rest[P:P + 4], rest[P + 4:]
        if earlier is not None:
            _, dg0_ref, dh_ref, dg_ref, dres_buf, h_buf, sems = rest
        elif want_bf16:
            dh_ref, dhb_ref, dg_ref, dres_buf, h_buf, sems = rest
        else:
            dh_ref, dg_ref, dres_buf, h_buf, sems = rest
        i, k = pl.program_id(0), pl.program_id(1)
        tile_rows = pl.ds(pl.multiple_of((t0 + i) * tm, tm), tm)
        fetch = [pltpu.make_async_copy(dres_hbm.at[tile_rows, :], dres_buf, sems.at[0]),
                 pltpu.make_async_copy(h_hbm.at[tile_rows, :], h_buf, sems.at[1])]

        @pl.when(k == 0)
        def _():
            for f in fetch:
                f.start()
            dh_ref[...] = jnp.zeros_like(dh_ref)

        @pl.when((i == 0) & (k == 0))
        def _():
            dg_ref[...] = jnp.zeros_like(dg_ref) if earlier is None else dg0_ref[...]

        for q in range(P):
            @pl.when(k % P == q)
            def _(q=q):
                dh_ref[...] += lax.dot_general(dz_ref[...], w_refs[q][...], _NT, preferred_element_type=F32)

        @pl.when(k == K - 1)
        def _():
            for f in fetch:
                f.wait()
            g = g_ref[...]

            def chunk(c, dg_acc):
                rows = pl.ds(pl.multiple_of(c * rc, rc), rc)
                du = dh_ref[rows, :]
                h = h_buf[rows, :]
                r = r_ref[rows, :]
                dug = du * g
                dh = dres_buf[rows, :] + r * dug - h * (r * r * r * jnp.mean(dug * h, axis=-1, keepdims=True))
                dh_ref[rows, :] = dh
                if want_bf16:
                    dhb_ref[rows, :] = dh.astype(BF16)
                return dg_acc + jnp.sum(du * (h * r), axis=0, keepdims=True)

            dg_ref[...] += lax.fori_loop(0, tm // rc, chunk, jnp.zeros((1, D), F32))

    row = pl.BlockSpec((tm, D), lambda i, k: (t0 + i, 0))
    vec = pl.BlockSpec((1, D), lambda i, k: (0, 0))
    hbm = pl.BlockSpec(memory_space=pl.ANY)
    out_specs = [row] + ([row] if want_bf16 else []) + [vec]
    out_shape = [_sds((Tp, D), F32)] + ([_sds((Tp, D), BF16)] if want_bf16 else []) + [_sds((1, D), F32)]
    in_specs = ([pl.BlockSpec((tm, n), lambda i, k: (t0 + i, k))]
                + [pl.BlockSpec((None, D, n), lambda i, k: (k // P, 0, 0))] * P
                + [hbm, hbm, pl.BlockSpec((tm, 1), lambda i, k: (t0 + i, 0)), vec])
    operands = (dz, *w_parts, dres, hin, rin, g)
    aliases = {}
    if earlier is not None:
        in_specs += [hbm, vec]
        operands += tuple(earlier)
        aliases = {P + 5: 0}
    return _hosted(
        body, tasks, name=name, grid=(nt, K), in_specs=in_specs, out_specs=out_specs, out_shape=out_shape,
        scratch_shapes=[pltpu.VMEM((tm, D), F32), pltpu.VMEM((tm, D), F32), pltpu.SemaphoreType.DMA((2,))],
        semantics=("arbitrary", "arbitrary"), operands=operands, aliases=aliases)


def _dmerged_bwd(dh1b, w_out, name):
    Tp, D = dh1b.shape
    tm = _tile(Tp, TILE["dmerged"])

    def body(g_ref, w_ref, o_ref):
        o_ref[...] = lax.dot_general(g_ref[...], w_ref[...], _NT, preferred_element_type=F32)

    row = pl.BlockSpec((tm, D), lambda i: (i, 0))
    return _call(
        body, name=name, grid=(Tp // tm,),
        in_specs=[row, pl.BlockSpec((D, D), lambda i: (0, 0))],
        out_specs=row, out_shape=_sds((Tp, D), F32),
        compiler_params=_cp("parallel"),
    )(dh1b, w_out)


def _pool_bwd(dmerged, proj, y_pool, d_pool, scale, pool_w, name, tasks=()):
    Tp, D = dmerged.shape
    G, Cg, _ = pool_w.shape
    tm = _tile(Tp, TILE["pool"])
    nt = Tp // tm

    def body(dm_ref, gp_ref, y_ref, d_ref, sc_ref, w_ref, dproj_hbm, dw_ref, dsc_ref, next_ref, out_buf, out_sems):
        t = pl.program_id(0)
        tile = nt - 1 - t
        slot = t % 2
        dv_ref, dgp_ref = out_buf.at[slot, 0], out_buf.at[slot, 1]
        tile_rows = pl.ds(pl.multiple_of(tile * tm, tm), tm)

        def out_copies(s):
            return [pltpu.make_async_copy(out_buf.at[s, k], dproj_hbm.at[tile_rows, pl.ds(piece * D, D)],
                                          out_sems.at[2 * s + k]) for k, piece in enumerate((0, 3))]

        @pl.when(t >= 2)
        def _():
            for cp in out_copies(slot):
                cp.wait()

        @pl.when(t == 0)
        def _():
            next_ref[...] = jnp.zeros_like(next_ref)
            dw_ref[...] = jnp.zeros_like(dw_ref)
            dsc_ref[...] = jnp.zeros_like(dsc_ref)

        rows = _row_ids(tile, tm)
        dm = dm_ref[...]
        y = y_ref[...]
        sc = sc_ref[...]
        sg = _sigmoid(gp_ref[...])
        dpo = dm * sg
        dgp_ref[...] = (dm * (y * sc) * sg * (1.0 - sg)).astype(BF16)
        dsc_ref[...] += jnp.sum(dpo * y, axis=0, keepdims=True)
        dyb = (dpo * sc).astype(BF16)
        for g, win in enumerate(POOL_WINDOWS):
            cols = slice(g * Cg, (g + 1) * Cg)
            dy = dyb[:, cols]
            dd = lax.dot_general(dy, w_ref[g], _NT, preferred_element_type=F32)
            dw_ref[g] += lax.dot_general(d_ref[:, cols], dy, _TN, preferred_element_type=F32)
            q = dd / jnp.minimum(rows + 1, win).astype(F32)
            s = jnp.concatenate([q, next_ref[:, cols]], axis=0)
            k = 1
            while k < win:
                s = s + pltpu.roll(s, s.shape[0] - k, axis=0)
                k *= 2
            dv_ref[:, cols] = (s[:tm] - dd).astype(BF16)
            next_ref[:, cols] = q[:MAX_WINDOW]
        for cp in out_copies(slot):
            cp.start()

        @pl.when(t == nt - 1)
        def _():
            for cp in out_copies(slot) + (out_copies(1 - slot) if nt > 1 else []):
                cp.wait()

    row = pl.BlockSpec((tm, D), lambda t: (nt - 1 - t, 0))
    return _hosted(
        body, tasks, name=name, grid=(nt,),
        in_specs=[row, pl.BlockSpec((tm, D), lambda t: (nt - 1 - t, 3)), row, row,
                  pl.BlockSpec((1, D), lambda t: (0, 0)), pl.BlockSpec((G, Cg, Cg), lambda t: (0, 0, 0))],
        out_specs=[pl.BlockSpec(memory_space=pl.ANY), pl.BlockSpec((G, Cg, Cg), lambda t: (0, 0, 0)),
                   pl.BlockSpec((1, D), lambda t: (0, 0))],
        out_shape=[_sds((Tp, proj.shape[1]), BF16), _sds((G, Cg, Cg), F32), _sds((1, D), F32)],
        scratch_shapes=[pltpu.VMEM((MAX_WINDOW, D), F32), pltpu.VMEM((2, 2, tm, D), BF16),
                        pltpu.SemaphoreType.DMA((4,))],
        semantics=("arbitrary",), operands=(dmerged, proj, y_pool, d_pool, scale, pool_w))


LRU_SMALL_ROWS = 8


def _lru_bwd(dmerged, proj, xc, r_gate, i_gate, a_gate, mult_gate, hs, lam, conv_w, wa, wx, dproj, name, tasks=()):
    Tp, D = dmerged.shape
    H, hd, _ = wa.shape
    tm = _tile(Tp, TILE["lru"])
    nt = Tp // tm
    nb = D // hd
    halo_blocks = tm // HALO

    def body(dm_ref, vl_ref, vg_ref, gl_ref, xc_ref, r_ref, i_ref, a_ref, mult_ref, hs_ref, hsp_ref, lam_ref, cw_ref,
             wa_ref, wx_ref, _, dproj_hbm, dwa_ref, dwx_ref, small_ref,
             mu_next_ref, dxc_next_ref, q_s, mu_s, out_buf, out_sems):
        h_id, t = pl.program_id(0), pl.program_id(1)
        tile = nt - 1 - t
        step = h_id * nt + t
        slot = step % 2
        dvl_ref, dvg_ref, dgl_ref = out_buf.at[slot, 0], out_buf.at[slot, 1], out_buf.at[slot, 2]
        tile_rows = pl.ds(pl.multiple_of(tile * tm, tm), tm)

        def out_copies(s):
            return [pltpu.make_async_copy(
                out_buf.at[s, k], dproj_hbm.at[tile_rows, pl.ds(pl.multiple_of((piece * nb + h_id) * hd, hd), hd)],
                out_sems.at[3 * s + k]) for k, piece in enumerate((1, 2, 4))]

        @pl.when(step >= 2)
        def _():
            for cp in out_copies(slot):
                cp.wait()

        @pl.when(t == 0)
        def _():
            mu_next_ref[...] = jnp.zeros_like(mu_next_ref)
            dxc_next_ref[...] = jnp.zeros_like(dxc_next_ref)
            dwa_ref[...] = jnp.zeros_like(dwa_ref)
            dwx_ref[...] = jnp.zeros_like(dwx_ref)
            small_ref[...] = jnp.zeros_like(small_ref)

        first = tile == 0
        dm = dm_ref[...]
        hs_t = hs_ref[...]
        xc_t = xc_ref[...]
        r = r_ref[...]
        i = i_ref[...]
        lam_v = lam_ref[...]
        sp = _softplus(-lam_v)
        a = a_ref[...]
        mult = mult_ref[...]

        sg = _sigmoid(gl_ref[...])
        ge, dge = _gelu_and_grad(vg_ref[...])
        dlo = dm * sg
        dgl_ref[...] = (dm * (hs_t * ge) * sg * (1.0 - sg)).astype(BF16)
        dvg_ref[...] = (dlo * hs_t * dge).astype(BF16)
        dhs = dlo * ge

        q_s[...] = a * dhs
        mu_first = _scan_chunks(a_ref, q_s, mu_s, mu_next_ref[0:1, :], tm, reverse=True)
        lam_t = dhs + _shift_up(mu_s[...], mu_next_ref[...], 1)
        mu_next_ref[...] = jnp.broadcast_to(mu_first, mu_next_ref.shape)

        h_prev = _shift_down(jnp.where(first, 0.0, hsp_ref[...]), hs_t, 1)
        da = lam_t * h_prev
        dmult = lam_t * (i * xc_t)
        di = lam_t * mult * xc_t
        dxc = lam_t * mult * i
        dlog_a = da * a - dmult * (a * a) / mult
        dr = dlog_a * (-LRU_C * sp)
        dlam_rows = dlog_a * (-LRU_C * r)
        dza = dr * r * (1.0 - r)
        dzx = di * i * (1.0 - i)
        dzab, dzxb = dza.astype(BF16), dzx.astype(BF16)
        xcb = xc_t.astype(BF16)
        dxc = dxc + lax.dot_general(dzab, wa_ref[...], _NT, preferred_element_type=F32)
        dxc = dxc + lax.dot_general(dzxb, wx_ref[...], _NT, preferred_element_type=F32)
        dwa_ref[...] += lax.dot_general(xcb, dzab, _TN, preferred_element_type=F32)
        dwx_ref[...] += lax.dot_general(xcb, dzxb, _TN, preferred_element_type=F32)

        dxc_next = dxc_next_ref[...]
        taps = [_shift_up(dxc, dxc_next, CONV_WIDTH - 1 - k) for k in range(CONV_WIDTH)]
        dv = jnp.zeros_like(dxc)
        for k in range(CONV_WIDTH):
            dv = dv + cw_ref[k:k + 1, :] * taps[k]
        dvl_ref[...] = dv.astype(BF16)
        dxc_next_ref[...] = dxc[:HALO, :]

        v_t = vl_ref[...]
        small = [jnp.sum(dza, axis=0, keepdims=True), jnp.sum(dzx, axis=0, keepdims=True),
                 jnp.sum(dlam_rows, axis=0, keepdims=True) * (-_sigmoid(-lam_v)),
                 jnp.sum(dxc, axis=0, keepdims=True)]
        for k in range(CONV_WIDTH):
            small.append(jnp.sum(taps[k] * v_t, axis=0, keepdims=True))
        for k, row in enumerate(small):
            small_ref[k:k + 1, :] += row
        for cp in out_copies(slot):
            cp.start()

        @pl.when(step == H * nt - 1)
        def _():
            for cp in out_copies(slot) + (out_copies(1 - slot) if H * nt > 1 else []):
                cp.wait()

    def piece(p):
        return pl.BlockSpec((tm, hd), lambda h, t: (nt - 1 - t, p * nb + h))

    def halo(p):
        return pl.BlockSpec((HALO, hd), lambda h, t: (jnp.maximum((nt - 1 - t) * halo_blocks - 1, 0), p * nb + h))

    blk = pl.BlockSpec((tm, hd), lambda h, t: (nt - 1 - t, h))
    vec = pl.BlockSpec((1, hd), lambda h, t: (0, h))
    mat = pl.BlockSpec((None, hd, hd), lambda h, t: (h, 0, 0))
    return _hosted(
        body, tasks, name=name, grid=(H, nt),
        in_specs=[blk, piece(1), piece(2), piece(4), blk, blk, blk, blk, blk, blk, halo(0), vec,
                  pl.BlockSpec((CONV_WIDTH, hd), lambda h, t: (0, h)), mat, mat, pl.BlockSpec(memory_space=pl.ANY)],
        out_specs=[pl.BlockSpec(memory_space=pl.ANY), mat, mat,
                   pl.BlockSpec((None, LRU_SMALL_ROWS, hd), lambda h, t: (h, 0, 0))],
        out_shape=[_sds(dproj.shape, BF16)] + [_sds((H, hd, hd), F32)] * 2 + [_sds((H, LRU_SMALL_ROWS, hd), F32)],
        scratch_shapes=[pltpu.VMEM((HALO, hd), F32), pltpu.VMEM((HALO, hd), F32),
                        pltpu.VMEM((tm, hd), F32), pltpu.VMEM((tm, hd), F32),
                        pltpu.VMEM((2, 3, tm, hd), BF16), pltpu.SemaphoreType.DMA((6,))],
        semantics=("arbitrary", "arbitrary"), aliases={15: 0},
        operands=(dmerged, proj, proj, proj, xc, r_gate, i_gate, a_gate, mult_gate, hs, hs, lam, conv_w, wa, wx,
                  dproj))


def _adamw(w, g, m, v):
    m = ADAM_B1 * m + (1.0 - ADAM_B1) * g
    v = ADAM_B2 * v + (1.0 - ADAM_B2) * (g * g)
    m_hat = m / (1.0 - ADAM_B1 ** ADAM_STEP)
    v_hat = v / (1.0 - ADAM_B2 ** ADAM_STEP)
    delta = -ADAM_LR * (m_hat / (jnp.sqrt(v_hat) + ADAM_EPS) + ADAM_WD * w)
    return delta, m, v


def _reduce_update(pair_sums, chip_sums, w, m, v, chip_slot, name, part=(0, 1), earlier=None):
    R, C = pair_sums.shape[1:]
    p, parts = part
    tr = _tile(R, TILE["update"])
    nblk = R // tr

    def body(slot_ref, own_ref, got_ref, w_ref, m_ref, v_ref, *rest):
        g_out, d_out, m_out, v_out = rest[-4:]
        g = own_ref[...].astype(F32)
        for k in range(3):
            g = g + got_ref[k].astype(F32)
        d, m_new, v_new = _adamw(w_ref[...], g, m_ref[...], v_ref[...])
        g_out[...] = g
        d_out[...] = d
        m_out[...] = m_new
        v_out[...] = v_new

    blk = pl.BlockSpec((tr, C), lambda i, s: (p * nblk + i, 0))
    in_specs = [pl.BlockSpec((None, tr, C), lambda i, s: (s[0], i, 0)),
                pl.BlockSpec((3, tr, C), lambda i, s: (0, i, 0)), blk, blk, blk]
    operands = (chip_slot, pair_sums, chip_sums, w, m, v)
    aliases = {}
    if earlier is not None:
        in_specs += [pl.BlockSpec(memory_space=pl.ANY)] * 4
        operands += tuple(earlier)
        aliases = {6 + k: k for k in range(4)}
    return _call(
        body, name=name,
        grid_spec=pltpu.PrefetchScalarGridSpec(
            num_scalar_prefetch=1, grid=(R // tr,), in_specs=in_specs, out_specs=[blk] * 4),
        out_shape=[_sds((parts * R, C), F32)] * 4,
        input_output_aliases=aliases,
        compiler_params=_cp("parallel"),
    )(*operands)


def _small_update(w, g, m, v, name):
    def body(w_ref, g_ref, m_ref, v_ref, d_out, m_out, v_out):
        d, m_new, v_new = _adamw(w_ref[...], g_ref[...], m_ref[...], v_ref[...])
        d_out[...] = d
        m_out[...] = m_new
        v_out[...] = v_new

    return _call(body, name=name, out_shape=[_sds(w.shape, F32)] * 3)(w, g, m, v)


def _slots_from_rows(full, lead):
    L, R, C = full.shape
    r = R // N_DEV
    return full.reshape(L, N_DEV, r, C).transpose(1, 0, 2, 3).reshape(N_DEV, L * r, C)


def _rows_from_slots(slots, lead):
    _, LR, C = slots.shape
    r = LR // lead
    return slots.reshape(N_DEV, lead, r, C).transpose(1, 0, 2, 3).reshape(lead, N_DEV * r, C)


def kernel(x, meta_tokens, norm1_g, w_in, pool_w, pool_scale, conv_w, conv_b, gate_a_w, gate_a_b, gate_x_w, gate_x_b, lru_lambda, w_out, norm2_g, mlp_w1, mlp_w2, final_g, loss_target, m_meta_tokens, m_norm1_g, m_w_in, m_pool_w, m_pool_scale, m_conv_w, m_conv_b, m_gate_a_w, m_gate_a_b, m_gate_x_w, m_gate_x_b, m_lru_lambda, m_w_out, m_norm2_g, m_mlp_w1, m_mlp_w2, m_final_g, v_meta_tokens, v_norm1_g, v_w_in, v_pool_w, v_pool_scale, v_conv_w, v_conv_b, v_gate_a_w, v_gate_a_b, v_gate_x_w, v_gate_x_b, v_lru_lambda, v_w_out, v_norm2_g, v_mlp_w1, v_mlp_w2, v_final_g):
    seq, D = x.shape[1], x.shape[2]
    n_meta = meta_tokens.shape[0]
    G, Cg = pool_w.shape[1], pool_w.shape[3]
    H, hd = gate_a_w.shape[1], gate_a_w.shape[3]
    T = n_meta + seq
    Tp = -(-T // ROW_ALIGN) * ROW_ALIGN
    ix, iy, ic = _pos()
    me = 4 * ix + 2 * iy + ic
    core = jnp.reshape(ic, (1,)).astype(jnp.int32)
    chip_slot = jnp.reshape(2 * ix + iy, (1,)).astype(jnp.int32)

    w_in_l, w1_l, w2_l, w_out_l = w_in[0], mlp_w1[0], mlp_w2[0], w_out[0]
    pool_l = pool_w[0].reshape(G * (Cg // N_DEV), Cg)
    wa_l = gate_a_w[0].reshape(H * (hd // N_DEV), hd)
    wx_l = gate_x_w[0].reshape(H * (hd // N_DEV), hd)
    small_params = jnp.concatenate(
        [meta_tokens, conv_w[0], jnp.zeros((4, D // N_DEV), F32)], axis=0)
    biases = jnp.concatenate([gate_a_b[0], gate_x_b[0]], axis=0)
    (small_g, bias_g) = _all_gather_relay([small_params, biases], [1, 1], "gather_small")
    small_full = small_g.transpose(1, 0, 2).reshape(n_meta + 8, D)
    meta_full = small_full[:n_meta]
    conv_full = small_full[n_meta:n_meta + CONV_WIDTH]
    bias_full = bias_g.transpose(1, 0, 2).reshape(2 * H, hd)
    ba_full, bx_full = bias_full[:H], bias_full[H:]

    h0 = jnp.concatenate([meta_full, x[0], jnp.zeros((Tp - T, D), F32)], axis=0)
    target = jnp.concatenate([jnp.zeros((n_meta, D), F32), loss_target[0], jnp.zeros((Tp - T, D), F32)], axis=0)
    u, r1 = _norm_fwd(h0, norm1_g, "norm1")
    order = jnp.stack([4 * px + 2 * py + pc for (px, py, pc) in (
        (ix, iy, ic), (ix, iy, 1 - ic), (1 - ix, iy, ic), (ix, 1 - iy, ic), (1 - ix, iy, 1 - ic),
        (ix, 1 - iy, 1 - ic), (1 - ix, 1 - iy, ic), (1 - ix, 1 - iy, 1 - ic))]).astype(jnp.int32)
    proj, w_in_g, (pool_g, wa_g, wx_g), ((w_out_g,),) = _proj_gather(
        u, w_in_l.astype(BF16), [pool_l.astype(BF16), wa_l.astype(BF16), wx_l.astype(BF16)], order, "proj",
        tasks=[_AgFull(w_out_l.astype(BF16))])
    pool_full = _rows_from_slots(pool_g, G)
    wa_full = _rows_from_slots(wa_g, H)
    wx_full = _rows_from_slots(wx_g, H)
    w_in_parts = [w_in_g]
    d_pool, y_pool = _pool_fwd(proj, pool_full, "pool_fwd")
    (xc, r_gate, i_gate, a_gate, mult_gate, hs, merged), ((w1_ici,),) = _lru_fwd(
        proj, y_pool, pool_scale, conv_full, conv_b, wa_full, ba_full, wx_full, bx_full, lru_lambda, "lru_fwd",
        tasks=[_AgFull(w1_l.astype(BF16), forward_here=False)])
    w_out_full = w_out_g.reshape(D, D)
    (h1, u2, r2), ((w1_g,),) = _wout_norm_fwd(merged, w_out_full, h0, norm2_g, "wout_norm2", tasks=[_AgForward(w1_ici)])
    (act, a1), ((w2_g,),) = _mlp1_fwd(u2, w1_g, "mlp1", tasks=[_AgFull(w2_l.astype(BF16))])
    dh2, dh2b, loss_tile, d_final_g = _mlp2_loss(
        act, w2_g.reshape(-1, D), h1, target, final_g.reshape(1, D), n_meta, seq, "mlp2_loss")

    def pair(part, got, tag):
        return _pair_sum(part, got, core, "pair_sum_" + tag)

    d_a1 = _dact_bwd(dh2b, w2_g, a1, "dact")
    dw2_p, _ = _weight_grad(act, dh2b, N_DEV, True, "dw2")
    dw1_p, ((dw2_got,),) = _weight_grad(u2, d_a1, N_DEV, False, "dw1", tasks=[_RsSibling(dw2_p)])
    dw2_pair = pair(dw2_p, dw2_got, "w2")
    (dh1, dh1b, d_norm2_g), ((dw2_chips,), (dw1_got,)) = _nt_norm_bwd(
        d_a1, [w1_g], dh2, h1, r2, norm2_g, True, "du2_norm2", tasks=[_RsChips(dw2_pair), _RsSibling(dw1_p)])
    dw1_pair = pair(dw1_p, dw1_got, "w1")
    dmerged = _dmerged_bwd(dh1b, w_out_full, "dmerged")
    dwout_p, _ = _weight_grad(merged, dh1b, 2, True, "dwout")
    dwout_p = dwout_p.reshape(N_DEV, D // N_DEV, D)
    (dproj_pool, dpool_full, d_scale), ((dwout_got,),) = _pool_bwd(
        dmerged, proj, y_pool, d_pool, pool_scale, pool_full, "pool_bwd", tasks=[_RsSibling(dwout_p)])
    dwout_pair = pair(dwout_p, dwout_got, "wout")
    (dproj, dwa_full, dwx_full, lru_small), ((dw1_chips,), (dwout_chips,)) = _lru_bwd(
        dmerged, proj, xc, r_gate, i_gate, a_gate, mult_gate, hs, lru_lambda, conv_full, wa_full, wx_full, dproj_pool,
        "lru_bwd", tasks=[_RsChips(dw1_pair), _RsChips(dwout_pair)])
    dwin_a, _ = _weight_grad(u, dproj, N_DEV, False, "dwin_a", part=(0, 2))
    dwin_b, ((dwin_a_got,),) = _weight_grad(u, dproj, N_DEV, False, "dwin_b", part=(1, 2), tasks=[_RsSibling(dwin_a)])
    dwin_a_pair = pair(dwin_a, dwin_a_got, "win_a")
    dpool_p = _slots_from_rows(dpool_full, G).astype(BF16)
    dwa_p = _slots_from_rows(dwa_full, H).astype(BF16)
    dwx_p = _slots_from_rows(dwx_full, H).astype(BF16)
    late = [dwin_b, dpool_p, dwa_p, dwx_p]
    n_tiles = Tp // _tile(Tp, TILE["nt"])
    n_first = max(1, n_tiles // 2)
    (dh0_a, d_norm1_a), ((dwin_a_chips,), *late_got) = _nt_norm_bwd(
        dproj, w_in_parts, dh1, h0, r1, norm1_g, False, "du_norm1_a",
        tasks=[_RsChips(dwin_a_pair)] + [_RsSibling(p) for p in late], tiles=(0, n_first))
    late_pair = [pair(p, g[0], "late%d" % k) for k, (p, g) in enumerate(zip(late, late_got))]
    (dh0, d_norm1_g), late_chips = _nt_norm_bwd(
        dproj, w_in_parts, dh1, h0, r1, norm1_g, False, "du_norm1_b", tasks=[_RsChips(p) for p in late_pair],
        tiles=(n_first, n_tiles - n_first), earlier=(dh0_a, d_norm1_a))
    grad_x = dh0[n_meta:T][None]

    pair_sums = [dw2_pair, dw1_pair, dwout_pair] + late_pair[1:]
    chip_sums = [dw2_chips, dw1_chips, dwout_chips] + [c[0] for c in late_chips[1:]]
    big = {}
    names = ["mlp_w2", "mlp_w1", "w_out", "pool_w", "gate_a_w", "gate_x_w"]
    trip = {"mlp_w2": (mlp_w2, m_mlp_w2, v_mlp_w2), "mlp_w1": (mlp_w1, m_mlp_w1, v_mlp_w1),
            "w_out": (w_out, m_w_out, v_w_out),
            "pool_w": (pool_w, m_pool_w, v_pool_w), "gate_a_w": (gate_a_w, m_gate_a_w, v_gate_a_w),
            "gate_x_w": (gate_x_w, m_gate_x_w, v_gate_x_w)}
    for k, nm in enumerate(names):
        w_, m_, v_ = trip[nm]
        shape2 = pair_sums[k].shape[1:]
        outs = _reduce_update(pair_sums[k], chip_sums[k], w_.reshape(shape2), m_.reshape(shape2), v_.reshape(shape2),
                              chip_slot, "update_" + nm)
        big[nm] = [o.reshape(w_.shape) for o in outs]
    win2 = [a_[0] for a_ in (w_in, m_w_in, v_w_in)]
    win_a = _reduce_update(dwin_a_pair, dwin_a_chips, *win2, chip_slot, "update_w_in_a", part=(0, 2))
    win_b = _reduce_update(late_pair[0], late_chips[0][0], *win2, chip_slot, "update_w_in_b", part=(1, 2), earlier=win_a)
    big["w_in"] = [o.reshape(w_in.shape) for o in win_b]
    names = names + ["w_in"]

    lru_rows = lru_small.transpose(1, 0, 2).reshape(LRU_SMALL_ROWS, D)
    small_part = jnp.concatenate(
        [dh0[:n_meta], d_norm1_g, d_scale, d_norm2_g, d_final_g, lru_rows, jnp.zeros((4, D), F32)], axis=0)
    (small_all,) = _all_gather([small_part], "gather_small_grads")
    small_sum = _sum_slots(small_all, "sum_small_grads")
    o = n_meta
    g_meta_full = small_sum[:o]
    g_norm1, g_scale, g_norm2, g_final = (small_sum[o + k:o + k + 1] for k in range(4))
    g_ba_full, g_bx_full, g_lam, g_cb = (small_sum[o + 4 + k:o + 5 + k] for k in range(4))
    g_cw_full = small_sum[o + 8:o + 8 + CONV_WIDTH]
    dcol = D // N_DEV
    g_meta = lax.dynamic_slice_in_dim(g_meta_full, me * dcol, dcol, axis=1)
    g_cw = lax.dynamic_slice_in_dim(g_cw_full, me * dcol, dcol, axis=1)
    hcol = hd // N_DEV
    g_ba = lax.dynamic_slice_in_dim(g_ba_full.reshape(H, hd), me * hcol, hcol, axis=1)
    g_bx = lax.dynamic_slice_in_dim(g_bx_full.reshape(H, hd), me * hcol, hcol, axis=1)

    rep_w = jnp.concatenate([norm1_g, pool_scale, conv_b, lru_lambda, norm2_g, final_g.reshape(1, D)], axis=0)
    rep_g = jnp.concatenate([g_norm1, g_scale, g_cb, g_lam, g_norm2, g_final], axis=0)
    rep_m = jnp.concatenate([m_norm1_g, m_pool_scale, m_conv_b, m_lru_lambda, m_norm2_g, m_final_g.reshape(1, D)], axis=0)
    rep_v = jnp.concatenate([v_norm1_g, v_pool_scale, v_conv_b, v_lru_lambda, v_norm2_g, v_final_g.reshape(1, D)], axis=0)
    rep_d, rep_nm, rep_nv = _small_update(rep_w, rep_g, rep_m, rep_v, "update_vectors")
    col_w = jnp.concatenate([meta_tokens, conv_w[0]], axis=0)
    col_g = jnp.concatenate([g_meta, g_cw], axis=0)
    col_m = jnp.concatenate([m_meta_tokens, m_conv_w[0]], axis=0)
    col_v = jnp.concatenate([v_meta_tokens, v_conv_w[0]], axis=0)
    col_d, col_nm, col_nv = _small_update(col_w, col_g, col_m, col_v, "update_columns")
    b_w = jnp.concatenate([gate_a_b[0], gate_x_b[0]], axis=0)
    b_g = jnp.concatenate([g_ba, g_bx], axis=0)
    b_m = jnp.concatenate([m_gate_a_b[0], m_gate_x_b[0]], axis=0)
    b_v = jnp.concatenate([v_gate_a_b[0], v_gate_x_b[0]], axis=0)
    b_d, b_nm, b_nv = _small_update(b_w, b_g, b_m, b_v, "update_biases")

    def rep(arr, k, like):
        return arr[k:k + 1].reshape(like.shape)

    rep_order = {"norm1_g": 0, "pool_scale": 1, "conv_b": 2, "lru_lambda": 3, "norm2_g": 4, "final_g": 5}
    like = {"norm1_g": norm1_g, "pool_scale": pool_scale, "conv_b": conv_b, "lru_lambda": lru_lambda,
            "norm2_g": norm2_g, "final_g": final_g}

    def leaves(kind):
        rep_src = [rep_g, rep_d, rep_nm, rep_nv][kind]
        col_src = [col_g, col_d, col_nm, col_nv][kind]
        b_src = [b_g, b_d, b_nm, b_nv][kind]
        out = {}
        out["meta_tokens"] = col_src[:n_meta]
        out["conv_w"] = col_src[n_meta:][None]
        out["gate_a_b"] = b_src[:H][None]
        out["gate_x_b"] = b_src[H:][None]
        for nm, k in rep_order.items():
            out[nm] = rep(rep_src, k, like[nm])
        for nm in names:
            out[nm] = big[nm][kind]
        order = ["meta_tokens", "norm1_g", "w_in", "pool_w", "pool_scale", "conv_w", "conv_b", "gate_a_w", "gate_a_b",
                 "gate_x_w", "gate_x_b", "lru_lambda", "w_out", "norm2_g", "mlp_w1", "mlp_w2", "final_g"]
        return [out[nm] for nm in order]

    loss = lax.psum(loss_tile[0, 0], ("x", "y", "c"))
    return (loss, grad_x, *leaves(0), *leaves(1), *leaves(2), *leaves(3))
```

```python
import functools

import jax
import jax.numpy as jnp
from jax import lax
from jax.experimental import pallas as pl
from jax.experimental.pallas import tpu as pltpu

F32 = jnp.float32
BF16 = jnp.bfloat16
MESH = pl.DeviceIdType.MESH
N_DEV = 8
POOL_WINDOWS = (2, 4, 8, 16)
MAX_WINDOW = 16
CONV_WIDTH = 4
HALO = 8
LRU_C = 8.0
NORM_EPS = 1e-6
ADAM_LR, ADAM_B1, ADAM_B2, ADAM_EPS, ADAM_WD, ADAM_STEP = 0.001, 0.9, 0.999, 1e-08, 0.01, 10
ROW_ALIGN = 128
VMEM_LIMIT = 56 << 20
TILE = dict(norm=384, proj=1408, pool=384, lru=704, wout=384, mlp1=1408, mlp2=704, dact=1408, tn=1408,
            nt=704, dmerged=384, update=256, pair=1024)
MLP2_K = 1024
EPILOGUE_ROWS = 176
W_IN_CHUNKS = 5
MID_STEP_PERCENT = 88

_NT = (((1,), (1,)), ((), ()))
_TN = (((0,), (0,)), ((), ()))


def _call(body, **kw):
    return pl.pallas_call(body, **kw)


def _cp(*sem):
    return pltpu.CompilerParams(dimension_semantics=sem, vmem_limit_bytes=VMEM_LIMIT)


def _tile(total, pref):
    best = None
    for t in range(16, min(total, pref) + 1, 16):
        if total % t == 0:
            best = t
    assert best is not None, (total, pref)
    return best


def _sds(shape, dtype):
    return jax.ShapeDtypeStruct(shape, dtype)


def _pos():
    return lax.axis_index("x"), lax.axis_index("y"), lax.axis_index("c")


def _all_gather(shards, name):
    n = len(shards)

    def body(*refs):
        ins, outs = refs[:n], refs[n:2 * n]
        send_sems, recv_sems, local_sems = refs[2 * n:]
        x, y, c = _pos()
        me, sib = (x, y, c), (x, y, 1 - c)
        chips = [(1 - x, y), (x, 1 - y), (1 - x, 1 - y)]

        def slot(p):
            return 4 * p[0] + 2 * p[1] + p[2]

        def copy(a, k, block, to, src=None):
            dst = outs[a].at[slot(block)]
            return pltpu.make_async_remote_copy(
                src_ref=dst if src is None else src, dst_ref=dst,
                send_sem=send_sems.at[7 * a + k], recv_sem=recv_sems.at[7 * a + k],
                device_id=to, device_id_type=MESH)

        mine = [pltpu.make_async_copy(ins[a], outs[a].at[slot(me)], local_sems.at[a]) for a in range(n)]
        for m in mine:
            m.start()
        first = []
        for a in range(n):
            first.append(copy(a, 0, me, sib, src=ins[a]))
            first += [copy(a, 1 + j, me, (*chip, c), src=ins[a]) for j, chip in enumerate(chips)]
        for cp in first:
            cp.start()
        passed = []
        for a in range(n):
            for j, chip in enumerate(chips):
                copy(a, 1 + j, (*chip, c), me).wait_recv()
                fwd = copy(a, 4 + j, (*chip, c), sib)
                fwd.start()
                passed.append(fwd)
        for a in range(n):
            copy(a, 0, sib, me).wait_recv()
            for j, chip in enumerate(chips):
                copy(a, 4 + j, (*chip, 1 - c), me).wait_recv()
        for cp in first + passed:
            cp.wait_send()
        for m in mine:
            m.wait()

    hbm = pl.BlockSpec(memory_space=pl.ANY)
    return _call(
        body, name=name,
        out_shape=[_sds((N_DEV,) + s.shape, s.dtype) for s in shards],
        in_specs=[hbm] * n, out_specs=[hbm] * n,
        scratch_shapes=[pltpu.SemaphoreType.DMA((7 * n,)), pltpu.SemaphoreType.DMA((7 * n,)),
                        pltpu.SemaphoreType.DMA((n,))],
    )(*shards)


def _all_gather_relay(shards, chunks, name):
    n = len(shards)
    units = []
    for a, s in enumerate(shards):
        if chunks[a] == 1:
            units.append((a, None, None))
        else:
            w = s.shape[-1] // chunks[a]
            units += [(a, q * w, w) for q in range(chunks[a])]
    nu = len(units)

    def body(*refs):
        ins, outs = refs[:n], refs[n:2 * n]
        send_sems, recv_sems, local_sems = refs[2 * n:]
        x, y, c = _pos()
        me, sib = (x, y, c), (x, y, 1 - c)
        x_nbr, y_nbr, diag = (1 - x, y, c), (x, 1 - y, c), (1 - x, 1 - y, c)
        came_from = (x + (1 - c) * (1 - 2 * x), y + c * (1 - 2 * y), c)
        pass_to = (x + c * (1 - 2 * x), y + (1 - c) * (1 - 2 * y), c)

        def slot(p):
            return 4 * p[0] + 2 * p[1] + p[2]

        def src_view(u):
            a, c0, w = units[u]
            return ins[a] if c0 is None else ins[a].at[:, pl.ds(c0, w)]

        def dst_view(u, p):
            a, c0, w = units[u]
            return outs[a].at[slot(p)] if c0 is None else outs[a].at[slot(p), :, pl.ds(c0, w)]

        def copy(u, k, block, to, from_shard=False):
            dst = dst_view(u, block)
            return pltpu.make_async_remote_copy(
                src_ref=src_view(u) if from_shard else dst, dst_ref=dst,
                send_sem=send_sems.at[7 * u + k], recv_sem=recv_sems.at[7 * u + k],
                device_id=to, device_id_type=MESH)

        mine = [pltpu.make_async_copy(src_view(u), dst_view(u, me), local_sems.at[u]) for u in range(nu)]
        for m in mine:
            m.start()
        sent = []
        for u in range(nu):
            sent += [copy(u, 0, me, sib, True), copy(u, 1, me, x_nbr, True), copy(u, 2, me, y_nbr, True)]
        for cp in sent:
            cp.start()
        for u in range(nu):
            copy(u, 1, x_nbr, me).wait_recv()
            copy(u, 2, y_nbr, me).wait_recv()
            sent += [copy(u, 3, came_from, pass_to), copy(u, 4, x_nbr, sib), copy(u, 5, y_nbr, sib)]
            for cp in sent[-3:]:
                cp.start()
        for u in range(nu):
            copy(u, 3, diag, me).wait_recv()
            sent.append(copy(u, 6, diag, sib))
            sent[-1].start()
        for u in range(nu):
            copy(u, 0, sib, me).wait_recv()
            for k, p in ((4, x_nbr), (5, y_nbr), (6, diag)):
                copy(u, k, (p[0], p[1], 1 - c), me).wait_recv()
        for cp in sent:
            cp.wait_send()
        for m in mine:
            m.wait()

    hbm = pl.BlockSpec(memory_space=pl.ANY)
    return _call(
        body, name=name,
        out_shape=[_sds((N_DEV,) + s.shape, s.dtype) for s in shards],
        in_specs=[hbm] * n, out_specs=[hbm] * n,
        scratch_shapes=[pltpu.SemaphoreType.DMA((7 * nu,)), pltpu.SemaphoreType.DMA((7 * nu,)),
                        pltpu.SemaphoreType.DMA((nu,))],
    )(*shards)


def _other_chips(x, y):
    return [(1 - x, y), (x, 1 - y), (1 - x, 1 - y)]


class _AgFull:
    n_sem, n_local = 7, 1

    def __init__(self, shard, forward_here=True):
        self.forward_here = forward_here
        self.ins = [shard]
        self.out_shapes = [_sds((N_DEV,) + shard.shape, shard.dtype)]
        self.aliases = []

    def _peers(self):
        x, y, c = _pos()
        return [(x, y, 1 - c)] + [(*chip, c) for chip in _other_chips(x, y)]

    def _sends(self, ins, outs, sems):
        send, recv, _, base, _ = sems
        x, y, c = _pos()
        mine = outs[0].at[4 * x + 2 * y + c]
        return [pltpu.make_async_remote_copy(src_ref=ins[0], dst_ref=mine, send_sem=send.at[base + k],
                                             recv_sem=recv.at[base + k], device_id=p, device_id_type=MESH)
                for k, p in enumerate(self._peers())]

    def _arrivals(self, outs, sems):
        send, recv, _, base, _ = sems
        res = []
        for k, p in enumerate(self._peers()):
            blk = outs[0].at[4 * p[0] + 2 * p[1] + p[2]]
            res.append(pltpu.make_async_remote_copy(src_ref=blk, dst_ref=blk, send_sem=send.at[base + k],
                                                    recv_sem=recv.at[base + k], device_id=p, device_id_type=MESH))
        return res

    def _own(self, ins, outs, sems):
        x, y, c = _pos()
        return pltpu.make_async_copy(ins[0], outs[0].at[4 * x + 2 * y + c], sems[2].at[sems[4]])

    def _forwards(self, outs, sems, core_of_block):
        return _forward_copies(outs[0], sems[0], sems[1], sems[3] + 4, core_of_block)

    def start(self, ins, outs, sems):
        self._own(ins, outs, sems).start()
        for cp in self._sends(ins, outs, sems):
            cp.start()

    def mid(self, ins, outs, sems):
        if self.forward_here:
            for cp in self._arrivals(outs, sems)[1:]:
                cp.wait_recv()
            for cp in self._forwards(outs, sems, "mine"):
                cp.start()

    def finish(self, ins, outs, sems):
        if self.forward_here:
            self._arrivals(outs, sems)[0].wait_recv()
            for cp in self._forwards(outs, sems, "sibling"):
                cp.wait_recv()
            for cp in self._sends(ins, outs, sems) + self._forwards(outs, sems, "mine"):
                cp.wait_send()
        else:
            for cp in self._arrivals(outs, sems):
                cp.wait_recv()
            for cp in self._sends(ins, outs, sems):
                cp.wait_send()
        self._own(ins, outs, sems).wait()


def _forward_copies(gathered_ref, send, recv, base, core_of_block):
    x, y, c = _pos()
    res = []
    for k, chip in enumerate(_other_chips(x, y)):
        blk = gathered_ref.at[4 * chip[0] + 2 * chip[1] + (c if core_of_block == "mine" else 1 - c)]
        res.append(pltpu.make_async_remote_copy(src_ref=blk, dst_ref=blk, send_sem=send.at[base + k],
                                                recv_sem=recv.at[base + k], device_id=(x, y, 1 - c),
                                                device_id_type=MESH))
    return res


class _AgForward:
    n_sem, n_local = 3, 0

    def __init__(self, gathered):
        self.ins = [gathered]
        self.out_shapes = [_sds(gathered.shape, gathered.dtype)]
        self.aliases = [(0, 0)]

    def start(self, ins, outs, sems):
        for cp in _forward_copies(outs[0], sems[0], sems[1], sems[3], "mine"):
            cp.start()

    def finish(self, ins, outs, sems):
        for cp in _forward_copies(outs[0], sems[0], sems[1], sems[3], "sibling"):
            cp.wait_recv()
        for cp in _forward_copies(outs[0], sems[0], sems[1], sems[3], "mine"):
            cp.wait_send()


class _RsSibling:
    n_sem, n_local = 4, 0

    def __init__(self, part):
        self.ins = [part]
        self.out_shapes = [_sds((4,) + part.shape[1:], part.dtype)]
        self.aliases = []

    def _copies(self, ins, outs, sems):
        send, recv, _, base, _ = sems
        x, y, c = _pos()
        return [pltpu.make_async_remote_copy(src_ref=ins[0].at[2 * q + (1 - c)], dst_ref=outs[0].at[q],
                                             send_sem=send.at[base + q], recv_sem=recv.at[base + q],
                                             device_id=(x, y, 1 - c), device_id_type=MESH) for q in range(4)]

    def start(self, ins, outs, sems):
        for cp in self._copies(ins, outs, sems):
            cp.start()

    def finish(self, ins, outs, sems):
        for cp in self._copies(ins, outs, sems):
            cp.wait()


class _RsChips:
    n_sem, n_local = 3, 0

    def __init__(self, pair):
        self.ins = [pair]
        self.out_shapes = [_sds((3,) + pair.shape[1:], pair.dtype)]
        self.aliases = []

    def _copies(self, ins, outs, sems):
        send, recv, _, base, _ = sems
        x, y, c = _pos()
        return [pltpu.make_async_remote_copy(src_ref=ins[0].at[2 * chip[0] + chip[1]], dst_ref=outs[0].at[k],
                                             send_sem=send.at[base + k], recv_sem=recv.at[base + k],
                                             device_id=(*chip, c), device_id_type=MESH)
                for k, chip in enumerate(_other_chips(x, y))]

    def start(self, ins, outs, sems):
        for cp in self._copies(ins, outs, sems):
            cp.start()

    def finish(self, ins, outs, sems):
        for cp in self._copies(ins, outs, sems):
            cp.wait()


def _hosted(body, tasks, *, grid, in_specs, out_specs, out_shape, scratch_shapes=(), name, semantics, operands,
            aliases=None):
    in_specs, out_specs, out_shape = list(in_specs), list(out_specs), list(out_shape)
    scratch_shapes = list(scratch_shapes)
    aliases = dict(aliases or {})
    if not tasks:
        res = _call(body, name=name, grid=grid, in_specs=in_specs, out_specs=out_specs, out_shape=out_shape,
                    scratch_shapes=scratch_shapes, input_output_aliases=aliases,
                    compiler_params=_cp(*semantics))(*operands)
        return list(res), []
    n_in, n_out, n_scr = len(in_specs), len(out_specs), len(scratch_shapes)
    t_ins = [a for t in tasks for a in t.ins]
    t_outs = [o for t in tasks for o in t.out_shapes]
    i0, o0 = n_in, n_out
    for t in tasks:
        for (i, o) in t.aliases:
            aliases[i0 + i] = o0 + o
        i0 += len(t.ins)
        o0 += len(t.out_shapes)
    n_sem = sum(t.n_sem for t in tasks)
    n_local = max(1, sum(t.n_local for t in tasks))
    n_steps = 1
    for g in grid:
        n_steps *= g
    mid_step = min(n_steps - 1, (n_steps * MID_STEP_PERCENT) // 100)

    def wrapped(*refs):
        cut = [n_in, len(t_ins), n_out, len(t_outs), n_scr]
        parts, at = [], 0
        for n in cut:
            parts.append(refs[at:at + n])
            at += n
        ins, tin, outs, tout, scratch = parts
        send, recv, local = refs[at:]
        step = pl.program_id(0)
        for d in range(1, len(grid)):
            step = step * grid[d] + pl.program_id(d)

        def each(method):
            i, o, s, l = 0, 0, 0, 0
            for t in tasks:
                if hasattr(t, method):
                    getattr(t, method)(tin[i:i + len(t.ins)], tout[o:o + len(t.out_shapes)], (send, recv, local, s, l))
                i, o, s, l = i + len(t.ins), o + len(t.out_shapes), s + t.n_sem, l + t.n_local

        @pl.when(step == 0)
        def _():
            each("start")

        body(*ins, *outs, *scratch)

        @pl.when(step == mid_step)
        def _():
            each("mid")

        @pl.when(step == n_steps - 1)
        def _():
            each("finish")

    hbm = pl.BlockSpec(memory_space=pl.ANY)
    res = _call(
        wrapped, name=name, grid=grid,
        in_specs=in_specs + [hbm] * len(t_ins), out_specs=out_specs + [hbm] * len(t_outs),
        out_shape=out_shape + t_outs,
        scratch_shapes=scratch_shapes + [pltpu.SemaphoreType.DMA((n_sem,)), pltpu.SemaphoreType.DMA((n_sem,)),
                                         pltpu.SemaphoreType.DMA((n_local,))],
        input_output_aliases=aliases,
        compiler_params=_cp(*(["arbitrary"] * len(grid))),
    )(*operands, *t_ins)
    res = list(res)
    task_outs, o = [], n_out
    for t in tasks:
        task_outs.append(res[o:o + len(t.out_shapes)])
        o += len(t.out_shapes)
    return res[:n_out], task_outs


def _pair_sum(part, got, core, name):
    _, R, C = part.shape
    tr = _tile(R, TILE["pair"]) if R % 16 == 0 else R

    def body(core_ref, p_ref, g_ref, o_ref):
        o_ref[...] = (p_ref[...].astype(F32) + g_ref[...].astype(F32)).astype(o_ref.dtype)

    return _call(
        body, name=name,
        grid_spec=pltpu.PrefetchScalarGridSpec(
            num_scalar_prefetch=1, grid=(4, R // tr),
            in_specs=[pl.BlockSpec((None, tr, C), lambda q, i, cr: (2 * q + cr[0], i, 0)),
                      pl.BlockSpec((None, tr, C), lambda q, i, cr: (q, i, 0))],
            out_specs=pl.BlockSpec((None, tr, C), lambda q, i, cr: (q, i, 0))),
        out_shape=_sds((4, R, C), part.dtype),
        compiler_params=_cp("parallel", "parallel"),
    )(core, part, got)


def _sum_slots(gathered, name):
    _, R, C = gathered.shape

    def body(g_ref, o_ref):
        acc = g_ref[0]
        for s in range(1, N_DEV):
            acc = acc + g_ref[s]
        o_ref[...] = acc

    return _call(body, name=name, out_shape=_sds((R, C), F32))(gathered)


def _sigmoid(z):
    return jax.nn.sigmoid(z)


def _softplus(z):
    e = jnp.exp(-jnp.abs(z))
    log1p_e = jnp.where(e < 0.01, e * (1.0 - e * (0.5 - e * (1.0 / 3.0))), jnp.log(1.0 + e))
    return jnp.maximum(z, 0.0) + log1p_e


_GELU_K = 0.7978845608028654
_GELU_C = 0.044715


def _gelu_and_grad(z):
    t = jnp.tanh(_GELU_K * (z + _GELU_C * z * z * z))
    g = 0.5 * z * (1.0 + t)
    dg = 0.5 * (1.0 + t) + 0.5 * z * (1.0 - t * t) * _GELU_K * (1.0 + 3.0 * _GELU_C * z * z)
    return g, dg


def _gelu(z):
    t = jnp.tanh(_GELU_K * (z + _GELU_C * z * z * z))
    return 0.5 * z * (1.0 + t)


def _row_ids(tile_index, tm, width=1):
    return tile_index * tm + lax.broadcasted_iota(jnp.int32, (tm, width), 0)


def _shift_down(prev, cur, k):
    if k == 0:
        return cur
    ext = jnp.concatenate([prev, cur], axis=0)
    return pltpu.roll(ext, k, axis=0)[prev.shape[0]:]


def _shift_up(cur, nxt, k):
    if k == 0:
        return cur
    ext = jnp.concatenate([cur, nxt], axis=0)
    return pltpu.roll(ext, ext.shape[0] - k, axis=0)[:cur.shape[0]]


def _lru_gates(r, sp):
    z = LRU_C * r * sp
    a = jnp.exp(-z)
    t = jnp.tanh(z)
    mult = jnp.sqrt(2.0 * t / (1.0 + t))
    return a, mult


def _scan_chunks(a_ref, b_ref, out_ref, carry, n_rows, reverse):
    n_chunks = n_rows // 8
    cols = a_ref.shape[1]
    rid = lax.broadcasted_iota(jnp.int32, (8, cols), 0)
    edge = 0 if reverse else 7

    def chunk(k, h):
        ci = (n_chunks - 1 - k) if reverse else k
        rows = pl.ds(pl.multiple_of(ci * 8, 8), 8)
        a = a_ref[rows, :]
        b = b_ref[rows, :]
        for s in (1, 2, 4):
            if reverse:
                keep = rid < 8 - s
                a_n, b_n = pltpu.roll(a, 8 - s, axis=0), pltpu.roll(b, 8 - s, axis=0)
            else:
                keep = rid >= s
                a_n, b_n = pltpu.roll(a, s, axis=0), pltpu.roll(b, s, axis=0)
            b = a * jnp.where(keep, b_n, 0.0) + b
            a = a * jnp.where(keep, a_n, 1.0)
        out_ref[rows, :] = a * h + b
        a_e = jnp.sum(jnp.where(rid == edge, a, 0.0), axis=0, keepdims=True)
        b_e = jnp.sum(jnp.where(rid == edge, b, 0.0), axis=0, keepdims=True)
        return a_e * h + b_e

    return lax.fori_loop(0, n_chunks, chunk, carry, unroll=4 if n_chunks % 4 == 0 else 1)


def _norm_fwd(h, g, name):
    Tp, D = h.shape
    tm = _tile(Tp, TILE["norm"])

    def body(h_ref, g_ref, u_ref, r_ref):
        x = h_ref[...]
        r = lax.rsqrt(jnp.mean(x * x, axis=-1, keepdims=True) + NORM_EPS)
        u_ref[...] = (x * r * g_ref[...]).astype(BF16)
        r_ref[...] = r

    return _call(
        body, name=name, grid=(Tp // tm,),
        in_specs=[pl.BlockSpec((tm, D), lambda i: (i, 0)), pl.BlockSpec((1, D), lambda i: (0, 0))],
        out_specs=[pl.BlockSpec((tm, D), lambda i: (i, 0)), pl.BlockSpec((tm, 1), lambda i: (i, 0))],
        out_shape=[_sds((Tp, D), BF16), _sds((Tp, 1), F32)],
        compiler_params=_cp("parallel"),
    )(h, g)


def _proj_gather(u, w_shard, extras, order, name, tasks=()):
    Tp, K = u.shape
    n = w_shard.shape[1]
    tm = _tile(Tp, TILE["proj"])
    nt = Tp // tm
    shards = [w_shard] + list(extras)
    nu = len(shards)
    t_ins = [a for t in tasks for a in t.ins]
    t_outs = [o for t in tasks for o in t.out_shapes]
    assert not any(t.aliases for t in tasks)
    n_task_sem = sum(t.n_sem for t in tasks)
    n_task_local = sum(t.n_local for t in tasks)

    def body(ord_ref, a_ref, *refs):
        cut = [nu, len(t_ins), 1, nu, len(t_outs)]
        parts, at = [], 0
        for m in cut:
            parts.append(refs[at:at + m])
            at += m
        ins, tin, (proj_ref,), outs, tout = parts
        wbuf, wsem, send, recv, local = refs[at:]
        k, i = pl.program_id(0), pl.program_id(1)
        x, y, c = _pos()
        me, sib = (x, y, c), (x, y, 1 - c)
        x_nbr, y_nbr, diag = (1 - x, y, c), (x, 1 - y, c), (1 - x, 1 - y, c)
        came_from = (x + (1 - c) * (1 - 2 * x), y + c * (1 - 2 * y), c)
        pass_to = (x + c * (1 - 2 * x), y + (1 - c) * (1 - 2 * y), c)

        def other(p):
            return (p[0], p[1], 1 - c)

        def slot(p):
            return 4 * p[0] + 2 * p[1] + p[2]

        def copy(u_, kk, block, to, from_shard=False):
            dst = outs[u_].at[slot(block)]
            return pltpu.make_async_remote_copy(
                src_ref=ins[u_] if from_shard else dst, dst_ref=dst,
                send_sem=send.at[7 * u_ + kk], recv_sem=recv.at[7 * u_ + kk], device_id=to, device_id_type=MESH)

        def own_copy(u_):
            return pltpu.make_async_copy(ins[u_], outs[u_].at[slot(me)], local.at[u_])

        def w_load(kk):
            return pltpu.make_async_copy(outs[0].at[ord_ref[kk]], wbuf.at[kk % 2], wsem.at[kk % 2])

        def each_task(method):
            ti, to_, s_, l_ = 0, 0, 7 * nu, nu
            for t in tasks:
                if hasattr(t, method):
                    getattr(t, method)(tin[ti:ti + len(t.ins)], tout[to_:to_ + len(t.out_shapes)],
                                       (send, recv, local, s_, l_))
                ti, to_, s_, l_ = ti + len(t.ins), to_ + len(t.out_shapes), s_ + t.n_sem, l_ + t.n_local

        def all_units(make):
            return [make(u_) for u_ in range(nu)]

        def ready(kk):
            if kk == 1:
                for cp in all_units(lambda u_: copy(u_, 0, sib, me)):
                    cp.wait_recv()
            elif kk == 2:
                for u_ in range(nu):
                    copy(u_, 1 + c, came_from, me).wait_recv()
                    copy(u_, 3, came_from, pass_to).start()
                    copy(u_, 4, came_from, sib).start()
            elif kk == 3:
                for u_ in range(nu):
                    copy(u_, 2 - c, pass_to, me).wait_recv()
                    copy(u_, 5, pass_to, sib).start()
                each_task("start")
            elif kk == 4:
                for cp in all_units(lambda u_: copy(u_, 4, other(pass_to), me)):
                    cp.wait_recv()
            elif kk == 5:
                for cp in all_units(lambda u_: copy(u_, 5, other(came_from), me)):
                    cp.wait_recv()
            elif kk == 6:
                for u_ in range(nu):
                    copy(u_, 3, diag, me).wait_recv()
                    copy(u_, 6, diag, sib).start()
            elif kk == 7:
                for cp in all_units(lambda u_: copy(u_, 6, other(diag), me)):
                    cp.wait_recv()
                each_task("mid")

        @pl.when((k == 0) & (i == 0))
        def _():
            for cp in all_units(own_copy):
                cp.start()
            for u_ in range(nu):
                for kk, p in ((0, sib), (1, x_nbr), (2, y_nbr)):
                    copy(u_, kk, me, p, True).start()
            own_copy(0).wait()
            w_load(0).start()

        for kk in range(1, N_DEV):
            @pl.when((k == kk - 1) & (i == nt - 1))
            def _(kk=kk):
                ready(kk)
                w_load(kk).start()

        for kk in range(N_DEV):
            @pl.when((k == kk) & (i == 0))
            def _(kk=kk):
                w_load(kk).wait()

        proj_ref[...] = jnp.dot(a_ref[...], wbuf[k % 2], preferred_element_type=F32)

        @pl.when((k == N_DEV - 1) & (i == nt - 1))
        def _():
            for u_ in range(nu):
                for kk, p in ((0, sib), (1, x_nbr), (2, y_nbr)):
                    copy(u_, kk, me, p, True).wait_send()
                copy(u_, 3, came_from, pass_to).wait_send()
                for kk, p in ((4, came_from), (5, pass_to), (6, diag)):
                    copy(u_, kk, p, sib).wait_send()
            for u_ in range(1, nu):
                own_copy(u_).wait()
            each_task("finish")

    hbm = pl.BlockSpec(memory_space=pl.ANY)
    res = _call(
        body, name=name,
        grid_spec=pltpu.PrefetchScalarGridSpec(
            num_scalar_prefetch=1, grid=(N_DEV, nt),
            in_specs=[pl.BlockSpec((tm, K), lambda k, i, o: (i, 0))] + [hbm] * (nu + len(t_ins)),
            out_specs=[pl.BlockSpec((tm, n), lambda k, i, o: (i, o[k]))] + [hbm] * (nu + len(t_outs)),
            scratch_shapes=[pltpu.VMEM((2, K, n), w_shard.dtype), pltpu.SemaphoreType.DMA((2,)),
                            pltpu.SemaphoreType.DMA((7 * nu + n_task_sem,)),
                            pltpu.SemaphoreType.DMA((7 * nu + n_task_sem,)),
                            pltpu.SemaphoreType.DMA((nu + max(1, n_task_local),))]),
        out_shape=[_sds((Tp, N_DEV * n), F32)] + [_sds((N_DEV,) + s.shape, s.dtype) for s in shards] + t_outs,
        compiler_params=_cp("arbitrary", "arbitrary"),
    )(order, u, *shards, *t_ins)
    res = list(res)
    task_outs, o = [], 1 + nu
    for t in tasks:
        task_outs.append(res[o:o + len(t.out_shapes)])
        o += len(t.out_shapes)
    return res[0], res[1], res[2:1 + nu], task_outs


def _proj_fwd(u, w_slots, name, tasks=(), part=(0, 1), earlier=None):
    Tp, K = u.shape
    S, _, n = w_slots.shape
    p, parts = part
    tm = _tile(Tp, TILE["proj"])

    def body(a_ref, b_ref, *rest):
        o_ref = rest[-1]
        o_ref[...] = jnp.dot(a_ref[...], b_ref[...], preferred_element_type=F32)

    in_specs = [pl.BlockSpec((tm, K), lambda j, i: (i, 0)), pl.BlockSpec((None, K, n), lambda j, i: (j, 0, 0))]
    operands = (u, w_slots)
    aliases = {}
    if earlier is not None:
        in_specs.append(pl.BlockSpec(memory_space=pl.ANY))
        operands += (earlier,)
        aliases = {2: 0}
    (proj,), extra = _hosted(
        body, tasks, name=name, grid=(S, Tp // tm), in_specs=in_specs,
        out_specs=[pl.BlockSpec((tm, n), lambda j, i: (i, j * parts + p))],
        out_shape=[_sds((Tp, S * parts * n), F32)],
        semantics=("parallel", "parallel"), operands=operands, aliases=aliases)
    return proj, extra


def _mlp1_fwd(u2, w_slots, name, tasks=()):
    Tp, K = u2.shape
    S, _, n = w_slots.shape
    tm = _tile(Tp, TILE["mlp1"])

    def body(a_ref, b_ref, act_ref, a1_ref):
        a1 = jnp.dot(a_ref[...], b_ref[...], preferred_element_type=F32)
        relu = jnp.maximum(a1, 0.0)
        act_ref[...] = (relu * relu).astype(BF16)
        a1_ref[...] = a1.astype(BF16)

    return _hosted(
        body, tasks, name=name, grid=(S, Tp // tm),
        in_specs=[pl.BlockSpec((tm, K), lambda j, i: (i, 0)), pl.BlockSpec((None, K, n), lambda j, i: (j, 0, 0))],
        out_specs=[pl.BlockSpec((tm, n), lambda j, i: (i, j))] * 2,
        out_shape=[_sds((Tp, S * n), BF16)] * 2,
        semantics=("parallel", "parallel"), operands=(u2, w_slots))


def _pool_fwd(proj, pool_w, name):
    Tp = proj.shape[0]
    G, Cg, _ = pool_w.shape
    D = G * Cg
    tm = _tile(Tp, TILE["pool"])

    def body(v_ref, w_ref, d_ref, y_ref, prev_ref):
        t = pl.program_id(0)

        @pl.when(t == 0)
        def _():
            prev_ref[...] = jnp.zeros_like(prev_ref)

        rows = _row_ids(t, tm)
        for g, win in enumerate(POOL_WINDOWS):
            cols = slice(g * Cg, (g + 1) * Cg)
            v = v_ref[:, cols]
            s = jnp.concatenate([prev_ref[:, cols], v], axis=0)
            k = 1
            while k < win:
                s = s + pltpu.roll(s, k, axis=0)
                k *= 2
            cnt = jnp.minimum(rows + 1, win).astype(F32)
            d = s[MAX_WINDOW:] / cnt - v
            d_ref[:, cols] = d.astype(BF16)
            y_ref[:, cols] = jnp.dot(d.astype(BF16), w_ref[g], preferred_element_type=F32)
        prev_ref[...] = v_ref[tm - MAX_WINDOW:, :]

    return _call(
        body, name=name, grid=(Tp // tm,),
        in_specs=[pl.BlockSpec((tm, D), lambda t: (t, 0)), pl.BlockSpec((G, Cg, Cg), lambda t: (0, 0, 0))],
        out_specs=[pl.BlockSpec((tm, D), lambda t: (t, 0))] * 2,
        out_shape=[_sds((Tp, D), BF16), _sds((Tp, D), F32)],
        scratch_shapes=[pltpu.VMEM((MAX_WINDOW, D), F32)],
        compiler_params=_cp("arbitrary"),
    )(proj, pool_w)


def _lru_fwd(proj, y_pool, scale, conv_w, conv_b, wa, ba, wx, bx, lam, name, tasks=()):
    Tp = proj.shape[0]
    H, hd, _ = wa.shape
    D = H * hd
    tm = _tile(Tp, TILE["lru"])
    nb = D // hd

    def body(vl_ref, vg_ref, gp_ref, gl_ref, y_ref, sc_ref, cw_ref, cb_ref, wa_ref, ba_ref, wx_ref, bx_ref,
             lam_ref, xc_ref, r_ref, i_ref, a_ref, mult_ref, hs_ref, m_ref, prev_ref, carry_ref, b_s):
        t = pl.program_id(1)

        @pl.when(t == 0)
        def _():
            prev_ref[...] = jnp.zeros_like(prev_ref)
            carry_ref[...] = jnp.zeros_like(carry_ref)

        v = vl_ref[...]
        prev = prev_ref[...]
        xc = jnp.zeros_like(v) + cb_ref[...]
        for k in range(CONV_WIDTH):
            xc = xc + cw_ref[k:k + 1, :] * _shift_down(prev, v, CONV_WIDTH - 1 - k)
        prev_ref[...] = v[tm - HALO:, :]
        xcb = xc.astype(BF16)
        r = _sigmoid(jnp.dot(xcb, wa_ref[...], preferred_element_type=F32) + ba_ref[...])
        i = _sigmoid(jnp.dot(xcb, wx_ref[...], preferred_element_type=F32) + bx_ref[...])
        a, mult = _lru_gates(r, _softplus(-lam_ref[...]))
        a_ref[...] = a
        mult_ref[...] = mult
        b_s[...] = mult * (i * xc)
        xc_ref[...] = xc
        r_ref[...] = r
        i_ref[...] = i
        carry_ref[0:1, :] = _scan_chunks(a_ref, b_s, hs_ref, carry_ref[0:1, :], tm, reverse=False)
        lru_out = hs_ref[...] * _gelu(vg_ref[...])
        pool_out = y_ref[...] * sc_ref[...]
        m_ref[...] = (_sigmoid(gp_ref[...]) * pool_out + _sigmoid(gl_ref[...]) * lru_out).astype(BF16)

    def piece(p):
        return pl.BlockSpec((tm, hd), lambda h, t: (t, p * nb + h))

    blk = pl.BlockSpec((tm, hd), lambda h, t: (t, h))
    vec = pl.BlockSpec((1, hd), lambda h, t: (0, h))
    mat = pl.BlockSpec((None, hd, hd), lambda h, t: (h, 0, 0))
    bias = pl.BlockSpec((None, 1, hd), lambda h, t: (h, 0, 0))
    return _hosted(
        body, tasks, name=name, grid=(H, Tp // tm),
        in_specs=[piece(1), piece(2), piece(3), piece(4), blk, vec,
                  pl.BlockSpec((CONV_WIDTH, hd), lambda h, t: (0, h)), vec, mat, bias, mat, bias, vec],
        out_specs=[blk] * 7,
        out_shape=[_sds((Tp, D), F32)] * 6 + [_sds((Tp, D), BF16)],
        scratch_shapes=[pltpu.VMEM((HALO, hd), F32), pltpu.VMEM((8, hd), F32), pltpu.VMEM((tm, hd), F32)],
        semantics=("parallel", "arbitrary"),
        operands=(proj, proj, proj, proj, y_pool, scale, conv_w, conv_b, wa, ba.reshape(H, 1, hd), wx,
                  bx.reshape(H, 1, hd), lam))


def _wout_norm_fwd(merged, w_out, h0, g2, name, tasks=()):
    Tp, D = h0.shape
    tm = _tile(Tp, TILE["wout"])

    def body(m_ref, w_ref, h0_ref, g_ref, h1_ref, u2_ref, r2_ref):
        h1 = h0_ref[...] + jnp.dot(m_ref[...], w_ref[...], preferred_element_type=F32)
        r = lax.rsqrt(jnp.mean(h1 * h1, axis=-1, keepdims=True) + NORM_EPS)
        h1_ref[...] = h1
        u2_ref[...] = (h1 * r * g_ref[...]).astype(BF16)
        r2_ref[...] = r

    row = pl.BlockSpec((tm, D), lambda i: (i, 0))
    return _hosted(
        body, tasks, name=name, grid=(Tp // tm,),
        in_specs=[row, pl.BlockSpec((D, D), lambda i: (0, 0)), row, pl.BlockSpec((1, D), lambda i: (0, 0))],
        out_specs=[row, row, pl.BlockSpec((tm, 1), lambda i: (i, 0))],
        out_shape=[_sds((Tp, D), F32), _sds((Tp, D), BF16), _sds((Tp, 1), F32)],
        semantics=("parallel",), operands=(merged, w_out, h0, g2))


def _mlp2_loss(act, w2, h1, target, gf, n_meta, seq, name):
    Tp, D = h1.shape
    K = act.shape[1]
    tm = _tile(Tp, TILE["mlp2"])
    tk = min(K, MLP2_K)
    nk = K // tk

    rc = _tile(tm, EPILOGUE_ROWS)

    def body(a_ref, w_ref, h1_hbm, t_hbm, g_ref, dh_ref, dhb_ref, loss_ref, dg_ref, h1_buf, t_buf, sems):
        i, k = pl.program_id(0), pl.program_id(1)
        tile_rows = pl.ds(pl.multiple_of(i * tm, tm), tm)
        fetch = [pltpu.make_async_copy(h1_hbm.at[tile_rows, :], h1_buf, sems.at[0]),
                 pltpu.make_async_copy(t_hbm.at[tile_rows, :], t_buf, sems.at[1])]

        @pl.when(k == 0)
        def _():
            for f in fetch:
                f.start()
            dh_ref[...] = jnp.zeros_like(dh_ref)

        @pl.when((i == 0) & (k == 0))
        def _():
            loss_ref[...] = jnp.zeros_like(loss_ref)
            dg_ref[...] = jnp.zeros_like(dg_ref)

        dh_ref[...] += jnp.dot(a_ref[...], w_ref[...], preferred_element_type=F32)

        @pl.when(k == nk - 1)
        def _():
            for f in fetch:
                f.wait()
            g = g_ref[...]

            def chunk(c, carry):
                loss_acc, dg_acc = carry
                rows = pl.ds(pl.multiple_of(c * rc, rc), rc)
                h2 = h1_buf[rows, :] + dh_ref[rows, :]
                r = lax.rsqrt(jnp.mean(h2 * h2, axis=-1, keepdims=True) + NORM_EPS)
                out = h2 * r * g
                row_id = i * tm + c * rc + lax.broadcasted_iota(jnp.int32, (rc, 1), 0)
                valid = (row_id >= n_meta) & (row_id < n_meta + seq)
                diff = jnp.where(valid, out - t_buf[rows, :], 0.0)
                dout = diff / D
                dog = dout * g
                dh = r * dog - h2 * (r * r * r * jnp.mean(dog * h2, axis=-1, keepdims=True))
                dh_ref[rows, :] = dh
                dhb_ref[rows, :] = dh.astype(BF16)
                loss_acc = loss_acc + 0.5 * jnp.sum(jnp.mean(diff * diff, axis=-1, keepdims=True), axis=0, keepdims=True)
                return loss_acc, dg_acc + jnp.sum(dout * (h2 * r), axis=0, keepdims=True)

            loss_sum, dg_sum = lax.fori_loop(0, tm // rc, chunk, (jnp.zeros((1, 1), F32), jnp.zeros((1, D), F32)))
            loss_ref[...] += loss_sum
            dg_ref[...] += dg_sum

    row = pl.BlockSpec((tm, D), lambda i, k: (i, 0))
    hbm = pl.BlockSpec(memory_space=pl.ANY)
    return _call(
        body, name=name, grid=(Tp // tm, nk),
        in_specs=[pl.BlockSpec((tm, tk), lambda i, k: (i, k)), pl.BlockSpec((tk, D), lambda i, k: (k, 0)),
                  hbm, hbm, pl.BlockSpec((1, D), lambda i, k: (0, 0))],
        out_specs=[row, row, pl.BlockSpec((8, 128), lambda i, k: (0, 0)), pl.BlockSpec((1, D), lambda i, k: (0, 0))],
        out_shape=[_sds((Tp, D), F32), _sds((Tp, D), BF16), _sds((8, 128), F32), _sds((1, D), F32)],
        scratch_shapes=[pltpu.VMEM((tm, D), F32), pltpu.VMEM((tm, D), F32), pltpu.SemaphoreType.DMA((2,))],
        compiler_params=_cp("arbitrary", "arbitrary"),
    )(act, w2, h1, target, gf)


def _dact_bwd(dh2b, w2_slots, a1, name):
    Tp, D = dh2b.shape
    S, n, _ = w2_slots.shape
    tm = _tile(Tp, TILE["dact"])

    def body(g_ref, w_ref, a1_ref, o_ref):
        dact = lax.dot_general(g_ref[...], w_ref[...], _NT, preferred_element_type=F32)
        o_ref[...] = (dact * (2.0 * jnp.maximum(a1_ref[...].astype(F32), 0.0))).astype(BF16)

    return _call(
        body, name=name, grid=(S, Tp // tm),
        in_specs=[pl.BlockSpec((tm, D), lambda j, i: (i, 0)), pl.BlockSpec((None, n, D), lambda j, i: (j, 0, 0)),
                  pl.BlockSpec((tm, n), lambda j, i: (i, j))],
        out_specs=pl.BlockSpec((tm, n), lambda j, i: (i, j)),
        out_shape=_sds((Tp, S * n), BF16),
        compiler_params=_cp("parallel", "parallel"),
    )(dh2b, w2_slots, a1)


def _weight_grad(a, g, blocks, block_a, name, tasks=(), part=(0, 1)):
    Tp, Ka = a.shape
    Ng = g.shape[1]
    p, parts = part
    assert parts == 1 or not block_a
    ka = Ka // blocks if block_a else Ka // parts
    ng = Ng if block_a else Ng // blocks
    tt = _tile(Tp, TILE["tn"])
    nt = Tp // tt

    def body(a_ref, g_ref, o_ref, acc_ref):
        t = pl.program_id(1)

        @pl.when(t == 0)
        def _():
            acc_ref[...] = jnp.zeros_like(acc_ref)

        acc_ref[...] += lax.dot_general(a_ref[...], g_ref[...], _TN, preferred_element_type=F32)

        @pl.when(t == nt - 1)
        def _():
            o_ref[...] = acc_ref[...].astype(o_ref.dtype)

    if block_a:
        a_spec = pl.BlockSpec((tt, ka), lambda j, t: (t, j))
        g_spec = pl.BlockSpec((tt, ng), lambda j, t: (t, 0))
    else:
        a_spec = pl.BlockSpec((tt, ka), lambda j, t: (t, p))
        g_spec = pl.BlockSpec((tt, ng), lambda j, t: (t, j))
    (dw,), extra = _hosted(
        body, tasks, name=name, grid=(blocks, nt),
        in_specs=[a_spec, g_spec],
        out_specs=[pl.BlockSpec((None, ka, ng), lambda j, t: (j, 0, 0))],
        out_shape=[_sds((blocks, ka, ng), BF16)],
        scratch_shapes=[pltpu.VMEM((ka, ng), F32)],
        semantics=("parallel", "arbitrary"), operands=(a, g))
    return dw, extra


def _nt_norm_bwd(dz, w_parts, dres, hin, rin, g, want_bf16, name, tasks=(), tiles=None, earlier=None):
    Tp, D = hin.shape
    P = len(w_parts)
    S, _, n = w_parts[0].shape
    K = S * P
    tm = _tile(Tp, TILE["nt"])
    t0, nt = tiles if tiles is not None else (0, Tp // tm)
    assert not (want_bf16 and earlier is not None)

    rc = _tile(tm, EPILOGUE_ROWS)

    def body(dz_ref, *rest):
        w_refs, (dres_hbm, h_hbm, r_ref, g_ref), rest = rest[:P], rest[P:P + 4], rest[P + 4:]
        if earlier is not None:
            _, dg0_ref, dh_ref, dg_ref, dres_buf, h_buf, sems = rest
        elif want_bf16:
            dh_ref, dhb_ref, dg_ref, dres_buf, h_buf, sems = rest
        else:
            dh_ref, dg_ref, dres_buf, h_buf, sems = rest
        i, k = pl.program_id(0), pl.program_id(1)
        tile_rows = pl.ds(pl.multiple_of((t0 + i) * tm, tm), tm)
        fetch = [pltpu.make_async_copy(dres_hbm.at[tile_rows, :], dres_buf, sems.at[0]),
                 pltpu.make_async_copy(h_hbm.at[tile_rows, :], h_buf, sems.at[1])]

        @pl.when(k == 0)
        def _():
            for f in fetch:
                f.start()
            dh_ref[...] = jnp.zeros_like(dh_ref)

        @pl.when((i == 0) & (k == 0))
        def _():
            dg_ref[...] = jnp.zeros_like(dg_ref) if earlier is None else dg0_ref[...]

        for q in range(P):
            @pl.when(k % P == q)
            def _(q=q):
                dh_ref[...] += lax.dot_general(dz_ref[...], w_refs[q][...], _NT, preferred_element_type=F32)

        @pl.when(k == K - 1)
        def _():
            for f in fetch:
                f.wait()
            g = g_ref[...]

            def chunk(c, dg_acc):
                rows = pl.ds(pl.multiple_of(c * rc, rc), rc)
                du = dh_ref[rows, :]
                h = h_buf[rows, :]
                r = r_ref[rows, :]
                dug = du * g
                dh = dres_buf[rows, :] + r * dug - h * (r * r * r * jnp.mean(dug * h, axis=-1, keepdims=True))
                dh_ref[rows, :] = dh
                if want_bf16:
                    dhb_ref[rows, :] = dh.astype(BF16)
                return dg_acc + jnp.sum(du * (h * r), axis=0, keepdims=True)

            dg_ref[...] += lax.fori_loop(0, tm // rc, chunk, jnp.zeros((1, D), F32))

    row = pl.BlockSpec((tm, D), lambda i, k: (t0 + i, 0))
    vec = pl.BlockSpec((1, D), lambda i, k: (0, 0))
    hbm = pl.BlockSpec(memory_space=pl.ANY)
    out_specs = [row] + ([row] if want_bf16 else []) + [vec]
    out_shape = [_sds((Tp, D), F32)] + ([_sds((Tp, D), BF16)] if want_bf16 else []) + [_sds((1, D), F32)]
    in_specs = ([pl.BlockSpec((tm, n), lambda i, k: (t0 + i, k))]
                + [pl.BlockSpec((None, D, n), lambda i, k: (k // P, 0, 0))] * P
                + [hbm, hbm, pl.BlockSpec((tm, 1), lambda i, k: (t0 + i, 0)), vec])
    operands = (dz, *w_parts, dres, hin, rin, g)
    aliases = {}
    if earlier is not None:
        in_specs += [hbm, vec]
        operands += tuple(earlier)
        aliases = {P + 5: 0}
    return _hosted(
        body, tasks, name=name, grid=(nt, K), in_specs=in_specs, out_specs=out_specs, out_shape=out_shape,
        scratch_shapes=[pltpu.VMEM((tm, D), F32), pltpu.VMEM((tm, D), F32), pltpu.SemaphoreType.DMA((2,))],
        semantics=("arbitrary", "arbitrary"), operands=operands, aliases=aliases)


def _dmerged_bwd(dh1b, w_out, name):
    Tp, D = dh1b.shape
    tm = _tile(Tp, TILE["dmerged"])

    def body(g_ref, w_ref, o_ref):
        o_ref[...] = lax.dot_general(g_ref[...], w_ref[...], _NT, preferred_element_type=F32)

    row = pl.BlockSpec((tm, D), lambda i: (i, 0))
    return _call(
        body, name=name, grid=(Tp // tm,),
        in_specs=[row, pl.BlockSpec((D, D), lambda i: (0, 0))],
        out_specs=row, out_shape=_sds((Tp, D), F32),
        compiler_params=_cp("parallel"),
    )(dh1b, w_out)


def _pool_bwd(dmerged, proj, y_pool, d_pool, scale, pool_w, name, tasks=()):
    Tp, D = dmerged.shape
    G, Cg, _ = pool_w.shape
    tm = _tile(Tp, TILE["pool"])
    nt = Tp // tm

    def body(dm_ref, gp_ref, y_ref, d_ref, sc_ref, w_ref, dproj_hbm, dw_ref, dsc_ref, next_ref, out_buf, out_sems):
        t = pl.program_id(0)
        tile = nt - 1 - t
        slot = t % 2
        dv_ref, dgp_ref = out_buf.at[slot, 0], out_buf.at[slot, 1]
        tile_rows = pl.ds(pl.multiple_of(tile * tm, tm), tm)

        def out_copies(s):
            return [pltpu.make_async_copy(out_buf.at[s, k], dproj_hbm.at[tile_rows, pl.ds(piece * D, D)],
                                          out_sems.at[2 * s + k]) for k, piece in enumerate((0, 3))]

        @pl.when(t >= 2)
        def _():
            for cp in out_copies(slot):
                cp.wait()

        @pl.when(t == 0)
        def _():
            next_ref[...] = jnp.zeros_like(next_ref)
            dw_ref[...] = jnp.zeros_like(dw_ref)
            dsc_ref[...] = jnp.zeros_like(dsc_ref)

        rows = _row_ids(tile, tm)
        dm = dm_ref[...]
        y = y_ref[...]
        sc = sc_ref[...]
        sg = _sigmoid(gp_ref[...])
        dpo = dm * sg
        dgp_ref[...] = (dm * (y * sc) * sg * (1.0 - sg)).astype(BF16)
        dsc_ref[...] += jnp.sum(dpo * y, axis=0, keepdims=True)
        dyb = (dpo * sc).astype(BF16)
        for g, win in enumerate(POOL_WINDOWS):
            cols = slice(g * Cg, (g + 1) * Cg)
            dy = dyb[:, cols]
            dd = lax.dot_general(dy, w_ref[g], _NT, preferred_element_type=F32)
            dw_ref[g] += lax.dot_general(d_ref[:, cols], dy, _TN, preferred_element_type=F32)
            q = dd / jnp.minimum(rows + 1, win).astype(F32)
            s = jnp.concatenate([q, next_ref[:, cols]], axis=0)
            k = 1
            while k < win:
                s = s + pltpu.roll(s, s.shape[0] - k, axis=0)
                k *= 2
            dv_ref[:, cols] = (s[:tm] - dd).astype(BF16)
            next_ref[:, cols] = q[:MAX_WINDOW]
        for cp in out_copies(slot):
            cp.start()

        @pl.when(t == nt - 1)
        def _():
            for cp in out_copies(slot) + (out_copies(1 - slot) if nt > 1 else []):
                cp.wait()

    row = pl.BlockSpec((tm, D), lambda t: (nt - 1 - t, 0))
    return _hosted(
        body, tasks, name=name, grid=(nt,),
        in_specs=[row, pl.BlockSpec((tm, D), lambda t: (nt - 1 - t, 3)), row, row,
                  pl.BlockSpec((1, D), lambda t: (0, 0)), pl.BlockSpec((G, Cg, Cg), lambda t: (0, 0, 0))],
        out_specs=[pl.BlockSpec(memory_space=pl.ANY), pl.BlockSpec((G, Cg, Cg), lambda t: (0, 0, 0)),
                   pl.BlockSpec((1, D), lambda t: (0, 0))],
        out_shape=[_sds((Tp, proj.shape[1]), BF16), _sds((G, Cg, Cg), F32), _sds((1, D), F32)],
        scratch_shapes=[pltpu.VMEM((MAX_WINDOW, D), F32), pltpu.VMEM((2, 2, tm, D), BF16),
                        pltpu.SemaphoreType.DMA((4,))],
        semantics=("arbitrary",), operands=(dmerged, proj, y_pool, d_pool, scale, pool_w))


LRU_SMALL_ROWS = 8


def _lru_bwd(dmerged, proj, xc, r_gate, i_gate, a_gate, mult_gate, hs, lam, conv_w, wa, wx, dproj, name, tasks=()):
    Tp, D = dmerged.shape
    H, hd, _ = wa.shape
    tm = _tile(Tp, TILE["lru"])
    nt = Tp // tm
    nb = D // hd
    halo_blocks = tm // HALO

    def body(dm_ref, vl_ref, vg_ref, gl_ref, xc_ref, r_ref, i_ref, a_ref, mult_ref, hs_ref, hsp_ref, lam_ref, cw_ref,
             wa_ref, wx_ref, _, dproj_hbm, dwa_ref, dwx_ref, small_ref,
             mu_next_ref, dxc_next_ref, q_s, mu_s, out_buf, out_sems):
        h_id, t = pl.program_id(0), pl.program_id(1)
        tile = nt - 1 - t
        step = h_id * nt + t
        slot = step % 2
        dvl_ref, dvg_ref, dgl_ref = out_buf.at[slot, 0], out_buf.at[slot, 1], out_buf.at[slot, 2]
        tile_rows = pl.ds(pl.multiple_of(tile * tm, tm), tm)

        def out_copies(s):
            return [pltpu.make_async_copy(
                out_buf.at[s, k], dproj_hbm.at[tile_rows, pl.ds(pl.multiple_of((piece * nb + h_id) * hd, hd), hd)],
                out_sems.at[3 * s + k]) for k, piece in enumerate((1, 2, 4))]

        @pl.when(step >= 2)
        def _():
            for cp in out_copies(slot):
                cp.wait()

        @pl.when(t == 0)
        def _():
            mu_next_ref[...] = jnp.zeros_like(mu_next_ref)
            dxc_next_ref[...] = jnp.zeros_like(dxc_next_ref)
            dwa_ref[...] = jnp.zeros_like(dwa_ref)
            dwx_ref[...] = jnp.zeros_like(dwx_ref)
            small_ref[...] = jnp.zeros_like(small_ref)

        first = tile == 0
        dm = dm_ref[...]
        hs_t = hs_ref[...]
        xc_t = xc_ref[...]
        r = r_ref[...]
        i = i_ref[...]
        lam_v = lam_ref[...]
        sp = _softplus(-lam_v)
        a = a_ref[...]
        mult = mult_ref[...]

        sg = _sigmoid(gl_ref[...])
        ge, dge = _gelu_and_grad(vg_ref[...])
        dlo = dm * sg
        dgl_ref[...] = (dm * (hs_t * ge) * sg * (1.0 - sg)).astype(BF16)
        dvg_ref[...] = (dlo * hs_t * dge).astype(BF16)
        dhs = dlo * ge

        q_s[...] = a * dhs
        mu_first = _scan_chunks(a_ref, q_s, mu_s, mu_next_ref[0:1, :], tm, reverse=True)
        lam_t = dhs + _shift_up(mu_s[...], mu_next_ref[...], 1)
        mu_next_ref[...] = jnp.broadcast_to(mu_first, mu_next_ref.shape)

        h_prev = _shift_down(jnp.where(first, 0.0, hsp_ref[...]), hs_t, 1)
        da = lam_t * h_prev
        dmult = lam_t * (i * xc_t)
        di = lam_t * mult * xc_t
        dxc = lam_t * mult * i
        dlog_a = da * a - dmult * (a * a) / mult
        dr = dlog_a * (-LRU_C * sp)
        dlam_rows = dlog_a * (-LRU_C * r)
        dza = dr * r * (1.0 - r)
        dzx = di * i * (1.0 - i)
        dzab, dzxb = dza.astype(BF16), dzx.astype(BF16)
        xcb = xc_t.astype(BF16)
        dxc = dxc + lax.dot_general(dzab, wa_ref[...], _NT, preferred_element_type=F32)
        dxc = dxc + lax.dot_general(dzxb, wx_ref[...], _NT, preferred_element_type=F32)
        dwa_ref[...] += lax.dot_general(xcb, dzab, _TN, preferred_element_type=F32)
        dwx_ref[...] += lax.dot_general(xcb, dzxb, _TN, preferred_element_type=F32)

        dxc_next = dxc_next_ref[...]
        taps = [_shift_up(dxc, dxc_next, CONV_WIDTH - 1 - k) for k in range(CONV_WIDTH)]
        dv = jnp.zeros_like(dxc)
        for k in range(CONV_WIDTH):
            dv = dv + cw_ref[k:k + 1, :] * taps[k]
        dvl_ref[...] = dv.astype(BF16)
        dxc_next_ref[...] = dxc[:HALO, :]

        v_t = vl_ref[...]
        small = [jnp.sum(dza, axis=0, keepdims=True), jnp.sum(dzx, axis=0, keepdims=True),
                 jnp.sum(dlam_rows, axis=0, keepdims=True) * (-_sigmoid(-lam_v)),
                 jnp.sum(dxc, axis=0, keepdims=True)]
        for k in range(CONV_WIDTH):
            small.append(jnp.sum(taps[k] * v_t, axis=0, keepdims=True))
        for k, row in enumerate(small):
            small_ref[k:k + 1, :] += row
        for cp in out_copies(slot):
            cp.start()

        @pl.when(step == H * nt - 1)
        def _():
            for cp in out_copies(slot) + (out_copies(1 - slot) if H * nt > 1 else []):
                cp.wait()

    def piece(p):
        return pl.BlockSpec((tm, hd), lambda h, t: (nt - 1 - t, p * nb + h))

    def halo(p):
        return pl.BlockSpec((HALO, hd), lambda h, t: (jnp.maximum((nt - 1 - t) * halo_blocks - 1, 0), p * nb + h))

    blk = pl.BlockSpec((tm, hd), lambda h, t: (nt - 1 - t, h))
    vec = pl.BlockSpec((1, hd), lambda h, t: (0, h))
    mat = pl.BlockSpec((None, hd, hd), lambda h, t: (h, 0, 0))
    return _hosted(
        body, tasks, name=name, grid=(H, nt),
        in_specs=[blk, piece(1), piece(2), piece(4), blk, blk, blk, blk, blk, blk, halo(0), vec,
                  pl.BlockSpec((CONV_WIDTH, hd), lambda h, t: (0, h)), mat, mat, pl.BlockSpec(memory_space=pl.ANY)],
        out_specs=[pl.BlockSpec(memory_space=pl.ANY), mat, mat,
                   pl.BlockSpec((None, LRU_SMALL_ROWS, hd), lambda h, t: (h, 0, 0))],
        out_shape=[_sds(dproj.shape, BF16)] + [_sds((H, hd, hd), F32)] * 2 + [_sds((H, LRU_SMALL_ROWS, hd), F32)],
        scratch_shapes=[pltpu.VMEM((HALO, hd), F32), pltpu.VMEM((HALO, hd), F32),
                        pltpu.VMEM((tm, hd), F32), pltpu.VMEM((tm, hd), F32),
                        pltpu.VMEM((2, 3, tm, hd), BF16), pltpu.SemaphoreType.DMA((6,))],
        semantics=("arbitrary", "arbitrary"), aliases={15: 0},
        operands=(dmerged, proj, proj, proj, xc, r_gate, i_gate, a_gate, mult_gate, hs, hs, lam, conv_w, wa, wx,
                  dproj))


def _adamw(w, g, m, v):
    m = ADAM_B1 * m + (1.0 - ADAM_B1) * g
    v = ADAM_B2 * v + (1.0 - ADAM_B2) * (g * g)
    m_hat = m / (1.0 - ADAM_B1 ** ADAM_STEP)
    v_hat = v / (1.0 - ADAM_B2 ** ADAM_STEP)
    delta = -ADAM_LR * (m_hat / (jnp.sqrt(v_hat) + ADAM_EPS) + ADAM_WD * w)
    return delta, m, v


def _reduce_update(pair_sums, chip_sums, w, m, v, chip_slot, name, part=(0, 1), earlier=None):
    R, C = pair_sums.shape[1:]
    p, parts = part
    tr = _tile(R, TILE["update"])
    nblk = R // tr

    def body(slot_ref, own_ref, got_ref, w_ref, m_ref, v_ref, *rest):
        g_out, d_out, m_out, v_out = rest[-4:]
        g = own_ref[...].astype(F32)
        for k in range(3):
            g = g + got_ref[k].astype(F32)
        d, m_new, v_new = _adamw(w_ref[...], g, m_ref[...], v_ref[...])
        g_out[...] = g
        d_out[...] = d
        m_out[...] = m_new
        v_out[...] = v_new

    blk = pl.BlockSpec((tr, C), lambda i, s: (p * nblk + i, 0))
    in_specs = [pl.BlockSpec((None, tr, C), lambda i, s: (s[0], i, 0)),
                pl.BlockSpec((3, tr, C), lambda i, s: (0, i, 0)), blk, blk, blk]
    operands = (chip_slot, pair_sums, chip_sums, w, m, v)
    aliases = {}
    if earlier is not None:
        in_specs += [pl.BlockSpec(memory_space=pl.ANY)] * 4
        operands += tuple(earlier)
        aliases = {6 + k: k for k in range(4)}
    return _call(
        body, name=name,
        grid_spec=pltpu.PrefetchScalarGridSpec(
            num_scalar_prefetch=1, grid=(R // tr,), in_specs=in_specs, out_specs=[blk] * 4),
        out_shape=[_sds((parts * R, C), F32)] * 4,
        input_output_aliases=aliases,
        compiler_params=_cp("parallel"),
    )(*operands)


def _small_update(w, g, m, v, name):
    def body(w_ref, g_ref, m_ref, v_ref, d_out, m_out, v_out):
        d, m_new, v_new = _adamw(w_ref[...], g_ref[...], m_ref[...], v_ref[...])
        d_out[...] = d
        m_out[...] = m_new
        v_out[...] = v_new

    return _call(body, name=name, out_shape=[_sds(w.shape, F32)] * 3)(w, g, m, v)


def _slots_from_rows(full, lead):
    L, R, C = full.shape
    r = R // N_DEV
    return full.reshape(L, N_DEV, r, C).transpose(1, 0, 2, 3).reshape(N_DEV, L * r, C)


def _rows_from_slots(slots, lead):
    _, LR, C = slots.shape
    r = LR // lead
    return slots.reshape(N_DEV, lead, r, C).transpose(1, 0, 2, 3).reshape(lead, N_DEV * r, C)


def kernel(x, meta_tokens, norm1_g, w_in, pool_w, pool_scale, conv_w, conv_b, gate_a_w, gate_a_b, gate_x_w, gate_x_b, lru_lambda, w_out, norm2_g, mlp_w1, mlp_w2, final_g, loss_target, m_meta_tokens, m_norm1_g, m_w_in, m_pool_w, m_pool_scale, m_conv_w, m_conv_b, m_gate_a_w, m_gate_a_b, m_gate_x_w, m_gate_x_b, m_lru_lambda, m_w_out, m_norm2_g, m_mlp_w1, m_mlp_w2, m_final_g, v_meta_tokens, v_norm1_g, v_w_in, v_pool_w, v_pool_scale, v_conv_w, v_conv_b, v_gate_a_w, v_gate_a_b, v_gate_x_w, v_gate_x_b, v_lru_lambda, v_w_out, v_norm2_g, v_mlp_w1, v_mlp_w2, v_final_g):
    seq, D = x.shape[1], x.shape[2]
    n_meta = meta_tokens.shape[0]
    G, Cg = pool_w.shape[1], pool_w.shape[3]
    H, hd = gate_a_w.shape[1], gate_a_w.shape[3]
    T = n_meta + seq
    Tp = -(-T // ROW_ALIGN) * ROW_ALIGN
    ix, iy, ic = _pos()
    me = 4 * ix + 2 * iy + ic
    core = jnp.reshape(ic, (1,)).astype(jnp.int32)
    chip_slot = jnp.reshape(2 * ix + iy, (1,)).astype(jnp.int32)

    w_in_l, w1_l, w2_l, w_out_l = w_in[0], mlp_w1[0], mlp_w2[0], w_out[0]
    pool_l = pool_w[0].reshape(G * (Cg // N_DEV), Cg)
    wa_l = gate_a_w[0].reshape(H * (hd // N_DEV), hd)
    wx_l = gate_x_w[0].reshape(H * (hd // N_DEV), hd)
    small_params = jnp.concatenate(
        [meta_tokens, conv_w[0], jnp.zeros((4, D // N_DEV), F32)], axis=0)
    biases = jnp.concatenate([gate_a_b[0], gate_x_b[0]], axis=0)
    (small_g, bias_g) = _all_gather_relay([small_params, biases], [1, 1], "gather_small")
    small_full = small_g.transpose(1, 0, 2).reshape(n_meta + 8, D)
    meta_full = small_full[:n_meta]
    conv_full = small_full[n_meta:n_meta + CONV_WIDTH]
    bias_full = bias_g.transpose(1, 0, 2).reshape(2 * H, hd)
    ba_full, bx_full = bias_full[:H], bias_full[H:]

    h0 = jnp.concatenate([meta_full, x[0], jnp.zeros((Tp - T, D), F32)], axis=0)
    target = jnp.concatenate([jnp.zeros((n_meta, D), F32), loss_target[0], jnp.zeros((Tp - T, D), F32)], axis=0)
    u, r1 = _norm_fwd(h0, norm1_g, "norm1")
    first_chip = (ix + (1 - ic) * (1 - 2 * ix), iy + ic * (1 - 2 * iy))
    second_chip = (ix + ic * (1 - 2 * ix), iy + (1 - ic) * (1 - 2 * iy))
    order = jnp.stack([4 * px + 2 * py + pc for (px, py, pc) in (
        (ix, iy, ic), (ix, iy, 1 - ic), (*first_chip, ic), (*second_chip, ic), (*second_chip, 1 - ic),
        (*first_chip, 1 - ic), (1 - ix, 1 - iy, ic), (1 - ix, 1 - iy, 1 - ic))]).astype(jnp.int32)
    proj, w_in_g, (pool_g, wa_g, wx_g), ((w_out_g,), (w1_ici,)) = _proj_gather(
        u, w_in_l.astype(BF16), [pool_l.astype(BF16), wa_l.astype(BF16), wx_l.astype(BF16)], order, "proj",
        tasks=[_AgFull(w_out_l.astype(BF16)), _AgFull(w1_l.astype(BF16), forward_here=False)])
    pool_full = _rows_from_slots(pool_g, G)
    wa_full = _rows_from_slots(wa_g, H)
    wx_full = _rows_from_slots(wx_g, H)
    w_in_parts = [w_in_g]
    d_pool, y_pool = _pool_fwd(proj, pool_full, "pool_fwd")
    (xc, r_gate, i_gate, a_gate, mult_gate, hs, merged), ((w1_g,), (w2_ici,)) = _lru_fwd(
        proj, y_pool, pool_scale, conv_full, conv_b, wa_full, ba_full, wx_full, bx_full, lru_lambda, "lru_fwd",
        tasks=[_AgForward(w1_ici), _AgFull(w2_l.astype(BF16), forward_here=False)])
    w_out_full = w_out_g.reshape(D, D)
    (h1, u2, r2), ((w2_g,),) = _wout_norm_fwd(merged, w_out_full, h0, norm2_g, "wout_norm2", tasks=[_AgForward(w2_ici)])
    (act, a1), _ = _mlp1_fwd(u2, w1_g, "mlp1")
    dh2, dh2b, loss_tile, d_final_g = _mlp2_loss(
        act, w2_g.reshape(-1, D), h1, target, final_g.reshape(1, D), n_meta, seq, "mlp2_loss")

    def pair(part, got, tag):
        return _pair_sum(part, got, core, "pair_sum_" + tag)

    d_a1 = _dact_bwd(dh2b, w2_g, a1, "dact")
    dw2_p, _ = _weight_grad(act, dh2b, N_DEV, True, "dw2")
    dw1_p, ((dw2_got,),) = _weight_grad(u2, d_a1, N_DEV, False, "dw1", tasks=[_RsSibling(dw2_p)])
    dw2_pair = pair(dw2_p, dw2_got, "w2")
    (dh1, dh1b, d_norm2_g), ((dw2_chips,), (dw1_got,)) = _nt_norm_bwd(
        d_a1, [w1_g], dh2, h1, r2, norm2_g, True, "du2_norm2", tasks=[_RsChips(dw2_pair), _RsSibling(dw1_p)])
    dw1_pair = pair(dw1_p, dw1_got, "w1")
    dmerged = _dmerged_bwd(dh1b, w_out_full, "dmerged")
    dwout_p, _ = _weight_grad(merged, dh1b, 2, True, "dwout")
    dwout_p = dwout_p.reshape(N_DEV, D // N_DEV, D)
    (dproj_pool, dpool_full, d_scale), ((dwout_got,),) = _pool_bwd(
        dmerged, proj, y_pool, d_pool, pool_scale, pool_full, "pool_bwd", tasks=[_RsSibling(dwout_p)])
    dwout_pair = pair(dwout_p, dwout_got, "wout")
    (dproj, dwa_full, dwx_full, lru_small), ((dw1_chips,), (dwout_chips,)) = _lru_bwd(
        dmerged, proj, xc, r_gate, i_gate, a_gate, mult_gate, hs, lru_lambda, conv_full, wa_full, wx_full, dproj_pool,
        "lru_bwd", tasks=[_RsChips(dw1_pair), _RsChips(dwout_pair)])
    dwin_a, _ = _weight_grad(u, dproj, N_DEV, False, "dwin_a", part=(0, 2))
    dwin_b, ((dwin_a_got,),) = _weight_grad(u, dproj, N_DEV, False, "dwin_b", part=(1, 2), tasks=[_RsSibling(dwin_a)])
    dwin_a_pair = pair(dwin_a, dwin_a_got, "win_a")
    dpool_p = _slots_from_rows(dpool_full, G).astype(BF16)
    dwa_p = _slots_from_rows(dwa_full, H).astype(BF16)
    dwx_p = _slots_from_rows(dwx_full, H).astype(BF16)
    late = [dwin_b, dpool_p, dwa_p, dwx_p]
    n_tiles = Tp // _tile(Tp, TILE["nt"])
    n_first = max(1, n_tiles // 2)
    (dh0_a, d_norm1_a), ((dwin_a_chips,), *late_got) = _nt_norm_bwd(
        dproj, w_in_parts, dh1, h0, r1, norm1_g, False, "du_norm1_a",
        tasks=[_RsChips(dwin_a_pair)] + [_RsSibling(p) for p in late], tiles=(0, n_first))
    late_pair = [pair(p, g[0], "late%d" % k) for k, (p, g) in enumerate(zip(late, late_got))]
    (dh0, d_norm1_g), late_chips = _nt_norm_bwd(
        dproj, w_in_parts, dh1, h0, r1, norm1_g, False, "du_norm1_b", tasks=[_RsChips(p) for p in late_pair],
        tiles=(n_first, n_tiles - n_first), earlier=(dh0_a, d_norm1_a))
    grad_x = dh0[n_meta:T][None]

    pair_sums = [dw2_pair, dw1_pair, dwout_pair] + late_pair[1:]
    chip_sums = [dw2_chips, dw1_chips, dwout_chips] + [c[0] for c in late_chips[1:]]
    big = {}
    names = ["mlp_w2", "mlp_w1", "w_out", "pool_w", "gate_a_w", "gate_x_w"]
    trip = {"mlp_w2": (mlp_w2, m_mlp_w2, v_mlp_w2), "mlp_w1": (mlp_w1, m_mlp_w1, v_mlp_w1),
            "w_out": (w_out, m_w_out, v_w_out),
            "pool_w": (pool_w, m_pool_w, v_pool_w), "gate_a_w": (gate_a_w, m_gate_a_w, v_gate_a_w),
            "gate_x_w": (gate_x_w, m_gate_x_w, v_gate_x_w)}
    for k, nm in enumerate(names):
        w_, m_, v_ = trip[nm]
        shape2 = pair_sums[k].shape[1:]
        outs = _reduce_update(pair_sums[k], chip_sums[k], w_.reshape(shape2), m_.reshape(shape2), v_.reshape(shape2),
                              chip_slot, "update_" + nm)
        big[nm] = [o.reshape(w_.shape) for o in outs]
    win2 = [a_[0] for a_ in (w_in, m_w_in, v_w_in)]
    win_a = _reduce_update(dwin_a_pair, dwin_a_chips, *win2, chip_slot, "update_w_in_a", part=(0, 2))
    win_b = _reduce_update(late_pair[0], late_chips[0][0], *win2, chip_slot, "update_w_in_b", part=(1, 2), earlier=win_a)
    big["w_in"] = [o.reshape(w_in.shape) for o in win_b]
    names = names + ["w_in"]

    lru_rows = lru_small.transpose(1, 0, 2).reshape(LRU_SMALL_ROWS, D)
    small_part = jnp.concatenate(
        [dh0[:n_meta], d_norm1_g, d_scale, d_norm2_g, d_final_g, lru_rows, jnp.zeros((4, D), F32)], axis=0)
    (small_all,) = _all_gather([small_part], "gather_small_grads")
    small_sum = _sum_slots(small_all, "sum_small_grads")
    o = n_meta
    g_meta_full = small_sum[:o]
    g_norm1, g_scale, g_norm2, g_final = (small_sum[o + k:o + k + 1] for k in range(4))
    g_ba_full, g_bx_full, g_lam, g_cb = (small_sum[o + 4 + k:o + 5 + k] for k in range(4))
    g_cw_full = small_sum[o + 8:o + 8 + CONV_WIDTH]
    dcol = D // N_DEV
    g_meta = lax.dynamic_slice_in_dim(g_meta_full, me * dcol, dcol, axis=1)
    g_cw = lax.dynamic_slice_in_dim(g_cw_full, me * dcol, dcol, axis=1)
    hcol = hd // N_DEV
    g_ba = lax.dynamic_slice_in_dim(g_ba_full.reshape(H, hd), me * hcol, hcol, axis=1)
    g_bx = lax.dynamic_slice_in_dim(g_bx_full.reshape(H, hd), me * hcol, hcol, axis=1)

    rep_w = jnp.concatenate([norm1_g, pool_scale, conv_b, lru_lambda, norm2_g, final_g.reshape(1, D)], axis=0)
    rep_g = jnp.concatenate([g_norm1, g_scale, g_cb, g_lam, g_norm2, g_final], axis=0)
    rep_m = jnp.concatenate([m_norm1_g, m_pool_scale, m_conv_b, m_lru_lambda, m_norm2_g, m_final_g.reshape(1, D)], axis=0)
    rep_v = jnp.concatenate([v_norm1_g, v_pool_scale, v_conv_b, v_lru_lambda, v_norm2_g, v_final_g.reshape(1, D)], axis=0)
    rep_d, rep_nm, rep_nv = _small_update(rep_w, rep_g, rep_m, rep_v, "update_vectors")
    col_w = jnp.concatenate([meta_tokens, conv_w[0]], axis=0)
    col_g = jnp.concatenate([g_meta, g_cw], axis=0)
    col_m = jnp.concatenate([m_meta_tokens, m_conv_w[0]], axis=0)
    col_v = jnp.concatenate([v_meta_tokens, v_conv_w[0]], axis=0)
    col_d, col_nm, col_nv = _small_update(col_w, col_g, col_m, col_v, "update_columns")
    b_w = jnp.concatenate([gate_a_b[0], gate_x_b[0]], axis=0)
    b_g = jnp.concatenate([g_ba, g_bx], axis=0)
    b_m = jnp.concatenate([m_gate_a_b[0], m_gate_x_b[0]], axis=0)
    b_v = jnp.concatenate([v_gate_a_b[0], v_gate_x_b[0]], axis=0)
    b_d, b_nm, b_nv = _small_update(b_w, b_g, b_m, b_v, "update_biases")

    def rep(arr, k, like):
        return arr[k:k + 1].reshape(like.shape)

    rep_order = {"norm1_g": 0, "pool_scale": 1, "conv_b": 2, "lru_lambda": 3, "norm2_g": 4, "final_g": 5}
    like = {"norm1_g": norm1_g, "pool_scale": pool_scale, "conv_b": conv_b, "lru_lambda": lru_lambda,
            "norm2_g": norm2_g, "final_g": final_g}

    def leaves(kind):
        rep_src = [rep_g, rep_d, rep_nm, rep_nv][kind]
        col_src = [col_g, col_d, col_nm, col_nv][kind]
        b_src = [b_g, b_d, b_nm, b_nv][kind]
        out = {}
        out["meta_tokens"] = col_src[:n_meta]
        out["conv_w"] = col_src[n_meta:][None]
        out["gate_a_b"] = b_src[:H][None]
        out["gate_x_b"] = b_src[H:][None]
        for nm, k in rep_order.items():
            out[nm] = rep(rep_src, k, like[nm])
        for nm in names:
            out[nm] = big[nm][kind]
        order = ["meta_tokens", "norm1_g", "w_in", "pool_w", "pool_scale", "conv_w", "conv_b", "gate_a_w", "gate_a_b",
                 "gate_x_w", "gate_x_b", "lru_lambda", "w_out", "norm2_g", "mlp_w1", "mlp_w2", "final_g"]
        return [out[nm] for nm in order]

    loss = lax.psum(loss_tile[0, 0], ("x", "y", "c"))
    return (loss, grad_x, *leaves(0), *leaves(1), *leaves(2), *leaves(3))
```

```python
import functools

import jax
import jax.numpy as jnp
from jax import lax
from jax.experimental import pallas as pl
from jax.experimental.pallas import tpu as pltpu

F32 = jnp.float32
BF16 = jnp.bfloat16
MESH = pl.DeviceIdType.MESH
N_DEV = 8
POOL_WINDOWS = (2, 4, 8, 16)
MAX_WINDOW = 16
CONV_WIDTH = 4
HALO = 8
LRU_C = 8.0
NORM_EPS = 1e-6
ADAM_LR, ADAM_B1, ADAM_B2, ADAM_EPS, ADAM_WD, ADAM_STEP = 0.001, 0.9, 0.999, 1e-08, 0.01, 10
ROW_ALIGN = 128
VMEM_LIMIT = 56 << 20
TILE = dict(norm=384, proj=1408, pool=384, lru=704, wout=384, mlp1=1408, mlp2=704, dact=1408, tn=1408,
            nt=704, dmerged=704, update=256, pair=1024)
MLP2_K = 1024
EPILOGUE_ROWS = 176
SCAN_GROUP = 4
W_IN_CHUNKS = 5
MID_STEP_PERCENT = 88

_NT = (((1,), (1,)), ((), ()))
_TN = (((0,), (0,)), ((), ()))


def _call(body, **kw):
    return pl.pallas_call(body, **kw)


def _cp(*sem):
    return pltpu.CompilerParams(dimension_semantics=sem, vmem_limit_bytes=VMEM_LIMIT)


def _tile(total, pref):
    best = None
    for t in range(16, min(total, pref) + 1, 16):
        if total % t == 0:
            best = t
    assert best is not None, (total, pref)
    return best


def _sds(shape, dtype):
    return jax.ShapeDtypeStruct(shape, dtype)


def _pos():
    return lax.axis_index("x"), lax.axis_index("y"), lax.axis_index("c")


def _all_gather(shards, name):
    n = len(shards)

    def body(*refs):
        ins, outs = refs[:n], refs[n:2 * n]
        send_sems, recv_sems, local_sems = refs[2 * n:]
        x, y, c = _pos()
        me, sib = (x, y, c), (x, y, 1 - c)
        chips = [(1 - x, y), (x, 1 - y), (1 - x, 1 - y)]

        def slot(p):
            return 4 * p[0] + 2 * p[1] + p[2]

        def copy(a, k, block, to, src=None):
            dst = outs[a].at[slot(block)]
            return pltpu.make_async_remote_copy(
                src_ref=dst if src is None else src, dst_ref=dst,
                send_sem=send_sems.at[7 * a + k], recv_sem=recv_sems.at[7 * a + k],
                device_id=to, device_id_type=MESH)

        mine = [pltpu.make_async_copy(ins[a], outs[a].at[slot(me)], local_sems.at[a]) for a in range(n)]
        for m in mine:
            m.start()
        first = []
        for a in range(n):
            first.append(copy(a, 0, me, sib, src=ins[a]))
            first += [copy(a, 1 + j, me, (*chip, c), src=ins[a]) for j, chip in enumerate(chips)]
        for cp in first:
            cp.start()
        passed = []
        for a in range(n):
            for j, chip in enumerate(chips):
                copy(a, 1 + j, (*chip, c), me).wait_recv()
                fwd = copy(a, 4 + j, (*chip, c), sib)
                fwd.start()
                passed.append(fwd)
        for a in range(n):
            copy(a, 0, sib, me).wait_recv()
            for j, chip in enumerate(chips):
                copy(a, 4 + j, (*chip, 1 - c), me).wait_recv()
        for cp in first + passed:
            cp.wait_send()
        for m in mine:
            m.wait()

    hbm = pl.BlockSpec(memory_space=pl.ANY)
    return _call(
        body, name=name,
        out_shape=[_sds((N_DEV,) + s.shape, s.dtype) for s in shards],
        in_specs=[hbm] * n, out_specs=[hbm] * n,
        scratch_shapes=[pltpu.SemaphoreType.DMA((7 * n,)), pltpu.SemaphoreType.DMA((7 * n,)),
                        pltpu.SemaphoreType.DMA((n,))],
    )(*shards)


def _all_gather_relay(shards, chunks, name):
    n = len(shards)
    units = []
    for a, s in enumerate(shards):
        if chunks[a] == 1:
            units.append((a, None, None))
        else:
            w = s.shape[-1] // chunks[a]
            units += [(a, q * w, w) for q in range(chunks[a])]
    nu = len(units)

    def body(*refs):
        ins, outs = refs[:n], refs[n:2 * n]
        send_sems, recv_sems, local_sems = refs[2 * n:]
        x, y, c = _pos()
        me, sib = (x, y, c), (x, y, 1 - c)
        x_nbr, y_nbr, diag = (1 - x, y, c), (x, 1 - y, c), (1 - x, 1 - y, c)
        came_from = (x + (1 - c) * (1 - 2 * x), y + c * (1 - 2 * y), c)
        pass_to = (x + c * (1 - 2 * x), y + (1 - c) * (1 - 2 * y), c)

        def slot(p):
            return 4 * p[0] + 2 * p[1] + p[2]

        def src_view(u):
            a, c0, w = units[u]
            return ins[a] if c0 is None else ins[a].at[:, pl.ds(c0, w)]

        def dst_view(u, p):
            a, c0, w = units[u]
            return outs[a].at[slot(p)] if c0 is None else outs[a].at[slot(p), :, pl.ds(c0, w)]

        def copy(u, k, block, to, from_shard=False):
            dst = dst_view(u, block)
            return pltpu.make_async_remote_copy(
                src_ref=src_view(u) if from_shard else dst, dst_ref=dst,
                send_sem=send_sems.at[7 * u + k], recv_sem=recv_sems.at[7 * u + k],
                device_id=to, device_id_type=MESH)

        mine = [pltpu.make_async_copy(src_view(u), dst_view(u, me), local_sems.at[u]) for u in range(nu)]
        for m in mine:
            m.start()
        sent = []
        for u in range(nu):
            sent += [copy(u, 0, me, sib, True), copy(u, 1, me, x_nbr, True), copy(u, 2, me, y_nbr, True)]
        for cp in sent:
            cp.start()
        for u in range(nu):
            copy(u, 1, x_nbr, me).wait_recv()
            copy(u, 2, y_nbr, me).wait_recv()
            sent += [copy(u, 3, came_from, pass_to), copy(u, 4, x_nbr, sib), copy(u, 5, y_nbr, sib)]
            for cp in sent[-3:]:
                cp.start()
        for u in range(nu):
            copy(u, 3, diag, me).wait_recv()
            sent.append(copy(u, 6, diag, sib))
            sent[-1].start()
        for u in range(nu):
            copy(u, 0, sib, me).wait_recv()
            for k, p in ((4, x_nbr), (5, y_nbr), (6, diag)):
                copy(u, k, (p[0], p[1], 1 - c), me).wait_recv()
        for cp in sent:
            cp.wait_send()
        for m in mine:
            m.wait()

    hbm = pl.BlockSpec(memory_space=pl.ANY)
    return _call(
        body, name=name,
        out_shape=[_sds((N_DEV,) + s.shape, s.dtype) for s in shards],
        in_specs=[hbm] * n, out_specs=[hbm] * n,
        scratch_shapes=[pltpu.SemaphoreType.DMA((7 * nu,)), pltpu.SemaphoreType.DMA((7 * nu,)),
                        pltpu.SemaphoreType.DMA((nu,))],
    )(*shards)


def _other_chips(x, y):
    return [(1 - x, y), (x, 1 - y), (1 - x, 1 - y)]


class _AgFull:
    n_sem, n_local = 7, 1

    def __init__(self, shard, forward_here=True):
        self.forward_here = forward_here
        self.ins = [shard]
        self.out_shapes = [_sds((N_DEV,) + shard.shape, shard.dtype)]
        self.aliases = []

    def _peers(self):
        x, y, c = _pos()
        return [(x, y, 1 - c)] + [(*chip, c) for chip in _other_chips(x, y)]

    def _sends(self, ins, outs, sems):
        send, recv, _, base, _ = sems
        x, y, c = _pos()
        mine = outs[0].at[4 * x + 2 * y + c]
        return [pltpu.make_async_remote_copy(src_ref=ins[0], dst_ref=mine, send_sem=send.at[base + k],
                                             recv_sem=recv.at[base + k], device_id=p, device_id_type=MESH)
                for k, p in enumerate(self._peers())]

    def _arrivals(self, outs, sems):
        send, recv, _, base, _ = sems
        res = []
        for k, p in enumerate(self._peers()):
            blk = outs[0].at[4 * p[0] + 2 * p[1] + p[2]]
            res.append(pltpu.make_async_remote_copy(src_ref=blk, dst_ref=blk, send_sem=send.at[base + k],
                                                    recv_sem=recv.at[base + k], device_id=p, device_id_type=MESH))
        return res

    def _own(self, ins, outs, sems):
        x, y, c = _pos()
        return pltpu.make_async_copy(ins[0], outs[0].at[4 * x + 2 * y + c], sems[2].at[sems[4]])

    def _forwards(self, outs, sems, core_of_block):
        return _forward_copies(outs[0], sems[0], sems[1], sems[3] + 4, core_of_block)

    def start(self, ins, outs, sems):
        self._own(ins, outs, sems).start()
        for cp in self._sends(ins, outs, sems):
            cp.start()

    def mid(self, ins, outs, sems):
        if self.forward_here:
            for cp in self._arrivals(outs, sems)[1:]:
                cp.wait_recv()
            for cp in self._forwards(outs, sems, "mine"):
                cp.start()

    def finish(self, ins, outs, sems):
        if self.forward_here:
            self._arrivals(outs, sems)[0].wait_recv()
            for cp in self._forwards(outs, sems, "sibling"):
                cp.wait_recv()
            for cp in self._sends(ins, outs, sems) + self._forwards(outs, sems, "mine"):
                cp.wait_send()
        else:
            for cp in self._arrivals(outs, sems):
                cp.wait_recv()
            for cp in self._sends(ins, outs, sems):
                cp.wait_send()
        self._own(ins, outs, sems).wait()


def _forward_copies(gathered_ref, send, recv, base, core_of_block):
    x, y, c = _pos()
    res = []
    for k, chip in enumerate(_other_chips(x, y)):
        blk = gathered_ref.at[4 * chip[0] + 2 * chip[1] + (c if core_of_block == "mine" else 1 - c)]
        res.append(pltpu.make_async_remote_copy(src_ref=blk, dst_ref=blk, send_sem=send.at[base + k],
                                                recv_sem=recv.at[base + k], device_id=(x, y, 1 - c),
                                                device_id_type=MESH))
    return res


class _AgForward:
    n_sem, n_local = 3, 0

    def __init__(self, gathered):
        self.ins = [gathered]
        self.out_shapes = [_sds(gathered.shape, gathered.dtype)]
        self.aliases = [(0, 0)]

    def start(self, ins, outs, sems):
        for cp in _forward_copies(outs[0], sems[0], sems[1], sems[3], "mine"):
            cp.start()

    def finish(self, ins, outs, sems):
        for cp in _forward_copies(outs[0], sems[0], sems[1], sems[3], "sibling"):
            cp.wait_recv()
        for cp in _forward_copies(outs[0], sems[0], sems[1], sems[3], "mine"):
            cp.wait_send()


class _RsSibling:
    n_sem, n_local = 4, 0

    def __init__(self, part):
        self.ins = [part]
        self.out_shapes = [_sds((4,) + part.shape[1:], part.dtype)]
        self.aliases = []

    def _copies(self, ins, outs, sems):
        send, recv, _, base, _ = sems
        x, y, c = _pos()
        return [pltpu.make_async_remote_copy(src_ref=ins[0].at[2 * q + (1 - c)], dst_ref=outs[0].at[q],
                                             send_sem=send.at[base + q], recv_sem=recv.at[base + q],
                                             device_id=(x, y, 1 - c), device_id_type=MESH) for q in range(4)]

    def start(self, ins, outs, sems):
        for cp in self._copies(ins, outs, sems):
            cp.start()

    def finish(self, ins, outs, sems):
        for cp in self._copies(ins, outs, sems):
            cp.wait()


class _RsChips:
    n_sem, n_local = 3, 0

    def __init__(self, pair):
        self.ins = [pair]
        self.out_shapes = [_sds((3,) + pair.shape[1:], pair.dtype)]
        self.aliases = []

    def _copies(self, ins, outs, sems):
        send, recv, _, base, _ = sems
        x, y, c = _pos()
        return [pltpu.make_async_remote_copy(src_ref=ins[0].at[2 * chip[0] + chip[1]], dst_ref=outs[0].at[k],
                                             send_sem=send.at[base + k], recv_sem=recv.at[base + k],
                                             device_id=(*chip, c), device_id_type=MESH)
                for k, chip in enumerate(_other_chips(x, y))]

    def start(self, ins, outs, sems):
        for cp in self._copies(ins, outs, sems):
            cp.start()

    def finish(self, ins, outs, sems):
        for cp in self._copies(ins, outs, sems):
            cp.wait()


def _hosted(body, tasks, *, grid, in_specs, out_specs, out_shape, scratch_shapes=(), name, semantics, operands,
            aliases=None):
    in_specs, out_specs, out_shape = list(in_specs), list(out_specs), list(out_shape)
    scratch_shapes = list(scratch_shapes)
    aliases = dict(aliases or {})
    if not tasks:
        res = _call(body, name=name, grid=grid, in_specs=in_specs, out_specs=out_specs, out_shape=out_shape,
                    scratch_shapes=scratch_shapes, input_output_aliases=aliases,
                    compiler_params=_cp(*semantics))(*operands)
        return list(res), []
    n_in, n_out, n_scr = len(in_specs), len(out_specs), len(scratch_shapes)
    t_ins = [a for t in tasks for a in t.ins]
    t_outs = [o for t in tasks for o in t.out_shapes]
    i0, o0 = n_in, n_out
    for t in tasks:
        for (i, o) in t.aliases:
            aliases[i0 + i] = o0 + o
        i0 += len(t.ins)
        o0 += len(t.out_shapes)
    n_sem = sum(t.n_sem for t in tasks)
    n_local = max(1, sum(t.n_local for t in tasks))
    n_steps = 1
    for g in grid:
        n_steps *= g
    mid_step = min(n_steps - 1, (n_steps * MID_STEP_PERCENT) // 100)

    def wrapped(*refs):
        cut = [n_in, len(t_ins), n_out, len(t_outs), n_scr]
        parts, at = [], 0
        for n in cut:
            parts.append(refs[at:at + n])
            at += n
        ins, tin, outs, tout, scratch = parts
        send, recv, local = refs[at:]
        step = pl.program_id(0)
        for d in range(1, len(grid)):
            step = step * grid[d] + pl.program_id(d)

        def each(method):
            i, o, s, l = 0, 0, 0, 0
            for t in tasks:
                if hasattr(t, method):
                    getattr(t, method)(tin[i:i + len(t.ins)], tout[o:o + len(t.out_shapes)], (send, recv, local, s, l))
                i, o, s, l = i + len(t.ins), o + len(t.out_shapes), s + t.n_sem, l + t.n_local

        @pl.when(step == 0)
        def _():
            each("start")

        body(*ins, *outs, *scratch)

        @pl.when(step == mid_step)
        def _():
            each("mid")

        @pl.when(step == n_steps - 1)
        def _():
            each("finish")

    hbm = pl.BlockSpec(memory_space=pl.ANY)
    res = _call(
        wrapped, name=name, grid=grid,
        in_specs=in_specs + [hbm] * len(t_ins), out_specs=out_specs + [hbm] * len(t_outs),
        out_shape=out_shape + t_outs,
        scratch_shapes=scratch_shapes + [pltpu.SemaphoreType.DMA((n_sem,)), pltpu.SemaphoreType.DMA((n_sem,)),
                                         pltpu.SemaphoreType.DMA((n_local,))],
        input_output_aliases=aliases,
        compiler_params=_cp(*(["arbitrary"] * len(grid))),
    )(*operands, *t_ins)
    res = list(res)
    task_outs, o = [], n_out
    for t in tasks:
        task_outs.append(res[o:o + len(t.out_shapes)])
        o += len(t.out_shapes)
    return res[:n_out], task_outs


def _pair_sum(part, got, core, name):
    _, R, C = part.shape
    tr = _tile(R, TILE["pair"]) if R % 16 == 0 else R

    def body(core_ref, p_ref, g_ref, o_ref):
        o_ref[...] = (p_ref[...].astype(F32) + g_ref[...].astype(F32)).astype(o_ref.dtype)

    return _call(
        body, name=name,
        grid_spec=pltpu.PrefetchScalarGridSpec(
            num_scalar_prefetch=1, grid=(4, R // tr),
            in_specs=[pl.BlockSpec((None, tr, C), lambda q, i, cr: (2 * q + cr[0], i, 0)),
                      pl.BlockSpec((None, tr, C), lambda q, i, cr: (q, i, 0))],
            out_specs=pl.BlockSpec((None, tr, C), lambda q, i, cr: (q, i, 0))),
        out_shape=_sds((4, R, C), part.dtype),
        compiler_params=_cp("parallel", "parallel"),
    )(core, part, got)


def _sum_slots(gathered, name):
    _, R, C = gathered.shape

    def body(g_ref, o_ref):
        acc = g_ref[0]
        for s in range(1, N_DEV):
            acc = acc + g_ref[s]
        o_ref[...] = acc

    return _call(body, name=name, out_shape=_sds((R, C), F32))(gathered)


def _sigmoid(z):
    return jax.nn.sigmoid(z)


def _softplus(z):
    e = jnp.exp(-jnp.abs(z))
    log1p_e = jnp.where(e < 0.01, e * (1.0 - e * (0.5 - e * (1.0 / 3.0))), jnp.log(1.0 + e))
    return jnp.maximum(z, 0.0) + log1p_e


_GELU_K = 0.7978845608028654
_GELU_C = 0.044715


def _gelu_and_grad(z):
    t = jnp.tanh(_GELU_K * (z + _GELU_C * z * z * z))
    g = 0.5 * z * (1.0 + t)
    dg = 0.5 * (1.0 + t) + 0.5 * z * (1.0 - t * t) * _GELU_K * (1.0 + 3.0 * _GELU_C * z * z)
    return g, dg


def _gelu(z):
    t = jnp.tanh(_GELU_K * (z + _GELU_C * z * z * z))
    return 0.5 * z * (1.0 + t)


def _row_ids(tile_index, tm, width=1):
    return tile_index * tm + lax.broadcasted_iota(jnp.int32, (tm, width), 0)


def _shift_down(prev, cur, k):
    if k == 0:
        return cur
    ext = jnp.concatenate([prev, cur], axis=0)
    return pltpu.roll(ext, k, axis=0)[prev.shape[0]:]


def _shift_up(cur, nxt, k):
    if k == 0:
        return cur
    ext = jnp.concatenate([cur, nxt], axis=0)
    return pltpu.roll(ext, ext.shape[0] - k, axis=0)[:cur.shape[0]]


def _lru_gates(r, sp):
    z = LRU_C * r * sp
    a = jnp.exp(-z)
    t = jnp.tanh(z)
    mult = jnp.sqrt(2.0 * t / (1.0 + t))
    return a, mult


def _scan_chunks(a_ref, b_ref, out_ref, carry, n_rows, reverse):
    n_chunks = n_rows // 8
    cols = a_ref.shape[1]
    rid = lax.broadcasted_iota(jnp.int32, (8, cols), 0)
    edge = 0 if reverse else 7
    group = SCAN_GROUP if n_chunks % SCAN_GROUP == 0 else 1

    def local_scan(ci):
        rows = pl.ds(pl.multiple_of(ci * 8, 8), 8)
        a = a_ref[rows, :]
        b = b_ref[rows, :]
        for s in (1, 2, 4):
            if reverse:
                keep = rid < 8 - s
                a_n, b_n = pltpu.roll(a, 8 - s, axis=0), pltpu.roll(b, 8 - s, axis=0)
            else:
                keep = rid >= s
                a_n, b_n = pltpu.roll(a, s, axis=0), pltpu.roll(b, s, axis=0)
            b = a * jnp.where(keep, b_n, 0.0) + b
            a = a * jnp.where(keep, a_n, 1.0)
        a_e = jnp.sum(jnp.where(rid == edge, a, 0.0), axis=0, keepdims=True)
        b_e = jnp.sum(jnp.where(rid == edge, b, 0.0), axis=0, keepdims=True)
        return rows, a, b, a_e, b_e

    def chunks(k, h):
        first = k * group
        scans = [local_scan((n_chunks - 1 - (first + j)) if reverse else first + j) for j in range(group)]
        for rows, a, b, a_e, b_e in scans:
            out_ref[rows, :] = a * h + b
            h = a_e * h + b_e
        return h

    return lax.fori_loop(0, n_chunks // group, chunks, carry)


def _norm_fwd(h, g, name):
    Tp, D = h.shape
    tm = _tile(Tp, TILE["norm"])

    def body(h_ref, g_ref, u_ref, r_ref):
        x = h_ref[...]
        r = lax.rsqrt(jnp.mean(x * x, axis=-1, keepdims=True) + NORM_EPS)
        u_ref[...] = (x * r * g_ref[...]).astype(BF16)
        r_ref[...] = r

    return _call(
        body, name=name, grid=(Tp // tm,),
        in_specs=[pl.BlockSpec((tm, D), lambda i: (i, 0)), pl.BlockSpec((1, D), lambda i: (0, 0))],
        out_specs=[pl.BlockSpec((tm, D), lambda i: (i, 0)), pl.BlockSpec((tm, 1), lambda i: (i, 0))],
        out_shape=[_sds((Tp, D), BF16), _sds((Tp, 1), F32)],
        compiler_params=_cp("parallel"),
    )(h, g)


def _proj_fwd(u, w_slots, name, tasks=(), part=(0, 1), earlier=None):
    Tp, K = u.shape
    S, _, n = w_slots.shape
    p, parts = part
    tm = _tile(Tp, TILE["proj"])

    def body(a_ref, b_ref, *rest):
        o_ref = rest[-1]
        o_ref[...] = jnp.dot(a_ref[...], b_ref[...], preferred_element_type=F32)

    in_specs = [pl.BlockSpec((tm, K), lambda j, i: (i, 0)), pl.BlockSpec((None, K, n), lambda j, i: (j, 0, 0))]
    operands = (u, w_slots)
    aliases = {}
    if earlier is not None:
        in_specs.append(pl.BlockSpec(memory_space=pl.ANY))
        operands += (earlier,)
        aliases = {2: 0}
    (proj,), extra = _hosted(
        body, tasks, name=name, grid=(S, Tp // tm), in_specs=in_specs,
        out_specs=[pl.BlockSpec((tm, n), lambda j, i: (i, j * parts + p))],
        out_shape=[_sds((Tp, S * parts * n), F32)],
        semantics=("parallel", "parallel"), operands=operands, aliases=aliases)
    return proj, extra


def _mlp1_fwd(u2, w_slots, name, tasks=()):
    Tp, K = u2.shape
    S, _, n = w_slots.shape
    tm = _tile(Tp, TILE["mlp1"])

    def body(a_ref, b_ref, act_ref, a1_ref):
        a1 = jnp.dot(a_ref[...], b_ref[...], preferred_element_type=F32)
        relu = jnp.maximum(a1, 0.0)
        act_ref[...] = (relu * relu).astype(BF16)
        a1_ref[...] = a1.astype(BF16)

    return _hosted(
        body, tasks, name=name, grid=(S, Tp // tm),
        in_specs=[pl.BlockSpec((tm, K), lambda j, i: (i, 0)), pl.BlockSpec((None, K, n), lambda j, i: (j, 0, 0))],
        out_specs=[pl.BlockSpec((tm, n), lambda j, i: (i, j))] * 2,
        out_shape=[_sds((Tp, S * n), BF16)] * 2,
        semantics=("parallel", "parallel"), operands=(u2, w_slots))


def _pool_fwd(proj, pool_w, name):
    Tp = proj.shape[0]
    G, Cg, _ = pool_w.shape
    D = G * Cg
    tm = _tile(Tp, TILE["pool"])

    def body(v_ref, w_ref, d_ref, y_ref, prev_ref):
        t = pl.program_id(0)

        @pl.when(t == 0)
        def _():
            prev_ref[...] = jnp.zeros_like(prev_ref)

        rows = _row_ids(t, tm)
        for g, win in enumerate(POOL_WINDOWS):
            cols = slice(g * Cg, (g + 1) * Cg)
            v = v_ref[:, cols]
            s = jnp.concatenate([prev_ref[:, cols], v], axis=0)
            k = 1
            while k < win:
                s = s + pltpu.roll(s, k, axis=0)
                k *= 2
            cnt = jnp.minimum(rows + 1, win).astype(F32)
            d = s[MAX_WINDOW:] / cnt - v
            d_ref[:, cols] = d.astype(BF16)
            y_ref[:, cols] = jnp.dot(d.astype(BF16), w_ref[g], preferred_element_type=F32)
        prev_ref[...] = v_ref[tm - MAX_WINDOW:, :]

    return _call(
        body, name=name, grid=(Tp // tm,),
        in_specs=[pl.BlockSpec((tm, D), lambda t: (t, 0)), pl.BlockSpec((G, Cg, Cg), lambda t: (0, 0, 0))],
        out_specs=[pl.BlockSpec((tm, D), lambda t: (t, 0))] * 2,
        out_shape=[_sds((Tp, D), BF16), _sds((Tp, D), F32)],
        scratch_shapes=[pltpu.VMEM((MAX_WINDOW, D), F32)],
        compiler_params=_cp("arbitrary"),
    )(proj, pool_w)


def _lru_fwd(proj, y_pool, scale, conv_w, conv_b, wa, ba, wx, bx, lam, name, tasks=()):
    Tp = proj.shape[0]
    H, hd, _ = wa.shape
    D = H * hd
    tm = _tile(Tp, TILE["lru"])
    nb = D // hd

    def body(vl_ref, vg_ref, gp_ref, gl_ref, y_ref, sc_ref, cw_ref, cb_ref, wa_ref, ba_ref, wx_ref, bx_ref,
             lam_ref, xc_ref, r_ref, i_ref, a_ref, mult_ref, hs_ref, m_ref, prev_ref, carry_ref, b_s):
        t = pl.program_id(1)

        @pl.when(t == 0)
        def _():
            prev_ref[...] = jnp.zeros_like(prev_ref)
            carry_ref[...] = jnp.zeros_like(carry_ref)

        v = vl_ref[...]
        prev = prev_ref[...]
        xc = jnp.zeros_like(v) + cb_ref[...]
        for k in range(CONV_WIDTH):
            xc = xc + cw_ref[k:k + 1, :] * _shift_down(prev, v, CONV_WIDTH - 1 - k)
        prev_ref[...] = v[tm - HALO:, :]
        xcb = xc.astype(BF16)
        r = _sigmoid(jnp.dot(xcb, wa_ref[...], preferred_element_type=F32) + ba_ref[...])
        i = _sigmoid(jnp.dot(xcb, wx_ref[...], preferred_element_type=F32) + bx_ref[...])
        a, mult = _lru_gates(r, _softplus(-lam_ref[...]))
        a_ref[...] = a
        mult_ref[...] = mult
        b_s[...] = mult * (i * xc)
        xc_ref[...] = xc
        r_ref[...] = r
        i_ref[...] = i
        carry_ref[0:1, :] = _scan_chunks(a_ref, b_s, hs_ref, carry_ref[0:1, :], tm, reverse=False)
        lru_out = hs_ref[...] * _gelu(vg_ref[...])
        pool_out = y_ref[...] * sc_ref[...]
        m_ref[...] = (_sigmoid(gp_ref[...]) * pool_out + _sigmoid(gl_ref[...]) * lru_out).astype(BF16)

    def piece(p):
        return pl.BlockSpec((tm, hd), lambda h, t: (t, p * nb + h))

    blk = pl.BlockSpec((tm, hd), lambda h, t: (t, h))
    vec = pl.BlockSpec((1, hd), lambda h, t: (0, h))
    mat = pl.BlockSpec((None, hd, hd), lambda h, t: (h, 0, 0))
    bias = pl.BlockSpec((None, 1, hd), lambda h, t: (h, 0, 0))
    return _hosted(
        body, tasks, name=name, grid=(H, Tp // tm),
        in_specs=[piece(1), piece(2), piece(3), piece(4), blk, vec,
                  pl.BlockSpec((CONV_WIDTH, hd), lambda h, t: (0, h)), vec, mat, bias, mat, bias, vec],
        out_specs=[blk] * 7,
        out_shape=[_sds((Tp, D), F32)] * 6 + [_sds((Tp, D), BF16)],
        scratch_shapes=[pltpu.VMEM((HALO, hd), F32), pltpu.VMEM((8, hd), F32), pltpu.VMEM((tm, hd), F32)],
        semantics=("parallel", "arbitrary"),
        operands=(proj, proj, proj, proj, y_pool, scale, conv_w, conv_b, wa, ba.reshape(H, 1, hd), wx,
                  bx.reshape(H, 1, hd), lam))


def _wout_norm_fwd(merged, w_out, h0, g2, name, tasks=()):
    Tp, D = h0.shape
    tm = _tile(Tp, TILE["wout"])

    def body(m_ref, w_ref, h0_ref, g_ref, h1_ref, u2_ref, r2_ref):
        h1 = h0_ref[...] + jnp.dot(m_ref[...], w_ref[...], preferred_element_type=F32)
        r = lax.rsqrt(jnp.mean(h1 * h1, axis=-1, keepdims=True) + NORM_EPS)
        h1_ref[...] = h1
        u2_ref[...] = (h1 * r * g_ref[...]).astype(BF16)
        r2_ref[...] = r

    row = pl.BlockSpec((tm, D), lambda i: (i, 0))
    return _hosted(
        body, tasks, name=name, grid=(Tp // tm,),
        in_specs=[row, pl.BlockSpec((D, D), lambda i: (0, 0)), row, pl.BlockSpec((1, D), lambda i: (0, 0))],
        out_specs=[row, row, pl.BlockSpec((tm, 1), lambda i: (i, 0))],
        out_shape=[_sds((Tp, D), F32), _sds((Tp, D), BF16), _sds((Tp, 1), F32)],
        semantics=("parallel",), operands=(merged, w_out, h0, g2))


def _mlp2_loss(act, w2, h1, target, gf, n_meta, seq, name):
    Tp, D = h1.shape
    K = act.shape[1]
    tm = _tile(Tp, TILE["mlp2"])
    tk = min(K, MLP2_K)
    nk = K // tk

    rc = _tile(tm, EPILOGUE_ROWS)
    nt = Tp // tm
    last_rows = n_meta + seq - (nt - 1) * tm
    assert nt >= 2 and n_meta % 8 == 0 and 0 < last_rows <= tm and last_rows % 8 == 0

    def body(a_ref, w_ref, h1_hbm, t_hbm, g_ref, dh_ref, dhb_ref, loss_ref, dg_ref, h1_buf, t_buf, sems):
        i, k = pl.program_id(0), pl.program_id(1)
        tile_rows = pl.ds(pl.multiple_of(i * tm, tm), tm)
        fetch_h1 = pltpu.make_async_copy(h1_hbm.at[tile_rows, :], h1_buf, sems.at[0])
        fetch_t = [
            (i == 0, pltpu.make_async_copy(t_hbm.at[pl.ds(0, tm - n_meta), :], t_buf.at[pl.ds(n_meta, tm - n_meta), :],
                                           sems.at[1])),
            ((i > 0) & (i < nt - 1),
             pltpu.make_async_copy(t_hbm.at[pl.ds(pl.multiple_of(i * tm - n_meta, 8), tm), :], t_buf, sems.at[1])),
            (i == nt - 1, pltpu.make_async_copy(t_hbm.at[pl.ds((nt - 1) * tm - n_meta, last_rows), :],
                                                t_buf.at[pl.ds(0, last_rows), :], sems.at[1])),
        ]

        @pl.when(k == 0)
        def _():
            fetch_h1.start()
            dh_ref[...] = jnp.zeros_like(dh_ref)

        for cond, f in fetch_t:
            @pl.when(cond & (k == 0))
            def _(f=f):
                f.start()

        @pl.when((i == 0) & (k == 0))
        def _():
            loss_ref[...] = jnp.zeros_like(loss_ref)
            dg_ref[...] = jnp.zeros_like(dg_ref)
            t_buf[0:n_meta, :] = jnp.zeros((n_meta, D), F32)

        dh_ref[...] += jnp.dot(a_ref[...], w_ref[...], preferred_element_type=F32)

        for cond, f in fetch_t:
            @pl.when(cond & (k == nk - 1))
            def _(f=f):
                f.wait()

        @pl.when(k == nk - 1)
        def _():
            fetch_h1.wait()
            g = g_ref[...]

            def chunk(c, carry):
                loss_acc, dg_acc = carry
                rows = pl.ds(pl.multiple_of(c * rc, rc), rc)
                h2 = h1_buf[rows, :] + dh_ref[rows, :]
                r = lax.rsqrt(jnp.mean(h2 * h2, axis=-1, keepdims=True) + NORM_EPS)
                out = h2 * r * g
                row_id = i * tm + c * rc + lax.broadcasted_iota(jnp.int32, (rc, 1), 0)
                valid = (row_id >= n_meta) & (row_id < n_meta + seq)
                diff = jnp.where(valid, out - t_buf[rows, :], 0.0)
                dout = diff / D
                dog = dout * g
                dh = r * dog - h2 * (r * r * r * jnp.mean(dog * h2, axis=-1, keepdims=True))
                dh_ref[rows, :] = dh
                dhb_ref[rows, :] = dh.astype(BF16)
                loss_acc = loss_acc + 0.5 * jnp.sum(jnp.mean(diff * diff, axis=-1, keepdims=True), axis=0, keepdims=True)
                return loss_acc, dg_acc + jnp.sum(dout * (h2 * r), axis=0, keepdims=True)

            loss_sum, dg_sum = lax.fori_loop(0, tm // rc, chunk, (jnp.zeros((1, 1), F32), jnp.zeros((1, D), F32)))
            loss_ref[...] += loss_sum
            dg_ref[...] += dg_sum

    row = pl.BlockSpec((tm, D), lambda i, k: (i, 0))
    hbm = pl.BlockSpec(memory_space=pl.ANY)
    return _call(
        body, name=name, grid=(Tp // tm, nk),
        in_specs=[pl.BlockSpec((tm, tk), lambda i, k: (i, k)), pl.BlockSpec((tk, D), lambda i, k: (k, 0)),
                  hbm, hbm, pl.BlockSpec((1, D), lambda i, k: (0, 0))],
        out_specs=[row, row, pl.BlockSpec((8, 128), lambda i, k: (0, 0)), pl.BlockSpec((1, D), lambda i, k: (0, 0))],
        out_shape=[_sds((Tp, D), F32), _sds((Tp, D), BF16), _sds((8, 128), F32), _sds((1, D), F32)],
        scratch_shapes=[pltpu.VMEM((tm, D), F32), pltpu.VMEM((tm, D), F32), pltpu.SemaphoreType.DMA((2,))],
        compiler_params=_cp("arbitrary", "arbitrary"),
    )(act, w2, h1, target, gf)


def _dact_bwd(dh2b, w2_slots, a1, name):
    Tp, D = dh2b.shape
    S, n, _ = w2_slots.shape
    tm = _tile(Tp, TILE["dact"])

    def body(g_ref, w_ref, a1_ref, o_ref):
        dact = lax.dot_general(g_ref[...], w_ref[...], _NT, preferred_element_type=F32)
        o_ref[...] = (dact * (2.0 * jnp.maximum(a1_ref[...].astype(F32), 0.0))).astype(BF16)

    return _call(
        body, name=name, grid=(S, Tp // tm),
        in_specs=[pl.BlockSpec((tm, D), lambda j, i: (i, 0)), pl.BlockSpec((None, n, D), lambda j, i: (j, 0, 0)),
                  pl.BlockSpec((tm, n), lambda j, i: (i, j))],
        out_specs=pl.BlockSpec((tm, n), lambda j, i: (i, j)),
        out_shape=_sds((Tp, S * n), BF16),
        compiler_params=_cp("parallel", "parallel"),
    )(dh2b, w2_slots, a1)


def _weight_grad(a, g, blocks, block_a, name, tasks=(), part=(0, 1)):
    Tp, Ka = a.shape
    Ng = g.shape[1]
    p, parts = part
    assert parts == 1 or not block_a
    ka = Ka // blocks if block_a else Ka // parts
    ng = Ng if block_a else Ng // blocks
    tt = _tile(Tp, TILE["tn"])
    nt = Tp // tt

    def body(a_ref, g_ref, o_ref, acc_ref):
        t = pl.program_id(1)

        @pl.when(t == 0)
        def _():
            acc_ref[...] = jnp.zeros_like(acc_ref)

        acc_ref[...] += lax.dot_general(a_ref[...], g_ref[...], _TN, preferred_element_type=F32)

        @pl.when(t == nt - 1)
        def _():
            o_ref[...] = acc_ref[...].astype(o_ref.dtype)

    if block_a:
        a_spec = pl.BlockSpec((tt, ka), lambda j, t: (t, j))
        g_spec = pl.BlockSpec((tt, ng), lambda j, t: (t, 0))
    else:
        a_spec = pl.BlockSpec((tt, ka), lambda j, t: (t, p))
        g_spec = pl.BlockSpec((tt, ng), lambda j, t: (t, j))
    (dw,), extra = _hosted(
        body, tasks, name=name, grid=(blocks, nt),
        in_specs=[a_spec, g_spec],
        out_specs=[pl.BlockSpec((None, ka, ng), lambda j, t: (j, 0, 0))],
        out_shape=[_sds((blocks, ka, ng), BF16)],
        scratch_shapes=[pltpu.VMEM((ka, ng), F32)],
        semantics=("parallel", "arbitrary"), operands=(a, g))
    return dw, extra


def _nt_norm_bwd(dz, w_parts, dres, hin, rin, g, want_bf16, name, tasks=(), tiles=None, earlier=None):
    Tp, D = hin.shape
    P = len(w_parts)
    S, _, n = w_parts[0].shape
    K = S * P
    tm = _tile(Tp, TILE["nt"])
    t0, nt = tiles if tiles is not None else (0, Tp // tm)
    assert not (want_bf16 and earlier is not None)

    rc = _tile(tm, EPILOGUE_ROWS)

    def body(dz_ref, *rest):
        w_refs, (dres_hbm, h_hbm, r_ref, g_ref), rest = rest[:P], rest[P:P + 4], rest[P + 4:]
        if earlier is not None:
            _, dg0_ref, dh_ref, dg_ref, dres_buf, h_buf, sems = rest
        elif want_bf16:
            dh_ref, dhb_ref, dg_ref, dres_buf, h_buf, sems = rest
        else:
            dh_ref, dg_ref, dres_buf, h_buf, sems = rest
        i, k = pl.program_id(0), pl.program_id(1)
        tile_rows = pl.ds(pl.multiple_of((t0 + i) * tm, tm), tm)
        fetch = [pltpu.make_async_copy(dres_hbm.at[tile_rows, :], dres_buf, sems.at[0]),
                 pltpu.make_async_copy(h_hbm.at[tile_rows, :], h_buf, sems.at[1])]

        @pl.when(k == 0)
        def _():
            for f in fetch:
                f.start()
            dh_ref[...] = jnp.zeros_like(dh_ref)

        @pl.when((i == 0) & (k == 0))
        def _():
            dg_ref[...] = jnp.zeros_like(dg_ref) if earlier is None else dg0_ref[...]

        for q in range(P):
            @pl.when(k % P == q)
            def _(q=q):
                dh_ref[...] += lax.dot_general(dz_ref[...], w_refs[q][...], _NT, preferred_element_type=F32)

        @pl.when(k == K - 1)
        def _():
            for f in fetch:
                f.wait()
            g = g_ref[...]

            def chunk(c, dg_acc):
                rows = pl.ds(pl.multiple_of(c * rc, rc), rc)
                du = dh_ref[rows, :]
                h = h_buf[rows, :]
                r = r_ref[rows, :]
                dug = du * g
                dh = dres_buf[rows, :] + r * dug - h * (r * r * r * jnp.mean(dug * h, axis=-1, keepdims=True))
                dh_ref[rows, :] = dh
                if want_bf16:
                    dhb_ref[rows, :] = dh.astype(BF16)
                return dg_acc + jnp.sum(du * (h * r), axis=0, keepdims=True)

            dg_ref[...] += lax.fori_loop(0, tm // rc, chunk, jnp.zeros((1, D), F32))

    row = pl.BlockSpec((tm, D), lambda i, k: (t0 + i, 0))
    vec = pl.BlockSpec((1, D), lambda i, k: (0, 0))
    hbm = pl.BlockSpec(memory_space=pl.ANY)
    out_specs = [row] + ([row] if want_bf16 else []) + [vec]
    out_shape = [_sds((Tp, D), F32)] + ([_sds((Tp, D), BF16)] if want_bf16 else []) + [_sds((1, D), F32)]
    in_specs = ([pl.BlockSpec((tm, n), lambda i, k: (t0 + i, k))]
                + [pl.BlockSpec((None, D, n), lambda i, k: (k // P, 0, 0))] * P
                + [hbm, hbm, pl.BlockSpec((tm, 1), lambda i, k: (t0 + i, 0)), vec])
    operands = (dz, *w_parts, dres, hin, rin, g)
    aliases = {}
    if earlier is not None:
        in_specs += [hbm, vec]
        operands += tuple(earlier)
        aliases = {P + 5: 0}
    return _hosted(
        body, tasks, name=name, grid=(nt, K), in_specs=in_specs, out_specs=out_specs, out_shape=out_shape,
        scratch_shapes=[pltpu.VMEM((tm, D), F32), pltpu.VMEM((tm, D), F32), pltpu.SemaphoreType.DMA((2,))],
        semantics=("arbitrary", "arbitrary"), operands=operands, aliases=aliases)


def _dmerged_bwd(dh1b, w_out, name):
    Tp, D = dh1b.shape
    tm = _tile(Tp, TILE["dmerged"])

    def body(g_ref, w_ref, o_ref):
        o_ref[...] = lax.dot_general(g_ref[...], w_ref[...], _NT, preferred_element_type=F32)

    row = pl.BlockSpec((tm, D), lambda i: (i, 0))
    return _call(
        body, name=name, grid=(Tp // tm,),
        in_specs=[row, pl.BlockSpec((D, D), lambda i: (0, 0))],
        out_specs=row, out_shape=_sds((Tp, D), F32),
        compiler_params=_cp("parallel"),
    )(dh1b, w_out)


def _pool_bwd(dmerged, proj, y_pool, d_pool, scale, pool_w, name, tasks=()):
    Tp, D = dmerged.shape
    G, Cg, _ = pool_w.shape
    tm = _tile(Tp, TILE["pool"])
    nt = Tp // tm

    def body(dm_ref, gp_ref, y_ref, d_ref, sc_ref, w_ref, dproj_hbm, dw_ref, dsc_ref, next_ref, out_buf, out_sems):
        t = pl.program_id(0)
        tile = nt - 1 - t
        slot = t % 2
        dv_ref, dgp_ref = out_buf.at[slot, 0], out_buf.at[slot, 1]
        tile_rows = pl.ds(pl.multiple_of(tile * tm, tm), tm)

        def out_copies(s):
            return [pltpu.make_async_copy(out_buf.at[s, k], dproj_hbm.at[tile_rows, pl.ds(piece * D, D)],
                                          out_sems.at[2 * s + k]) for k, piece in enumerate((0, 3))]

        @pl.when(t >= 2)
        def _():
            for cp in out_copies(slot):
                cp.wait()

        @pl.when(t == 0)
        def _():
            next_ref[...] = jnp.zeros_like(next_ref)
            dw_ref[...] = jnp.zeros_like(dw_ref)
            dsc_ref[...] = jnp.zeros_like(dsc_ref)

        rows = _row_ids(tile, tm)
        dm = dm_ref[...]
        y = y_ref[...]
        sc = sc_ref[...]
        sg = _sigmoid(gp_ref[...])
        dpo = dm * sg
        dgp_ref[...] = (dm * (y * sc) * sg * (1.0 - sg)).astype(BF16)
        dsc_ref[...] += jnp.sum(dpo * y, axis=0, keepdims=True)
        dyb = (dpo * sc).astype(BF16)
        for g, win in enumerate(POOL_WINDOWS):
            cols = slice(g * Cg, (g + 1) * Cg)
            dy = dyb[:, cols]
            dd = lax.dot_general(dy, w_ref[g], _NT, preferred_element_type=F32)
            dw_ref[g] += lax.dot_general(d_ref[:, cols], dy, _TN, preferred_element_type=F32)
            q = dd / jnp.minimum(rows + 1, win).astype(F32)
            s = jnp.concatenate([q, next_ref[:, cols]], axis=0)
            k = 1
            while k < win:
                s = s + pltpu.roll(s, s.shape[0] - k, axis=0)
                k *= 2
            dv_ref[:, cols] = (s[:tm] - dd).astype(BF16)
            next_ref[:, cols] = q[:MAX_WINDOW]
        for cp in out_copies(slot):
            cp.start()

        @pl.when(t == nt - 1)
        def _():
            for cp in out_copies(slot) + (out_copies(1 - slot) if nt > 1 else []):
                cp.wait()

    row = pl.BlockSpec((tm, D), lambda t: (nt - 1 - t, 0))
    return _hosted(
        body, tasks, name=name, grid=(nt,),
        in_specs=[row, pl.BlockSpec((tm, D), lambda t: (nt - 1 - t, 3)), row, row,
                  pl.BlockSpec((1, D), lambda t: (0, 0)), pl.BlockSpec((G, Cg, Cg), lambda t: (0, 0, 0))],
        out_specs=[pl.BlockSpec(memory_space=pl.ANY), pl.BlockSpec((G, Cg, Cg), lambda t: (0, 0, 0)),
                   pl.BlockSpec((1, D), lambda t: (0, 0))],
        out_shape=[_sds((Tp, proj.shape[1]), BF16), _sds((G, Cg, Cg), F32), _sds((1, D), F32)],
        scratch_shapes=[pltpu.VMEM((MAX_WINDOW, D), F32), pltpu.VMEM((2, 2, tm, D), BF16),
                        pltpu.SemaphoreType.DMA((4,))],
        semantics=("arbitrary",), operands=(dmerged, proj, y_pool, d_pool, scale, pool_w))


LRU_SMALL_ROWS = 8


def _lru_bwd(dmerged, proj, xc, r_gate, i_gate, a_gate, mult_gate, hs, lam, conv_w, wa, wx, dproj, name, tasks=()):
    Tp, D = dmerged.shape
    H, hd, _ = wa.shape
    tm = _tile(Tp, TILE["lru"])
    nt = Tp // tm
    nb = D // hd
    halo_blocks = tm // HALO

    def body(dm_ref, vl_ref, vg_ref, gl_ref, xc_ref, r_ref, i_ref, a_ref, mult_ref, hs_ref, hsp_ref, lam_ref, cw_ref,
             wa_ref, wx_ref, _, dproj_hbm, dwa_ref, dwx_ref, small_ref,
             mu_next_ref, dxc_next_ref, q_s, mu_s, out_buf, out_sems):
        h_id, t = pl.program_id(0), pl.program_id(1)
        tile = nt - 1 - t
        step = h_id * nt + t
        slot = step % 2
        dvl_ref, dvg_ref, dgl_ref = out_buf.at[slot, 0], out_buf.at[slot, 1], out_buf.at[slot, 2]
        tile_rows = pl.ds(pl.multiple_of(tile * tm, tm), tm)

        def out_copies(s):
            return [pltpu.make_async_copy(
                out_buf.at[s, k], dproj_hbm.at[tile_rows, pl.ds(pl.multiple_of((piece * nb + h_id) * hd, hd), hd)],
                out_sems.at[3 * s + k]) for k, piece in enumerate((1, 2, 4))]

        @pl.when(step >= 2)
        def _():
            for cp in out_copies(slot):
                cp.wait()

        @pl.when(t == 0)
        def _():
            mu_next_ref[...] = jnp.zeros_like(mu_next_ref)
            dxc_next_ref[...] = jnp.zeros_like(dxc_next_ref)
            dwa_ref[...] = jnp.zeros_like(dwa_ref)
            dwx_ref[...] = jnp.zeros_like(dwx_ref)
            small_ref[...] = jnp.zeros_like(small_ref)

        first = tile == 0
        dm = dm_ref[...]
        hs_t = hs_ref[...]
        xc_t = xc_ref[...]
        r = r_ref[...]
        i = i_ref[...]
        lam_v = lam_ref[...]
        sp = _softplus(-lam_v)
        a = a_ref[...]
        mult = mult_ref[...]

        sg = _sigmoid(gl_ref[...])
        ge, dge = _gelu_and_grad(vg_ref[...])
        dlo = dm * sg
        dgl_ref[...] = (dm * (hs_t * ge) * sg * (1.0 - sg)).astype(BF16)
        dvg_ref[...] = (dlo * hs_t * dge).astype(BF16)
        dhs = dlo * ge

        q_s[...] = a * dhs
        mu_first = _scan_chunks(a_ref, q_s, mu_s, mu_next_ref[0:1, :], tm, reverse=True)
        lam_t = dhs + _shift_up(mu_s[...], mu_next_ref[...], 1)
        mu_next_ref[...] = jnp.broadcast_to(mu_first, mu_next_ref.shape)

        h_prev = _shift_down(jnp.where(first, 0.0, hsp_ref[...]), hs_t, 1)
        da = lam_t * h_prev
        dmult = lam_t * (i * xc_t)
        di = lam_t * mult * xc_t
        dxc = lam_t * mult * i
        dlog_a = da * a - dmult * (a * a) / mult
        dr = dlog_a * (-LRU_C * sp)
        dlam_rows = dlog_a * (-LRU_C * r)
        dza = dr * r * (1.0 - r)
        dzx = di * i * (1.0 - i)
        dzab, dzxb = dza.astype(BF16), dzx.astype(BF16)
        xcb = xc_t.astype(BF16)
        dxc = dxc + lax.dot_general(dzab, wa_ref[...], _NT, preferred_element_type=F32)
        dxc = dxc + lax.dot_general(dzxb, wx_ref[...], _NT, preferred_element_type=F32)
        dwa_ref[...] += lax.dot_general(xcb, dzab, _TN, preferred_element_type=F32)
        dwx_ref[...] += lax.dot_general(xcb, dzxb, _TN, preferred_element_type=F32)

        dxc_next = dxc_next_ref[...]
        taps = [_shift_up(dxc, dxc_next, CONV_WIDTH - 1 - k) for k in range(CONV_WIDTH)]
        dv = jnp.zeros_like(dxc)
        for k in range(CONV_WIDTH):
            dv = dv + cw_ref[k:k + 1, :] * taps[k]
        dvl_ref[...] = dv.astype(BF16)
        dxc_next_ref[...] = dxc[:HALO, :]

        v_t = vl_ref[...]
        small = [jnp.sum(dza, axis=0, keepdims=True), jnp.sum(dzx, axis=0, keepdims=True),
                 jnp.sum(dlam_rows, axis=0, keepdims=True) * (-_sigmoid(-lam_v)),
                 jnp.sum(dxc, axis=0, keepdims=True)]
        for k in range(CONV_WIDTH):
            small.append(jnp.sum(taps[k] * v_t, axis=0, keepdims=True))
        for k, row in enumerate(small):
            small_ref[k:k + 1, :] += row
        for cp in out_copies(slot):
            cp.start()

        @pl.when(step == H * nt - 1)
        def _():
            for cp in out_copies(slot) + (out_copies(1 - slot) if H * nt > 1 else []):
                cp.wait()

    def piece(p):
        return pl.BlockSpec((tm, hd), lambda h, t: (nt - 1 - t, p * nb + h))

    def halo(p):
        return pl.BlockSpec((HALO, hd), lambda h, t: (jnp.maximum((nt - 1 - t) * halo_blocks - 1, 0), p * nb + h))

    blk = pl.BlockSpec((tm, hd), lambda h, t: (nt - 1 - t, h))
    vec = pl.BlockSpec((1, hd), lambda h, t: (0, h))
    mat = pl.BlockSpec((None, hd, hd), lambda h, t: (h, 0, 0))
    return _hosted(
        body, tasks, name=name, grid=(H, nt),
        in_specs=[blk, piece(1), piece(2), piece(4), blk, blk, blk, blk, blk, blk, halo(0), vec,
                  pl.BlockSpec((CONV_WIDTH, hd), lambda h, t: (0, h)), mat, mat, pl.BlockSpec(memory_space=pl.ANY)],
        out_specs=[pl.BlockSpec(memory_space=pl.ANY), mat, mat,
                   pl.BlockSpec((None, LRU_SMALL_ROWS, hd), lambda h, t: (h, 0, 0))],
        out_shape=[_sds(dproj.shape, BF16)] + [_sds((H, hd, hd), F32)] * 2 + [_sds((H, LRU_SMALL_ROWS, hd), F32)],
        scratch_shapes=[pltpu.VMEM((HALO, hd), F32), pltpu.VMEM((HALO, hd), F32),
                        pltpu.VMEM((tm, hd), F32), pltpu.VMEM((tm, hd), F32),
                        pltpu.VMEM((2, 3, tm, hd), BF16), pltpu.SemaphoreType.DMA((6,))],
        semantics=("arbitrary", "arbitrary"), aliases={15: 0},
        operands=(dmerged, proj, proj, proj, xc, r_gate, i_gate, a_gate, mult_gate, hs, hs, lam, conv_w, wa, wx,
                  dproj))


def _adamw(w, g, m, v):
    m = ADAM_B1 * m + (1.0 - ADAM_B1) * g
    v = ADAM_B2 * v + (1.0 - ADAM_B2) * (g * g)
    m_hat = m / (1.0 - ADAM_B1 ** ADAM_STEP)
    v_hat = v / (1.0 - ADAM_B2 ** ADAM_STEP)
    delta = -ADAM_LR * (m_hat / (jnp.sqrt(v_hat) + ADAM_EPS) + ADAM_WD * w)
    return delta, m, v


def _reduce_update(pair_sums, chip_sums, w, m, v, chip_slot, name, part=(0, 1), earlier=None):
    R, C = pair_sums.shape[1:]
    p, parts = part
    tr = _tile(R, TILE["update"])
    nblk = R // tr

    def body(slot_ref, own_ref, got_ref, w_ref, m_ref, v_ref, *rest):
        g_out, d_out, m_out, v_out = rest[-4:]
        g = own_ref[...].astype(F32)
        for k in range(3):
            g = g + got_ref[k].astype(F32)
        d, m_new, v_new = _adamw(w_ref[...], g, m_ref[...], v_ref[...])
        g_out[...] = g
        d_out[...] = d
        m_out[...] = m_new
        v_out[...] = v_new

    blk = pl.BlockSpec((tr, C), lambda i, s: (p * nblk + i, 0))
    in_specs = [pl.BlockSpec((None, tr, C), lambda i, s: (s[0], i, 0)),
                pl.BlockSpec((3, tr, C), lambda i, s: (0, i, 0)), blk, blk, blk]
    operands = (chip_slot, pair_sums, chip_sums, w, m, v)
    aliases = {}
    if earlier is not None:
        in_specs += [pl.BlockSpec(memory_space=pl.ANY)] * 4
        operands += tuple(earlier)
        aliases = {6 + k: k for k in range(4)}
    return _call(
        body, name=name,
        grid_spec=pltpu.PrefetchScalarGridSpec(
            num_scalar_prefetch=1, grid=(R // tr,), in_specs=in_specs, out_specs=[blk] * 4),
        out_shape=[_sds((parts * R, C), F32)] * 4,
        input_output_aliases=aliases,
        compiler_params=_cp("parallel"),
    )(*operands)


def _small_update(w, g, m, v, name):
    def body(w_ref, g_ref, m_ref, v_ref, d_out, m_out, v_out):
        d, m_new, v_new = _adamw(w_ref[...], g_ref[...], m_ref[...], v_ref[...])
        d_out[...] = d
        m_out[...] = m_new
        v_out[...] = v_new

    return _call(body, name=name, out_shape=[_sds(w.shape, F32)] * 3)(w, g, m, v)


def _slots_from_rows(full, lead):
    L, R, C = full.shape
    r = R // N_DEV
    return full.reshape(L, N_DEV, r, C).transpose(1, 0, 2, 3).reshape(N_DEV, L * r, C)


def _rows_from_slots(slots, lead):
    _, LR, C = slots.shape
    r = LR // lead
    return slots.reshape(N_DEV, lead, r, C).transpose(1, 0, 2, 3).reshape(lead, N_DEV * r, C)


def kernel(x, meta_tokens, norm1_g, w_in, pool_w, pool_scale, conv_w, conv_b, gate_a_w, gate_a_b, gate_x_w, gate_x_b, lru_lambda, w_out, norm2_g, mlp_w1, mlp_w2, final_g, loss_target, m_meta_tokens, m_norm1_g, m_w_in, m_pool_w, m_pool_scale, m_conv_w, m_conv_b, m_gate_a_w, m_gate_a_b, m_gate_x_w, m_gate_x_b, m_lru_lambda, m_w_out, m_norm2_g, m_mlp_w1, m_mlp_w2, m_final_g, v_meta_tokens, v_norm1_g, v_w_in, v_pool_w, v_pool_scale, v_conv_w, v_conv_b, v_gate_a_w, v_gate_a_b, v_gate_x_w, v_gate_x_b, v_lru_lambda, v_w_out, v_norm2_g, v_mlp_w1, v_mlp_w2, v_final_g):
    seq, D = x.shape[1], x.shape[2]
    n_meta = meta_tokens.shape[0]
    G, Cg = pool_w.shape[1], pool_w.shape[3]
    H, hd = gate_a_w.shape[1], gate_a_w.shape[3]
    T = n_meta + seq
    Tp = -(-T // ROW_ALIGN) * ROW_ALIGN
    ix, iy, ic = _pos()
    me = 4 * ix + 2 * iy + ic
    core = jnp.reshape(ic, (1,)).astype(jnp.int32)
    chip_slot = jnp.reshape(2 * ix + iy, (1,)).astype(jnp.int32)

    w_in_l, w1_l, w2_l, w_out_l = w_in[0], mlp_w1[0], mlp_w2[0], w_out[0]
    pool_l = pool_w[0].reshape(G * (Cg // N_DEV), Cg)
    wa_l = gate_a_w[0].reshape(H * (hd // N_DEV), hd)
    wx_l = gate_x_w[0].reshape(H * (hd // N_DEV), hd)
    small_params = jnp.concatenate(
        [meta_tokens, conv_w[0], jnp.zeros((4, D // N_DEV), F32)], axis=0)
    biases = jnp.concatenate([gate_a_b[0], gate_x_b[0]], axis=0)
    (w_in_g, pool_g, wa_g, wx_g, small_g, bias_g) = _all_gather_relay(
        [w_in_l.astype(BF16), pool_l.astype(BF16), wa_l.astype(BF16), wx_l.astype(BF16), small_params, biases],
        [W_IN_CHUNKS, 1, 1, 1, 1, 1], "gather_first")
    pool_full = _rows_from_slots(pool_g, G)
    wa_full = _rows_from_slots(wa_g, H)
    wx_full = _rows_from_slots(wx_g, H)
    small_full = small_g.transpose(1, 0, 2).reshape(n_meta + 8, D)
    meta_full = small_full[:n_meta]
    conv_full = small_full[n_meta:n_meta + CONV_WIDTH]
    bias_full = bias_g.transpose(1, 0, 2).reshape(2 * H, hd)
    ba_full, bx_full = bias_full[:H], bias_full[H:]

    h0 = jnp.concatenate([meta_full, x[0], jnp.zeros((Tp - T, D), F32)], axis=0)
    u, r1 = _norm_fwd(h0, norm1_g, "norm1")
    proj, ((w_out_g,), (w1_ici,)) = _proj_fwd(
        u, w_in_g, "proj", tasks=[_AgFull(w_out_l.astype(BF16)), _AgFull(w1_l.astype(BF16), forward_here=False)])
    w_in_parts = [w_in_g]
    d_pool, y_pool = _pool_fwd(proj, pool_full, "pool_fwd")
    (xc, r_gate, i_gate, a_gate, mult_gate, hs, merged), ((w1_g,), (w2_ici,)) = _lru_fwd(
        proj, y_pool, pool_scale, conv_full, conv_b, wa_full, ba_full, wx_full, bx_full, lru_lambda, "lru_fwd",
        tasks=[_AgForward(w1_ici), _AgFull(w2_l.astype(BF16), forward_here=False)])
    w_out_full = w_out_g.reshape(D, D)
    (h1, u2, r2), ((w2_g,),) = _wout_norm_fwd(merged, w_out_full, h0, norm2_g, "wout_norm2", tasks=[_AgForward(w2_ici)])
    (act, a1), _ = _mlp1_fwd(u2, w1_g, "mlp1")
    dh2, dh2b, loss_tile, d_final_g = _mlp2_loss(
        act, w2_g.reshape(-1, D), h1, loss_target[0], final_g.reshape(1, D), n_meta, seq, "mlp2_loss")

    def pair(part, got, tag):
        return _pair_sum(part, got, core, "pair_sum_" + tag)

    d_a1 = _dact_bwd(dh2b, w2_g, a1, "dact")
    dw2_p, _ = _weight_grad(act, dh2b, N_DEV, True, "dw2")
    dw1_p, ((dw2_got,),) = _weight_grad(u2, d_a1, N_DEV, False, "dw1", tasks=[_RsSibling(dw2_p)])
    dw2_pair = pair(dw2_p, dw2_got, "w2")
    (dh1, dh1b, d_norm2_g), ((dw2_chips,), (dw1_got,)) = _nt_norm_bwd(
        d_a1, [w1_g], dh2, h1, r2, norm2_g, True, "du2_norm2", tasks=[_RsChips(dw2_pair), _RsSibling(dw1_p)])
    dw1_pair = pair(dw1_p, dw1_got, "w1")
    dmerged = _dmerged_bwd(dh1b, w_out_full, "dmerged")
    dwout_p, _ = _weight_grad(merged, dh1b, 2, True, "dwout")
    dwout_p = dwout_p.reshape(N_DEV, D // N_DEV, D)
    (dproj_pool, dpool_full, d_scale), ((dwout_got,),) = _pool_bwd(
        dmerged, proj, y_pool, d_pool, pool_scale, pool_full, "pool_bwd", tasks=[_RsSibling(dwout_p)])
    dwout_pair = pair(dwout_p, dwout_got, "wout")
    (dproj, dwa_full, dwx_full, lru_small), ((dw1_chips,), (dwout_chips,)) = _lru_bwd(
        dmerged, proj, xc, r_gate, i_gate, a_gate, mult_gate, hs, lru_lambda, conv_full, wa_full, wx_full, dproj_pool,
        "lru_bwd", tasks=[_RsChips(dw1_pair), _RsChips(dwout_pair)])
    dwin_a, _ = _weight_grad(u, dproj, N_DEV, False, "dwin_a", part=(0, 2))
    dwin_b, ((dwin_a_got,),) = _weight_grad(u, dproj, N_DEV, False, "dwin_b", part=(1, 2), tasks=[_RsSibling(dwin_a)])
    dwin_a_pair = pair(dwin_a, dwin_a_got, "win_a")
    dpool_p = _slots_from_rows(dpool_full, G).astype(BF16)
    dwa_p = _slots_from_rows(dwa_full, H).astype(BF16)
    dwx_p = _slots_from_rows(dwx_full, H).astype(BF16)
    late = [dwin_b, dpool_p, dwa_p, dwx_p]
    n_tiles = Tp // _tile(Tp, TILE["nt"])
    n_first = max(1, n_tiles // 2)
    (dh0_a, d_norm1_a), ((dwin_a_chips,), *late_got) = _nt_norm_bwd(
        dproj, w_in_parts, dh1, h0, r1, norm1_g, False, "du_norm1_a",
        tasks=[_RsChips(dwin_a_pair)] + [_RsSibling(p) for p in late], tiles=(0, n_first))
    late_pair = [pair(p, g[0], "late%d" % k) for k, (p, g) in enumerate(zip(late, late_got))]
    (dh0, d_norm1_g), late_chips = _nt_norm_bwd(
        dproj, w_in_parts, dh1, h0, r1, norm1_g, False, "du_norm1_b", tasks=[_RsChips(p) for p in late_pair],
        tiles=(n_first, n_tiles - n_first), earlier=(dh0_a, d_norm1_a))
    grad_x = dh0[n_meta:T][None]

    pair_sums = [dw2_pair, dw1_pair, dwout_pair] + late_pair[1:]
    chip_sums = [dw2_chips, dw1_chips, dwout_chips] + [c[0] for c in late_chips[1:]]
    big = {}
    names = ["mlp_w2", "mlp_w1", "w_out", "pool_w", "gate_a_w", "gate_x_w"]
    trip = {"mlp_w2": (mlp_w2, m_mlp_w2, v_mlp_w2), "mlp_w1": (mlp_w1, m_mlp_w1, v_mlp_w1),
            "w_out": (w_out, m_w_out, v_w_out),
            "pool_w": (pool_w, m_pool_w, v_pool_w), "gate_a_w": (gate_a_w, m_gate_a_w, v_gate_a_w),
            "gate_x_w": (gate_x_w, m_gate_x_w, v_gate_x_w)}
    for k, nm in enumerate(names):
        w_, m_, v_ = trip[nm]
        shape2 = pair_sums[k].shape[1:]
        outs = _reduce_update(pair_sums[k], chip_sums[k], w_.reshape(shape2), m_.reshape(shape2), v_.reshape(shape2),
                              chip_slot, "update_" + nm)
        big[nm] = [o.reshape(w_.shape) for o in outs]
    win2 = [a_[0] for a_ in (w_in, m_w_in, v_w_in)]
    win_a = _reduce_update(dwin_a_pair, dwin_a_chips, *win2, chip_slot, "update_w_in_a", part=(0, 2))
    win_b = _reduce_update(late_pair[0], late_chips[0][0], *win2, chip_slot, "update_w_in_b", part=(1, 2), earlier=win_a)
    big["w_in"] = [o.reshape(w_in.shape) for o in win_b]
    names = names + ["w_in"]

    lru_rows = lru_small.transpose(1, 0, 2).reshape(LRU_SMALL_ROWS, D)
    small_part = jnp.concatenate(
        [dh0[:n_meta], d_norm1_g, d_scale, d_norm2_g, d_final_g, lru_rows, jnp.zeros((4, D), F32)], axis=0)
    (small_all,) = _all_gather([small_part], "gather_small_grads")
    small_sum = _sum_slots(small_all, "sum_small_grads")
    o = n_meta
    g_meta_full = small_sum[:o]
    g_norm1, g_scale, g_norm2, g_final = (small_sum[o + k:o + k + 1] for k in range(4))
    g_ba_full, g_bx_full, g_lam, g_cb = (small_sum[o + 4 + k:o + 5 + k] for k in range(4))
    g_cw_full = small_sum[o + 8:o + 8 + CONV_WIDTH]
    dcol = D // N_DEV
    g_meta = lax.dynamic_slice_in_dim(g_meta_full, me * dcol, dcol, axis=1)
    g_cw = lax.dynamic_slice_in_dim(g_cw_full, me * dcol, dcol, axis=1)
    hcol = hd // N_DEV
    g_ba = lax.dynamic_slice_in_dim(g_ba_full.reshape(H, hd), me * hcol, hcol, axis=1)
    g_bx = lax.dynamic_slice_in_dim(g_bx_full.reshape(H, hd), me * hcol, hcol, axis=1)

    rep_w = jnp.concatenate([norm1_g, pool_scale, conv_b, lru_lambda, norm2_g, final_g.reshape(1, D)], axis=0)
    rep_g = jnp.concatenate([g_norm1, g_scale, g_cb, g_lam, g_norm2, g_final], axis=0)
    rep_m = jnp.concatenate([m_norm1_g, m_pool_scale, m_conv_b, m_lru_lambda, m_norm2_g, m_final_g.reshape(1, D)], axis=0)
    rep_v = jnp.concatenate([v_norm1_g, v_pool_scale, v_conv_b, v_lru_lambda, v_norm2_g, v_final_g.reshape(1, D)], axis=0)
    rep_d, rep_nm, rep_nv = _small_update(rep_w, rep_g, rep_m, rep_v, "update_vectors")
    col_w = jnp.concatenate([meta_tokens, conv_w[0]], axis=0)
    col_g = jnp.concatenate([g_meta, g_cw], axis=0)
    col_m = jnp.concatenate([m_meta_tokens, m_conv_w[0]], axis=0)
    col_v = jnp.concatenate([v_meta_tokens, v_conv_w[0]], axis=0)
    col_d, col_nm, col_nv = _small_update(col_w, col_g, col_m, col_v, "update_columns")
    b_w = jnp.concatenate([gate_a_b[0], gate_x_b[0]], axis=0)
    b_g = jnp.concatenate([g_ba, g_bx], axis=0)
    b_m = jnp.concatenate([m_gate_a_b[0], m_gate_x_b[0]], axis=0)
    b_v = jnp.concatenate([v_gate_a_b[0], v_gate_x_b[0]], axis=0)
    b_d, b_nm, b_nv = _small_update(b_w, b_g, b_m, b_v, "update_biases")

    def rep(arr, k, like):
        return arr[k:k + 1].reshape(like.shape)

    rep_order = {"norm1_g": 0, "pool_scale": 1, "conv_b": 2, "lru_lambda": 3, "norm2_g": 4, "final_g": 5}
    like = {"norm1_g": norm1_g, "pool_scale": pool_scale, "conv_b": conv_b, "lru_lambda": lru_lambda,
            "norm2_g": norm2_g, "final_g": final_g}

    def leaves(kind):
        rep_src = [rep_g, rep_d, rep_nm, rep_nv][kind]
        col_src = [col_g, col_d, col_nm, col_nv][kind]
        b_src = [b_g, b_d, b_nm, b_nv][kind]
        out = {}
        out["meta_tokens"] = col_src[:n_meta]
        out["conv_w"] = col_src[n_meta:][None]
        out["gate_a_b"] = b_src[:H][None]
        out["gate_x_b"] = b_src[H:][None]
        for nm, k in rep_order.items():
            out[nm] = rep(rep_src, k, like[nm])
        for nm in names:
            out[nm] = big[nm][kind]
        order = ["meta_tokens", "norm1_g", "w_in", "pool_w", "pool_scale", "conv_w", "conv_b", "gate_a_w", "gate_a_b",
                 "gate_x_w", "gate_x_b", "lru_lambda", "w_out", "norm2_g", "mlp_w1", "mlp_w2", "final_g"]
        return [out[nm] for nm in order]

    loss = lax.psum(loss_tile[0, 0], ("x", "y", "c"))
    return (loss, grad_x, *leaves(0), *leaves(1), *leaves(2), *leaves(3))
```

```python
import functools

import jax
import jax.numpy as jnp
from jax import lax
from jax.experimental import pallas as pl
from jax.experimental.pallas import tpu as pltpu

F32 = jnp.float32
BF16 = jnp.bfloat16
MESH = pl.DeviceIdType.MESH
N_DEV = 8
POOL_WINDOWS = (2, 4, 8, 16)
MAX_WINDOW = 16
CONV_WIDTH = 4
HALO = 8
LRU_C = 8.0
NORM_EPS = 1e-6
ADAM_LR, ADAM_B1, ADAM_B2, ADAM_EPS, ADAM_WD, ADAM_STEP = 0.001, 0.9, 0.999, 1e-08, 0.01, 10
ROW_ALIGN = 128
VMEM_LIMIT = 56 << 20
TILE = dict(norm=384, proj=1408, pool=384, lru=704, wout=384, mlp1=1408, mlp2=704, dact=1408, tn=2112,
            nt=704, dmerged=704, update=256, pair=1024)
MLP2_K = 1024
EPILOGUE_ROWS = 176
SCAN_GROUP = 4
W_IN_CHUNKS = 5
MID_STEP_PERCENT = 88

_NT = (((1,), (1,)), ((), ()))
_TN = (((0,), (0,)), ((), ()))


def _call(body, **kw):
    return pl.pallas_call(body, **kw)


def _cp(*sem):
    return pltpu.CompilerParams(dimension_semantics=sem, vmem_limit_bytes=VMEM_LIMIT)


def _tile(total, pref):
    best = None
    for t in range(16, min(total, pref) + 1, 16):
        if total % t == 0:
            best = t
    assert best is not None, (total, pref)
    return best


def _sds(shape, dtype):
    return jax.ShapeDtypeStruct(shape, dtype)


def _pos():
    return lax.axis_index("x"), lax.axis_index("y"), lax.axis_index("c")


def _all_gather(shards, name):
    n = len(shards)

    def body(*refs):
        ins, outs = refs[:n], refs[n:2 * n]
        send_sems, recv_sems, local_sems = refs[2 * n:]
        x, y, c = _pos()
        me, sib = (x, y, c), (x, y, 1 - c)
        chips = [(1 - x, y), (x, 1 - y), (1 - x, 1 - y)]

        def slot(p):
            return 4 * p[0] + 2 * p[1] + p[2]

        def copy(a, k, block, to, src=None):
            dst = outs[a].at[slot(block)]
            return pltpu.make_async_remote_copy(
                src_ref=dst if src is None else src, dst_ref=dst,
                send_sem=send_sems.at[7 * a + k], recv_sem=recv_sems.at[7 * a + k],
                device_id=to, device_id_type=MESH)

        mine = [pltpu.make_async_copy(ins[a], outs[a].at[slot(me)], local_sems.at[a]) for a in range(n)]
        for m in mine:
            m.start()
        first = []
        for a in range(n):
            first.append(copy(a, 0, me, sib, src=ins[a]))
            first += [copy(a, 1 + j, me, (*chip, c), src=ins[a]) for j, chip in enumerate(chips)]
        for cp in first:
            cp.start()
        passed = []
        for a in range(n):
            for j, chip in enumerate(chips):
                copy(a, 1 + j, (*chip, c), me).wait_recv()
                fwd = copy(a, 4 + j, (*chip, c), sib)
                fwd.start()
                passed.append(fwd)
        for a in range(n):
            copy(a, 0, sib, me).wait_recv()
            for j, chip in enumerate(chips):
                copy(a, 4 + j, (*chip, 1 - c), me).wait_recv()
        for cp in first + passed:
            cp.wait_send()
        for m in mine:
            m.wait()

    hbm = pl.BlockSpec(memory_space=pl.ANY)
    return _call(
        body, name=name,
        out_shape=[_sds((N_DEV,) + s.shape, s.dtype) for s in shards],
        in_specs=[hbm] * n, out_specs=[hbm] * n,
        scratch_shapes=[pltpu.SemaphoreType.DMA((7 * n,)), pltpu.SemaphoreType.DMA((7 * n,)),
                        pltpu.SemaphoreType.DMA((n,))],
    )(*shards)


def _all_gather_relay(shards, chunks, name):
    n = len(shards)
    units = []
    for a, s in enumerate(shards):
        if chunks[a] == 1:
            units.append((a, None, None))
        else:
            w = s.shape[-1] // chunks[a]
            units += [(a, q * w, w) for q in range(chunks[a])]
    nu = len(units)

    def body(*refs):
        ins, outs = refs[:n], refs[n:2 * n]
        send_sems, recv_sems, local_sems = refs[2 * n:]
        x, y, c = _pos()
        me, sib = (x, y, c), (x, y, 1 - c)
        x_nbr, y_nbr, diag = (1 - x, y, c), (x, 1 - y, c), (1 - x, 1 - y, c)
        came_from = (x + (1 - c) * (1 - 2 * x), y + c * (1 - 2 * y), c)
        pass_to = (x + c * (1 - 2 * x), y + (1 - c) * (1 - 2 * y), c)

        def slot(p):
            return 4 * p[0] + 2 * p[1] + p[2]

        def src_view(u):
            a, c0, w = units[u]
            return ins[a] if c0 is None else ins[a].at[:, pl.ds(c0, w)]

        def dst_view(u, p):
            a, c0, w = units[u]
            return outs[a].at[slot(p)] if c0 is None else outs[a].at[slot(p), :, pl.ds(c0, w)]

        def copy(u, k, block, to, from_shard=False):
            dst = dst_view(u, block)
            return pltpu.make_async_remote_copy(
                src_ref=src_view(u) if from_shard else dst, dst_ref=dst,
                send_sem=send_sems.at[7 * u + k], recv_sem=recv_sems.at[7 * u + k],
                device_id=to, device_id_type=MESH)

        mine = [pltpu.make_async_copy(src_view(u), dst_view(u, me), local_sems.at[u]) for u in range(nu)]
        for m in mine:
            m.start()
        sent = []
        for u in range(nu):
            sent += [copy(u, 0, me, sib, True), copy(u, 1, me, x_nbr, True), copy(u, 2, me, y_nbr, True)]
        for cp in sent:
            cp.start()
        for u in range(nu):
            copy(u, 1, x_nbr, me).wait_recv()
            copy(u, 2, y_nbr, me).wait_recv()
            sent += [copy(u, 3, came_from, pass_to), copy(u, 4, x_nbr, sib), copy(u, 5, y_nbr, sib)]
            for cp in sent[-3:]:
                cp.start()
        for u in range(nu):
            copy(u, 3, diag, me).wait_recv()
            sent.append(copy(u, 6, diag, sib))
            sent[-1].start()
        for u in range(nu):
            copy(u, 0, sib, me).wait_recv()
            for k, p in ((4, x_nbr), (5, y_nbr), (6, diag)):
                copy(u, k, (p[0], p[1], 1 - c), me).wait_recv()
        for cp in sent:
            cp.wait_send()
        for m in mine:
            m.wait()

    hbm = pl.BlockSpec(memory_space=pl.ANY)
    return _call(
        body, name=name,
        out_shape=[_sds((N_DEV,) + s.shape, s.dtype) for s in shards],
        in_specs=[hbm] * n, out_specs=[hbm] * n,
        scratch_shapes=[pltpu.SemaphoreType.DMA((7 * nu,)), pltpu.SemaphoreType.DMA((7 * nu,)),
                        pltpu.SemaphoreType.DMA((nu,))],
    )(*shards)


def _other_chips(x, y):
    return [(1 - x, y), (x, 1 - y), (1 - x, 1 - y)]


class _AgFull:
    n_sem, n_local = 7, 1

    def __init__(self, shard, forward_here=True):
        self.forward_here = forward_here
        self.ins = [shard]
        self.out_shapes = [_sds((N_DEV,) + shard.shape, shard.dtype)]
        self.aliases = []

    def _peers(self):
        x, y, c = _pos()
        return [(x, y, 1 - c)] + [(*chip, c) for chip in _other_chips(x, y)]

    def _sends(self, ins, outs, sems):
        send, recv, _, base, _ = sems
        x, y, c = _pos()
        mine = outs[0].at[4 * x + 2 * y + c]
        return [pltpu.make_async_remote_copy(src_ref=ins[0], dst_ref=mine, send_sem=send.at[base + k],
                                             recv_sem=recv.at[base + k], device_id=p, device_id_type=MESH)
                for k, p in enumerate(self._peers())]

    def _arrivals(self, outs, sems):
        send, recv, _, base, _ = sems
        res = []
        for k, p in enumerate(self._peers()):
            blk = outs[0].at[4 * p[0] + 2 * p[1] + p[2]]
            res.append(pltpu.make_async_remote_copy(src_ref=blk, dst_ref=blk, send_sem=send.at[base + k],
                                                    recv_sem=recv.at[base + k], device_id=p, device_id_type=MESH))
        return res

    def _own(self, ins, outs, sems):
        x, y, c = _pos()
        return pltpu.make_async_copy(ins[0], outs[0].at[4 * x + 2 * y + c], sems[2].at[sems[4]])

    def _forwards(self, outs, sems, core_of_block):
        return _forward_copies(outs[0], sems[0], sems[1], sems[3] + 4, core_of_block)

    def start(self, ins, outs, sems):
        self._own(ins, outs, sems).start()
        for cp in self._sends(ins, outs, sems):
            cp.start()

    def mid(self, ins, outs, sems):
        if self.forward_here:
            for cp in self._arrivals(outs, sems)[1:]:
                cp.wait_recv()
            for cp in self._forwards(outs, sems, "mine"):
                cp.start()

    def finish(self, ins, outs, sems):
        if self.forward_here:
            self._arrivals(outs, sems)[0].wait_recv()
            for cp in self._forwards(outs, sems, "sibling"):
                cp.wait_recv()
            for cp in self._sends(ins, outs, sems) + self._forwards(outs, sems, "mine"):
                cp.wait_send()
        else:
            for cp in self._arrivals(outs, sems):
                cp.wait_recv()
            for cp in self._sends(ins, outs, sems):
                cp.wait_send()
        self._own(ins, outs, sems).wait()


def _forward_copies(gathered_ref, send, recv, base, core_of_block):
    x, y, c = _pos()
    res = []
    for k, chip in enumerate(_other_chips(x, y)):
        blk = gathered_ref.at[4 * chip[0] + 2 * chip[1] + (c if core_of_block == "mine" else 1 - c)]
        res.append(pltpu.make_async_remote_copy(src_ref=blk, dst_ref=blk, send_sem=send.at[base + k],
                                                recv_sem=recv.at[base + k], device_id=(x, y, 1 - c),
                                                device_id_type=MESH))
    return res


class _AgForward:
    n_sem, n_local = 3, 0

    def __init__(self, gathered):
        self.ins = [gathered]
        self.out_shapes = [_sds(gathered.shape, gathered.dtype)]
        self.aliases = [(0, 0)]

    def start(self, ins, outs, sems):
        for cp in _forward_copies(outs[0], sems[0], sems[1], sems[3], "mine"):
            cp.start()

    def finish(self, ins, outs, sems):
        for cp in _forward_copies(outs[0], sems[0], sems[1], sems[3], "sibling"):
            cp.wait_recv()
        for cp in _forward_copies(outs[0], sems[0], sems[1], sems[3], "mine"):
            cp.wait_send()


class _RsSibling:
    n_sem, n_local = 4, 0

    def __init__(self, part):
        self.ins = [part]
        self.out_shapes = [_sds((4,) + part.shape[1:], part.dtype)]
        self.aliases = []

    def _copies(self, ins, outs, sems):
        send, recv, _, base, _ = sems
        x, y, c = _pos()
        return [pltpu.make_async_remote_copy(src_ref=ins[0].at[2 * q + (1 - c)], dst_ref=outs[0].at[q],
                                             send_sem=send.at[base + q], recv_sem=recv.at[base + q],
                                             device_id=(x, y, 1 - c), device_id_type=MESH) for q in range(4)]

    def start(self, ins, outs, sems):
        for cp in self._copies(ins, outs, sems):
            cp.start()

    def finish(self, ins, outs, sems):
        for cp in self._copies(ins, outs, sems):
            cp.wait()


class _RsChips:
    n_sem, n_local = 3, 0

    def __init__(self, pair):
        self.ins = [pair]
        self.out_shapes = [_sds((3,) + pair.shape[1:], pair.dtype)]
        self.aliases = []

    def _copies(self, ins, outs, sems):
        send, recv, _, base, _ = sems
        x, y, c = _pos()
        return [pltpu.make_async_remote_copy(src_ref=ins[0].at[2 * chip[0] + chip[1]], dst_ref=outs[0].at[k],
                                             send_sem=send.at[base + k], recv_sem=recv.at[base + k],
                                             device_id=(*chip, c), device_id_type=MESH)
                for k, chip in enumerate(_other_chips(x, y))]

    def start(self, ins, outs, sems):
        for cp in self._copies(ins, outs, sems):
            cp.start()

    def finish(self, ins, outs, sems):
        for cp in self._copies(ins, outs, sems):
            cp.wait()


def _hosted(body, tasks, *, grid, in_specs, out_specs, out_shape, scratch_shapes=(), name, semantics, operands,
            aliases=None):
    in_specs, out_specs, out_shape = list(in_specs), list(out_specs), list(out_shape)
    scratch_shapes = list(scratch_shapes)
    aliases = dict(aliases or {})
    if not tasks:
        res = _call(body, name=name, grid=grid, in_specs=in_specs, out_specs=out_specs, out_shape=out_shape,
                    scratch_shapes=scratch_shapes, input_output_aliases=aliases,
                    compiler_params=_cp(*semantics))(*operands)
        return list(res), []
    n_in, n_out, n_scr = len(in_specs), len(out_specs), len(scratch_shapes)
    t_ins = [a for t in tasks for a in t.ins]
    t_outs = [o for t in tasks for o in t.out_shapes]
    i0, o0 = n_in, n_out
    for t in tasks:
        for (i, o) in t.aliases:
            aliases[i0 + i] = o0 + o
        i0 += len(t.ins)
        o0 += len(t.out_shapes)
    n_sem = sum(t.n_sem for t in tasks)
    n_local = max(1, sum(t.n_local for t in tasks))
    n_steps = 1
    for g in grid:
        n_steps *= g
    mid_step = min(n_steps - 1, (n_steps * MID_STEP_PERCENT) // 100)

    def wrapped(*refs):
        cut = [n_in, len(t_ins), n_out, len(t_outs), n_scr]
        parts, at = [], 0
        for n in cut:
            parts.append(refs[at:at + n])
            at += n
        ins, tin, outs, tout, scratch = parts
        send, recv, local = refs[at:]
        step = pl.program_id(0)
        for d in range(1, len(grid)):
            step = step * grid[d] + pl.program_id(d)

        def each(method):
            i, o, s, l = 0, 0, 0, 0
            for t in tasks:
                if hasattr(t, method):
                    getattr(t, method)(tin[i:i + len(t.ins)], tout[o:o + len(t.out_shapes)], (send, recv, local, s, l))
                i, o, s, l = i + len(t.ins), o + len(t.out_shapes), s + t.n_sem, l + t.n_local

        @pl.when(step == 0)
        def _():
            each("start")

        body(*ins, *outs, *scratch)

        @pl.when(step == mid_step)
        def _():
            each("mid")

        @pl.when(step == n_steps - 1)
        def _():
            each("finish")

    hbm = pl.BlockSpec(memory_space=pl.ANY)
    res = _call(
        wrapped, name=name, grid=grid,
        in_specs=in_specs + [hbm] * len(t_ins), out_specs=out_specs + [hbm] * len(t_outs),
        out_shape=out_shape + t_outs,
        scratch_shapes=scratch_shapes + [pltpu.SemaphoreType.DMA((n_sem,)), pltpu.SemaphoreType.DMA((n_sem,)),
                                         pltpu.SemaphoreType.DMA((n_local,))],
        input_output_aliases=aliases,
        compiler_params=_cp(*(["arbitrary"] * len(grid))),
    )(*operands, *t_ins)
    res = list(res)
    task_outs, o = [], n_out
    for t in tasks:
        task_outs.append(res[o:o + len(t.out_shapes)])
        o += len(t.out_shapes)
    return res[:n_out], task_outs


def _pair_sum(part, got, core, name):
    _, R, C = part.shape
    tr = _tile(R, TILE["pair"]) if R % 16 == 0 else R

    def body(core_ref, p_ref, g_ref, o_ref):
        o_ref[...] = (p_ref[...].astype(F32) + g_ref[...].astype(F32)).astype(o_ref.dtype)

    return _call(
        body, name=name,
        grid_spec=pltpu.PrefetchScalarGridSpec(
            num_scalar_prefetch=1, grid=(4, R // tr),
            in_specs=[pl.BlockSpec((None, tr, C), lambda q, i, cr: (2 * q + cr[0], i, 0)),
                      pl.BlockSpec((None, tr, C), lambda q, i, cr: (q, i, 0))],
            out_specs=pl.BlockSpec((None, tr, C), lambda q, i, cr: (q, i, 0))),
        out_shape=_sds((4, R, C), part.dtype),
        compiler_params=_cp("parallel", "parallel"),
    )(core, part, got)


def _sum_slots(gathered, name):
    _, R, C = gathered.shape

    def body(g_ref, o_ref):
        acc = g_ref[0]
        for s in range(1, N_DEV):
            acc = acc + g_ref[s]
        o_ref[...] = acc

    return _call(body, name=name, out_shape=_sds((R, C), F32))(gathered)


def _sigmoid(z):
    return jax.nn.sigmoid(z)


def _softplus(z):
    e = jnp.exp(-jnp.abs(z))
    log1p_e = jnp.where(e < 0.01, e * (1.0 - e * (0.5 - e * (1.0 / 3.0))), jnp.log(1.0 + e))
    return jnp.maximum(z, 0.0) + log1p_e


_GELU_K = 0.7978845608028654
_GELU_C = 0.044715


def _gelu_and_grad(z):
    t = jnp.tanh(_GELU_K * (z + _GELU_C * z * z * z))
    g = 0.5 * z * (1.0 + t)
    dg = 0.5 * (1.0 + t) + 0.5 * z * (1.0 - t * t) * _GELU_K * (1.0 + 3.0 * _GELU_C * z * z)
    return g, dg


def _gelu(z):
    t = jnp.tanh(_GELU_K * (z + _GELU_C * z * z * z))
    return 0.5 * z * (1.0 + t)


def _row_ids(tile_index, tm, width=1):
    return tile_index * tm + lax.broadcasted_iota(jnp.int32, (tm, width), 0)


def _shift_down(prev, cur, k):
    if k == 0:
        return cur
    ext = jnp.concatenate([prev, cur], axis=0)
    return pltpu.roll(ext, k, axis=0)[prev.shape[0]:]


def _shift_up(cur, nxt, k):
    if k == 0:
        return cur
    ext = jnp.concatenate([cur, nxt], axis=0)
    return pltpu.roll(ext, ext.shape[0] - k, axis=0)[:cur.shape[0]]


def _lru_gates(r, sp):
    z = LRU_C * r * sp
    a = jnp.exp(-z)
    t = jnp.tanh(z)
    mult = jnp.sqrt(2.0 * t / (1.0 + t))
    return a, mult


def _scan_chunks(a_ref, b_ref, out_ref, carry, n_rows, reverse):
    n_chunks = n_rows // 8
    cols = a_ref.shape[1]
    rid = lax.broadcasted_iota(jnp.int32, (8, cols), 0)
    edge = 0 if reverse else 7
    group = SCAN_GROUP if n_chunks % SCAN_GROUP == 0 else 1

    def local_scan(ci):
        rows = pl.ds(pl.multiple_of(ci * 8, 8), 8)
        a = a_ref[rows, :]
        b = b_ref[rows, :]
        for s in (1, 2, 4):
            if reverse:
                keep = rid < 8 - s
                a_n, b_n = pltpu.roll(a, 8 - s, axis=0), pltpu.roll(b, 8 - s, axis=0)
            else:
                keep = rid >= s
                a_n, b_n = pltpu.roll(a, s, axis=0), pltpu.roll(b, s, axis=0)
            b = a * jnp.where(keep, b_n, 0.0) + b
            a = a * jnp.where(keep, a_n, 1.0)
        a_e = jnp.sum(jnp.where(rid == edge, a, 0.0), axis=0, keepdims=True)
        b_e = jnp.sum(jnp.where(rid == edge, b, 0.0), axis=0, keepdims=True)
        return rows, a, b, a_e, b_e

    def chunks(k, h):
        first = k * group
        scans = [local_scan((n_chunks - 1 - (first + j)) if reverse else first + j) for j in range(group)]
        for rows, a, b, a_e, b_e in scans:
            out_ref[rows, :] = a * h + b
            h = a_e * h + b_e
        return h

    return lax.fori_loop(0, n_chunks // group, chunks, carry)


def _norm_fwd(h, g, name):
    Tp, D = h.shape
    tm = _tile(Tp, TILE["norm"])

    def body(h_ref, g_ref, u_ref, r_ref):
        x = h_ref[...]
        r = lax.rsqrt(jnp.mean(x * x, axis=-1, keepdims=True) + NORM_EPS)
        u_ref[...] = (x * r * g_ref[...]).astype(BF16)
        r_ref[...] = r

    return _call(
        body, name=name, grid=(Tp // tm,),
        in_specs=[pl.BlockSpec((tm, D), lambda i: (i, 0)), pl.BlockSpec((1, D), lambda i: (0, 0))],
        out_specs=[pl.BlockSpec((tm, D), lambda i: (i, 0)), pl.BlockSpec((tm, 1), lambda i: (i, 0))],
        out_shape=[_sds((Tp, D), BF16), _sds((Tp, 1), F32)],
        compiler_params=_cp("parallel"),
    )(h, g)


def _proj_fwd(u, w_slots, name, tasks=(), part=(0, 1), earlier=None):
    Tp, K = u.shape
    S, _, n = w_slots.shape
    p, parts = part
    tm = _tile(Tp, TILE["proj"])

    def body(a_ref, b_ref, *rest):
        o_ref = rest[-1]
        o_ref[...] = jnp.dot(a_ref[...], b_ref[...], preferred_element_type=F32)

    in_specs = [pl.BlockSpec((tm, K), lambda j, i: (i, 0)), pl.BlockSpec((None, K, n), lambda j, i: (j, 0, 0))]
    operands = (u, w_slots)
    aliases = {}
    if earlier is not None:
        in_specs.append(pl.BlockSpec(memory_space=pl.ANY))
        operands += (earlier,)
        aliases = {2: 0}
    (proj,), extra = _hosted(
        body, tasks, name=name, grid=(S, Tp // tm), in_specs=in_specs,
        out_specs=[pl.BlockSpec((tm, n), lambda j, i: (i, j * parts + p))],
        out_shape=[_sds((Tp, S * parts * n), F32)],
        semantics=("parallel", "parallel"), operands=operands, aliases=aliases)
    return proj, extra


def _mlp1_fwd(u2, w_slots, name, tasks=()):
    Tp, K = u2.shape
    S, _, n = w_slots.shape
    tm = _tile(Tp, TILE["mlp1"])

    def body(a_ref, b_ref, act_ref, a1_ref):
        a1 = jnp.dot(a_ref[...], b_ref[...], preferred_element_type=F32)
        relu = jnp.maximum(a1, 0.0)
        act_ref[...] = (relu * relu).astype(BF16)
        a1_ref[...] = a1.astype(BF16)

    return _hosted(
        body, tasks, name=name, grid=(S, Tp // tm),
        in_specs=[pl.BlockSpec((tm, K), lambda j, i: (i, 0)), pl.BlockSpec((None, K, n), lambda j, i: (j, 0, 0))],
        out_specs=[pl.BlockSpec((tm, n), lambda j, i: (i, j))] * 2,
        out_shape=[_sds((Tp, S * n), BF16)] * 2,
        semantics=("parallel", "parallel"), operands=(u2, w_slots))


def _pool_fwd(proj, pool_w, name):
    Tp = proj.shape[0]
    G, Cg, _ = pool_w.shape
    D = G * Cg
    tm = _tile(Tp, TILE["pool"])

    def body(v_ref, w_ref, d_ref, y_ref, prev_ref):
        t = pl.program_id(0)

        @pl.when(t == 0)
        def _():
            prev_ref[...] = jnp.zeros_like(prev_ref)

        rows = _row_ids(t, tm)
        for g, win in enumerate(POOL_WINDOWS):
            cols = slice(g * Cg, (g + 1) * Cg)
            v = v_ref[:, cols]
            s = jnp.concatenate([prev_ref[:, cols], v], axis=0)
            k = 1
            while k < win:
                s = s + pltpu.roll(s, k, axis=0)
                k *= 2
            cnt = jnp.minimum(rows + 1, win).astype(F32)
            d = s[MAX_WINDOW:] / cnt - v
            d_ref[:, cols] = d.astype(BF16)
            y_ref[:, cols] = jnp.dot(d.astype(BF16), w_ref[g], preferred_element_type=F32)
        prev_ref[...] = v_ref[tm - MAX_WINDOW:, :]

    return _call(
        body, name=name, grid=(Tp // tm,),
        in_specs=[pl.BlockSpec((tm, D), lambda t: (t, 0)), pl.BlockSpec((G, Cg, Cg), lambda t: (0, 0, 0))],
        out_specs=[pl.BlockSpec((tm, D), lambda t: (t, 0))] * 2,
        out_shape=[_sds((Tp, D), BF16), _sds((Tp, D), F32)],
        scratch_shapes=[pltpu.VMEM((MAX_WINDOW, D), F32)],
        compiler_params=_cp("arbitrary"),
    )(proj, pool_w)


def _lru_fwd(proj, y_pool, scale, conv_w, conv_b, wa, ba, wx, bx, lam, name, tasks=()):
    Tp = proj.shape[0]
    H, hd, _ = wa.shape
    D = H * hd
    tm = _tile(Tp, TILE["lru"])
    nb = D // hd

    def body(vl_ref, vg_ref, gp_ref, gl_ref, y_ref, sc_ref, cw_ref, cb_ref, wa_ref, ba_ref, wx_ref, bx_ref,
             lam_ref, xc_ref, r_ref, i_ref, a_ref, mult_ref, hs_ref, m_ref, prev_ref, carry_ref, b_s):
        t = pl.program_id(1)

        @pl.when(t == 0)
        def _():
            prev_ref[...] = jnp.zeros_like(prev_ref)
            carry_ref[...] = jnp.zeros_like(carry_ref)

        v = vl_ref[...]
        prev = prev_ref[...]
        xc = jnp.zeros_like(v) + cb_ref[...]
        for k in range(CONV_WIDTH):
            xc = xc + cw_ref[k:k + 1, :] * _shift_down(prev, v, CONV_WIDTH - 1 - k)
        prev_ref[...] = v[tm - HALO:, :]
        xcb = xc.astype(BF16)
        r = _sigmoid(jnp.dot(xcb, wa_ref[...], preferred_element_type=F32) + ba_ref[...])
        i = _sigmoid(jnp.dot(xcb, wx_ref[...], preferred_element_type=F32) + bx_ref[...])
        a, mult = _lru_gates(r, _softplus(-lam_ref[...]))
        a_ref[...] = a
        mult_ref[...] = mult
        b_s[...] = mult * (i * xc)
        xc_ref[...] = xc
        r_ref[...] = r
        i_ref[...] = i
        carry_ref[0:1, :] = _scan_chunks(a_ref, b_s, hs_ref, carry_ref[0:1, :], tm, reverse=False)
        lru_out = hs_ref[...] * _gelu(vg_ref[...])
        pool_out = y_ref[...] * sc_ref[...]
        m_ref[...] = (_sigmoid(gp_ref[...]) * pool_out + _sigmoid(gl_ref[...]) * lru_out).astype(BF16)

    def piece(p):
        return pl.BlockSpec((tm, hd), lambda h, t: (t, p * nb + h))

    blk = pl.BlockSpec((tm, hd), lambda h, t: (t, h))
    vec = pl.BlockSpec((1, hd), lambda h, t: (0, h))
    mat = pl.BlockSpec((None, hd, hd), lambda h, t: (h, 0, 0))
    bias = pl.BlockSpec((None, 1, hd), lambda h, t: (h, 0, 0))
    return _hosted(
        body, tasks, name=name, grid=(H, Tp // tm),
        in_specs=[piece(1), piece(2), piece(3), piece(4), blk, vec,
                  pl.BlockSpec((CONV_WIDTH, hd), lambda h, t: (0, h)), vec, mat, bias, mat, bias, vec],
        out_specs=[blk] * 7,
        out_shape=[_sds((Tp, D), F32)] * 6 + [_sds((Tp, D), BF16)],
        scratch_shapes=[pltpu.VMEM((HALO, hd), F32), pltpu.VMEM((8, hd), F32), pltpu.VMEM((tm, hd), F32)],
        semantics=("parallel", "arbitrary"),
        operands=(proj, proj, proj, proj, y_pool, scale, conv_w, conv_b, wa, ba.reshape(H, 1, hd), wx,
                  bx.reshape(H, 1, hd), lam))


def _wout_norm_fwd(merged, w_out, h0, g2, name, tasks=()):
    Tp, D = h0.shape
    tm = _tile(Tp, TILE["wout"])

    def body(m_ref, w_ref, h0_ref, g_ref, h1_ref, u2_ref, r2_ref):
        h1 = h0_ref[...] + jnp.dot(m_ref[...], w_ref[...], preferred_element_type=F32)
        r = lax.rsqrt(jnp.mean(h1 * h1, axis=-1, keepdims=True) + NORM_EPS)
        h1_ref[...] = h1
        u2_ref[...] = (h1 * r * g_ref[...]).astype(BF16)
        r2_ref[...] = r

    row = pl.BlockSpec((tm, D), lambda i: (i, 0))
    return _hosted(
        body, tasks, name=name, grid=(Tp // tm,),
        in_specs=[row, pl.BlockSpec((D, D), lambda i: (0, 0)), row, pl.BlockSpec((1, D), lambda i: (0, 0))],
        out_specs=[row, row, pl.BlockSpec((tm, 1), lambda i: (i, 0))],
        out_shape=[_sds((Tp, D), F32), _sds((Tp, D), BF16), _sds((Tp, 1), F32)],
        semantics=("parallel",), operands=(merged, w_out, h0, g2))


def _mlp2_loss(act, w2, h1, target, gf, n_meta, seq, name):
    Tp, D = h1.shape
    K = act.shape[1]
    tm = _tile(Tp, TILE["mlp2"])
    tk = min(K, MLP2_K)
    nk = K // tk

    rc = _tile(tm, EPILOGUE_ROWS)
    nt = Tp // tm
    last_rows = n_meta + seq - (nt - 1) * tm
    assert nt >= 2 and n_meta % 8 == 0 and 0 < last_rows <= tm and last_rows % 8 == 0

    def body(a_ref, w_ref, h1_hbm, t_hbm, g_ref, dh_ref, dhb_ref, loss_ref, dg_ref, h1_buf, t_buf, sems):
        i, k = pl.program_id(0), pl.program_id(1)
        tile_rows = pl.ds(pl.multiple_of(i * tm, tm), tm)
        fetch_h1 = pltpu.make_async_copy(h1_hbm.at[tile_rows, :], h1_buf, sems.at[0])
        fetch_t = [
            (i == 0, pltpu.make_async_copy(t_hbm.at[pl.ds(0, tm - n_meta), :], t_buf.at[pl.ds(n_meta, tm - n_meta), :],
                                           sems.at[1])),
            ((i > 0) & (i < nt - 1),
             pltpu.make_async_copy(t_hbm.at[pl.ds(pl.multiple_of(i * tm - n_meta, 8), tm), :], t_buf, sems.at[1])),
            (i == nt - 1, pltpu.make_async_copy(t_hbm.at[pl.ds((nt - 1) * tm - n_meta, last_rows), :],
                                                t_buf.at[pl.ds(0, last_rows), :], sems.at[1])),
        ]

        @pl.when(k == 0)
        def _():
            fetch_h1.start()
            dh_ref[...] = jnp.zeros_like(dh_ref)

        for cond, f in fetch_t:
            @pl.when(cond & (k == 0))
            def _(f=f):
                f.start()

        @pl.when((i == 0) & (k == 0))
        def _():
            loss_ref[...] = jnp.zeros_like(loss_ref)
            dg_ref[...] = jnp.zeros_like(dg_ref)
            t_buf[0:n_meta, :] = jnp.zeros((n_meta, D), F32)

        dh_ref[...] += jnp.dot(a_ref[...], w_ref[...], preferred_element_type=F32)

        for cond, f in fetch_t:
            @pl.when(cond & (k == nk - 1))
            def _(f=f):
                f.wait()

        @pl.when(k == nk - 1)
        def _():
            fetch_h1.wait()
            g = g_ref[...]

            def chunk(c, carry):
                loss_acc, dg_acc = carry
                rows = pl.ds(pl.multiple_of(c * rc, rc), rc)
                h2 = h1_buf[rows, :] + dh_ref[rows, :]
                r = lax.rsqrt(jnp.mean(h2 * h2, axis=-1, keepdims=True) + NORM_EPS)
                out = h2 * r * g
                row_id = i * tm + c * rc + lax.broadcasted_iota(jnp.int32, (rc, 1), 0)
                valid = (row_id >= n_meta) & (row_id < n_meta + seq)
                diff = jnp.where(valid, out - t_buf[rows, :], 0.0)
                dout = diff / D
                dog = dout * g
                dh = r * dog - h2 * (r * r * r * jnp.mean(dog * h2, axis=-1, keepdims=True))
                dh_ref[rows, :] = dh
                dhb_ref[rows, :] = dh.astype(BF16)
                loss_acc = loss_acc + 0.5 * jnp.sum(jnp.mean(diff * diff, axis=-1, keepdims=True), axis=0, keepdims=True)
                return loss_acc, dg_acc + jnp.sum(dout * (h2 * r), axis=0, keepdims=True)

            loss_sum, dg_sum = lax.fori_loop(0, tm // rc, chunk, (jnp.zeros((1, 1), F32), jnp.zeros((1, D), F32)))
            loss_ref[...] += loss_sum
            dg_ref[...] += dg_sum

    row = pl.BlockSpec((tm, D), lambda i, k: (i, 0))
    hbm = pl.BlockSpec(memory_space=pl.ANY)
    return _call(
        body, name=name, grid=(Tp // tm, nk),
        in_specs=[pl.BlockSpec((tm, tk), lambda i, k: (i, k)), pl.BlockSpec((tk, D), lambda i, k: (k, 0)),
                  hbm, hbm, pl.BlockSpec((1, D), lambda i, k: (0, 0))],
        out_specs=[row, row, pl.BlockSpec((8, 128), lambda i, k: (0, 0)), pl.BlockSpec((1, D), lambda i, k: (0, 0))],
        out_shape=[_sds((Tp, D), F32), _sds((Tp, D), BF16), _sds((8, 128), F32), _sds((1, D), F32)],
        scratch_shapes=[pltpu.VMEM((tm, D), F32), pltpu.VMEM((tm, D), F32), pltpu.SemaphoreType.DMA((2,))],
        compiler_params=_cp("arbitrary", "arbitrary"),
    )(act, w2, h1, target, gf)


def _dact_bwd(dh2b, w2_slots, a1, name):
    Tp, D = dh2b.shape
    S, n, _ = w2_slots.shape
    tm = _tile(Tp, TILE["dact"])

    def body(g_ref, w_ref, a1_ref, o_ref):
        dact = lax.dot_general(g_ref[...], w_ref[...], _NT, preferred_element_type=F32)
        o_ref[...] = (dact * (2.0 * jnp.maximum(a1_ref[...].astype(F32), 0.0))).astype(BF16)

    return _call(
        body, name=name, grid=(S, Tp // tm),
        in_specs=[pl.BlockSpec((tm, D), lambda j, i: (i, 0)), pl.BlockSpec((None, n, D), lambda j, i: (j, 0, 0)),
                  pl.BlockSpec((tm, n), lambda j, i: (i, j))],
        out_specs=pl.BlockSpec((tm, n), lambda j, i: (i, j)),
        out_shape=_sds((Tp, S * n), BF16),
        compiler_params=_cp("parallel", "parallel"),
    )(dh2b, w2_slots, a1)


def _weight_grad(a, g, blocks, block_a, name, tasks=(), part=(0, 1)):
    Tp, Ka = a.shape
    Ng = g.shape[1]
    p, parts = part
    assert parts == 1 or not block_a
    ka = Ka // blocks if block_a else Ka // parts
    ng = Ng if block_a else Ng // blocks
    tt = _tile(Tp, TILE["tn"])
    nt = Tp // tt

    def body(a_ref, g_ref, o_ref, acc_ref):
        t = pl.program_id(1)

        @pl.when(t == 0)
        def _():
            acc_ref[...] = jnp.zeros_like(acc_ref)

        acc_ref[...] += lax.dot_general(a_ref[...], g_ref[...], _TN, preferred_element_type=F32)

        @pl.when(t == nt - 1)
        def _():
            o_ref[...] = acc_ref[...].astype(o_ref.dtype)

    if block_a:
        a_spec = pl.BlockSpec((tt, ka), lambda j, t: (t, j))
        g_spec = pl.BlockSpec((tt, ng), lambda j, t: (t, 0))
    else:
        a_spec = pl.BlockSpec((tt, ka), lambda j, t: (t, p))
        g_spec = pl.BlockSpec((tt, ng), lambda j, t: (t, j))
    (dw,), extra = _hosted(
        body, tasks, name=name, grid=(blocks, nt),
        in_specs=[a_spec, g_spec],
        out_specs=[pl.BlockSpec((None, ka, ng), lambda j, t: (j, 0, 0))],
        out_shape=[_sds((blocks, ka, ng), BF16)],
        scratch_shapes=[pltpu.VMEM((ka, ng), F32)],
        semantics=("parallel", "arbitrary"), operands=(a, g))
    return dw, extra


def _nt_norm_bwd(dz, w_parts, dres, hin, rin, g, want_bf16, name, tasks=(), tiles=None, earlier=None):
    Tp, D = hin.shape
    P = len(w_parts)
    S, _, n = w_parts[0].shape
    K = S * P
    tm = _tile(Tp, TILE["nt"])
    t0, nt = tiles if tiles is not None else (0, Tp // tm)
    assert not (want_bf16 and earlier is not None)

    rc = _tile(tm, EPILOGUE_ROWS)

    def body(dz_ref, *rest):
        w_refs, (dres_hbm, h_hbm, r_ref, g_ref), rest = rest[:P], rest[P:P + 4], rest[P + 4:]
        if earlier is not None:
            _, dg0_ref, dh_ref, dg_ref, dres_buf, h_buf, sems = rest
        elif want_bf16:
            dh_ref, dhb_ref, dg_ref, dres_buf, h_buf, sems = rest
        else:
            dh_ref, dg_ref, dres_buf, h_buf, sems = rest
        i, k = pl.program_id(0), pl.program_id(1)
        tile_rows = pl.ds(pl.multiple_of((t0 + i) * tm, tm), tm)
        fetch = [pltpu.make_async_copy(dres_hbm.at[tile_rows, :], dres_buf, sems.at[0]),
                 pltpu.make_async_copy(h_hbm.at[tile_rows, :], h_buf, sems.at[1])]

        @pl.when(k == 0)
        def _():
            for f in fetch:
                f.start()
            dh_ref[...] = jnp.zeros_like(dh_ref)

        @pl.when((i == 0) & (k == 0))
        def _():
            dg_ref[...] = jnp.zeros_like(dg_ref) if earlier is None else dg0_ref[...]

        for q in range(P):
            @pl.when(k % P == q)
            def _(q=q):
                dh_ref[...] += lax.dot_general(dz_ref[...], w_refs[q][...], _NT, preferred_element_type=F32)

        @pl.when(k == K - 1)
        def _():
            for f in fetch:
                f.wait()
            g = g_ref[...]

            def chunk(c, dg_acc):
                rows = pl.ds(pl.multiple_of(c * rc, rc), rc)
                du = dh_ref[rows, :]
                h = h_buf[rows, :]
                r = r_ref[rows, :]
                dug = du * g
                dh = dres_buf[rows, :] + r * dug - h * (r * r * r * jnp.mean(dug * h, axis=-1, keepdims=True))
                dh_ref[rows, :] = dh
                if want_bf16:
                    dhb_ref[rows, :] = dh.astype(BF16)
                return dg_acc + jnp.sum(du * (h * r), axis=0, keepdims=True)

            dg_ref[...] += lax.fori_loop(0, tm // rc, chunk, jnp.zeros((1, D), F32))

    row = pl.BlockSpec((tm, D), lambda i, k: (t0 + i, 0))
    vec = pl.BlockSpec((1, D), lambda i, k: (0, 0))
    hbm = pl.BlockSpec(memory_space=pl.ANY)
    out_specs = [row] + ([row] if want_bf16 else []) + [vec]
    out_shape = [_sds((Tp, D), F32)] + ([_sds((Tp, D), BF16)] if want_bf16 else []) + [_sds((1, D), F32)]
    in_specs = ([pl.BlockSpec((tm, n), lambda i, k: (t0 + i, k))]
                + [pl.BlockSpec((None, D, n), lambda i, k: (k // P, 0, 0))] * P
                + [hbm, hbm, pl.BlockSpec((tm, 1), lambda i, k: (t0 + i, 0)), vec])
    operands = (dz, *w_parts, dres, hin, rin, g)
    aliases = {}
    if earlier is not None:
        in_specs += [hbm, vec]
        operands += tuple(earlier)
        aliases = {P + 5: 0}
    return _hosted(
        body, tasks, name=name, grid=(nt, K), in_specs=in_specs, out_specs=out_specs, out_shape=out_shape,
        scratch_shapes=[pltpu.VMEM((tm, D), F32), pltpu.VMEM((tm, D), F32), pltpu.SemaphoreType.DMA((2,))],
        semantics=("arbitrary", "arbitrary"), operands=operands, aliases=aliases)


def _dmerged_bwd(dh1b, w_out, name):
    Tp, D = dh1b.shape
    tm = _tile(Tp, TILE["dmerged"])

    def body(g_ref, w_ref, o_ref):
        o_ref[...] = lax.dot_general(g_ref[...], w_ref[...], _NT, preferred_element_type=F32)

    row = pl.BlockSpec((tm, D), lambda i: (i, 0))
    return _call(
        body, name=name, grid=(Tp // tm,),
        in_specs=[row, pl.BlockSpec((D, D), lambda i: (0, 0))],
        out_specs=row, out_shape=_sds((Tp, D), F32),
        compiler_params=_cp("parallel"),
    )(dh1b, w_out)


def _pool_bwd(dmerged, proj, y_pool, d_pool, scale, pool_w, name, tasks=()):
    Tp, D = dmerged.shape
    G, Cg, _ = pool_w.shape
    tm = _tile(Tp, TILE["pool"])
    nt = Tp // tm

    def body(dm_ref, gp_ref, y_ref, d_ref, sc_ref, w_ref, dproj_hbm, dw_ref, dsc_ref, next_ref, out_buf, out_sems):
        t = pl.program_id(0)
        tile = nt - 1 - t
        slot = t % 2
        dv_ref, dgp_ref = out_buf.at[slot, 0], out_buf.at[slot, 1]
        tile_rows = pl.ds(pl.multiple_of(tile * tm, tm), tm)

        def out_copies(s):
            return [pltpu.make_async_copy(out_buf.at[s, k], dproj_hbm.at[tile_rows, pl.ds(piece * D, D)],
                                          out_sems.at[2 * s + k]) for k, piece in enumerate((0, 3))]

        @pl.when(t >= 2)
        def _():
            for cp in out_copies(slot):
                cp.wait()

        @pl.when(t == 0)
        def _():
            next_ref[...] = jnp.zeros_like(next_ref)
            dw_ref[...] = jnp.zeros_like(dw_ref)
            dsc_ref[...] = jnp.zeros_like(dsc_ref)

        rows = _row_ids(tile, tm)
        dm = dm_ref[...]
        y = y_ref[...]
        sc = sc_ref[...]
        sg = _sigmoid(gp_ref[...])
        dpo = dm * sg
        dgp_ref[...] = (dm * (y * sc) * sg * (1.0 - sg)).astype(BF16)
        dsc_ref[...] += jnp.sum(dpo * y, axis=0, keepdims=True)
        dyb = (dpo * sc).astype(BF16)
        for g, win in enumerate(POOL_WINDOWS):
            cols = slice(g * Cg, (g + 1) * Cg)
            dy = dyb[:, cols]
            dd = lax.dot_general(dy, w_ref[g], _NT, preferred_element_type=F32)
            dw_ref[g] += lax.dot_general(d_ref[:, cols], dy, _TN, preferred_element_type=F32)
            q = dd / jnp.minimum(rows + 1, win).astype(F32)
            s = jnp.concatenate([q, next_ref[:, cols]], axis=0)
            k = 1
            while k < win:
                s = s + pltpu.roll(s, s.shape[0] - k, axis=0)
                k *= 2
            dv_ref[:, cols] = (s[:tm] - dd).astype(BF16)
            next_ref[:, cols] = q[:MAX_WINDOW]
        for cp in out_copies(slot):
            cp.start()

        @pl.when(t == nt - 1)
        def _():
            for cp in out_copies(slot) + (out_copies(1 - slot) if nt > 1 else []):
                cp.wait()

    row = pl.BlockSpec((tm, D), lambda t: (nt - 1 - t, 0))
    return _hosted(
        body, tasks, name=name, grid=(nt,),
        in_specs=[row, pl.BlockSpec((tm, D), lambda t: (nt - 1 - t, 3)), row, row,
                  pl.BlockSpec((1, D), lambda t: (0, 0)), pl.BlockSpec((G, Cg, Cg), lambda t: (0, 0, 0))],
        out_specs=[pl.BlockSpec(memory_space=pl.ANY), pl.BlockSpec((G, Cg, Cg), lambda t: (0, 0, 0)),
                   pl.BlockSpec((1, D), lambda t: (0, 0))],
        out_shape=[_sds((Tp, proj.shape[1]), BF16), _sds((G, Cg, Cg), F32), _sds((1, D), F32)],
        scratch_shapes=[pltpu.VMEM((MAX_WINDOW, D), F32), pltpu.VMEM((2, 2, tm, D), BF16),
                        pltpu.SemaphoreType.DMA((4,))],
        semantics=("arbitrary",), operands=(dmerged, proj, y_pool, d_pool, scale, pool_w))


LRU_SMALL_ROWS = 8


def _lru_bwd(dmerged, proj, xc, r_gate, i_gate, a_gate, mult_gate, hs, lam, conv_w, wa, wx, dproj, name, tasks=()):
    Tp, D = dmerged.shape
    H, hd, _ = wa.shape
    tm = _tile(Tp, TILE["lru"])
    nt = Tp // tm
    nb = D // hd
    halo_blocks = tm // HALO

    def body(dm_ref, vl_ref, vg_ref, gl_ref, xc_ref, r_ref, i_ref, a_ref, mult_ref, hs_ref, hsp_ref, lam_ref, cw_ref,
             wa_ref, wx_ref, _, dproj_hbm, dwa_ref, dwx_ref, small_ref,
             mu_next_ref, dxc_next_ref, q_s, mu_s, out_buf, out_sems):
        h_id, t = pl.program_id(0), pl.program_id(1)
        tile = nt - 1 - t
        step = h_id * nt + t
        slot = step % 2
        dvl_ref, dvg_ref, dgl_ref = out_buf.at[slot, 0], out_buf.at[slot, 1], out_buf.at[slot, 2]
        tile_rows = pl.ds(pl.multiple_of(tile * tm, tm), tm)

        def out_copies(s):
            return [pltpu.make_async_copy(
                out_buf.at[s, k], dproj_hbm.at[tile_rows, pl.ds(pl.multiple_of((piece * nb + h_id) * hd, hd), hd)],
                out_sems.at[3 * s + k]) for k, piece in enumerate((1, 2, 4))]

        @pl.when(step >= 2)
        def _():
            for cp in out_copies(slot):
                cp.wait()

        @pl.when(t == 0)
        def _():
            mu_next_ref[...] = jnp.zeros_like(mu_next_ref)
            dxc_next_ref[...] = jnp.zeros_like(dxc_next_ref)
            dwa_ref[...] = jnp.zeros_like(dwa_ref)
            dwx_ref[...] = jnp.zeros_like(dwx_ref)
            small_ref[...] = jnp.zeros_like(small_ref)

        first = tile == 0
        dm = dm_ref[...]
        hs_t = hs_ref[...]
        xc_t = xc_ref[...]
        r = r_ref[...]
        i = i_ref[...]
        lam_v = lam_ref[...]
        sp = _softplus(-lam_v)
        a = a_ref[...]
        mult = mult_ref[...]

        sg = _sigmoid(gl_ref[...])
        ge, dge = _gelu_and_grad(vg_ref[...])
        dlo = dm * sg
        dgl_ref[...] = (dm * (hs_t * ge) * sg * (1.0 - sg)).astype(BF16)
        dvg_ref[...] = (dlo * hs_t * dge).astype(BF16)
        dhs = dlo * ge

        q_s[...] = a * dhs
        mu_first = _scan_chunks(a_ref, q_s, mu_s, mu_next_ref[0:1, :], tm, reverse=True)
        lam_t = dhs + _shift_up(mu_s[...], mu_next_ref[...], 1)
        mu_next_ref[...] = jnp.broadcast_to(mu_first, mu_next_ref.shape)

        h_prev = _shift_down(jnp.where(first, 0.0, hsp_ref[...]), hs_t, 1)
        da = lam_t * h_prev
        dmult = lam_t * (i * xc_t)
        di = lam_t * mult * xc_t
        dxc = lam_t * mult * i
        dlog_a = da * a - dmult * (a * a) / mult
        dr = dlog_a * (-LRU_C * sp)
        dlam_rows = dlog_a * (-LRU_C * r)
        dza = dr * r * (1.0 - r)
        dzx = di * i * (1.0 - i)
        dzab, dzxb = dza.astype(BF16), dzx.astype(BF16)
        xcb = xc_t.astype(BF16)
        dxc = dxc + lax.dot_general(dzab, wa_ref[...], _NT, preferred_element_type=F32)
        dxc = dxc + lax.dot_general(dzxb, wx_ref[...], _NT, preferred_element_type=F32)
        dwa_ref[...] += lax.dot_general(xcb, dzab, _TN, preferred_element_type=F32)
        dwx_ref[...] += lax.dot_general(xcb, dzxb, _TN, preferred_element_type=F32)

        dxc_next = dxc_next_ref[...]
        taps = [_shift_up(dxc, dxc_next, CONV_WIDTH - 1 - k) for k in range(CONV_WIDTH)]
        dv = jnp.zeros_like(dxc)
        for k in range(CONV_WIDTH):
            dv = dv + cw_ref[k:k + 1, :] * taps[k]
        dvl_ref[...] = dv.astype(BF16)
        dxc_next_ref[...] = dxc[:HALO, :]

        v_t = vl_ref[...]
        small = [jnp.sum(dza, axis=0, keepdims=True), jnp.sum(dzx, axis=0, keepdims=True),
                 jnp.sum(dlam_rows, axis=0, keepdims=True) * (-_sigmoid(-lam_v)),
                 jnp.sum(dxc, axis=0, keepdims=True)]
        for k in range(CONV_WIDTH):
            small.append(jnp.sum(taps[k] * v_t, axis=0, keepdims=True))
        for k, row in enumerate(small):
            small_ref[k:k + 1, :] += row
        for cp in out_copies(slot):
            cp.start()

        @pl.when(step == H * nt - 1)
        def _():
            for cp in out_copies(slot) + (out_copies(1 - slot) if H * nt > 1 else []):
                cp.wait()

    def piece(p):
        return pl.BlockSpec((tm, hd), lambda h, t: (nt - 1 - t, p * nb + h))

    def halo(p):
        return pl.BlockSpec((HALO, hd), lambda h, t: (jnp.maximum((nt - 1 - t) * halo_blocks - 1, 0), p * nb + h))

    blk = pl.BlockSpec((tm, hd), lambda h, t: (nt - 1 - t, h))
    vec = pl.BlockSpec((1, hd), lambda h, t: (0, h))
    mat = pl.BlockSpec((None, hd, hd), lambda h, t: (h, 0, 0))
    return _hosted(
        body, tasks, name=name, grid=(H, nt),
        in_specs=[blk, piece(1), piece(2), piece(4), blk, blk, blk, blk, blk, blk, halo(0), vec,
                  pl.BlockSpec((CONV_WIDTH, hd), lambda h, t: (0, h)), mat, mat, pl.BlockSpec(memory_space=pl.ANY)],
        out_specs=[pl.BlockSpec(memory_space=pl.ANY), mat, mat,
                   pl.BlockSpec((None, LRU_SMALL_ROWS, hd), lambda h, t: (h, 0, 0))],
        out_shape=[_sds(dproj.shape, BF16)] + [_sds((H, hd, hd), F32)] * 2 + [_sds((H, LRU_SMALL_ROWS, hd), F32)],
        scratch_shapes=[pltpu.VMEM((HALO, hd), F32), pltpu.VMEM((HALO, hd), F32),
                        pltpu.VMEM((tm, hd), F32), pltpu.VMEM((tm, hd), F32),
                        pltpu.VMEM((2, 3, tm, hd), BF16), pltpu.SemaphoreType.DMA((6,))],
        semantics=("arbitrary", "arbitrary"), aliases={15: 0},
        operands=(dmerged, proj, proj, proj, xc, r_gate, i_gate, a_gate, mult_gate, hs, hs, lam, conv_w, wa, wx,
                  dproj))


def _adamw(w, g, m, v):
    m = ADAM_B1 * m + (1.0 - ADAM_B1) * g
    v = ADAM_B2 * v + (1.0 - ADAM_B2) * (g * g)
    m_hat = m / (1.0 - ADAM_B1 ** ADAM_STEP)
    v_hat = v / (1.0 - ADAM_B2 ** ADAM_STEP)
    delta = -ADAM_LR * (m_hat / (jnp.sqrt(v_hat) + ADAM_EPS) + ADAM_WD * w)
    return delta, m, v


def _reduce_update(pair_sums, chip_sums, w, m, v, chip_slot, name, part=(0, 1), earlier=None):
    R, C = pair_sums.shape[1:]
    p, parts = part
    tr = _tile(R, TILE["update"])
    nblk = R // tr

    def body(slot_ref, own_ref, got_ref, w_ref, m_ref, v_ref, *rest):
        g_out, d_out, m_out, v_out = rest[-4:]
        g = own_ref[...].astype(F32)
        for k in range(3):
            g = g + got_ref[k].astype(F32)
        d, m_new, v_new = _adamw(w_ref[...], g, m_ref[...], v_ref[...])
        g_out[...] = g
        d_out[...] = d
        m_out[...] = m_new
        v_out[...] = v_new

    blk = pl.BlockSpec((tr, C), lambda i, s: (p * nblk + i, 0))
    in_specs = [pl.BlockSpec((None, tr, C), lambda i, s: (s[0], i, 0)),
                pl.BlockSpec((3, tr, C), lambda i, s: (0, i, 0)), blk, blk, blk]
    operands = (chip_slot, pair_sums, chip_sums, w, m, v)
    aliases = {}
    if earlier is not None:
        in_specs += [pl.BlockSpec(memory_space=pl.ANY)] * 4
        operands += tuple(earlier)
        aliases = {6 + k: k for k in range(4)}
    return _call(
        body, name=name,
        grid_spec=pltpu.PrefetchScalarGridSpec(
            num_scalar_prefetch=1, grid=(R // tr,), in_specs=in_specs, out_specs=[blk] * 4),
        out_shape=[_sds((parts * R, C), F32)] * 4,
        input_output_aliases=aliases,
        compiler_params=_cp("parallel"),
    )(*operands)


def _small_updates(small_sum, bias_grads, col0, specs, name):
    n = len(specs)

    def body(col_ref, ss_ref, bg_ref, *refs):
        ins, outs = refs[:3 * n], refs[3 * n:]
        for k, (w, _, _, kind, r0) in enumerate(specs):
            nr, nc = w.shape
            if kind == "rows":
                g = ss_ref[r0:r0 + nr, :]
            elif kind == "cols":
                g = ss_ref[r0:r0 + nr, pl.ds(pl.multiple_of(col_ref[0], nc), nc)]
            else:
                g = bg_ref[r0:r0 + nr, :]
            d, m_new, v_new = _adamw(ins[3 * k][...], g, ins[3 * k + 1][...], ins[3 * k + 2][...])
            for o, val in zip(outs[4 * k:4 * k + 4], (g, d, m_new, v_new)):
                o[...] = val

    def whole(a):
        return pl.BlockSpec(a.shape, lambda i, c: (0,) * a.ndim)

    arrays = [a for (w, m, v, _, _) in specs for a in (w, m, v)]
    res = _call(
        body, name=name,
        grid_spec=pltpu.PrefetchScalarGridSpec(
            num_scalar_prefetch=1, grid=(1,),
            in_specs=[whole(small_sum), whole(bias_grads)] + [whole(a) for a in arrays],
            out_specs=[whole(w) for (w, _, _, _, _) in specs for _ in range(4)]),
        out_shape=[_sds(w.shape, F32) for (w, _, _, _, _) in specs for _ in range(4)],
        compiler_params=_cp("arbitrary"),
    )(col0, small_sum, bias_grads, *arrays)
    return [tuple(res[4 * k:4 * k + 4]) for k in range(n)]


def _slots_from_rows(full, lead):
    L, R, C = full.shape
    r = R // N_DEV
    return full.reshape(L, N_DEV, r, C).transpose(1, 0, 2, 3).reshape(N_DEV, L * r, C)


def _rows_from_slots(slots, lead):
    _, LR, C = slots.shape
    r = LR // lead
    return slots.reshape(N_DEV, lead, r, C).transpose(1, 0, 2, 3).reshape(lead, N_DEV * r, C)


def kernel(x, meta_tokens, norm1_g, w_in, pool_w, pool_scale, conv_w, conv_b, gate_a_w, gate_a_b, gate_x_w, gate_x_b, lru_lambda, w_out, norm2_g, mlp_w1, mlp_w2, final_g, loss_target, m_meta_tokens, m_norm1_g, m_w_in, m_pool_w, m_pool_scale, m_conv_w, m_conv_b, m_gate_a_w, m_gate_a_b, m_gate_x_w, m_gate_x_b, m_lru_lambda, m_w_out, m_norm2_g, m_mlp_w1, m_mlp_w2, m_final_g, v_meta_tokens, v_norm1_g, v_w_in, v_pool_w, v_pool_scale, v_conv_w, v_conv_b, v_gate_a_w, v_gate_a_b, v_gate_x_w, v_gate_x_b, v_lru_lambda, v_w_out, v_norm2_g, v_mlp_w1, v_mlp_w2, v_final_g):
    seq, D = x.shape[1], x.shape[2]
    n_meta = meta_tokens.shape[0]
    G, Cg = pool_w.shape[1], pool_w.shape[3]
    H, hd = gate_a_w.shape[1], gate_a_w.shape[3]
    T = n_meta + seq
    Tp = -(-T // ROW_ALIGN) * ROW_ALIGN
    ix, iy, ic = _pos()
    me = 4 * ix + 2 * iy + ic
    core = jnp.reshape(ic, (1,)).astype(jnp.int32)
    chip_slot = jnp.reshape(2 * ix + iy, (1,)).astype(jnp.int32)

    w_in_l, w1_l, w2_l, w_out_l = w_in[0], mlp_w1[0], mlp_w2[0], w_out[0]
    pool_l = pool_w[0].reshape(G * (Cg // N_DEV), Cg)
    wa_l = gate_a_w[0].reshape(H * (hd // N_DEV), hd)
    wx_l = gate_x_w[0].reshape(H * (hd // N_DEV), hd)
    small_params = jnp.concatenate(
        [meta_tokens, conv_w[0], jnp.zeros((4, D // N_DEV), F32)], axis=0)
    biases = jnp.concatenate([gate_a_b[0], gate_x_b[0]], axis=0)
    (w_in_g, pool_g, wa_g, wx_g, small_g, bias_g) = _all_gather_relay(
        [w_in_l.astype(BF16), pool_l.astype(BF16), wa_l.astype(BF16), wx_l.astype(BF16), small_params, biases],
        [W_IN_CHUNKS, 1, 1, 1, 1, 1], "gather_first")
    pool_full = _rows_from_slots(pool_g, G)
    wa_full = _rows_from_slots(wa_g, H)
    wx_full = _rows_from_slots(wx_g, H)
    small_full = small_g.transpose(1, 0, 2).reshape(n_meta + 8, D)
    meta_full = small_full[:n_meta]
    conv_full = small_full[n_meta:n_meta + CONV_WIDTH]
    bias_full = bias_g.transpose(1, 0, 2).reshape(2 * H, hd)
    ba_full, bx_full = bias_full[:H], bias_full[H:]

    h0 = jnp.concatenate([meta_full, x[0], jnp.zeros((Tp - T, D), F32)], axis=0)
    u, r1 = _norm_fwd(h0, norm1_g, "norm1")
    proj, ((w_out_g,), (w1_ici,)) = _proj_fwd(
        u, w_in_g, "proj", tasks=[_AgFull(w_out_l.astype(BF16)), _AgFull(w1_l.astype(BF16), forward_here=False)])
    w_in_parts = [w_in_g]
    d_pool, y_pool = _pool_fwd(proj, pool_full, "pool_fwd")
    (xc, r_gate, i_gate, a_gate, mult_gate, hs, merged), ((w1_g,), (w2_ici,)) = _lru_fwd(
        proj, y_pool, pool_scale, conv_full, conv_b, wa_full, ba_full, wx_full, bx_full, lru_lambda, "lru_fwd",
        tasks=[_AgForward(w1_ici), _AgFull(w2_l.astype(BF16), forward_here=False)])
    w_out_full = w_out_g.reshape(D, D)
    (h1, u2, r2), ((w2_g,),) = _wout_norm_fwd(merged, w_out_full, h0, norm2_g, "wout_norm2", tasks=[_AgForward(w2_ici)])
    (act, a1), _ = _mlp1_fwd(u2, w1_g, "mlp1")
    dh2, dh2b, loss_tile, d_final_g = _mlp2_loss(
        act, w2_g.reshape(-1, D), h1, loss_target[0], final_g.reshape(1, D), n_meta, seq, "mlp2_loss")

    def pair(part, got, tag):
        return _pair_sum(part, got, core, "pair_sum_" + tag)

    d_a1 = _dact_bwd(dh2b, w2_g, a1, "dact")
    dw2_p, _ = _weight_grad(act, dh2b, N_DEV, True, "dw2")
    dw1_p, ((dw2_got,),) = _weight_grad(u2, d_a1, N_DEV, False, "dw1", tasks=[_RsSibling(dw2_p)])
    dw2_pair = pair(dw2_p, dw2_got, "w2")
    (dh1, dh1b, d_norm2_g), ((dw2_chips,), (dw1_got,)) = _nt_norm_bwd(
        d_a1, [w1_g], dh2, h1, r2, norm2_g, True, "du2_norm2", tasks=[_RsChips(dw2_pair), _RsSibling(dw1_p)])
    dw1_pair = pair(dw1_p, dw1_got, "w1")
    dmerged = _dmerged_bwd(dh1b, w_out_full, "dmerged")
    dwout_p, _ = _weight_grad(merged, dh1b, 2, True, "dwout")
    dwout_p = dwout_p.reshape(N_DEV, D // N_DEV, D)
    (dproj_pool, dpool_full, d_scale), ((dwout_got,),) = _pool_bwd(
        dmerged, proj, y_pool, d_pool, pool_scale, pool_full, "pool_bwd", tasks=[_RsSibling(dwout_p)])
    dwout_pair = pair(dwout_p, dwout_got, "wout")
    (dproj, dwa_full, dwx_full, lru_small), ((dw1_chips,), (dwout_chips,)) = _lru_bwd(
        dmerged, proj, xc, r_gate, i_gate, a_gate, mult_gate, hs, lru_lambda, conv_full, wa_full, wx_full, dproj_pool,
        "lru_bwd", tasks=[_RsChips(dw1_pair), _RsChips(dwout_pair)])
    dwin_a, _ = _weight_grad(u, dproj, N_DEV, False, "dwin_a", part=(0, 2))
    dwin_b, ((dwin_a_got,),) = _weight_grad(u, dproj, N_DEV, False, "dwin_b", part=(1, 2), tasks=[_RsSibling(dwin_a)])
    dwin_a_pair = pair(dwin_a, dwin_a_got, "win_a")
    dpool_p = _slots_from_rows(dpool_full, G).astype(BF16)
    dwa_p = _slots_from_rows(dwa_full, H).astype(BF16)
    dwx_p = _slots_from_rows(dwx_full, H).astype(BF16)
    late = [dwin_b, dpool_p, dwa_p, dwx_p]
    n_tiles = Tp // _tile(Tp, TILE["nt"])
    n_first = max(1, n_tiles // 2)
    (dh0_a, d_norm1_a), ((dwin_a_chips,), *late_got) = _nt_norm_bwd(
        dproj, w_in_parts, dh1, h0, r1, norm1_g, False, "du_norm1_a",
        tasks=[_RsChips(dwin_a_pair)] + [_RsSibling(p) for p in late], tiles=(0, n_first))
    late_pair = [pair(p, g[0], "late%d" % k) for k, (p, g) in enumerate(zip(late, late_got))]
    (dh0, d_norm1_g), late_chips = _nt_norm_bwd(
        dproj, w_in_parts, dh1, h0, r1, norm1_g, False, "du_norm1_b", tasks=[_RsChips(p) for p in late_pair],
        tiles=(n_first, n_tiles - n_first), earlier=(dh0_a, d_norm1_a))
    grad_x = dh0[n_meta:T][None]

    pair_sums = [dw2_pair, dw1_pair, dwout_pair] + late_pair[1:]
    chip_sums = [dw2_chips, dw1_chips, dwout_chips] + [c[0] for c in late_chips[1:]]
    big = {}
    names = ["mlp_w2", "mlp_w1", "w_out", "pool_w", "gate_a_w", "gate_x_w"]
    trip = {"mlp_w2": (mlp_w2, m_mlp_w2, v_mlp_w2), "mlp_w1": (mlp_w1, m_mlp_w1, v_mlp_w1),
            "w_out": (w_out, m_w_out, v_w_out),
            "pool_w": (pool_w, m_pool_w, v_pool_w), "gate_a_w": (gate_a_w, m_gate_a_w, v_gate_a_w),
            "gate_x_w": (gate_x_w, m_gate_x_w, v_gate_x_w)}
    for k, nm in enumerate(names):
        w_, m_, v_ = trip[nm]
        shape2 = pair_sums[k].shape[1:]
        outs = _reduce_update(pair_sums[k], chip_sums[k], w_.reshape(shape2), m_.reshape(shape2), v_.reshape(shape2),
                              chip_slot, "update_" + nm)
        big[nm] = [o.reshape(w_.shape) for o in outs]
    win2 = [a_[0] for a_ in (w_in, m_w_in, v_w_in)]
    win_a = _reduce_update(dwin_a_pair, dwin_a_chips, *win2, chip_slot, "update_w_in_a", part=(0, 2))
    win_b = _reduce_update(late_pair[0], late_chips[0][0], *win2, chip_slot, "update_w_in_b", part=(1, 2), earlier=win_a)
    big["w_in"] = [o.reshape(w_in.shape) for o in win_b]
    names = names + ["w_in"]

    lru_rows = lru_small.transpose(1, 0, 2).reshape(LRU_SMALL_ROWS, D)
    small_part = jnp.concatenate(
        [dh0[:n_meta], d_norm1_g, d_scale, d_norm2_g, d_final_g, lru_rows, jnp.zeros((4, D), F32)], axis=0)
    (small_all,) = _all_gather([small_part], "gather_small_grads")
    small_sum = _sum_slots(small_all, "sum_small_grads")
    o = n_meta
    dcol, hcol = D // N_DEV, hd // N_DEV
    bias_grads = jnp.concatenate(
        [lax.dynamic_slice_in_dim(small_sum[o + 4 + k].reshape(H, hd), me * hcol, hcol, axis=1) for k in range(2)], axis=0)
    col0 = jnp.reshape(me * dcol, (1,)).astype(jnp.int32)
    small = {"meta_tokens": (meta_tokens, m_meta_tokens, v_meta_tokens, "cols", 0),
             "norm1_g": (norm1_g, m_norm1_g, v_norm1_g, "rows", o),
             "pool_scale": (pool_scale, m_pool_scale, v_pool_scale, "rows", o + 1),
             "norm2_g": (norm2_g, m_norm2_g, v_norm2_g, "rows", o + 2),
             "final_g": (final_g, m_final_g, v_final_g, "rows", o + 3),
             "gate_a_b": (gate_a_b, m_gate_a_b, v_gate_a_b, "bias", 0),
             "gate_x_b": (gate_x_b, m_gate_x_b, v_gate_x_b, "bias", H),
             "lru_lambda": (lru_lambda, m_lru_lambda, v_lru_lambda, "rows", o + 6),
             "conv_b": (conv_b, m_conv_b, v_conv_b, "rows", o + 7),
             "conv_w": (conv_w, m_conv_w, v_conv_w, "cols", o + 8)}

    def two_d(a):
        return a.reshape(-1, a.shape[-1])

    small_out = _small_updates(
        small_sum, bias_grads, col0,
        [(two_d(w_), two_d(m_), two_d(v_), kind, row) for (w_, m_, v_, kind, row) in small.values()], "update_small")
    small_res = {nm: [r.reshape(spec[0].shape) for r in res] for (nm, spec), res in zip(small.items(), small_out)}

    def leaves(kind):
        out = {nm: res[kind] for nm, res in small_res.items()}
        for nm in names:
            out[nm] = big[nm][kind]
        order = ["meta_tokens", "norm1_g", "w_in", "pool_w", "pool_scale", "conv_w", "conv_b", "gate_a_w", "gate_a_b",
                 "gate_x_w", "gate_x_b", "lru_lambda", "w_out", "norm2_g", "mlp_w1", "mlp_w2", "final_g"]
        return [out[nm] for nm in order]

    loss = lax.psum(loss_tile[0, 0], ("x", "y", "c"))
    return (loss, grad_x, *leaves(0), *leaves(1), *leaves(2), *leaves(3))
```

```python
import functools

import jax
import jax.numpy as jnp
from jax import lax
from jax.experimental import pallas as pl
from jax.experimental.pallas import tpu as pltpu

F32 = jnp.float32
BF16 = jnp.bfloat16
MESH = pl.DeviceIdType.MESH
N_DEV = 8
POOL_WINDOWS = (2, 4, 8, 16)
MAX_WINDOW = 16
CONV_WIDTH = 4
HALO = 8
LRU_C = 8.0
NORM_EPS = 1e-6
ADAM_LR, ADAM_B1, ADAM_B2, ADAM_EPS, ADAM_WD, ADAM_STEP = 0.001, 0.9, 0.999, 1e-08, 0.01, 10
ROW_ALIGN = 128
VMEM_LIMIT = 56 << 20
TILE = dict(norm=384, proj=1408, pool=384, lru=704, wout=384, mlp1=1408, mlp2=704, dact=1408, tn=2112,
            nt=704, dmerged=704, update=256, pair=1024)
MLP2_K = 1024
EPILOGUE_ROWS = 176
SCAN_GROUP = 4
W_IN_CHUNKS = 5
MID_STEP_PERCENT = 88

_NT = (((1,), (1,)), ((), ()))
_TN = (((0,), (0,)), ((), ()))


def _call(body, **kw):
    return pl.pallas_call(body, **kw)


def _cp(*sem):
    return pltpu.CompilerParams(dimension_semantics=sem, vmem_limit_bytes=VMEM_LIMIT)


def _tile(total, pref):
    best = None
    for t in range(16, min(total, pref) + 1, 16):
        if total % t == 0:
            best = t
    assert best is not None, (total, pref)
    return best


def _sds(shape, dtype):
    return jax.ShapeDtypeStruct(shape, dtype)


def _pos():
    return lax.axis_index("x"), lax.axis_index("y"), lax.axis_index("c")


def _all_gather(shards, name):
    n = len(shards)

    def body(*refs):
        ins, outs = refs[:n], refs[n:2 * n]
        send_sems, recv_sems, local_sems = refs[2 * n:]
        x, y, c = _pos()
        me, sib = (x, y, c), (x, y, 1 - c)
        chips = [(1 - x, y), (x, 1 - y), (1 - x, 1 - y)]

        def slot(p):
            return 4 * p[0] + 2 * p[1] + p[2]

        def copy(a, k, block, to, src=None):
            dst = outs[a].at[slot(block)]
            return pltpu.make_async_remote_copy(
                src_ref=dst if src is None else src, dst_ref=dst,
                send_sem=send_sems.at[7 * a + k], recv_sem=recv_sems.at[7 * a + k],
                device_id=to, device_id_type=MESH)

        mine = [pltpu.make_async_copy(ins[a], outs[a].at[slot(me)], local_sems.at[a]) for a in range(n)]
        for m in mine:
            m.start()
        first = []
        for a in range(n):
            first.append(copy(a, 0, me, sib, src=ins[a]))
            first += [copy(a, 1 + j, me, (*chip, c), src=ins[a]) for j, chip in enumerate(chips)]
        for cp in first:
            cp.start()
        passed = []
        for a in range(n):
            for j, chip in enumerate(chips):
                copy(a, 1 + j, (*chip, c), me).wait_recv()
                fwd = copy(a, 4 + j, (*chip, c), sib)
                fwd.start()
                passed.append(fwd)
        for a in range(n):
            copy(a, 0, sib, me).wait_recv()
            for j, chip in enumerate(chips):
                copy(a, 4 + j, (*chip, 1 - c), me).wait_recv()
        for cp in first + passed:
            cp.wait_send()
        for m in mine:
            m.wait()

    hbm = pl.BlockSpec(memory_space=pl.ANY)
    return _call(
        body, name=name,
        out_shape=[_sds((N_DEV,) + s.shape, s.dtype) for s in shards],
        in_specs=[hbm] * n, out_specs=[hbm] * n,
        scratch_shapes=[pltpu.SemaphoreType.DMA((7 * n,)), pltpu.SemaphoreType.DMA((7 * n,)),
                        pltpu.SemaphoreType.DMA((n,))],
    )(*shards)


def _all_gather_relay(shards, chunks, name):
    n = len(shards)
    units = []
    for a, s in enumerate(shards):
        if chunks[a] == 1:
            units.append((a, None, None))
        else:
            w = s.shape[-1] // chunks[a]
            units += [(a, q * w, w) for q in range(chunks[a])]
    nu = len(units)

    def body(*refs):
        ins, outs = refs[:n], refs[n:2 * n]
        send_sems, recv_sems, local_sems = refs[2 * n:]
        x, y, c = _pos()
        me, sib = (x, y, c), (x, y, 1 - c)
        x_nbr, y_nbr, diag = (1 - x, y, c), (x, 1 - y, c), (1 - x, 1 - y, c)
        came_from = (x + (1 - c) * (1 - 2 * x), y + c * (1 - 2 * y), c)
        pass_to = (x + c * (1 - 2 * x), y + (1 - c) * (1 - 2 * y), c)

        def slot(p):
            return 4 * p[0] + 2 * p[1] + p[2]

        def src_view(u):
            a, c0, w = units[u]
            return ins[a] if c0 is None else ins[a].at[:, pl.ds(c0, w)]

        def dst_view(u, p):
            a, c0, w = units[u]
            return outs[a].at[slot(p)] if c0 is None else outs[a].at[slot(p), :, pl.ds(c0, w)]

        def copy(u, k, block, to, from_shard=False):
            dst = dst_view(u, block)
            return pltpu.make_async_remote_copy(
                src_ref=src_view(u) if from_shard else dst, dst_ref=dst,
                send_sem=send_sems.at[7 * u + k], recv_sem=recv_sems.at[7 * u + k],
                device_id=to, device_id_type=MESH)

        mine = [pltpu.make_async_copy(src_view(u), dst_view(u, me), local_sems.at[u]) for u in range(nu)]
        for m in mine:
            m.start()
        sent = []
        for u in range(nu):
            sent += [copy(u, 0, me, sib, True), copy(u, 1, me, x_nbr, True), copy(u, 2, me, y_nbr, True)]
        for cp in sent:
            cp.start()
        for u in range(nu):
            copy(u, 1, x_nbr, me).wait_recv()
            copy(u, 2, y_nbr, me).wait_recv()
            sent += [copy(u, 3, came_from, pass_to), copy(u, 4, x_nbr, sib), copy(u, 5, y_nbr, sib)]
            for cp in sent[-3:]:
                cp.start()
        for u in range(nu):
            copy(u, 3, diag, me).wait_recv()
            sent.append(copy(u, 6, diag, sib))
            sent[-1].start()
        for u in range(nu):
            copy(u, 0, sib, me).wait_recv()
            for k, p in ((4, x_nbr), (5, y_nbr), (6, diag)):
                copy(u, k, (p[0], p[1], 1 - c), me).wait_recv()
        for cp in sent:
            cp.wait_send()
        for m in mine:
            m.wait()

    hbm = pl.BlockSpec(memory_space=pl.ANY)
    return _call(
        body, name=name,
        out_shape=[_sds((N_DEV,) + s.shape, s.dtype) for s in shards],
        in_specs=[hbm] * n, out_specs=[hbm] * n,
        scratch_shapes=[pltpu.SemaphoreType.DMA((7 * nu,)), pltpu.SemaphoreType.DMA((7 * nu,)),
                        pltpu.SemaphoreType.DMA((nu,))],
    )(*shards)


def _other_chips(x, y):
    return [(1 - x, y), (x, 1 - y), (1 - x, 1 - y)]


class _AgFull:
    n_sem, n_local = 7, 1

    def __init__(self, shard, forward_here=True):
        self.forward_here = forward_here
        self.ins = [shard]
        self.out_shapes = [_sds((N_DEV,) + shard.shape, shard.dtype)]
        self.aliases = []

    def _peers(self):
        x, y, c = _pos()
        return [(x, y, 1 - c)] + [(*chip, c) for chip in _other_chips(x, y)]

    def _sends(self, ins, outs, sems):
        send, recv, _, base, _ = sems
        x, y, c = _pos()
        mine = outs[0].at[4 * x + 2 * y + c]
        return [pltpu.make_async_remote_copy(src_ref=ins[0], dst_ref=mine, send_sem=send.at[base + k],
                                             recv_sem=recv.at[base + k], device_id=p, device_id_type=MESH)
                for k, p in enumerate(self._peers())]

    def _arrivals(self, outs, sems):
        send, recv, _, base, _ = sems
        res = []
        for k, p in enumerate(self._peers()):
            blk = outs[0].at[4 * p[0] + 2 * p[1] + p[2]]
            res.append(pltpu.make_async_remote_copy(src_ref=blk, dst_ref=blk, send_sem=send.at[base + k],
                                                    recv_sem=recv.at[base + k], device_id=p, device_id_type=MESH))
        return res

    def _own(self, ins, outs, sems):
        x, y, c = _pos()
        return pltpu.make_async_copy(ins[0], outs[0].at[4 * x + 2 * y + c], sems[2].at[sems[4]])

    def _forwards(self, outs, sems, core_of_block):
        return _forward_copies(outs[0], sems[0], sems[1], sems[3] + 4, core_of_block)

    def start(self, ins, outs, sems):
        self._own(ins, outs, sems).start()
        for cp in self._sends(ins, outs, sems):
            cp.start()

    def mid(self, ins, outs, sems):
        if self.forward_here:
            for cp in self._arrivals(outs, sems)[1:]:
                cp.wait_recv()
            for cp in self._forwards(outs, sems, "mine"):
                cp.start()

    def finish(self, ins, outs, sems):
        if self.forward_here:
            self._arrivals(outs, sems)[0].wait_recv()
            for cp in self._forwards(outs, sems, "sibling"):
                cp.wait_recv()
            for cp in self._sends(ins, outs, sems) + self._forwards(outs, sems, "mine"):
                cp.wait_send()
        else:
            for cp in self._arrivals(outs, sems):
                cp.wait_recv()
            for cp in self._sends(ins, outs, sems):
                cp.wait_send()
        self._own(ins, outs, sems).wait()


def _forward_copies(gathered_ref, send, recv, base, core_of_block):
    x, y, c = _pos()
    res = []
    for k, chip in enumerate(_other_chips(x, y)):
        blk = gathered_ref.at[4 * chip[0] + 2 * chip[1] + (c if core_of_block == "mine" else 1 - c)]
        res.append(pltpu.make_async_remote_copy(src_ref=blk, dst_ref=blk, send_sem=send.at[base + k],
                                                recv_sem=recv.at[base + k], device_id=(x, y, 1 - c),
                                                device_id_type=MESH))
    return res


class _AgForward:
    n_sem, n_local = 3, 0

    def __init__(self, gathered):
        self.ins = [gathered]
        self.out_shapes = [_sds(gathered.shape, gathered.dtype)]
        self.aliases = [(0, 0)]

    def start(self, ins, outs, sems):
        for cp in _forward_copies(outs[0], sems[0], sems[1], sems[3], "mine"):
            cp.start()

    def finish(self, ins, outs, sems):
        for cp in _forward_copies(outs[0], sems[0], sems[1], sems[3], "sibling"):
            cp.wait_recv()
        for cp in _forward_copies(outs[0], sems[0], sems[1], sems[3], "mine"):
            cp.wait_send()


class _RsSibling:
    n_sem, n_local = 4, 0

    def __init__(self, part):
        self.ins = [part]
        self.out_shapes = [_sds((4,) + part.shape[1:], part.dtype)]
        self.aliases = []

    def _copies(self, ins, outs, sems):
        send, recv, _, base, _ = sems
        x, y, c = _pos()
        return [pltpu.make_async_remote_copy(src_ref=ins[0].at[2 * q + (1 - c)], dst_ref=outs[0].at[q],
                                             send_sem=send.at[base + q], recv_sem=recv.at[base + q],
                                             device_id=(x, y, 1 - c), device_id_type=MESH) for q in range(4)]

    def start(self, ins, outs, sems):
        for cp in self._copies(ins, outs, sems):
            cp.start()

    def finish(self, ins, outs, sems):
        for cp in self._copies(ins, outs, sems):
            cp.wait()


class _RsChips:
    n_sem, n_local = 3, 0

    def __init__(self, pair):
        self.ins = [pair]
        self.out_shapes = [_sds((3,) + pair.shape[1:], pair.dtype)]
        self.aliases = []

    def _copies(self, ins, outs, sems):
        send, recv, _, base, _ = sems
        x, y, c = _pos()
        return [pltpu.make_async_remote_copy(src_ref=ins[0].at[2 * chip[0] + chip[1]], dst_ref=outs[0].at[k],
                                             send_sem=send.at[base + k], recv_sem=recv.at[base + k],
                                             device_id=(*chip, c), device_id_type=MESH)
                for k, chip in enumerate(_other_chips(x, y))]

    def start(self, ins, outs, sems):
        for cp in self._copies(ins, outs, sems):
            cp.start()

    def finish(self, ins, outs, sems):
        for cp in self._copies(ins, outs, sems):
            cp.wait()


def _hosted(body, tasks, *, grid, in_specs, out_specs, out_shape, scratch_shapes=(), name, semantics, operands,
            aliases=None):
    in_specs, out_specs, out_shape = list(in_specs), list(out_specs), list(out_shape)
    scratch_shapes = list(scratch_shapes)
    aliases = dict(aliases or {})
    if not tasks:
        res = _call(body, name=name, grid=grid, in_specs=in_specs, out_specs=out_specs, out_shape=out_shape,
                    scratch_shapes=scratch_shapes, input_output_aliases=aliases,
                    compiler_params=_cp(*semantics))(*operands)
        return list(res), []
    n_in, n_out, n_scr = len(in_specs), len(out_specs), len(scratch_shapes)
    t_ins = [a for t in tasks for a in t.ins]
    t_outs = [o for t in tasks for o in t.out_shapes]
    i0, o0 = n_in, n_out
    for t in tasks:
        for (i, o) in t.aliases:
            aliases[i0 + i] = o0 + o
        i0 += len(t.ins)
        o0 += len(t.out_shapes)
    n_sem = sum(t.n_sem for t in tasks)
    n_local = max(1, sum(t.n_local for t in tasks))
    n_steps = 1
    for g in grid:
        n_steps *= g
    mid_step = min(n_steps - 1, (n_steps * MID_STEP_PERCENT) // 100)

    def wrapped(*refs):
        cut = [n_in, len(t_ins), n_out, len(t_outs), n_scr]
        parts, at = [], 0
        for n in cut:
            parts.append(refs[at:at + n])
            at += n
        ins, tin, outs, tout, scratch = parts
        send, recv, local = refs[at:]
        step = pl.program_id(0)
        for d in range(1, len(grid)):
            step = step * grid[d] + pl.program_id(d)

        def each(method):
            i, o, s, l = 0, 0, 0, 0
            for t in tasks:
                if hasattr(t, method):
                    getattr(t, method)(tin[i:i + len(t.ins)], tout[o:o + len(t.out_shapes)], (send, recv, local, s, l))
                i, o, s, l = i + len(t.ins), o + len(t.out_shapes), s + t.n_sem, l + t.n_local

        @pl.when(step == 0)
        def _():
            each("start")

        body(*ins, *outs, *scratch)

        @pl.when(step == mid_step)
        def _():
            each("mid")

        @pl.when(step == n_steps - 1)
        def _():
            each("finish")

    hbm = pl.BlockSpec(memory_space=pl.ANY)
    res = _call(
        wrapped, name=name, grid=grid,
        in_specs=in_specs + [hbm] * len(t_ins), out_specs=out_specs + [hbm] * len(t_outs),
        out_shape=out_shape + t_outs,
        scratch_shapes=scratch_shapes + [pltpu.SemaphoreType.DMA((n_sem,)), pltpu.SemaphoreType.DMA((n_sem,)),
                                         pltpu.SemaphoreType.DMA((n_local,))],
        input_output_aliases=aliases,
        compiler_params=_cp(*(["arbitrary"] * len(grid))),
    )(*operands, *t_ins)
    res = list(res)
    task_outs, o = [], n_out
    for t in tasks:
        task_outs.append(res[o:o + len(t.out_shapes)])
        o += len(t.out_shapes)
    return res[:n_out], task_outs


def _pair_sum(part, got, core, name):
    _, R, C = part.shape
    tr = _tile(R, TILE["pair"]) if R % 16 == 0 else R

    def body(core_ref, p_ref, g_ref, o_ref):
        o_ref[...] = (p_ref[...].astype(F32) + g_ref[...].astype(F32)).astype(o_ref.dtype)

    return _call(
        body, name=name,
        grid_spec=pltpu.PrefetchScalarGridSpec(
            num_scalar_prefetch=1, grid=(4, R // tr),
            in_specs=[pl.BlockSpec((None, tr, C), lambda q, i, cr: (2 * q + cr[0], i, 0)),
                      pl.BlockSpec((None, tr, C), lambda q, i, cr: (q, i, 0))],
            out_specs=pl.BlockSpec((None, tr, C), lambda q, i, cr: (q, i, 0))),
        out_shape=_sds((4, R, C), part.dtype),
        compiler_params=_cp("parallel", "parallel"),
    )(core, part, got)


def _sum_slots(gathered, name):
    _, R, C = gathered.shape

    def body(g_ref, o_ref):
        acc = g_ref[0]
        for s in range(1, N_DEV):
            acc = acc + g_ref[s]
        o_ref[...] = acc

    return _call(body, name=name, out_shape=_sds((R, C), F32))(gathered)


def _sigmoid(z):
    return jax.nn.sigmoid(z)


def _softplus(z):
    e = jnp.exp(-jnp.abs(z))
    log1p_e = jnp.where(e < 0.01, e * (1.0 - e * (0.5 - e * (1.0 / 3.0))), jnp.log(1.0 + e))
    return jnp.maximum(z, 0.0) + log1p_e


_GELU_K = 0.7978845608028654
_GELU_C = 0.044715


def _gelu_and_grad(z):
    t = jnp.tanh(_GELU_K * (z + _GELU_C * z * z * z))
    g = 0.5 * z * (1.0 + t)
    dg = 0.5 * (1.0 + t) + 0.5 * z * (1.0 - t * t) * _GELU_K * (1.0 + 3.0 * _GELU_C * z * z)
    return g, dg


def _gelu(z):
    t = jnp.tanh(_GELU_K * (z + _GELU_C * z * z * z))
    return 0.5 * z * (1.0 + t)


def _row_ids(tile_index, tm, width=1):
    return tile_index * tm + lax.broadcasted_iota(jnp.int32, (tm, width), 0)


def _shift_down(prev, cur, k):
    if k == 0:
        return cur
    ext = jnp.concatenate([prev, cur], axis=0)
    return pltpu.roll(ext, k, axis=0)[prev.shape[0]:]


def _shift_up(cur, nxt, k):
    if k == 0:
        return cur
    ext = jnp.concatenate([cur, nxt], axis=0)
    return pltpu.roll(ext, ext.shape[0] - k, axis=0)[:cur.shape[0]]


def _lru_gates(r, sp):
    z = LRU_C * r * sp
    a = jnp.exp(-z)
    t = jnp.tanh(z)
    mult = jnp.sqrt(2.0 * t / (1.0 + t))
    return a, mult


def _scan_chunks(a_ref, b_ref, out_ref, carry, n_rows, reverse):
    n_chunks = n_rows // 8
    cols = a_ref.shape[1]
    rid = lax.broadcasted_iota(jnp.int32, (8, cols), 0)
    edge = 0 if reverse else 7
    group = SCAN_GROUP if n_chunks % SCAN_GROUP == 0 else 1

    def local_scan(ci):
        rows = pl.ds(pl.multiple_of(ci * 8, 8), 8)
        a = a_ref[rows, :]
        b = b_ref[rows, :]
        for s in (1, 2, 4):
            if reverse:
                keep = rid < 8 - s
                a_n, b_n = pltpu.roll(a, 8 - s, axis=0), pltpu.roll(b, 8 - s, axis=0)
            else:
                keep = rid >= s
                a_n, b_n = pltpu.roll(a, s, axis=0), pltpu.roll(b, s, axis=0)
            b = a * jnp.where(keep, b_n, 0.0) + b
            a = a * jnp.where(keep, a_n, 1.0)
        a_e = jnp.sum(jnp.where(rid == edge, a, 0.0), axis=0, keepdims=True)
        b_e = jnp.sum(jnp.where(rid == edge, b, 0.0), axis=0, keepdims=True)
        return rows, a, b, a_e, b_e

    def chunks(k, h):
        first = k * group
        scans = [local_scan((n_chunks - 1 - (first + j)) if reverse else first + j) for j in range(group)]
        for rows, a, b, a_e, b_e in scans:
            out_ref[rows, :] = a * h + b
            h = a_e * h + b_e
        return h

    return lax.fori_loop(0, n_chunks // group, chunks, carry)


def _norm_fwd(h, g, name):
    Tp, D = h.shape
    tm = _tile(Tp, TILE["norm"])

    def body(h_ref, g_ref, u_ref, r_ref):
        x = h_ref[...]
        r = lax.rsqrt(jnp.mean(x * x, axis=-1, keepdims=True) + NORM_EPS)
        u_ref[...] = (x * r * g_ref[...]).astype(BF16)
        r_ref[...] = r

    return _call(
        body, name=name, grid=(Tp // tm,),
        in_specs=[pl.BlockSpec((tm, D), lambda i: (i, 0)), pl.BlockSpec((1, D), lambda i: (0, 0))],
        out_specs=[pl.BlockSpec((tm, D), lambda i: (i, 0)), pl.BlockSpec((tm, 1), lambda i: (i, 0))],
        out_shape=[_sds((Tp, D), BF16), _sds((Tp, 1), F32)],
        compiler_params=_cp("parallel"),
    )(h, g)


def _proj_fwd(u, w_slots, name, tasks=(), part=(0, 1), earlier=None):
    Tp, K = u.shape
    S, _, n = w_slots.shape
    p, parts = part
    tm = _tile(Tp, TILE["proj"])

    def body(a_ref, b_ref, *rest):
        o_ref = rest[-1]
        o_ref[...] = jnp.dot(a_ref[...], b_ref[...], preferred_element_type=F32)

    in_specs = [pl.BlockSpec((tm, K), lambda j, i: (i, 0)), pl.BlockSpec((None, K, n), lambda j, i: (j, 0, 0))]
    operands = (u, w_slots)
    aliases = {}
    if earlier is not None:
        in_specs.append(pl.BlockSpec(memory_space=pl.ANY))
        operands += (earlier,)
        aliases = {2: 0}
    (proj,), extra = _hosted(
        body, tasks, name=name, grid=(S, Tp // tm), in_specs=in_specs,
        out_specs=[pl.BlockSpec((tm, n), lambda j, i: (i, j * parts + p))],
        out_shape=[_sds((Tp, S * parts * n), F32)],
        semantics=("parallel", "parallel"), operands=operands, aliases=aliases)
    return proj, extra


def _mlp1_fwd(u2, w_slots, name, tasks=()):
    Tp, K = u2.shape
    S, _, n = w_slots.shape
    tm = _tile(Tp, TILE["mlp1"])

    def body(a_ref, b_ref, act_ref, a1_ref):
        a1 = jnp.dot(a_ref[...], b_ref[...], preferred_element_type=F32)
        relu = jnp.maximum(a1, 0.0)
        act_ref[...] = (relu * relu).astype(BF16)
        a1_ref[...] = a1.astype(BF16)

    return _hosted(
        body, tasks, name=name, grid=(S, Tp // tm),
        in_specs=[pl.BlockSpec((tm, K), lambda j, i: (i, 0)), pl.BlockSpec((None, K, n), lambda j, i: (j, 0, 0))],
        out_specs=[pl.BlockSpec((tm, n), lambda j, i: (i, j))] * 2,
        out_shape=[_sds((Tp, S * n), BF16)] * 2,
        semantics=("parallel", "parallel"), operands=(u2, w_slots))


def _pool_fwd(proj, pool_w, name):
    Tp = proj.shape[0]
    G, Cg, _ = pool_w.shape
    D = G * Cg
    tm = _tile(Tp, TILE["pool"])

    def body(v_ref, w_ref, d_ref, y_ref, prev_ref):
        t = pl.program_id(0)

        @pl.when(t == 0)
        def _():
            prev_ref[...] = jnp.zeros_like(prev_ref)

        rows = _row_ids(t, tm)
        for g, win in enumerate(POOL_WINDOWS):
            cols = slice(g * Cg, (g + 1) * Cg)
            v = v_ref[:, cols]
            s = jnp.concatenate([prev_ref[:, cols], v], axis=0)
            k = 1
            while k < win:
                s = s + pltpu.roll(s, k, axis=0)
                k *= 2
            cnt = jnp.minimum(rows + 1, win).astype(F32)
            d = s[MAX_WINDOW:] / cnt - v
            d_ref[:, cols] = d.astype(BF16)
            y_ref[:, cols] = jnp.dot(d.astype(BF16), w_ref[g], preferred_element_type=F32)
        prev_ref[...] = v_ref[tm - MAX_WINDOW:, :]

    return _call(
        body, name=name, grid=(Tp // tm,),
        in_specs=[pl.BlockSpec((tm, D), lambda t: (t, 0)), pl.BlockSpec((G, Cg, Cg), lambda t: (0, 0, 0))],
        out_specs=[pl.BlockSpec((tm, D), lambda t: (t, 0))] * 2,
        out_shape=[_sds((Tp, D), BF16), _sds((Tp, D), F32)],
        scratch_shapes=[pltpu.VMEM((MAX_WINDOW, D), F32)],
        compiler_params=_cp("arbitrary"),
    )(proj, pool_w)


def _lru_fwd(proj, y_pool, scale, conv_w, conv_b, wa, ba, wx, bx, lam, name, tasks=()):
    Tp = proj.shape[0]
    H, hd, _ = wa.shape
    D = H * hd
    tm = _tile(Tp, TILE["lru"])
    nb = D // hd

    def body(vl_ref, vg_ref, gp_ref, gl_ref, y_ref, sc_ref, cw_ref, cb_ref, wa_ref, ba_ref, wx_ref, bx_ref,
             lam_ref, xc_ref, r_ref, i_ref, a_ref, mult_ref, hs_ref, m_ref, prev_ref, carry_ref, b_s):
        t = pl.program_id(1)

        @pl.when(t == 0)
        def _():
            prev_ref[...] = jnp.zeros_like(prev_ref)
            carry_ref[...] = jnp.zeros_like(carry_ref)

        v = vl_ref[...]
        prev = prev_ref[...]
        xc = jnp.zeros_like(v) + cb_ref[...]
        for k in range(CONV_WIDTH):
            xc = xc + cw_ref[k:k + 1, :] * _shift_down(prev, v, CONV_WIDTH - 1 - k)
        prev_ref[...] = v[tm - HALO:, :]
        xcb = xc.astype(BF16)
        r = _sigmoid(jnp.dot(xcb, wa_ref[...], preferred_element_type=F32) + ba_ref[...])
        i = _sigmoid(jnp.dot(xcb, wx_ref[...], preferred_element_type=F32) + bx_ref[...])
        a, mult = _lru_gates(r, _softplus(-lam_ref[...]))
        a_ref[...] = a
        mult_ref[...] = mult
        b_s[...] = mult * (i * xc)
        xc_ref[...] = xc
        r_ref[...] = r
        i_ref[...] = i
        carry_ref[0:1, :] = _scan_chunks(a_ref, b_s, hs_ref, carry_ref[0:1, :], tm, reverse=False)
        lru_out = hs_ref[...] * _gelu(vg_ref[...])
        pool_out = y_ref[...] * sc_ref[...]
        m_ref[...] = (_sigmoid(gp_ref[...]) * pool_out + _sigmoid(gl_ref[...]) * lru_out).astype(BF16)

    def piece(p):
        return pl.BlockSpec((tm, hd), lambda h, t: (t, p * nb + h))

    blk = pl.BlockSpec((tm, hd), lambda h, t: (t, h))
    vec = pl.BlockSpec((1, hd), lambda h, t: (0, h))
    mat = pl.BlockSpec((None, hd, hd), lambda h, t: (h, 0, 0))
    bias = pl.BlockSpec((None, 1, hd), lambda h, t: (h, 0, 0))
    return _hosted(
        body, tasks, name=name, grid=(H, Tp // tm),
        in_specs=[piece(1), piece(2), piece(3), piece(4), blk, vec,
                  pl.BlockSpec((CONV_WIDTH, hd), lambda h, t: (0, h)), vec, mat, bias, mat, bias, vec],
        out_specs=[blk] * 7,
        out_shape=[_sds((Tp, D), F32)] * 6 + [_sds((Tp, D), BF16)],
        scratch_shapes=[pltpu.VMEM((HALO, hd), F32), pltpu.VMEM((8, hd), F32), pltpu.VMEM((tm, hd), F32)],
        semantics=("parallel", "arbitrary"),
        operands=(proj, proj, proj, proj, y_pool, scale, conv_w, conv_b, wa, ba.reshape(H, 1, hd), wx,
                  bx.reshape(H, 1, hd), lam))


def _wout_norm_fwd(merged, w_out, h0, g2, name, tasks=()):
    Tp, D = h0.shape
    tm = _tile(Tp, TILE["wout"])

    def body(m_ref, w_ref, h0_ref, g_ref, h1_ref, u2_ref, r2_ref):
        h1 = h0_ref[...] + jnp.dot(m_ref[...], w_ref[...], preferred_element_type=F32)
        r = lax.rsqrt(jnp.mean(h1 * h1, axis=-1, keepdims=True) + NORM_EPS)
        h1_ref[...] = h1
        u2_ref[...] = (h1 * r * g_ref[...]).astype(BF16)
        r2_ref[...] = r

    row = pl.BlockSpec((tm, D), lambda i: (i, 0))
    return _hosted(
        body, tasks, name=name, grid=(Tp // tm,),
        in_specs=[row, pl.BlockSpec((D, D), lambda i: (0, 0)), row, pl.BlockSpec((1, D), lambda i: (0, 0))],
        out_specs=[row, row, pl.BlockSpec((tm, 1), lambda i: (i, 0))],
        out_shape=[_sds((Tp, D), F32), _sds((Tp, D), BF16), _sds((Tp, 1), F32)],
        semantics=("parallel",), operands=(merged, w_out, h0, g2))


def _mlp2_loss(act, w2, h1, target, gf, n_meta, seq, name):
    Tp, D = h1.shape
    K = act.shape[1]
    tm = _tile(Tp, TILE["mlp2"])
    tk = min(K, MLP2_K)
    nk = K // tk

    rc = _tile(tm, EPILOGUE_ROWS)
    nt = Tp // tm
    last_rows = n_meta + seq - (nt - 1) * tm
    assert nt >= 2 and n_meta % 8 == 0 and 0 < last_rows <= tm and last_rows % 8 == 0

    def body(a_ref, w_ref, h1_hbm, t_hbm, g_ref, dh_ref, dhb_ref, loss_ref, dg_ref, h1_buf, t_buf, sems):
        i, k = pl.program_id(0), pl.program_id(1)
        tile_rows = pl.ds(pl.multiple_of(i * tm, tm), tm)
        fetch_h1 = pltpu.make_async_copy(h1_hbm.at[tile_rows, :], h1_buf, sems.at[0])
        fetch_t = [
            (i == 0, pltpu.make_async_copy(t_hbm.at[pl.ds(0, tm - n_meta), :], t_buf.at[pl.ds(n_meta, tm - n_meta), :],
                                           sems.at[1])),
            ((i > 0) & (i < nt - 1),
             pltpu.make_async_copy(t_hbm.at[pl.ds(pl.multiple_of(i * tm - n_meta, 8), tm), :], t_buf, sems.at[1])),
            (i == nt - 1, pltpu.make_async_copy(t_hbm.at[pl.ds((nt - 1) * tm - n_meta, last_rows), :],
                                                t_buf.at[pl.ds(0, last_rows), :], sems.at[1])),
        ]

        @pl.when(k == 0)
        def _():
            fetch_h1.start()
            dh_ref[...] = jnp.zeros_like(dh_ref)

        for cond, f in fetch_t:
            @pl.when(cond & (k == 0))
            def _(f=f):
                f.start()

        @pl.when((i == 0) & (k == 0))
        def _():
            loss_ref[...] = jnp.zeros_like(loss_ref)
            dg_ref[...] = jnp.zeros_like(dg_ref)
            t_buf[0:n_meta, :] = jnp.zeros((n_meta, D), F32)

        dh_ref[...] += jnp.dot(a_ref[...], w_ref[...], preferred_element_type=F32)

        for cond, f in fetch_t:
            @pl.when(cond & (k == nk - 1))
            def _(f=f):
                f.wait()

        @pl.when(k == nk - 1)
        def _():
            fetch_h1.wait()
            g = g_ref[...]

            def chunk(c, carry):
                loss_acc, dg_acc = carry
                rows = pl.ds(pl.multiple_of(c * rc, rc), rc)
                h2 = h1_buf[rows, :] + dh_ref[rows, :]
                r = lax.rsqrt(jnp.mean(h2 * h2, axis=-1, keepdims=True) + NORM_EPS)
                out = h2 * r * g
                row_id = i * tm + c * rc + lax.broadcasted_iota(jnp.int32, (rc, 1), 0)
                valid = (row_id >= n_meta) & (row_id < n_meta + seq)
                diff = jnp.where(valid, out - t_buf[rows, :], 0.0)
                dout = diff / D
                dog = dout * g
                dh = r * dog - h2 * (r * r * r * jnp.mean(dog * h2, axis=-1, keepdims=True))
                dh_ref[rows, :] = dh
                dhb_ref[rows, :] = dh.astype(BF16)
                loss_acc = loss_acc + 0.5 * jnp.sum(jnp.mean(diff * diff, axis=-1, keepdims=True), axis=0, keepdims=True)
                return loss_acc, dg_acc + jnp.sum(dout * (h2 * r), axis=0, keepdims=True)

            loss_sum, dg_sum = lax.fori_loop(0, tm // rc, chunk, (jnp.zeros((1, 1), F32), jnp.zeros((1, D), F32)))
            loss_ref[...] += loss_sum
            dg_ref[...] += dg_sum

    row = pl.BlockSpec((tm, D), lambda i, k: (i, 0))
    hbm = pl.BlockSpec(memory_space=pl.ANY)
    return _call(
        body, name=name, grid=(Tp // tm, nk),
        in_specs=[pl.BlockSpec((tm, tk), lambda i, k: (i, k)), pl.BlockSpec((tk, D), lambda i, k: (k, 0)),
                  hbm, hbm, pl.BlockSpec((1, D), lambda i, k: (0, 0))],
        out_specs=[row, row, pl.BlockSpec((8, 128), lambda i, k: (0, 0)), pl.BlockSpec((1, D), lambda i, k: (0, 0))],
        out_shape=[_sds((Tp, D), F32), _sds((Tp, D), BF16), _sds((8, 128), F32), _sds((1, D), F32)],
        scratch_shapes=[pltpu.VMEM((tm, D), F32), pltpu.VMEM((tm, D), F32), pltpu.SemaphoreType.DMA((2,))],
        compiler_params=_cp("arbitrary", "arbitrary"),
    )(act, w2, h1, target, gf)


def _dact_bwd(dh2b, w2_slots, a1, name):
    Tp, D = dh2b.shape
    S, n, _ = w2_slots.shape
    tm = _tile(Tp, TILE["dact"])

    def body(g_ref, w_ref, a1_ref, o_ref):
        dact = lax.dot_general(g_ref[...], w_ref[...], _NT, preferred_element_type=F32)
        o_ref[...] = (dact * (2.0 * jnp.maximum(a1_ref[...].astype(F32), 0.0))).astype(BF16)

    return _call(
        body, name=name, grid=(S, Tp // tm),
        in_specs=[pl.BlockSpec((tm, D), lambda j, i: (i, 0)), pl.BlockSpec((None, n, D), lambda j, i: (j, 0, 0)),
                  pl.BlockSpec((tm, n), lambda j, i: (i, j))],
        out_specs=pl.BlockSpec((tm, n), lambda j, i: (i, j)),
        out_shape=_sds((Tp, S * n), BF16),
        compiler_params=_cp("parallel", "parallel"),
    )(dh2b, w2_slots, a1)


def _weight_grad(a, g, blocks, block_a, name, tasks=(), part=(0, 1)):
    Tp, Ka = a.shape
    Ng = g.shape[1]
    p, parts = part
    assert parts == 1 or not block_a
    ka = Ka // blocks if block_a else Ka // parts
    ng = Ng if block_a else Ng // blocks
    tt = _tile(Tp, TILE["tn"])
    nt = Tp // tt

    def body(a_ref, g_ref, o_ref, acc_ref):
        t = pl.program_id(1)

        @pl.when(t == 0)
        def _():
            acc_ref[...] = jnp.zeros_like(acc_ref)

        acc_ref[...] += lax.dot_general(a_ref[...], g_ref[...], _TN, preferred_element_type=F32)

        @pl.when(t == nt - 1)
        def _():
            o_ref[...] = acc_ref[...].astype(o_ref.dtype)

    if block_a:
        a_spec = pl.BlockSpec((tt, ka), lambda j, t: (t, j))
        g_spec = pl.BlockSpec((tt, ng), lambda j, t: (t, 0))
    else:
        a_spec = pl.BlockSpec((tt, ka), lambda j, t: (t, p))
        g_spec = pl.BlockSpec((tt, ng), lambda j, t: (t, j))
    (dw,), extra = _hosted(
        body, tasks, name=name, grid=(blocks, nt),
        in_specs=[a_spec, g_spec],
        out_specs=[pl.BlockSpec((None, ka, ng), lambda j, t: (j, 0, 0))],
        out_shape=[_sds((blocks, ka, ng), BF16)],
        scratch_shapes=[pltpu.VMEM((ka, ng), F32)],
        semantics=("parallel", "arbitrary"), operands=(a, g))
    return dw, extra


def _nt_norm_bwd(dz, w_parts, dres, hin, rin, g, want_bf16, name, tasks=(), tiles=None, earlier=None, seq_rows=None):
    Tp, D = hin.shape
    P = len(w_parts)
    S, _, n = w_parts[0].shape
    K = S * P
    tm = _tile(Tp, TILE["nt"])
    n_all = Tp // tm
    t0, nt = tiles if tiles is not None else (0, n_all)
    assert not (want_bf16 and (earlier is not None or seq_rows is not None))
    if seq_rows is not None:
        n_meta, seq = seq_rows
        last_rows = n_meta + seq - (n_all - 1) * tm
        assert n_all >= 2 and n_meta % 8 == 0 and 0 < last_rows <= tm and last_rows % 8 == 0

    rc = _tile(tm, EPILOGUE_ROWS)

    def body(dz_ref, *rest):
        w_refs, (dres_hbm, h_hbm, r_ref, g_ref), rest = rest[:P], rest[P:P + 4], rest[P + 4:]
        if seq_rows is not None:
            rest, seq_buf = rest[:-1], rest[-1]
        if earlier is not None and seq_rows is not None:
            _, dg0_ref, _, dh_ref, dg_ref, gx_ref, dres_buf, h_buf, sems = rest
        elif earlier is not None:
            _, dg0_ref, dh_ref, dg_ref, dres_buf, h_buf, sems = rest
        elif want_bf16:
            dh_ref, dhb_ref, dg_ref, dres_buf, h_buf, sems = rest
        elif seq_rows is not None:
            dh_ref, dg_ref, gx_ref, dres_buf, h_buf, sems = rest
        else:
            dh_ref, dg_ref, dres_buf, h_buf, sems = rest
        i, k = pl.program_id(0), pl.program_id(1)
        tile_rows = pl.ds(pl.multiple_of((t0 + i) * tm, tm), tm)
        fetch = [pltpu.make_async_copy(dres_hbm.at[tile_rows, :], dres_buf, sems.at[0]),
                 pltpu.make_async_copy(h_hbm.at[tile_rows, :], h_buf, sems.at[1])]

        @pl.when(k == 0)
        def _():
            for f in fetch:
                f.start()
            dh_ref[...] = jnp.zeros_like(dh_ref)

        @pl.when((i == 0) & (k == 0))
        def _():
            dg_ref[...] = jnp.zeros_like(dg_ref) if earlier is None else dg0_ref[...]

        for q in range(P):
            @pl.when(k % P == q)
            def _(q=q):
                dh_ref[...] += lax.dot_general(dz_ref[...], w_refs[q][...], _NT, preferred_element_type=F32)

        @pl.when(k == K - 1)
        def _():
            for f in fetch:
                f.wait()
            g = g_ref[...]

            if seq_rows is not None:
                def seq_copy(case, tile):
                    if case == "first":
                        return pltpu.make_async_copy(seq_buf.at[pl.ds(n_meta, tm - n_meta), :],
                                                     gx_ref.at[pl.ds(0, tm - n_meta), :], sems.at[2])
                    if case == "middle":
                        return pltpu.make_async_copy(
                            seq_buf, gx_ref.at[pl.ds(pl.multiple_of(tile * tm - n_meta, 8), tm), :], sems.at[2])
                    return pltpu.make_async_copy(seq_buf.at[pl.ds(0, last_rows), :],
                                                 gx_ref.at[pl.ds((n_all - 1) * tm - n_meta, last_rows), :], sems.at[2])

                def for_tile(tile, act):
                    pl.when(tile == 0)(lambda: act(seq_copy("first", tile)))
                    pl.when((tile > 0) & (tile < n_all - 1))(lambda: act(seq_copy("middle", tile)))
                    pl.when(tile == n_all - 1)(lambda: act(seq_copy("last", tile)))

                pl.when(i > 0)(lambda: for_tile(t0 + i - 1, lambda cp: cp.wait()))

            def chunk(c, dg_acc):
                rows = pl.ds(pl.multiple_of(c * rc, rc), rc)
                du = dh_ref[rows, :]
                h = h_buf[rows, :]
                r = r_ref[rows, :]
                dug = du * g
                dh = dres_buf[rows, :] + r * dug - h * (r * r * r * jnp.mean(dug * h, axis=-1, keepdims=True))
                dh_ref[rows, :] = dh
                if want_bf16:
                    dhb_ref[rows, :] = dh.astype(BF16)
                if seq_rows is not None:
                    seq_buf[rows, :] = dh
                return dg_acc + jnp.sum(du * (h * r), axis=0, keepdims=True)

            dg_ref[...] += lax.fori_loop(0, tm // rc, chunk, jnp.zeros((1, D), F32))

            if seq_rows is not None:
                for_tile(t0 + i, lambda cp: cp.start())
                pl.when(i == nt - 1)(lambda: for_tile(t0 + i, lambda cp: cp.wait()))

    row = pl.BlockSpec((tm, D), lambda i, k: (t0 + i, 0))
    vec = pl.BlockSpec((1, D), lambda i, k: (0, 0))
    hbm = pl.BlockSpec(memory_space=pl.ANY)
    out_specs = [row] + ([row] if want_bf16 else []) + [vec] + ([hbm] if seq_rows is not None else [])
    out_shape = ([_sds((Tp, D), F32)] + ([_sds((Tp, D), BF16)] if want_bf16 else []) + [_sds((1, D), F32)]
                 + ([_sds((seq_rows[1], D), F32)] if seq_rows is not None else []))
    in_specs = ([pl.BlockSpec((tm, n), lambda i, k: (t0 + i, k))]
                + [pl.BlockSpec((None, D, n), lambda i, k: (k // P, 0, 0))] * P
                + [hbm, hbm, pl.BlockSpec((tm, 1), lambda i, k: (t0 + i, 0)), vec])
    operands = (dz, *w_parts, dres, hin, rin, g)
    aliases = {}
    if earlier is not None:
        in_specs += [hbm, vec] + ([hbm] if seq_rows is not None else [])
        operands += tuple(earlier)
        aliases = {P + 5: 0}
        if seq_rows is not None:
            aliases[P + 7] = 2
    return _hosted(
        body, tasks, name=name, grid=(nt, K), in_specs=in_specs, out_specs=out_specs, out_shape=out_shape,
        scratch_shapes=[pltpu.VMEM((tm, D), F32), pltpu.VMEM((tm, D), F32), pltpu.SemaphoreType.DMA((3,))]
        + ([pltpu.VMEM((tm, D), F32)] if seq_rows is not None else []),
        semantics=("arbitrary", "arbitrary"), operands=operands, aliases=aliases)


def _dmerged_bwd(dh1b, w_out, name):
    Tp, D = dh1b.shape
    tm = _tile(Tp, TILE["dmerged"])

    def body(g_ref, w_ref, o_ref):
        o_ref[...] = lax.dot_general(g_ref[...], w_ref[...], _NT, preferred_element_type=F32)

    row = pl.BlockSpec((tm, D), lambda i: (i, 0))
    return _call(
        body, name=name, grid=(Tp // tm,),
        in_specs=[row, pl.BlockSpec((D, D), lambda i: (0, 0))],
        out_specs=row, out_shape=_sds((Tp, D), F32),
        compiler_params=_cp("parallel"),
    )(dh1b, w_out)


def _pool_bwd(dmerged, proj, y_pool, d_pool, scale, pool_w, name, tasks=()):
    Tp, D = dmerged.shape
    G, Cg, _ = pool_w.shape
    tm = _tile(Tp, TILE["pool"])
    nt = Tp // tm

    def body(dm_ref, gp_ref, y_ref, d_ref, sc_ref, w_ref, dproj_hbm, dw_ref, dsc_ref, next_ref, out_buf, out_sems):
        t = pl.program_id(0)
        tile = nt - 1 - t
        slot = t % 2
        dv_ref, dgp_ref = out_buf.at[slot, 0], out_buf.at[slot, 1]
        tile_rows = pl.ds(pl.multiple_of(tile * tm, tm), tm)

        def out_copies(s):
            return [pltpu.make_async_copy(out_buf.at[s, k], dproj_hbm.at[tile_rows, pl.ds(piece * D, D)],
                                          out_sems.at[2 * s + k]) for k, piece in enumerate((0, 3))]

        @pl.when(t >= 2)
        def _():
            for cp in out_copies(slot):
                cp.wait()

        @pl.when(t == 0)
        def _():
            next_ref[...] = jnp.zeros_like(next_ref)
            dw_ref[...] = jnp.zeros_like(dw_ref)
            dsc_ref[...] = jnp.zeros_like(dsc_ref)

        rows = _row_ids(tile, tm)
        dm = dm_ref[...]
        y = y_ref[...]
        sc = sc_ref[...]
        sg = _sigmoid(gp_ref[...])
        dpo = dm * sg
        dgp_ref[...] = (dm * (y * sc) * sg * (1.0 - sg)).astype(BF16)
        dsc_ref[...] += jnp.sum(dpo * y, axis=0, keepdims=True)
        dyb = (dpo * sc).astype(BF16)
        for g, win in enumerate(POOL_WINDOWS):
            cols = slice(g * Cg, (g + 1) * Cg)
            dy = dyb[:, cols]
            dd = lax.dot_general(dy, w_ref[g], _NT, preferred_element_type=F32)
            dw_ref[g] += lax.dot_general(d_ref[:, cols], dy, _TN, preferred_element_type=F32)
            q = dd / jnp.minimum(rows + 1, win).astype(F32)
            s = jnp.concatenate([q, next_ref[:, cols]], axis=0)
            k = 1
            while k < win:
                s = s + pltpu.roll(s, s.shape[0] - k, axis=0)
                k *= 2
            dv_ref[:, cols] = (s[:tm] - dd).astype(BF16)
            next_ref[:, cols] = q[:MAX_WINDOW]
        for cp in out_copies(slot):
            cp.start()

        @pl.when(t == nt - 1)
        def _():
            for cp in out_copies(slot) + (out_copies(1 - slot) if nt > 1 else []):
                cp.wait()

    row = pl.BlockSpec((tm, D), lambda t: (nt - 1 - t, 0))
    return _hosted(
        body, tasks, name=name, grid=(nt,),
        in_specs=[row, pl.BlockSpec((tm, D), lambda t: (nt - 1 - t, 3)), row, row,
                  pl.BlockSpec((1, D), lambda t: (0, 0)), pl.BlockSpec((G, Cg, Cg), lambda t: (0, 0, 0))],
        out_specs=[pl.BlockSpec(memory_space=pl.ANY), pl.BlockSpec((G, Cg, Cg), lambda t: (0, 0, 0)),
                   pl.BlockSpec((1, D), lambda t: (0, 0))],
        out_shape=[_sds((Tp, proj.shape[1]), BF16), _sds((G, Cg, Cg), F32), _sds((1, D), F32)],
        scratch_shapes=[pltpu.VMEM((MAX_WINDOW, D), F32), pltpu.VMEM((2, 2, tm, D), BF16),
                        pltpu.SemaphoreType.DMA((4,))],
        semantics=("arbitrary",), operands=(dmerged, proj, y_pool, d_pool, scale, pool_w))


LRU_SMALL_ROWS = 8


def _lru_bwd(dmerged, proj, xc, r_gate, i_gate, a_gate, mult_gate, hs, lam, conv_w, wa, wx, dproj, name, tasks=()):
    Tp, D = dmerged.shape
    H, hd, _ = wa.shape
    tm = _tile(Tp, TILE["lru"])
    nt = Tp // tm
    nb = D // hd
    halo_blocks = tm // HALO

    def body(dm_ref, vl_ref, vg_ref, gl_ref, xc_ref, r_ref, i_ref, a_ref, mult_ref, hs_ref, hsp_ref, lam_ref, cw_ref,
             wa_ref, wx_ref, _, dproj_hbm, dwa_ref, dwx_ref, small_ref,
             mu_next_ref, dxc_next_ref, q_s, mu_s, out_buf, out_sems):
        h_id, t = pl.program_id(0), pl.program_id(1)
        tile = nt - 1 - t
        step = h_id * nt + t
        slot = step % 2
        dvl_ref, dvg_ref, dgl_ref = out_buf.at[slot, 0], out_buf.at[slot, 1], out_buf.at[slot, 2]
        tile_rows = pl.ds(pl.multiple_of(tile * tm, tm), tm)

        def out_copies(s):
            return [pltpu.make_async_copy(
                out_buf.at[s, k], dproj_hbm.at[tile_rows, pl.ds(pl.multiple_of((piece * nb + h_id) * hd, hd), hd)],
                out_sems.at[3 * s + k]) for k, piece in enumerate((1, 2, 4))]

        @pl.when(step >= 2)
        def _():
            for cp in out_copies(slot):
                cp.wait()

        @pl.when(t == 0)
        def _():
            mu_next_ref[...] = jnp.zeros_like(mu_next_ref)
            dxc_next_ref[...] = jnp.zeros_like(dxc_next_ref)
            dwa_ref[...] = jnp.zeros_like(dwa_ref)
            dwx_ref[...] = jnp.zeros_like(dwx_ref)
            small_ref[...] = jnp.zeros_like(small_ref)

        first = tile == 0
        dm = dm_ref[...]
        hs_t = hs_ref[...]
        xc_t = xc_ref[...]
        r = r_ref[...]
        i = i_ref[...]
        lam_v = lam_ref[...]
        sp = _softplus(-lam_v)
        a = a_ref[...]
        mult = mult_ref[...]

        sg = _sigmoid(gl_ref[...])
        ge, dge = _gelu_and_grad(vg_ref[...])
        dlo = dm * sg
        dgl_ref[...] = (dm * (hs_t * ge) * sg * (1.0 - sg)).astype(BF16)
        dvg_ref[...] = (dlo * hs_t * dge).astype(BF16)
        dhs = dlo * ge

        q_s[...] = a * dhs
        mu_first = _scan_chunks(a_ref, q_s, mu_s, mu_next_ref[0:1, :], tm, reverse=True)
        lam_t = dhs + _shift_up(mu_s[...], mu_next_ref[...], 1)
        mu_next_ref[...] = jnp.broadcast_to(mu_first, mu_next_ref.shape)

        h_prev = _shift_down(jnp.where(first, 0.0, hsp_ref[...]), hs_t, 1)
        da = lam_t * h_prev
        dmult = lam_t * (i * xc_t)
        di = lam_t * mult * xc_t
        dxc = lam_t * mult * i
        dlog_a = da * a - dmult * (a * a) / mult
        dr = dlog_a * (-LRU_C * sp)
        dlam_rows = dlog_a * (-LRU_C * r)
        dza = dr * r * (1.0 - r)
        dzx = di * i * (1.0 - i)
        dzab, dzxb = dza.astype(BF16), dzx.astype(BF16)
        xcb = xc_t.astype(BF16)
        dxc = dxc + lax.dot_general(dzab, wa_ref[...], _NT, preferred_element_type=F32)
        dxc = dxc + lax.dot_general(dzxb, wx_ref[...], _NT, preferred_element_type=F32)
        dwa_ref[...] += lax.dot_general(xcb, dzab, _TN, preferred_element_type=F32)
        dwx_ref[...] += lax.dot_general(xcb, dzxb, _TN, preferred_element_type=F32)

        dxc_next = dxc_next_ref[...]
        taps = [_shift_up(dxc, dxc_next, CONV_WIDTH - 1 - k) for k in range(CONV_WIDTH)]
        dv = jnp.zeros_like(dxc)
        for k in range(CONV_WIDTH):
            dv = dv + cw_ref[k:k + 1, :] * taps[k]
        dvl_ref[...] = dv.astype(BF16)
        dxc_next_ref[...] = dxc[:HALO, :]

        v_t = vl_ref[...]
        small = [jnp.sum(dza, axis=0, keepdims=True), jnp.sum(dzx, axis=0, keepdims=True),
                 jnp.sum(dlam_rows, axis=0, keepdims=True) * (-_sigmoid(-lam_v)),
                 jnp.sum(dxc, axis=0, keepdims=True)]
        for k in range(CONV_WIDTH):
            small.append(jnp.sum(taps[k] * v_t, axis=0, keepdims=True))
        for k, row in enumerate(small):
            small_ref[k:k + 1, :] += row
        for cp in out_copies(slot):
            cp.start()

        @pl.when(step == H * nt - 1)
        def _():
            for cp in out_copies(slot) + (out_copies(1 - slot) if H * nt > 1 else []):
                cp.wait()

    def piece(p):
        return pl.BlockSpec((tm, hd), lambda h, t: (nt - 1 - t, p * nb + h))

    def halo(p):
        return pl.BlockSpec((HALO, hd), lambda h, t: (jnp.maximum((nt - 1 - t) * halo_blocks - 1, 0), p * nb + h))

    blk = pl.BlockSpec((tm, hd), lambda h, t: (nt - 1 - t, h))
    vec = pl.BlockSpec((1, hd), lambda h, t: (0, h))
    mat = pl.BlockSpec((None, hd, hd), lambda h, t: (h, 0, 0))
    return _hosted(
        body, tasks, name=name, grid=(H, nt),
        in_specs=[blk, piece(1), piece(2), piece(4), blk, blk, blk, blk, blk, blk, halo(0), vec,
                  pl.BlockSpec((CONV_WIDTH, hd), lambda h, t: (0, h)), mat, mat, pl.BlockSpec(memory_space=pl.ANY)],
        out_specs=[pl.BlockSpec(memory_space=pl.ANY), mat, mat,
                   pl.BlockSpec((None, LRU_SMALL_ROWS, hd), lambda h, t: (h, 0, 0))],
        out_shape=[_sds(dproj.shape, BF16)] + [_sds((H, hd, hd), F32)] * 2 + [_sds((H, LRU_SMALL_ROWS, hd), F32)],
        scratch_shapes=[pltpu.VMEM((HALO, hd), F32), pltpu.VMEM((HALO, hd), F32),
                        pltpu.VMEM((tm, hd), F32), pltpu.VMEM((tm, hd), F32),
                        pltpu.VMEM((2, 3, tm, hd), BF16), pltpu.SemaphoreType.DMA((6,))],
        semantics=("arbitrary", "arbitrary"), aliases={15: 0},
        operands=(dmerged, proj, proj, proj, xc, r_gate, i_gate, a_gate, mult_gate, hs, hs, lam, conv_w, wa, wx,
                  dproj))


def _adamw(w, g, m, v):
    m = ADAM_B1 * m + (1.0 - ADAM_B1) * g
    v = ADAM_B2 * v + (1.0 - ADAM_B2) * (g * g)
    m_hat = m / (1.0 - ADAM_B1 ** ADAM_STEP)
    v_hat = v / (1.0 - ADAM_B2 ** ADAM_STEP)
    delta = -ADAM_LR * (m_hat / (jnp.sqrt(v_hat) + ADAM_EPS) + ADAM_WD * w)
    return delta, m, v


def _reduce_update(pair_sums, chip_sums, w, m, v, chip_slot, name, part=(0, 1), earlier=None):
    R, C = pair_sums.shape[1:]
    p, parts = part
    tr = _tile(R, TILE["update"])
    nblk = R // tr

    def body(slot_ref, own_ref, got_ref, w_ref, m_ref, v_ref, *rest):
        g_out, d_out, m_out, v_out = rest[-4:]
        g = own_ref[...].astype(F32)
        for k in range(3):
            g = g + got_ref[k].astype(F32)
        d, m_new, v_new = _adamw(w_ref[...], g, m_ref[...], v_ref[...])
        g_out[...] = g
        d_out[...] = d
        m_out[...] = m_new
        v_out[...] = v_new

    blk = pl.BlockSpec((tr, C), lambda i, s: (p * nblk + i, 0))
    in_specs = [pl.BlockSpec((None, tr, C), lambda i, s: (s[0], i, 0)),
                pl.BlockSpec((3, tr, C), lambda i, s: (0, i, 0)), blk, blk, blk]
    operands = (chip_slot, pair_sums, chip_sums, w, m, v)
    aliases = {}
    if earlier is not None:
        in_specs += [pl.BlockSpec(memory_space=pl.ANY)] * 4
        operands += tuple(earlier)
        aliases = {6 + k: k for k in range(4)}
    return _call(
        body, name=name,
        grid_spec=pltpu.PrefetchScalarGridSpec(
            num_scalar_prefetch=1, grid=(R // tr,), in_specs=in_specs, out_specs=[blk] * 4),
        out_shape=[_sds((parts * R, C), F32)] * 4,
        input_output_aliases=aliases,
        compiler_params=_cp("parallel"),
    )(*operands)


def _small_updates(small_sum, bias_grads, col0, specs, name):
    n = len(specs)

    def body(col_ref, ss_ref, bg_ref, *refs):
        ins, outs = refs[:3 * n], refs[3 * n:]
        for k, (w, _, _, kind, r0) in enumerate(specs):
            nr, nc = w.shape
            if kind == "rows":
                g = ss_ref[r0:r0 + nr, :]
            elif kind == "cols":
                g = ss_ref[r0:r0 + nr, pl.ds(pl.multiple_of(col_ref[0], nc), nc)]
            else:
                g = bg_ref[r0:r0 + nr, :]
            d, m_new, v_new = _adamw(ins[3 * k][...], g, ins[3 * k + 1][...], ins[3 * k + 2][...])
            for o, val in zip(outs[4 * k:4 * k + 4], (g, d, m_new, v_new)):
                o[...] = val

    def whole(a):
        return pl.BlockSpec(a.shape, lambda i, c: (0,) * a.ndim)

    arrays = [a for (w, m, v, _, _) in specs for a in (w, m, v)]
    res = _call(
        body, name=name,
        grid_spec=pltpu.PrefetchScalarGridSpec(
            num_scalar_prefetch=1, grid=(1,),
            in_specs=[whole(small_sum), whole(bias_grads)] + [whole(a) for a in arrays],
            out_specs=[whole(w) for (w, _, _, _, _) in specs for _ in range(4)]),
        out_shape=[_sds(w.shape, F32) for (w, _, _, _, _) in specs for _ in range(4)],
        compiler_params=_cp("arbitrary"),
    )(col0, small_sum, bias_grads, *arrays)
    return [tuple(res[4 * k:4 * k + 4]) for k in range(n)]


def _slots_from_rows(full, lead):
    L, R, C = full.shape
    r = R // N_DEV
    return full.reshape(L, N_DEV, r, C).transpose(1, 0, 2, 3).reshape(N_DEV, L * r, C)


def _rows_from_slots(slots, lead):
    _, LR, C = slots.shape
    r = LR // lead
    return slots.reshape(N_DEV, lead, r, C).transpose(1, 0, 2, 3).reshape(lead, N_DEV * r, C)


def kernel(x, meta_tokens, norm1_g, w_in, pool_w, pool_scale, conv_w, conv_b, gate_a_w, gate_a_b, gate_x_w, gate_x_b, lru_lambda, w_out, norm2_g, mlp_w1, mlp_w2, final_g, loss_target, m_meta_tokens, m_norm1_g, m_w_in, m_pool_w, m_pool_scale, m_conv_w, m_conv_b, m_gate_a_w, m_gate_a_b, m_gate_x_w, m_gate_x_b, m_lru_lambda, m_w_out, m_norm2_g, m_mlp_w1, m_mlp_w2, m_final_g, v_meta_tokens, v_norm1_g, v_w_in, v_pool_w, v_pool_scale, v_conv_w, v_conv_b, v_gate_a_w, v_gate_a_b, v_gate_x_w, v_gate_x_b, v_lru_lambda, v_w_out, v_norm2_g, v_mlp_w1, v_mlp_w2, v_final_g):
    seq, D = x.shape[1], x.shape[2]
    n_meta = meta_tokens.shape[0]
    G, Cg = pool_w.shape[1], pool_w.shape[3]
    H, hd = gate_a_w.shape[1], gate_a_w.shape[3]
    T = n_meta + seq
    Tp = -(-T // ROW_ALIGN) * ROW_ALIGN
    ix, iy, ic = _pos()
    me = 4 * ix + 2 * iy + ic
    core = jnp.reshape(ic, (1,)).astype(jnp.int32)
    chip_slot = jnp.reshape(2 * ix + iy, (1,)).astype(jnp.int32)

    w_in_l, w1_l, w2_l, w_out_l = w_in[0], mlp_w1[0], mlp_w2[0], w_out[0]
    pool_l = pool_w[0].reshape(G * (Cg // N_DEV), Cg)
    wa_l = gate_a_w[0].reshape(H * (hd // N_DEV), hd)
    wx_l = gate_x_w[0].reshape(H * (hd // N_DEV), hd)
    small_params = jnp.concatenate(
        [meta_tokens, conv_w[0], jnp.zeros((4, D // N_DEV), F32)], axis=0)
    biases = jnp.concatenate([gate_a_b[0], gate_x_b[0]], axis=0)
    (w_in_g, pool_g, wa_g, wx_g, small_g, bias_g) = _all_gather_relay(
        [w_in_l.astype(BF16), pool_l.astype(BF16), wa_l.astype(BF16), wx_l.astype(BF16), small_params, biases],
        [W_IN_CHUNKS, 1, 1, 1, 1, 1], "gather_first")
    pool_full = _rows_from_slots(pool_g, G)
    wa_full = _rows_from_slots(wa_g, H)
    wx_full = _rows_from_slots(wx_g, H)
    small_full = small_g.transpose(1, 0, 2).reshape(n_meta + 8, D)
    meta_full = small_full[:n_meta]
    conv_full = small_full[n_meta:n_meta + CONV_WIDTH]
    bias_full = bias_g.transpose(1, 0, 2).reshape(2 * H, hd)
    ba_full, bx_full = bias_full[:H], bias_full[H:]

    h0 = jnp.concatenate([meta_full, x[0], jnp.zeros((Tp - T, D), F32)], axis=0)
    u, r1 = _norm_fwd(h0, norm1_g, "norm1")
    proj, ((w_out_g,), (w1_ici,)) = _proj_fwd(
        u, w_in_g, "proj", tasks=[_AgFull(w_out_l.astype(BF16)), _AgFull(w1_l.astype(BF16), forward_here=False)])
    w_in_parts = [w_in_g]
    d_pool, y_pool = _pool_fwd(proj, pool_full, "pool_fwd")
    (xc, r_gate, i_gate, a_gate, mult_gate, hs, merged), ((w1_g,), (w2_ici,)) = _lru_fwd(
        proj, y_pool, pool_scale, conv_full, conv_b, wa_full, ba_full, wx_full, bx_full, lru_lambda, "lru_fwd",
        tasks=[_AgForward(w1_ici), _AgFull(w2_l.astype(BF16), forward_here=False)])
    w_out_full = w_out_g.reshape(D, D)
    (h1, u2, r2), ((w2_g,),) = _wout_norm_fwd(merged, w_out_full, h0, norm2_g, "wout_norm2", tasks=[_AgForward(w2_ici)])
    (act, a1), _ = _mlp1_fwd(u2, w1_g, "mlp1")
    dh2, dh2b, loss_tile, d_final_g = _mlp2_loss(
        act, w2_g.reshape(-1, D), h1, loss_target[0], final_g.reshape(1, D), n_meta, seq, "mlp2_loss")

    def pair(part, got, tag):
        return _pair_sum(part, got, core, "pair_sum_" + tag)

    d_a1 = _dact_bwd(dh2b, w2_g, a1, "dact")
    dw2_p, _ = _weight_grad(act, dh2b, N_DEV, True, "dw2")
    dw1_p, ((dw2_got,),) = _weight_grad(u2, d_a1, N_DEV, False, "dw1", tasks=[_RsSibling(dw2_p)])
    dw2_pair = pair(dw2_p, dw2_got, "w2")
    (dh1, dh1b, d_norm2_g), ((dw2_chips,), (dw1_got,)) = _nt_norm_bwd(
        d_a1, [w1_g], dh2, h1, r2, norm2_g, True, "du2_norm2", tasks=[_RsChips(dw2_pair), _RsSibling(dw1_p)])
    dw1_pair = pair(dw1_p, dw1_got, "w1")
    dmerged = _dmerged_bwd(dh1b, w_out_full, "dmerged")
    dwout_p, _ = _weight_grad(merged, dh1b, 2, True, "dwout")
    dwout_p = dwout_p.reshape(N_DEV, D // N_DEV, D)
    (dproj_pool, dpool_full, d_scale), ((dwout_got,),) = _pool_bwd(
        dmerged, proj, y_pool, d_pool, pool_scale, pool_full, "pool_bwd", tasks=[_RsSibling(dwout_p)])
    dwout_pair = pair(dwout_p, dwout_got, "wout")
    (dproj, dwa_full, dwx_full, lru_small), ((dw1_chips,), (dwout_chips,)) = _lru_bwd(
        dmerged, proj, xc, r_gate, i_gate, a_gate, mult_gate, hs, lru_lambda, conv_full, wa_full, wx_full, dproj_pool,
        "lru_bwd", tasks=[_RsChips(dw1_pair), _RsChips(dwout_pair)])
    dwin_a, _ = _weight_grad(u, dproj, N_DEV, False, "dwin_a", part=(0, 2))
    dwin_b, ((dwin_a_got,),) = _weight_grad(u, dproj, N_DEV, False, "dwin_b", part=(1, 2), tasks=[_RsSibling(dwin_a)])
    dwin_a_pair = pair(dwin_a, dwin_a_got, "win_a")
    dpool_p = _slots_from_rows(dpool_full, G).astype(BF16)
    dwa_p = _slots_from_rows(dwa_full, H).astype(BF16)
    dwx_p = _slots_from_rows(dwx_full, H).astype(BF16)
    late = [dwin_b, dpool_p, dwa_p, dwx_p]
    n_tiles = Tp // _tile(Tp, TILE["nt"])
    n_first = max(1, n_tiles // 2)
    first_half, late_first = _nt_norm_bwd(
        dproj, w_in_parts, dh1, h0, r1, norm1_g, False, "du_norm1_a",
        tasks=[_RsChips(dwin_a_pair)] + [_RsSibling(p) for p in late], tiles=(0, n_first), seq_rows=(n_meta, seq))
    (dwin_a_chips,), late_got = late_first[0], late_first[1:]
    late_pair = [pair(p, g[0], "late%d" % k) for k, (p, g) in enumerate(zip(late, late_got))]
    (dh0, d_norm1_g, grad_x_rows), late_chips = _nt_norm_bwd(
        dproj, w_in_parts, dh1, h0, r1, norm1_g, False, "du_norm1_b", tasks=[_RsChips(p) for p in late_pair],
        tiles=(n_first, n_tiles - n_first), earlier=first_half, seq_rows=(n_meta, seq))
    grad_x = grad_x_rows[None]

    pair_sums = [dw2_pair, dw1_pair, dwout_pair] + late_pair[1:]
    chip_sums = [dw2_chips, dw1_chips, dwout_chips] + [c[0] for c in late_chips[1:]]
    big = {}
    names = ["mlp_w2", "mlp_w1", "w_out", "pool_w", "gate_a_w", "gate_x_w"]
    trip = {"mlp_w2": (mlp_w2, m_mlp_w2, v_mlp_w2), "mlp_w1": (mlp_w1, m_mlp_w1, v_mlp_w1),
            "w_out": (w_out, m_w_out, v_w_out),
            "pool_w": (pool_w, m_pool_w, v_pool_w), "gate_a_w": (gate_a_w, m_gate_a_w, v_gate_a_w),
            "gate_x_w": (gate_x_w, m_gate_x_w, v_gate_x_w)}
    for k, nm in enumerate(names):
        w_, m_, v_ = trip[nm]
        shape2 = pair_sums[k].shape[1:]
        outs = _reduce_update(pair_sums[k], chip_sums[k], w_.reshape(shape2), m_.reshape(shape2), v_.reshape(shape2),
                              chip_slot, "update_" + nm)
        big[nm] = [o.reshape(w_.shape) for o in outs]
    win2 = [a_[0] for a_ in (w_in, m_w_in, v_w_in)]
    win_a = _reduce_update(dwin_a_pair, dwin_a_chips, *win2, chip_slot, "update_w_in_a", part=(0, 2))
    win_b = _reduce_update(late_pair[0], late_chips[0][0], *win2, chip_slot, "update_w_in_b", part=(1, 2), earlier=win_a)
    big["w_in"] = [o.reshape(w_in.shape) for o in win_b]
    names = names + ["w_in"]

    lru_rows = lru_small.transpose(1, 0, 2).reshape(LRU_SMALL_ROWS, D)
    small_part = jnp.concatenate(
        [dh0[:n_meta], d_norm1_g, d_scale, d_norm2_g, d_final_g, lru_rows, jnp.zeros((4, D), F32)], axis=0)
    (small_all,) = _all_gather([small_part], "gather_small_grads")
    small_sum = _sum_slots(small_all, "sum_small_grads")
    o = n_meta
    dcol, hcol = D // N_DEV, hd // N_DEV
    bias_grads = jnp.concatenate(
        [lax.dynamic_slice_in_dim(small_sum[o + 4 + k].reshape(H, hd), me * hcol, hcol, axis=1) for k in range(2)], axis=0)
    col0 = jnp.reshape(me * dcol, (1,)).astype(jnp.int32)
    small = {"meta_tokens": (meta_tokens, m_meta_tokens, v_meta_tokens, "cols", 0),
             "norm1_g": (norm1_g, m_norm1_g, v_norm1_g, "rows", o),
             "pool_scale": (pool_scale, m_pool_scale, v_pool_scale, "rows", o + 1),
             "norm2_g": (norm2_g, m_norm2_g, v_norm2_g, "rows", o + 2),
             "final_g": (final_g, m_final_g, v_final_g, "rows", o + 3),
             "gate_a_b": (gate_a_b, m_gate_a_b, v_gate_a_b, "bias", 0),
             "gate_x_b": (gate_x_b, m_gate_x_b, v_gate_x_b, "bias", H),
             "lru_lambda": (lru_lambda, m_lru_lambda, v_lru_lambda, "rows", o + 6),
             "conv_b": (conv_b, m_conv_b, v_conv_b, "rows", o + 7),
             "conv_w": (conv_w, m_conv_w, v_conv_w, "cols", o + 8)}

    def two_d(a):
        return a.reshape(-1, a.shape[-1])

    small_out = _small_updates(
        small_sum, bias_grads, col0,
        [(two_d(w_), two_d(m_), two_d(v_), kind, row) for (w_, m_, v_, kind, row) in small.values()], "update_small")
    small_res = {nm: [r.reshape(spec[0].shape) for r in res] for (nm, spec), res in zip(small.items(), small_out)}

    def leaves(kind):
        out = {nm: res[kind] for nm, res in small_res.items()}
        for nm in names:
            out[nm] = big[nm][kind]
        order = ["meta_tokens", "norm1_g", "w_in", "pool_w", "pool_scale", "conv_w", "conv_b", "gate_a_w", "gate_a_b",
                 "gate_x_w", "gate_x_b", "lru_lambda", "w_out", "norm2_g", "mlp_w1", "mlp_w2", "final_g"]
        return [out[nm] for nm in order]

    loss = lax.psum(loss_tile[0, 0], ("x", "y", "c"))
    return (loss, grad_x, *leaves(0), *leaves(1), *leaves(2), *leaves(3))
```

```python
import jax
import jax.numpy as jnp
from jax import lax
from jax.experimental import pallas as pl
from jax.experimental.pallas import tpu as pltpu

F32 = jnp.float32
BF16 = jnp.bfloat16
MESH = pl.DeviceIdType.MESH
N_DEV = 8
POOL_WINDOWS = (2, 4, 8, 16)
MAX_WINDOW = 16
CONV_WIDTH = 4
HALO = 8
LRU_C = 8.0
NORM_EPS = 1e-6
ADAM_LR, ADAM_B1, ADAM_B2, ADAM_EPS, ADAM_WD, ADAM_STEP = 0.001, 0.9, 0.999, 1e-08, 0.01, 10
ROW_ALIGN = 128
VMEM_LIMIT = 56 << 20
TILE = dict(norm=704, proj=1408, pool=384, lru=704, wout=384, mlp1=1408, mlp2=704, dact=1408, tn=2112,
            nt=704, dmerged=704, update=256, pair=1024)
MLP2_K = 1024
EPILOGUE_ROWS = 176
SCAN_GROUP = 4
W_IN_CHUNKS = 5
MID_STEP_PERCENT = 88

_NT = (((1,), (1,)), ((), ()))
_TN = (((0,), (0,)), ((), ()))


def _call(body, **kw):
    return pl.pallas_call(body, **kw)


def _cp(*sem):
    return pltpu.CompilerParams(dimension_semantics=sem, vmem_limit_bytes=VMEM_LIMIT)


def _tile(total, pref):
    best = None
    for t in range(16, min(total, pref) + 1, 16):
        if total % t == 0:
            best = t
    assert best is not None, (total, pref)
    return best


def _sds(shape, dtype):
    return jax.ShapeDtypeStruct(shape, dtype)


def _pos():
    return lax.axis_index("x"), lax.axis_index("y"), lax.axis_index("c")


def _all_gather(shards, name):
    n = len(shards)

    def body(*refs):
        ins, outs = refs[:n], refs[n:2 * n]
        send_sems, recv_sems, local_sems = refs[2 * n:]
        x, y, c = _pos()
        me, sib = (x, y, c), (x, y, 1 - c)
        chips = [(1 - x, y), (x, 1 - y), (1 - x, 1 - y)]

        def slot(p):
            return 4 * p[0] + 2 * p[1] + p[2]

        def copy(a, k, block, to, src=None):
            dst = outs[a].at[slot(block)]
            return pltpu.make_async_remote_copy(
                src_ref=dst if src is None else src, dst_ref=dst,
                send_sem=send_sems.at[7 * a + k], recv_sem=recv_sems.at[7 * a + k],
                device_id=to, device_id_type=MESH)

        mine = [pltpu.make_async_copy(ins[a], outs[a].at[slot(me)], local_sems.at[a]) for a in range(n)]
        for m in mine:
            m.start()
        first = []
        for a in range(n):
            first.append(copy(a, 0, me, sib, src=ins[a]))
            first += [copy(a, 1 + j, me, (*chip, c), src=ins[a]) for j, chip in enumerate(chips)]
        for cp in first:
            cp.start()
        passed = []
        for a in range(n):
            for j, chip in enumerate(chips):
                copy(a, 1 + j, (*chip, c), me).wait_recv()
                fwd = copy(a, 4 + j, (*chip, c), sib)
                fwd.start()
                passed.append(fwd)
        for a in range(n):
            copy(a, 0, sib, me).wait_recv()
            for j, chip in enumerate(chips):
                copy(a, 4 + j, (*chip, 1 - c), me).wait_recv()
        for cp in first + passed:
            cp.wait_send()
        for m in mine:
            m.wait()

    hbm = pl.BlockSpec(memory_space=pl.ANY)
    return _call(
        body, name=name,
        out_shape=[_sds((N_DEV,) + s.shape, s.dtype) for s in shards],
        in_specs=[hbm] * n, out_specs=[hbm] * n,
        scratch_shapes=[pltpu.SemaphoreType.DMA((7 * n,)), pltpu.SemaphoreType.DMA((7 * n,)),
                        pltpu.SemaphoreType.DMA((n,))],
    )(*shards)


def _all_gather_relay(shards, chunks, name):
    n = len(shards)
    units = []
    for a, s in enumerate(shards):
        if chunks[a] == 1:
            units.append((a, None, None))
        else:
            w = s.shape[-1] // chunks[a]
            units += [(a, q * w, w) for q in range(chunks[a])]
    nu = len(units)

    def body(*refs):
        ins, outs = refs[:n], refs[n:2 * n]
        send_sems, recv_sems, local_sems = refs[2 * n:]
        x, y, c = _pos()
        me, sib = (x, y, c), (x, y, 1 - c)
        x_nbr, y_nbr, diag = (1 - x, y, c), (x, 1 - y, c), (1 - x, 1 - y, c)
        came_from = (x + (1 - c) * (1 - 2 * x), y + c * (1 - 2 * y), c)
        pass_to = (x + c * (1 - 2 * x), y + (1 - c) * (1 - 2 * y), c)

        def slot(p):
            return 4 * p[0] + 2 * p[1] + p[2]

        def src_view(u):
            a, c0, w = units[u]
            return ins[a] if c0 is None else ins[a].at[:, pl.ds(c0, w)]

        def dst_view(u, p):
            a, c0, w = units[u]
            return outs[a].at[slot(p)] if c0 is None else outs[a].at[slot(p), :, pl.ds(c0, w)]

        def copy(u, k, block, to, from_shard=False):
            dst = dst_view(u, block)
            return pltpu.make_async_remote_copy(
                src_ref=src_view(u) if from_shard else dst, dst_ref=dst,
                send_sem=send_sems.at[7 * u + k], recv_sem=recv_sems.at[7 * u + k],
                device_id=to, device_id_type=MESH)

        mine = [pltpu.make_async_copy(src_view(u), dst_view(u, me), local_sems.at[u]) for u in range(nu)]
        for m in mine:
            m.start()
        sent = []
        for u in range(nu):
            sent += [copy(u, 0, me, sib, True), copy(u, 1, me, x_nbr, True), copy(u, 2, me, y_nbr, True)]
        for cp in sent:
            cp.start()
        for u in range(nu):
            copy(u, 1, x_nbr, me).wait_recv()
            copy(u, 2, y_nbr, me).wait_recv()
            sent += [copy(u, 3, came_from, pass_to), copy(u, 4, x_nbr, sib), copy(u, 5, y_nbr, sib)]
            for cp in sent[-3:]:
                cp.start()
        for u in range(nu):
            copy(u, 3, diag, me).wait_recv()
            sent.append(copy(u, 6, diag, sib))
            sent[-1].start()
        for u in range(nu):
            copy(u, 0, sib, me).wait_recv()
            for k, p in ((4, x_nbr), (5, y_nbr), (6, diag)):
                copy(u, k, (p[0], p[1], 1 - c), me).wait_recv()
        for cp in sent:
            cp.wait_send()
        for m in mine:
            m.wait()

    hbm = pl.BlockSpec(memory_space=pl.ANY)
    return _call(
        body, name=name,
        out_shape=[_sds((N_DEV,) + s.shape, s.dtype) for s in shards],
        in_specs=[hbm] * n, out_specs=[hbm] * n,
        scratch_shapes=[pltpu.SemaphoreType.DMA((7 * nu,)), pltpu.SemaphoreType.DMA((7 * nu,)),
                        pltpu.SemaphoreType.DMA((nu,))],
    )(*shards)


def _other_chips(x, y):
    return [(1 - x, y), (x, 1 - y), (1 - x, 1 - y)]


class _AgFull:
    n_sem, n_local = 7, 1

    def __init__(self, shard, forward_here=True):
        self.forward_here = forward_here
        self.ins = [shard]
        self.out_shapes = [_sds((N_DEV,) + shard.shape, shard.dtype)]
        self.aliases = []

    def _peers(self):
        x, y, c = _pos()
        return [(x, y, 1 - c)] + [(*chip, c) for chip in _other_chips(x, y)]

    def _sends(self, ins, outs, sems):
        send, recv, _, base, _ = sems
        x, y, c = _pos()
        mine = outs[0].at[4 * x + 2 * y + c]
        return [pltpu.make_async_remote_copy(src_ref=ins[0], dst_ref=mine, send_sem=send.at[base + k],
                                             recv_sem=recv.at[base + k], device_id=p, device_id_type=MESH)
                for k, p in enumerate(self._peers())]

    def _arrivals(self, outs, sems):
        send, recv, _, base, _ = sems
        res = []
        for k, p in enumerate(self._peers()):
            blk = outs[0].at[4 * p[0] + 2 * p[1] + p[2]]
            res.append(pltpu.make_async_remote_copy(src_ref=blk, dst_ref=blk, send_sem=send.at[base + k],
                                                    recv_sem=recv.at[base + k], device_id=p, device_id_type=MESH))
        return res

    def _own(self, ins, outs, sems):
        x, y, c = _pos()
        return pltpu.make_async_copy(ins[0], outs[0].at[4 * x + 2 * y + c], sems[2].at[sems[4]])

    def _forwards(self, outs, sems, core_of_block):
        return _forward_copies(outs[0], sems[0], sems[1], sems[3] + 4, core_of_block)

    def start(self, ins, outs, sems):
        self._own(ins, outs, sems).start()
        for cp in self._sends(ins, outs, sems):
            cp.start()

    def mid(self, ins, outs, sems):
        if self.forward_here:
            for cp in self._arrivals(outs, sems)[1:]:
                cp.wait_recv()
            for cp in self._forwards(outs, sems, "mine"):
                cp.start()

    def finish(self, ins, outs, sems):
        if self.forward_here:
            self._arrivals(outs, sems)[0].wait_recv()
            for cp in self._forwards(outs, sems, "sibling"):
                cp.wait_recv()
            for cp in self._sends(ins, outs, sems) + self._forwards(outs, sems, "mine"):
                cp.wait_send()
        else:
            for cp in self._arrivals(outs, sems):
                cp.wait_recv()
            for cp in self._sends(ins, outs, sems):
                cp.wait_send()
        self._own(ins, outs, sems).wait()


def _forward_copies(gathered_ref, send, recv, base, core_of_block):
    x, y, c = _pos()
    res = []
    for k, chip in enumerate(_other_chips(x, y)):
        blk = gathered_ref.at[4 * chip[0] + 2 * chip[1] + (c if core_of_block == "mine" else 1 - c)]
        res.append(pltpu.make_async_remote_copy(src_ref=blk, dst_ref=blk, send_sem=send.at[base + k],
                                                recv_sem=recv.at[base + k], device_id=(x, y, 1 - c),
                                                device_id_type=MESH))
    return res


class _AgForward:
    n_sem, n_local = 3, 0

    def __init__(self, gathered):
        self.ins = [gathered]
        self.out_shapes = [_sds(gathered.shape, gathered.dtype)]
        self.aliases = [(0, 0)]

    def start(self, ins, outs, sems):
        for cp in _forward_copies(outs[0], sems[0], sems[1], sems[3], "mine"):
            cp.start()

    def finish(self, ins, outs, sems):
        for cp in _forward_copies(outs[0], sems[0], sems[1], sems[3], "sibling"):
            cp.wait_recv()
        for cp in _forward_copies(outs[0], sems[0], sems[1], sems[3], "mine"):
            cp.wait_send()


class _RsSibling:
    n_sem, n_local = 4, 0

    def __init__(self, part):
        self.ins = [part]
        self.out_shapes = [_sds((4,) + part.shape[1:], part.dtype)]
        self.aliases = []

    def _copies(self, ins, outs, sems):
        send, recv, _, base, _ = sems
        x, y, c = _pos()
        return [pltpu.make_async_remote_copy(src_ref=ins[0].at[2 * q + (1 - c)], dst_ref=outs[0].at[q],
                                             send_sem=send.at[base + q], recv_sem=recv.at[base + q],
                                             device_id=(x, y, 1 - c), device_id_type=MESH) for q in range(4)]

    def start(self, ins, outs, sems):
        for cp in self._copies(ins, outs, sems):
            cp.start()

    def finish(self, ins, outs, sems):
        for cp in self._copies(ins, outs, sems):
            cp.wait()


class _RsChips:
    n_sem, n_local = 3, 0

    def __init__(self, pair):
        self.ins = [pair]
        self.out_shapes = [_sds((3,) + pair.shape[1:], pair.dtype)]
        self.aliases = []

    def _copies(self, ins, outs, sems):
        send, recv, _, base, _ = sems
        x, y, c = _pos()
        return [pltpu.make_async_remote_copy(src_ref=ins[0].at[2 * chip[0] + chip[1]], dst_ref=outs[0].at[k],
                                             send_sem=send.at[base + k], recv_sem=recv.at[base + k],
                                             device_id=(*chip, c), device_id_type=MESH)
                for k, chip in enumerate(_other_chips(x, y))]

    def start(self, ins, outs, sems):
        for cp in self._copies(ins, outs, sems):
            cp.start()

    def finish(self, ins, outs, sems):
        for cp in self._copies(ins, outs, sems):
            cp.wait()


def _hosted(body, tasks, *, grid, in_specs, out_specs, out_shape, scratch_shapes=(), name, semantics, operands,
            aliases=None):
    in_specs, out_specs, out_shape = list(in_specs), list(out_specs), list(out_shape)
    scratch_shapes = list(scratch_shapes)
    aliases = dict(aliases or {})
    if not tasks:
        res = _call(body, name=name, grid=grid, in_specs=in_specs, out_specs=out_specs, out_shape=out_shape,
                    scratch_shapes=scratch_shapes, input_output_aliases=aliases,
                    compiler_params=_cp(*semantics))(*operands)
        return list(res), []
    n_in, n_out, n_scr = len(in_specs), len(out_specs), len(scratch_shapes)
    t_ins = [a for t in tasks for a in t.ins]
    t_outs = [o for t in tasks for o in t.out_shapes]
    i0, o0 = n_in, n_out
    for t in tasks:
        for (i, o) in t.aliases:
            aliases[i0 + i] = o0 + o
        i0 += len(t.ins)
        o0 += len(t.out_shapes)
    n_sem = sum(t.n_sem for t in tasks)
    n_local = max(1, sum(t.n_local for t in tasks))
    n_steps = 1
    for g in grid:
        n_steps *= g
    mid_step = min(n_steps - 1, (n_steps * MID_STEP_PERCENT) // 100)

    def wrapped(*refs):
        cut = [n_in, len(t_ins), n_out, len(t_outs), n_scr]
        parts, at = [], 0
        for n in cut:
            parts.append(refs[at:at + n])
            at += n
        ins, tin, outs, tout, scratch = parts
        send, recv, local = refs[at:]
        step = pl.program_id(0)
        for d in range(1, len(grid)):
            step = step * grid[d] + pl.program_id(d)

        def each(method):
            i, o, s, l = 0, 0, 0, 0
            for t in tasks:
                if hasattr(t, method):
                    getattr(t, method)(tin[i:i + len(t.ins)], tout[o:o + len(t.out_shapes)], (send, recv, local, s, l))
                i, o, s, l = i + len(t.ins), o + len(t.out_shapes), s + t.n_sem, l + t.n_local

        @pl.when(step == 0)
        def _():
            each("start")

        body(*ins, *outs, *scratch)

        @pl.when(step == mid_step)
        def _():
            each("mid")

        @pl.when(step == n_steps - 1)
        def _():
            each("finish")

    hbm = pl.BlockSpec(memory_space=pl.ANY)
    res = _call(
        wrapped, name=name, grid=grid,
        in_specs=in_specs + [hbm] * len(t_ins), out_specs=out_specs + [hbm] * len(t_outs),
        out_shape=out_shape + t_outs,
        scratch_shapes=scratch_shapes + [pltpu.SemaphoreType.DMA((n_sem,)), pltpu.SemaphoreType.DMA((n_sem,)),
                                         pltpu.SemaphoreType.DMA((n_local,))],
        input_output_aliases=aliases,
        compiler_params=_cp(*(["arbitrary"] * len(grid))),
    )(*operands, *t_ins)
    res = list(res)
    task_outs, o = [], n_out
    for t in tasks:
        task_outs.append(res[o:o + len(t.out_shapes)])
        o += len(t.out_shapes)
    return res[:n_out], task_outs


def _pair_sum(part, got, core, name):
    _, R, C = part.shape
    tr = _tile(R, TILE["pair"]) if R % 16 == 0 else R

    def body(core_ref, p_ref, g_ref, o_ref):
        o_ref[...] = (p_ref[...].astype(F32) + g_ref[...].astype(F32)).astype(o_ref.dtype)

    return _call(
        body, name=name,
        grid_spec=pltpu.PrefetchScalarGridSpec(
            num_scalar_prefetch=1, grid=(4, R // tr),
            in_specs=[pl.BlockSpec((None, tr, C), lambda q, i, cr: (2 * q + cr[0], i, 0)),
                      pl.BlockSpec((None, tr, C), lambda q, i, cr: (q, i, 0))],
            out_specs=pl.BlockSpec((None, tr, C), lambda q, i, cr: (q, i, 0))),
        out_shape=_sds((4, R, C), part.dtype),
        compiler_params=_cp("parallel", "parallel"),
    )(core, part, got)


def _sum_slots(gathered, name):
    _, R, C = gathered.shape

    def body(g_ref, o_ref):
        acc = g_ref[0]
        for s in range(1, N_DEV):
            acc = acc + g_ref[s]
        o_ref[...] = acc

    return _call(body, name=name, out_shape=_sds((R, C), F32))(gathered)


def _sigmoid(z):
    return jax.nn.sigmoid(z)


def _softplus(z):
    e = jnp.exp(-jnp.abs(z))
    log1p_e = jnp.where(e < 0.01, e * (1.0 - e * (0.5 - e * (1.0 / 3.0))), jnp.log(1.0 + e))
    return jnp.maximum(z, 0.0) + log1p_e


_GELU_K = 0.7978845608028654
_GELU_C = 0.044715


def _gelu_and_grad(z):
    t = jnp.tanh(_GELU_K * (z + _GELU_C * z * z * z))
    g = 0.5 * z * (1.0 + t)
    dg = 0.5 * (1.0 + t) + 0.5 * z * (1.0 - t * t) * _GELU_K * (1.0 + 3.0 * _GELU_C * z * z)
    return g, dg


def _gelu(z):
    t = jnp.tanh(_GELU_K * (z + _GELU_C * z * z * z))
    return 0.5 * z * (1.0 + t)


def _row_ids(tile_index, tm, width=1):
    return tile_index * tm + lax.broadcasted_iota(jnp.int32, (tm, width), 0)


def _shift_down(prev, cur, k):
    if k == 0:
        return cur
    ext = jnp.concatenate([prev, cur], axis=0)
    return pltpu.roll(ext, k, axis=0)[prev.shape[0]:]


def _shift_up(cur, nxt, k):
    if k == 0:
        return cur
    ext = jnp.concatenate([cur, nxt], axis=0)
    return pltpu.roll(ext, ext.shape[0] - k, axis=0)[:cur.shape[0]]


def _lru_gates(r, sp):
    z = LRU_C * r * sp
    a = jnp.exp(-z)
    t = jnp.tanh(z)
    mult = jnp.sqrt(2.0 * t / (1.0 + t))
    return a, mult


def _scan_chunks(a_ref, b_ref, out_ref, carry, n_rows, reverse):
    n_chunks = n_rows // 8
    cols = a_ref.shape[1]
    rid = lax.broadcasted_iota(jnp.int32, (8, cols), 0)
    edge = 0 if reverse else 7
    group = SCAN_GROUP if n_chunks % SCAN_GROUP == 0 else 1

    def local_scan(ci):
        rows = pl.ds(pl.multiple_of(ci * 8, 8), 8)
        a = a_ref[rows, :]
        b = b_ref[rows, :]
        for s in (1, 2, 4):
            if reverse:
                keep = rid < 8 - s
                a_n, b_n = pltpu.roll(a, 8 - s, axis=0), pltpu.roll(b, 8 - s, axis=0)
            else:
                keep = rid >= s
                a_n, b_n = pltpu.roll(a, s, axis=0), pltpu.roll(b, s, axis=0)
            b = a * jnp.where(keep, b_n, 0.0) + b
            a = a * jnp.where(keep, a_n, 1.0)
        a_e = jnp.sum(jnp.where(rid == edge, a, 0.0), axis=0, keepdims=True)
        b_e = jnp.sum(jnp.where(rid == edge, b, 0.0), axis=0, keepdims=True)
        return rows, a, b, a_e, b_e

    def chunks(k, h):
        first = k * group
        scans = [local_scan((n_chunks - 1 - (first + j)) if reverse else first + j) for j in range(group)]
        for rows, a, b, a_e, b_e in scans:
            out_ref[rows, :] = a * h + b
            h = a_e * h + b_e
        return h

    return lax.fori_loop(0, n_chunks // group, chunks, carry)


def _norm_fwd(h, g, name):
    Tp, D = h.shape
    tm = _tile(Tp, TILE["norm"])

    def body(h_ref, g_ref, u_ref, r_ref):
        x = h_ref[...]
        r = lax.rsqrt(jnp.mean(x * x, axis=-1, keepdims=True) + NORM_EPS)
        u_ref[...] = (x * r * g_ref[...]).astype(BF16)
        r_ref[...] = r

    return _call(
        body, name=name, grid=(Tp // tm,),
        in_specs=[pl.BlockSpec((tm, D), lambda i: (i, 0)), pl.BlockSpec((1, D), lambda i: (0, 0))],
        out_specs=[pl.BlockSpec((tm, D), lambda i: (i, 0)), pl.BlockSpec((tm, 1), lambda i: (i, 0))],
        out_shape=[_sds((Tp, D), BF16), _sds((Tp, 1), F32)],
        compiler_params=_cp("parallel"),
    )(h, g)


def _proj_fwd(u, w_slots, name, tasks=()):
    Tp, K = u.shape
    S, _, n = w_slots.shape
    tm = _tile(Tp, TILE["proj"])

    def body(a_ref, b_ref, o_ref):
        o_ref[...] = jnp.dot(a_ref[...], b_ref[...], preferred_element_type=F32)

    (proj,), extra = _hosted(
        body, tasks, name=name, grid=(S, Tp // tm),
        in_specs=[pl.BlockSpec((tm, K), lambda j, i: (i, 0)), pl.BlockSpec((None, K, n), lambda j, i: (j, 0, 0))],
        out_specs=[pl.BlockSpec((tm, n), lambda j, i: (i, j))],
        out_shape=[_sds((Tp, S * n), F32)],
        semantics=("parallel", "parallel"), operands=(u, w_slots))
    return proj, extra


def _mlp1_fwd(u2, w_slots, name, tasks=()):
    Tp, K = u2.shape
    S, _, n = w_slots.shape
    tm = _tile(Tp, TILE["mlp1"])

    def body(a_ref, b_ref, act_ref, a1_ref):
        a1 = jnp.dot(a_ref[...], b_ref[...], preferred_element_type=F32)
        relu = jnp.maximum(a1, 0.0)
        act_ref[...] = (relu * relu).astype(BF16)
        a1_ref[...] = a1.astype(BF16)

    return _hosted(
        body, tasks, name=name, grid=(S, Tp // tm),
        in_specs=[pl.BlockSpec((tm, K), lambda j, i: (i, 0)), pl.BlockSpec((None, K, n), lambda j, i: (j, 0, 0))],
        out_specs=[pl.BlockSpec((tm, n), lambda j, i: (i, j))] * 2,
        out_shape=[_sds((Tp, S * n), BF16)] * 2,
        semantics=("parallel", "parallel"), operands=(u2, w_slots))


def _pool_fwd(proj, pool_w, name):
    Tp = proj.shape[0]
    G, Cg, _ = pool_w.shape
    D = G * Cg
    tm = _tile(Tp, TILE["pool"])

    def body(v_ref, w_ref, d_ref, y_ref, prev_ref):
        t = pl.program_id(0)

        @pl.when(t == 0)
        def _():
            prev_ref[...] = jnp.zeros_like(prev_ref)

        rows = _row_ids(t, tm)
        for g, win in enumerate(POOL_WINDOWS):
            cols = slice(g * Cg, (g + 1) * Cg)
            v = v_ref[:, cols]
            s = jnp.concatenate([prev_ref[:, cols], v], axis=0)
            k = 1
            while k < win:
                s = s + pltpu.roll(s, k, axis=0)
                k *= 2
            cnt = jnp.minimum(rows + 1, win).astype(F32)
            d = s[MAX_WINDOW:] / cnt - v
            d_ref[:, cols] = d.astype(BF16)
            y_ref[:, cols] = jnp.dot(d.astype(BF16), w_ref[g], preferred_element_type=F32)
        prev_ref[...] = v_ref[tm - MAX_WINDOW:, :]

    return _call(
        body, name=name, grid=(Tp // tm,),
        in_specs=[pl.BlockSpec((tm, D), lambda t: (t, 0)), pl.BlockSpec((G, Cg, Cg), lambda t: (0, 0, 0))],
        out_specs=[pl.BlockSpec((tm, D), lambda t: (t, 0))] * 2,
        out_shape=[_sds((Tp, D), BF16), _sds((Tp, D), F32)],
        scratch_shapes=[pltpu.VMEM((MAX_WINDOW, D), F32)],
        compiler_params=_cp("arbitrary"),
    )(proj, pool_w)


def _lru_fwd(proj, y_pool, scale, conv_w, conv_b, wa, ba, wx, bx, lam, name, tasks=()):
    Tp = proj.shape[0]
    H, hd, _ = wa.shape
    D = H * hd
    tm = _tile(Tp, TILE["lru"])
    nb = D // hd

    def body(vl_ref, vg_ref, gp_ref, gl_ref, y_ref, sc_ref, cw_ref, cb_ref, wa_ref, ba_ref, wx_ref, bx_ref,
             lam_ref, xc_ref, r_ref, i_ref, a_ref, mult_ref, hs_ref, m_ref, prev_ref, carry_ref, b_s):
        t = pl.program_id(1)

        @pl.when(t == 0)
        def _():
            prev_ref[...] = jnp.zeros_like(prev_ref)
            carry_ref[...] = jnp.zeros_like(carry_ref)

        v = vl_ref[...]
        prev = prev_ref[...]
        xc = jnp.zeros_like(v) + cb_ref[...]
        for k in range(CONV_WIDTH):
            xc = xc + cw_ref[k:k + 1, :] * _shift_down(prev, v, CONV_WIDTH - 1 - k)
        prev_ref[...] = v[tm - HALO:, :]
        xcb = xc.astype(BF16)
        r = _sigmoid(jnp.dot(xcb, wa_ref[...], preferred_element_type=F32) + ba_ref[...])
        i = _sigmoid(jnp.dot(xcb, wx_ref[...], preferred_element_type=F32) + bx_ref[...])
        a, mult = _lru_gates(r, _softplus(-lam_ref[...]))
        a_ref[...] = a
        mult_ref[...] = mult
        b_s[...] = mult * (i * xc)
        xc_ref[...] = xc
        r_ref[...] = r
        i_ref[...] = i
        carry_ref[0:1, :] = _scan_chunks(a_ref, b_s, hs_ref, carry_ref[0:1, :], tm, reverse=False)
        lru_out = hs_ref[...] * _gelu(vg_ref[...])
        pool_out = y_ref[...] * sc_ref[...]
        m_ref[...] = (_sigmoid(gp_ref[...]) * pool_out + _sigmoid(gl_ref[...]) * lru_out).astype(BF16)

    def piece(p):
        return pl.BlockSpec((tm, hd), lambda h, t: (t, p * nb + h))

    blk = pl.BlockSpec((tm, hd), lambda h, t: (t, h))
    vec = pl.BlockSpec((1, hd), lambda h, t: (0, h))
    mat = pl.BlockSpec((None, hd, hd), lambda h, t: (h, 0, 0))
    bias = pl.BlockSpec((None, 1, hd), lambda h, t: (h, 0, 0))
    return _hosted(
        body, tasks, name=name, grid=(H, Tp // tm),
        in_specs=[piece(1), piece(2), piece(3), piece(4), blk, vec,
                  pl.BlockSpec((CONV_WIDTH, hd), lambda h, t: (0, h)), vec, mat, bias, mat, bias, vec],
        out_specs=[blk] * 7,
        out_shape=[_sds((Tp, D), F32)] * 6 + [_sds((Tp, D), BF16)],
        scratch_shapes=[pltpu.VMEM((HALO, hd), F32), pltpu.VMEM((8, hd), F32), pltpu.VMEM((tm, hd), F32)],
        semantics=("parallel", "arbitrary"),
        operands=(proj, proj, proj, proj, y_pool, scale, conv_w, conv_b, wa, ba.reshape(H, 1, hd), wx,
                  bx.reshape(H, 1, hd), lam))


def _wout_norm_fwd(merged, w_out, h0, g2, name, tasks=()):
    Tp, D = h0.shape
    tm = _tile(Tp, TILE["wout"])

    def body(m_ref, w_ref, h0_ref, g_ref, h1_ref, u2_ref, r2_ref):
        h1 = h0_ref[...] + jnp.dot(m_ref[...], w_ref[...], preferred_element_type=F32)
        r = lax.rsqrt(jnp.mean(h1 * h1, axis=-1, keepdims=True) + NORM_EPS)
        h1_ref[...] = h1
        u2_ref[...] = (h1 * r * g_ref[...]).astype(BF16)
        r2_ref[...] = r

    row = pl.BlockSpec((tm, D), lambda i: (i, 0))
    return _hosted(
        body, tasks, name=name, grid=(Tp // tm,),
        in_specs=[row, pl.BlockSpec((D, D), lambda i: (0, 0)), row, pl.BlockSpec((1, D), lambda i: (0, 0))],
        out_specs=[row, row, pl.BlockSpec((tm, 1), lambda i: (i, 0))],
        out_shape=[_sds((Tp, D), F32), _sds((Tp, D), BF16), _sds((Tp, 1), F32)],
        semantics=("parallel",), operands=(merged, w_out, h0, g2))


def _mlp2_loss(act, w2, h1, target, gf, n_meta, seq, name):
    Tp, D = h1.shape
    K = act.shape[1]
    tm = _tile(Tp, TILE["mlp2"])
    tk = min(K, MLP2_K)
    nk = K // tk

    rc = _tile(tm, EPILOGUE_ROWS)
    nt = Tp // tm
    last_rows = n_meta + seq - (nt - 1) * tm
    assert nt >= 2 and n_meta % 8 == 0 and 0 < last_rows <= tm and last_rows % 8 == 0

    def body(a_ref, w_ref, h1_hbm, t_hbm, g_ref, dh_ref, dhb_ref, loss_ref, dg_ref, h1_buf, t_buf, sems):
        i, k = pl.program_id(0), pl.program_id(1)
        tile_rows = pl.ds(pl.multiple_of(i * tm, tm), tm)
        fetch_h1 = pltpu.make_async_copy(h1_hbm.at[tile_rows, :], h1_buf, sems.at[0])
        fetch_t = [
            (i == 0, pltpu.make_async_copy(t_hbm.at[pl.ds(0, tm - n_meta), :], t_buf.at[pl.ds(n_meta, tm - n_meta), :],
                                           sems.at[1])),
            ((i > 0) & (i < nt - 1),
             pltpu.make_async_copy(t_hbm.at[pl.ds(pl.multiple_of(i * tm - n_meta, 8), tm), :], t_buf, sems.at[1])),
            (i == nt - 1, pltpu.make_async_copy(t_hbm.at[pl.ds((nt - 1) * tm - n_meta, last_rows), :],
                                                t_buf.at[pl.ds(0, last_rows), :], sems.at[1])),
        ]

        @pl.when(k == 0)
        def _():
            fetch_h1.start()
            dh_ref[...] = jnp.zeros_like(dh_ref)

        for cond, f in fetch_t:
            @pl.when(cond & (k == 0))
            def _(f=f):
                f.start()

        @pl.when((i == 0) & (k == 0))
        def _():
            loss_ref[...] = jnp.zeros_like(loss_ref)
            dg_ref[...] = jnp.zeros_like(dg_ref)
            t_buf[0:n_meta, :] = jnp.zeros((n_meta, D), F32)

        dh_ref[...] += jnp.dot(a_ref[...], w_ref[...], preferred_element_type=F32)

        for cond, f in fetch_t:
            @pl.when(cond & (k == nk - 1))
            def _(f=f):
                f.wait()

        @pl.when(k == nk - 1)
        def _():
            fetch_h1.wait()
            g = g_ref[...]

            def chunk(c, carry):
                loss_acc, dg_acc = carry
                rows = pl.ds(pl.multiple_of(c * rc, rc), rc)
                h2 = h1_buf[rows, :] + dh_ref[rows, :]
                r = lax.rsqrt(jnp.mean(h2 * h2, axis=-1, keepdims=True) + NORM_EPS)
                out = h2 * r * g
                row_id = i * tm + c * rc + lax.broadcasted_iota(jnp.int32, (rc, 1), 0)
                valid = (row_id >= n_meta) & (row_id < n_meta + seq)
                diff = jnp.where(valid, out - t_buf[rows, :], 0.0)
                dout = diff / D
                dog = dout * g
                dh = r * dog - h2 * (r * r * r * jnp.mean(dog * h2, axis=-1, keepdims=True))
                dh_ref[rows, :] = dh
                dhb_ref[rows, :] = dh.astype(BF16)
                loss_acc = loss_acc + 0.5 * jnp.sum(jnp.mean(diff * diff, axis=-1, keepdims=True), axis=0, keepdims=True)
                return loss_acc, dg_acc + jnp.sum(dout * (h2 * r), axis=0, keepdims=True)

            loss_sum, dg_sum = lax.fori_loop(0, tm // rc, chunk, (jnp.zeros((1, 1), F32), jnp.zeros((1, D), F32)))
            loss_ref[...] += loss_sum
            dg_ref[...] += dg_sum

    row = pl.BlockSpec((tm, D), lambda i, k: (i, 0))
    hbm = pl.BlockSpec(memory_space=pl.ANY)
    return _call(
        body, name=name, grid=(Tp // tm, nk),
        in_specs=[pl.BlockSpec((tm, tk), lambda i, k: (i, k)), pl.BlockSpec((tk, D), lambda i, k: (k, 0)),
                  hbm, hbm, pl.BlockSpec((1, D), lambda i, k: (0, 0))],
        out_specs=[row, row, pl.BlockSpec((8, 128), lambda i, k: (0, 0)), pl.BlockSpec((1, D), lambda i, k: (0, 0))],
        out_shape=[_sds((Tp, D), F32), _sds((Tp, D), BF16), _sds((8, 128), F32), _sds((1, D), F32)],
        scratch_shapes=[pltpu.VMEM((tm, D), F32), pltpu.VMEM((tm, D), F32), pltpu.SemaphoreType.DMA((2,))],
        compiler_params=_cp("arbitrary", "arbitrary"),
    )(act, w2, h1, target, gf)


def _dact_bwd(dh2b, w2_slots, a1, name):
    Tp, D = dh2b.shape
    S, n, _ = w2_slots.shape
    tm = _tile(Tp, TILE["dact"])

    def body(g_ref, w_ref, a1_ref, o_ref):
        dact = lax.dot_general(g_ref[...], w_ref[...], _NT, preferred_element_type=F32)
        o_ref[...] = (dact * (2.0 * jnp.maximum(a1_ref[...].astype(F32), 0.0))).astype(BF16)

    return _call(
        body, name=name, grid=(S, Tp // tm),
        in_specs=[pl.BlockSpec((tm, D), lambda j, i: (i, 0)), pl.BlockSpec((None, n, D), lambda j, i: (j, 0, 0)),
                  pl.BlockSpec((tm, n), lambda j, i: (i, j))],
        out_specs=pl.BlockSpec((tm, n), lambda j, i: (i, j)),
        out_shape=_sds((Tp, S * n), BF16),
        compiler_params=_cp("parallel", "parallel"),
    )(dh2b, w2_slots, a1)


def _weight_grad(a, g, blocks, block_a, name, tasks=(), part=(0, 1)):
    Tp, Ka = a.shape
    Ng = g.shape[1]
    p, parts = part
    ka = Ka // (blocks * parts) if block_a else Ka // parts
    ng = Ng if block_a else Ng // blocks
    tt = _tile(Tp, TILE["tn"])
    nt = Tp // tt

    def body(a_ref, g_ref, o_ref, acc_ref):
        t = pl.program_id(1)

        @pl.when(t == 0)
        def _():
            acc_ref[...] = jnp.zeros_like(acc_ref)

        acc_ref[...] += lax.dot_general(a_ref[...], g_ref[...], _TN, preferred_element_type=F32)

        @pl.when(t == nt - 1)
        def _():
            o_ref[...] = acc_ref[...].astype(o_ref.dtype)

    if block_a:
        a_spec = pl.BlockSpec((tt, ka), lambda j, t: (t, j * parts + p))
        g_spec = pl.BlockSpec((tt, ng), lambda j, t: (t, 0))
    else:
        a_spec = pl.BlockSpec((tt, ka), lambda j, t: (t, p))
        g_spec = pl.BlockSpec((tt, ng), lambda j, t: (t, j))
    (dw,), extra = _hosted(
        body, tasks, name=name, grid=(blocks, nt),
        in_specs=[a_spec, g_spec],
        out_specs=[pl.BlockSpec((None, ka, ng), lambda j, t: (j, 0, 0))],
        out_shape=[_sds((blocks, ka, ng), BF16)],
        scratch_shapes=[pltpu.VMEM((ka, ng), F32)],
        semantics=("parallel", "arbitrary"), operands=(a, g))
    return dw, extra


def _nt_norm_bwd(dz, w_parts, dres, hin, rin, g, want_bf16, name, tasks=(), tiles=None, earlier=None):
    Tp, D = hin.shape
    P = len(w_parts)
    S, _, n = w_parts[0].shape
    K = S * P
    tm = _tile(Tp, TILE["nt"])
    t0, nt = tiles if tiles is not None else (0, Tp // tm)
    assert not (want_bf16 and earlier is not None)

    rc = _tile(tm, EPILOGUE_ROWS)

    def body(dz_ref, *rest):
        w_refs, (dres_hbm, h_hbm, r_ref, g_ref), rest = rest[:P], rest[P:P + 4], rest[P + 4:]
        if earlier is not None:
            _, dg0_ref, dh_ref, dg_ref, dres_buf, h_buf, sems = rest
        elif want_bf16:
            dh_ref, dhb_ref, dg_ref, dres_buf, h_buf, sems = rest
        else:
            dh_ref, dg_ref, dres_buf, h_buf, sems = rest
        i, k = pl.program_id(0), pl.program_id(1)
        tile_rows = pl.ds(pl.multiple_of((t0 + i) * tm, tm), tm)
        fetch = [pltpu.make_async_copy(dres_hbm.at[tile_rows, :], dres_buf, sems.at[0]),
                 pltpu.make_async_copy(h_hbm.at[tile_rows, :], h_buf, sems.at[1])]

        @pl.when(k == 0)
        def _():
            for f in fetch:
                f.start()
            dh_ref[...] = jnp.zeros_like(dh_ref)

        @pl.when((i == 0) & (k == 0))
        def _():
            dg_ref[...] = jnp.zeros_like(dg_ref) if earlier is None else dg0_ref[...]

        for q in range(P):
            @pl.when(k % P == q)
            def _(q=q):
                dh_ref[...] += lax.dot_general(dz_ref[...], w_refs[q][...], _NT, preferred_element_type=F32)

        @pl.when(k == K - 1)
        def _():
            for f in fetch:
                f.wait()
            g = g_ref[...]

            def chunk(c, dg_acc):
                rows = pl.ds(pl.multiple_of(c * rc, rc), rc)
                du = dh_ref[rows, :]
                h = h_buf[rows, :]
                r = r_ref[rows, :]
                dug = du * g
                dh = dres_buf[rows, :] + r * dug - h * (r * r * r * jnp.mean(dug * h, axis=-1, keepdims=True))
                dh_ref[rows, :] = dh
                if want_bf16:
                    dhb_ref[rows, :] = dh.astype(BF16)
                return dg_acc + jnp.sum(du * (h * r), axis=0, keepdims=True)

            dg_ref[...] += lax.fori_loop(0, tm // rc, chunk, jnp.zeros((1, D), F32))

    row = pl.BlockSpec((tm, D), lambda i, k: (t0 + i, 0))
    vec = pl.BlockSpec((1, D), lambda i, k: (0, 0))
    hbm = pl.BlockSpec(memory_space=pl.ANY)
    out_specs = [row] + ([row] if want_bf16 else []) + [vec]
    out_shape = [_sds((Tp, D), F32)] + ([_sds((Tp, D), BF16)] if want_bf16 else []) + [_sds((1, D), F32)]
    in_specs = ([pl.BlockSpec((tm, n), lambda i, k: (t0 + i, k))]
                + [pl.BlockSpec((None, D, n), lambda i, k: (k // P, 0, 0))] * P
                + [hbm, hbm, pl.BlockSpec((tm, 1), lambda i, k: (t0 + i, 0)), vec])
    operands = (dz, *w_parts, dres, hin, rin, g)
    aliases = {}
    if earlier is not None:
        in_specs += [hbm, vec]
        operands += tuple(earlier)
        aliases = {P + 5: 0}
    return _hosted(
        body, tasks, name=name, grid=(nt, K), in_specs=in_specs, out_specs=out_specs, out_shape=out_shape,
        scratch_shapes=[pltpu.VMEM((tm, D), F32), pltpu.VMEM((tm, D), F32), pltpu.SemaphoreType.DMA((2,))],
        semantics=("arbitrary", "arbitrary"), operands=operands, aliases=aliases)


def _dmerged_bwd(dh1b, w_out, name):
    Tp, D = dh1b.shape
    tm = _tile(Tp, TILE["dmerged"])

    def body(g_ref, w_ref, o_ref):
        o_ref[...] = lax.dot_general(g_ref[...], w_ref[...], _NT, preferred_element_type=F32)

    row = pl.BlockSpec((tm, D), lambda i: (i, 0))
    return _call(
        body, name=name, grid=(Tp // tm,),
        in_specs=[row, pl.BlockSpec((D, D), lambda i: (0, 0))],
        out_specs=row, out_shape=_sds((Tp, D), F32),
        compiler_params=_cp("parallel"),
    )(dh1b, w_out)


def _pool_bwd(dmerged, proj, y_pool, d_pool, scale, pool_w, name, tasks=()):
    Tp, D = dmerged.shape
    G, Cg, _ = pool_w.shape
    tm = _tile(Tp, TILE["pool"])
    nt = Tp // tm

    def body(dm_ref, gp_ref, y_ref, d_ref, sc_ref, w_ref, dproj_hbm, dw_ref, dsc_ref, next_ref, out_buf, out_sems):
        t = pl.program_id(0)
        tile = nt - 1 - t
        slot = t % 2
        dv_ref, dgp_ref = out_buf.at[slot, 0], out_buf.at[slot, 1]
        tile_rows = pl.ds(pl.multiple_of(tile * tm, tm), tm)

        def out_copies(s):
            return [pltpu.make_async_copy(out_buf.at[s, k], dproj_hbm.at[tile_rows, pl.ds(piece * D, D)],
                                          out_sems.at[2 * s + k]) for k, piece in enumerate((0, 3))]

        @pl.when(t >= 2)
        def _():
            for cp in out_copies(slot):
                cp.wait()

        @pl.when(t == 0)
        def _():
            next_ref[...] = jnp.zeros_like(next_ref)
            dw_ref[...] = jnp.zeros_like(dw_ref)
            dsc_ref[...] = jnp.zeros_like(dsc_ref)

        rows = _row_ids(tile, tm)
        dm = dm_ref[...]
        y = y_ref[...]
        sc = sc_ref[...]
        sg = _sigmoid(gp_ref[...])
        dpo = dm * sg
        dgp_ref[...] = (dm * (y * sc) * sg * (1.0 - sg)).astype(BF16)
        dsc_ref[...] += jnp.sum(dpo * y, axis=0, keepdims=True)
        dyb = (dpo * sc).astype(BF16)
        for g, win in enumerate(POOL_WINDOWS):
            cols = slice(g * Cg, (g + 1) * Cg)
            dy = dyb[:, cols]
            dd = lax.dot_general(dy, w_ref[g], _NT, preferred_element_type=F32)
            dw_ref[g] += lax.dot_general(d_ref[:, cols], dy, _TN, preferred_element_type=F32)
            q = dd / jnp.minimum(rows + 1, win).astype(F32)
            s = jnp.concatenate([q, next_ref[:, cols]], axis=0)
            k = 1
            while k < win:
                s = s + pltpu.roll(s, s.shape[0] - k, axis=0)
                k *= 2
            dv_ref[:, cols] = (s[:tm] - dd).astype(BF16)
            next_ref[:, cols] = q[:MAX_WINDOW]
        for cp in out_copies(slot):
            cp.start()

        @pl.when(t == nt - 1)
        def _():
            for cp in out_copies(slot) + (out_copies(1 - slot) if nt > 1 else []):
                cp.wait()

    row = pl.BlockSpec((tm, D), lambda t: (nt - 1 - t, 0))
    return _hosted(
        body, tasks, name=name, grid=(nt,),
        in_specs=[row, pl.BlockSpec((tm, D), lambda t: (nt - 1 - t, 3)), row, row,
                  pl.BlockSpec((1, D), lambda t: (0, 0)), pl.BlockSpec((G, Cg, Cg), lambda t: (0, 0, 0))],
        out_specs=[pl.BlockSpec(memory_space=pl.ANY), pl.BlockSpec((G, Cg, Cg), lambda t: (0, 0, 0)),
                   pl.BlockSpec((1, D), lambda t: (0, 0))],
        out_shape=[_sds((Tp, proj.shape[1]), BF16), _sds((G, Cg, Cg), F32), _sds((1, D), F32)],
        scratch_shapes=[pltpu.VMEM((MAX_WINDOW, D), F32), pltpu.VMEM((2, 2, tm, D), BF16),
                        pltpu.SemaphoreType.DMA((4,))],
        semantics=("arbitrary",), operands=(dmerged, proj, y_pool, d_pool, scale, pool_w))


LRU_SMALL_ROWS = 8


def _lru_bwd(dmerged, proj, xc, r_gate, i_gate, a_gate, mult_gate, hs, lam, conv_w, wa, wx, dproj, name, tasks=()):
    Tp, D = dmerged.shape
    H, hd, _ = wa.shape
    tm = _tile(Tp, TILE["lru"])
    nt = Tp // tm
    nb = D // hd
    halo_blocks = tm // HALO

    def body(dm_ref, vl_ref, vg_ref, gl_ref, xc_ref, r_ref, i_ref, a_ref, mult_ref, hs_ref, hsp_ref, lam_ref, cw_ref,
             wa_ref, wx_ref, _, dproj_hbm, dwa_ref, dwx_ref, small_ref,
             mu_next_ref, dxc_next_ref, q_s, mu_s, out_buf, out_sems):
        h_id, t = pl.program_id(0), pl.program_id(1)
        tile = nt - 1 - t
        step = h_id * nt + t
        slot = step % 2
        dvl_ref, dvg_ref, dgl_ref = out_buf.at[slot, 0], out_buf.at[slot, 1], out_buf.at[slot, 2]
        tile_rows = pl.ds(pl.multiple_of(tile * tm, tm), tm)

        def out_copies(s):
            return [pltpu.make_async_copy(
                out_buf.at[s, k], dproj_hbm.at[tile_rows, pl.ds(pl.multiple_of((piece * nb + h_id) * hd, hd), hd)],
                out_sems.at[3 * s + k]) for k, piece in enumerate((1, 2, 4))]

        @pl.when(step >= 2)
        def _():
            for cp in out_copies(slot):
                cp.wait()

        @pl.when(t == 0)
        def _():
            mu_next_ref[...] = jnp.zeros_like(mu_next_ref)
            dxc_next_ref[...] = jnp.zeros_like(dxc_next_ref)
            dwa_ref[...] = jnp.zeros_like(dwa_ref)
            dwx_ref[...] = jnp.zeros_like(dwx_ref)
            small_ref[...] = jnp.zeros_like(small_ref)

        first = tile == 0
        dm = dm_ref[...]
        hs_t = hs_ref[...]
        xc_t = xc_ref[...]
        r = r_ref[...]
        i = i_ref[...]
        lam_v = lam_ref[...]
        sp = _softplus(-lam_v)
        a = a_ref[...]
        mult = mult_ref[...]

        sg = _sigmoid(gl_ref[...])
        ge, dge = _gelu_and_grad(vg_ref[...])
        dlo = dm * sg
        dgl_ref[...] = (dm * (hs_t * ge) * sg * (1.0 - sg)).astype(BF16)
        dvg_ref[...] = (dlo * hs_t * dge).astype(BF16)
        dhs = dlo * ge

        q_s[...] = a * dhs
        mu_first = _scan_chunks(a_ref, q_s, mu_s, mu_next_ref[0:1, :], tm, reverse=True)
        lam_t = dhs + _shift_up(mu_s[...], mu_next_ref[...], 1)
        mu_next_ref[...] = jnp.broadcast_to(mu_first, mu_next_ref.shape)

        h_prev = _shift_down(jnp.where(first, 0.0, hsp_ref[...]), hs_t, 1)
        da = lam_t * h_prev
        dmult = lam_t * (i * xc_t)
        di = lam_t * mult * xc_t
        dxc = lam_t * mult * i
        dlog_a = da * a - dmult * (a * a) / mult
        dr = dlog_a * (-LRU_C * sp)
        dlam_rows = dlog_a * (-LRU_C * r)
        dza = dr * r * (1.0 - r)
        dzx = di * i * (1.0 - i)
        dzab, dzxb = dza.astype(BF16), dzx.astype(BF16)
        xcb = xc_t.astype(BF16)
        dxc = dxc + lax.dot_general(dzab, wa_ref[...], _NT, preferred_element_type=F32)
        dxc = dxc + lax.dot_general(dzxb, wx_ref[...], _NT, preferred_element_type=F32)
        dwa_ref[...] += lax.dot_general(xcb, dzab, _TN, preferred_element_type=F32)
        dwx_ref[...] += lax.dot_general(xcb, dzxb, _TN, preferred_element_type=F32)

        dxc_next = dxc_next_ref[...]
        taps = [_shift_up(dxc, dxc_next, CONV_WIDTH - 1 - k) for k in range(CONV_WIDTH)]
        dv = jnp.zeros_like(dxc)
        for k in range(CONV_WIDTH):
            dv = dv + cw_ref[k:k + 1, :] * taps[k]
        dvl_ref[...] = dv.astype(BF16)
        dxc_next_ref[...] = dxc[:HALO, :]

        v_t = vl_ref[...]
        small = [jnp.sum(dza, axis=0, keepdims=True), jnp.sum(dzx, axis=0, keepdims=True),
                 jnp.sum(dlam_rows, axis=0, keepdims=True) * (-_sigmoid(-lam_v)),
                 jnp.sum(dxc, axis=0, keepdims=True)]
        for k in range(CONV_WIDTH):
            small.append(jnp.sum(taps[k] * v_t, axis=0, keepdims=True))
        for k, row in enumerate(small):
            small_ref[k:k + 1, :] += row
        for cp in out_copies(slot):
            cp.start()

        @pl.when(step == H * nt - 1)
        def _():
            for cp in out_copies(slot) + (out_copies(1 - slot) if H * nt > 1 else []):
                cp.wait()

    def piece(p):
        return pl.BlockSpec((tm, hd), lambda h, t: (nt - 1 - t, p * nb + h))

    def halo(p):
        return pl.BlockSpec((HALO, hd), lambda h, t: (jnp.maximum((nt - 1 - t) * halo_blocks - 1, 0), p * nb + h))

    blk = pl.BlockSpec((tm, hd), lambda h, t: (nt - 1 - t, h))
    vec = pl.BlockSpec((1, hd), lambda h, t: (0, h))
    mat = pl.BlockSpec((None, hd, hd), lambda h, t: (h, 0, 0))
    return _hosted(
        body, tasks, name=name, grid=(H, nt),
        in_specs=[blk, piece(1), piece(2), piece(4), blk, blk, blk, blk, blk, blk, halo(0), vec,
                  pl.BlockSpec((CONV_WIDTH, hd), lambda h, t: (0, h)), mat, mat, pl.BlockSpec(memory_space=pl.ANY)],
        out_specs=[pl.BlockSpec(memory_space=pl.ANY), mat, mat,
                   pl.BlockSpec((None, LRU_SMALL_ROWS, hd), lambda h, t: (h, 0, 0))],
        out_shape=[_sds(dproj.shape, BF16)] + [_sds((H, hd, hd), F32)] * 2 + [_sds((H, LRU_SMALL_ROWS, hd), F32)],
        scratch_shapes=[pltpu.VMEM((HALO, hd), F32), pltpu.VMEM((HALO, hd), F32),
                        pltpu.VMEM((tm, hd), F32), pltpu.VMEM((tm, hd), F32),
                        pltpu.VMEM((2, 3, tm, hd), BF16), pltpu.SemaphoreType.DMA((6,))],
        semantics=("arbitrary", "arbitrary"), aliases={15: 0},
        operands=(dmerged, proj, proj, proj, xc, r_gate, i_gate, a_gate, mult_gate, hs, hs, lam, conv_w, wa, wx,
                  dproj))


def _adamw(w, g, m, v):
    m = ADAM_B1 * m + (1.0 - ADAM_B1) * g
    v = ADAM_B2 * v + (1.0 - ADAM_B2) * (g * g)
    m_hat = m / (1.0 - ADAM_B1 ** ADAM_STEP)
    v_hat = v / (1.0 - ADAM_B2 ** ADAM_STEP)
    delta = -ADAM_LR * (m_hat / (jnp.sqrt(v_hat) + ADAM_EPS) + ADAM_WD * w)
    return delta, m, v


def _reduce_update(pair_sums, chip_sums, w, m, v, chip_slot, name, part=(0, 1), earlier=None):
    R, C = pair_sums.shape[1:]
    p, parts = part
    tr = _tile(R, TILE["update"])
    nblk = R // tr

    def body(slot_ref, own_ref, got_ref, w_ref, m_ref, v_ref, *rest):
        g_out, d_out, m_out, v_out = rest[-4:]
        g = own_ref[...].astype(F32)
        for k in range(3):
            g = g + got_ref[k].astype(F32)
        d, m_new, v_new = _adamw(w_ref[...], g, m_ref[...], v_ref[...])
        g_out[...] = g
        d_out[...] = d
        m_out[...] = m_new
        v_out[...] = v_new

    blk = pl.BlockSpec((tr, C), lambda i, s: (p * nblk + i, 0))
    in_specs = [pl.BlockSpec((None, tr, C), lambda i, s: (s[0], i, 0)),
                pl.BlockSpec((3, tr, C), lambda i, s: (0, i, 0)), blk, blk, blk]
    operands = (chip_slot, pair_sums, chip_sums, w, m, v)
    aliases = {}
    if earlier is not None:
        in_specs += [pl.BlockSpec(memory_space=pl.ANY)] * 4
        operands += tuple(earlier)
        aliases = {6 + k: k for k in range(4)}
    return _call(
        body, name=name,
        grid_spec=pltpu.PrefetchScalarGridSpec(
            num_scalar_prefetch=1, grid=(R // tr,), in_specs=in_specs, out_specs=[blk] * 4),
        out_shape=[_sds((parts * R, C), F32)] * 4,
        input_output_aliases=aliases,
        compiler_params=_cp("parallel"),
    )(*operands)


def _small_updates(small_sum, bias_grads, col0, specs, name):
    n = len(specs)

    def body(col_ref, ss_ref, bg_ref, *refs):
        ins, outs = refs[:3 * n], refs[3 * n:]
        for k, (w, _, _, kind, r0) in enumerate(specs):
            nr, nc = w.shape
            if kind == "rows":
                g = ss_ref[r0:r0 + nr, :]
            elif kind == "cols":
                g = ss_ref[r0:r0 + nr, pl.ds(pl.multiple_of(col_ref[0], nc), nc)]
            else:
                g = bg_ref[r0:r0 + nr, :]
            d, m_new, v_new = _adamw(ins[3 * k][...], g, ins[3 * k + 1][...], ins[3 * k + 2][...])
            for o, val in zip(outs[4 * k:4 * k + 4], (g, d, m_new, v_new)):
                o[...] = val

    def whole(a):
        return pl.BlockSpec(a.shape, lambda i, c: (0,) * a.ndim)

    arrays = [a for (w, m, v, _, _) in specs for a in (w, m, v)]
    res = _call(
        body, name=name,
        grid_spec=pltpu.PrefetchScalarGridSpec(
            num_scalar_prefetch=1, grid=(1,),
            in_specs=[whole(small_sum), whole(bias_grads)] + [whole(a) for a in arrays],
            out_specs=[whole(w) for (w, _, _, _, _) in specs for _ in range(4)]),
        out_shape=[_sds(w.shape, F32) for (w, _, _, _, _) in specs for _ in range(4)],
        compiler_params=_cp("arbitrary"),
    )(col0, small_sum, bias_grads, *arrays)
    return [tuple(res[4 * k:4 * k + 4]) for k in range(n)]


def _slots_from_rows(full, lead):
    L, R, C = full.shape
    r = R // N_DEV
    return full.reshape(L, N_DEV, r, C).transpose(1, 0, 2, 3).reshape(N_DEV, L * r, C)


def _rows_from_slots(slots, lead):
    _, LR, C = slots.shape
    r = LR // lead
    return slots.reshape(N_DEV, lead, r, C).transpose(1, 0, 2, 3).reshape(lead, N_DEV * r, C)


def kernel(x, meta_tokens, norm1_g, w_in, pool_w, pool_scale, conv_w, conv_b, gate_a_w, gate_a_b, gate_x_w, gate_x_b, lru_lambda, w_out, norm2_g, mlp_w1, mlp_w2, final_g, loss_target, m_meta_tokens, m_norm1_g, m_w_in, m_pool_w, m_pool_scale, m_conv_w, m_conv_b, m_gate_a_w, m_gate_a_b, m_gate_x_w, m_gate_x_b, m_lru_lambda, m_w_out, m_norm2_g, m_mlp_w1, m_mlp_w2, m_final_g, v_meta_tokens, v_norm1_g, v_w_in, v_pool_w, v_pool_scale, v_conv_w, v_conv_b, v_gate_a_w, v_gate_a_b, v_gate_x_w, v_gate_x_b, v_lru_lambda, v_w_out, v_norm2_g, v_mlp_w1, v_mlp_w2, v_final_g):
    seq, D = x.shape[1], x.shape[2]
    n_meta = meta_tokens.shape[0]
    G, Cg = pool_w.shape[1], pool_w.shape[3]
    H, hd = gate_a_w.shape[1], gate_a_w.shape[3]
    T = n_meta + seq
    Tp = -(-T // ROW_ALIGN) * ROW_ALIGN
    ix, iy, ic = _pos()
    me = 4 * ix + 2 * iy + ic
    core = jnp.reshape(ic, (1,)).astype(jnp.int32)
    chip_slot = jnp.reshape(2 * ix + iy, (1,)).astype(jnp.int32)

    w_in_l, w1_l, w2_l, w_out_l = w_in[0], mlp_w1[0], mlp_w2[0], w_out[0]
    pool_l = pool_w[0].reshape(G * (Cg // N_DEV), Cg)
    wa_l = gate_a_w[0].reshape(H * (hd // N_DEV), hd)
    wx_l = gate_x_w[0].reshape(H * (hd // N_DEV), hd)
    small_params = jnp.concatenate(
        [meta_tokens, conv_w[0], jnp.zeros((4, D // N_DEV), F32)], axis=0)
    biases = jnp.concatenate([gate_a_b[0], gate_x_b[0]], axis=0)
    (w_in_g, pool_g, wa_g, wx_g, small_g, bias_g) = _all_gather_relay(
        [w_in_l.astype(BF16), pool_l.astype(BF16), wa_l.astype(BF16), wx_l.astype(BF16), small_params, biases],
        [W_IN_CHUNKS, 1, 1, 1, 1, 1], "gather_first")
    pool_full = _rows_from_slots(pool_g, G)
    wa_full = _rows_from_slots(wa_g, H)
    wx_full = _rows_from_slots(wx_g, H)
    small_full = small_g.transpose(1, 0, 2).reshape(n_meta + 8, D)
    meta_full = small_full[:n_meta]
    conv_full = small_full[n_meta:n_meta + CONV_WIDTH]
    bias_full = bias_g.transpose(1, 0, 2).reshape(2 * H, hd)
    ba_full, bx_full = bias_full[:H], bias_full[H:]

    h0 = jnp.concatenate([meta_full, x[0], jnp.zeros((Tp - T, D), F32)], axis=0)
    u, r1 = _norm_fwd(h0, norm1_g, "norm1")
    proj, ((w_out_g,), (w1_ici,)) = _proj_fwd(
        u, w_in_g, "proj", tasks=[_AgFull(w_out_l.astype(BF16)), _AgFull(w1_l.astype(BF16), forward_here=False)])
    w_in_parts = [w_in_g]
    d_pool, y_pool = _pool_fwd(proj, pool_full, "pool_fwd")
    (xc, r_gate, i_gate, a_gate, mult_gate, hs, merged), ((w1_g,), (w2_ici,)) = _lru_fwd(
        proj, y_pool, pool_scale, conv_full, conv_b, wa_full, ba_full, wx_full, bx_full, lru_lambda, "lru_fwd",
        tasks=[_AgForward(w1_ici), _AgFull(w2_l.astype(BF16), forward_here=False)])
    w_out_full = w_out_g.reshape(D, D)
    (h1, u2, r2), ((w2_g,),) = _wout_norm_fwd(merged, w_out_full, h0, norm2_g, "wout_norm2", tasks=[_AgForward(w2_ici)])
    (act, a1), _ = _mlp1_fwd(u2, w1_g, "mlp1")
    dh2, dh2b, loss_tile, d_final_g = _mlp2_loss(
        act, w2_g.reshape(-1, D), h1, loss_target[0], final_g.reshape(1, D), n_meta, seq, "mlp2_loss")

    def pair(part, got, tag):
        return _pair_sum(part, got, core, "pair_sum_" + tag)

    d_a1 = _dact_bwd(dh2b, w2_g, a1, "dact")
    dw2_a, _ = _weight_grad(act, dh2b, N_DEV, True, "dw2_a", part=(0, 2))
    dw2_b, ((dw2_a_got,),) = _weight_grad(act, dh2b, N_DEV, True, "dw2_b", part=(1, 2), tasks=[_RsSibling(dw2_a)])
    dw2_a_pair = pair(dw2_a, dw2_a_got, "w2_a")
    dw1_p, ((dw2_a_chips,), (dw2_b_got,)) = _weight_grad(
        u2, d_a1, N_DEV, False, "dw1", tasks=[_RsChips(dw2_a_pair), _RsSibling(dw2_b)])
    dw2_b_pair = pair(dw2_b, dw2_b_got, "w2_b")
    (dh1, dh1b, d_norm2_g), ((dw2_b_chips,), (dw1_got,)) = _nt_norm_bwd(
        d_a1, [w1_g], dh2, h1, r2, norm2_g, True, "du2_norm2", tasks=[_RsChips(dw2_b_pair), _RsSibling(dw1_p)])
    dw1_pair = pair(dw1_p, dw1_got, "w1")
    dmerged = _dmerged_bwd(dh1b, w_out_full, "dmerged")
    dwout_p, _ = _weight_grad(merged, dh1b, 2, True, "dwout")
    dwout_p = dwout_p.reshape(N_DEV, D // N_DEV, D)
    (dproj_pool, dpool_full, d_scale), ((dwout_got,),) = _pool_bwd(
        dmerged, proj, y_pool, d_pool, pool_scale, pool_full, "pool_bwd", tasks=[_RsSibling(dwout_p)])
    dwout_pair = pair(dwout_p, dwout_got, "wout")
    (dproj, dwa_full, dwx_full, lru_small), ((dw1_chips,), (dwout_chips,)) = _lru_bwd(
        dmerged, proj, xc, r_gate, i_gate, a_gate, mult_gate, hs, lru_lambda, conv_full, wa_full, wx_full, dproj_pool,
        "lru_bwd", tasks=[_RsChips(dw1_pair), _RsChips(dwout_pair)])
    dwin_a, _ = _weight_grad(u, dproj, N_DEV, False, "dwin_a", part=(0, 2))
    dwin_b, ((dwin_a_got,),) = _weight_grad(u, dproj, N_DEV, False, "dwin_b", part=(1, 2), tasks=[_RsSibling(dwin_a)])
    dwin_a_pair = pair(dwin_a, dwin_a_got, "win_a")
    dpool_p = _slots_from_rows(dpool_full, G).astype(BF16)
    dwa_p = _slots_from_rows(dwa_full, H).astype(BF16)
    dwx_p = _slots_from_rows(dwx_full, H).astype(BF16)
    late = [dwin_b, dpool_p, dwa_p, dwx_p]
    n_tiles = Tp // _tile(Tp, TILE["nt"])
    n_first = max(1, n_tiles // 2)
    (dh0_a, d_norm1_a), ((dwin_a_chips,), *late_got) = _nt_norm_bwd(
        dproj, w_in_parts, dh1, h0, r1, norm1_g, False, "du_norm1_a",
        tasks=[_RsChips(dwin_a_pair)] + [_RsSibling(p) for p in late], tiles=(0, n_first))
    late_pair = [pair(p, g[0], "late%d" % k) for k, (p, g) in enumerate(zip(late, late_got))]
    (dh0, d_norm1_g), late_chips = _nt_norm_bwd(
        dproj, w_in_parts, dh1, h0, r1, norm1_g, False, "du_norm1_b", tasks=[_RsChips(p) for p in late_pair],
        tiles=(n_first, n_tiles - n_first), earlier=(dh0_a, d_norm1_a))
    grad_x = dh0[n_meta:T][None]

    pair_sums = [dw1_pair, dwout_pair] + late_pair[1:]
    chip_sums = [dw1_chips, dwout_chips] + [c[0] for c in late_chips[1:]]
    big = {}
    names = ["mlp_w1", "w_out", "pool_w", "gate_a_w", "gate_x_w"]
    trip = {"mlp_w1": (mlp_w1, m_mlp_w1, v_mlp_w1),
            "w_out": (w_out, m_w_out, v_w_out),
            "pool_w": (pool_w, m_pool_w, v_pool_w), "gate_a_w": (gate_a_w, m_gate_a_w, v_gate_a_w),
            "gate_x_w": (gate_x_w, m_gate_x_w, v_gate_x_w)}
    for k, nm in enumerate(names):
        w_, m_, v_ = trip[nm]
        shape2 = pair_sums[k].shape[1:]
        outs = _reduce_update(pair_sums[k], chip_sums[k], w_.reshape(shape2), m_.reshape(shape2), v_.reshape(shape2),
                              chip_slot, "update_" + nm)
        big[nm] = [o.reshape(w_.shape) for o in outs]
    win2 = [a_[0] for a_ in (w_in, m_w_in, v_w_in)]
    win_a = _reduce_update(dwin_a_pair, dwin_a_chips, *win2, chip_slot, "update_w_in_a", part=(0, 2))
    win_b = _reduce_update(late_pair[0], late_chips[0][0], *win2, chip_slot, "update_w_in_b", part=(1, 2), earlier=win_a)
    big["w_in"] = [o.reshape(w_in.shape) for o in win_b]
    w2_3 = [a_[0] for a_ in (mlp_w2, m_mlp_w2, v_mlp_w2)]
    w2_a = _reduce_update(dw2_a_pair, dw2_a_chips, *w2_3, chip_slot, "update_mlp_w2_a", part=(0, 2))
    w2_b = _reduce_update(dw2_b_pair, dw2_b_chips, *w2_3, chip_slot, "update_mlp_w2_b", part=(1, 2), earlier=w2_a)
    big["mlp_w2"] = [o.reshape(mlp_w2.shape) for o in w2_b]
    names = names + ["w_in", "mlp_w2"]

    lru_rows = lru_small.transpose(1, 0, 2).reshape(LRU_SMALL_ROWS, D)
    small_part = jnp.concatenate(
        [dh0[:n_meta], d_norm1_g, d_scale, d_norm2_g, d_final_g, lru_rows, jnp.zeros((4, D), F32)], axis=0)
    (small_all,) = _all_gather([small_part], "gather_small_grads")
    small_sum = _sum_slots(small_all, "sum_small_grads")
    o = n_meta
    dcol, hcol = D // N_DEV, hd // N_DEV
    bias_grads = jnp.concatenate(
        [lax.dynamic_slice_in_dim(small_sum[o + 4 + k].reshape(H, hd), me * hcol, hcol, axis=1) for k in range(2)], axis=0)
    col0 = jnp.reshape(me * dcol, (1,)).astype(jnp.int32)
    small = {"meta_tokens": (meta_tokens, m_meta_tokens, v_meta_tokens, "cols", 0),
             "norm1_g": (norm1_g, m_norm1_g, v_norm1_g, "rows", o),
             "pool_scale": (pool_scale, m_pool_scale, v_pool_scale, "rows", o + 1),
             "norm2_g": (norm2_g, m_norm2_g, v_norm2_g, "rows", o + 2),
             "final_g": (final_g, m_final_g, v_final_g, "rows", o + 3),
             "gate_a_b": (gate_a_b, m_gate_a_b, v_gate_a_b, "bias", 0),
             "gate_x_b": (gate_x_b, m_gate_x_b, v_gate_x_b, "bias", H),
             "lru_lambda": (lru_lambda, m_lru_lambda, v_lru_lambda, "rows", o + 6),
             "conv_b": (conv_b, m_conv_b, v_conv_b, "rows", o + 7),
             "conv_w": (conv_w, m_conv_w, v_conv_w, "cols", o + 8)}

    def two_d(a):
        return a.reshape(-1, a.shape[-1])

    small_out = _small_updates(
        small_sum, bias_grads, col0,
        [(two_d(w_), two_d(m_), two_d(v_), kind, row) for (w_, m_, v_, kind, row) in small.values()], "update_small")
    small_res = {nm: [r.reshape(spec[0].shape) for r in res] for (nm, spec), res in zip(small.items(), small_out)}

    def leaves(kind):
        out = {nm: res[kind] for nm, res in small_res.items()}
        for nm in names:
            out[nm] = big[nm][kind]
        order = ["meta_tokens", "norm1_g", "w_in", "pool_w", "pool_scale", "conv_w", "conv_b", "gate_a_w", "gate_a_b",
                 "gate_x_w", "gate_x_b", "lru_lambda", "w_out", "norm2_g", "mlp_w1", "mlp_w2", "final_g"]
        return [out[nm] for nm in order]

    loss = lax.psum(loss_tile[0, 0], ("x", "y", "c"))
    return (loss, grad_x, *leaves(0), *leaves(1), *leaves(2), *leaves(3))
```

```python
import jax
import jax.numpy as jnp
from jax import lax
from jax.experimental import pallas as pl
from jax.experimental.pallas import tpu as pltpu

F32 = jnp.float32
BF16 = jnp.bfloat16
MESH = pl.DeviceIdType.MESH
N_DEV = 8
POOL_WINDOWS = (2, 4, 8, 16)
MAX_WINDOW = 16
CONV_WIDTH = 4
HALO = 8
LRU_C = 8.0
NORM_EPS = 1e-6
ADAM_LR, ADAM_B1, ADAM_B2, ADAM_EPS, ADAM_WD, ADAM_STEP = 0.001, 0.9, 0.999, 1e-08, 0.01, 10
ROW_ALIGN = 128
VMEM_LIMIT = 56 << 20
TILE = dict(norm=384, proj=1408, pool=384, lru=704, wout=384, mlp1=1408, mlp2=704, dact=1408, tn=2112,
            nt=704, dmerged=704, update=256, pair=1024)
MLP2_K = 1024
EPILOGUE_ROWS = 176
SCAN_GROUP = 4
W_IN_CHUNKS = 5
MID_STEP_PERCENT = 88

_NT = (((1,), (1,)), ((), ()))
_TN = (((0,), (0,)), ((), ()))


def _call(body, **kw):
    return pl.pallas_call(body, **kw)


def _cp(*sem):
    return pltpu.CompilerParams(dimension_semantics=sem, vmem_limit_bytes=VMEM_LIMIT)


def _tile(total, pref):
    best = None
    for t in range(16, min(total, pref) + 1, 16):
        if total % t == 0:
            best = t
    assert best is not None, (total, pref)
    return best


def _sds(shape, dtype):
    return jax.ShapeDtypeStruct(shape, dtype)


def _pos():
    return lax.axis_index("x"), lax.axis_index("y"), lax.axis_index("c")


def _all_gather(shards, name):
    n = len(shards)

    def body(*refs):
        ins, outs = refs[:n], refs[n:2 * n]
        send_sems, recv_sems, local_sems = refs[2 * n:]
        x, y, c = _pos()
        me, sib = (x, y, c), (x, y, 1 - c)
        chips = [(1 - x, y), (x, 1 - y), (1 - x, 1 - y)]

        def slot(p):
            return 4 * p[0] + 2 * p[1] + p[2]

        def copy(a, k, block, to, src=None):
            dst = outs[a].at[slot(block)]
            return pltpu.make_async_remote_copy(
                src_ref=dst if src is None else src, dst_ref=dst,
                send_sem=send_sems.at[7 * a + k], recv_sem=recv_sems.at[7 * a + k],
                device_id=to, device_id_type=MESH)

        mine = [pltpu.make_async_copy(ins[a], outs[a].at[slot(me)], local_sems.at[a]) for a in range(n)]
        for m in mine:
            m.start()
        first = []
        for a in range(n):
            first.append(copy(a, 0, me, sib, src=ins[a]))
            first += [copy(a, 1 + j, me, (*chip, c), src=ins[a]) for j, chip in enumerate(chips)]
        for cp in first:
            cp.start()
        passed = []
        for a in range(n):
            for j, chip in enumerate(chips):
                copy(a, 1 + j, (*chip, c), me).wait_recv()
                fwd = copy(a, 4 + j, (*chip, c), sib)
                fwd.start()
                passed.append(fwd)
        for a in range(n):
            copy(a, 0, sib, me).wait_recv()
            for j, chip in enumerate(chips):
                copy(a, 4 + j, (*chip, 1 - c), me).wait_recv()
        for cp in first + passed:
            cp.wait_send()
        for m in mine:
            m.wait()

    hbm = pl.BlockSpec(memory_space=pl.ANY)
    return _call(
        body, name=name,
        out_shape=[_sds((N_DEV,) + s.shape, s.dtype) for s in shards],
        in_specs=[hbm] * n, out_specs=[hbm] * n,
        scratch_shapes=[pltpu.SemaphoreType.DMA((7 * n,)), pltpu.SemaphoreType.DMA((7 * n,)),
                        pltpu.SemaphoreType.DMA((n,))],
    )(*shards)


def _all_gather_relay(shards, chunks, name):
    n = len(shards)
    units = []
    for a, s in enumerate(shards):
        if chunks[a] == 1:
            units.append((a, None, None))
        else:
            w = s.shape[-1] // chunks[a]
            units += [(a, q * w, w) for q in range(chunks[a])]
    nu = len(units)

    def body(*refs):
        ins, outs = refs[:n], refs[n:2 * n]
        send_sems, recv_sems, local_sems = refs[2 * n:]
        x, y, c = _pos()
        me, sib = (x, y, c), (x, y, 1 - c)
        x_nbr, y_nbr, diag = (1 - x, y, c), (x, 1 - y, c), (1 - x, 1 - y, c)
        came_from = (x + (1 - c) * (1 - 2 * x), y + c * (1 - 2 * y), c)
        pass_to = (x + c * (1 - 2 * x), y + (1 - c) * (1 - 2 * y), c)

        def slot(p):
            return 4 * p[0] + 2 * p[1] + p[2]

        def src_view(u):
            a, c0, w = units[u]
            return ins[a] if c0 is None else ins[a].at[:, pl.ds(c0, w)]

        def dst_view(u, p):
            a, c0, w = units[u]
            return outs[a].at[slot(p)] if c0 is None else outs[a].at[slot(p), :, pl.ds(c0, w)]

        def copy(u, k, block, to, from_shard=False):
            dst = dst_view(u, block)
            return pltpu.make_async_remote_copy(
                src_ref=src_view(u) if from_shard else dst, dst_ref=dst,
                send_sem=send_sems.at[7 * u + k], recv_sem=recv_sems.at[7 * u + k],
                device_id=to, device_id_type=MESH)

        mine = [pltpu.make_async_copy(src_view(u), dst_view(u, me), local_sems.at[u]) for u in range(nu)]
        for m in mine:
            m.start()
        sent = []
        for u in range(nu):
            sent += [copy(u, 0, me, sib, True), copy(u, 1, me, x_nbr, True), copy(u, 2, me, y_nbr, True)]
        for cp in sent:
            cp.start()
        for u in range(nu):
            copy(u, 1, x_nbr, me).wait_recv()
            copy(u, 2, y_nbr, me).wait_recv()
            sent += [copy(u, 3, came_from, pass_to), copy(u, 4, x_nbr, sib), copy(u, 5, y_nbr, sib)]
            for cp in sent[-3:]:
                cp.start()
        for u in range(nu):
            copy(u, 3, diag, me).wait_recv()
            sent.append(copy(u, 6, diag, sib))
            sent[-1].start()
        for u in range(nu):
            copy(u, 0, sib, me).wait_recv()
            for k, p in ((4, x_nbr), (5, y_nbr), (6, diag)):
                copy(u, k, (p[0], p[1], 1 - c), me).wait_recv()
        for cp in sent:
            cp.wait_send()
        for m in mine:
            m.wait()

    hbm = pl.BlockSpec(memory_space=pl.ANY)
    return _call(
        body, name=name,
        out_shape=[_sds((N_DEV,) + s.shape, s.dtype) for s in shards],
        in_specs=[hbm] * n, out_specs=[hbm] * n,
        scratch_shapes=[pltpu.SemaphoreType.DMA((7 * nu,)), pltpu.SemaphoreType.DMA((7 * nu,)),
                        pltpu.SemaphoreType.DMA((nu,))],
    )(*shards)


def _other_chips(x, y):
    return [(1 - x, y), (x, 1 - y), (1 - x, 1 - y)]


class _AgFull:
    n_sem, n_local = 7, 1

    def __init__(self, shard, forward_here=True):
        self.forward_here = forward_here
        self.ins = [shard]
        self.out_shapes = [_sds((N_DEV,) + shard.shape, shard.dtype)]
        self.aliases = []

    def _peers(self):
        x, y, c = _pos()
        return [(x, y, 1 - c)] + [(*chip, c) for chip in _other_chips(x, y)]

    def _sends(self, ins, outs, sems):
        send, recv, _, base, _ = sems
        x, y, c = _pos()
        mine = outs[0].at[4 * x + 2 * y + c]
        return [pltpu.make_async_remote_copy(src_ref=ins[0], dst_ref=mine, send_sem=send.at[base + k],
                                             recv_sem=recv.at[base + k], device_id=p, device_id_type=MESH)
                for k, p in enumerate(self._peers())]

    def _arrivals(self, outs, sems):
        send, recv, _, base, _ = sems
        res = []
        for k, p in enumerate(self._peers()):
            blk = outs[0].at[4 * p[0] + 2 * p[1] + p[2]]
            res.append(pltpu.make_async_remote_copy(src_ref=blk, dst_ref=blk, send_sem=send.at[base + k],
                                                    recv_sem=recv.at[base + k], device_id=p, device_id_type=MESH))
        return res

    def _own(self, ins, outs, sems):
        x, y, c = _pos()
        return pltpu.make_async_copy(ins[0], outs[0].at[4 * x + 2 * y + c], sems[2].at[sems[4]])

    def _forwards(self, outs, sems, core_of_block):
        return _forward_copies(outs[0], sems[0], sems[1], sems[3] + 4, core_of_block)

    def start(self, ins, outs, sems):
        self._own(ins, outs, sems).start()
        for cp in self._sends(ins, outs, sems):
            cp.start()

    def mid(self, ins, outs, sems):
        if self.forward_here:
            for cp in self._arrivals(outs, sems)[1:]:
                cp.wait_recv()
            for cp in self._forwards(outs, sems, "mine"):
                cp.start()

    def finish(self, ins, outs, sems):
        if self.forward_here:
            self._arrivals(outs, sems)[0].wait_recv()
            for cp in self._forwards(outs, sems, "sibling"):
                cp.wait_recv()
            for cp in self._sends(ins, outs, sems) + self._forwards(outs, sems, "mine"):
                cp.wait_send()
        else:
            for cp in self._arrivals(outs, sems):
                cp.wait_recv()
            for cp in self._sends(ins, outs, sems):
                cp.wait_send()
        self._own(ins, outs, sems).wait()


def _forward_copies(gathered_ref, send, recv, base, core_of_block):
    x, y, c = _pos()
    res = []
    for k, chip in enumerate(_other_chips(x, y)):
        blk = gathered_ref.at[4 * chip[0] + 2 * chip[1] + (c if core_of_block == "mine" else 1 - c)]
        res.append(pltpu.make_async_remote_copy(src_ref=blk, dst_ref=blk, send_sem=send.at[base + k],
                                                recv_sem=recv.at[base + k], device_id=(x, y, 1 - c),
                                                device_id_type=MESH))
    return res


class _AgForward:
    n_sem, n_local = 3, 0

    def __init__(self, gathered):
        self.ins = [gathered]
        self.out_shapes = [_sds(gathered.shape, gathered.dtype)]
        self.aliases = [(0, 0)]

    def start(self, ins, outs, sems):
        for cp in _forward_copies(outs[0], sems[0], sems[1], sems[3], "mine"):
            cp.start()

    def finish(self, ins, outs, sems):
        for cp in _forward_copies(outs[0], sems[0], sems[1], sems[3], "sibling"):
            cp.wait_recv()
        for cp in _forward_copies(outs[0], sems[0], sems[1], sems[3], "mine"):
            cp.wait_send()


class _RsSibling:
    n_sem, n_local = 4, 0

    def __init__(self, part):
        self.ins = [part]
        self.out_shapes = [_sds((4,) + part.shape[1:], part.dtype)]
        self.aliases = []

    def _copies(self, ins, outs, sems):
        send, recv, _, base, _ = sems
        x, y, c = _pos()
        return [pltpu.make_async_remote_copy(src_ref=ins[0].at[2 * q + (1 - c)], dst_ref=outs[0].at[q],
                                             send_sem=send.at[base + q], recv_sem=recv.at[base + q],
                                             device_id=(x, y, 1 - c), device_id_type=MESH) for q in range(4)]

    def start(self, ins, outs, sems):
        for cp in self._copies(ins, outs, sems):
            cp.start()

    def finish(self, ins, outs, sems):
        for cp in self._copies(ins, outs, sems):
            cp.wait()


class _RsChips:
    n_sem, n_local = 3, 0

    def __init__(self, pair):
        self.ins = [pair]
        self.out_shapes = [_sds((3,) + pair.shape[1:], pair.dtype)]
        self.aliases = []

    def _copies(self, ins, outs, sems):
        send, recv, _, base, _ = sems
        x, y, c = _pos()
        return [pltpu.make_async_remote_copy(src_ref=ins[0].at[2 * chip[0] + chip[1]], dst_ref=outs[0].at[k],
                                             send_sem=send.at[base + k], recv_sem=recv.at[base + k],
                                             device_id=(*chip, c), device_id_type=MESH)
                for k, chip in enumerate(_other_chips(x, y))]

    def start(self, ins, outs, sems):
        for cp in self._copies(ins, outs, sems):
            cp.start()

    def finish(self, ins, outs, sems):
        for cp in self._copies(ins, outs, sems):
            cp.wait()


def _hosted(body, tasks, *, grid, in_specs, out_specs, out_shape, scratch_shapes=(), name, semantics, operands,
            aliases=None):
    in_specs, out_specs, out_shape = list(in_specs), list(out_specs), list(out_shape)
    scratch_shapes = list(scratch_shapes)
    aliases = dict(aliases or {})
    if not tasks:
        res = _call(body, name=name, grid=grid, in_specs=in_specs, out_specs=out_specs, out_shape=out_shape,
                    scratch_shapes=scratch_shapes, input_output_aliases=aliases,
                    compiler_params=_cp(*semantics))(*operands)
        return list(res), []
    n_in, n_out, n_scr = len(in_specs), len(out_specs), len(scratch_shapes)
    t_ins = [a for t in tasks for a in t.ins]
    t_outs = [o for t in tasks for o in t.out_shapes]
    i0, o0 = n_in, n_out
    for t in tasks:
        for (i, o) in t.aliases:
            aliases[i0 + i] = o0 + o
        i0 += len(t.ins)
        o0 += len(t.out_shapes)
    n_sem = sum(t.n_sem for t in tasks)
    n_local = max(1, sum(t.n_local for t in tasks))
    n_steps = 1
    for g in grid:
        n_steps *= g
    mid_step = min(n_steps - 1, (n_steps * MID_STEP_PERCENT) // 100)

    def wrapped(*refs):
        cut = [n_in, len(t_ins), n_out, len(t_outs), n_scr]
        parts, at = [], 0
        for n in cut:
            parts.append(refs[at:at + n])
            at += n
        ins, tin, outs, tout, scratch = parts
        send, recv, local = refs[at:]
        step = pl.program_id(0)
        for d in range(1, len(grid)):
            step = step * grid[d] + pl.program_id(d)

        def each(method):
            i, o, s, l = 0, 0, 0, 0
            for t in tasks:
                if hasattr(t, method):
                    getattr(t, method)(tin[i:i + len(t.ins)], tout[o:o + len(t.out_shapes)], (send, recv, local, s, l))
                i, o, s, l = i + len(t.ins), o + len(t.out_shapes), s + t.n_sem, l + t.n_local

        @pl.when(step == 0)
        def _():
            each("start")

        body(*ins, *outs, *scratch)

        @pl.when(step == mid_step)
        def _():
            each("mid")

        @pl.when(step == n_steps - 1)
        def _():
            each("finish")

    hbm = pl.BlockSpec(memory_space=pl.ANY)
    res = _call(
        wrapped, name=name, grid=grid,
        in_specs=in_specs + [hbm] * len(t_ins), out_specs=out_specs + [hbm] * len(t_outs),
        out_shape=out_shape + t_outs,
        scratch_shapes=scratch_shapes + [pltpu.SemaphoreType.DMA((n_sem,)), pltpu.SemaphoreType.DMA((n_sem,)),
                                         pltpu.SemaphoreType.DMA((n_local,))],
        input_output_aliases=aliases,
        compiler_params=_cp(*(["arbitrary"] * len(grid))),
    )(*operands, *t_ins)
    res = list(res)
    task_outs, o = [], n_out
    for t in tasks:
        task_outs.append(res[o:o + len(t.out_shapes)])
        o += len(t.out_shapes)
    return res[:n_out], task_outs


def _pair_sum(part, got, core, name):
    _, R, C = part.shape
    tr = _tile(R, TILE["pair"]) if R % 16 == 0 else R

    def body(core_ref, p_ref, g_ref, o_ref):
        o_ref[...] = (p_ref[...].astype(F32) + g_ref[...].astype(F32)).astype(o_ref.dtype)

    return _call(
        body, name=name,
        grid_spec=pltpu.PrefetchScalarGridSpec(
            num_scalar_prefetch=1, grid=(4, R // tr),
            in_specs=[pl.BlockSpec((None, tr, C), lambda q, i, cr: (2 * q + cr[0], i, 0)),
                      pl.BlockSpec((None, tr, C), lambda q, i, cr: (q, i, 0))],
            out_specs=pl.BlockSpec((None, tr, C), lambda q, i, cr: (q, i, 0))),
        out_shape=_sds((4, R, C), part.dtype),
        compiler_params=_cp("parallel", "parallel"),
    )(core, part, got)


def _sum_slots(gathered, name):
    _, R, C = gathered.shape

    def body(g_ref, o_ref):
        acc = g_ref[0]
        for s in range(1, N_DEV):
            acc = acc + g_ref[s]
        o_ref[...] = acc

    return _call(body, name=name, out_shape=_sds((R, C), F32))(gathered)


def _sigmoid(z):
    return jax.nn.sigmoid(z)


def _softplus(z):
    e = jnp.exp(-jnp.abs(z))
    log1p_e = jnp.where(e < 0.01, e * (1.0 - e * (0.5 - e * (1.0 / 3.0))), jnp.log(1.0 + e))
    return jnp.maximum(z, 0.0) + log1p_e


_GELU_K = 0.7978845608028654
_GELU_C = 0.044715


def _gelu_and_grad(z):
    t = jnp.tanh(_GELU_K * (z + _GELU_C * z * z * z))
    g = 0.5 * z * (1.0 + t)
    dg = 0.5 * (1.0 + t) + 0.5 * z * (1.0 - t * t) * _GELU_K * (1.0 + 3.0 * _GELU_C * z * z)
    return g, dg


def _gelu(z):
    t = jnp.tanh(_GELU_K * (z + _GELU_C * z * z * z))
    return 0.5 * z * (1.0 + t)


def _row_ids(tile_index, tm, width=1):
    return tile_index * tm + lax.broadcasted_iota(jnp.int32, (tm, width), 0)


def _shift_down(prev, cur, k):
    if k == 0:
        return cur
    ext = jnp.concatenate([prev, cur], axis=0)
    return pltpu.roll(ext, k, axis=0)[prev.shape[0]:]


def _shift_up(cur, nxt, k):
    if k == 0:
        return cur
    ext = jnp.concatenate([cur, nxt], axis=0)
    return pltpu.roll(ext, ext.shape[0] - k, axis=0)[:cur.shape[0]]


def _lru_gates(r, sp):
    z = LRU_C * r * sp
    a = jnp.exp(-z)
    t = jnp.tanh(z)
    mult = jnp.sqrt(2.0 * t / (1.0 + t))
    return a, mult


def _scan_chunks(a_ref, b_ref, out_ref, carry, n_rows, reverse):
    n_chunks = n_rows // 8
    cols = a_ref.shape[1]
    rid = lax.broadcasted_iota(jnp.int32, (8, cols), 0)
    edge = 0 if reverse else 7
    group = SCAN_GROUP if n_chunks % SCAN_GROUP == 0 else 1

    def local_scan(ci):
        rows = pl.ds(pl.multiple_of(ci * 8, 8), 8)
        a = a_ref[rows, :]
        b = b_ref[rows, :]
        for s in (1, 2, 4):
            if reverse:
                keep = rid < 8 - s
                a_n, b_n = pltpu.roll(a, 8 - s, axis=0), pltpu.roll(b, 8 - s, axis=0)
            else:
                keep = rid >= s
                a_n, b_n = pltpu.roll(a, s, axis=0), pltpu.roll(b, s, axis=0)
            b = a * jnp.where(keep, b_n, 0.0) + b
            a = a * jnp.where(keep, a_n, 1.0)
        a_e = jnp.sum(jnp.where(rid == edge, a, 0.0), axis=0, keepdims=True)
        b_e = jnp.sum(jnp.where(rid == edge, b, 0.0), axis=0, keepdims=True)
        return rows, a, b, a_e, b_e

    def chunks(k, h):
        first = k * group
        scans = [local_scan((n_chunks - 1 - (first + j)) if reverse else first + j) for j in range(group)]
        for rows, a, b, a_e, b_e in scans:
            out_ref[rows, :] = a * h + b
            h = a_e * h + b_e
        return h

    return lax.fori_loop(0, n_chunks // group, chunks, carry)


def _norm_fwd(h, g, name):
    Tp, D = h.shape
    tm = _tile(Tp, TILE["norm"])

    def body(h_ref, g_ref, u_ref, r_ref):
        x = h_ref[...]
        r = lax.rsqrt(jnp.mean(x * x, axis=-1, keepdims=True) + NORM_EPS)
        u_ref[...] = (x * r * g_ref[...]).astype(BF16)
        r_ref[...] = r

    return _call(
        body, name=name, grid=(Tp // tm,),
        in_specs=[pl.BlockSpec((tm, D), lambda i: (i, 0)), pl.BlockSpec((1, D), lambda i: (0, 0))],
        out_specs=[pl.BlockSpec((tm, D), lambda i: (i, 0)), pl.BlockSpec((tm, 1), lambda i: (i, 0))],
        out_shape=[_sds((Tp, D), BF16), _sds((Tp, 1), F32)],
        compiler_params=_cp("parallel"),
    )(h, g)


def _proj_fwd(u, w_slots, name, tasks=()):
    Tp, K = u.shape
    S, _, n = w_slots.shape
    tm = _tile(Tp, TILE["proj"])

    def body(a_ref, b_ref, o_ref):
        o_ref[...] = jnp.dot(a_ref[...], b_ref[...], preferred_element_type=F32)

    (proj,), extra = _hosted(
        body, tasks, name=name, grid=(S, Tp // tm),
        in_specs=[pl.BlockSpec((tm, K), lambda j, i: (i, 0)), pl.BlockSpec((None, K, n), lambda j, i: (j, 0, 0))],
        out_specs=[pl.BlockSpec((tm, n), lambda j, i: (i, j))],
        out_shape=[_sds((Tp, S * n), F32)],
        semantics=("parallel", "parallel"), operands=(u, w_slots))
    return proj, extra


def _mlp1_fwd(u2, w_slots, name, tasks=()):
    Tp, K = u2.shape
    S, _, n = w_slots.shape
    tm = _tile(Tp, TILE["mlp1"])

    def body(a_ref, b_ref, act_ref, a1_ref):
        a1 = jnp.dot(a_ref[...], b_ref[...], preferred_element_type=F32)
        relu = jnp.maximum(a1, 0.0)
        act_ref[...] = (relu * relu).astype(BF16)
        a1_ref[...] = a1.astype(BF16)

    return _hosted(
        body, tasks, name=name, grid=(S, Tp // tm),
        in_specs=[pl.BlockSpec((tm, K), lambda j, i: (i, 0)), pl.BlockSpec((None, K, n), lambda j, i: (j, 0, 0))],
        out_specs=[pl.BlockSpec((tm, n), lambda j, i: (i, j))] * 2,
        out_shape=[_sds((Tp, S * n), BF16)] * 2,
        semantics=("parallel", "parallel"), operands=(u2, w_slots))


def _pool_fwd(proj, pool_w, name):
    Tp = proj.shape[0]
    G, Cg, _ = pool_w.shape
    D = G * Cg
    tm = _tile(Tp, TILE["pool"])

    def body(v_ref, w_ref, d_ref, y_ref, prev_ref):
        t = pl.program_id(0)

        @pl.when(t == 0)
        def _():
            prev_ref[...] = jnp.zeros_like(prev_ref)

        rows = _row_ids(t, tm)
        for g, win in enumerate(POOL_WINDOWS):
            cols = slice(g * Cg, (g + 1) * Cg)
            v = v_ref[:, cols]
            s = jnp.concatenate([prev_ref[:, cols], v], axis=0)
            k = 1
            while k < win:
                s = s + pltpu.roll(s, k, axis=0)
                k *= 2
            cnt = jnp.minimum(rows + 1, win).astype(F32)
            d = s[MAX_WINDOW:] / cnt - v
            d_ref[:, cols] = d.astype(BF16)
            y_ref[:, cols] = jnp.dot(d.astype(BF16), w_ref[g], preferred_element_type=F32)
        prev_ref[...] = v_ref[tm - MAX_WINDOW:, :]

    return _call(
        body, name=name, grid=(Tp // tm,),
        in_specs=[pl.BlockSpec((tm, D), lambda t: (t, 0)), pl.BlockSpec((G, Cg, Cg), lambda t: (0, 0, 0))],
        out_specs=[pl.BlockSpec((tm, D), lambda t: (t, 0))] * 2,
        out_shape=[_sds((Tp, D), BF16), _sds((Tp, D), F32)],
        scratch_shapes=[pltpu.VMEM((MAX_WINDOW, D), F32)],
        compiler_params=_cp("arbitrary"),
    )(proj, pool_w)


def _lru_fwd(proj, y_pool, scale, conv_w, conv_b, wa, ba, wx, bx, lam, name, tasks=()):
    Tp = proj.shape[0]
    H, hd, _ = wa.shape
    D = H * hd
    tm = _tile(Tp, TILE["lru"])
    nb = D // hd

    def body(vl_ref, vg_ref, gp_ref, gl_ref, y_ref, sc_ref, cw_ref, cb_ref, wa_ref, ba_ref, wx_ref, bx_ref,
             lam_ref, xc_ref, r_ref, i_ref, a_ref, mult_ref, hs_ref, m_ref, prev_ref, carry_ref, b_s):
        t = pl.program_id(1)

        @pl.when(t == 0)
        def _():
            prev_ref[...] = jnp.zeros_like(prev_ref)
            carry_ref[...] = jnp.zeros_like(carry_ref)

        v = vl_ref[...]
        prev = prev_ref[...]
        xc = jnp.zeros_like(v) + cb_ref[...]
        for k in range(CONV_WIDTH):
            xc = xc + cw_ref[k:k + 1, :] * _shift_down(prev, v, CONV_WIDTH - 1 - k)
        prev_ref[...] = v[tm - HALO:, :]
        xcb = xc.astype(BF16)
        r = _sigmoid(jnp.dot(xcb, wa_ref[...], preferred_element_type=F32) + ba_ref[...])
        i = _sigmoid(jnp.dot(xcb, wx_ref[...], preferred_element_type=F32) + bx_ref[...])
        a, mult = _lru_gates(r, _softplus(-lam_ref[...]))
        a_ref[...] = a
        mult_ref[...] = mult
        b_s[...] = mult * (i * xc)
        xc_ref[...] = xc
        r_ref[...] = r
        i_ref[...] = i
        carry_ref[0:1, :] = _scan_chunks(a_ref, b_s, hs_ref, carry_ref[0:1, :], tm, reverse=False)
        lru_out = hs_ref[...] * _gelu(vg_ref[...])
        pool_out = y_ref[...] * sc_ref[...]
        m_ref[...] = (_sigmoid(gp_ref[...]) * pool_out + _sigmoid(gl_ref[...]) * lru_out).astype(BF16)

    def piece(p):
        return pl.BlockSpec((tm, hd), lambda h, t: (t, p * nb + h))

    blk = pl.BlockSpec((tm, hd), lambda h, t: (t, h))
    vec = pl.BlockSpec((1, hd), lambda h, t: (0, h))
    mat = pl.BlockSpec((None, hd, hd), lambda h, t: (h, 0, 0))
    bias = pl.BlockSpec((None, 1, hd), lambda h, t: (h, 0, 0))
    return _hosted(
        body, tasks, name=name, grid=(H, Tp // tm),
        in_specs=[piece(1), piece(2), piece(3), piece(4), blk, vec,
                  pl.BlockSpec((CONV_WIDTH, hd), lambda h, t: (0, h)), vec, mat, bias, mat, bias, vec],
        out_specs=[blk] * 7,
        out_shape=[_sds((Tp, D), F32)] * 6 + [_sds((Tp, D), BF16)],
        scratch_shapes=[pltpu.VMEM((HALO, hd), F32), pltpu.VMEM((8, hd), F32), pltpu.VMEM((tm, hd), F32)],
        semantics=("parallel", "arbitrary"),
        operands=(proj, proj, proj, proj, y_pool, scale, conv_w, conv_b, wa, ba.reshape(H, 1, hd), wx,
                  bx.reshape(H, 1, hd), lam))


def _wout_norm_fwd(merged, w_out, h0, g2, name, tasks=()):
    Tp, D = h0.shape
    tm = _tile(Tp, TILE["wout"])

    def body(m_ref, w_ref, h0_ref, g_ref, h1_ref, u2_ref, r2_ref):
        h1 = h0_ref[...] + jnp.dot(m_ref[...], w_ref[...], preferred_element_type=F32)
        r = lax.rsqrt(jnp.mean(h1 * h1, axis=-1, keepdims=True) + NORM_EPS)
        h1_ref[...] = h1
        u2_ref[...] = (h1 * r * g_ref[...]).astype(BF16)
        r2_ref[...] = r

    row = pl.BlockSpec((tm, D), lambda i: (i, 0))
    return _hosted(
        body, tasks, name=name, grid=(Tp // tm,),
        in_specs=[row, pl.BlockSpec((D, D), lambda i: (0, 0)), row, pl.BlockSpec((1, D), lambda i: (0, 0))],
        out_specs=[row, row, pl.BlockSpec((tm, 1), lambda i: (i, 0))],
        out_shape=[_sds((Tp, D), F32), _sds((Tp, D), BF16), _sds((Tp, 1), F32)],
        semantics=("parallel",), operands=(merged, w_out, h0, g2))


def _mlp2_loss(act, w2, h1, target, gf, n_meta, seq, name):
    Tp, D = h1.shape
    K = act.shape[1]
    tm = _tile(Tp, TILE["mlp2"])
    tk = min(K, MLP2_K)
    nk = K // tk

    rc = _tile(tm, EPILOGUE_ROWS)
    nt = Tp // tm
    last_rows = n_meta + seq - (nt - 1) * tm
    assert nt >= 2 and n_meta % 8 == 0 and 0 < last_rows <= tm and last_rows % 8 == 0

    def body(a_ref, w_ref, h1_hbm, t_hbm, g_ref, dh_ref, dhb_ref, loss_ref, dg_ref, h1_buf, t_buf, sems):
        i, k = pl.program_id(0), pl.program_id(1)
        tile_rows = pl.ds(pl.multiple_of(i * tm, tm), tm)
        fetch_h1 = pltpu.make_async_copy(h1_hbm.at[tile_rows, :], h1_buf, sems.at[0])
        fetch_t = [
            (i == 0, pltpu.make_async_copy(t_hbm.at[pl.ds(0, tm - n_meta), :], t_buf.at[pl.ds(n_meta, tm - n_meta), :],
                                           sems.at[1])),
            ((i > 0) & (i < nt - 1),
             pltpu.make_async_copy(t_hbm.at[pl.ds(pl.multiple_of(i * tm - n_meta, 8), tm), :], t_buf, sems.at[1])),
            (i == nt - 1, pltpu.make_async_copy(t_hbm.at[pl.ds((nt - 1) * tm - n_meta, last_rows), :],
                                                t_buf.at[pl.ds(0, last_rows), :], sems.at[1])),
        ]

        @pl.when(k == 0)
        def _():
            fetch_h1.start()
            dh_ref[...] = jnp.zeros_like(dh_ref)

        for cond, f in fetch_t:
            @pl.when(cond & (k == 0))
            def _(f=f):
                f.start()

        @pl.when((i == 0) & (k == 0))
        def _():
            loss_ref[...] = jnp.zeros_like(loss_ref)
            dg_ref[...] = jnp.zeros_like(dg_ref)
            t_buf[0:n_meta, :] = jnp.zeros((n_meta, D), F32)

        dh_ref[...] += jnp.dot(a_ref[...], w_ref[...], preferred_element_type=F32)

        for cond, f in fetch_t:
            @pl.when(cond & (k == nk - 1))
            def _(f=f):
                f.wait()

        @pl.when(k == nk - 1)
        def _():
            fetch_h1.wait()
            g = g_ref[...]

            def chunk(c, carry):
                loss_acc, dg_acc = carry
                rows = pl.ds(pl.multiple_of(c * rc, rc), rc)
                h2 = h1_buf[rows, :] + dh_ref[rows, :]
                r = lax.rsqrt(jnp.mean(h2 * h2, axis=-1, keepdims=True) + NORM_EPS)
                out = h2 * r * g
                row_id = i * tm + c * rc + lax.broadcasted_iota(jnp.int32, (rc, 1), 0)
                valid = (row_id >= n_meta) & (row_id < n_meta + seq)
                diff = jnp.where(valid, out - t_buf[rows, :], 0.0)
                dout = diff / D
                dog = dout * g
                dh = r * dog - h2 * (r * r * r * jnp.mean(dog * h2, axis=-1, keepdims=True))
                dh_ref[rows, :] = dh
                dhb_ref[rows, :] = dh.astype(BF16)
                loss_acc = loss_acc + 0.5 * jnp.sum(jnp.mean(diff * diff, axis=-1, keepdims=True), axis=0, keepdims=True)
                return loss_acc, dg_acc + jnp.sum(dout * (h2 * r), axis=0, keepdims=True)

            loss_sum, dg_sum = lax.fori_loop(0, tm // rc, chunk, (jnp.zeros((1, 1), F32), jnp.zeros((1, D), F32)))
            loss_ref[...] += loss_sum
            dg_ref[...] += dg_sum

    row = pl.BlockSpec((tm, D), lambda i, k: (i, 0))
    hbm = pl.BlockSpec(memory_space=pl.ANY)
    return _call(
        body, name=name, grid=(Tp // tm, nk),
        in_specs=[pl.BlockSpec((tm, tk), lambda i, k: (i, k)), pl.BlockSpec((tk, D), lambda i, k: (k, 0)),
                  hbm, hbm, pl.BlockSpec((1, D), lambda i, k: (0, 0))],
        out_specs=[row, row, pl.BlockSpec((8, 128), lambda i, k: (0, 0)), pl.BlockSpec((1, D), lambda i, k: (0, 0))],
        out_shape=[_sds((Tp, D), F32), _sds((Tp, D), BF16), _sds((8, 128), F32), _sds((1, D), F32)],
        scratch_shapes=[pltpu.VMEM((tm, D), F32), pltpu.VMEM((tm, D), F32), pltpu.SemaphoreType.DMA((2,))],
        compiler_params=_cp("arbitrary", "arbitrary"),
    )(act, w2, h1, target, gf)


def _dact_bwd(dh2b, w2_slots, a1, name):
    Tp, D = dh2b.shape
    S, n, _ = w2_slots.shape
    tm = _tile(Tp, TILE["dact"])

    def body(g_ref, w_ref, a1_ref, o_ref):
        dact = lax.dot_general(g_ref[...], w_ref[...], _NT, preferred_element_type=F32)
        o_ref[...] = (dact * (2.0 * jnp.maximum(a1_ref[...].astype(F32), 0.0))).astype(BF16)

    return _call(
        body, name=name, grid=(S, Tp // tm),
        in_specs=[pl.BlockSpec((tm, D), lambda j, i: (i, 0)), pl.BlockSpec((None, n, D), lambda j, i: (j, 0, 0)),
                  pl.BlockSpec((tm, n), lambda j, i: (i, j))],
        out_specs=pl.BlockSpec((tm, n), lambda j, i: (i, j)),
        out_shape=_sds((Tp, S * n), BF16),
        compiler_params=_cp("parallel", "parallel"),
    )(dh2b, w2_slots, a1)


def _weight_grad(a, g, blocks, block_a, name, tasks=(), part=(0, 1)):
    Tp, Ka = a.shape
    Ng = g.shape[1]
    p, parts = part
    assert parts == 1 or not block_a
    ka = Ka // blocks if block_a else Ka // parts
    ng = Ng if block_a else Ng // blocks
    tt = _tile(Tp, TILE["tn"])
    nt = Tp // tt

    def body(a_ref, g_ref, o_ref, acc_ref):
        t = pl.program_id(1)

        @pl.when(t == 0)
        def _():
            acc_ref[...] = jnp.zeros_like(acc_ref)

        acc_ref[...] += lax.dot_general(a_ref[...], g_ref[...], _TN, preferred_element_type=F32)

        @pl.when(t == nt - 1)
        def _():
            o_ref[...] = acc_ref[...].astype(o_ref.dtype)

    if block_a:
        a_spec = pl.BlockSpec((tt, ka), lambda j, t: (t, j))
        g_spec = pl.BlockSpec((tt, ng), lambda j, t: (t, 0))
    else:
        a_spec = pl.BlockSpec((tt, ka), lambda j, t: (t, p))
        g_spec = pl.BlockSpec((tt, ng), lambda j, t: (t, j))
    (dw,), extra = _hosted(
        body, tasks, name=name, grid=(blocks, nt),
        in_specs=[a_spec, g_spec],
        out_specs=[pl.BlockSpec((None, ka, ng), lambda j, t: (j, 0, 0))],
        out_shape=[_sds((blocks, ka, ng), BF16)],
        scratch_shapes=[pltpu.VMEM((ka, ng), F32)],
        semantics=("parallel", "arbitrary"), operands=(a, g))
    return dw, extra


def _nt_norm_bwd(dz, w_parts, dres, hin, rin, g, want_bf16, name, tasks=(), tiles=None, earlier=None):
    Tp, D = hin.shape
    P = len(w_parts)
    S, _, n = w_parts[0].shape
    K = S * P
    tm = _tile(Tp, TILE["nt"])
    t0, nt = tiles if tiles is not None else (0, Tp // tm)
    assert not (want_bf16 and earlier is not None)

    rc = _tile(tm, EPILOGUE_ROWS)

    def body(dz_ref, *rest):
        w_refs, (dres_hbm, h_hbm, r_ref, g_ref), rest = rest[:P], rest[P:P + 4], rest[P + 4:]
        if earlier is not None:
            _, dg0_ref, dh_ref, dg_ref, dres_buf, h_buf, sems = rest
        elif want_bf16:
            dh_ref, dhb_ref, dg_ref, dres_buf, h_buf, sems = rest
        else:
            dh_ref, dg_ref, dres_buf, h_buf, sems = rest
        i, k = pl.program_id(0), pl.program_id(1)
        tile_rows = pl.ds(pl.multiple_of((t0 + i) * tm, tm), tm)
        fetch = [pltpu.make_async_copy(dres_hbm.at[tile_rows, :], dres_buf, sems.at[0]),
                 pltpu.make_async_copy(h_hbm.at[tile_rows, :], h_buf, sems.at[1])]

        @pl.when(k == 0)
        def _():
            for f in fetch:
                f.start()
            dh_ref[...] = jnp.zeros_like(dh_ref)

        @pl.when((i == 0) & (k == 0))
        def _():
            dg_ref[...] = jnp.zeros_like(dg_ref) if earlier is None else dg0_ref[...]

        for q in range(P):
            @pl.when(k % P == q)
            def _(q=q):
                dh_ref[...] += lax.dot_general(dz_ref[...], w_refs[q][...], _NT, preferred_element_type=F32)

        @pl.when(k == K - 1)
        def _():
            for f in fetch:
                f.wait()
            g = g_ref[...]

            def chunk(c, dg_acc):
                rows = pl.ds(pl.multiple_of(c * rc, rc), rc)
                du = dh_ref[rows, :]
                h = h_buf[rows, :]
                r = r_ref[rows, :]
                dug = du * g
                dh = dres_buf[rows, :] + r * dug - h * (r * r * r * jnp.mean(dug * h, axis=-1, keepdims=True))
                dh_ref[rows, :] = dh
                if want_bf16:
                    dhb_ref[rows, :] = dh.astype(BF16)
                return dg_acc + jnp.sum(du * (h * r), axis=0, keepdims=True)

            dg_ref[...] += lax.fori_loop(0, tm // rc, chunk, jnp.zeros((1, D), F32))

    row = pl.BlockSpec((tm, D), lambda i, k: (t0 + i, 0))
    vec = pl.BlockSpec((1, D), lambda i, k: (0, 0))
    hbm = pl.BlockSpec(memory_space=pl.ANY)
    out_specs = [row] + ([row] if want_bf16 else []) + [vec]
    out_shape = [_sds((Tp, D), F32)] + ([_sds((Tp, D), BF16)] if want_bf16 else []) + [_sds((1, D), F32)]
    in_specs = ([pl.BlockSpec((tm, n), lambda i, k: (t0 + i, k))]
                + [pl.BlockSpec((None, D, n), lambda i, k: (k // P, 0, 0))] * P
                + [hbm, hbm, pl.BlockSpec((tm, 1), lambda i, k: (t0 + i, 0)), vec])
    operands = (dz, *w_parts, dres, hin, rin, g)
    aliases = {}
    if earlier is not None:
        in_specs += [hbm, vec]
        operands += tuple(earlier)
        aliases = {P + 5: 0}
    return _hosted(
        body, tasks, name=name, grid=(nt, K), in_specs=in_specs, out_specs=out_specs, out_shape=out_shape,
        scratch_shapes=[pltpu.VMEM((tm, D), F32), pltpu.VMEM((tm, D), F32), pltpu.SemaphoreType.DMA((2,))],
        semantics=("arbitrary", "arbitrary"), operands=operands, aliases=aliases)


def _dmerged_bwd(dh1b, w_out, name):
    Tp, D = dh1b.shape
    tm = _tile(Tp, TILE["dmerged"])

    def body(g_ref, w_ref, o_ref):
        o_ref[...] = lax.dot_general(g_ref[...], w_ref[...], _NT, preferred_element_type=F32)

    row = pl.BlockSpec((tm, D), lambda i: (i, 0))
    return _call(
        body, name=name, grid=(Tp // tm,),
        in_specs=[row, pl.BlockSpec((D, D), lambda i: (0, 0))],
        out_specs=row, out_shape=_sds((Tp, D), F32),
        compiler_params=_cp("parallel"),
    )(dh1b, w_out)


def _pool_bwd(dmerged, proj, y_pool, d_pool, scale, pool_w, name, tasks=()):
    Tp, D = dmerged.shape
    G, Cg, _ = pool_w.shape
    tm = _tile(Tp, TILE["pool"])
    nt = Tp // tm

    def body(dm_ref, gp_ref, y_ref, d_ref, sc_ref, w_ref, dproj_hbm, dw_ref, dsc_ref, next_ref, out_buf, out_sems):
        t = pl.program_id(0)
        tile = nt - 1 - t
        slot = t % 2
        dv_ref, dgp_ref = out_buf.at[slot, 0], out_buf.at[slot, 1]
        tile_rows = pl.ds(pl.multiple_of(tile * tm, tm), tm)

        def out_copies(s):
            return [pltpu.make_async_copy(out_buf.at[s, k], dproj_hbm.at[tile_rows, pl.ds(piece * D, D)],
                                          out_sems.at[2 * s + k]) for k, piece in enumerate((0, 3))]

        @pl.when(t >= 2)
        def _():
            for cp in out_copies(slot):
                cp.wait()

        @pl.when(t == 0)
        def _():
            next_ref[...] = jnp.zeros_like(next_ref)
            dw_ref[...] = jnp.zeros_like(dw_ref)
            dsc_ref[...] = jnp.zeros_like(dsc_ref)

        rows = _row_ids(tile, tm)
        dm = dm_ref[...]
        y = y_ref[...]
        sc = sc_ref[...]
        sg = _sigmoid(gp_ref[...])
        dpo = dm * sg
        dgp_ref[...] = (dm * (y * sc) * sg * (1.0 - sg)).astype(BF16)
        dsc_ref[...] += jnp.sum(dpo * y, axis=0, keepdims=True)
        dyb = (dpo * sc).astype(BF16)
        for g, win in enumerate(POOL_WINDOWS):
            cols = slice(g * Cg, (g + 1) * Cg)
            dy = dyb[:, cols]
            dd = lax.dot_general(dy, w_ref[g], _NT, preferred_element_type=F32)
            dw_ref[g] += lax.dot_general(d_ref[:, cols], dy, _TN, preferred_element_type=F32)
            q = dd / jnp.minimum(rows + 1, win).astype(F32)
            s = jnp.concatenate([q, next_ref[:, cols]], axis=0)
            k = 1
            while k < win:
                s = s + pltpu.roll(s, s.shape[0] - k, axis=0)
                k *= 2
            dv_ref[:, cols] = (s[:tm] - dd).astype(BF16)
            next_ref[:, cols] = q[:MAX_WINDOW]
        for cp in out_copies(slot):
            cp.start()

        @pl.when(t == nt - 1)
        def _():
            for cp in out_copies(slot) + (out_copies(1 - slot) if nt > 1 else []):
                cp.wait()

    row = pl.BlockSpec((tm, D), lambda t: (nt - 1 - t, 0))
    return _hosted(
        body, tasks, name=name, grid=(nt,),
        in_specs=[row, pl.BlockSpec((tm, D), lambda t: (nt - 1 - t, 3)), row, row,
                  pl.BlockSpec((1, D), lambda t: (0, 0)), pl.BlockSpec((G, Cg, Cg), lambda t: (0, 0, 0))],
        out_specs=[pl.BlockSpec(memory_space=pl.ANY), pl.BlockSpec((G, Cg, Cg), lambda t: (0, 0, 0)),
                   pl.BlockSpec((1, D), lambda t: (0, 0))],
        out_shape=[_sds((Tp, proj.shape[1]), BF16), _sds((G, Cg, Cg), F32), _sds((1, D), F32)],
        scratch_shapes=[pltpu.VMEM((MAX_WINDOW, D), F32), pltpu.VMEM((2, 2, tm, D), BF16),
                        pltpu.SemaphoreType.DMA((4,))],
        semantics=("arbitrary",), operands=(dmerged, proj, y_pool, d_pool, scale, pool_w))


LRU_SMALL_ROWS = 8


def _lru_bwd(dmerged, proj, xc, r_gate, i_gate, a_gate, mult_gate, hs, lam, conv_w, wa, wx, dproj, name, tasks=()):
    Tp, D = dmerged.shape
    H, hd, _ = wa.shape
    tm = _tile(Tp, TILE["lru"])
    nt = Tp // tm
    nb = D // hd
    halo_blocks = tm // HALO

    def body(dm_ref, vl_ref, vg_ref, gl_ref, xc_ref, r_ref, i_ref, a_ref, mult_ref, hs_ref, hsp_ref, lam_ref, cw_ref,
             wa_ref, wx_ref, _, dproj_hbm, dwa_ref, dwx_ref, small_ref,
             mu_next_ref, dxc_next_ref, q_s, mu_s, out_buf, out_sems):
        h_id, t = pl.program_id(0), pl.program_id(1)
        tile = nt - 1 - t
        step = h_id * nt + t
        slot = step % 2
        dvl_ref, dvg_ref, dgl_ref = out_buf.at[slot, 0], out_buf.at[slot, 1], out_buf.at[slot, 2]
        tile_rows = pl.ds(pl.multiple_of(tile * tm, tm), tm)

        def out_copies(s):
            return [pltpu.make_async_copy(
                out_buf.at[s, k], dproj_hbm.at[tile_rows, pl.ds(pl.multiple_of((piece * nb + h_id) * hd, hd), hd)],
                out_sems.at[3 * s + k]) for k, piece in enumerate((1, 2, 4))]

        @pl.when(step >= 2)
        def _():
            for cp in out_copies(slot):
                cp.wait()

        @pl.when(t == 0)
        def _():
            mu_next_ref[...] = jnp.zeros_like(mu_next_ref)
            dxc_next_ref[...] = jnp.zeros_like(dxc_next_ref)
            dwa_ref[...] = jnp.zeros_like(dwa_ref)
            dwx_ref[...] = jnp.zeros_like(dwx_ref)
            small_ref[...] = jnp.zeros_like(small_ref)

        first = tile == 0
        dm = dm_ref[...]
        hs_t = hs_ref[...]
        xc_t = xc_ref[...]
        r = r_ref[...]
        i = i_ref[...]
        lam_v = lam_ref[...]
        sp = _softplus(-lam_v)
        a = a_ref[...]
        mult = mult_ref[...]

        sg = _sigmoid(gl_ref[...])
        ge, dge = _gelu_and_grad(vg_ref[...])
        dlo = dm * sg
        dgl_ref[...] = (dm * (hs_t * ge) * sg * (1.0 - sg)).astype(BF16)
        dvg_ref[...] = (dlo * hs_t * dge).astype(BF16)
        dhs = dlo * ge

        q_s[...] = a * dhs
        mu_first = _scan_chunks(a_ref, q_s, mu_s, mu_next_ref[0:1, :], tm, reverse=True)
        lam_t = dhs + _shift_up(mu_s[...], mu_next_ref[...], 1)
        mu_next_ref[...] = jnp.broadcast_to(mu_first, mu_next_ref.shape)

        h_prev = _shift_down(jnp.where(first, 0.0, hsp_ref[...]), hs_t, 1)
        da = lam_t * h_prev
        dmult = lam_t * (i * xc_t)
        di = lam_t * mult * xc_t
        dxc = lam_t * mult * i
        dlog_a = da * a - dmult * (a * a) / mult
        dr = dlog_a * (-LRU_C * sp)
        dlam_rows = dlog_a * (-LRU_C * r)
        dza = dr * r * (1.0 - r)
        dzx = di * i * (1.0 - i)
        dzab, dzxb = dza.astype(BF16), dzx.astype(BF16)
        xcb = xc_t.astype(BF16)
        dxc = dxc + lax.dot_general(dzab, wa_ref[...], _NT, preferred_element_type=F32)
        dxc = dxc + lax.dot_general(dzxb, wx_ref[...], _NT, preferred_element_type=F32)
        dwa_ref[...] += lax.dot_general(xcb, dzab, _TN, preferred_element_type=F32)
        dwx_ref[...] += lax.dot_general(xcb, dzxb, _TN, preferred_element_type=F32)

        dxc_next = dxc_next_ref[...]
        taps = [_shift_up(dxc, dxc_next, CONV_WIDTH - 1 - k) for k in range(CONV_WIDTH)]
        dv = jnp.zeros_like(dxc)
        for k in range(CONV_WIDTH):
            dv = dv + cw_ref[k:k + 1, :] * taps[k]
        dvl_ref[...] = dv.astype(BF16)
        dxc_next_ref[...] = dxc[:HALO, :]

        v_t = vl_ref[...]
        small = [jnp.sum(dza, axis=0, keepdims=True), jnp.sum(dzx, axis=0, keepdims=True),
                 jnp.sum(dlam_rows, axis=0, keepdims=True) * (-_sigmoid(-lam_v)),
                 jnp.sum(dxc, axis=0, keepdims=True)]
        for k in range(CONV_WIDTH):
            small.append(jnp.sum(taps[k] * v_t, axis=0, keepdims=True))
        for k, row in enumerate(small):
            small_ref[k:k + 1, :] += row
        for cp in out_copies(slot):
            cp.start()

        @pl.when(step == H * nt - 1)
        def _():
            for cp in out_copies(slot) + (out_copies(1 - slot) if H * nt > 1 else []):
                cp.wait()

    def piece(p):
        return pl.BlockSpec((tm, hd), lambda h, t: (nt - 1 - t, p * nb + h))

    def halo(p):
        return pl.BlockSpec((HALO, hd), lambda h, t: (jnp.maximum((nt - 1 - t) * halo_blocks - 1, 0), p * nb + h))

    blk = pl.BlockSpec((tm, hd), lambda h, t: (nt - 1 - t, h))
    vec = pl.BlockSpec((1, hd), lambda h, t: (0, h))
    mat = pl.BlockSpec((None, hd, hd), lambda h, t: (h, 0, 0))
    return _hosted(
        body, tasks, name=name, grid=(H, nt),
        in_specs=[blk, piece(1), piece(2), piece(4), blk, blk, blk, blk, blk, blk, halo(0), vec,
                  pl.BlockSpec((CONV_WIDTH, hd), lambda h, t: (0, h)), mat, mat, pl.BlockSpec(memory_space=pl.ANY)],
        out_specs=[pl.BlockSpec(memory_space=pl.ANY), mat, mat,
                   pl.BlockSpec((None, LRU_SMALL_ROWS, hd), lambda h, t: (h, 0, 0))],
        out_shape=[_sds(dproj.shape, BF16)] + [_sds((H, hd, hd), F32)] * 2 + [_sds((H, LRU_SMALL_ROWS, hd), F32)],
        scratch_shapes=[pltpu.VMEM((HALO, hd), F32), pltpu.VMEM((HALO, hd), F32),
                        pltpu.VMEM((tm, hd), F32), pltpu.VMEM((tm, hd), F32),
                        pltpu.VMEM((2, 3, tm, hd), BF16), pltpu.SemaphoreType.DMA((6,))],
        semantics=("arbitrary", "arbitrary"), aliases={15: 0},
        operands=(dmerged, proj, proj, proj, xc, r_gate, i_gate, a_gate, mult_gate, hs, hs, lam, conv_w, wa, wx,
                  dproj))


def _adamw(w, g, m, v):
    m = ADAM_B1 * m + (1.0 - ADAM_B1) * g
    v = ADAM_B2 * v + (1.0 - ADAM_B2) * (g * g)
    m_hat = m / (1.0 - ADAM_B1 ** ADAM_STEP)
    v_hat = v / (1.0 - ADAM_B2 ** ADAM_STEP)
    delta = -ADAM_LR * (m_hat / (jnp.sqrt(v_hat) + ADAM_EPS) + ADAM_WD * w)
    return delta, m, v


def _reduce_update(pair_sums, chip_sums, w, m, v, chip_slot, name, part=(0, 1), earlier=None):
    R, C = pair_sums.shape[1:]
    p, parts = part
    tr = _tile(R, TILE["update"])
    nblk = R // tr

    def body(slot_ref, own_ref, got_ref, w_ref, m_ref, v_ref, *rest):
        g_out, d_out, m_out, v_out = rest[-4:]
        g = own_ref[...].astype(F32)
        for k in range(3):
            g = g + got_ref[k].astype(F32)
        d, m_new, v_new = _adamw(w_ref[...], g, m_ref[...], v_ref[...])
        g_out[...] = g
        d_out[...] = d
        m_out[...] = m_new
        v_out[...] = v_new

    blk = pl.BlockSpec((tr, C), lambda i, s: (p * nblk + i, 0))
    in_specs = [pl.BlockSpec((None, tr, C), lambda i, s: (s[0], i, 0)),
                pl.BlockSpec((3, tr, C), lambda i, s: (0, i, 0)), blk, blk, blk]
    operands = (chip_slot, pair_sums, chip_sums, w, m, v)
    aliases = {}
    if earlier is not None:
        in_specs += [pl.BlockSpec(memory_space=pl.ANY)] * 4
        operands += tuple(earlier)
        aliases = {6 + k: k for k in range(4)}
    return _call(
        body, name=name,
        grid_spec=pltpu.PrefetchScalarGridSpec(
            num_scalar_prefetch=1, grid=(R // tr,), in_specs=in_specs, out_specs=[blk] * 4),
        out_shape=[_sds((parts * R, C), F32)] * 4,
        input_output_aliases=aliases,
        compiler_params=_cp("parallel"),
    )(*operands)


def _small_updates(small_sum, bias_grads, col0, specs, name):
    n = len(specs)

    def body(col_ref, ss_ref, bg_ref, *refs):
        ins, outs = refs[:3 * n], refs[3 * n:]
        for k, (w, _, _, kind, r0) in enumerate(specs):
            nr, nc = w.shape
            if kind == "rows":
                g = ss_ref[r0:r0 + nr, :]
            elif kind == "cols":
                g = ss_ref[r0:r0 + nr, pl.ds(pl.multiple_of(col_ref[0], nc), nc)]
            else:
                g = bg_ref[r0:r0 + nr, :]
            d, m_new, v_new = _adamw(ins[3 * k][...], g, ins[3 * k + 1][...], ins[3 * k + 2][...])
            for o, val in zip(outs[4 * k:4 * k + 4], (g, d, m_new, v_new)):
                o[...] = val

    def whole(a):
        return pl.BlockSpec(a.shape, lambda i, c: (0,) * a.ndim)

    arrays = [a for (w, m, v, _, _) in specs for a in (w, m, v)]
    res = _call(
        body, name=name,
        grid_spec=pltpu.PrefetchScalarGridSpec(
            num_scalar_prefetch=1, grid=(1,),
            in_specs=[whole(small_sum), whole(bias_grads)] + [whole(a) for a in arrays],
            out_specs=[whole(w) for (w, _, _, _, _) in specs for _ in range(4)]),
        out_shape=[_sds(w.shape, F32) for (w, _, _, _, _) in specs for _ in range(4)],
        compiler_params=_cp("arbitrary"),
    )(col0, small_sum, bias_grads, *arrays)
    return [tuple(res[4 * k:4 * k + 4]) for k in range(n)]


def _slots_from_rows(full, lead):
    L, R, C = full.shape
    r = R // N_DEV
    return full.reshape(L, N_DEV, r, C).transpose(1, 0, 2, 3).reshape(N_DEV, L * r, C)


def _rows_from_slots(slots, lead):
    _, LR, C = slots.shape
    r = LR // lead
    return slots.reshape(N_DEV, lead, r, C).transpose(1, 0, 2, 3).reshape(lead, N_DEV * r, C)


def kernel(x, meta_tokens, norm1_g, w_in, pool_w, pool_scale, conv_w, conv_b, gate_a_w, gate_a_b, gate_x_w, gate_x_b, lru_lambda, w_out, norm2_g, mlp_w1, mlp_w2, final_g, loss_target, m_meta_tokens, m_norm1_g, m_w_in, m_pool_w, m_pool_scale, m_conv_w, m_conv_b, m_gate_a_w, m_gate_a_b, m_gate_x_w, m_gate_x_b, m_lru_lambda, m_w_out, m_norm2_g, m_mlp_w1, m_mlp_w2, m_final_g, v_meta_tokens, v_norm1_g, v_w_in, v_pool_w, v_pool_scale, v_conv_w, v_conv_b, v_gate_a_w, v_gate_a_b, v_gate_x_w, v_gate_x_b, v_lru_lambda, v_w_out, v_norm2_g, v_mlp_w1, v_mlp_w2, v_final_g):
    seq, D = x.shape[1], x.shape[2]
    n_meta = meta_tokens.shape[0]
    G, Cg = pool_w.shape[1], pool_w.shape[3]
    H, hd = gate_a_w.shape[1], gate_a_w.shape[3]
    T = n_meta + seq
    Tp = -(-T // ROW_ALIGN) * ROW_ALIGN
    ix, iy, ic = _pos()
    me = 4 * ix + 2 * iy + ic
    core = jnp.reshape(ic, (1,)).astype(jnp.int32)
    chip_slot = jnp.reshape(2 * ix + iy, (1,)).astype(jnp.int32)

    w_in_l, w1_l, w2_l, w_out_l = w_in[0], mlp_w1[0], mlp_w2[0], w_out[0]
    pool_l = pool_w[0].reshape(G * (Cg // N_DEV), Cg)
    wa_l = gate_a_w[0].reshape(H * (hd // N_DEV), hd)
    wx_l = gate_x_w[0].reshape(H * (hd // N_DEV), hd)
    small_params = jnp.concatenate(
        [meta_tokens, conv_w[0], jnp.zeros((4, D // N_DEV), F32)], axis=0)
    biases = jnp.concatenate([gate_a_b[0], gate_x_b[0]], axis=0)
    (w_in_g, small_g, bias_g) = _all_gather_relay(
        [w_in_l.astype(BF16), small_params, biases], [W_IN_CHUNKS, 1, 1], "gather_first")
    small_full = small_g.transpose(1, 0, 2).reshape(n_meta + 8, D)
    meta_full = small_full[:n_meta]
    conv_full = small_full[n_meta:n_meta + CONV_WIDTH]
    bias_full = bias_g.transpose(1, 0, 2).reshape(2 * H, hd)
    ba_full, bx_full = bias_full[:H], bias_full[H:]

    h0 = jnp.concatenate([meta_full, x[0], jnp.zeros((Tp - T, D), F32)], axis=0)
    u, r1 = _norm_fwd(h0, norm1_g, "norm1")
    proj, ((pool_g,), (wa_g,), (wx_g,), (w_out_g,), (w1_ici,)) = _proj_fwd(
        u, w_in_g, "proj",
        tasks=[_AgFull(pool_l.astype(BF16)), _AgFull(wa_l.astype(BF16)), _AgFull(wx_l.astype(BF16)),
               _AgFull(w_out_l.astype(BF16)), _AgFull(w1_l.astype(BF16), forward_here=False)])
    pool_full = _rows_from_slots(pool_g, G)
    wa_full = _rows_from_slots(wa_g, H)
    wx_full = _rows_from_slots(wx_g, H)
    w_in_parts = [w_in_g]
    d_pool, y_pool = _pool_fwd(proj, pool_full, "pool_fwd")
    (xc, r_gate, i_gate, a_gate, mult_gate, hs, merged), ((w1_g,), (w2_ici,)) = _lru_fwd(
        proj, y_pool, pool_scale, conv_full, conv_b, wa_full, ba_full, wx_full, bx_full, lru_lambda, "lru_fwd",
        tasks=[_AgForward(w1_ici), _AgFull(w2_l.astype(BF16), forward_here=False)])
    w_out_full = w_out_g.reshape(D, D)
    (h1, u2, r2), ((w2_g,),) = _wout_norm_fwd(merged, w_out_full, h0, norm2_g, "wout_norm2", tasks=[_AgForward(w2_ici)])
    (act, a1), _ = _mlp1_fwd(u2, w1_g, "mlp1")
    dh2, dh2b, loss_tile, d_final_g = _mlp2_loss(
        act, w2_g.reshape(-1, D), h1, loss_target[0], final_g.reshape(1, D), n_meta, seq, "mlp2_loss")

    def pair(part, got, tag):
        return _pair_sum(part, got, core, "pair_sum_" + tag)

    d_a1 = _dact_bwd(dh2b, w2_g, a1, "dact")
    dw2_p, _ = _weight_grad(act, dh2b, N_DEV, True, "dw2")
    dw1_p, ((dw2_got,),) = _weight_grad(u2, d_a1, N_DEV, False, "dw1", tasks=[_RsSibling(dw2_p)])
    dw2_pair = pair(dw2_p, dw2_got, "w2")
    (dh1, dh1b, d_norm2_g), ((dw2_chips,), (dw1_got,)) = _nt_norm_bwd(
        d_a1, [w1_g], dh2, h1, r2, norm2_g, True, "du2_norm2", tasks=[_RsChips(dw2_pair), _RsSibling(dw1_p)])
    dw1_pair = pair(dw1_p, dw1_got, "w1")
    dmerged = _dmerged_bwd(dh1b, w_out_full, "dmerged")
    dwout_p, _ = _weight_grad(merged, dh1b, 2, True, "dwout")
    dwout_p = dwout_p.reshape(N_DEV, D // N_DEV, D)
    (dproj_pool, dpool_full, d_scale), ((dwout_got,),) = _pool_bwd(
        dmerged, proj, y_pool, d_pool, pool_scale, pool_full, "pool_bwd", tasks=[_RsSibling(dwout_p)])
    dwout_pair = pair(dwout_p, dwout_got, "wout")
    (dproj, dwa_full, dwx_full, lru_small), ((dw1_chips,), (dwout_chips,)) = _lru_bwd(
        dmerged, proj, xc, r_gate, i_gate, a_gate, mult_gate, hs, lru_lambda, conv_full, wa_full, wx_full, dproj_pool,
        "lru_bwd", tasks=[_RsChips(dw1_pair), _RsChips(dwout_pair)])
    dwin_a, _ = _weight_grad(u, dproj, N_DEV, False, "dwin_a", part=(0, 2))
    dwin_b, ((dwin_a_got,),) = _weight_grad(u, dproj, N_DEV, False, "dwin_b", part=(1, 2), tasks=[_RsSibling(dwin_a)])
    dwin_a_pair = pair(dwin_a, dwin_a_got, "win_a")
    dpool_p = _slots_from_rows(dpool_full, G).astype(BF16)
    dwa_p = _slots_from_rows(dwa_full, H).astype(BF16)
    dwx_p = _slots_from_rows(dwx_full, H).astype(BF16)
    late = [dwin_b, dpool_p, dwa_p, dwx_p]
    n_tiles = Tp // _tile(Tp, TILE["nt"])
    n_first = max(1, n_tiles // 2)
    (dh0_a, d_norm1_a), ((dwin_a_chips,), *late_got) = _nt_norm_bwd(
        dproj, w_in_parts, dh1, h0, r1, norm1_g, False, "du_norm1_a",
        tasks=[_RsChips(dwin_a_pair)] + [_RsSibling(p) for p in late], tiles=(0, n_first))
    late_pair = [pair(p, g[0], "late%d" % k) for k, (p, g) in enumerate(zip(late, late_got))]
    (dh0, d_norm1_g), late_chips = _nt_norm_bwd(
        dproj, w_in_parts, dh1, h0, r1, norm1_g, False, "du_norm1_b", tasks=[_RsChips(p) for p in late_pair],
        tiles=(n_first, n_tiles - n_first), earlier=(dh0_a, d_norm1_a))
    grad_x = dh0[n_meta:T][None]

    pair_sums = [dw2_pair, dw1_pair, dwout_pair] + late_pair[1:]
    chip_sums = [dw2_chips, dw1_chips, dwout_chips] + [c[0] for c in late_chips[1:]]
    big = {}
    names = ["mlp_w2", "mlp_w1", "w_out", "pool_w", "gate_a_w", "gate_x_w"]
    trip = {"mlp_w2": (mlp_w2, m_mlp_w2, v_mlp_w2), "mlp_w1": (mlp_w1, m_mlp_w1, v_mlp_w1),
            "w_out": (w_out, m_w_out, v_w_out),
            "pool_w": (pool_w, m_pool_w, v_pool_w), "gate_a_w": (gate_a_w, m_gate_a_w, v_gate_a_w),
            "gate_x_w": (gate_x_w, m_gate_x_w, v_gate_x_w)}
    for k, nm in enumerate(names):
        w_, m_, v_ = trip[nm]
        shape2 = pair_sums[k].shape[1:]
        outs = _reduce_update(pair_sums[k], chip_sums[k], w_.reshape(shape2), m_.reshape(shape2), v_.reshape(shape2),
                              chip_slot, "update_" + nm)
        big[nm] = [o.reshape(w_.shape) for o in outs]
    win2 = [a_[0] for a_ in (w_in, m_w_in, v_w_in)]
    win_a = _reduce_update(dwin_a_pair, dwin_a_chips, *win2, chip_slot, "update_w_in_a", part=(0, 2))
    win_b = _reduce_update(late_pair[0], late_chips[0][0], *win2, chip_slot, "update_w_in_b", part=(1, 2), earlier=win_a)
    big["w_in"] = [o.reshape(w_in.shape) for o in win_b]
    names = names + ["w_in"]

    lru_rows = lru_small.transpose(1, 0, 2).reshape(LRU_SMALL_ROWS, D)
    small_part = jnp.concatenate(
        [dh0[:n_meta], d_norm1_g, d_scale, d_norm2_g, d_final_g, lru_rows, jnp.zeros((4, D), F32)], axis=0)
    (small_all,) = _all_gather([small_part], "gather_small_grads")
    small_sum = _sum_slots(small_all, "sum_small_grads")
    o = n_meta
    dcol, hcol = D // N_DEV, hd // N_DEV
    bias_grads = jnp.concatenate(
        [lax.dynamic_slice_in_dim(small_sum[o + 4 + k].reshape(H, hd), me * hcol, hcol, axis=1) for k in range(2)], axis=0)
    col0 = jnp.reshape(me * dcol, (1,)).astype(jnp.int32)
    small = {"meta_tokens": (meta_tokens, m_meta_tokens, v_meta_tokens, "cols", 0),
             "norm1_g": (norm1_g, m_norm1_g, v_norm1_g, "rows", o),
             "pool_scale": (pool_scale, m_pool_scale, v_pool_scale, "rows", o + 1),
             "norm2_g": (norm2_g, m_norm2_g, v_norm2_g, "rows", o + 2),
             "final_g": (final_g, m_final_g, v_final_g, "rows", o + 3),
             "gate_a_b": (gate_a_b, m_gate_a_b, v_gate_a_b, "bias", 0),
             "gate_x_b": (gate_x_b, m_gate_x_b, v_gate_x_b, "bias", H),
             "lru_lambda": (lru_lambda, m_lru_lambda, v_lru_lambda, "rows", o + 6),
             "conv_b": (conv_b, m_conv_b, v_conv_b, "rows", o + 7),
             "conv_w": (conv_w, m_conv_w, v_conv_w, "cols", o + 8)}

    def two_d(a):
        return a.reshape(-1, a.shape[-1])

    small_out = _small_updates(
        small_sum, bias_grads, col0,
        [(two_d(w_), two_d(m_), two_d(v_), kind, row) for (w_, m_, v_, kind, row) in small.values()], "update_small")
    small_res = {nm: [r.reshape(spec[0].shape) for r in res] for (nm, spec), res in zip(small.items(), small_out)}

    def leaves(kind):
        out = {nm: res[kind] for nm, res in small_res.items()}
        for nm in names:
            out[nm] = big[nm][kind]
        order = ["meta_tokens", "norm1_g", "w_in", "pool_w", "pool_scale", "conv_w", "conv_b", "gate_a_w", "gate_a_b",
                 "gate_x_w", "gate_x_b", "lru_lambda", "w_out", "norm2_g", "mlp_w1", "mlp_w2", "final_g"]
        return [out[nm] for nm in order]

    loss = lax.psum(loss_tile[0, 0], ("x", "y", "c"))
    return (loss, grad_x, *leaves(0), *leaves(1), *leaves(2), *leaves(3))
```

```python
import jax
import jax.numpy as jnp
from jax import lax
from jax.experimental import pallas as pl
from jax.experimental.pallas import tpu as pltpu

F32 = jnp.float32
BF16 = jnp.bfloat16
MESH = pl.DeviceIdType.MESH
N_DEV = 8
POOL_WINDOWS = (2, 4, 8, 16)
MAX_WINDOW = 16
CONV_WIDTH = 4
HALO = 8
LRU_C = 8.0
NORM_EPS = 1e-6
ADAM_LR, ADAM_B1, ADAM_B2, ADAM_EPS, ADAM_WD, ADAM_STEP = 0.001, 0.9, 0.999, 1e-08, 0.01, 10
ROW_ALIGN = 128
VMEM_LIMIT = 56 << 20
TILE = dict(norm=384, proj=1408, pool=384, lru=704, wout=384, mlp1=1408, mlp2=704, dact=1408, tn=2112,
            nt=704, dmerged=704, update=256, pair=1024)
MLP2_K = 1024
EPILOGUE_ROWS = 176
SCAN_GROUP = 4
W_IN_CHUNKS = 5
RELAY_STEP_PERCENT = 45
MID_STEP_PERCENT = 88

_NT = (((1,), (1,)), ((), ()))
_TN = (((0,), (0,)), ((), ()))


def _call(body, **kw):
    return pl.pallas_call(body, **kw)


def _cp(*sem):
    return pltpu.CompilerParams(dimension_semantics=sem, vmem_limit_bytes=VMEM_LIMIT)


def _tile(total, pref):
    best = None
    for t in range(16, min(total, pref) + 1, 16):
        if total % t == 0:
            best = t
    assert best is not None, (total, pref)
    return best


def _sds(shape, dtype):
    return jax.ShapeDtypeStruct(shape, dtype)


def _pos():
    return lax.axis_index("x"), lax.axis_index("y"), lax.axis_index("c")


def _all_gather(shards, name):
    n = len(shards)

    def body(*refs):
        ins, outs = refs[:n], refs[n:2 * n]
        send_sems, recv_sems, local_sems = refs[2 * n:]
        x, y, c = _pos()
        me, sib = (x, y, c), (x, y, 1 - c)
        chips = [(1 - x, y), (x, 1 - y), (1 - x, 1 - y)]

        def slot(p):
            return 4 * p[0] + 2 * p[1] + p[2]

        def copy(a, k, block, to, src=None):
            dst = outs[a].at[slot(block)]
            return pltpu.make_async_remote_copy(
                src_ref=dst if src is None else src, dst_ref=dst,
                send_sem=send_sems.at[7 * a + k], recv_sem=recv_sems.at[7 * a + k],
                device_id=to, device_id_type=MESH)

        mine = [pltpu.make_async_copy(ins[a], outs[a].at[slot(me)], local_sems.at[a]) for a in range(n)]
        for m in mine:
            m.start()
        first = []
        for a in range(n):
            first.append(copy(a, 0, me, sib, src=ins[a]))
            first += [copy(a, 1 + j, me, (*chip, c), src=ins[a]) for j, chip in enumerate(chips)]
        for cp in first:
            cp.start()
        passed = []
        for a in range(n):
            for j, chip in enumerate(chips):
                copy(a, 1 + j, (*chip, c), me).wait_recv()
                fwd = copy(a, 4 + j, (*chip, c), sib)
                fwd.start()
                passed.append(fwd)
        for a in range(n):
            copy(a, 0, sib, me).wait_recv()
            for j, chip in enumerate(chips):
                copy(a, 4 + j, (*chip, 1 - c), me).wait_recv()
        for cp in first + passed:
            cp.wait_send()
        for m in mine:
            m.wait()

    hbm = pl.BlockSpec(memory_space=pl.ANY)
    return _call(
        body, name=name,
        out_shape=[_sds((N_DEV,) + s.shape, s.dtype) for s in shards],
        in_specs=[hbm] * n, out_specs=[hbm] * n,
        scratch_shapes=[pltpu.SemaphoreType.DMA((7 * n,)), pltpu.SemaphoreType.DMA((7 * n,)),
                        pltpu.SemaphoreType.DMA((n,))],
    )(*shards)


def _all_gather_relay(shards, chunks, name):
    n = len(shards)
    units = []
    for a, s in enumerate(shards):
        if chunks[a] == 1:
            units.append((a, None, None))
        else:
            w = s.shape[-1] // chunks[a]
            units += [(a, q * w, w) for q in range(chunks[a])]
    nu = len(units)

    def body(*refs):
        ins, outs = refs[:n], refs[n:2 * n]
        send_sems, recv_sems, local_sems = refs[2 * n:]
        x, y, c = _pos()
        me, sib = (x, y, c), (x, y, 1 - c)
        x_nbr, y_nbr, diag = (1 - x, y, c), (x, 1 - y, c), (1 - x, 1 - y, c)
        came_from = (x + (1 - c) * (1 - 2 * x), y + c * (1 - 2 * y), c)
        pass_to = (x + c * (1 - 2 * x), y + (1 - c) * (1 - 2 * y), c)

        def slot(p):
            return 4 * p[0] + 2 * p[1] + p[2]

        def src_view(u):
            a, c0, w = units[u]
            return ins[a] if c0 is None else ins[a].at[:, pl.ds(c0, w)]

        def dst_view(u, p):
            a, c0, w = units[u]
            return outs[a].at[slot(p)] if c0 is None else outs[a].at[slot(p), :, pl.ds(c0, w)]

        def copy(u, k, block, to, from_shard=False):
            dst = dst_view(u, block)
            return pltpu.make_async_remote_copy(
                src_ref=src_view(u) if from_shard else dst, dst_ref=dst,
                send_sem=send_sems.at[7 * u + k], recv_sem=recv_sems.at[7 * u + k],
                device_id=to, device_id_type=MESH)

        mine = [pltpu.make_async_copy(src_view(u), dst_view(u, me), local_sems.at[u]) for u in range(nu)]
        for m in mine:
            m.start()
        sent = []
        for u in range(nu):
            sent += [copy(u, 0, me, sib, True), copy(u, 1, me, x_nbr, True), copy(u, 2, me, y_nbr, True)]
        for cp in sent:
            cp.start()
        for u in range(nu):
            copy(u, 1, x_nbr, me).wait_recv()
            copy(u, 2, y_nbr, me).wait_recv()
            sent += [copy(u, 3, came_from, pass_to), copy(u, 4, x_nbr, sib), copy(u, 5, y_nbr, sib)]
            for cp in sent[-3:]:
                cp.start()
        for u in range(nu):
            copy(u, 3, diag, me).wait_recv()
            sent.append(copy(u, 6, diag, sib))
            sent[-1].start()
        for u in range(nu):
            copy(u, 0, sib, me).wait_recv()
            for k, p in ((4, x_nbr), (5, y_nbr), (6, diag)):
                copy(u, k, (p[0], p[1], 1 - c), me).wait_recv()
        for cp in sent:
            cp.wait_send()
        for m in mine:
            m.wait()

    hbm = pl.BlockSpec(memory_space=pl.ANY)
    return _call(
        body, name=name,
        out_shape=[_sds((N_DEV,) + s.shape, s.dtype) for s in shards],
        in_specs=[hbm] * n, out_specs=[hbm] * n,
        scratch_shapes=[pltpu.SemaphoreType.DMA((7 * nu,)), pltpu.SemaphoreType.DMA((7 * nu,)),
                        pltpu.SemaphoreType.DMA((nu,))],
    )(*shards)


def _other_chips(x, y):
    return [(1 - x, y), (x, 1 - y), (1 - x, 1 - y)]


class _AgFull:
    n_sem, n_local = 7, 1

    def __init__(self, shard, forward_here=True):
        self.forward_here = forward_here
        self.ins = [shard]
        self.out_shapes = [_sds((N_DEV,) + shard.shape, shard.dtype)]
        self.aliases = []

    def _places(self):
        x, y, c = _pos()
        came_from = (x + (1 - c) * (1 - 2 * x), y + c * (1 - 2 * y), c)
        pass_to = (x + c * (1 - 2 * x), y + (1 - c) * (1 - 2 * y), c)
        return (x, y, 1 - c), (1 - x, y, c), (x, 1 - y, c), (1 - x, 1 - y, c), came_from, pass_to, c

    def _copy(self, outs, sems, k, block, to, src=None):
        send, recv, _, base, _ = sems
        dst = outs[0].at[4 * block[0] + 2 * block[1] + block[2]]
        return pltpu.make_async_remote_copy(src_ref=dst if src is None else src, dst_ref=dst, send_sem=send.at[base + k],
                                            recv_sem=recv.at[base + k], device_id=to, device_id_type=MESH)

    def _sends(self, ins, outs, sems):
        sib, x_nbr, y_nbr = self._places()[:3]
        x, y, c = _pos()
        return [self._copy(outs, sems, k, (x, y, c), p, src=ins[0]) for k, p in enumerate((sib, x_nbr, y_nbr))]

    def _own(self, ins, outs, sems):
        x, y, c = _pos()
        return pltpu.make_async_copy(ins[0], outs[0].at[4 * x + 2 * y + c], sems[2].at[sems[4]])

    def _forwards(self, outs, sems, core_of_block):
        return _forward_copies(outs[0], sems[0], sems[1], sems[3] + 4, core_of_block)

    def start(self, ins, outs, sems):
        self._own(ins, outs, sems).start()
        for cp in self._sends(ins, outs, sems):
            cp.start()

    def relay(self, ins, outs, sems):
        _, _, _, _, came_from, pass_to, c = self._places()
        self._copy(outs, sems, 1 + c, came_from, came_from).wait_recv()
        self._copy(outs, sems, 3, came_from, pass_to).start()

    def _late_arrivals(self, outs, sems):
        _, _, _, diag, came_from, pass_to, c = self._places()
        return [self._copy(outs, sems, 2 - c, pass_to, pass_to), self._copy(outs, sems, 3, diag, diag)]

    def mid(self, ins, outs, sems):
        if self.forward_here:
            for cp in self._late_arrivals(outs, sems):
                cp.wait_recv()
            for cp in self._forwards(outs, sems, "mine"):
                cp.start()

    def finish(self, ins, outs, sems):
        sib, _, _, _, came_from, pass_to, c = self._places()
        self._copy(outs, sems, 0, sib, sib).wait_recv()
        if self.forward_here:
            for cp in self._forwards(outs, sems, "sibling"):
                cp.wait_recv()
            for cp in self._forwards(outs, sems, "mine"):
                cp.wait_send()
        else:
            for cp in self._late_arrivals(outs, sems):
                cp.wait_recv()
        for cp in self._sends(ins, outs, sems) + [self._copy(outs, sems, 3, came_from, pass_to)]:
            cp.wait_send()
        self._own(ins, outs, sems).wait()


def _forward_copies(gathered_ref, send, recv, base, core_of_block):
    x, y, c = _pos()
    res = []
    for k, chip in enumerate(_other_chips(x, y)):
        blk = gathered_ref.at[4 * chip[0] + 2 * chip[1] + (c if core_of_block == "mine" else 1 - c)]
        res.append(pltpu.make_async_remote_copy(src_ref=blk, dst_ref=blk, send_sem=send.at[base + k],
                                                recv_sem=recv.at[base + k], device_id=(x, y, 1 - c),
                                                device_id_type=MESH))
    return res


class _AgForward:
    n_sem, n_local = 3, 0

    def __init__(self, gathered):
        self.ins = [gathered]
        self.out_shapes = [_sds(gathered.shape, gathered.dtype)]
        self.aliases = [(0, 0)]

    def start(self, ins, outs, sems):
        for cp in _forward_copies(outs[0], sems[0], sems[1], sems[3], "mine"):
            cp.start()

    def finish(self, ins, outs, sems):
        for cp in _forward_copies(outs[0], sems[0], sems[1], sems[3], "sibling"):
            cp.wait_recv()
        for cp in _forward_copies(outs[0], sems[0], sems[1], sems[3], "mine"):
            cp.wait_send()


class _RsSibling:
    n_sem, n_local = 4, 0

    def __init__(self, part):
        self.ins = [part]
        self.out_shapes = [_sds((4,) + part.shape[1:], part.dtype)]
        self.aliases = []

    def _copies(self, ins, outs, sems):
        send, recv, _, base, _ = sems
        x, y, c = _pos()
        return [pltpu.make_async_remote_copy(src_ref=ins[0].at[2 * q + (1 - c)], dst_ref=outs[0].at[q],
                                             send_sem=send.at[base + q], recv_sem=recv.at[base + q],
                                             device_id=(x, y, 1 - c), device_id_type=MESH) for q in range(4)]

    def start(self, ins, outs, sems):
        for cp in self._copies(ins, outs, sems):
            cp.start()

    def finish(self, ins, outs, sems):
        for cp in self._copies(ins, outs, sems):
            cp.wait()


class _RsChips:
    n_sem, n_local = 3, 0

    def __init__(self, pair):
        self.ins = [pair]
        self.out_shapes = [_sds((3,) + pair.shape[1:], pair.dtype)]
        self.aliases = []

    def _copies(self, ins, outs, sems):
        send, recv, _, base, _ = sems
        x, y, c = _pos()
        return [pltpu.make_async_remote_copy(src_ref=ins[0].at[2 * chip[0] + chip[1]], dst_ref=outs[0].at[k],
                                             send_sem=send.at[base + k], recv_sem=recv.at[base + k],
                                             device_id=(*chip, c), device_id_type=MESH)
                for k, chip in enumerate(_other_chips(x, y))]

    def start(self, ins, outs, sems):
        for cp in self._copies(ins, outs, sems):
            cp.start()

    def finish(self, ins, outs, sems):
        for cp in self._copies(ins, outs, sems):
            cp.wait()


def _hosted(body, tasks, *, grid, in_specs, out_specs, out_shape, scratch_shapes=(), name, semantics, operands,
            aliases=None):
    in_specs, out_specs, out_shape = list(in_specs), list(out_specs), list(out_shape)
    scratch_shapes = list(scratch_shapes)
    aliases = dict(aliases or {})
    if not tasks:
        res = _call(body, name=name, grid=grid, in_specs=in_specs, out_specs=out_specs, out_shape=out_shape,
                    scratch_shapes=scratch_shapes, input_output_aliases=aliases,
                    compiler_params=_cp(*semantics))(*operands)
        return list(res), []
    n_in, n_out, n_scr = len(in_specs), len(out_specs), len(scratch_shapes)
    t_ins = [a for t in tasks for a in t.ins]
    t_outs = [o for t in tasks for o in t.out_shapes]
    i0, o0 = n_in, n_out
    for t in tasks:
        for (i, o) in t.aliases:
            aliases[i0 + i] = o0 + o
        i0 += len(t.ins)
        o0 += len(t.out_shapes)
    n_sem = sum(t.n_sem for t in tasks)
    n_local = max(1, sum(t.n_local for t in tasks))
    n_steps = 1
    for g in grid:
        n_steps *= g
    mid_step = min(n_steps - 1, (n_steps * MID_STEP_PERCENT) // 100)
    relay_step = min(mid_step, (n_steps * RELAY_STEP_PERCENT) // 100)

    def wrapped(*refs):
        cut = [n_in, len(t_ins), n_out, len(t_outs), n_scr]
        parts, at = [], 0
        for n in cut:
            parts.append(refs[at:at + n])
            at += n
        ins, tin, outs, tout, scratch = parts
        send, recv, local = refs[at:]
        step = pl.program_id(0)
        for d in range(1, len(grid)):
            step = step * grid[d] + pl.program_id(d)

        def each(method):
            i, o, s, l = 0, 0, 0, 0
            for t in tasks:
                if hasattr(t, method):
                    getattr(t, method)(tin[i:i + len(t.ins)], tout[o:o + len(t.out_shapes)], (send, recv, local, s, l))
                i, o, s, l = i + len(t.ins), o + len(t.out_shapes), s + t.n_sem, l + t.n_local

        @pl.when(step == 0)
        def _():
            each("start")

        body(*ins, *outs, *scratch)

        @pl.when(step == relay_step)
        def _():
            each("relay")

        @pl.when(step == mid_step)
        def _():
            each("mid")

        @pl.when(step == n_steps - 1)
        def _():
            each("finish")

    hbm = pl.BlockSpec(memory_space=pl.ANY)
    res = _call(
        wrapped, name=name, grid=grid,
        in_specs=in_specs + [hbm] * len(t_ins), out_specs=out_specs + [hbm] * len(t_outs),
        out_shape=out_shape + t_outs,
        scratch_shapes=scratch_shapes + [pltpu.SemaphoreType.DMA((n_sem,)), pltpu.SemaphoreType.DMA((n_sem,)),
                                         pltpu.SemaphoreType.DMA((n_local,))],
        input_output_aliases=aliases,
        compiler_params=_cp(*(["arbitrary"] * len(grid))),
    )(*operands, *t_ins)
    res = list(res)
    task_outs, o = [], n_out
    for t in tasks:
        task_outs.append(res[o:o + len(t.out_shapes)])
        o += len(t.out_shapes)
    return res[:n_out], task_outs


def _pair_sum(part, got, core, name):
    _, R, C = part.shape
    tr = _tile(R, TILE["pair"]) if R % 16 == 0 else R

    def body(core_ref, p_ref, g_ref, o_ref):
        o_ref[...] = (p_ref[...].astype(F32) + g_ref[...].astype(F32)).astype(o_ref.dtype)

    return _call(
        body, name=name,
        grid_spec=pltpu.PrefetchScalarGridSpec(
            num_scalar_prefetch=1, grid=(4, R // tr),
            in_specs=[pl.BlockSpec((None, tr, C), lambda q, i, cr: (2 * q + cr[0], i, 0)),
                      pl.BlockSpec((None, tr, C), lambda q, i, cr: (q, i, 0))],
            out_specs=pl.BlockSpec((None, tr, C), lambda q, i, cr: (q, i, 0))),
        out_shape=_sds((4, R, C), part.dtype),
        compiler_params=_cp("parallel", "parallel"),
    )(core, part, got)


def _sum_slots(gathered, name):
    _, R, C = gathered.shape

    def body(g_ref, o_ref):
        acc = g_ref[0]
        for s in range(1, N_DEV):
            acc = acc + g_ref[s]
        o_ref[...] = acc

    return _call(body, name=name, out_shape=_sds((R, C), F32))(gathered)


def _sigmoid(z):
    return jax.nn.sigmoid(z)


def _softplus(z):
    e = jnp.exp(-jnp.abs(z))
    log1p_e = jnp.where(e < 0.01, e * (1.0 - e * (0.5 - e * (1.0 / 3.0))), jnp.log(1.0 + e))
    return jnp.maximum(z, 0.0) + log1p_e


_GELU_K = 0.7978845608028654
_GELU_C = 0.044715


def _gelu_and_grad(z):
    t = jnp.tanh(_GELU_K * (z + _GELU_C * z * z * z))
    g = 0.5 * z * (1.0 + t)
    dg = 0.5 * (1.0 + t) + 0.5 * z * (1.0 - t * t) * _GELU_K * (1.0 + 3.0 * _GELU_C * z * z)
    return g, dg


def _gelu(z):
    t = jnp.tanh(_GELU_K * (z + _GELU_C * z * z * z))
    return 0.5 * z * (1.0 + t)


def _row_ids(tile_index, tm, width=1):
    return tile_index * tm + lax.broadcasted_iota(jnp.int32, (tm, width), 0)


def _shift_down(prev, cur, k):
    if k == 0:
        return cur
    ext = jnp.concatenate([prev, cur], axis=0)
    return pltpu.roll(ext, k, axis=0)[prev.shape[0]:]


def _shift_up(cur, nxt, k):
    if k == 0:
        return cur
    ext = jnp.concatenate([cur, nxt], axis=0)
    return pltpu.roll(ext, ext.shape[0] - k, axis=0)[:cur.shape[0]]


def _lru_gates(r, sp):
    z = LRU_C * r * sp
    a = jnp.exp(-z)
    t = jnp.tanh(z)
    mult = jnp.sqrt(2.0 * t / (1.0 + t))
    return a, mult


def _scan_chunks(a_ref, b_ref, out_ref, carry, n_rows, reverse):
    n_chunks = n_rows // 8
    cols = a_ref.shape[1]
    rid = lax.broadcasted_iota(jnp.int32, (8, cols), 0)
    edge = 0 if reverse else 7
    group = SCAN_GROUP if n_chunks % SCAN_GROUP == 0 else 1

    def local_scan(ci):
        rows = pl.ds(pl.multiple_of(ci * 8, 8), 8)
        a = a_ref[rows, :]
        b = b_ref[rows, :]
        for s in (1, 2, 4):
            if reverse:
                keep = rid < 8 - s
                a_n, b_n = pltpu.roll(a, 8 - s, axis=0), pltpu.roll(b, 8 - s, axis=0)
            else:
                keep = rid >= s
                a_n, b_n = pltpu.roll(a, s, axis=0), pltpu.roll(b, s, axis=0)
            b = a * jnp.where(keep, b_n, 0.0) + b
            a = a * jnp.where(keep, a_n, 1.0)
        a_e = jnp.sum(jnp.where(rid == edge, a, 0.0), axis=0, keepdims=True)
        b_e = jnp.sum(jnp.where(rid == edge, b, 0.0), axis=0, keepdims=True)
        return rows, a, b, a_e, b_e

    def chunks(k, h):
        first = k * group
        scans = [local_scan((n_chunks - 1 - (first + j)) if reverse else first + j) for j in range(group)]
        for rows, a, b, a_e, b_e in scans:
            out_ref[rows, :] = a * h + b
            h = a_e * h + b_e
        return h

    return lax.fori_loop(0, n_chunks // group, chunks, carry)


def _norm_fwd(h, g, name):
    Tp, D = h.shape
    tm = _tile(Tp, TILE["norm"])

    def body(h_ref, g_ref, u_ref, r_ref):
        x = h_ref[...]
        r = lax.rsqrt(jnp.mean(x * x, axis=-1, keepdims=True) + NORM_EPS)
        u_ref[...] = (x * r * g_ref[...]).astype(BF16)
        r_ref[...] = r

    return _call(
        body, name=name, grid=(Tp // tm,),
        in_specs=[pl.BlockSpec((tm, D), lambda i: (i, 0)), pl.BlockSpec((1, D), lambda i: (0, 0))],
        out_specs=[pl.BlockSpec((tm, D), lambda i: (i, 0)), pl.BlockSpec((tm, 1), lambda i: (i, 0))],
        out_shape=[_sds((Tp, D), BF16), _sds((Tp, 1), F32)],
        compiler_params=_cp("parallel"),
    )(h, g)


def _proj_fwd(u, w_slots, name, tasks=()):
    Tp, K = u.shape
    S, _, n = w_slots.shape
    tm = _tile(Tp, TILE["proj"])

    def body(a_ref, b_ref, o_ref):
        o_ref[...] = jnp.dot(a_ref[...], b_ref[...], preferred_element_type=F32)

    (proj,), extra = _hosted(
        body, tasks, name=name, grid=(S, Tp // tm),
        in_specs=[pl.BlockSpec((tm, K), lambda j, i: (i, 0)), pl.BlockSpec((None, K, n), lambda j, i: (j, 0, 0))],
        out_specs=[pl.BlockSpec((tm, n), lambda j, i: (i, j))],
        out_shape=[_sds((Tp, S * n), F32)],
        semantics=("parallel", "parallel"), operands=(u, w_slots))
    return proj, extra


def _mlp1_fwd(u2, w_slots, name, tasks=()):
    Tp, K = u2.shape
    S, _, n = w_slots.shape
    tm = _tile(Tp, TILE["mlp1"])

    def body(a_ref, b_ref, act_ref, a1_ref):
        a1 = jnp.dot(a_ref[...], b_ref[...], preferred_element_type=F32)
        relu = jnp.maximum(a1, 0.0)
        act_ref[...] = (relu * relu).astype(BF16)
        a1_ref[...] = a1.astype(BF16)

    return _hosted(
        body, tasks, name=name, grid=(S, Tp // tm),
        in_specs=[pl.BlockSpec((tm, K), lambda j, i: (i, 0)), pl.BlockSpec((None, K, n), lambda j, i: (j, 0, 0))],
        out_specs=[pl.BlockSpec((tm, n), lambda j, i: (i, j))] * 2,
        out_shape=[_sds((Tp, S * n), BF16)] * 2,
        semantics=("parallel", "parallel"), operands=(u2, w_slots))


def _pool_fwd(proj, pool_w, name):
    Tp = proj.shape[0]
    G, Cg, _ = pool_w.shape
    D = G * Cg
    tm = _tile(Tp, TILE["pool"])

    def body(v_ref, w_ref, d_ref, y_ref, prev_ref):
        t = pl.program_id(0)

        @pl.when(t == 0)
        def _():
            prev_ref[...] = jnp.zeros_like(prev_ref)

        rows = _row_ids(t, tm)
        for g, win in enumerate(POOL_WINDOWS):
            cols = slice(g * Cg, (g + 1) * Cg)
            v = v_ref[:, cols]
            s = jnp.concatenate([prev_ref[:, cols], v], axis=0)
            k = 1
            while k < win:
                s = s + pltpu.roll(s, k, axis=0)
                k *= 2
            cnt = jnp.minimum(rows + 1, win).astype(F32)
            d = s[MAX_WINDOW:] / cnt - v
            d_ref[:, cols] = d.astype(BF16)
            y_ref[:, cols] = jnp.dot(d.astype(BF16), w_ref[g], preferred_element_type=F32)
        prev_ref[...] = v_ref[tm - MAX_WINDOW:, :]

    return _call(
        body, name=name, grid=(Tp // tm,),
        in_specs=[pl.BlockSpec((tm, D), lambda t: (t, 0)), pl.BlockSpec((G, Cg, Cg), lambda t: (0, 0, 0))],
        out_specs=[pl.BlockSpec((tm, D), lambda t: (t, 0))] * 2,
        out_shape=[_sds((Tp, D), BF16), _sds((Tp, D), F32)],
        scratch_shapes=[pltpu.VMEM((MAX_WINDOW, D), F32)],
        compiler_params=_cp("arbitrary"),
    )(proj, pool_w)


def _lru_fwd(proj, y_pool, scale, conv_w, conv_b, wa, ba, wx, bx, lam, name, tasks=()):
    Tp = proj.shape[0]
    H, hd, _ = wa.shape
    D = H * hd
    tm = _tile(Tp, TILE["lru"])
    nb = D // hd

    def body(vl_ref, vg_ref, gp_ref, gl_ref, y_ref, sc_ref, cw_ref, cb_ref, wa_ref, ba_ref, wx_ref, bx_ref,
             lam_ref, xc_ref, r_ref, i_ref, a_ref, mult_ref, hs_ref, m_ref, prev_ref, carry_ref, b_s):
        t = pl.program_id(1)

        @pl.when(t == 0)
        def _():
            prev_ref[...] = jnp.zeros_like(prev_ref)
            carry_ref[...] = jnp.zeros_like(carry_ref)

        v = vl_ref[...]
        prev = prev_ref[...]
        xc = jnp.zeros_like(v) + cb_ref[...]
        for k in range(CONV_WIDTH):
            xc = xc + cw_ref[k:k + 1, :] * _shift_down(prev, v, CONV_WIDTH - 1 - k)
        prev_ref[...] = v[tm - HALO:, :]
        xcb = xc.astype(BF16)
        r = _sigmoid(jnp.dot(xcb, wa_ref[...], preferred_element_type=F32) + ba_ref[...])
        i = _sigmoid(jnp.dot(xcb, wx_ref[...], preferred_element_type=F32) + bx_ref[...])
        a, mult = _lru_gates(r, _softplus(-lam_ref[...]))
        a_ref[...] = a
        mult_ref[...] = mult
        b_s[...] = mult * (i * xc)
        xc_ref[...] = xc
        r_ref[...] = r
        i_ref[...] = i
        carry_ref[0:1, :] = _scan_chunks(a_ref, b_s, hs_ref, carry_ref[0:1, :], tm, reverse=False)
        lru_out = hs_ref[...] * _gelu(vg_ref[...])
        pool_out = y_ref[...] * sc_ref[...]
        m_ref[...] = (_sigmoid(gp_ref[...]) * pool_out + _sigmoid(gl_ref[...]) * lru_out).astype(BF16)

    def piece(p):
        return pl.BlockSpec((tm, hd), lambda h, t: (t, p * nb + h))

    blk = pl.BlockSpec((tm, hd), lambda h, t: (t, h))
    vec = pl.BlockSpec((1, hd), lambda h, t: (0, h))
    mat = pl.BlockSpec((None, hd, hd), lambda h, t: (h, 0, 0))
    bias = pl.BlockSpec((None, 1, hd), lambda h, t: (h, 0, 0))
    return _hosted(
        body, tasks, name=name, grid=(H, Tp // tm),
        in_specs=[piece(1), piece(2), piece(3), piece(4), blk, vec,
                  pl.BlockSpec((CONV_WIDTH, hd), lambda h, t: (0, h)), vec, mat, bias, mat, bias, vec],
        out_specs=[blk] * 7,
        out_shape=[_sds((Tp, D), F32)] * 6 + [_sds((Tp, D), BF16)],
        scratch_shapes=[pltpu.VMEM((HALO, hd), F32), pltpu.VMEM((8, hd), F32), pltpu.VMEM((tm, hd), F32)],
        semantics=("parallel", "arbitrary"),
        operands=(proj, proj, proj, proj, y_pool, scale, conv_w, conv_b, wa, ba.reshape(H, 1, hd), wx,
                  bx.reshape(H, 1, hd), lam))


def _wout_norm_fwd(merged, w_out, h0, g2, name, tasks=()):
    Tp, D = h0.shape
    tm = _tile(Tp, TILE["wout"])

    def body(m_ref, w_ref, h0_ref, g_ref, h1_ref, u2_ref, r2_ref):
        h1 = h0_ref[...] + jnp.dot(m_ref[...], w_ref[...], preferred_element_type=F32)
        r = lax.rsqrt(jnp.mean(h1 * h1, axis=-1, keepdims=True) + NORM_EPS)
        h1_ref[...] = h1
        u2_ref[...] = (h1 * r * g_ref[...]).astype(BF16)
        r2_ref[...] = r

    row = pl.BlockSpec((tm, D), lambda i: (i, 0))
    return _hosted(
        body, tasks, name=name, grid=(Tp // tm,),
        in_specs=[row, pl.BlockSpec((D, D), lambda i: (0, 0)), row, pl.BlockSpec((1, D), lambda i: (0, 0))],
        out_specs=[row, row, pl.BlockSpec((tm, 1), lambda i: (i, 0))],
        out_shape=[_sds((Tp, D), F32), _sds((Tp, D), BF16), _sds((Tp, 1), F32)],
        semantics=("parallel",), operands=(merged, w_out, h0, g2))


def _mlp2_loss(act, w2, h1, target, gf, n_meta, seq, name):
    Tp, D = h1.shape
    K = act.shape[1]
    tm = _tile(Tp, TILE["mlp2"])
    tk = min(K, MLP2_K)
    nk = K // tk

    rc = _tile(tm, EPILOGUE_ROWS)
    nt = Tp // tm
    last_rows = n_meta + seq - (nt - 1) * tm
    assert nt >= 2 and n_meta % 8 == 0 and 0 < last_rows <= tm and last_rows % 8 == 0

    def body(a_ref, w_ref, h1_hbm, t_hbm, g_ref, dh_ref, dhb_ref, loss_ref, dg_ref, h1_buf, t_buf, sems):
        i, k = pl.program_id(0), pl.program_id(1)
        tile_rows = pl.ds(pl.multiple_of(i * tm, tm), tm)
        fetch_h1 = pltpu.make_async_copy(h1_hbm.at[tile_rows, :], h1_buf, sems.at[0])
        fetch_t = [
            (i == 0, pltpu.make_async_copy(t_hbm.at[pl.ds(0, tm - n_meta), :], t_buf.at[pl.ds(n_meta, tm - n_meta), :],
                                           sems.at[1])),
            ((i > 0) & (i < nt - 1),
             pltpu.make_async_copy(t_hbm.at[pl.ds(pl.multiple_of(i * tm - n_meta, 8), tm), :], t_buf, sems.at[1])),
            (i == nt - 1, pltpu.make_async_copy(t_hbm.at[pl.ds((nt - 1) * tm - n_meta, last_rows), :],
                                                t_buf.at[pl.ds(0, last_rows), :], sems.at[1])),
        ]

        @pl.when(k == 0)
        def _():
            fetch_h1.start()
            dh_ref[...] = jnp.zeros_like(dh_ref)

        for cond, f in fetch_t:
            @pl.when(cond & (k == 0))
            def _(f=f):
                f.start()

        @pl.when((i == 0) & (k == 0))
        def _():
            loss_ref[...] = jnp.zeros_like(loss_ref)
            dg_ref[...] = jnp.zeros_like(dg_ref)
            t_buf[0:n_meta, :] = jnp.zeros((n_meta, D), F32)

        dh_ref[...] += jnp.dot(a_ref[...], w_ref[...], preferred_element_type=F32)

        for cond, f in fetch_t:
            @pl.when(cond & (k == nk - 1))
            def _(f=f):
                f.wait()

        @pl.when(k == nk - 1)
        def _():
            fetch_h1.wait()
            g = g_ref[...]

            def chunk(c, carry):
                loss_acc, dg_acc = carry
                rows = pl.ds(pl.multiple_of(c * rc, rc), rc)
                h2 = h1_buf[rows, :] + dh_ref[rows, :]
                r = lax.rsqrt(jnp.mean(h2 * h2, axis=-1, keepdims=True) + NORM_EPS)
                out = h2 * r * g
                row_id = i * tm + c * rc + lax.broadcasted_iota(jnp.int32, (rc, 1), 0)
                valid = (row_id >= n_meta) & (row_id < n_meta + seq)
                diff = jnp.where(valid, out - t_buf[rows, :], 0.0)
                dout = diff / D
                dog = dout * g
                dh = r * dog - h2 * (r * r * r * jnp.mean(dog * h2, axis=-1, keepdims=True))
                dh_ref[rows, :] = dh
                dhb_ref[rows, :] = dh.astype(BF16)
                loss_acc = loss_acc + 0.5 * jnp.sum(jnp.mean(diff * diff, axis=-1, keepdims=True), axis=0, keepdims=True)
                return loss_acc, dg_acc + jnp.sum(dout * (h2 * r), axis=0, keepdims=True)

            loss_sum, dg_sum = lax.fori_loop(0, tm // rc, chunk, (jnp.zeros((1, 1), F32), jnp.zeros((1, D), F32)))
            loss_ref[...] += loss_sum
            dg_ref[...] += dg_sum

    row = pl.BlockSpec((tm, D), lambda i, k: (i, 0))
    hbm = pl.BlockSpec(memory_space=pl.ANY)
    return _call(
        body, name=name, grid=(Tp // tm, nk),
        in_specs=[pl.BlockSpec((tm, tk), lambda i, k: (i, k)), pl.BlockSpec((tk, D), lambda i, k: (k, 0)),
                  hbm, hbm, pl.BlockSpec((1, D), lambda i, k: (0, 0))],
        out_specs=[row, row, pl.BlockSpec((8, 128), lambda i, k: (0, 0)), pl.BlockSpec((1, D), lambda i, k: (0, 0))],
        out_shape=[_sds((Tp, D), F32), _sds((Tp, D), BF16), _sds((8, 128), F32), _sds((1, D), F32)],
        scratch_shapes=[pltpu.VMEM((tm, D), F32), pltpu.VMEM((tm, D), F32), pltpu.SemaphoreType.DMA((2,))],
        compiler_params=_cp("arbitrary", "arbitrary"),
    )(act, w2, h1, target, gf)


def _dact_bwd(dh2b, w2_slots, a1, name):
    Tp, D = dh2b.shape
    S, n, _ = w2_slots.shape
    tm = _tile(Tp, TILE["dact"])

    def body(g_ref, w_ref, a1_ref, o_ref):
        dact = lax.dot_general(g_ref[...], w_ref[...], _NT, preferred_element_type=F32)
        o_ref[...] = (dact * (2.0 * jnp.maximum(a1_ref[...].astype(F32), 0.0))).astype(BF16)

    return _call(
        body, name=name, grid=(S, Tp // tm),
        in_specs=[pl.BlockSpec((tm, D), lambda j, i: (i, 0)), pl.BlockSpec((None, n, D), lambda j, i: (j, 0, 0)),
                  pl.BlockSpec((tm, n), lambda j, i: (i, j))],
        out_specs=pl.BlockSpec((tm, n), lambda j, i: (i, j)),
        out_shape=_sds((Tp, S * n), BF16),
        compiler_params=_cp("parallel", "parallel"),
    )(dh2b, w2_slots, a1)


def _weight_grad(a, g, blocks, block_a, name, tasks=(), part=(0, 1)):
    Tp, Ka = a.shape
    Ng = g.shape[1]
    p, parts = part
    assert parts == 1 or not block_a
    ka = Ka // blocks if block_a else Ka // parts
    ng = Ng if block_a else Ng // blocks
    tt = _tile(Tp, TILE["tn"])
    nt = Tp // tt

    def body(a_ref, g_ref, o_ref, acc_ref):
        t = pl.program_id(1)

        @pl.when(t == 0)
        def _():
            acc_ref[...] = jnp.zeros_like(acc_ref)

        acc_ref[...] += lax.dot_general(a_ref[...], g_ref[...], _TN, preferred_element_type=F32)

        @pl.when(t == nt - 1)
        def _():
            o_ref[...] = acc_ref[...].astype(o_ref.dtype)

    if block_a:
        a_spec = pl.BlockSpec((tt, ka), lambda j, t: (t, j))
        g_spec = pl.BlockSpec((tt, ng), lambda j, t: (t, 0))
    else:
        a_spec = pl.BlockSpec((tt, ka), lambda j, t: (t, p))
        g_spec = pl.BlockSpec((tt, ng), lambda j, t: (t, j))
    (dw,), extra = _hosted(
        body, tasks, name=name, grid=(blocks, nt),
        in_specs=[a_spec, g_spec],
        out_specs=[pl.BlockSpec((None, ka, ng), lambda j, t: (j, 0, 0))],
        out_shape=[_sds((blocks, ka, ng), BF16)],
        scratch_shapes=[pltpu.VMEM((ka, ng), F32)],
        semantics=("parallel", "arbitrary"), operands=(a, g))
    return dw, extra


def _nt_norm_bwd(dz, w_parts, dres, hin, rin, g, want_bf16, name, tasks=(), tiles=None, earlier=None):
    Tp, D = hin.shape
    P = len(w_parts)
    S, _, n = w_parts[0].shape
    K = S * P
    tm = _tile(Tp, TILE["nt"])
    t0, nt = tiles if tiles is not None else (0, Tp // tm)
    assert not (want_bf16 and earlier is not None)

    rc = _tile(tm, EPILOGUE_ROWS)

    def body(dz_ref, *rest):
        w_refs, (dres_hbm, h_hbm, r_ref, g_ref), rest = rest[:P], rest[P:P + 4], rest[P + 4:]
        if earlier is not None:
            _, dg0_ref, dh_ref, dg_ref, dres_buf, h_buf, sems = rest
        elif want_bf16:
            dh_ref, dhb_ref, dg_ref, dres_buf, h_buf, sems = rest
        else:
            dh_ref, dg_ref, dres_buf, h_buf, sems = rest
        i, k = pl.program_id(0), pl.program_id(1)
        tile_rows = pl.ds(pl.multiple_of((t0 + i) * tm, tm), tm)
        fetch = [pltpu.make_async_copy(dres_hbm.at[tile_rows, :], dres_buf, sems.at[0]),
                 pltpu.make_async_copy(h_hbm.at[tile_rows, :], h_buf, sems.at[1])]

        @pl.when(k == 0)
        def _():
            for f in fetch:
                f.start()
            dh_ref[...] = jnp.zeros_like(dh_ref)

        @pl.when((i == 0) & (k == 0))
        def _():
            dg_ref[...] = jnp.zeros_like(dg_ref) if earlier is None else dg0_ref[...]

        for q in range(P):
            @pl.when(k % P == q)
            def _(q=q):
                dh_ref[...] += lax.dot_general(dz_ref[...], w_refs[q][...], _NT, preferred_element_type=F32)

        @pl.when(k == K - 1)
        def _():
            for f in fetch:
                f.wait()
            g = g_ref[...]

            def chunk(c, dg_acc):
                rows = pl.ds(pl.multiple_of(c * rc, rc), rc)
                du = dh_ref[rows, :]
                h = h_buf[rows, :]
                r = r_ref[rows, :]
                dug = du * g
                dh = dres_buf[rows, :] + r * dug - h * (r * r * r * jnp.mean(dug * h, axis=-1, keepdims=True))
                dh_ref[rows, :] = dh
                if want_bf16:
                    dhb_ref[rows, :] = dh.astype(BF16)
                return dg_acc + jnp.sum(du * (h * r), axis=0, keepdims=True)

            dg_ref[...] += lax.fori_loop(0, tm // rc, chunk, jnp.zeros((1, D), F32))

    row = pl.BlockSpec((tm, D), lambda i, k: (t0 + i, 0))
    vec = pl.BlockSpec((1, D), lambda i, k: (0, 0))
    hbm = pl.BlockSpec(memory_space=pl.ANY)
    out_specs = [row] + ([row] if want_bf16 else []) + [vec]
    out_shape = [_sds((Tp, D), F32)] + ([_sds((Tp, D), BF16)] if want_bf16 else []) + [_sds((1, D), F32)]
    in_specs = ([pl.BlockSpec((tm, n), lambda i, k: (t0 + i, k))]
                + [pl.BlockSpec((None, D, n), lambda i, k: (k // P, 0, 0))] * P
                + [hbm, hbm, pl.BlockSpec((tm, 1), lambda i, k: (t0 + i, 0)), vec])
    operands = (dz, *w_parts, dres, hin, rin, g)
    aliases = {}
    if earlier is not None:
        in_specs += [hbm, vec]
        operands += tuple(earlier)
        aliases = {P + 5: 0}
    return _hosted(
        body, tasks, name=name, grid=(nt, K), in_specs=in_specs, out_specs=out_specs, out_shape=out_shape,
        scratch_shapes=[pltpu.VMEM((tm, D), F32), pltpu.VMEM((tm, D), F32), pltpu.SemaphoreType.DMA((2,))],
        semantics=("arbitrary", "arbitrary"), operands=operands, aliases=aliases)


def _dmerged_bwd(dh1b, w_out, name):
    Tp, D = dh1b.shape
    tm = _tile(Tp, TILE["dmerged"])

    def body(g_ref, w_ref, o_ref):
        o_ref[...] = lax.dot_general(g_ref[...], w_ref[...], _NT, preferred_element_type=F32)

    row = pl.BlockSpec((tm, D), lambda i: (i, 0))
    return _call(
        body, name=name, grid=(Tp // tm,),
        in_specs=[row, pl.BlockSpec((D, D), lambda i: (0, 0))],
        out_specs=row, out_shape=_sds((Tp, D), F32),
        compiler_params=_cp("parallel"),
    )(dh1b, w_out)


def _pool_bwd(dmerged, proj, y_pool, d_pool, scale, pool_w, name, tasks=()):
    Tp, D = dmerged.shape
    G, Cg, _ = pool_w.shape
    tm = _tile(Tp, TILE["pool"])
    nt = Tp // tm

    def body(dm_ref, gp_ref, y_ref, d_ref, sc_ref, w_ref, dproj_hbm, dw_ref, dsc_ref, next_ref, out_buf, out_sems):
        t = pl.program_id(0)
        tile = nt - 1 - t
        slot = t % 2
        dv_ref, dgp_ref = out_buf.at[slot, 0], out_buf.at[slot, 1]
        tile_rows = pl.ds(pl.multiple_of(tile * tm, tm), tm)

        def out_copies(s):
            return [pltpu.make_async_copy(out_buf.at[s, k], dproj_hbm.at[tile_rows, pl.ds(piece * D, D)],
                                          out_sems.at[2 * s + k]) for k, piece in enumerate((0, 3))]

        @pl.when(t >= 2)
        def _():
            for cp in out_copies(slot):
                cp.wait()

        @pl.when(t == 0)
        def _():
            next_ref[...] = jnp.zeros_like(next_ref)
            dw_ref[...] = jnp.zeros_like(dw_ref)
            dsc_ref[...] = jnp.zeros_like(dsc_ref)

        rows = _row_ids(tile, tm)
        dm = dm_ref[...]
        y = y_ref[...]
        sc = sc_ref[...]
        sg = _sigmoid(gp_ref[...])
        dpo = dm * sg
        dgp_ref[...] = (dm * (y * sc) * sg * (1.0 - sg)).astype(BF16)
        dsc_ref[...] += jnp.sum(dpo * y, axis=0, keepdims=True)
        dyb = (dpo * sc).astype(BF16)
        for g, win in enumerate(POOL_WINDOWS):
            cols = slice(g * Cg, (g + 1) * Cg)
            dy = dyb[:, cols]
            dd = lax.dot_general(dy, w_ref[g], _NT, preferred_element_type=F32)
            dw_ref[g] += lax.dot_general(d_ref[:, cols], dy, _TN, preferred_element_type=F32)
            q = dd / jnp.minimum(rows + 1, win).astype(F32)
            s = jnp.concatenate([q, next_ref[:, cols]], axis=0)
            k = 1
            while k < win:
                s = s + pltpu.roll(s, s.shape[0] - k, axis=0)
                k *= 2
            dv_ref[:, cols] = (s[:tm] - dd).astype(BF16)
            next_ref[:, cols] = q[:MAX_WINDOW]
        for cp in out_copies(slot):
            cp.start()

        @pl.when(t == nt - 1)
        def _():
            for cp in out_copies(slot) + (out_copies(1 - slot) if nt > 1 else []):
                cp.wait()

    row = pl.BlockSpec((tm, D), lambda t: (nt - 1 - t, 0))
    return _hosted(
        body, tasks, name=name, grid=(nt,),
        in_specs=[row, pl.BlockSpec((tm, D), lambda t: (nt - 1 - t, 3)), row, row,
                  pl.BlockSpec((1, D), lambda t: (0, 0)), pl.BlockSpec((G, Cg, Cg), lambda t: (0, 0, 0))],
        out_specs=[pl.BlockSpec(memory_space=pl.ANY), pl.BlockSpec((G, Cg, Cg), lambda t: (0, 0, 0)),
                   pl.BlockSpec((1, D), lambda t: (0, 0))],
        out_shape=[_sds((Tp, proj.shape[1]), BF16), _sds((G, Cg, Cg), F32), _sds((1, D), F32)],
        scratch_shapes=[pltpu.VMEM((MAX_WINDOW, D), F32), pltpu.VMEM((2, 2, tm, D), BF16),
                        pltpu.SemaphoreType.DMA((4,))],
        semantics=("arbitrary",), operands=(dmerged, proj, y_pool, d_pool, scale, pool_w))


LRU_SMALL_ROWS = 8


def _lru_bwd(dmerged, proj, xc, r_gate, i_gate, a_gate, mult_gate, hs, lam, conv_w, wa, wx, dproj, name, tasks=()):
    Tp, D = dmerged.shape
    H, hd, _ = wa.shape
    tm = _tile(Tp, TILE["lru"])
    nt = Tp // tm
    nb = D // hd
    halo_blocks = tm // HALO

    def body(dm_ref, vl_ref, vg_ref, gl_ref, xc_ref, r_ref, i_ref, a_ref, mult_ref, hs_ref, hsp_ref, lam_ref, cw_ref,
             wa_ref, wx_ref, _, dproj_hbm, dwa_ref, dwx_ref, small_ref,
             mu_next_ref, dxc_next_ref, q_s, mu_s, out_buf, out_sems):
        h_id, t = pl.program_id(0), pl.program_id(1)
        tile = nt - 1 - t
        step = h_id * nt + t
        slot = step % 2
        dvl_ref, dvg_ref, dgl_ref = out_buf.at[slot, 0], out_buf.at[slot, 1], out_buf.at[slot, 2]
        tile_rows = pl.ds(pl.multiple_of(tile * tm, tm), tm)

        def out_copies(s):
            return [pltpu.make_async_copy(
                out_buf.at[s, k], dproj_hbm.at[tile_rows, pl.ds(pl.multiple_of((piece * nb + h_id) * hd, hd), hd)],
                out_sems.at[3 * s + k]) for k, piece in enumerate((1, 2, 4))]

        @pl.when(step >= 2)
        def _():
            for cp in out_copies(slot):
                cp.wait()

        @pl.when(t == 0)
        def _():
            mu_next_ref[...] = jnp.zeros_like(mu_next_ref)
            dxc_next_ref[...] = jnp.zeros_like(dxc_next_ref)
            dwa_ref[...] = jnp.zeros_like(dwa_ref)
            dwx_ref[...] = jnp.zeros_like(dwx_ref)
            small_ref[...] = jnp.zeros_like(small_ref)

        first = tile == 0
        dm = dm_ref[...]
        hs_t = hs_ref[...]
        xc_t = xc_ref[...]
        r = r_ref[...]
        i = i_ref[...]
        lam_v = lam_ref[...]
        sp = _softplus(-lam_v)
        a = a_ref[...]
        mult = mult_ref[...]

        sg = _sigmoid(gl_ref[...])
        ge, dge = _gelu_and_grad(vg_ref[...])
        dlo = dm * sg
        dgl_ref[...] = (dm * (hs_t * ge) * sg * (1.0 - sg)).astype(BF16)
        dvg_ref[...] = (dlo * hs_t * dge).astype(BF16)
        dhs = dlo * ge

        q_s[...] = a * dhs
        mu_first = _scan_chunks(a_ref, q_s, mu_s, mu_next_ref[0:1, :], tm, reverse=True)
        lam_t = dhs + _shift_up(mu_s[...], mu_next_ref[...], 1)
        mu_next_ref[...] = jnp.broadcast_to(mu_first, mu_next_ref.shape)

        h_prev = _shift_down(jnp.where(first, 0.0, hsp_ref[...]), hs_t, 1)
        da = lam_t * h_prev
        dmult = lam_t * (i * xc_t)
        di = lam_t * mult * xc_t
        dxc = lam_t * mult * i
        dlog_a = da * a - dmult * (a * a) / mult
        dr = dlog_a * (-LRU_C * sp)
        dlam_rows = dlog_a * (-LRU_C * r)
        dza = dr * r * (1.0 - r)
        dzx = di * i * (1.0 - i)
        dzab, dzxb = dza.astype(BF16), dzx.astype(BF16)
        xcb = xc_t.astype(BF16)
        dxc = dxc + lax.dot_general(dzab, wa_ref[...], _NT, preferred_element_type=F32)
        dxc = dxc + lax.dot_general(dzxb, wx_ref[...], _NT, preferred_element_type=F32)
        dwa_ref[...] += lax.dot_general(xcb, dzab, _TN, preferred_element_type=F32)
        dwx_ref[...] += lax.dot_general(xcb, dzxb, _TN, preferred_element_type=F32)

        dxc_next = dxc_next_ref[...]
        taps = [_shift_up(dxc, dxc_next, CONV_WIDTH - 1 - k) for k in range(CONV_WIDTH)]
        dv = jnp.zeros_like(dxc)
        for k in range(CONV_WIDTH):
            dv = dv + cw_ref[k:k + 1, :] * taps[k]
        dvl_ref[...] = dv.astype(BF16)
        dxc_next_ref[...] = dxc[:HALO, :]

        v_t = vl_ref[...]
        small = [jnp.sum(dza, axis=0, keepdims=True), jnp.sum(dzx, axis=0, keepdims=True),
                 jnp.sum(dlam_rows, axis=0, keepdims=True) * (-_sigmoid(-lam_v)),
                 jnp.sum(dxc, axis=0, keepdims=True)]
        for k in range(CONV_WIDTH):
            small.append(jnp.sum(taps[k] * v_t, axis=0, keepdims=True))
        for k, row in enumerate(small):
            small_ref[k:k + 1, :] += row
        for cp in out_copies(slot):
            cp.start()

        @pl.when(step == H * nt - 1)
        def _():
            for cp in out_copies(slot) + (out_copies(1 - slot) if H * nt > 1 else []):
                cp.wait()

    def piece(p):
        return pl.BlockSpec((tm, hd), lambda h, t: (nt - 1 - t, p * nb + h))

    def halo(p):
        return pl.BlockSpec((HALO, hd), lambda h, t: (jnp.maximum((nt - 1 - t) * halo_blocks - 1, 0), p * nb + h))

    blk = pl.BlockSpec((tm, hd), lambda h, t: (nt - 1 - t, h))
    vec = pl.BlockSpec((1, hd), lambda h, t: (0, h))
    mat = pl.BlockSpec((None, hd, hd), lambda h, t: (h, 0, 0))
    return _hosted(
        body, tasks, name=name, grid=(H, nt),
        in_specs=[blk, piece(1), piece(2), piece(4), blk, blk, blk, blk, blk, blk, halo(0), vec,
                  pl.BlockSpec((CONV_WIDTH, hd), lambda h, t: (0, h)), mat, mat, pl.BlockSpec(memory_space=pl.ANY)],
        out_specs=[pl.BlockSpec(memory_space=pl.ANY), mat, mat,
                   pl.BlockSpec((None, LRU_SMALL_ROWS, hd), lambda h, t: (h, 0, 0))],
        out_shape=[_sds(dproj.shape, BF16)] + [_sds((H, hd, hd), F32)] * 2 + [_sds((H, LRU_SMALL_ROWS, hd), F32)],
        scratch_shapes=[pltpu.VMEM((HALO, hd), F32), pltpu.VMEM((HALO, hd), F32),
                        pltpu.VMEM((tm, hd), F32), pltpu.VMEM((tm, hd), F32),
                        pltpu.VMEM((2, 3, tm, hd), BF16), pltpu.SemaphoreType.DMA((6,))],
        semantics=("arbitrary", "arbitrary"), aliases={15: 0},
        operands=(dmerged, proj, proj, proj, xc, r_gate, i_gate, a_gate, mult_gate, hs, hs, lam, conv_w, wa, wx,
                  dproj))


def _adamw(w, g, m, v):
    m = ADAM_B1 * m + (1.0 - ADAM_B1) * g
    v = ADAM_B2 * v + (1.0 - ADAM_B2) * (g * g)
    m_hat = m / (1.0 - ADAM_B1 ** ADAM_STEP)
    v_hat = v / (1.0 - ADAM_B2 ** ADAM_STEP)
    delta = -ADAM_LR * (m_hat / (jnp.sqrt(v_hat) + ADAM_EPS) + ADAM_WD * w)
    return delta, m, v


def _reduce_update(pair_sums, chip_sums, w, m, v, chip_slot, name, part=(0, 1), earlier=None):
    R, C = pair_sums.shape[1:]
    p, parts = part
    tr = _tile(R, TILE["update"])
    nblk = R // tr

    def body(slot_ref, own_ref, got_ref, w_ref, m_ref, v_ref, *rest):
        g_out, d_out, m_out, v_out = rest[-4:]
        g = own_ref[...].astype(F32)
        for k in range(3):
            g = g + got_ref[k].astype(F32)
        d, m_new, v_new = _adamw(w_ref[...], g, m_ref[...], v_ref[...])
        g_out[...] = g
        d_out[...] = d
        m_out[...] = m_new
        v_out[...] = v_new

    blk = pl.BlockSpec((tr, C), lambda i, s: (p * nblk + i, 0))
    in_specs = [pl.BlockSpec((None, tr, C), lambda i, s: (s[0], i, 0)),
                pl.BlockSpec((3, tr, C), lambda i, s: (0, i, 0)), blk, blk, blk]
    operands = (chip_slot, pair_sums, chip_sums, w, m, v)
    aliases = {}
    if earlier is not None:
        in_specs += [pl.BlockSpec(memory_space=pl.ANY)] * 4
        operands += tuple(earlier)
        aliases = {6 + k: k for k in range(4)}
    return _call(
        body, name=name,
        grid_spec=pltpu.PrefetchScalarGridSpec(
            num_scalar_prefetch=1, grid=(R // tr,), in_specs=in_specs, out_specs=[blk] * 4),
        out_shape=[_sds((parts * R, C), F32)] * 4,
        input_output_aliases=aliases,
        compiler_params=_cp("parallel"),
    )(*operands)


def _small_updates(small_sum, bias_grads, col0, specs, name):
    n = len(specs)

    def body(col_ref, ss_ref, bg_ref, *refs):
        ins, outs = refs[:3 * n], refs[3 * n:]
        for k, (w, _, _, kind, r0) in enumerate(specs):
            nr, nc = w.shape
            if kind == "rows":
                g = ss_ref[r0:r0 + nr, :]
            elif kind == "cols":
                g = ss_ref[r0:r0 + nr, pl.ds(pl.multiple_of(col_ref[0], nc), nc)]
            else:
                g = bg_ref[r0:r0 + nr, :]
            d, m_new, v_new = _adamw(ins[3 * k][...], g, ins[3 * k + 1][...], ins[3 * k + 2][...])
            for o, val in zip(outs[4 * k:4 * k + 4], (g, d, m_new, v_new)):
                o[...] = val

    def whole(a):
        return pl.BlockSpec(a.shape, lambda i, c: (0,) * a.ndim)

    arrays = [a for (w, m, v, _, _) in specs for a in (w, m, v)]
    res = _call(
        body, name=name,
        grid_spec=pltpu.PrefetchScalarGridSpec(
            num_scalar_prefetch=1, grid=(1,),
            in_specs=[whole(small_sum), whole(bias_grads)] + [whole(a) for a in arrays],
            out_specs=[whole(w) for (w, _, _, _, _) in specs for _ in range(4)]),
        out_shape=[_sds(w.shape, F32) for (w, _, _, _, _) in specs for _ in range(4)],
        compiler_params=_cp("arbitrary"),
    )(col0, small_sum, bias_grads, *arrays)
    return [tuple(res[4 * k:4 * k + 4]) for k in range(n)]


def _slots_from_rows(full, lead):
    L, R, C = full.shape
    r = R // N_DEV
    return full.reshape(L, N_DEV, r, C).transpose(1, 0, 2, 3).reshape(N_DEV, L * r, C)


def _rows_from_slots(slots, lead):
    _, LR, C = slots.shape
    r = LR // lead
    return slots.reshape(N_DEV, lead, r, C).transpose(1, 0, 2, 3).reshape(lead, N_DEV * r, C)


def kernel(x, meta_tokens, norm1_g, w_in, pool_w, pool_scale, conv_w, conv_b, gate_a_w, gate_a_b, gate_x_w, gate_x_b, lru_lambda, w_out, norm2_g, mlp_w1, mlp_w2, final_g, loss_target, m_meta_tokens, m_norm1_g, m_w_in, m_pool_w, m_pool_scale, m_conv_w, m_conv_b, m_gate_a_w, m_gate_a_b, m_gate_x_w, m_gate_x_b, m_lru_lambda, m_w_out, m_norm2_g, m_mlp_w1, m_mlp_w2, m_final_g, v_meta_tokens, v_norm1_g, v_w_in, v_pool_w, v_pool_scale, v_conv_w, v_conv_b, v_gate_a_w, v_gate_a_b, v_gate_x_w, v_gate_x_b, v_lru_lambda, v_w_out, v_norm2_g, v_mlp_w1, v_mlp_w2, v_final_g):
    seq, D = x.shape[1], x.shape[2]
    n_meta = meta_tokens.shape[0]
    G, Cg = pool_w.shape[1], pool_w.shape[3]
    H, hd = gate_a_w.shape[1], gate_a_w.shape[3]
    T = n_meta + seq
    Tp = -(-T // ROW_ALIGN) * ROW_ALIGN
    ix, iy, ic = _pos()
    me = 4 * ix + 2 * iy + ic
    core = jnp.reshape(ic, (1,)).astype(jnp.int32)
    chip_slot = jnp.reshape(2 * ix + iy, (1,)).astype(jnp.int32)

    w_in_l, w1_l, w2_l, w_out_l = w_in[0], mlp_w1[0], mlp_w2[0], w_out[0]
    pool_l = pool_w[0].reshape(G * (Cg // N_DEV), Cg)
    wa_l = gate_a_w[0].reshape(H * (hd // N_DEV), hd)
    wx_l = gate_x_w[0].reshape(H * (hd // N_DEV), hd)
    small_params = jnp.concatenate(
        [meta_tokens, conv_w[0], jnp.zeros((4, D // N_DEV), F32)], axis=0)
    biases = jnp.concatenate([gate_a_b[0], gate_x_b[0]], axis=0)
    (w_in_g, small_g, bias_g) = _all_gather_relay(
        [w_in_l.astype(BF16), small_params, biases], [W_IN_CHUNKS, 1, 1], "gather_first")
    small_full = small_g.transpose(1, 0, 2).reshape(n_meta + 8, D)
    meta_full = small_full[:n_meta]
    conv_full = small_full[n_meta:n_meta + CONV_WIDTH]
    bias_full = bias_g.transpose(1, 0, 2).reshape(2 * H, hd)
    ba_full, bx_full = bias_full[:H], bias_full[H:]

    h0 = jnp.concatenate([meta_full, x[0], jnp.zeros((Tp - T, D), F32)], axis=0)
    u, r1 = _norm_fwd(h0, norm1_g, "norm1")
    proj, ((pool_g,), (wa_g,), (wx_g,), (w_out_g,), (w1_ici,)) = _proj_fwd(
        u, w_in_g, "proj",
        tasks=[_AgFull(pool_l.astype(BF16)), _AgFull(wa_l.astype(BF16)), _AgFull(wx_l.astype(BF16)),
               _AgFull(w_out_l.astype(BF16)), _AgFull(w1_l.astype(BF16), forward_here=False)])
    pool_full = _rows_from_slots(pool_g, G)
    wa_full = _rows_from_slots(wa_g, H)
    wx_full = _rows_from_slots(wx_g, H)
    w_in_parts = [w_in_g]
    d_pool, y_pool = _pool_fwd(proj, pool_full, "pool_fwd")
    (xc, r_gate, i_gate, a_gate, mult_gate, hs, merged), ((w1_g,), (w2_ici,)) = _lru_fwd(
        proj, y_pool, pool_scale, conv_full, conv_b, wa_full, ba_full, wx_full, bx_full, lru_lambda, "lru_fwd",
        tasks=[_AgForward(w1_ici), _AgFull(w2_l.astype(BF16), forward_here=False)])
    w_out_full = w_out_g.reshape(D, D)
    (h1, u2, r2), ((w2_g,),) = _wout_norm_fwd(merged, w_out_full, h0, norm2_g, "wout_norm2", tasks=[_AgForward(w2_ici)])
    (act, a1), _ = _mlp1_fwd(u2, w1_g, "mlp1")
    dh2, dh2b, loss_tile, d_final_g = _mlp2_loss(
        act, w2_g.reshape(-1, D), h1, loss_target[0], final_g.reshape(1, D), n_meta, seq, "mlp2_loss")

    def pair(part, got, tag):
        return _pair_sum(part, got, core, "pair_sum_" + tag)

    d_a1 = _dact_bwd(dh2b, w2_g, a1, "dact")
    dw2_p, _ = _weight_grad(act, dh2b, N_DEV, True, "dw2")
    dw1_p, ((dw2_got,),) = _weight_grad(u2, d_a1, N_DEV, False, "dw1", tasks=[_RsSibling(dw2_p)])
    dw2_pair = pair(dw2_p, dw2_got, "w2")
    (dh1, dh1b, d_norm2_g), ((dw2_chips,), (dw1_got,)) = _nt_norm_bwd(
        d_a1, [w1_g], dh2, h1, r2, norm2_g, True, "du2_norm2", tasks=[_RsChips(dw2_pair), _RsSibling(dw1_p)])
    dw1_pair = pair(dw1_p, dw1_got, "w1")
    dmerged = _dmerged_bwd(dh1b, w_out_full, "dmerged")
    dwout_p, _ = _weight_grad(merged, dh1b, 2, True, "dwout")
    dwout_p = dwout_p.reshape(N_DEV, D // N_DEV, D)
    (dproj_pool, dpool_full, d_scale), ((dwout_got,),) = _pool_bwd(
        dmerged, proj, y_pool, d_pool, pool_scale, pool_full, "pool_bwd", tasks=[_RsSibling(dwout_p)])
    dwout_pair = pair(dwout_p, dwout_got, "wout")
    (dproj, dwa_full, dwx_full, lru_small), ((dw1_chips,), (dwout_chips,)) = _lru_bwd(
        dmerged, proj, xc, r_gate, i_gate, a_gate, mult_gate, hs, lru_lambda, conv_full, wa_full, wx_full, dproj_pool,
        "lru_bwd", tasks=[_RsChips(dw1_pair), _RsChips(dwout_pair)])
    dwin_a, _ = _weight_grad(u, dproj, N_DEV, False, "dwin_a", part=(0, 2))
    dwin_b, ((dwin_a_got,),) = _weight_grad(u, dproj, N_DEV, False, "dwin_b", part=(1, 2), tasks=[_RsSibling(dwin_a)])
    dwin_a_pair = pair(dwin_a, dwin_a_got, "win_a")
    dpool_p = _slots_from_rows(dpool_full, G).astype(BF16)
    dwa_p = _slots_from_rows(dwa_full, H).astype(BF16)
    dwx_p = _slots_from_rows(dwx_full, H).astype(BF16)
    late = [dwin_b, dpool_p, dwa_p, dwx_p]
    n_tiles = Tp // _tile(Tp, TILE["nt"])
    n_first = max(1, n_tiles // 2)
    (dh0_a, d_norm1_a), ((dwin_a_chips,), *late_got) = _nt_norm_bwd(
        dproj, w_in_parts, dh1, h0, r1, norm1_g, False, "du_norm1_a",
        tasks=[_RsChips(dwin_a_pair)] + [_RsSibling(p) for p in late], tiles=(0, n_first))
    late_pair = [pair(p, g[0], "late%d" % k) for k, (p, g) in enumerate(zip(late, late_got))]
    (dh0, d_norm1_g), late_chips = _nt_norm_bwd(
        dproj, w_in_parts, dh1, h0, r1, norm1_g, False, "du_norm1_b", tasks=[_RsChips(p) for p in late_pair],
        tiles=(n_first, n_tiles - n_first), earlier=(dh0_a, d_norm1_a))
    grad_x = dh0[n_meta:T][None]

    pair_sums = [dw2_pair, dw1_pair, dwout_pair] + late_pair[1:]
    chip_sums = [dw2_chips, dw1_chips, dwout_chips] + [c[0] for c in late_chips[1:]]
    big = {}
    names = ["mlp_w2", "mlp_w1", "w_out", "pool_w", "gate_a_w", "gate_x_w"]
    trip = {"mlp_w2": (mlp_w2, m_mlp_w2, v_mlp_w2), "mlp_w1": (mlp_w1, m_mlp_w1, v_mlp_w1),
            "w_out": (w_out, m_w_out, v_w_out),
            "pool_w": (pool_w, m_pool_w, v_pool_w), "gate_a_w": (gate_a_w, m_gate_a_w, v_gate_a_w),
            "gate_x_w": (gate_x_w, m_gate_x_w, v_gate_x_w)}
    for k, nm in enumerate(names):
        w_, m_, v_ = trip[nm]
        shape2 = pair_sums[k].shape[1:]
        outs = _reduce_update(pair_sums[k], chip_sums[k], w_.reshape(shape2), m_.reshape(shape2), v_.reshape(shape2),
                              chip_slot, "update_" + nm)
        big[nm] = [o.reshape(w_.shape) for o in outs]
    win2 = [a_[0] for a_ in (w_in, m_w_in, v_w_in)]
    win_a = _reduce_update(dwin_a_pair, dwin_a_chips, *win2, chip_slot, "update_w_in_a", part=(0, 2))
    win_b = _reduce_update(late_pair[0], late_chips[0][0], *win2, chip_slot, "update_w_in_b", part=(1, 2), earlier=win_a)
    big["w_in"] = [o.reshape(w_in.shape) for o in win_b]
    names = names + ["w_in"]

    lru_rows = lru_small.transpose(1, 0, 2).reshape(LRU_SMALL_ROWS, D)
    small_part = jnp.concatenate(
        [dh0[:n_meta], d_norm1_g, d_scale, d_norm2_g, d_final_g, lru_rows, jnp.zeros((4, D), F32)], axis=0)
    (small_all,) = _all_gather([small_part], "gather_small_grads")
    small_sum = _sum_slots(small_all, "sum_small_grads")
    o = n_meta
    dcol, hcol = D // N_DEV, hd // N_DEV
    bias_grads = jnp.concatenate(
        [lax.dynamic_slice_in_dim(small_sum[o + 4 + k].reshape(H, hd), me * hcol, hcol, axis=1) for k in range(2)], axis=0)
    col0 = jnp.reshape(me * dcol, (1,)).astype(jnp.int32)
    small = {"meta_tokens": (meta_tokens, m_meta_tokens, v_meta_tokens, "cols", 0),
             "norm1_g": (norm1_g, m_norm1_g, v_norm1_g, "rows", o),
             "pool_scale": (pool_scale, m_pool_scale, v_pool_scale, "rows", o + 1),
             "norm2_g": (norm2_g, m_norm2_g, v_norm2_g, "rows", o + 2),
             "final_g": (final_g, m_final_g, v_final_g, "rows", o + 3),
             "gate_a_b": (gate_a_b, m_gate_a_b, v_gate_a_b, "bias", 0),
             "gate_x_b": (gate_x_b, m_gate_x_b, v_gate_x_b, "bias", H),
             "lru_lambda": (lru_lambda, m_lru_lambda, v_lru_lambda, "rows", o + 6),
             "conv_b": (conv_b, m_conv_b, v_conv_b, "rows", o + 7),
             "conv_w": (conv_w, m_conv_w, v_conv_w, "cols", o + 8)}

    def two_d(a):
        return a.reshape(-1, a.shape[-1])

    small_out = _small_updates(
        small_sum, bias_grads, col0,
        [(two_d(w_), two_d(m_), two_d(v_), kind, row) for (w_, m_, v_, kind, row) in small.values()], "update_small")
    small_res = {nm: [r.reshape(spec[0].shape) for r in res] for (nm, spec), res in zip(small.items(), small_out)}

    def leaves(kind):
        out = {nm: res[kind] for nm, res in small_res.items()}
        for nm in names:
            out[nm] = big[nm][kind]
        order = ["meta_tokens", "norm1_g", "w_in", "pool_w", "pool_scale", "conv_w", "conv_b", "gate_a_w", "gate_a_b",
                 "gate_x_w", "gate_x_b", "lru_lambda", "w_out", "norm2_g", "mlp_w1", "mlp_w2", "final_g"]
        return [out[nm] for nm in order]

    loss = lax.psum(loss_tile[0, 0], ("x", "y", "c"))
    return (loss, grad_x, *leaves(0), *leaves(1), *leaves(2), *leaves(3))
```

```python
import jax
import jax.numpy as jnp
from jax import lax
from jax.experimental import pallas as pl
from jax.experimental.pallas import tpu as pltpu

F32 = jnp.float32
BF16 = jnp.bfloat16
MESH = pl.DeviceIdType.MESH
N_DEV = 8
POOL_WINDOWS = (2, 4, 8, 16)
MAX_WINDOW = 16
CONV_WIDTH = 4
HALO = 8
LRU_C = 8.0
NORM_EPS = 1e-6
ADAM_LR, ADAM_B1, ADAM_B2, ADAM_EPS, ADAM_WD, ADAM_STEP = 0.001, 0.9, 0.999, 1e-08, 0.01, 10
ROW_ALIGN = 128
VMEM_LIMIT = 56 << 20
TILE = dict(norm=384, proj=1408, pool=384, lru=704, wout=384, mlp1=1408, mlp2=704, dact=1408, tn=2112,
            nt=704, dmerged=704, update=256, pair=1024)
MLP2_K = 1024
EPILOGUE_ROWS = 176
SCAN_GROUP = 4
W_IN_CHUNKS = 5
RELAY_STEP_PERCENT = 56
MID_STEP_PERCENT = 88

_NT = (((1,), (1,)), ((), ()))
_TN = (((0,), (0,)), ((), ()))


def _call(body, **kw):
    return pl.pallas_call(body, **kw)


def _cp(*sem):
    return pltpu.CompilerParams(dimension_semantics=sem, vmem_limit_bytes=VMEM_LIMIT)


def _tile(total, pref):
    best = None
    for t in range(16, min(total, pref) + 1, 16):
        if total % t == 0:
            best = t
    assert best is not None, (total, pref)
    return best


def _sds(shape, dtype):
    return jax.ShapeDtypeStruct(shape, dtype)


def _pos():
    return lax.axis_index("x"), lax.axis_index("y"), lax.axis_index("c")


def _all_gather(shards, name):
    n = len(shards)

    def body(*refs):
        ins, outs = refs[:n], refs[n:2 * n]
        send_sems, recv_sems, local_sems = refs[2 * n:]
        x, y, c = _pos()
        me, sib = (x, y, c), (x, y, 1 - c)
        chips = [(1 - x, y), (x, 1 - y), (1 - x, 1 - y)]

        def slot(p):
            return 4 * p[0] + 2 * p[1] + p[2]

        def copy(a, k, block, to, src=None):
            dst = outs[a].at[slot(block)]
            return pltpu.make_async_remote_copy(
                src_ref=dst if src is None else src, dst_ref=dst,
                send_sem=send_sems.at[7 * a + k], recv_sem=recv_sems.at[7 * a + k],
                device_id=to, device_id_type=MESH)

        mine = [pltpu.make_async_copy(ins[a], outs[a].at[slot(me)], local_sems.at[a]) for a in range(n)]
        for m in mine:
            m.start()
        first = []
        for a in range(n):
            first.append(copy(a, 0, me, sib, src=ins[a]))
            first += [copy(a, 1 + j, me, (*chip, c), src=ins[a]) for j, chip in enumerate(chips)]
        for cp in first:
            cp.start()
        passed = []
        for a in range(n):
            for j, chip in enumerate(chips):
                copy(a, 1 + j, (*chip, c), me).wait_recv()
                fwd = copy(a, 4 + j, (*chip, c), sib)
                fwd.start()
                passed.append(fwd)
        for a in range(n):
            copy(a, 0, sib, me).wait_recv()
            for j, chip in enumerate(chips):
                copy(a, 4 + j, (*chip, 1 - c), me).wait_recv()
        for cp in first + passed:
            cp.wait_send()
        for m in mine:
            m.wait()

    hbm = pl.BlockSpec(memory_space=pl.ANY)
    return _call(
        body, name=name,
        out_shape=[_sds((N_DEV,) + s.shape, s.dtype) for s in shards],
        in_specs=[hbm] * n, out_specs=[hbm] * n,
        scratch_shapes=[pltpu.SemaphoreType.DMA((7 * n,)), pltpu.SemaphoreType.DMA((7 * n,)),
                        pltpu.SemaphoreType.DMA((n,))],
    )(*shards)


def _all_gather_relay(shards, chunks, name):
    n = len(shards)
    units = []
    for a, s in enumerate(shards):
        if chunks[a] == 1:
            units.append((a, None, None))
        else:
            w = s.shape[-1] // chunks[a]
            units += [(a, q * w, w) for q in range(chunks[a])]
    nu = len(units)

    def body(*refs):
        ins, outs = refs[:n], refs[n:2 * n]
        send_sems, recv_sems, local_sems = refs[2 * n:]
        x, y, c = _pos()
        me, sib = (x, y, c), (x, y, 1 - c)
        x_nbr, y_nbr, diag = (1 - x, y, c), (x, 1 - y, c), (1 - x, 1 - y, c)
        came_from = (x + (1 - c) * (1 - 2 * x), y + c * (1 - 2 * y), c)
        pass_to = (x + c * (1 - 2 * x), y + (1 - c) * (1 - 2 * y), c)

        def slot(p):
            return 4 * p[0] + 2 * p[1] + p[2]

        def src_view(u):
            a, c0, w = units[u]
            return ins[a] if c0 is None else ins[a].at[:, pl.ds(c0, w)]

        def dst_view(u, p):
            a, c0, w = units[u]
            return outs[a].at[slot(p)] if c0 is None else outs[a].at[slot(p), :, pl.ds(c0, w)]

        def copy(u, k, block, to, from_shard=False):
            dst = dst_view(u, block)
            return pltpu.make_async_remote_copy(
                src_ref=src_view(u) if from_shard else dst, dst_ref=dst,
                send_sem=send_sems.at[7 * u + k], recv_sem=recv_sems.at[7 * u + k],
                device_id=to, device_id_type=MESH)

        mine = [pltpu.make_async_copy(src_view(u), dst_view(u, me), local_sems.at[u]) for u in range(nu)]
        for m in mine:
            m.start()
        sent = []
        for u in range(nu):
            sent += [copy(u, 0, me, sib, True), copy(u, 1, me, x_nbr, True), copy(u, 2, me, y_nbr, True)]
        for cp in sent:
            cp.start()
        for u in range(nu):
            copy(u, 1, x_nbr, me).wait_recv()
            copy(u, 2, y_nbr, me).wait_recv()
            sent += [copy(u, 3, came_from, pass_to), copy(u, 4, x_nbr, sib), copy(u, 5, y_nbr, sib)]
            for cp in sent[-3:]:
                cp.start()
        for u in range(nu):
            copy(u, 3, diag, me).wait_recv()
            sent.append(copy(u, 6, diag, sib))
            sent[-1].start()
        for u in range(nu):
            copy(u, 0, sib, me).wait_recv()
            for k, p in ((4, x_nbr), (5, y_nbr), (6, diag)):
                copy(u, k, (p[0], p[1], 1 - c), me).wait_recv()
        for cp in sent:
            cp.wait_send()
        for m in mine:
            m.wait()

    hbm = pl.BlockSpec(memory_space=pl.ANY)
    return _call(
        body, name=name,
        out_shape=[_sds((N_DEV,) + s.shape, s.dtype) for s in shards],
        in_specs=[hbm] * n, out_specs=[hbm] * n,
        scratch_shapes=[pltpu.SemaphoreType.DMA((7 * nu,)), pltpu.SemaphoreType.DMA((7 * nu,)),
                        pltpu.SemaphoreType.DMA((nu,))],
    )(*shards)


def _other_chips(x, y):
    return [(1 - x, y), (x, 1 - y), (1 - x, 1 - y)]


class _AgFull:
    n_sem, n_local = 7, 1

    def __init__(self, shard, forward_here=True):
        self.forward_here = forward_here
        self.ins = [shard]
        self.out_shapes = [_sds((N_DEV,) + shard.shape, shard.dtype)]
        self.aliases = []

    def _places(self):
        x, y, c = _pos()
        came_from = (x + (1 - c) * (1 - 2 * x), y + c * (1 - 2 * y), c)
        pass_to = (x + c * (1 - 2 * x), y + (1 - c) * (1 - 2 * y), c)
        return (x, y, 1 - c), (1 - x, y, c), (x, 1 - y, c), (1 - x, 1 - y, c), came_from, pass_to, c

    def _copy(self, outs, sems, k, block, to, src=None):
        send, recv, _, base, _ = sems
        dst = outs[0].at[4 * block[0] + 2 * block[1] + block[2]]
        return pltpu.make_async_remote_copy(src_ref=dst if src is None else src, dst_ref=dst, send_sem=send.at[base + k],
                                            recv_sem=recv.at[base + k], device_id=to, device_id_type=MESH)

    def _sends(self, ins, outs, sems):
        sib, x_nbr, y_nbr = self._places()[:3]
        x, y, c = _pos()
        return [self._copy(outs, sems, k, (x, y, c), p, src=ins[0]) for k, p in enumerate((sib, x_nbr, y_nbr))]

    def _own(self, ins, outs, sems):
        x, y, c = _pos()
        return pltpu.make_async_copy(ins[0], outs[0].at[4 * x + 2 * y + c], sems[2].at[sems[4]])

    def _forwards(self, outs, sems, core_of_block):
        return _forward_copies(outs[0], sems[0], sems[1], sems[3] + 4, core_of_block)

    def start(self, ins, outs, sems):
        self._own(ins, outs, sems).start()
        for cp in self._sends(ins, outs, sems):
            cp.start()

    def relay(self, ins, outs, sems):
        _, _, _, _, came_from, pass_to, c = self._places()
        self._copy(outs, sems, 1 + c, came_from, came_from).wait_recv()
        self._copy(outs, sems, 3, came_from, pass_to).start()

    def _late_arrivals(self, outs, sems):
        _, _, _, diag, came_from, pass_to, c = self._places()
        return [self._copy(outs, sems, 2 - c, pass_to, pass_to), self._copy(outs, sems, 3, diag, diag)]

    def mid(self, ins, outs, sems):
        if self.forward_here:
            for cp in self._late_arrivals(outs, sems):
                cp.wait_recv()
            for cp in self._forwards(outs, sems, "mine"):
                cp.start()

    def finish(self, ins, outs, sems):
        sib, _, _, _, came_from, pass_to, c = self._places()
        self._copy(outs, sems, 0, sib, sib).wait_recv()
        if self.forward_here:
            for cp in self._forwards(outs, sems, "sibling"):
                cp.wait_recv()
            for cp in self._forwards(outs, sems, "mine"):
                cp.wait_send()
        else:
            for cp in self._late_arrivals(outs, sems):
                cp.wait_recv()
        for cp in self._sends(ins, outs, sems) + [self._copy(outs, sems, 3, came_from, pass_to)]:
            cp.wait_send()
        self._own(ins, outs, sems).wait()


def _forward_copies(gathered_ref, send, recv, base, core_of_block):
    x, y, c = _pos()
    res = []
    for k, chip in enumerate(_other_chips(x, y)):
        blk = gathered_ref.at[4 * chip[0] + 2 * chip[1] + (c if core_of_block == "mine" else 1 - c)]
        res.append(pltpu.make_async_remote_copy(src_ref=blk, dst_ref=blk, send_sem=send.at[base + k],
                                                recv_sem=recv.at[base + k], device_id=(x, y, 1 - c),
                                                device_id_type=MESH))
    return res


class _AgForward:
    n_sem, n_local = 3, 0

    def __init__(self, gathered):
        self.ins = [gathered]
        self.out_shapes = [_sds(gathered.shape, gathered.dtype)]
        self.aliases = [(0, 0)]

    def start(self, ins, outs, sems):
        for cp in _forward_copies(outs[0], sems[0], sems[1], sems[3], "mine"):
            cp.start()

    def finish(self, ins, outs, sems):
        for cp in _forward_copies(outs[0], sems[0], sems[1], sems[3], "sibling"):
            cp.wait_recv()
        for cp in _forward_copies(outs[0], sems[0], sems[1], sems[3], "mine"):
            cp.wait_send()


class _RsSibling:
    n_sem, n_local = 4, 0

    def __init__(self, part):
        self.ins = [part]
        self.out_shapes = [_sds((4,) + part.shape[1:], part.dtype)]
        self.aliases = []

    def _copies(self, ins, outs, sems):
        send, recv, _, base, _ = sems
        x, y, c = _pos()
        return [pltpu.make_async_remote_copy(src_ref=ins[0].at[2 * q + (1 - c)], dst_ref=outs[0].at[q],
                                             send_sem=send.at[base + q], recv_sem=recv.at[base + q],
                                             device_id=(x, y, 1 - c), device_id_type=MESH) for q in range(4)]

    def start(self, ins, outs, sems):
        for cp in self._copies(ins, outs, sems):
            cp.start()

    def finish(self, ins, outs, sems):
        for cp in self._copies(ins, outs, sems):
            cp.wait()


class _RsChips:
    n_sem, n_local = 3, 0

    def __init__(self, pair):
        self.ins = [pair]
        self.out_shapes = [_sds((3,) + pair.shape[1:], pair.dtype)]
        self.aliases = []

    def _copies(self, ins, outs, sems):
        send, recv, _, base, _ = sems
        x, y, c = _pos()
        return [pltpu.make_async_remote_copy(src_ref=ins[0].at[2 * chip[0] + chip[1]], dst_ref=outs[0].at[k],
                                             send_sem=send.at[base + k], recv_sem=recv.at[base + k],
                                             device_id=(*chip, c), device_id_type=MESH)
                for k, chip in enumerate(_other_chips(x, y))]

    def start(self, ins, outs, sems):
        for cp in self._copies(ins, outs, sems):
            cp.start()

    def finish(self, ins, outs, sems):
        for cp in self._copies(ins, outs, sems):
            cp.wait()


def _hosted(body, tasks, *, grid, in_specs, out_specs, out_shape, scratch_shapes=(), name, semantics, operands,
            aliases=None):
    in_specs, out_specs, out_shape = list(in_specs), list(out_specs), list(out_shape)
    scratch_shapes = list(scratch_shapes)
    aliases = dict(aliases or {})
    if not tasks:
        res = _call(body, name=name, grid=grid, in_specs=in_specs, out_specs=out_specs, out_shape=out_shape,
                    scratch_shapes=scratch_shapes, input_output_aliases=aliases,
                    compiler_params=_cp(*semantics))(*operands)
        return list(res), []
    n_in, n_out, n_scr = len(in_specs), len(out_specs), len(scratch_shapes)
    t_ins = [a for t in tasks for a in t.ins]
    t_outs = [o for t in tasks for o in t.out_shapes]
    i0, o0 = n_in, n_out
    for t in tasks:
        for (i, o) in t.aliases:
            aliases[i0 + i] = o0 + o
        i0 += len(t.ins)
        o0 += len(t.out_shapes)
    n_sem = sum(t.n_sem for t in tasks)
    n_local = max(1, sum(t.n_local for t in tasks))
    n_steps = 1
    for g in grid:
        n_steps *= g
    mid_step = min(n_steps - 1, (n_steps * MID_STEP_PERCENT) // 100)
    relay_step = min(mid_step, (n_steps * RELAY_STEP_PERCENT) // 100)

    def wrapped(*refs):
        cut = [n_in, len(t_ins), n_out, len(t_outs), n_scr]
        parts, at = [], 0
        for n in cut:
            parts.append(refs[at:at + n])
            at += n
        ins, tin, outs, tout, scratch = parts
        send, recv, local = refs[at:]
        step = pl.program_id(0)
        for d in range(1, len(grid)):
            step = step * grid[d] + pl.program_id(d)

        def each(method):
            i, o, s, l = 0, 0, 0, 0
            for t in tasks:
                if hasattr(t, method):
                    getattr(t, method)(tin[i:i + len(t.ins)], tout[o:o + len(t.out_shapes)], (send, recv, local, s, l))
                i, o, s, l = i + len(t.ins), o + len(t.out_shapes), s + t.n_sem, l + t.n_local

        @pl.when(step == 0)
        def _():
            each("start")

        body(*ins, *outs, *scratch)

        @pl.when(step == relay_step)
        def _():
            each("relay")

        @pl.when(step == mid_step)
        def _():
            each("mid")

        @pl.when(step == n_steps - 1)
        def _():
            each("finish")

    hbm = pl.BlockSpec(memory_space=pl.ANY)
    res = _call(
        wrapped, name=name, grid=grid,
        in_specs=in_specs + [hbm] * len(t_ins), out_specs=out_specs + [hbm] * len(t_outs),
        out_shape=out_shape + t_outs,
        scratch_shapes=scratch_shapes + [pltpu.SemaphoreType.DMA((n_sem,)), pltpu.SemaphoreType.DMA((n_sem,)),
                                         pltpu.SemaphoreType.DMA((n_local,))],
        input_output_aliases=aliases,
        compiler_params=_cp(*(["arbitrary"] * len(grid))),
    )(*operands, *t_ins)
    res = list(res)
    task_outs, o = [], n_out
    for t in tasks:
        task_outs.append(res[o:o + len(t.out_shapes)])
        o += len(t.out_shapes)
    return res[:n_out], task_outs


def _pair_sum(part, got, core, name):
    _, R, C = part.shape
    tr = _tile(R, TILE["pair"]) if R % 16 == 0 else R

    def body(core_ref, p_ref, g_ref, o_ref):
        o_ref[...] = (p_ref[...].astype(F32) + g_ref[...].astype(F32)).astype(o_ref.dtype)

    return _call(
        body, name=name,
        grid_spec=pltpu.PrefetchScalarGridSpec(
            num_scalar_prefetch=1, grid=(4, R // tr),
            in_specs=[pl.BlockSpec((None, tr, C), lambda q, i, cr: (2 * q + cr[0], i, 0)),
                      pl.BlockSpec((None, tr, C), lambda q, i, cr: (q, i, 0))],
            out_specs=pl.BlockSpec((None, tr, C), lambda q, i, cr: (q, i, 0))),
        out_shape=_sds((4, R, C), part.dtype),
        compiler_params=_cp("parallel", "parallel"),
    )(core, part, got)


def _sum_slots(gathered, name):
    _, R, C = gathered.shape

    def body(g_ref, o_ref):
        acc = g_ref[0]
        for s in range(1, N_DEV):
            acc = acc + g_ref[s]
        o_ref[...] = acc

    return _call(body, name=name, out_shape=_sds((R, C), F32))(gathered)


def _sigmoid(z):
    return jax.nn.sigmoid(z)


def _softplus(z):
    e = jnp.exp(-jnp.abs(z))
    log1p_e = jnp.where(e < 0.01, e * (1.0 - e * (0.5 - e * (1.0 / 3.0))), jnp.log(1.0 + e))
    return jnp.maximum(z, 0.0) + log1p_e


_GELU_K = 0.7978845608028654
_GELU_C = 0.044715


def _gelu_and_grad(z):
    t = jnp.tanh(_GELU_K * (z + _GELU_C * z * z * z))
    g = 0.5 * z * (1.0 + t)
    dg = 0.5 * (1.0 + t) + 0.5 * z * (1.0 - t * t) * _GELU_K * (1.0 + 3.0 * _GELU_C * z * z)
    return g, dg


def _gelu(z):
    t = jnp.tanh(_GELU_K * (z + _GELU_C * z * z * z))
    return 0.5 * z * (1.0 + t)


def _row_ids(tile_index, tm, width=1):
    return tile_index * tm + lax.broadcasted_iota(jnp.int32, (tm, width), 0)


def _shift_down(prev, cur, k):
    if k == 0:
        return cur
    ext = jnp.concatenate([prev, cur], axis=0)
    return pltpu.roll(ext, k, axis=0)[prev.shape[0]:]


def _shift_up(cur, nxt, k):
    if k == 0:
        return cur
    ext = jnp.concatenate([cur, nxt], axis=0)
    return pltpu.roll(ext, ext.shape[0] - k, axis=0)[:cur.shape[0]]


def _lru_gates(r, sp):
    z = LRU_C * r * sp
    a = jnp.exp(-z)
    t = jnp.tanh(z)
    mult = jnp.sqrt(2.0 * t / (1.0 + t))
    return a, mult


def _scan_chunks(a_ref, b_ref, out_ref, carry, n_rows, reverse):
    n_chunks = n_rows // 8
    cols = a_ref.shape[1]
    rid = lax.broadcasted_iota(jnp.int32, (8, cols), 0)
    edge = 0 if reverse else 7
    group = SCAN_GROUP if n_chunks % SCAN_GROUP == 0 else 1

    def local_scan(ci):
        rows = pl.ds(pl.multiple_of(ci * 8, 8), 8)
        a = a_ref[rows, :]
        b = b_ref[rows, :]
        for s in (1, 2, 4):
            if reverse:
                keep = rid < 8 - s
                a_n, b_n = pltpu.roll(a, 8 - s, axis=0), pltpu.roll(b, 8 - s, axis=0)
            else:
                keep = rid >= s
                a_n, b_n = pltpu.roll(a, s, axis=0), pltpu.roll(b, s, axis=0)
            b = a * jnp.where(keep, b_n, 0.0) + b
            a = a * jnp.where(keep, a_n, 1.0)
        a_e = jnp.sum(jnp.where(rid == edge, a, 0.0), axis=0, keepdims=True)
        b_e = jnp.sum(jnp.where(rid == edge, b, 0.0), axis=0, keepdims=True)
        return rows, a, b, a_e, b_e

    def chunks(k, h):
        first = k * group
        scans = [local_scan((n_chunks - 1 - (first + j)) if reverse else first + j) for j in range(group)]
        for rows, a, b, a_e, b_e in scans:
            out_ref[rows, :] = a * h + b
            h = a_e * h + b_e
        return h

    return lax.fori_loop(0, n_chunks // group, chunks, carry)


def _norm_fwd(h, g, name):
    Tp, D = h.shape
    tm = _tile(Tp, TILE["norm"])

    def body(h_ref, g_ref, u_ref, r_ref):
        x = h_ref[...]
        r = lax.rsqrt(jnp.mean(x * x, axis=-1, keepdims=True) + NORM_EPS)
        u_ref[...] = (x * r * g_ref[...]).astype(BF16)
        r_ref[...] = r

    return _call(
        body, name=name, grid=(Tp // tm,),
        in_specs=[pl.BlockSpec((tm, D), lambda i: (i, 0)), pl.BlockSpec((1, D), lambda i: (0, 0))],
        out_specs=[pl.BlockSpec((tm, D), lambda i: (i, 0)), pl.BlockSpec((tm, 1), lambda i: (i, 0))],
        out_shape=[_sds((Tp, D), BF16), _sds((Tp, 1), F32)],
        compiler_params=_cp("parallel"),
    )(h, g)


def _proj_fwd(u, w_slots, name, tasks=()):
    Tp, K = u.shape
    S, _, n = w_slots.shape
    tm = _tile(Tp, TILE["proj"])

    def body(a_ref, b_ref, o_ref):
        o_ref[...] = jnp.dot(a_ref[...], b_ref[...], preferred_element_type=F32)

    (proj,), extra = _hosted(
        body, tasks, name=name, grid=(S, Tp // tm),
        in_specs=[pl.BlockSpec((tm, K), lambda j, i: (i, 0)), pl.BlockSpec((None, K, n), lambda j, i: (j, 0, 0))],
        out_specs=[pl.BlockSpec((tm, n), lambda j, i: (i, j))],
        out_shape=[_sds((Tp, S * n), F32)],
        semantics=("parallel", "parallel"), operands=(u, w_slots))
    return proj, extra


def _mlp1_fwd(u2, w_slots, name, tasks=()):
    Tp, K = u2.shape
    S, _, n = w_slots.shape
    tm = _tile(Tp, TILE["mlp1"])

    def body(a_ref, b_ref, act_ref, a1_ref):
        a1 = jnp.dot(a_ref[...], b_ref[...], preferred_element_type=F32)
        relu = jnp.maximum(a1, 0.0)
        act_ref[...] = (relu * relu).astype(BF16)
        a1_ref[...] = a1.astype(BF16)

    return _hosted(
        body, tasks, name=name, grid=(S, Tp // tm),
        in_specs=[pl.BlockSpec((tm, K), lambda j, i: (i, 0)), pl.BlockSpec((None, K, n), lambda j, i: (j, 0, 0))],
        out_specs=[pl.BlockSpec((tm, n), lambda j, i: (i, j))] * 2,
        out_shape=[_sds((Tp, S * n), BF16)] * 2,
        semantics=("parallel", "parallel"), operands=(u2, w_slots))


def _pool_fwd(proj, pool_w, name):
    Tp = proj.shape[0]
    G, Cg, _ = pool_w.shape
    D = G * Cg
    tm = _tile(Tp, TILE["pool"])

    def body(v_ref, w_ref, d_ref, y_ref, prev_ref):
        t = pl.program_id(0)

        @pl.when(t == 0)
        def _():
            prev_ref[...] = jnp.zeros_like(prev_ref)

        rows = _row_ids(t, tm)
        for g, win in enumerate(POOL_WINDOWS):
            cols = slice(g * Cg, (g + 1) * Cg)
            v = v_ref[:, cols]
            s = jnp.concatenate([prev_ref[:, cols], v], axis=0)
            k = 1
            while k < win:
                s = s + pltpu.roll(s, k, axis=0)
                k *= 2
            cnt = jnp.minimum(rows + 1, win).astype(F32)
            d = s[MAX_WINDOW:] / cnt - v
            d_ref[:, cols] = d.astype(BF16)
            y_ref[:, cols] = jnp.dot(d.astype(BF16), w_ref[g], preferred_element_type=F32)
        prev_ref[...] = v_ref[tm - MAX_WINDOW:, :]

    return _call(
        body, name=name, grid=(Tp // tm,),
        in_specs=[pl.BlockSpec((tm, D), lambda t: (t, 0)), pl.BlockSpec((G, Cg, Cg), lambda t: (0, 0, 0))],
        out_specs=[pl.BlockSpec((tm, D), lambda t: (t, 0))] * 2,
        out_shape=[_sds((Tp, D), BF16), _sds((Tp, D), F32)],
        scratch_shapes=[pltpu.VMEM((MAX_WINDOW, D), F32)],
        compiler_params=_cp("arbitrary"),
    )(proj, pool_w)


def _lru_fwd(proj, y_pool, scale, conv_w, conv_b, wa, ba, wx, bx, lam, name, tasks=()):
    Tp = proj.shape[0]
    H, hd, _ = wa.shape
    D = H * hd
    tm = _tile(Tp, TILE["lru"])
    nb = D // hd

    def body(vl_ref, vg_ref, gp_ref, gl_ref, y_ref, sc_ref, cw_ref, cb_ref, wa_ref, ba_ref, wx_ref, bx_ref,
             lam_ref, xc_ref, r_ref, i_ref, a_ref, mult_ref, hs_ref, m_ref, prev_ref, carry_ref, b_s):
        t = pl.program_id(1)

        @pl.when(t == 0)
        def _():
            prev_ref[...] = jnp.zeros_like(prev_ref)
            carry_ref[...] = jnp.zeros_like(carry_ref)

        v = vl_ref[...]
        prev = prev_ref[...]
        xc = jnp.zeros_like(v) + cb_ref[...]
        for k in range(CONV_WIDTH):
            xc = xc + cw_ref[k:k + 1, :] * _shift_down(prev, v, CONV_WIDTH - 1 - k)
        prev_ref[...] = v[tm - HALO:, :]
        xcb = xc.astype(BF16)
        r = _sigmoid(jnp.dot(xcb, wa_ref[...], preferred_element_type=F32) + ba_ref[...])
        i = _sigmoid(jnp.dot(xcb, wx_ref[...], preferred_element_type=F32) + bx_ref[...])
        a, mult = _lru_gates(r, _softplus(-lam_ref[...]))
        a_ref[...] = a
        mult_ref[...] = mult
        b_s[...] = mult * (i * xc)
        xc_ref[...] = xc
        r_ref[...] = r
        i_ref[...] = i
        carry_ref[0:1, :] = _scan_chunks(a_ref, b_s, hs_ref, carry_ref[0:1, :], tm, reverse=False)
        lru_out = hs_ref[...] * _gelu(vg_ref[...])
        pool_out = y_ref[...] * sc_ref[...]
        m_ref[...] = (_sigmoid(gp_ref[...]) * pool_out + _sigmoid(gl_ref[...]) * lru_out).astype(BF16)

    def piece(p):
        return pl.BlockSpec((tm, hd), lambda h, t: (t, p * nb + h))

    blk = pl.BlockSpec((tm, hd), lambda h, t: (t, h))
    vec = pl.BlockSpec((1, hd), lambda h, t: (0, h))
    mat = pl.BlockSpec((None, hd, hd), lambda h, t: (h, 0, 0))
    bias = pl.BlockSpec((None, 1, hd), lambda h, t: (h, 0, 0))
    return _hosted(
        body, tasks, name=name, grid=(H, Tp // tm),
        in_specs=[piece(1), piece(2), piece(3), piece(4), blk, vec,
                  pl.BlockSpec((CONV_WIDTH, hd), lambda h, t: (0, h)), vec, mat, bias, mat, bias, vec],
        out_specs=[blk] * 7,
        out_shape=[_sds((Tp, D), F32)] * 6 + [_sds((Tp, D), BF16)],
        scratch_shapes=[pltpu.VMEM((HALO, hd), F32), pltpu.VMEM((8, hd), F32), pltpu.VMEM((tm, hd), F32)],
        semantics=("parallel", "arbitrary"),
        operands=(proj, proj, proj, proj, y_pool, scale, conv_w, conv_b, wa, ba.reshape(H, 1, hd), wx,
                  bx.reshape(H, 1, hd), lam))


def _wout_norm_fwd(merged, w_out, h0, g2, name, tasks=()):
    Tp, D = h0.shape
    tm = _tile(Tp, TILE["wout"])

    def body(m_ref, w_ref, h0_ref, g_ref, h1_ref, u2_ref, r2_ref):
        h1 = h0_ref[...] + jnp.dot(m_ref[...], w_ref[...], preferred_element_type=F32)
        r = lax.rsqrt(jnp.mean(h1 * h1, axis=-1, keepdims=True) + NORM_EPS)
        h1_ref[...] = h1
        u2_ref[...] = (h1 * r * g_ref[...]).astype(BF16)
        r2_ref[...] = r

    row = pl.BlockSpec((tm, D), lambda i: (i, 0))
    return _hosted(
        body, tasks, name=name, grid=(Tp // tm,),
        in_specs=[row, pl.BlockSpec((D, D), lambda i: (0, 0)), row, pl.BlockSpec((1, D), lambda i: (0, 0))],
        out_specs=[row, row, pl.BlockSpec((tm, 1), lambda i: (i, 0))],
        out_shape=[_sds((Tp, D), F32), _sds((Tp, D), BF16), _sds((Tp, 1), F32)],
        semantics=("parallel",), operands=(merged, w_out, h0, g2))


def _mlp2_loss(act, w2, h1, target, gf, n_meta, seq, name):
    Tp, D = h1.shape
    K = act.shape[1]
    tm = _tile(Tp, TILE["mlp2"])
    tk = min(K, MLP2_K)
    nk = K // tk

    rc = _tile(tm, EPILOGUE_ROWS)
    nt = Tp // tm
    last_rows = n_meta + seq - (nt - 1) * tm
    assert nt >= 2 and n_meta % 8 == 0 and 0 < last_rows <= tm and last_rows % 8 == 0

    def body(a_ref, w_ref, h1_hbm, t_hbm, g_ref, dh_ref, dhb_ref, loss_ref, dg_ref, h1_buf, t_buf, sems):
        i, k = pl.program_id(0), pl.program_id(1)
        tile_rows = pl.ds(pl.multiple_of(i * tm, tm), tm)
        fetch_h1 = pltpu.make_async_copy(h1_hbm.at[tile_rows, :], h1_buf, sems.at[0])
        fetch_t = [
            (i == 0, pltpu.make_async_copy(t_hbm.at[pl.ds(0, tm - n_meta), :], t_buf.at[pl.ds(n_meta, tm - n_meta), :],
                                           sems.at[1])),
            ((i > 0) & (i < nt - 1),
             pltpu.make_async_copy(t_hbm.at[pl.ds(pl.multiple_of(i * tm - n_meta, 8), tm), :], t_buf, sems.at[1])),
            (i == nt - 1, pltpu.make_async_copy(t_hbm.at[pl.ds((nt - 1) * tm - n_meta, last_rows), :],
                                                t_buf.at[pl.ds(0, last_rows), :], sems.at[1])),
        ]

        @pl.when(k == 0)
        def _():
            fetch_h1.start()
            dh_ref[...] = jnp.zeros_like(dh_ref)

        for cond, f in fetch_t:
            @pl.when(cond & (k == 0))
            def _(f=f):
                f.start()

        @pl.when((i == 0) & (k == 0))
        def _():
            loss_ref[...] = jnp.zeros_like(loss_ref)
            dg_ref[...] = jnp.zeros_like(dg_ref)
            t_buf[0:n_meta, :] = jnp.zeros((n_meta, D), F32)

        dh_ref[...] += jnp.dot(a_ref[...], w_ref[...], preferred_element_type=F32)

        for cond, f in fetch_t:
            @pl.when(cond & (k == nk - 1))
            def _(f=f):
                f.wait()

        @pl.when(k == nk - 1)
        def _():
            fetch_h1.wait()
            g = g_ref[...]

            def chunk(c, carry):
                loss_acc, dg_acc = carry
                rows = pl.ds(pl.multiple_of(c * rc, rc), rc)
                h2 = h1_buf[rows, :] + dh_ref[rows, :]
                r = lax.rsqrt(jnp.mean(h2 * h2, axis=-1, keepdims=True) + NORM_EPS)
                out = h2 * r * g
                row_id = i * tm + c * rc + lax.broadcasted_iota(jnp.int32, (rc, 1), 0)
                valid = (row_id >= n_meta) & (row_id < n_meta + seq)
                diff = jnp.where(valid, out - t_buf[rows, :], 0.0)
                dout = diff / D
                dog = dout * g
                dh = r * dog - h2 * (r * r * r * jnp.mean(dog * h2, axis=-1, keepdims=True))
                dh_ref[rows, :] = dh
                dhb_ref[rows, :] = dh.astype(BF16)
                loss_acc = loss_acc + 0.5 * jnp.sum(jnp.mean(diff * diff, axis=-1, keepdims=True), axis=0, keepdims=True)
                return loss_acc, dg_acc + jnp.sum(dout * (h2 * r), axis=0, keepdims=True)

            loss_sum, dg_sum = lax.fori_loop(0, tm // rc, chunk, (jnp.zeros((1, 1), F32), jnp.zeros((1, D), F32)))
            loss_ref[...] += loss_sum
            dg_ref[...] += dg_sum

    row = pl.BlockSpec((tm, D), lambda i, k: (i, 0))
    hbm = pl.BlockSpec(memory_space=pl.ANY)
    return _call(
        body, name=name, grid=(Tp // tm, nk),
        in_specs=[pl.BlockSpec((tm, tk), lambda i, k: (i, k)), pl.BlockSpec((tk, D), lambda i, k: (k, 0)),
                  hbm, hbm, pl.BlockSpec((1, D), lambda i, k: (0, 0))],
        out_specs=[row, row, pl.BlockSpec((8, 128), lambda i, k: (0, 0)), pl.BlockSpec((1, D), lambda i, k: (0, 0))],
        out_shape=[_sds((Tp, D), F32), _sds((Tp, D), BF16), _sds((8, 128), F32), _sds((1, D), F32)],
        scratch_shapes=[pltpu.VMEM((tm, D), F32), pltpu.VMEM((tm, D), F32), pltpu.SemaphoreType.DMA((2,))],
        compiler_params=_cp("arbitrary", "arbitrary"),
    )(act, w2, h1, target, gf)


def _dact_bwd(dh2b, w2_slots, a1, name):
    Tp, D = dh2b.shape
    S, n, _ = w2_slots.shape
    tm = _tile(Tp, TILE["dact"])

    def body(g_ref, w_ref, a1_ref, o_ref):
        dact = lax.dot_general(g_ref[...], w_ref[...], _NT, preferred_element_type=F32)
        o_ref[...] = (dact * (2.0 * jnp.maximum(a1_ref[...].astype(F32), 0.0))).astype(BF16)

    return _call(
        body, name=name, grid=(S, Tp // tm),
        in_specs=[pl.BlockSpec((tm, D), lambda j, i: (i, 0)), pl.BlockSpec((None, n, D), lambda j, i: (j, 0, 0)),
                  pl.BlockSpec((tm, n), lambda j, i: (i, j))],
        out_specs=pl.BlockSpec((tm, n), lambda j, i: (i, j)),
        out_shape=_sds((Tp, S * n), BF16),
        compiler_params=_cp("parallel", "parallel"),
    )(dh2b, w2_slots, a1)


def _weight_grad(a, g, blocks, block_a, name, tasks=(), part=(0, 1)):
    Tp, Ka = a.shape
    Ng = g.shape[1]
    p, parts = part
    assert parts == 1 or not block_a
    ka = Ka // blocks if block_a else Ka // parts
    ng = Ng if block_a else Ng // blocks
    tt = _tile(Tp, TILE["tn"])
    nt = Tp // tt

    def body(a_ref, g_ref, o_ref, acc_ref):
        t = pl.program_id(1)

        @pl.when(t == 0)
        def _():
            acc_ref[...] = jnp.zeros_like(acc_ref)

        acc_ref[...] += lax.dot_general(a_ref[...], g_ref[...], _TN, preferred_element_type=F32)

        @pl.when(t == nt - 1)
        def _():
            o_ref[...] = acc_ref[...].astype(o_ref.dtype)

    if block_a:
        a_spec = pl.BlockSpec((tt, ka), lambda j, t: (t, j))
        g_spec = pl.BlockSpec((tt, ng), lambda j, t: (t, 0))
    else:
        a_spec = pl.BlockSpec((tt, ka), lambda j, t: (t, p))
        g_spec = pl.BlockSpec((tt, ng), lambda j, t: (t, j))
    (dw,), extra = _hosted(
        body, tasks, name=name, grid=(blocks, nt),
        in_specs=[a_spec, g_spec],
        out_specs=[pl.BlockSpec((None, ka, ng), lambda j, t: (j, 0, 0))],
        out_shape=[_sds((blocks, ka, ng), BF16)],
        scratch_shapes=[pltpu.VMEM((ka, ng), F32)],
        semantics=("parallel", "arbitrary"), operands=(a, g))
    return dw, extra


def _nt_norm_bwd(dz, w_parts, dres, hin, rin, g, want_bf16, name, tasks=(), tiles=None, earlier=None):
    Tp, D = hin.shape
    P = len(w_parts)
    S, _, n = w_parts[0].shape
    K = S * P
    tm = _tile(Tp, TILE["nt"])
    t0, nt = tiles if tiles is not None else (0, Tp // tm)
    assert not (want_bf16 and earlier is not None)

    rc = _tile(tm, EPILOGUE_ROWS)

    def body(dz_ref, *rest):
        w_refs, (dres_hbm, h_hbm, r_ref, g_ref), rest = rest[:P], rest[P:P + 4], rest[P + 4:]
        if earlier is not None:
            _, dg0_ref, dh_ref, dg_ref, dres_buf, h_buf, sems = rest
        elif want_bf16:
            dh_ref, dhb_ref, dg_ref, dres_buf, h_buf, sems = rest
        else:
            dh_ref, dg_ref, dres_buf, h_buf, sems = rest
        i, k = pl.program_id(0), pl.program_id(1)
        tile_rows = pl.ds(pl.multiple_of((t0 + i) * tm, tm), tm)
        fetch = [pltpu.make_async_copy(dres_hbm.at[tile_rows, :], dres_buf, sems.at[0]),
                 pltpu.make_async_copy(h_hbm.at[tile_rows, :], h_buf, sems.at[1])]

        @pl.when(k == 0)
        def _():
            for f in fetch:
                f.start()
            dh_ref[...] = jnp.zeros_like(dh_ref)

        @pl.when((i == 0) & (k == 0))
        def _():
            dg_ref[...] = jnp.zeros_like(dg_ref) if earlier is None else dg0_ref[...]

        for q in range(P):
            @pl.when(k % P == q)
            def _(q=q):
                dh_ref[...] += lax.dot_general(dz_ref[...], w_refs[q][...], _NT, preferred_element_type=F32)

        @pl.when(k == K - 1)
        def _():
            for f in fetch:
                f.wait()
            g = g_ref[...]

            def chunk(c, dg_acc):
                rows = pl.ds(pl.multiple_of(c * rc, rc), rc)
                du = dh_ref[rows, :]
                h = h_buf[rows, :]
                r = r_ref[rows, :]
                dug = du * g
                dh = dres_buf[rows, :] + r * dug - h * (r * r * r * jnp.mean(dug * h, axis=-1, keepdims=True))
                dh_ref[rows, :] = dh
                if want_bf16:
                    dhb_ref[rows, :] = dh.astype(BF16)
                return dg_acc + jnp.sum(du * (h * r), axis=0, keepdims=True)

            dg_ref[...] += lax.fori_loop(0, tm // rc, chunk, jnp.zeros((1, D), F32))

    row = pl.BlockSpec((tm, D), lambda i, k: (t0 + i, 0))
    vec = pl.BlockSpec((1, D), lambda i, k: (0, 0))
    hbm = pl.BlockSpec(memory_space=pl.ANY)
    out_specs = [row] + ([row] if want_bf16 else []) + [vec]
    out_shape = [_sds((Tp, D), F32)] + ([_sds((Tp, D), BF16)] if want_bf16 else []) + [_sds((1, D), F32)]
    in_specs = ([pl.BlockSpec((tm, n), lambda i, k: (t0 + i, k))]
                + [pl.BlockSpec((None, D, n), lambda i, k: (k // P, 0, 0))] * P
                + [hbm, hbm, pl.BlockSpec((tm, 1), lambda i, k: (t0 + i, 0)), vec])
    operands = (dz, *w_parts, dres, hin, rin, g)
    aliases = {}
    if earlier is not None:
        in_specs += [hbm, vec]
        operands += tuple(earlier)
        aliases = {P + 5: 0}
    return _hosted(
        body, tasks, name=name, grid=(nt, K), in_specs=in_specs, out_specs=out_specs, out_shape=out_shape,
        scratch_shapes=[pltpu.VMEM((tm, D), F32), pltpu.VMEM((tm, D), F32), pltpu.SemaphoreType.DMA((2,))],
        semantics=("arbitrary", "arbitrary"), operands=operands, aliases=aliases)


def _dmerged_bwd(dh1b, w_out, name):
    Tp, D = dh1b.shape
    tm = _tile(Tp, TILE["dmerged"])

    def body(g_ref, w_ref, o_ref):
        o_ref[...] = lax.dot_general(g_ref[...], w_ref[...], _NT, preferred_element_type=F32)

    row = pl.BlockSpec((tm, D), lambda i: (i, 0))
    return _call(
        body, name=name, grid=(Tp // tm,),
        in_specs=[row, pl.BlockSpec((D, D), lambda i: (0, 0))],
        out_specs=row, out_shape=_sds((Tp, D), F32),
        compiler_params=_cp("parallel"),
    )(dh1b, w_out)


def _pool_bwd(dmerged, proj, y_pool, d_pool, scale, pool_w, name, tasks=()):
    Tp, D = dmerged.shape
    G, Cg, _ = pool_w.shape
    tm = _tile(Tp, TILE["pool"])
    nt = Tp // tm

    def body(dm_ref, gp_ref, y_ref, d_ref, sc_ref, w_ref, dproj_hbm, dw_ref, dsc_ref, next_ref, out_buf, out_sems):
        t = pl.program_id(0)
        tile = nt - 1 - t
        slot = t % 2
        dv_ref, dgp_ref = out_buf.at[slot, 0], out_buf.at[slot, 1]
        tile_rows = pl.ds(pl.multiple_of(tile * tm, tm), tm)

        def out_copies(s):
            return [pltpu.make_async_copy(out_buf.at[s, k], dproj_hbm.at[tile_rows, pl.ds(piece * D, D)],
                                          out_sems.at[2 * s + k]) for k, piece in enumerate((0, 3))]

        @pl.when(t >= 2)
        def _():
            for cp in out_copies(slot):
                cp.wait()

        @pl.when(t == 0)
        def _():
            next_ref[...] = jnp.zeros_like(next_ref)
            dw_ref[...] = jnp.zeros_like(dw_ref)
            dsc_ref[...] = jnp.zeros_like(dsc_ref)

        rows = _row_ids(tile, tm)
        dm = dm_ref[...]
        y = y_ref[...]
        sc = sc_ref[...]
        sg = _sigmoid(gp_ref[...])
        dpo = dm * sg
        dgp_ref[...] = (dm * (y * sc) * sg * (1.0 - sg)).astype(BF16)
        dsc_ref[...] += jnp.sum(dpo * y, axis=0, keepdims=True)
        dyb = (dpo * sc).astype(BF16)
        for g, win in enumerate(POOL_WINDOWS):
            cols = slice(g * Cg, (g + 1) * Cg)
            dy = dyb[:, cols]
            dd = lax.dot_general(dy, w_ref[g], _NT, preferred_element_type=F32)
            dw_ref[g] += lax.dot_general(d_ref[:, cols], dy, _TN, preferred_element_type=F32)
            q = dd / jnp.minimum(rows + 1, win).astype(F32)
            s = jnp.concatenate([q, next_ref[:, cols]], axis=0)
            k = 1
            while k < win:
                s = s + pltpu.roll(s, s.shape[0] - k, axis=0)
                k *= 2
            dv_ref[:, cols] = (s[:tm] - dd).astype(BF16)
            next_ref[:, cols] = q[:MAX_WINDOW]
        for cp in out_copies(slot):
            cp.start()

        @pl.when(t == nt - 1)
        def _():
            for cp in out_copies(slot) + (out_copies(1 - slot) if nt > 1 else []):
                cp.wait()

    row = pl.BlockSpec((tm, D), lambda t: (nt - 1 - t, 0))
    return _hosted(
        body, tasks, name=name, grid=(nt,),
        in_specs=[row, pl.BlockSpec((tm, D), lambda t: (nt - 1 - t, 3)), row, row,
                  pl.BlockSpec((1, D), lambda t: (0, 0)), pl.BlockSpec((G, Cg, Cg), lambda t: (0, 0, 0))],
        out_specs=[pl.BlockSpec(memory_space=pl.ANY), pl.BlockSpec((G, Cg, Cg), lambda t: (0, 0, 0)),
                   pl.BlockSpec((1, D), lambda t: (0, 0))],
        out_shape=[_sds((Tp, proj.shape[1]), BF16), _sds((G, Cg, Cg), F32), _sds((1, D), F32)],
        scratch_shapes=[pltpu.VMEM((MAX_WINDOW, D), F32), pltpu.VMEM((2, 2, tm, D), BF16),
                        pltpu.SemaphoreType.DMA((4,))],
        semantics=("arbitrary",), operands=(dmerged, proj, y_pool, d_pool, scale, pool_w))


LRU_SMALL_ROWS = 8


def _lru_bwd(dmerged, proj, xc, r_gate, i_gate, a_gate, mult_gate, hs, lam, conv_w, wa, wx, dproj, name, tasks=()):
    Tp, D = dmerged.shape
    H, hd, _ = wa.shape
    tm = _tile(Tp, TILE["lru"])
    nt = Tp // tm
    nb = D // hd
    halo_blocks = tm // HALO

    def body(dm_ref, vl_ref, vg_ref, gl_ref, xc_ref, r_ref, i_ref, a_ref, mult_ref, hs_ref, hsp_ref, lam_ref, cw_ref,
             wa_ref, wx_ref, _, dproj_hbm, dwa_ref, dwx_ref, small_ref,
             mu_next_ref, dxc_next_ref, q_s, mu_s, out_buf, out_sems):
        h_id, t = pl.program_id(0), pl.program_id(1)
        tile = nt - 1 - t
        step = h_id * nt + t
        slot = step % 2
        dvl_ref, dvg_ref, dgl_ref = out_buf.at[slot, 0], out_buf.at[slot, 1], out_buf.at[slot, 2]
        tile_rows = pl.ds(pl.multiple_of(tile * tm, tm), tm)

        def out_copies(s):
            return [pltpu.make_async_copy(
                out_buf.at[s, k], dproj_hbm.at[tile_rows, pl.ds(pl.multiple_of((piece * nb + h_id) * hd, hd), hd)],
                out_sems.at[3 * s + k]) for k, piece in enumerate((1, 2, 4))]

        @pl.when(step >= 2)
        def _():
            for cp in out_copies(slot):
                cp.wait()

        @pl.when(t == 0)
        def _():
            mu_next_ref[...] = jnp.zeros_like(mu_next_ref)
            dxc_next_ref[...] = jnp.zeros_like(dxc_next_ref)
            dwa_ref[...] = jnp.zeros_like(dwa_ref)
            dwx_ref[...] = jnp.zeros_like(dwx_ref)
            small_ref[...] = jnp.zeros_like(small_ref)

        first = tile == 0
        dm = dm_ref[...]
        hs_t = hs_ref[...]
        xc_t = xc_ref[...]
        r = r_ref[...]
        i = i_ref[...]
        lam_v = lam_ref[...]
        sp = _softplus(-lam_v)
        a = a_ref[...]
        mult = mult_ref[...]

        sg = _sigmoid(gl_ref[...])
        ge, dge = _gelu_and_grad(vg_ref[...])
        dlo = dm * sg
        dgl_ref[...] = (dm * (hs_t * ge) * sg * (1.0 - sg)).astype(BF16)
        dvg_ref[...] = (dlo * hs_t * dge).astype(BF16)
        dhs = dlo * ge

        q_s[...] = a * dhs
        mu_first = _scan_chunks(a_ref, q_s, mu_s, mu_next_ref[0:1, :], tm, reverse=True)
        lam_t = dhs + _shift_up(mu_s[...], mu_next_ref[...], 1)
        mu_next_ref[...] = jnp.broadcast_to(mu_first, mu_next_ref.shape)

        h_prev = _shift_down(jnp.where(first, 0.0, hsp_ref[...]), hs_t, 1)
        da = lam_t * h_prev
        dmult = lam_t * (i * xc_t)
        di = lam_t * mult * xc_t
        dxc = lam_t * mult * i
        dlog_a = da * a - dmult * (a * a) / mult
        dr = dlog_a * (-LRU_C * sp)
        dlam_rows = dlog_a * (-LRU_C * r)
        dza = dr * r * (1.0 - r)
        dzx = di * i * (1.0 - i)
        dzab, dzxb = dza.astype(BF16), dzx.astype(BF16)
        xcb = xc_t.astype(BF16)
        dxc = dxc + lax.dot_general(dzab, wa_ref[...], _NT, preferred_element_type=F32)
        dxc = dxc + lax.dot_general(dzxb, wx_ref[...], _NT, preferred_element_type=F32)
        dwa_ref[...] += lax.dot_general(xcb, dzab, _TN, preferred_element_type=F32)
        dwx_ref[...] += lax.dot_general(xcb, dzxb, _TN, preferred_element_type=F32)

        dxc_next = dxc_next_ref[...]
        taps = [_shift_up(dxc, dxc_next, CONV_WIDTH - 1 - k) for k in range(CONV_WIDTH)]
        dv = jnp.zeros_like(dxc)
        for k in range(CONV_WIDTH):
            dv = dv + cw_ref[k:k + 1, :] * taps[k]
        dvl_ref[...] = dv.astype(BF16)
        dxc_next_ref[...] = dxc[:HALO, :]

        v_t = vl_ref[...]
        small = [jnp.sum(dza, axis=0, keepdims=True), jnp.sum(dzx, axis=0, keepdims=True),
                 jnp.sum(dlam_rows, axis=0, keepdims=True) * (-_sigmoid(-lam_v)),
                 jnp.sum(dxc, axis=0, keepdims=True)]
        for k in range(CONV_WIDTH):
            small.append(jnp.sum(taps[k] * v_t, axis=0, keepdims=True))
        for k, row in enumerate(small):
            small_ref[k:k + 1, :] += row
        for cp in out_copies(slot):
            cp.start()

        @pl.when(step == H * nt - 1)
        def _():
            for cp in out_copies(slot) + (out_copies(1 - slot) if H * nt > 1 else []):
                cp.wait()

    def piece(p):
        return pl.BlockSpec((tm, hd), lambda h, t: (nt - 1 - t, p * nb + h))

    def halo(p):
        return pl.BlockSpec((HALO, hd), lambda h, t: (jnp.maximum((nt - 1 - t) * halo_blocks - 1, 0), p * nb + h))

    blk = pl.BlockSpec((tm, hd), lambda h, t: (nt - 1 - t, h))
    vec = pl.BlockSpec((1, hd), lambda h, t: (0, h))
    mat = pl.BlockSpec((None, hd, hd), lambda h, t: (h, 0, 0))
    return _hosted(
        body, tasks, name=name, grid=(H, nt),
        in_specs=[blk, piece(1), piece(2), piece(4), blk, blk, blk, blk, blk, blk, halo(0), vec,
                  pl.BlockSpec((CONV_WIDTH, hd), lambda h, t: (0, h)), mat, mat, pl.BlockSpec(memory_space=pl.ANY)],
        out_specs=[pl.BlockSpec(memory_space=pl.ANY), mat, mat,
                   pl.BlockSpec((None, LRU_SMALL_ROWS, hd), lambda h, t: (h, 0, 0))],
        out_shape=[_sds(dproj.shape, BF16)] + [_sds((H, hd, hd), F32)] * 2 + [_sds((H, LRU_SMALL_ROWS, hd), F32)],
        scratch_shapes=[pltpu.VMEM((HALO, hd), F32), pltpu.VMEM((HALO, hd), F32),
                        pltpu.VMEM((tm, hd), F32), pltpu.VMEM((tm, hd), F32),
                        pltpu.VMEM((2, 3, tm, hd), BF16), pltpu.SemaphoreType.DMA((6,))],
        semantics=("arbitrary", "arbitrary"), aliases={15: 0},
        operands=(dmerged, proj, proj, proj, xc, r_gate, i_gate, a_gate, mult_gate, hs, hs, lam, conv_w, wa, wx,
                  dproj))


def _adamw(w, g, m, v):
    m = ADAM_B1 * m + (1.0 - ADAM_B1) * g
    v = ADAM_B2 * v + (1.0 - ADAM_B2) * (g * g)
    m_hat = m / (1.0 - ADAM_B1 ** ADAM_STEP)
    v_hat = v / (1.0 - ADAM_B2 ** ADAM_STEP)
    delta = -ADAM_LR * (m_hat / (jnp.sqrt(v_hat) + ADAM_EPS) + ADAM_WD * w)
    return delta, m, v


def _reduce_update(pair_sums, chip_sums, w, m, v, chip_slot, name, part=(0, 1), earlier=None):
    R, C = pair_sums.shape[1:]
    p, parts = part
    tr = _tile(R, TILE["update"])
    nblk = R // tr

    def body(slot_ref, own_ref, got_ref, w_ref, m_ref, v_ref, *rest):
        g_out, d_out, m_out, v_out = rest[-4:]
        g = own_ref[...].astype(F32)
        for k in range(3):
            g = g + got_ref[k].astype(F32)
        d, m_new, v_new = _adamw(w_ref[...], g, m_ref[...], v_ref[...])
        g_out[...] = g
        d_out[...] = d
        m_out[...] = m_new
        v_out[...] = v_new

    blk = pl.BlockSpec((tr, C), lambda i, s: (p * nblk + i, 0))
    in_specs = [pl.BlockSpec((None, tr, C), lambda i, s: (s[0], i, 0)),
                pl.BlockSpec((3, tr, C), lambda i, s: (0, i, 0)), blk, blk, blk]
    operands = (chip_slot, pair_sums, chip_sums, w, m, v)
    aliases = {}
    if earlier is not None:
        in_specs += [pl.BlockSpec(memory_space=pl.ANY)] * 4
        operands += tuple(earlier)
        aliases = {6 + k: k for k in range(4)}
    return _call(
        body, name=name,
        grid_spec=pltpu.PrefetchScalarGridSpec(
            num_scalar_prefetch=1, grid=(R // tr,), in_specs=in_specs, out_specs=[blk] * 4),
        out_shape=[_sds((parts * R, C), F32)] * 4,
        input_output_aliases=aliases,
        compiler_params=_cp("parallel"),
    )(*operands)


def _small_updates(small_sum, bias_grads, col0, specs, name):
    n = len(specs)

    def body(col_ref, ss_ref, bg_ref, *refs):
        ins, outs = refs[:3 * n], refs[3 * n:]
        for k, (w, _, _, kind, r0) in enumerate(specs):
            nr, nc = w.shape
            if kind == "rows":
                g = ss_ref[r0:r0 + nr, :]
            elif kind == "cols":
                g = ss_ref[r0:r0 + nr, pl.ds(pl.multiple_of(col_ref[0], nc), nc)]
            else:
                g = bg_ref[r0:r0 + nr, :]
            d, m_new, v_new = _adamw(ins[3 * k][...], g, ins[3 * k + 1][...], ins[3 * k + 2][...])
            for o, val in zip(outs[4 * k:4 * k + 4], (g, d, m_new, v_new)):
                o[...] = val

    def whole(a):
        return pl.BlockSpec(a.shape, lambda i, c: (0,) * a.ndim)

    arrays = [a for (w, m, v, _, _) in specs for a in (w, m, v)]
    res = _call(
        body, name=name,
        grid_spec=pltpu.PrefetchScalarGridSpec(
            num_scalar_prefetch=1, grid=(1,),
            in_specs=[whole(small_sum), whole(bias_grads)] + [whole(a) for a in arrays],
            out_specs=[whole(w) for (w, _, _, _, _) in specs for _ in range(4)]),
        out_shape=[_sds(w.shape, F32) for (w, _, _, _, _) in specs for _ in range(4)],
        compiler_params=_cp("arbitrary"),
    )(col0, small_sum, bias_grads, *arrays)
    return [tuple(res[4 * k:4 * k + 4]) for k in range(n)]


def _slots_from_rows(full, lead):
    L, R, C = full.shape
    r = R // N_DEV
    return full.reshape(L, N_DEV, r, C).transpose(1, 0, 2, 3).reshape(N_DEV, L * r, C)


def _rows_from_slots(slots, lead):
    _, LR, C = slots.shape
    r = LR // lead
    return slots.reshape(N_DEV, lead, r, C).transpose(1, 0, 2, 3).reshape(lead, N_DEV * r, C)


def kernel(x, meta_tokens, norm1_g, w_in, pool_w, pool_scale, conv_w, conv_b, gate_a_w, gate_a_b, gate_x_w, gate_x_b, lru_lambda, w_out, norm2_g, mlp_w1, mlp_w2, final_g, loss_target, m_meta_tokens, m_norm1_g, m_w_in, m_pool_w, m_pool_scale, m_conv_w, m_conv_b, m_gate_a_w, m_gate_a_b, m_gate_x_w, m_gate_x_b, m_lru_lambda, m_w_out, m_norm2_g, m_mlp_w1, m_mlp_w2, m_final_g, v_meta_tokens, v_norm1_g, v_w_in, v_pool_w, v_pool_scale, v_conv_w, v_conv_b, v_gate_a_w, v_gate_a_b, v_gate_x_w, v_gate_x_b, v_lru_lambda, v_w_out, v_norm2_g, v_mlp_w1, v_mlp_w2, v_final_g):
    seq, D = x.shape[1], x.shape[2]
    n_meta = meta_tokens.shape[0]
    G, Cg = pool_w.shape[1], pool_w.shape[3]
    H, hd = gate_a_w.shape[1], gate_a_w.shape[3]
    T = n_meta + seq
    Tp = -(-T // ROW_ALIGN) * ROW_ALIGN
    ix, iy, ic = _pos()
    me = 4 * ix + 2 * iy + ic
    core = jnp.reshape(ic, (1,)).astype(jnp.int32)
    chip_slot = jnp.reshape(2 * ix + iy, (1,)).astype(jnp.int32)

    w_in_l, w1_l, w2_l, w_out_l = w_in[0], mlp_w1[0], mlp_w2[0], w_out[0]
    pool_l = pool_w[0].reshape(G * (Cg // N_DEV), Cg)
    wa_l = gate_a_w[0].reshape(H * (hd // N_DEV), hd)
    wx_l = gate_x_w[0].reshape(H * (hd // N_DEV), hd)
    small_params = jnp.concatenate(
        [meta_tokens, conv_w[0], jnp.zeros((4, D // N_DEV), F32)], axis=0)
    biases = jnp.concatenate([gate_a_b[0], gate_x_b[0]], axis=0)
    (w_in_g, small_g, bias_g) = _all_gather_relay(
        [w_in_l.astype(BF16), small_params, biases], [W_IN_CHUNKS, 1, 1], "gather_first")
    small_full = small_g.transpose(1, 0, 2).reshape(n_meta + 8, D)
    meta_full = small_full[:n_meta]
    conv_full = small_full[n_meta:n_meta + CONV_WIDTH]
    bias_full = bias_g.transpose(1, 0, 2).reshape(2 * H, hd)
    ba_full, bx_full = bias_full[:H], bias_full[H:]

    h0 = jnp.concatenate([meta_full, x[0], jnp.zeros((Tp - T, D), F32)], axis=0)
    u, r1 = _norm_fwd(h0, norm1_g, "norm1")
    proj, ((pool_g,), (wa_g,), (wx_g,), (w_out_g,), (w1_ici,)) = _proj_fwd(
        u, w_in_g, "proj",
        tasks=[_AgFull(pool_l.astype(BF16)), _AgFull(wa_l.astype(BF16)), _AgFull(wx_l.astype(BF16)),
               _AgFull(w_out_l.astype(BF16)), _AgFull(w1_l.astype(BF16), forward_here=False)])
    pool_full = _rows_from_slots(pool_g, G)
    wa_full = _rows_from_slots(wa_g, H)
    wx_full = _rows_from_slots(wx_g, H)
    w_in_parts = [w_in_g]
    d_pool, y_pool = _pool_fwd(proj, pool_full, "pool_fwd")
    (xc, r_gate, i_gate, a_gate, mult_gate, hs, merged), ((w1_g,), (w2_ici,)) = _lru_fwd(
        proj, y_pool, pool_scale, conv_full, conv_b, wa_full, ba_full, wx_full, bx_full, lru_lambda, "lru_fwd",
        tasks=[_AgForward(w1_ici), _AgFull(w2_l.astype(BF16), forward_here=False)])
    w_out_full = w_out_g.reshape(D, D)
    (h1, u2, r2), ((w2_g,),) = _wout_norm_fwd(merged, w_out_full, h0, norm2_g, "wout_norm2", tasks=[_AgForward(w2_ici)])
    (act, a1), _ = _mlp1_fwd(u2, w1_g, "mlp1")
    dh2, dh2b, loss_tile, d_final_g = _mlp2_loss(
        act, w2_g.reshape(-1, D), h1, loss_target[0], final_g.reshape(1, D), n_meta, seq, "mlp2_loss")

    def pair(part, got, tag):
        return _pair_sum(part, got, core, "pair_sum_" + tag)

    d_a1 = _dact_bwd(dh2b, w2_g, a1, "dact")
    dw2_p, _ = _weight_grad(act, dh2b, N_DEV, True, "dw2")
    dw1_p, ((dw2_got,),) = _weight_grad(u2, d_a1, N_DEV, False, "dw1", tasks=[_RsSibling(dw2_p)])
    dw2_pair = pair(dw2_p, dw2_got, "w2")
    (dh1, dh1b, d_norm2_g), ((dw2_chips,), (dw1_got,)) = _nt_norm_bwd(
        d_a1, [w1_g], dh2, h1, r2, norm2_g, True, "du2_norm2", tasks=[_RsChips(dw2_pair), _RsSibling(dw1_p)])
    dw1_pair = pair(dw1_p, dw1_got, "w1")
    dmerged = _dmerged_bwd(dh1b, w_out_full, "dmerged")
    dwout_p, _ = _weight_grad(merged, dh1b, 2, True, "dwout")
    dwout_p = dwout_p.reshape(N_DEV, D // N_DEV, D)
    (dproj_pool, dpool_full, d_scale), ((dwout_got,),) = _pool_bwd(
        dmerged, proj, y_pool, d_pool, pool_scale, pool_full, "pool_bwd", tasks=[_RsSibling(dwout_p)])
    dwout_pair = pair(dwout_p, dwout_got, "wout")
    (dproj, dwa_full, dwx_full, lru_small), ((dw1_chips,), (dwout_chips,)) = _lru_bwd(
        dmerged, proj, xc, r_gate, i_gate, a_gate, mult_gate, hs, lru_lambda, conv_full, wa_full, wx_full, dproj_pool,
        "lru_bwd", tasks=[_RsChips(dw1_pair), _RsChips(dwout_pair)])
    dwin_a, _ = _weight_grad(u, dproj, N_DEV, False, "dwin_a", part=(0, 2))
    dwin_b, ((dwin_a_got,),) = _weight_grad(u, dproj, N_DEV, False, "dwin_b", part=(1, 2), tasks=[_RsSibling(dwin_a)])
    dwin_a_pair = pair(dwin_a, dwin_a_got, "win_a")
    dpool_p = _slots_from_rows(dpool_full, G).astype(BF16)
    dwa_p = _slots_from_rows(dwa_full, H).astype(BF16)
    dwx_p = _slots_from_rows(dwx_full, H).astype(BF16)
    late = [dwin_b, dpool_p, dwa_p, dwx_p]
    n_tiles = Tp // _tile(Tp, TILE["nt"])
    n_first = max(1, n_tiles // 2)
    (dh0_a, d_norm1_a), ((dwin_a_chips,), *late_got) = _nt_norm_bwd(
        dproj, w_in_parts, dh1, h0, r1, norm1_g, False, "du_norm1_a",
        tasks=[_RsChips(dwin_a_pair)] + [_RsSibling(p) for p in late], tiles=(0, n_first))
    late_pair = [pair(p, g[0], "late%d" % k) for k, (p, g) in enumerate(zip(late, late_got))]
    (dh0, d_norm1_g), late_chips = _nt_norm_bwd(
        dproj, w_in_parts, dh1, h0, r1, norm1_g, False, "du_norm1_b", tasks=[_RsChips(p) for p in late_pair],
        tiles=(n_first, n_tiles - n_first), earlier=(dh0_a, d_norm1_a))
    grad_x = dh0[n_meta:T][None]

    pair_sums = [dw2_pair, dw1_pair, dwout_pair] + late_pair[1:]
    chip_sums = [dw2_chips, dw1_chips, dwout_chips] + [c[0] for c in late_chips[1:]]
    big = {}
    names = ["mlp_w2", "mlp_w1", "w_out", "pool_w", "gate_a_w", "gate_x_w"]
    trip = {"mlp_w2": (mlp_w2, m_mlp_w2, v_mlp_w2), "mlp_w1": (mlp_w1, m_mlp_w1, v_mlp_w1),
            "w_out": (w_out, m_w_out, v_w_out),
            "pool_w": (pool_w, m_pool_w, v_pool_w), "gate_a_w": (gate_a_w, m_gate_a_w, v_gate_a_w),
            "gate_x_w": (gate_x_w, m_gate_x_w, v_gate_x_w)}
    for k, nm in enumerate(names):
        w_, m_, v_ = trip[nm]
        shape2 = pair_sums[k].shape[1:]
        outs = _reduce_update(pair_sums[k], chip_sums[k], w_.reshape(shape2), m_.reshape(shape2), v_.reshape(shape2),
                              chip_slot, "update_" + nm)
        big[nm] = [o.reshape(w_.shape) for o in outs]
    win2 = [a_[0] for a_ in (w_in, m_w_in, v_w_in)]
    win_a = _reduce_update(dwin_a_pair, dwin_a_chips, *win2, chip_slot, "update_w_in_a", part=(0, 2))
    win_b = _reduce_update(late_pair[0], late_chips[0][0], *win2, chip_slot, "update_w_in_b", part=(1, 2), earlier=win_a)
    big["w_in"] = [o.reshape(w_in.shape) for o in win_b]
    names = names + ["w_in"]

    lru_rows = lru_small.transpose(1, 0, 2).reshape(LRU_SMALL_ROWS, D)
    small_part = jnp.concatenate(
        [dh0[:n_meta], d_norm1_g, d_scale, d_norm2_g, d_final_g, lru_rows, jnp.zeros((4, D), F32)], axis=0)
    (small_all,) = _all_gather([small_part], "gather_small_grads")
    small_sum = _sum_slots(small_all, "sum_small_grads")
    o = n_meta
    dcol, hcol = D // N_DEV, hd // N_DEV
    bias_grads = jnp.concatenate(
        [lax.dynamic_slice_in_dim(small_sum[o + 4 + k].reshape(H, hd), me * hcol, hcol, axis=1) for k in range(2)], axis=0)
    col0 = jnp.reshape(me * dcol, (1,)).astype(jnp.int32)
    small = {"meta_tokens": (meta_tokens, m_meta_tokens, v_meta_tokens, "cols", 0),
             "norm1_g": (norm1_g, m_norm1_g, v_norm1_g, "rows", o),
             "pool_scale": (pool_scale, m_pool_scale, v_pool_scale, "rows", o + 1),
             "norm2_g": (norm2_g, m_norm2_g, v_norm2_g, "rows", o + 2),
             "final_g": (final_g, m_final_g, v_final_g, "rows", o + 3),
             "gate_a_b": (gate_a_b, m_gate_a_b, v_gate_a_b, "bias", 0),
             "gate_x_b": (gate_x_b, m_gate_x_b, v_gate_x_b, "bias", H),
             "lru_lambda": (lru_lambda, m_lru_lambda, v_lru_lambda, "rows", o + 6),
             "conv_b": (conv_b, m_conv_b, v_conv_b, "rows", o + 7),
             "conv_w": (conv_w, m_conv_w, v_conv_w, "cols", o + 8)}

    def two_d(a):
        return a.reshape(-1, a.shape[-1])

    small_out = _small_updates(
        small_sum, bias_grads, col0,
        [(two_d(w_), two_d(m_), two_d(v_), kind, row) for (w_, m_, v_, kind, row) in small.values()], "update_small")
    small_res = {nm: [r.reshape(spec[0].shape) for r in res] for (nm, spec), res in zip(small.items(), small_out)}

    def leaves(kind):
        out = {nm: res[kind] for nm, res in small_res.items()}
        for nm in names:
            out[nm] = big[nm][kind]
        order = ["meta_tokens", "norm1_g", "w_in", "pool_w", "pool_scale", "conv_w", "conv_b", "gate_a_w", "gate_a_b",
                 "gate_x_w", "gate_x_b", "lru_lambda", "w_out", "norm2_g", "mlp_w1", "mlp_w2", "final_g"]
        return [out[nm] for nm in order]

    loss = lax.psum(loss_tile[0, 0], ("x", "y", "c"))
    return (loss, grad_x, *leaves(0), *leaves(1), *leaves(2), *leaves(3))
```

```python
import jax
import jax.numpy as jnp
from jax import lax
from jax.experimental import pallas as pl
from jax.experimental.pallas import tpu as pltpu

F32 = jnp.float32
BF16 = jnp.bfloat16
MESH = pl.DeviceIdType.MESH
N_DEV = 8
POOL_WINDOWS = (2, 4, 8, 16)
MAX_WINDOW = 16
CONV_WIDTH = 4
HALO = 8
LRU_C = 8.0
NORM_EPS = 1e-6
ADAM_LR, ADAM_B1, ADAM_B2, ADAM_EPS, ADAM_WD, ADAM_STEP = 0.001, 0.9, 0.999, 1e-08, 0.01, 10
ROW_ALIGN = 128
VMEM_LIMIT = 56 << 20
TILE = dict(norm=384, proj=1408, pool=384, lru=704, wout=384, mlp1=1408, mlp2=704, dact=1408, tn=2112,
            nt=704, dmerged=704, update=256, pair=1024)
MLP2_K = 1024
EPILOGUE_ROWS = 176
SCAN_GROUP = 4
W_IN_CHUNKS = 5
RELAY_STEP_PERCENT = 56
MID_STEP_PERCENT = 88

_NT = (((1,), (1,)), ((), ()))
_TN = (((0,), (0,)), ((), ()))


def _call(body, **kw):
    return pl.pallas_call(body, **kw)


def _cp(*sem):
    return pltpu.CompilerParams(dimension_semantics=sem, vmem_limit_bytes=VMEM_LIMIT)


def _tile(total, pref):
    best = None
    for t in range(16, min(total, pref) + 1, 16):
        if total % t == 0:
            best = t
    assert best is not None, (total, pref)
    return best


def _sds(shape, dtype):
    return jax.ShapeDtypeStruct(shape, dtype)


def _pos():
    return lax.axis_index("x"), lax.axis_index("y"), lax.axis_index("c")


def _all_gather(shards, name):
    n = len(shards)

    def body(*refs):
        ins, outs = refs[:n], refs[n:2 * n]
        send_sems, recv_sems, local_sems = refs[2 * n:]
        x, y, c = _pos()
        me, sib = (x, y, c), (x, y, 1 - c)
        chips = [(1 - x, y), (x, 1 - y), (1 - x, 1 - y)]

        def slot(p):
            return 4 * p[0] + 2 * p[1] + p[2]

        def copy(a, k, block, to, src=None):
            dst = outs[a].at[slot(block)]
            return pltpu.make_async_remote_copy(
                src_ref=dst if src is None else src, dst_ref=dst,
                send_sem=send_sems.at[7 * a + k], recv_sem=recv_sems.at[7 * a + k],
                device_id=to, device_id_type=MESH)

        mine = [pltpu.make_async_copy(ins[a], outs[a].at[slot(me)], local_sems.at[a]) for a in range(n)]
        for m in mine:
            m.start()
        first = []
        for a in range(n):
            first.append(copy(a, 0, me, sib, src=ins[a]))
            first += [copy(a, 1 + j, me, (*chip, c), src=ins[a]) for j, chip in enumerate(chips)]
        for cp in first:
            cp.start()
        passed = []
        for a in range(n):
            for j, chip in enumerate(chips):
                copy(a, 1 + j, (*chip, c), me).wait_recv()
                fwd = copy(a, 4 + j, (*chip, c), sib)
                fwd.start()
                passed.append(fwd)
        for a in range(n):
            copy(a, 0, sib, me).wait_recv()
            for j, chip in enumerate(chips):
                copy(a, 4 + j, (*chip, 1 - c), me).wait_recv()
        for cp in first + passed:
            cp.wait_send()
        for m in mine:
            m.wait()

    hbm = pl.BlockSpec(memory_space=pl.ANY)
    return _call(
        body, name=name,
        out_shape=[_sds((N_DEV,) + s.shape, s.dtype) for s in shards],
        in_specs=[hbm] * n, out_specs=[hbm] * n,
        scratch_shapes=[pltpu.SemaphoreType.DMA((7 * n,)), pltpu.SemaphoreType.DMA((7 * n,)),
                        pltpu.SemaphoreType.DMA((n,))],
    )(*shards)


def _all_gather_relay(shards, chunks, name):
    n = len(shards)
    units = []
    for a, s in enumerate(shards):
        if chunks[a] == 1:
            units.append((a, None, None))
        else:
            w = s.shape[-1] // chunks[a]
            units += [(a, q * w, w) for q in range(chunks[a])]
    nu = len(units)

    def body(*refs):
        ins, outs = refs[:n], refs[n:2 * n]
        send_sems, recv_sems, local_sems = refs[2 * n:]
        x, y, c = _pos()
        me, sib = (x, y, c), (x, y, 1 - c)
        x_nbr, y_nbr, diag = (1 - x, y, c), (x, 1 - y, c), (1 - x, 1 - y, c)
        came_from = (x + (1 - c) * (1 - 2 * x), y + c * (1 - 2 * y), c)
        pass_to = (x + c * (1 - 2 * x), y + (1 - c) * (1 - 2 * y), c)

        def slot(p):
            return 4 * p[0] + 2 * p[1] + p[2]

        def src_view(u):
            a, c0, w = units[u]
            return ins[a] if c0 is None else ins[a].at[:, pl.ds(c0, w)]

        def dst_view(u, p):
            a, c0, w = units[u]
            return outs[a].at[slot(p)] if c0 is None else outs[a].at[slot(p), :, pl.ds(c0, w)]

        def copy(u, k, block, to, from_shard=False):
            dst = dst_view(u, block)
            return pltpu.make_async_remote_copy(
                src_ref=src_view(u) if from_shard else dst, dst_ref=dst,
                send_sem=send_sems.at[7 * u + k], recv_sem=recv_sems.at[7 * u + k],
                device_id=to, device_id_type=MESH)

        mine = [pltpu.make_async_copy(src_view(u), dst_view(u, me), local_sems.at[u]) for u in range(nu)]
        for m in mine:
            m.start()
        sent = []
        for u in range(nu):
            sent += [copy(u, 0, me, sib, True), copy(u, 1, me, x_nbr, True), copy(u, 2, me, y_nbr, True)]
        for cp in sent:
            cp.start()
        for u in range(nu):
            copy(u, 1, x_nbr, me).wait_recv()
            copy(u, 2, y_nbr, me).wait_recv()
            sent += [copy(u, 3, came_from, pass_to), copy(u, 4, x_nbr, sib), copy(u, 5, y_nbr, sib)]
            for cp in sent[-3:]:
                cp.start()
        for u in range(nu):
            copy(u, 3, diag, me).wait_recv()
            sent.append(copy(u, 6, diag, sib))
            sent[-1].start()
        for u in range(nu):
            copy(u, 0, sib, me).wait_recv()
            for k, p in ((4, x_nbr), (5, y_nbr), (6, diag)):
                copy(u, k, (p[0], p[1], 1 - c), me).wait_recv()
        for cp in sent:
            cp.wait_send()
        for m in mine:
            m.wait()

    hbm = pl.BlockSpec(memory_space=pl.ANY)
    return _call(
        body, name=name,
        out_shape=[_sds((N_DEV,) + s.shape, s.dtype) for s in shards],
        in_specs=[hbm] * n, out_specs=[hbm] * n,
        scratch_shapes=[pltpu.SemaphoreType.DMA((7 * nu,)), pltpu.SemaphoreType.DMA((7 * nu,)),
                        pltpu.SemaphoreType.DMA((nu,))],
    )(*shards)


def _other_chips(x, y):
    return [(1 - x, y), (x, 1 - y), (1 - x, 1 - y)]


class _AgFull:
    n_sem, n_local = 7, 1

    def __init__(self, shard, forward_here=True):
        self.forward_here = forward_here
        self.ins = [shard]
        self.out_shapes = [_sds((N_DEV,) + shard.shape, shard.dtype)]
        self.aliases = []

    def _places(self):
        x, y, c = _pos()
        came_from = (x + (1 - c) * (1 - 2 * x), y + c * (1 - 2 * y), c)
        pass_to = (x + c * (1 - 2 * x), y + (1 - c) * (1 - 2 * y), c)
        return (x, y, 1 - c), (1 - x, y, c), (x, 1 - y, c), (1 - x, 1 - y, c), came_from, pass_to, c

    def _copy(self, outs, sems, k, block, to, src=None):
        send, recv, _, base, _ = sems
        dst = outs[0].at[4 * block[0] + 2 * block[1] + block[2]]
        return pltpu.make_async_remote_copy(src_ref=dst if src is None else src, dst_ref=dst, send_sem=send.at[base + k],
                                            recv_sem=recv.at[base + k], device_id=to, device_id_type=MESH)

    def _sends(self, ins, outs, sems):
        sib, x_nbr, y_nbr = self._places()[:3]
        x, y, c = _pos()
        return [self._copy(outs, sems, k, (x, y, c), p, src=ins[0]) for k, p in enumerate((sib, x_nbr, y_nbr))]

    def _own(self, ins, outs, sems):
        x, y, c = _pos()
        return pltpu.make_async_copy(ins[0], outs[0].at[4 * x + 2 * y + c], sems[2].at[sems[4]])

    def _forwards(self, outs, sems, core_of_block):
        return _forward_copies(outs[0], sems[0], sems[1], sems[3] + 4, core_of_block)

    def start(self, ins, outs, sems):
        self._own(ins, outs, sems).start()
        for cp in self._sends(ins, outs, sems):
            cp.start()

    def relay(self, ins, outs, sems):
        _, _, _, _, came_from, pass_to, c = self._places()
        self._copy(outs, sems, 1 + c, came_from, came_from).wait_recv()
        self._copy(outs, sems, 3, came_from, pass_to).start()

    def _late_arrivals(self, outs, sems):
        _, _, _, diag, came_from, pass_to, c = self._places()
        return [self._copy(outs, sems, 2 - c, pass_to, pass_to), self._copy(outs, sems, 3, diag, diag)]

    def mid(self, ins, outs, sems):
        if self.forward_here:
            for cp in self._late_arrivals(outs, sems):
                cp.wait_recv()
            for cp in self._forwards(outs, sems, "mine"):
                cp.start()

    def finish(self, ins, outs, sems):
        sib, _, _, _, came_from, pass_to, c = self._places()
        self._copy(outs, sems, 0, sib, sib).wait_recv()
        if self.forward_here:
            for cp in self._forwards(outs, sems, "sibling"):
                cp.wait_recv()
            for cp in self._forwards(outs, sems, "mine"):
                cp.wait_send()
        else:
            for cp in self._late_arrivals(outs, sems):
                cp.wait_recv()
        for cp in self._sends(ins, outs, sems) + [self._copy(outs, sems, 3, came_from, pass_to)]:
            cp.wait_send()
        self._own(ins, outs, sems).wait()


def _forward_copies(gathered_ref, send, recv, base, core_of_block):
    x, y, c = _pos()
    res = []
    for k, chip in enumerate(_other_chips(x, y)):
        blk = gathered_ref.at[4 * chip[0] + 2 * chip[1] + (c if core_of_block == "mine" else 1 - c)]
        res.append(pltpu.make_async_remote_copy(src_ref=blk, dst_ref=blk, send_sem=send.at[base + k],
                                                recv_sem=recv.at[base + k], device_id=(x, y, 1 - c),
                                                device_id_type=MESH))
    return res


class _AgForward:
    n_sem, n_local = 3, 0

    def __init__(self, gathered):
        self.ins = [gathered]
        self.out_shapes = [_sds(gathered.shape, gathered.dtype)]
        self.aliases = [(0, 0)]

    def start(self, ins, outs, sems):
        for cp in _forward_copies(outs[0], sems[0], sems[1], sems[3], "mine"):
            cp.start()

    def finish(self, ins, outs, sems):
        for cp in _forward_copies(outs[0], sems[0], sems[1], sems[3], "sibling"):
            cp.wait_recv()
        for cp in _forward_copies(outs[0], sems[0], sems[1], sems[3], "mine"):
            cp.wait_send()


class _RsSibling:
    n_sem, n_local = 4, 0

    def __init__(self, part):
        self.ins = [part]
        self.out_shapes = [_sds((4,) + part.shape[1:], part.dtype)]
        self.aliases = []

    def _copies(self, ins, outs, sems):
        send, recv, _, base, _ = sems
        x, y, c = _pos()
        return [pltpu.make_async_remote_copy(src_ref=ins[0].at[2 * q + (1 - c)], dst_ref=outs[0].at[q],
                                             send_sem=send.at[base + q], recv_sem=recv.at[base + q],
                                             device_id=(x, y, 1 - c), device_id_type=MESH) for q in range(4)]

    def start(self, ins, outs, sems):
        for cp in self._copies(ins, outs, sems):
            cp.start()

    def finish(self, ins, outs, sems):
        for cp in self._copies(ins, outs, sems):
            cp.wait()


class _RsChips:
    n_sem, n_local = 3, 0

    def __init__(self, pair):
        self.ins = [pair]
        self.out_shapes = [_sds((3,) + pair.shape[1:], pair.dtype)]
        self.aliases = []

    def _copies(self, ins, outs, sems):
        send, recv, _, base, _ = sems
        x, y, c = _pos()
        return [pltpu.make_async_remote_copy(src_ref=ins[0].at[2 * chip[0] + chip[1]], dst_ref=outs[0].at[k],
                                             send_sem=send.at[base + k], recv_sem=recv.at[base + k],
                                             device_id=(*chip, c), device_id_type=MESH)
                for k, chip in enumerate(_other_chips(x, y))]

    def start(self, ins, outs, sems):
        for cp in self._copies(ins, outs, sems):
            cp.start()

    def finish(self, ins, outs, sems):
        for cp in self._copies(ins, outs, sems):
            cp.wait()


def _hosted(body, tasks, *, grid, in_specs, out_specs, out_shape, scratch_shapes=(), name, semantics, operands,
            aliases=None):
    in_specs, out_specs, out_shape = list(in_specs), list(out_specs), list(out_shape)
    scratch_shapes = list(scratch_shapes)
    aliases = dict(aliases or {})
    if not tasks:
        res = _call(body, name=name, grid=grid, in_specs=in_specs, out_specs=out_specs, out_shape=out_shape,
                    scratch_shapes=scratch_shapes, input_output_aliases=aliases,
                    compiler_params=_cp(*semantics))(*operands)
        return list(res), []
    n_in, n_out, n_scr = len(in_specs), len(out_specs), len(scratch_shapes)
    t_ins = [a for t in tasks for a in t.ins]
    t_outs = [o for t in tasks for o in t.out_shapes]
    i0, o0 = n_in, n_out
    for t in tasks:
        for (i, o) in t.aliases:
            aliases[i0 + i] = o0 + o
        i0 += len(t.ins)
        o0 += len(t.out_shapes)
    n_sem = sum(t.n_sem for t in tasks)
    n_local = max(1, sum(t.n_local for t in tasks))
    n_steps = 1
    for g in grid:
        n_steps *= g
    mid_step = min(n_steps - 1, (n_steps * MID_STEP_PERCENT) // 100)
    relay_step = min(mid_step, (n_steps * RELAY_STEP_PERCENT) // 100)

    def wrapped(*refs):
        cut = [n_in, len(t_ins), n_out, len(t_outs), n_scr]
        parts, at = [], 0
        for n in cut:
            parts.append(refs[at:at + n])
            at += n
        ins, tin, outs, tout, scratch = parts
        send, recv, local = refs[at:]
        step = pl.program_id(0)
        for d in range(1, len(grid)):
            step = step * grid[d] + pl.program_id(d)

        def each(method):
            i, o, s, l = 0, 0, 0, 0
            for t in tasks:
                if hasattr(t, method):
                    getattr(t, method)(tin[i:i + len(t.ins)], tout[o:o + len(t.out_shapes)], (send, recv, local, s, l))
                i, o, s, l = i + len(t.ins), o + len(t.out_shapes), s + t.n_sem, l + t.n_local

        @pl.when(step == 0)
        def _():
            each("start")

        body(*ins, *outs, *scratch)

        @pl.when(step == relay_step)
        def _():
            each("relay")

        @pl.when(step == mid_step)
        def _():
            each("mid")

        @pl.when(step == n_steps - 1)
        def _():
            each("finish")

    hbm = pl.BlockSpec(memory_space=pl.ANY)
    res = _call(
        wrapped, name=name, grid=grid,
        in_specs=in_specs + [hbm] * len(t_ins), out_specs=out_specs + [hbm] * len(t_outs),
        out_shape=out_shape + t_outs,
        scratch_shapes=scratch_shapes + [pltpu.SemaphoreType.DMA((n_sem,)), pltpu.SemaphoreType.DMA((n_sem,)),
                                         pltpu.SemaphoreType.DMA((n_local,))],
        input_output_aliases=aliases,
        compiler_params=_cp(*(["arbitrary"] * len(grid))),
    )(*operands, *t_ins)
    res = list(res)
    task_outs, o = [], n_out
    for t in tasks:
        task_outs.append(res[o:o + len(t.out_shapes)])
        o += len(t.out_shapes)
    return res[:n_out], task_outs


def _pair_sum(part, got, core, name):
    _, R, C = part.shape
    tr = _tile(R, TILE["pair"]) if R % 16 == 0 else R

    def body(core_ref, p_ref, g_ref, o_ref):
        o_ref[...] = (p_ref[...].astype(F32) + g_ref[...].astype(F32)).astype(o_ref.dtype)

    return _call(
        body, name=name,
        grid_spec=pltpu.PrefetchScalarGridSpec(
            num_scalar_prefetch=1, grid=(4, R // tr),
            in_specs=[pl.BlockSpec((None, tr, C), lambda q, i, cr: (2 * q + cr[0], i, 0)),
                      pl.BlockSpec((None, tr, C), lambda q, i, cr: (q, i, 0))],
            out_specs=pl.BlockSpec((None, tr, C), lambda q, i, cr: (q, i, 0))),
        out_shape=_sds((4, R, C), part.dtype),
        compiler_params=_cp("parallel", "parallel"),
    )(core, part, got)


def _sum_slots(gathered, name):
    _, R, C = gathered.shape

    def body(g_ref, o_ref):
        acc = g_ref[0]
        for s in range(1, N_DEV):
            acc = acc + g_ref[s]
        o_ref[...] = acc

    return _call(body, name=name, out_shape=_sds((R, C), F32))(gathered)


def _sigmoid(z):
    return jax.nn.sigmoid(z)


def _softplus(z):
    e = jnp.exp(-jnp.abs(z))
    log1p_e = jnp.where(e < 0.01, e * (1.0 - e * (0.5 - e * (1.0 / 3.0))), jnp.log(1.0 + e))
    return jnp.maximum(z, 0.0) + log1p_e


_GELU_K = 0.7978845608028654
_GELU_C = 0.044715


def _gelu_and_grad(z):
    t = jnp.tanh(_GELU_K * (z + _GELU_C * z * z * z))
    g = 0.5 * z * (1.0 + t)
    dg = 0.5 * (1.0 + t) + 0.5 * z * (1.0 - t * t) * _GELU_K * (1.0 + 3.0 * _GELU_C * z * z)
    return g, dg


def _gelu(z):
    t = jnp.tanh(_GELU_K * (z + _GELU_C * z * z * z))
    return 0.5 * z * (1.0 + t)


def _row_ids(tile_index, tm, width=1):
    return tile_index * tm + lax.broadcasted_iota(jnp.int32, (tm, width), 0)


def _shift_down(prev, cur, k):
    if k == 0:
        return cur
    ext = jnp.concatenate([prev, cur], axis=0)
    return pltpu.roll(ext, k, axis=0)[prev.shape[0]:]


def _shift_up(cur, nxt, k):
    if k == 0:
        return cur
    ext = jnp.concatenate([cur, nxt], axis=0)
    return pltpu.roll(ext, ext.shape[0] - k, axis=0)[:cur.shape[0]]


def _lru_gates(r, sp):
    z = LRU_C * r * sp
    a = jnp.exp(-z)
    t = jnp.tanh(z)
    mult = jnp.sqrt(2.0 * t / (1.0 + t))
    return a, mult


def _scan_chunks(a_ref, b_ref, out_ref, carry, n_rows, reverse):
    n_chunks = n_rows // 8
    cols = a_ref.shape[1]
    rid = lax.broadcasted_iota(jnp.int32, (8, cols), 0)
    edge = 0 if reverse else 7
    group = SCAN_GROUP if n_chunks % SCAN_GROUP == 0 else 1

    def local_scan(ci):
        rows = pl.ds(pl.multiple_of(ci * 8, 8), 8)
        a = a_ref[rows, :]
        b = b_ref[rows, :]
        for s in (1, 2, 4):
            if reverse:
                keep = rid < 8 - s
                a_n, b_n = pltpu.roll(a, 8 - s, axis=0), pltpu.roll(b, 8 - s, axis=0)
            else:
                keep = rid >= s
                a_n, b_n = pltpu.roll(a, s, axis=0), pltpu.roll(b, s, axis=0)
            b = a * jnp.where(keep, b_n, 0.0) + b
            a = a * jnp.where(keep, a_n, 1.0)
        a_e = jnp.sum(jnp.where(rid == edge, a, 0.0), axis=0, keepdims=True)
        b_e = jnp.sum(jnp.where(rid == edge, b, 0.0), axis=0, keepdims=True)
        return rows, a, b, a_e, b_e

    def chunks(k, h):
        first = k * group
        scans = [local_scan((n_chunks - 1 - (first + j)) if reverse else first + j) for j in range(group)]
        for rows, a, b, a_e, b_e in scans:
            out_ref[rows, :] = a * h + b
            h = a_e * h + b_e
        return h

    return lax.fori_loop(0, n_chunks // group, chunks, carry)


def _norm_fwd(h, g, name):
    Tp, D = h.shape
    tm = _tile(Tp, TILE["norm"])

    def body(h_ref, g_ref, u_ref, r_ref):
        x = h_ref[...]
        r = lax.rsqrt(jnp.mean(x * x, axis=-1, keepdims=True) + NORM_EPS)
        u_ref[...] = (x * r * g_ref[...]).astype(BF16)
        r_ref[...] = r

    return _call(
        body, name=name, grid=(Tp // tm,),
        in_specs=[pl.BlockSpec((tm, D), lambda i: (i, 0)), pl.BlockSpec((1, D), lambda i: (0, 0))],
        out_specs=[pl.BlockSpec((tm, D), lambda i: (i, 0)), pl.BlockSpec((tm, 1), lambda i: (i, 0))],
        out_shape=[_sds((Tp, D), BF16), _sds((Tp, 1), F32)],
        compiler_params=_cp("parallel"),
    )(h, g)


def _proj_fwd(u, w_slots, name, tasks=()):
    Tp, K = u.shape
    S, _, n = w_slots.shape
    tm = _tile(Tp, TILE["proj"])

    def body(a_ref, b_ref, o_ref):
        o_ref[...] = jnp.dot(a_ref[...], b_ref[...], preferred_element_type=F32)

    (proj,), extra = _hosted(
        body, tasks, name=name, grid=(S, Tp // tm),
        in_specs=[pl.BlockSpec((tm, K), lambda j, i: (i, 0)), pl.BlockSpec((None, K, n), lambda j, i: (j, 0, 0))],
        out_specs=[pl.BlockSpec((tm, n), lambda j, i: (i, j))],
        out_shape=[_sds((Tp, S * n), F32)],
        semantics=("parallel", "parallel"), operands=(u, w_slots))
    return proj, extra


def _mlp1_fwd(u2, w_slots, name, tasks=()):
    Tp, K = u2.shape
    S, _, n = w_slots.shape
    tm = _tile(Tp, TILE["mlp1"])

    def body(a_ref, b_ref, act_ref, a1_ref):
        a1 = jnp.dot(a_ref[...], b_ref[...], preferred_element_type=F32)
        relu = jnp.maximum(a1, 0.0)
        act_ref[...] = (relu * relu).astype(BF16)
        a1_ref[...] = a1.astype(BF16)

    return _hosted(
        body, tasks, name=name, grid=(S, Tp // tm),
        in_specs=[pl.BlockSpec((tm, K), lambda j, i: (i, 0)), pl.BlockSpec((None, K, n), lambda j, i: (j, 0, 0))],
        out_specs=[pl.BlockSpec((tm, n), lambda j, i: (i, j))] * 2,
        out_shape=[_sds((Tp, S * n), BF16)] * 2,
        semantics=("parallel", "parallel"), operands=(u2, w_slots))


def _pool_fwd(proj, pool_w, name):
    Tp = proj.shape[0]
    G, Cg, _ = pool_w.shape
    D = G * Cg
    tm = _tile(Tp, TILE["pool"])

    def body(v_ref, w_ref, d_ref, y_ref, prev_ref):
        t = pl.program_id(0)

        @pl.when(t == 0)
        def _():
            prev_ref[...] = jnp.zeros_like(prev_ref)

        rows = _row_ids(t, tm)
        for g, win in enumerate(POOL_WINDOWS):
            cols = slice(g * Cg, (g + 1) * Cg)
            v = v_ref[:, cols]
            s = jnp.concatenate([prev_ref[:, cols], v], axis=0)
            k = 1
            while k < win:
                s = s + pltpu.roll(s, k, axis=0)
                k *= 2
            cnt = jnp.minimum(rows + 1, win).astype(F32)
            d = s[MAX_WINDOW:] / cnt - v
            d_ref[:, cols] = d.astype(BF16)
            y_ref[:, cols] = jnp.dot(d.astype(BF16), w_ref[g], preferred_element_type=F32)
        prev_ref[...] = v_ref[tm - MAX_WINDOW:, :]

    return _call(
        body, name=name, grid=(Tp // tm,),
        in_specs=[pl.BlockSpec((tm, D), lambda t: (t, 0)), pl.BlockSpec((G, Cg, Cg), lambda t: (0, 0, 0))],
        out_specs=[pl.BlockSpec((tm, D), lambda t: (t, 0))] * 2,
        out_shape=[_sds((Tp, D), BF16), _sds((Tp, D), F32)],
        scratch_shapes=[pltpu.VMEM((MAX_WINDOW, D), F32)],
        compiler_params=_cp("arbitrary"),
    )(proj, pool_w)


def _lru_fwd(proj, y_pool, scale, conv_w, conv_b, wa, ba, wx, bx, lam, name, tasks=()):
    Tp = proj.shape[0]
    H, hd, _ = wa.shape
    D = H * hd
    tm = _tile(Tp, TILE["lru"])
    nb = D // hd

    def body(vl_ref, vg_ref, gp_ref, gl_ref, y_ref, sc_ref, cw_ref, cb_ref, wa_ref, ba_ref, wx_ref, bx_ref,
             lam_ref, xc_ref, r_ref, i_ref, a_ref, mult_ref, hs_ref, m_ref, prev_ref, carry_ref, b_s):
        t = pl.program_id(1)

        @pl.when(t == 0)
        def _():
            prev_ref[...] = jnp.zeros_like(prev_ref)
            carry_ref[...] = jnp.zeros_like(carry_ref)

        v = vl_ref[...]
        prev = prev_ref[...]
        xc = jnp.zeros_like(v) + cb_ref[...]
        for k in range(CONV_WIDTH):
            xc = xc + cw_ref[k:k + 1, :] * _shift_down(prev, v, CONV_WIDTH - 1 - k)
        prev_ref[...] = v[tm - HALO:, :]
        xcb = xc.astype(BF16)
        r = _sigmoid(jnp.dot(xcb, wa_ref[...], preferred_element_type=F32) + ba_ref[...])
        i = _sigmoid(jnp.dot(xcb, wx_ref[...], preferred_element_type=F32) + bx_ref[...])
        a, mult = _lru_gates(r, _softplus(-lam_ref[...]))
        a_ref[...] = a
        mult_ref[...] = mult
        b_s[...] = mult * (i * xc)
        xc_ref[...] = xc
        r_ref[...] = r
        i_ref[...] = i
        carry_ref[0:1, :] = _scan_chunks(a_ref, b_s, hs_ref, carry_ref[0:1, :], tm, reverse=False)
        lru_out = hs_ref[...] * _gelu(vg_ref[...])
        pool_out = y_ref[...] * sc_ref[...]
        m_ref[...] = (_sigmoid(gp_ref[...]) * pool_out + _sigmoid(gl_ref[...]) * lru_out).astype(BF16)

    def piece(p):
        return pl.BlockSpec((tm, hd), lambda h, t: (t, p * nb + h))

    blk = pl.BlockSpec((tm, hd), lambda h, t: (t, h))
    vec = pl.BlockSpec((1, hd), lambda h, t: (0, h))
    mat = pl.BlockSpec((None, hd, hd), lambda h, t: (h, 0, 0))
    bias = pl.BlockSpec((None, 1, hd), lambda h, t: (h, 0, 0))
    return _hosted(
        body, tasks, name=name, grid=(H, Tp // tm),
        in_specs=[piece(1), piece(2), piece(3), piece(4), blk, vec,
                  pl.BlockSpec((CONV_WIDTH, hd), lambda h, t: (0, h)), vec, mat, bias, mat, bias, vec],
        out_specs=[blk] * 7,
        out_shape=[_sds((Tp, D), F32)] * 6 + [_sds((Tp, D), BF16)],
        scratch_shapes=[pltpu.VMEM((HALO, hd), F32), pltpu.VMEM((8, hd), F32), pltpu.VMEM((tm, hd), F32)],
        semantics=("parallel", "arbitrary"),
        operands=(proj, proj, proj, proj, y_pool, scale, conv_w, conv_b, wa, ba.reshape(H, 1, hd), wx,
                  bx.reshape(H, 1, hd), lam))


def _wout_norm_fwd(merged, w_out, h0, g2, name, tasks=()):
    Tp, D = h0.shape
    tm = _tile(Tp, TILE["wout"])

    def body(m_ref, w_ref, h0_ref, g_ref, h1_ref, u2_ref, r2_ref):
        h1 = h0_ref[...] + jnp.dot(m_ref[...], w_ref[...], preferred_element_type=F32)
        r = lax.rsqrt(jnp.mean(h1 * h1, axis=-1, keepdims=True) + NORM_EPS)
        h1_ref[...] = h1
        u2_ref[...] = (h1 * r * g_ref[...]).astype(BF16)
        r2_ref[...] = r

    row = pl.BlockSpec((tm, D), lambda i: (i, 0))
    return _hosted(
        body, tasks, name=name, grid=(Tp // tm,),
        in_specs=[row, pl.BlockSpec((D, D), lambda i: (0, 0)), row, pl.BlockSpec((1, D), lambda i: (0, 0))],
        out_specs=[row, row, pl.BlockSpec((tm, 1), lambda i: (i, 0))],
        out_shape=[_sds((Tp, D), F32), _sds((Tp, D), BF16), _sds((Tp, 1), F32)],
        semantics=("parallel",), operands=(merged, w_out, h0, g2))


def _mlp2_loss(act, w2, h1, target, gf, n_meta, seq, name):
    Tp, D = h1.shape
    K = act.shape[1]
    tm = _tile(Tp, TILE["mlp2"])
    tk = min(K, MLP2_K)
    nk = K // tk

    rc = _tile(tm, EPILOGUE_ROWS)
    nt = Tp // tm
    last_rows = n_meta + seq - (nt - 1) * tm
    assert nt >= 2 and n_meta % 8 == 0 and 0 < last_rows <= tm and last_rows % 8 == 0

    def body(a_ref, w_ref, h1_hbm, t_hbm, g_ref, dh_ref, dhb_ref, loss_ref, dg_ref, h1_buf, t_buf, sems):
        i, k = pl.program_id(0), pl.program_id(1)
        tile_rows = pl.ds(pl.multiple_of(i * tm, tm), tm)
        fetch_h1 = pltpu.make_async_copy(h1_hbm.at[tile_rows, :], h1_buf, sems.at[0])
        fetch_t = [
            (i == 0, pltpu.make_async_copy(t_hbm.at[pl.ds(0, tm - n_meta), :], t_buf.at[pl.ds(n_meta, tm - n_meta), :],
                                           sems.at[1])),
            ((i > 0) & (i < nt - 1),
             pltpu.make_async_copy(t_hbm.at[pl.ds(pl.multiple_of(i * tm - n_meta, 8), tm), :], t_buf, sems.at[1])),
            (i == nt - 1, pltpu.make_async_copy(t_hbm.at[pl.ds((nt - 1) * tm - n_meta, last_rows), :],
                                                t_buf.at[pl.ds(0, last_rows), :], sems.at[1])),
        ]

        @pl.when(k == 0)
        def _():
            fetch_h1.start()
            dh_ref[...] = jnp.zeros_like(dh_ref)

        for cond, f in fetch_t:
            @pl.when(cond & (k == 0))
            def _(f=f):
                f.start()

        @pl.when((i == 0) & (k == 0))
        def _():
            loss_ref[...] = jnp.zeros_like(loss_ref)
            dg_ref[...] = jnp.zeros_like(dg_ref)
            t_buf[0:n_meta, :] = jnp.zeros((n_meta, D), F32)

        dh_ref[...] += jnp.dot(a_ref[...], w_ref[...], preferred_element_type=F32)

        for cond, f in fetch_t:
            @pl.when(cond & (k == nk - 1))
            def _(f=f):
                f.wait()

        @pl.when(k == nk - 1)
        def _():
            fetch_h1.wait()
            g = g_ref[...]

            def chunk(c, carry):
                loss_acc, dg_acc = carry
                rows = pl.ds(pl.multiple_of(c * rc, rc), rc)
                h2 = h1_buf[rows, :] + dh_ref[rows, :]
                r = lax.rsqrt(jnp.mean(h2 * h2, axis=-1, keepdims=True) + NORM_EPS)
                out = h2 * r * g
                row_id = i * tm + c * rc + lax.broadcasted_iota(jnp.int32, (rc, 1), 0)
                valid = (row_id >= n_meta) & (row_id < n_meta + seq)
                diff = jnp.where(valid, out - t_buf[rows, :], 0.0)
                dout = diff / D
                dog = dout * g
                dh = r * dog - h2 * (r * r * r * jnp.mean(dog * h2, axis=-1, keepdims=True))
                dh_ref[rows, :] = dh
                dhb_ref[rows, :] = dh.astype(BF16)
                loss_acc = loss_acc + 0.5 * jnp.sum(jnp.mean(diff * diff, axis=-1, keepdims=True), axis=0, keepdims=True)
                return loss_acc, dg_acc + jnp.sum(dout * (h2 * r), axis=0, keepdims=True)

            loss_sum, dg_sum = lax.fori_loop(0, tm // rc, chunk, (jnp.zeros((1, 1), F32), jnp.zeros((1, D), F32)))
            loss_ref[...] += loss_sum
            dg_ref[...] += dg_sum

    row = pl.BlockSpec((tm, D), lambda i, k: (i, 0))
    hbm = pl.BlockSpec(memory_space=pl.ANY)
    return _call(
        body, name=name, grid=(Tp // tm, nk),
        in_specs=[pl.BlockSpec((tm, tk), lambda i, k: (i, k)), pl.BlockSpec((tk, D), lambda i, k: (k, 0)),
                  hbm, hbm, pl.BlockSpec((1, D), lambda i, k: (0, 0))],
        out_specs=[row, row, pl.BlockSpec((8, 128), lambda i, k: (0, 0)), pl.BlockSpec((1, D), lambda i, k: (0, 0))],
        out_shape=[_sds((Tp, D), F32), _sds((Tp, D), BF16), _sds((8, 128), F32), _sds((1, D), F32)],
        scratch_shapes=[pltpu.VMEM((tm, D), F32), pltpu.VMEM((tm, D), F32), pltpu.SemaphoreType.DMA((2,))],
        compiler_params=_cp("arbitrary", "arbitrary"),
    )(act, w2, h1, target, gf)


def _dact_bwd(dh2b, w2_slots, a1, name):
    Tp, D = dh2b.shape
    S, n, _ = w2_slots.shape
    tm = _tile(Tp, TILE["dact"])

    def body(g_ref, w_ref, a1_ref, o_ref):
        dact = lax.dot_general(g_ref[...], w_ref[...], _NT, preferred_element_type=F32)
        o_ref[...] = (dact * (2.0 * jnp.maximum(a1_ref[...].astype(F32), 0.0))).astype(BF16)

    return _call(
        body, name=name, grid=(S, Tp // tm),
        in_specs=[pl.BlockSpec((tm, D), lambda j, i: (i, 0)), pl.BlockSpec((None, n, D), lambda j, i: (j, 0, 0)),
                  pl.BlockSpec((tm, n), lambda j, i: (i, j))],
        out_specs=pl.BlockSpec((tm, n), lambda j, i: (i, j)),
        out_shape=_sds((Tp, S * n), BF16),
        compiler_params=_cp("parallel", "parallel"),
    )(dh2b, w2_slots, a1)


def _weight_grad(a, g, blocks, block_a, name, tasks=(), part=(0, 1)):
    Tp, Ka = a.shape
    Ng = g.shape[1]
    p, parts = part
    assert parts == 1 or not block_a
    ka = Ka // blocks if block_a else Ka // parts
    ng = Ng if block_a else Ng // blocks
    tt = _tile(Tp, TILE["tn"])
    nt = Tp // tt

    def body(a_ref, g_ref, o_ref, acc_ref):
        t = pl.program_id(1)

        @pl.when(t == 0)
        def _():
            acc_ref[...] = jnp.zeros_like(acc_ref)

        acc_ref[...] += lax.dot_general(a_ref[...], g_ref[...], _TN, preferred_element_type=F32)

        @pl.when(t == nt - 1)
        def _():
            o_ref[...] = acc_ref[...].astype(o_ref.dtype)

    if block_a:
        a_spec = pl.BlockSpec((tt, ka), lambda j, t: (t, j))
        g_spec = pl.BlockSpec((tt, ng), lambda j, t: (t, 0))
    else:
        a_spec = pl.BlockSpec((tt, ka), lambda j, t: (t, p))
        g_spec = pl.BlockSpec((tt, ng), lambda j, t: (t, j))
    (dw,), extra = _hosted(
        body, tasks, name=name, grid=(blocks, nt),
        in_specs=[a_spec, g_spec],
        out_specs=[pl.BlockSpec((None, ka, ng), lambda j, t: (j, 0, 0))],
        out_shape=[_sds((blocks, ka, ng), BF16)],
        scratch_shapes=[pltpu.VMEM((ka, ng), F32)],
        semantics=("parallel", "arbitrary"), operands=(a, g))
    return dw, extra


def _nt_norm_bwd(dz, w_parts, dres, hin, rin, g, want_bf16, name, tasks=(), tiles=None, earlier=None):
    Tp, D = hin.shape
    P = len(w_parts)
    S, _, n = w_parts[0].shape
    K = S * P
    tm = _tile(Tp, TILE["nt"])
    t0, nt = tiles if tiles is not None else (0, Tp // tm)
    assert not (want_bf16 and earlier is not None)

    rc = _tile(tm, EPILOGUE_ROWS)

    def body(dz_ref, *rest):
        w_refs, (dres_hbm, h_hbm, r_ref, g_ref), rest = rest[:P], rest[P:P + 4], rest[P + 4:]
        if earlier is not None:
            _, dg0_ref, dh_ref, dg_ref, dres_buf, h_buf, sems = rest
        elif want_bf16:
            dh_ref, dhb_ref, dg_ref, dres_buf, h_buf, sems = rest
        else:
            dh_ref, dg_ref, dres_buf, h_buf, sems = rest
        i, k = pl.program_id(0), pl.program_id(1)
        tile_rows = pl.ds(pl.multiple_of((t0 + i) * tm, tm), tm)
        fetch = [pltpu.make_async_copy(dres_hbm.at[tile_rows, :], dres_buf, sems.at[0]),
                 pltpu.make_async_copy(h_hbm.at[tile_rows, :], h_buf, sems.at[1])]

        @pl.when(k == 0)
        def _():
            for f in fetch:
                f.start()
            dh_ref[...] = jnp.zeros_like(dh_ref)

        @pl.when((i == 0) & (k == 0))
        def _():
            dg_ref[...] = jnp.zeros_like(dg_ref) if earlier is None else dg0_ref[...]

        for q in range(P):
            @pl.when(k % P == q)
            def _(q=q):
                dh_ref[...] += lax.dot_general(dz_ref[...], w_refs[q][...], _NT, preferred_element_type=F32)

        @pl.when(k == K - 1)
        def _():
            for f in fetch:
                f.wait()
            g = g_ref[...]

            def chunk(c, dg_acc):
                rows = pl.ds(pl.multiple_of(c * rc, rc), rc)
                du = dh_ref[rows, :]
                h = h_buf[rows, :]
                r = r_ref[rows, :]
                dug = du * g
                dh = dres_buf[rows, :] + r * dug - h * (r * r * r * jnp.mean(dug * h, axis=-1, keepdims=True))
                dh_ref[rows, :] = dh
                if want_bf16:
                    dhb_ref[rows, :] = dh.astype(BF16)
                return dg_acc + jnp.sum(du * (h * r), axis=0, keepdims=True)

            dg_ref[...] += lax.fori_loop(0, tm // rc, chunk, jnp.zeros((1, D), F32))

    row = pl.BlockSpec((tm, D), lambda i, k: (t0 + i, 0))
    vec = pl.BlockSpec((1, D), lambda i, k: (0, 0))
    hbm = pl.BlockSpec(memory_space=pl.ANY)
    out_specs = [row] + ([row] if want_bf16 else []) + [vec]
    out_shape = [_sds((Tp, D), F32)] + ([_sds((Tp, D), BF16)] if want_bf16 else []) + [_sds((1, D), F32)]
    in_specs = ([pl.BlockSpec((tm, n), lambda i, k: (t0 + i, k))]
                + [pl.BlockSpec((None, D, n), lambda i, k: (k // P, 0, 0))] * P
                + [hbm, hbm, pl.BlockSpec((tm, 1), lambda i, k: (t0 + i, 0)), vec])
    operands = (dz, *w_parts, dres, hin, rin, g)
    aliases = {}
    if earlier is not None:
        in_specs += [hbm, vec]
        operands += tuple(earlier)
        aliases = {P + 5: 0}
    return _hosted(
        body, tasks, name=name, grid=(nt, K), in_specs=in_specs, out_specs=out_specs, out_shape=out_shape,
        scratch_shapes=[pltpu.VMEM((tm, D), F32), pltpu.VMEM((tm, D), F32), pltpu.SemaphoreType.DMA((2,))],
        semantics=("arbitrary", "arbitrary"), operands=operands, aliases=aliases)


def _dmerged_bwd(dh1b, w_out, name):
    Tp, D = dh1b.shape
    tm = _tile(Tp, TILE["dmerged"])

    def body(g_ref, w_ref, o_ref):
        o_ref[...] = lax.dot_general(g_ref[...], w_ref[...], _NT, preferred_element_type=F32)

    row = pl.BlockSpec((tm, D), lambda i: (i, 0))
    return _call(
        body, name=name, grid=(Tp // tm,),
        in_specs=[row, pl.BlockSpec((D, D), lambda i: (0, 0))],
        out_specs=row, out_shape=_sds((Tp, D), F32),
        compiler_params=_cp("parallel"),
    )(dh1b, w_out)


def _pool_bwd(dmerged, proj, y_pool, d_pool, scale, pool_w, name, tasks=()):
    Tp, D = dmerged.shape
    G, Cg, _ = pool_w.shape
    tm = _tile(Tp, TILE["pool"])
    nt = Tp // tm

    def body(dm_ref, gp_ref, y_ref, d_ref, sc_ref, w_ref, dproj_hbm, dw_ref, dsc_ref, next_ref, out_buf, out_sems):
        t = pl.program_id(0)
        tile = nt - 1 - t
        slot = t % 2
        dv_ref, dgp_ref = out_buf.at[slot, 0], out_buf.at[slot, 1]
        tile_rows = pl.ds(pl.multiple_of(tile * tm, tm), tm)

        def out_copies(s):
            return [pltpu.make_async_copy(out_buf.at[s, k], dproj_hbm.at[tile_rows, pl.ds(piece * D, D)],
                                          out_sems.at[2 * s + k]) for k, piece in enumerate((0, 3))]

        @pl.when(t >= 2)
        def _():
            for cp in out_copies(slot):
                cp.wait()

        @pl.when(t == 0)
        def _():
            next_ref[...] = jnp.zeros_like(next_ref)
            dw_ref[...] = jnp.zeros_like(dw_ref)
            dsc_ref[...] = jnp.zeros_like(dsc_ref)

        rows = _row_ids(tile, tm)
        dm = dm_ref[...]
        y = y_ref[...]
        sc = sc_ref[...]
        sg = _sigmoid(gp_ref[...])
        dpo = dm * sg
        dgp_ref[...] = (dm * (y * sc) * sg * (1.0 - sg)).astype(BF16)
        dsc_ref[...] += jnp.sum(dpo * y, axis=0, keepdims=True)
        dyb = (dpo * sc).astype(BF16)
        for g, win in enumerate(POOL_WINDOWS):
            cols = slice(g * Cg, (g + 1) * Cg)
            dy = dyb[:, cols]
            dd = lax.dot_general(dy, w_ref[g], _NT, preferred_element_type=F32)
            dw_ref[g] += lax.dot_general(d_ref[:, cols], dy, _TN, preferred_element_type=F32)
            q = dd / jnp.minimum(rows + 1, win).astype(F32)
            s = jnp.concatenate([q, next_ref[:, cols]], axis=0)
            k = 1
            while k < win:
                s = s + pltpu.roll(s, s.shape[0] - k, axis=0)
                k *= 2
            dv_ref[:, cols] = (s[:tm] - dd).astype(BF16)
            next_ref[:, cols] = q[:MAX_WINDOW]
        for cp in out_copies(slot):
            cp.start()

        @pl.when(t == nt - 1)
        def _():
            for cp in out_copies(slot) + (out_copies(1 - slot) if nt > 1 else []):
                cp.wait()

    row = pl.BlockSpec((tm, D), lambda t: (nt - 1 - t, 0))
    return _hosted(
        body, tasks, name=name, grid=(nt,),
        in_specs=[row, pl.BlockSpec((tm, D), lambda t: (nt - 1 - t, 3)), row, row,
                  pl.BlockSpec((1, D), lambda t: (0, 0)), pl.BlockSpec((G, Cg, Cg), lambda t: (0, 0, 0))],
        out_specs=[pl.BlockSpec(memory_space=pl.ANY), pl.BlockSpec((G, Cg, Cg), lambda t: (0, 0, 0)),
                   pl.BlockSpec((1, D), lambda t: (0, 0))],
        out_shape=[_sds((Tp, proj.shape[1]), BF16), _sds((G, Cg, Cg), F32), _sds((1, D), F32)],
        scratch_shapes=[pltpu.VMEM((MAX_WINDOW, D), F32), pltpu.VMEM((2, 2, tm, D), BF16),
                        pltpu.SemaphoreType.DMA((4,))],
        semantics=("arbitrary",), operands=(dmerged, proj, y_pool, d_pool, scale, pool_w))


LRU_SMALL_ROWS = 8


def _lru_bwd(dmerged, proj, xc, r_gate, i_gate, a_gate, mult_gate, hs, lam, conv_w, wa, wx, dproj, name, tasks=()):
    Tp, D = dmerged.shape
    H, hd, _ = wa.shape
    tm = _tile(Tp, TILE["lru"])
    nt = Tp // tm
    nb = D // hd
    halo_blocks = tm // HALO

    def body(dm_ref, vl_ref, vg_ref, gl_ref, xc_ref, r_ref, i_ref, a_ref, mult_ref, hs_ref, hsp_ref, lam_ref, cw_ref,
             wa_ref, wx_ref, _, dproj_hbm, dwa_ref, dwx_ref, small_ref,
             mu_next_ref, dxc_next_ref, q_s, mu_s, out_buf, out_sems):
        h_id, t = pl.program_id(0), pl.program_id(1)
        tile = nt - 1 - t
        step = h_id * nt + t
        slot = step % 2
        dvl_ref, dvg_ref, dgl_ref = out_buf.at[slot, 0], out_buf.at[slot, 1], out_buf.at[slot, 2]
        tile_rows = pl.ds(pl.multiple_of(tile * tm, tm), tm)

        def out_copies(s):
            return [pltpu.make_async_copy(
                out_buf.at[s, k], dproj_hbm.at[tile_rows, pl.ds(pl.multiple_of((piece * nb + h_id) * hd, hd), hd)],
                out_sems.at[3 * s + k]) for k, piece in enumerate((1, 2, 4))]

        @pl.when(step >= 2)
        def _():
            for cp in out_copies(slot):
                cp.wait()

        @pl.when(t == 0)
        def _():
            mu_next_ref[...] = jnp.zeros_like(mu_next_ref)
            dxc_next_ref[...] = jnp.zeros_like(dxc_next_ref)
            dwa_ref[...] = jnp.zeros_like(dwa_ref)
            dwx_ref[...] = jnp.zeros_like(dwx_ref)
            small_ref[...] = jnp.zeros_like(small_ref)

        first = tile == 0
        dm = dm_ref[...]
        hs_t = hs_ref[...]
        xc_t = xc_ref[...]
        r = r_ref[...]
        i = i_ref[...]
        lam_v = lam_ref[...]
        sp = _softplus(-lam_v)
        a = a_ref[...]
        mult = mult_ref[...]

        sg = _sigmoid(gl_ref[...])
        ge, dge = _gelu_and_grad(vg_ref[...])
        dlo = dm * sg
        dgl_ref[...] = (dm * (hs_t * ge) * sg * (1.0 - sg)).astype(BF16)
        dvg_ref[...] = (dlo * hs_t * dge).astype(BF16)
        dhs = dlo * ge

        q_s[...] = a * dhs
        mu_first = _scan_chunks(a_ref, q_s, mu_s, mu_next_ref[0:1, :], tm, reverse=True)
        lam_t = dhs + _shift_up(mu_s[...], mu_next_ref[...], 1)
        mu_next_ref[...] = jnp.broadcast_to(mu_first, mu_next_ref.shape)

        h_prev = _shift_down(jnp.where(first, 0.0, hsp_ref[...]), hs_t, 1)
        da = lam_t * h_prev
        dmult = lam_t * (i * xc_t)
        di = lam_t * mult * xc_t
        dxc = lam_t * mult * i
        dlog_a = da * a - dmult * (a * a) / mult
        dr = dlog_a * (-LRU_C * sp)
        dlam_rows = dlog_a * (-LRU_C * r)
        dza = dr * r * (1.0 - r)
        dzx = di * i * (1.0 - i)
        dzab, dzxb = dza.astype(BF16), dzx.astype(BF16)
        xcb = xc_t.astype(BF16)
        dxc = dxc + lax.dot_general(dzab, wa_ref[...], _NT, preferred_element_type=F32)
        dxc = dxc + lax.dot_general(dzxb, wx_ref[...], _NT, preferred_element_type=F32)
        dwa_ref[...] += lax.dot_general(xcb, dzab, _TN, preferred_element_type=F32)
        dwx_ref[...] += lax.dot_general(xcb, dzxb, _TN, preferred_element_type=F32)

        dxc_next = dxc_next_ref[...]
        taps = [_shift_up(dxc, dxc_next, CONV_WIDTH - 1 - k) for k in range(CONV_WIDTH)]
        dv = jnp.zeros_like(dxc)
        for k in range(CONV_WIDTH):
            dv = dv + cw_ref[k:k + 1, :] * taps[k]
        dvl_ref[...] = dv.astype(BF16)
        dxc_next_ref[...] = dxc[:HALO, :]

        v_t = vl_ref[...]
        small = [jnp.sum(dza, axis=0, keepdims=True), jnp.sum(dzx, axis=0, keepdims=True),
                 jnp.sum(dlam_rows, axis=0, keepdims=True) * (-_sigmoid(-lam_v)),
                 jnp.sum(dxc, axis=0, keepdims=True)]
        for k in range(CONV_WIDTH):
            small.append(jnp.sum(taps[k] * v_t, axis=0, keepdims=True))
        for k, row in enumerate(small):
            small_ref[k:k + 1, :] += row
        for cp in out_copies(slot):
            cp.start()

        @pl.when(step == H * nt - 1)
        def _():
            for cp in out_copies(slot) + (out_copies(1 - slot) if H * nt > 1 else []):
                cp.wait()

    def piece(p):
        return pl.BlockSpec((tm, hd), lambda h, t: (nt - 1 - t, p * nb + h))

    def halo(p):
        return pl.BlockSpec((HALO, hd), lambda h, t: (jnp.maximum((nt - 1 - t) * halo_blocks - 1, 0), p * nb + h))

    blk = pl.BlockSpec((tm, hd), lambda h, t: (nt - 1 - t, h))
    vec = pl.BlockSpec((1, hd), lambda h, t: (0, h))
    mat = pl.BlockSpec((None, hd, hd), lambda h, t: (h, 0, 0))
    return _hosted(
        body, tasks, name=name, grid=(H, nt),
        in_specs=[blk, piece(1), piece(2), piece(4), blk, blk, blk, blk, blk, blk, halo(0), vec,
                  pl.BlockSpec((CONV_WIDTH, hd), lambda h, t: (0, h)), mat, mat, pl.BlockSpec(memory_space=pl.ANY)],
        out_specs=[pl.BlockSpec(memory_space=pl.ANY), mat, mat,
                   pl.BlockSpec((None, LRU_SMALL_ROWS, hd), lambda h, t: (h, 0, 0))],
        out_shape=[_sds(dproj.shape, BF16)] + [_sds((H, hd, hd), F32)] * 2 + [_sds((H, LRU_SMALL_ROWS, hd), F32)],
        scratch_shapes=[pltpu.VMEM((HALO, hd), F32), pltpu.VMEM((HALO, hd), F32),
                        pltpu.VMEM((tm, hd), F32), pltpu.VMEM((tm, hd), F32),
                        pltpu.VMEM((2, 3, tm, hd), BF16), pltpu.SemaphoreType.DMA((6,))],
        semantics=("arbitrary", "arbitrary"), aliases={15: 0},
        operands=(dmerged, proj, proj, proj, xc, r_gate, i_gate, a_gate, mult_gate, hs, hs, lam, conv_w, wa, wx,
                  dproj))


def _adamw(w, g, m, v):
    m = ADAM_B1 * m + (1.0 - ADAM_B1) * g
    v = ADAM_B2 * v + (1.0 - ADAM_B2) * (g * g)
    m_hat = m / (1.0 - ADAM_B1 ** ADAM_STEP)
    v_hat = v / (1.0 - ADAM_B2 ** ADAM_STEP)
    delta = -ADAM_LR * (m_hat / (jnp.sqrt(v_hat) + ADAM_EPS) + ADAM_WD * w)
    return delta, m, v


def _reduce_update(pair_sums, chip_sums, w, m, v, chip_slot, name, part=(0, 1), earlier=None):
    R, C = pair_sums.shape[1:]
    p, parts = part
    tr = _tile(R, TILE["update"])
    nblk = R // tr

    def body(slot_ref, own_ref, got_ref, w_ref, m_ref, v_ref, *rest):
        g_out, d_out, m_out, v_out = rest[-4:]
        g = own_ref[...].astype(F32)
        for k in range(3):
            g = g + got_ref[k].astype(F32)
        d, m_new, v_new = _adamw(w_ref[...], g, m_ref[...], v_ref[...])
        g_out[...] = g
        d_out[...] = d
        m_out[...] = m_new
        v_out[...] = v_new

    blk = pl.BlockSpec((tr, C), lambda i, s: (p * nblk + i, 0))
    in_specs = [pl.BlockSpec((None, tr, C), lambda i, s: (s[0], i, 0)),
                pl.BlockSpec((3, tr, C), lambda i, s: (0, i, 0)), blk, blk, blk]
    operands = (chip_slot, pair_sums, chip_sums, w, m, v)
    aliases = {}
    if earlier is not None:
        in_specs += [pl.BlockSpec(memory_space=pl.ANY)] * 4
        operands += tuple(earlier)
        aliases = {6 + k: k for k in range(4)}
    return _call(
        body, name=name,
        grid_spec=pltpu.PrefetchScalarGridSpec(
            num_scalar_prefetch=1, grid=(R // tr,), in_specs=in_specs, out_specs=[blk] * 4),
        out_shape=[_sds((parts * R, C), F32)] * 4,
        input_output_aliases=aliases,
        compiler_params=_cp("parallel"),
    )(*operands)


def _small_updates(small_sum, bias_grads, col0, specs, name):
    n = len(specs)

    def body(col_ref, ss_ref, bg_ref, *refs):
        ins, outs = refs[:3 * n], refs[3 * n:]
        for k, (w, _, _, kind, r0) in enumerate(specs):
            nr, nc = w.shape
            if kind == "rows":
                g = ss_ref[r0:r0 + nr, :]
            elif kind == "cols":
                g = ss_ref[r0:r0 + nr, pl.ds(pl.multiple_of(col_ref[0], nc), nc)]
            else:
                g = bg_ref[r0:r0 + nr, :]
            d, m_new, v_new = _adamw(ins[3 * k][...], g, ins[3 * k + 1][...], ins[3 * k + 2][...])
            for o, val in zip(outs[4 * k:4 * k + 4], (g, d, m_new, v_new)):
                o[...] = val

    def whole(a):
        return pl.BlockSpec(a.shape, lambda i, c: (0,) * a.ndim)

    arrays = [a for (w, m, v, _, _) in specs for a in (w, m, v)]
    res = _call(
        body, name=name,
        grid_spec=pltpu.PrefetchScalarGridSpec(
            num_scalar_prefetch=1, grid=(1,),
            in_specs=[whole(small_sum), whole(bias_grads)] + [whole(a) for a in arrays],
            out_specs=[whole(w) for (w, _, _, _, _) in specs for _ in range(4)]),
        out_shape=[_sds(w.shape, F32) for (w, _, _, _, _) in specs for _ in range(4)],
        compiler_params=_cp("arbitrary"),
    )(col0, small_sum, bias_grads, *arrays)
    return [tuple(res[4 * k:4 * k + 4]) for k in range(n)]


def _slots_from_rows(full, lead):
    L, R, C = full.shape
    r = R // N_DEV
    return full.reshape(L, N_DEV, r, C).transpose(1, 0, 2, 3).reshape(N_DEV, L * r, C)


def _rows_from_slots(slots, lead):
    _, LR, C = slots.shape
    r = LR // lead
    return slots.reshape(N_DEV, lead, r, C).transpose(1, 0, 2, 3).reshape(lead, N_DEV * r, C)


def kernel(x, meta_tokens, norm1_g, w_in, pool_w, pool_scale, conv_w, conv_b, gate_a_w, gate_a_b, gate_x_w, gate_x_b, lru_lambda, w_out, norm2_g, mlp_w1, mlp_w2, final_g, loss_target, m_meta_tokens, m_norm1_g, m_w_in, m_pool_w, m_pool_scale, m_conv_w, m_conv_b, m_gate_a_w, m_gate_a_b, m_gate_x_w, m_gate_x_b, m_lru_lambda, m_w_out, m_norm2_g, m_mlp_w1, m_mlp_w2, m_final_g, v_meta_tokens, v_norm1_g, v_w_in, v_pool_w, v_pool_scale, v_conv_w, v_conv_b, v_gate_a_w, v_gate_a_b, v_gate_x_w, v_gate_x_b, v_lru_lambda, v_w_out, v_norm2_g, v_mlp_w1, v_mlp_w2, v_final_g):
    seq, D = x.shape[1], x.shape[2]
    n_meta = meta_tokens.shape[0]
    G, Cg = pool_w.shape[1], pool_w.shape[3]
    H, hd = gate_a_w.shape[1], gate_a_w.shape[3]
    T = n_meta + seq
    Tp = -(-T // ROW_ALIGN) * ROW_ALIGN
    ix, iy, ic = _pos()
    me = 4 * ix + 2 * iy + ic
    core = jnp.reshape(ic, (1,)).astype(jnp.int32)
    chip_slot = jnp.reshape(2 * ix + iy, (1,)).astype(jnp.int32)

    w_in_l, w1_l, w2_l, w_out_l = w_in[0], mlp_w1[0], mlp_w2[0], w_out[0]
    pool_l = pool_w[0].reshape(G * (Cg // N_DEV), Cg)
    wa_l = gate_a_w[0].reshape(H * (hd // N_DEV), hd)
    wx_l = gate_x_w[0].reshape(H * (hd // N_DEV), hd)
    small_params = jnp.concatenate(
        [meta_tokens, conv_w[0], jnp.zeros((4, D // N_DEV), F32)], axis=0)
    biases = jnp.concatenate([gate_a_b[0], gate_x_b[0]], axis=0)
    (w_in_g, small_g, bias_g) = _all_gather_relay(
        [w_in_l.astype(BF16), small_params, biases], [W_IN_CHUNKS, 1, 1], "gather_first")
    small_full = small_g.transpose(1, 0, 2).reshape(n_meta + 8, D)
    meta_full = small_full[:n_meta]
    conv_full = small_full[n_meta:n_meta + CONV_WIDTH]
    bias_full = bias_g.transpose(1, 0, 2).reshape(2 * H, hd)
    ba_full, bx_full = bias_full[:H], bias_full[H:]

    h0 = jnp.concatenate([meta_full, x[0], jnp.zeros((Tp - T, D), F32)], axis=0)
    u, r1 = _norm_fwd(h0, norm1_g, "norm1")
    proj, ((pool_g,), (wa_g,), (wx_g,), (w_out_g,), (w1_g,)) = _proj_fwd(
        u, w_in_g, "proj",
        tasks=[_AgFull(pool_l.astype(BF16)), _AgFull(wa_l.astype(BF16)), _AgFull(wx_l.astype(BF16)),
               _AgFull(w_out_l.astype(BF16)), _AgFull(w1_l.astype(BF16))])
    pool_full = _rows_from_slots(pool_g, G)
    wa_full = _rows_from_slots(wa_g, H)
    wx_full = _rows_from_slots(wx_g, H)
    w_in_parts = [w_in_g]
    d_pool, y_pool = _pool_fwd(proj, pool_full, "pool_fwd")
    (xc, r_gate, i_gate, a_gate, mult_gate, hs, merged), ((w2_g,),) = _lru_fwd(
        proj, y_pool, pool_scale, conv_full, conv_b, wa_full, ba_full, wx_full, bx_full, lru_lambda, "lru_fwd",
        tasks=[_AgFull(w2_l.astype(BF16))])
    w_out_full = w_out_g.reshape(D, D)
    (h1, u2, r2), _ = _wout_norm_fwd(merged, w_out_full, h0, norm2_g, "wout_norm2")
    (act, a1), _ = _mlp1_fwd(u2, w1_g, "mlp1")
    dh2, dh2b, loss_tile, d_final_g = _mlp2_loss(
        act, w2_g.reshape(-1, D), h1, loss_target[0], final_g.reshape(1, D), n_meta, seq, "mlp2_loss")

    def pair(part, got, tag):
        return _pair_sum(part, got, core, "pair_sum_" + tag)

    d_a1 = _dact_bwd(dh2b, w2_g, a1, "dact")
    dw2_p, _ = _weight_grad(act, dh2b, N_DEV, True, "dw2")
    dw1_p, ((dw2_got,),) = _weight_grad(u2, d_a1, N_DEV, False, "dw1", tasks=[_RsSibling(dw2_p)])
    dw2_pair = pair(dw2_p, dw2_got, "w2")
    (dh1, dh1b, d_norm2_g), ((dw2_chips,), (dw1_got,)) = _nt_norm_bwd(
        d_a1, [w1_g], dh2, h1, r2, norm2_g, True, "du2_norm2", tasks=[_RsChips(dw2_pair), _RsSibling(dw1_p)])
    dw1_pair = pair(dw1_p, dw1_got, "w1")
    dmerged = _dmerged_bwd(dh1b, w_out_full, "dmerged")
    dwout_p, _ = _weight_grad(merged, dh1b, 2, True, "dwout")
    dwout_p = dwout_p.reshape(N_DEV, D // N_DEV, D)
    (dproj_pool, dpool_full, d_scale), ((dwout_got,),) = _pool_bwd(
        dmerged, proj, y_pool, d_pool, pool_scale, pool_full, "pool_bwd", tasks=[_RsSibling(dwout_p)])
    dwout_pair = pair(dwout_p, dwout_got, "wout")
    (dproj, dwa_full, dwx_full, lru_small), ((dw1_chips,), (dwout_chips,)) = _lru_bwd(
        dmerged, proj, xc, r_gate, i_gate, a_gate, mult_gate, hs, lru_lambda, conv_full, wa_full, wx_full, dproj_pool,
        "lru_bwd", tasks=[_RsChips(dw1_pair), _RsChips(dwout_pair)])
    dwin_a, _ = _weight_grad(u, dproj, N_DEV, False, "dwin_a", part=(0, 2))
    dwin_b, ((dwin_a_got,),) = _weight_grad(u, dproj, N_DEV, False, "dwin_b", part=(1, 2), tasks=[_RsSibling(dwin_a)])
    dwin_a_pair = pair(dwin_a, dwin_a_got, "win_a")
    dpool_p = _slots_from_rows(dpool_full, G).astype(BF16)
    dwa_p = _slots_from_rows(dwa_full, H).astype(BF16)
    dwx_p = _slots_from_rows(dwx_full, H).astype(BF16)
    late = [dwin_b, dpool_p, dwa_p, dwx_p]
    n_tiles = Tp // _tile(Tp, TILE["nt"])
    n_first = max(1, n_tiles // 2)
    (dh0_a, d_norm1_a), ((dwin_a_chips,), *late_got) = _nt_norm_bwd(
        dproj, w_in_parts, dh1, h0, r1, norm1_g, False, "du_norm1_a",
        tasks=[_RsChips(dwin_a_pair)] + [_RsSibling(p) for p in late], tiles=(0, n_first))
    late_pair = [pair(p, g[0], "late%d" % k) for k, (p, g) in enumerate(zip(late, late_got))]
    (dh0, d_norm1_g), late_chips = _nt_norm_bwd(
        dproj, w_in_parts, dh1, h0, r1, norm1_g, False, "du_norm1_b", tasks=[_RsChips(p) for p in late_pair],
        tiles=(n_first, n_tiles - n_first), earlier=(dh0_a, d_norm1_a))
    grad_x = dh0[n_meta:T][None]

    pair_sums = [dw2_pair, dw1_pair, dwout_pair] + late_pair[1:]
    chip_sums = [dw2_chips, dw1_chips, dwout_chips] + [c[0] for c in late_chips[1:]]
    big = {}
    names = ["mlp_w2", "mlp_w1", "w_out", "pool_w", "gate_a_w", "gate_x_w"]
    trip = {"mlp_w2": (mlp_w2, m_mlp_w2, v_mlp_w2), "mlp_w1": (mlp_w1, m_mlp_w1, v_mlp_w1),
            "w_out": (w_out, m_w_out, v_w_out),
            "pool_w": (pool_w, m_pool_w, v_pool_w), "gate_a_w": (gate_a_w, m_gate_a_w, v_gate_a_w),
            "gate_x_w": (gate_x_w, m_gate_x_w, v_gate_x_w)}
    for k, nm in enumerate(names):
        w_, m_, v_ = trip[nm]
        shape2 = pair_sums[k].shape[1:]
        outs = _reduce_update(pair_sums[k], chip_sums[k], w_.reshape(shape2), m_.reshape(shape2), v_.reshape(shape2),
                              chip_slot, "update_" + nm)
        big[nm] = [o.reshape(w_.shape) for o in outs]
    win2 = [a_[0] for a_ in (w_in, m_w_in, v_w_in)]
    win_a = _reduce_update(dwin_a_pair, dwin_a_chips, *win2, chip_slot, "update_w_in_a", part=(0, 2))
    win_b = _reduce_update(late_pair[0], late_chips[0][0], *win2, chip_slot, "update_w_in_b", part=(1, 2), earlier=win_a)
    big["w_in"] = [o.reshape(w_in.shape) for o in win_b]
    names = names + ["w_in"]

    lru_rows = lru_small.transpose(1, 0, 2).reshape(LRU_SMALL_ROWS, D)
    small_part = jnp.concatenate(
        [dh0[:n_meta], d_norm1_g, d_scale, d_norm2_g, d_final_g, lru_rows, jnp.zeros((4, D), F32)], axis=0)
    (small_all,) = _all_gather([small_part], "gather_small_grads")
    small_sum = _sum_slots(small_all, "sum_small_grads")
    o = n_meta
    dcol, hcol = D // N_DEV, hd // N_DEV
    bias_grads = jnp.concatenate(
        [lax.dynamic_slice_in_dim(small_sum[o + 4 + k].reshape(H, hd), me * hcol, hcol, axis=1) for k in range(2)], axis=0)
    col0 = jnp.reshape(me * dcol, (1,)).astype(jnp.int32)
    small = {"meta_tokens": (meta_tokens, m_meta_tokens, v_meta_tokens, "cols", 0),
             "norm1_g": (norm1_g, m_norm1_g, v_norm1_g, "rows", o),
             "pool_scale": (pool_scale, m_pool_scale, v_pool_scale, "rows", o + 1),
             "norm2_g": (norm2_g, m_norm2_g, v_norm2_g, "rows", o + 2),
             "final_g": (final_g, m_final_g, v_final_g, "rows", o + 3),
             "gate_a_b": (gate_a_b, m_gate_a_b, v_gate_a_b, "bias", 0),
             "gate_x_b": (gate_x_b, m_gate_x_b, v_gate_x_b, "bias", H),
             "lru_lambda": (lru_lambda, m_lru_lambda, v_lru_lambda, "rows", o + 6),
             "conv_b": (conv_b, m_conv_b, v_conv_b, "rows", o + 7),
             "conv_w": (conv_w, m_conv_w, v_conv_w, "cols", o + 8)}

    def two_d(a):
        return a.reshape(-1, a.shape[-1])

    small_out = _small_updates(
        small_sum, bias_grads, col0,
        [(two_d(w_), two_d(m_), two_d(v_), kind, row) for (w_, m_, v_, kind, row) in small.values()], "update_small")
    small_res = {nm: [r.reshape(spec[0].shape) for r in res] for (nm, spec), res in zip(small.items(), small_out)}

    def leaves(kind):
        out = {nm: res[kind] for nm, res in small_res.items()}
        for nm in names:
            out[nm] = big[nm][kind]
        order = ["meta_tokens", "norm1_g", "w_in", "pool_w", "pool_scale", "conv_w", "conv_b", "gate_a_w", "gate_a_b",
                 "gate_x_w", "gate_x_b", "lru_lambda", "w_out", "norm2_g", "mlp_w1", "mlp_w2", "final_g"]
        return [out[nm] for nm in order]

    loss = lax.psum(loss_tile[0, 0], ("x", "y", "c"))
    return (loss, grad_x, *leaves(0), *leaves(1), *leaves(2), *leaves(3))
```
